```python
import jax, jax.numpy as jnp
from jax import lax
import numpy as np

D_MODEL = 1024
BATCH = 8
SEQ = 4096
DEPTH = 2

GRID_W = 64
CTX_LEN = 256
POOL_WIDTH = 512
POOL_GROUPS = 4
POOL_GROUP_DIM = POOL_WIDTH // POOL_GROUPS
POOL_WINDOWS = (2, 4, 8, 16)
N_HEADS = 8
N_KV_HEADS = 2
HEAD_DIM = 64
Q_GROUP = N_HEADS // N_KV_HEADS
ATTN_WIDTH = N_HEADS * HEAD_DIM
KV_WIDTH = N_KV_HEADS * HEAD_DIM
MIX_WIDTH = POOL_WIDTH + ATTN_WIDTH
PROJ_WIDTH = POOL_WIDTH + ATTN_WIDTH + 2 * KV_WIDTH
WINDOW = 128
BLOCK = 128
ROPE_BASE = 10000.0
ROPE_AXIS_DIM = HEAD_DIM // 2
D_FF = 2816
N_MOD = 9
EPS = 1e-6
NEG_INF = -1e30

kernel_name = "hybrid_pool_swa_macaron_dit_block"


def rmsnorm(x, g):
    xf = x.astype(jnp.float32)
    y = xf * lax.rsqrt(jnp.mean(xf * xf, axis=-1, keepdims=True) + EPS)
    return (y * g.astype(jnp.float32)).astype(x.dtype)


def norm_modulate(x, g, shift, scale):
    return rmsnorm(x, g) * (1 + scale) + shift


def swiglu(n, w_in, w_out):
    a, b = jnp.split(n @ w_in, 2, axis=-1)
    return (jax.nn.silu(a) * b) @ w_out


def axial_rope_tables(T):
    rows = T // GRID_W
    row = jnp.repeat(jnp.arange(rows), GRID_W).astype(jnp.float32)
    col = jnp.tile(jnp.arange(GRID_W), rows).astype(jnp.float32)
    inv = ROPE_BASE ** (-jnp.arange(0, ROPE_AXIS_DIM, 2, dtype=jnp.float32) / ROPE_AXIS_DIM)
    ang = jnp.concatenate([row[:, None] * inv, col[:, None] * inv], axis=-1)
    return jnp.cos(ang), jnp.sin(ang)


def apply_rope(x, cos, sin):
    xf = x.astype(jnp.float32)
    x1, x2 = xf[..., :HEAD_DIM // 2], xf[..., HEAD_DIM // 2:]
    c, s = cos[None, :, None, :], sin[None, :, None, :]
    return jnp.concatenate([x1 * c - x2 * s, x2 * c + x1 * s], axis=-1).astype(x.dtype)


def pool_mixer(u, w_pool, pool_scale):
    B, T, _ = u.shape
    uf = u.astype(jnp.float32)
    cs = jnp.pad(jnp.cumsum(uf, axis=1), ((0, 0), (1, 0), (0, 0)))
    t = jnp.arange(T)
    outs = []
    for g, w in enumerate(POOL_WINDOWS):
        lo = jnp.clip(t - w // 2, 0, T)
        hi = jnp.clip(t + w - w // 2, 0, T)
        csg = cs[..., g * POOL_GROUP_DIM:(g + 1) * POOL_GROUP_DIM]
        outs.append((csg[:, hi] - csg[:, lo]) / (hi - lo).astype(jnp.float32)[None, :, None])
    pooled = (jnp.concatenate(outs, axis=-1) - uf).astype(u.dtype)
    pooled = pooled.reshape(B, T, POOL_GROUPS, POOL_GROUP_DIM)
    mixed = jnp.einsum("btgc,gcd->btgd", pooled, w_pool).reshape(B, T, POOL_WIDTH)
    return mixed * pool_scale


def latent_attention(q, k, v, kc, vc, sink):
    B, T = q.shape[:2]
    nb = T // BLOCK
    scale = HEAD_DIM ** -0.5
    qb = q.reshape(B, nb, BLOCK, N_KV_HEADS, Q_GROUP, HEAD_DIM)
    pad = ((0, 0), (BLOCK, BLOCK), (0, 0), (0, 0))
    kp = jnp.pad(k, pad).reshape(B, nb + 2, BLOCK, N_KV_HEADS, HEAD_DIM)
    vp = jnp.pad(v, pad).reshape(B, nb + 2, BLOCK, N_KV_HEADS, HEAD_DIM)
    kb = jnp.concatenate([kp[:, :-2], kp[:, 1:-1], kp[:, 2:]], axis=2)
    vb = jnp.concatenate([vp[:, :-2], vp[:, 1:-1], vp[:, 2:]], axis=2)
    s_loc = jnp.einsum("bnqhgd,bnkhd->bnhgqk", qb, kb).astype(jnp.float32) * scale
    n_idx = jnp.arange(nb)[:, None, None]
    qpos = n_idx * BLOCK + jnp.arange(BLOCK)[None, :, None]
    kpos = n_idx * BLOCK + jnp.arange(3 * BLOCK)[None, None, :] - BLOCK
    valid = (kpos >= 0) & (kpos < T) & (jnp.abs(kpos - qpos) <= WINDOW)
    s_loc = jnp.where(valid[None, :, None, None], s_loc, NEG_INF)
    s_ctx = jnp.einsum("bnqhgd,bmhd->bnhgqm", qb, kc).astype(jnp.float32) * scale
    s_sink = jnp.broadcast_to(sink.astype(jnp.float32).reshape(1, 1, N_KV_HEADS, Q_GROUP, 1, 1),
                              s_loc.shape[:-1] + (1,))
    p = jax.nn.softmax(jnp.concatenate([s_loc, s_ctx, s_sink], axis=-1), axis=-1)
    L = kc.shape[1]
    p_loc = p[..., :3 * BLOCK].astype(v.dtype)
    p_ctx = p[..., 3 * BLOCK:3 * BLOCK + L].astype(v.dtype)
    o = jnp.einsum("bnhgqk,bnkhd->bnqhgd", p_loc, vb) + jnp.einsum("bnhgqm,bmhd->bnqhgd", p_ctx, vc)
    return o.reshape(B, T, ATTN_WIDTH)


def context_attention(qc, kc, vc, sink):
    B, L = qc.shape[:2]
    qg = qc.reshape(B, L, N_KV_HEADS, Q_GROUP, HEAD_DIM)
    s = jnp.einsum("blhgd,bmhd->bhglm", qg, kc).astype(jnp.float32) * HEAD_DIM ** -0.5
    s_sink = jnp.broadcast_to(sink.astype(jnp.float32).reshape(1, N_KV_HEADS, Q_GROUP, 1, 1),
                              s.shape[:-1] + (1,))
    p = jax.nn.softmax(jnp.concatenate([s, s_sink], axis=-1), axis=-1)[..., :L].astype(vc.dtype)
    return jnp.einsum("bhglm,bmhd->blhgd", p, vc).reshape(B, L, ATTN_WIDTH)


def context_kv(nc, w_in):
    B, L = nc.shape[:2]
    kc, vc = jnp.split(nc @ w_in[:, MIX_WIDTH:], 2, axis=-1)
    return (kc.reshape(B, L, N_KV_HEADS, HEAD_DIM), vc.reshape(B, L, N_KV_HEADS, HEAD_DIM))


def mix_latent(n, kc, vc, w_in, w_pool, pool_scale, sink, w_out, cos, sin):
    B, T = n.shape[:2]
    u, q, k, v = jnp.split(n @ w_in, [POOL_WIDTH, MIX_WIDTH, MIX_WIDTH + KV_WIDTH], axis=-1)
    pool_out = pool_mixer(u, w_pool, pool_scale)
    q = apply_rope(q.reshape(B, T, N_HEADS, HEAD_DIM), cos, sin)
    k = apply_rope(k.reshape(B, T, N_KV_HEADS, HEAD_DIM), cos, sin)
    v = v.reshape(B, T, N_KV_HEADS, HEAD_DIM)
    attn_out = latent_attention(q, k, v, kc, vc, sink)
    return jnp.concatenate([pool_out, attn_out], axis=-1) @ w_out


def mix_context(nc, kc, vc, w_in, w_pool, pool_scale, sink, w_out):
    B, L = nc.shape[:2]
    u, q = jnp.split(nc @ w_in[:, :MIX_WIDTH], [POOL_WIDTH], axis=-1)
    pool_out = pool_mixer(u, w_pool, pool_scale)
    attn_out = context_attention(q.reshape(B, L, N_HEADS, HEAD_DIM), kc, vc, sink)
    return jnp.concatenate([pool_out, attn_out], axis=-1) @ w_out


def _fwd_setup_inputs(seed: int = 0) -> dict:
    key = jax.random.key(seed)
    ks = jax.random.split(key, 24)
    f32 = jnp.float32
    nrm = lambda k, shape, s: jax.random.normal(k, shape, f32) * s
    gain = lambda k, shape: 1.0 + 0.1 * jax.random.normal(k, shape, f32)
    return {
        "x": nrm(ks[0], (BATCH, SEQ, D_MODEL), 1.0),
        "c": nrm(ks[1], (BATCH, D_MODEL), 1.0),
        "ctx": nrm(ks[2], (BATCH, CTX_LEN, D_MODEL), 1.0),
        "c_ctx": nrm(ks[3], (D_MODEL,), 1.0),
        "w_mod": nrm(ks[4], (DEPTH, D_MODEL, N_MOD * D_MODEL), D_MODEL ** -0.5),
        "b_mod": nrm(ks[5], (DEPTH, N_MOD * D_MODEL), 0.02),
        "norm_ffn1": gain(ks[6], (DEPTH, D_MODEL)),
        "w_ffn1_in": nrm(ks[7], (DEPTH, D_MODEL, 2 * D_FF), D_MODEL ** -0.5),
        "w_ffn1_out": nrm(ks[8], (DEPTH, D_FF, D_MODEL), D_FF ** -0.5),
        "norm_mix": gain(ks[9], (DEPTH, D_MODEL)),
        "w_in": nrm(ks[10], (DEPTH, D_MODEL, PROJ_WIDTH), D_MODEL ** -0.5),
        "w_pool": nrm(ks[11], (DEPTH, POOL_GROUPS, POOL_GROUP_DIM, POOL_GROUP_DIM), POOL_GROUP_DIM ** -0.5),
        "pool_scale": gain(ks[12], (DEPTH, POOL_WIDTH)),
        "sink": nrm(ks[13], (DEPTH, N_HEADS), 1.0),
        "w_out": nrm(ks[14], (DEPTH, MIX_WIDTH, D_MODEL), MIX_WIDTH ** -0.5),
        "norm_ffn2": gain(ks[15], (DEPTH, D_MODEL)),
        "w_ffn2_in": nrm(ks[16], (DEPTH, D_MODEL, 2 * D_FF), D_MODEL ** -0.5),
        "w_ffn2_out": nrm(ks[17], (DEPTH, D_FF, D_MODEL), D_FF ** -0.5),
        "norm_final": gain(ks[18], (D_MODEL,)),
    }


def _fwd_reference(x, c, ctx, c_ctx, w_mod, b_mod, norm_ffn1, w_ffn1_in, w_ffn1_out, norm_mix, w_in,
              w_pool, pool_scale, sink, w_out, norm_ffn2, w_ffn2_in, w_ffn2_out, norm_final):
    B = x.shape[0]
    cos, sin = axial_rope_tables(x.shape[1])
    h, hc = x, ctx
    for l in range(DEPTH):
        last = l == DEPTH - 1
        mx = (jax.nn.silu(c) @ w_mod[l] + b_mod[l]).reshape(B, N_MOD, 1, D_MODEL)
        mc = (jax.nn.silu(c_ctx) @ w_mod[l] + b_mod[l]).reshape(N_MOD, D_MODEL)
        h = h + 0.5 * mx[:, 2] * swiglu(norm_modulate(h, norm_ffn1[l], mx[:, 0], mx[:, 1]),
                                         w_ffn1_in[l], w_ffn1_out[l])
        hc = hc + 0.5 * mc[2] * swiglu(norm_modulate(hc, norm_ffn1[l], mc[0], mc[1]),
                                        w_ffn1_in[l], w_ffn1_out[l])
        n = norm_modulate(h, norm_mix[l], mx[:, 3], mx[:, 4])
        nc = norm_modulate(hc, norm_mix[l], mc[3], mc[4])
        kc, vc = context_kv(nc, w_in[l])
        h = h + mx[:, 5] * mix_latent(n, kc, vc, w_in[l], w_pool[l], pool_scale[l], sink[l], w_out[l], cos, sin)
        if not last:
            hc = hc + mc[5] * mix_context(nc, kc, vc, w_in[l], w_pool[l], pool_scale[l], sink[l], w_out[l])
            hc = hc + 0.5 * mc[8] * swiglu(norm_modulate(hc, norm_ffn2[l], mc[6], mc[7]),
                                            w_ffn2_in[l], w_ffn2_out[l])
        h = h + 0.5 * mx[:, 8] * swiglu(norm_modulate(h, norm_ffn2[l], mx[:, 6], mx[:, 7]),
                                         w_ffn2_in[l], w_ffn2_out[l])
    return rmsnorm(h, norm_final)


import jax as _jax
import jax.numpy as _jnp

TWIN_FORMAT = 'train_step'
FWD_PARAMS = ['x', 'c', 'ctx', 'c_ctx', 'w_mod', 'b_mod', 'norm_ffn1', 'w_ffn1_in', 'w_ffn1_out', 'norm_mix', 'w_in', 'w_pool', 'pool_scale', 'sink', 'w_out', 'norm_ffn2', 'w_ffn2_in', 'w_ffn2_out', 'norm_final']
TWIN_WEIGHTS = ['c_ctx', 'w_mod', 'b_mod', 'norm_ffn1', 'w_ffn1_in', 'w_ffn1_out', 'norm_mix', 'w_in', 'w_pool', 'pool_scale', 'sink', 'w_out', 'norm_ffn2', 'w_ffn2_in', 'w_ffn2_out', 'norm_final']
TWIN_DIFF_INPUT = 'x'
TWIN_INPUTS = ['x', 'c', 'ctx', 'c_ctx', 'w_mod', 'b_mod', 'norm_ffn1', 'w_ffn1_in', 'w_ffn1_out', 'norm_mix', 'w_in', 'w_pool', 'pool_scale', 'sink', 'w_out', 'norm_ffn2', 'w_ffn2_in', 'w_ffn2_out', 'norm_final', 'loss_target', 'm_c_ctx', 'm_w_mod', 'm_b_mod', 'm_norm_ffn1', 'm_w_ffn1_in', 'm_w_ffn1_out', 'm_norm_mix', 'm_w_in', 'm_w_pool', 'm_pool_scale', 'm_sink', 'm_w_out', 'm_norm_ffn2', 'm_w_ffn2_in', 'm_w_ffn2_out', 'm_norm_final', 'v_c_ctx', 'v_w_mod', 'v_b_mod', 'v_norm_ffn1', 'v_w_ffn1_in', 'v_w_ffn1_out', 'v_norm_mix', 'v_w_in', 'v_w_pool', 'v_pool_scale', 'v_sink', 'v_w_out', 'v_norm_ffn2', 'v_w_ffn2_in', 'v_w_ffn2_out', 'v_norm_final']
TWIN_OUTPUTS = ['loss', 'grad_x', 'grad_c_ctx', 'grad_w_mod', 'grad_b_mod', 'grad_norm_ffn1', 'grad_w_ffn1_in', 'grad_w_ffn1_out', 'grad_norm_mix', 'grad_w_in', 'grad_w_pool', 'grad_pool_scale', 'grad_sink', 'grad_w_out', 'grad_norm_ffn2', 'grad_w_ffn2_in', 'grad_w_ffn2_out', 'grad_norm_final', 'delta_c_ctx', 'delta_w_mod', 'delta_b_mod', 'delta_norm_ffn1', 'delta_w_ffn1_in', 'delta_w_ffn1_out', 'delta_norm_mix', 'delta_w_in', 'delta_w_pool', 'delta_pool_scale', 'delta_sink', 'delta_w_out', 'delta_norm_ffn2', 'delta_w_ffn2_in', 'delta_w_ffn2_out', 'delta_norm_final', 'new_m_c_ctx', 'new_m_w_mod', 'new_m_b_mod', 'new_m_norm_ffn1', 'new_m_w_ffn1_in', 'new_m_w_ffn1_out', 'new_m_norm_mix', 'new_m_w_in', 'new_m_w_pool', 'new_m_pool_scale', 'new_m_sink', 'new_m_w_out', 'new_m_norm_ffn2', 'new_m_w_ffn2_in', 'new_m_w_ffn2_out', 'new_m_norm_final', 'new_v_c_ctx', 'new_v_w_mod', 'new_v_b_mod', 'new_v_norm_ffn1', 'new_v_w_ffn1_in', 'new_v_w_ffn1_out', 'new_v_norm_mix', 'new_v_w_in', 'new_v_w_pool', 'new_v_pool_scale', 'new_v_sink', 'new_v_w_out', 'new_v_norm_ffn2', 'new_v_w_ffn2_in', 'new_v_w_ffn2_out', 'new_v_norm_final']
TWIN_LEAF_KINDS = {'loss': 'loss', 'grad_x': 'grad_x', 'grad_c_ctx': 'grad_w', 'grad_w_mod': 'grad_w', 'grad_b_mod': 'grad_w', 'grad_norm_ffn1': 'grad_w', 'grad_w_ffn1_in': 'grad_w', 'grad_w_ffn1_out': 'grad_w', 'grad_norm_mix': 'grad_w', 'grad_w_in': 'grad_w', 'grad_w_pool': 'grad_w', 'grad_pool_scale': 'grad_w', 'grad_sink': 'grad_w', 'grad_w_out': 'grad_w', 'grad_norm_ffn2': 'grad_w', 'grad_w_ffn2_in': 'grad_w', 'grad_w_ffn2_out': 'grad_w', 'grad_norm_final': 'grad_w', 'delta_c_ctx': 'delta_w', 'delta_w_mod': 'delta_w', 'delta_b_mod': 'delta_w', 'delta_norm_ffn1': 'delta_w', 'delta_w_ffn1_in': 'delta_w', 'delta_w_ffn1_out': 'delta_w', 'delta_norm_mix': 'delta_w', 'delta_w_in': 'delta_w', 'delta_w_pool': 'delta_w', 'delta_pool_scale': 'delta_w', 'delta_sink': 'delta_w', 'delta_w_out': 'delta_w', 'delta_norm_ffn2': 'delta_w', 'delta_w_ffn2_in': 'delta_w', 'delta_w_ffn2_out': 'delta_w', 'delta_norm_final': 'delta_w', 'new_m_c_ctx': 'new_m', 'new_m_w_mod': 'new_m', 'new_m_b_mod': 'new_m', 'new_m_norm_ffn1': 'new_m', 'new_m_w_ffn1_in': 'new_m', 'new_m_w_ffn1_out': 'new_m', 'new_m_norm_mix': 'new_m', 'new_m_w_in': 'new_m', 'new_m_w_pool': 'new_m', 'new_m_pool_scale': 'new_m', 'new_m_sink': 'new_m', 'new_m_w_out': 'new_m', 'new_m_norm_ffn2': 'new_m', 'new_m_w_ffn2_in': 'new_m', 'new_m_w_ffn2_out': 'new_m', 'new_m_norm_final': 'new_m', 'new_v_c_ctx': 'new_v', 'new_v_w_mod': 'new_v', 'new_v_b_mod': 'new_v', 'new_v_norm_ffn1': 'new_v', 'new_v_w_ffn1_in': 'new_v', 'new_v_w_ffn1_out': 'new_v', 'new_v_norm_mix': 'new_v', 'new_v_w_in': 'new_v', 'new_v_w_pool': 'new_v', 'new_v_pool_scale': 'new_v', 'new_v_sink': 'new_v', 'new_v_w_out': 'new_v', 'new_v_norm_ffn2': 'new_v', 'new_v_w_ffn2_in': 'new_v', 'new_v_w_ffn2_out': 'new_v', 'new_v_norm_final': 'new_v'}


def _forward(args):
    return _fwd_reference(*[args[k] for k in FWD_PARAMS])


def _output_shape():
    def fwd():
        inp = _fwd_setup_inputs(0)
        return _fwd_reference(*[inp[k] for k in FWD_PARAMS])
    out = _jax.eval_shape(fwd)
    return out.shape, out.dtype

N_MICROBATCH = 1
ADAM_LR = 0.001
ADAM_B1 = 0.9
ADAM_B2 = 0.999
ADAM_EPS = 1e-08
ADAM_WD = 0.01
ADAM_STEP = 10
PER_EXAMPLE_BATCH_AXIS = {'x': 0, 'c': 0, 'ctx': 0, 'loss_target': 0}
SHARED_INPUTS = []
_WEIGHT_DTYPES = {'c_ctx': _jnp.float32, 'w_mod': _jnp.float32, 'b_mod': _jnp.float32, 'norm_ffn1': _jnp.float32, 'w_ffn1_in': _jnp.float32, 'w_ffn1_out': _jnp.float32, 'norm_mix': _jnp.float32, 'w_in': _jnp.float32, 'w_pool': _jnp.float32, 'pool_scale': _jnp.float32, 'sink': _jnp.float32, 'w_out': _jnp.float32, 'norm_ffn2': _jnp.float32, 'w_ffn2_in': _jnp.float32, 'w_ffn2_out': _jnp.float32, 'norm_final': _jnp.float32}
MOMENT_SCALE = {'c_ctx': 1.090555e-01, 'w_mod': 1.128156e-01, 'b_mod': 2.468637e-01, 'norm_ffn1': 7.083061e-02, 'w_ffn1_in': 3.694271e-02, 'w_ffn1_out': 6.217133e-02, 'norm_mix': 8.390151e-02, 'w_in': 1.092532e-01, 'w_pool': 1.071329e-01, 'pool_scale': 1.107612e-01, 'sink': 2.974955e-03, 'w_out': 1.200196e-01, 'norm_ffn2': 6.314845e-02, 'w_ffn2_in': 3.203753e-02, 'w_ffn2_out': 5.459690e-02, 'norm_final': 3.235419e+01}


def _to_microbatches(a, axis):
    t = _jnp.moveaxis(a, axis, 0)
    t = t.reshape((N_MICROBATCH, t.shape[0] // N_MICROBATCH) + t.shape[1:])
    return _jnp.moveaxis(t, 1, axis + 1)


def setup_inputs(seed: int = 0) -> dict:
    inp = _fwd_setup_inputs(seed)
    key = _jax.random.fold_in(_jax.random.key(seed), 7919)
    shape, _ = _output_shape()
    out = dict(inp)
    out["loss_target"] = _jax.random.normal(_jax.random.fold_in(key, 0), shape, _jnp.float32)
    for i, name in enumerate(TWIN_WEIGHTS):
        w = inp[name].astype(_jnp.float32)
        if MOMENT_SCALE is None:
            s = _jnp.sqrt(_jnp.mean(_jnp.square(w)) + 1e-30)
        else:
            s = MOMENT_SCALE[name]
        km, kv = _jax.random.split(_jax.random.fold_in(key, i + 1))
        out[name] = w
        out["m_" + name] = s * _jax.random.normal(km, w.shape, _jnp.float32)
        out["v_" + name] = (s * s) * _jax.random.uniform(kv, w.shape, _jnp.float32, 0.5, 1.5)
    if N_MICROBATCH > 1:
        for name, axis in PER_EXAMPLE_BATCH_AXIS.items():
            out[name] = _to_microbatches(out[name], axis)
    return {'x': out['x'], 'c': out['c'], 'ctx': out['ctx'], 'c_ctx': out['c_ctx'], 'w_mod': out['w_mod'], 'b_mod': out['b_mod'], 'norm_ffn1': out['norm_ffn1'], 'w_ffn1_in': out['w_ffn1_in'], 'w_ffn1_out': out['w_ffn1_out'], 'norm_mix': out['norm_mix'], 'w_in': out['w_in'], 'w_pool': out['w_pool'], 'pool_scale': out['pool_scale'], 'sink': out['sink'], 'w_out': out['w_out'], 'norm_ffn2': out['norm_ffn2'], 'w_ffn2_in': out['w_ffn2_in'], 'w_ffn2_out': out['w_ffn2_out'], 'norm_final': out['norm_final'], 'loss_target': out['loss_target'], 'm_c_ctx': out['m_c_ctx'], 'm_w_mod': out['m_w_mod'], 'm_b_mod': out['m_b_mod'], 'm_norm_ffn1': out['m_norm_ffn1'], 'm_w_ffn1_in': out['m_w_ffn1_in'], 'm_w_ffn1_out': out['m_w_ffn1_out'], 'm_norm_mix': out['m_norm_mix'], 'm_w_in': out['m_w_in'], 'm_w_pool': out['m_w_pool'], 'm_pool_scale': out['m_pool_scale'], 'm_sink': out['m_sink'], 'm_w_out': out['m_w_out'], 'm_norm_ffn2': out['m_norm_ffn2'], 'm_w_ffn2_in': out['m_w_ffn2_in'], 'm_w_ffn2_out': out['m_w_ffn2_out'], 'm_norm_final': out['m_norm_final'], 'v_c_ctx': out['v_c_ctx'], 'v_w_mod': out['v_w_mod'], 'v_b_mod': out['v_b_mod'], 'v_norm_ffn1': out['v_norm_ffn1'], 'v_w_ffn1_in': out['v_w_ffn1_in'], 'v_w_ffn1_out': out['v_w_ffn1_out'], 'v_norm_mix': out['v_norm_mix'], 'v_w_in': out['v_w_in'], 'v_w_pool': out['v_w_pool'], 'v_pool_scale': out['v_pool_scale'], 'v_sink': out['v_sink'], 'v_w_out': out['v_w_out'], 'v_norm_ffn2': out['v_norm_ffn2'], 'v_w_ffn2_in': out['v_w_ffn2_in'], 'v_w_ffn2_out': out['v_w_ffn2_out'], 'v_norm_final': out['v_norm_final']}


def _loss(weights, diff, rest, loss_target):
    with _jax.named_scope("forward"):
        args = {**rest, TWIN_DIFF_INPUT: diff, **{k: w.astype(_WEIGHT_DTYPES[k]) for k, w in weights.items()}}
        y = _forward(args)
    with _jax.named_scope("loss_head"):
        err = _jnp.square(y.astype(_jnp.float32) - loss_target)
        return 0.5 * _jnp.sum(_jnp.mean(err, axis=-1)) if err.ndim else 0.5 * err


def _adamw(w, g, m, v):
    m = ADAM_B1 * m + (1.0 - ADAM_B1) * g
    v = ADAM_B2 * v + (1.0 - ADAM_B2) * _jnp.square(g)
    m_hat = m / (1.0 - ADAM_B1 ** ADAM_STEP)
    v_hat = v / (1.0 - ADAM_B2 ** ADAM_STEP)
    delta = -ADAM_LR * (m_hat / (_jnp.sqrt(v_hat) + ADAM_EPS) + ADAM_WD * w)
    return delta, m, v


def reference(x, c, ctx, c_ctx, w_mod, b_mod, norm_ffn1, w_ffn1_in, w_ffn1_out, norm_mix, w_in, w_pool, pool_scale, sink, w_out, norm_ffn2, w_ffn2_in, w_ffn2_out, norm_final, loss_target, m_c_ctx, m_w_mod, m_b_mod, m_norm_ffn1, m_w_ffn1_in, m_w_ffn1_out, m_norm_mix, m_w_in, m_w_pool, m_pool_scale, m_sink, m_w_out, m_norm_ffn2, m_w_ffn2_in, m_w_ffn2_out, m_norm_final, v_c_ctx, v_w_mod, v_b_mod, v_norm_ffn1, v_w_ffn1_in, v_w_ffn1_out, v_norm_mix, v_w_in, v_w_pool, v_pool_scale, v_sink, v_w_out, v_norm_ffn2, v_w_ffn2_in, v_w_ffn2_out, v_norm_final):
    given = dict(x=x, c=c, ctx=ctx, c_ctx=c_ctx, w_mod=w_mod, b_mod=b_mod, norm_ffn1=norm_ffn1, w_ffn1_in=w_ffn1_in, w_ffn1_out=w_ffn1_out, norm_mix=norm_mix, w_in=w_in, w_pool=w_pool, pool_scale=pool_scale, sink=sink, w_out=w_out, norm_ffn2=norm_ffn2, w_ffn2_in=w_ffn2_in, w_ffn2_out=w_ffn2_out, norm_final=norm_final, loss_target=loss_target, m_c_ctx=m_c_ctx, m_w_mod=m_w_mod, m_b_mod=m_b_mod, m_norm_ffn1=m_norm_ffn1, m_w_ffn1_in=m_w_ffn1_in, m_w_ffn1_out=m_w_ffn1_out, m_norm_mix=m_norm_mix, m_w_in=m_w_in, m_w_pool=m_w_pool, m_pool_scale=m_pool_scale, m_sink=m_sink, m_w_out=m_w_out, m_norm_ffn2=m_norm_ffn2, m_w_ffn2_in=m_w_ffn2_in, m_w_ffn2_out=m_w_ffn2_out, m_norm_final=m_norm_final, v_c_ctx=v_c_ctx, v_w_mod=v_w_mod, v_b_mod=v_b_mod, v_norm_ffn1=v_norm_ffn1, v_w_ffn1_in=v_w_ffn1_in, v_w_ffn1_out=v_w_ffn1_out, v_norm_mix=v_norm_mix, v_w_in=v_w_in, v_w_pool=v_w_pool, v_pool_scale=v_pool_scale, v_sink=v_sink, v_w_out=v_w_out, v_norm_ffn2=v_norm_ffn2, v_w_ffn2_in=v_w_ffn2_in, v_w_ffn2_out=v_w_ffn2_out, v_norm_final=v_norm_final)
    weights = {n: given[n] for n in TWIN_WEIGHTS}
    shared = {n: given[n] for n in SHARED_INPUTS}
    per_example = {n: given[n] for n in ['x', 'c', 'ctx']}
    grad_fn = _jax.value_and_grad(_loss, argnums=(0, 1))

    def one_microbatch(ex, loss_target):
        ex = dict(ex)
        diff = ex.pop(TWIN_DIFF_INPUT)
        return grad_fn(weights, diff, {**shared, **ex}, loss_target)

    if N_MICROBATCH == 1:
        loss, (grad_w, grad_x) = one_microbatch(per_example, given["loss_target"])
    else:
        def body(carry, xs):
            loss_sum, grad_sum = carry
            l_k, (gw_k, gx_k) = one_microbatch(xs[0], xs[1])
            with _jax.named_scope("update"):
                return (loss_sum + l_k, _jax.tree.map(_jnp.add, grad_sum, gw_k)), gx_k

        init = (_jnp.zeros((), _jnp.float32), _jax.tree.map(_jnp.zeros_like, weights))
        (loss, grad_w), grad_x = _jax.lax.scan(body, init, (per_example, given["loss_target"]))
    with _jax.named_scope("update"):
        delta_w, new_m, new_v = {}, {}, {}
        for n in TWIN_WEIGHTS:
            delta_w[n], new_m[n], new_v[n] = _adamw(weights[n], grad_w[n], given["m_" + n], given["v_" + n])
    return (loss, grad_x, *[grad_w[n] for n in TWIN_WEIGHTS], *[delta_w[n] for n in TWIN_WEIGHTS],
            *[new_m[n] for n in TWIN_WEIGHTS], *[new_v[n] for n in TWIN_WEIGHTS])
```

```python
import functools

import jax
import jax.numpy as jnp
from jax import lax
from jax.experimental import pallas as pl
from jax.experimental.pallas import tpu as pltpu

F32, BF16 = jnp.float32, jnp.bfloat16

D = 1024
LC = 256
DFF = 2816
NMOD = 9
PW = 512
AW = 512
KVW = 128
PROJ = PW + AW + 2 * KVW
HD = 64
BLK = 128
GRID_W = 64
POOL_WINDOWS = (2, 4, 8, 16)
EPS = 1e-6
NEG = -1e30
ROPE_BASE = 10000.0
NDEV = 8
MESH = pl.DeviceIdType.MESH

ADAM_LR, ADAM_B1, ADAM_B2, ADAM_EPS, ADAM_WD, ADAM_STEP = 0.001, 0.9, 0.999, 1e-08, 0.01, 10

VMEM_LIMIT = 56 * 1024 * 1024
TM = 256
FCH = 1408

ANY = pl.BlockSpec(memory_space=pl.ANY)
SMEM = pl.BlockSpec(memory_space=pltpu.SMEM)


def _params(ngrid=1):
    return pltpu.CompilerParams(dimension_semantics=("arbitrary",) * ngrid, vmem_limit_bytes=VMEM_LIMIT)


def _dot(a, b):
    return jnp.dot(a, b, preferred_element_type=F32)


def _dot_nt(a, b):
    return lax.dot_general(a, b, (((1,), (1,)), ((), ())), preferred_element_type=F32)


def _dot_tn(a, b):
    return lax.dot_general(a, b, (((0,), (0,)), ((), ())), preferred_element_type=F32)


def _sigmoid(x):
    return 1.0 / (1.0 + jnp.exp(-x))


def _rows(tm, w):
    return pl.BlockSpec((tm, w), lambda i: (i, 0))


def _full(shape):
    nd = len(shape)
    return pl.BlockSpec(shape, lambda *_: (0,) * nd)


def _sds(shape, dtype):
    return jax.ShapeDtypeStruct(shape, dtype)


def _norm_mod(h, g, shift, scale):
    r = lax.rsqrt(jnp.mean(h * h, axis=-1, keepdims=True) + EPS)
    xhat = h * r
    y = xhat * g
    return r, xhat, y, y * (1.0 + scale) + shift


def _norm_mod_bwd(dn, r, xhat, y, g, scale):
    dshift = jnp.sum(dn, axis=0, keepdims=True)
    dscale = jnp.sum(dn * y, axis=0, keepdims=True)
    dy = dn * (1.0 + scale)
    dg = jnp.sum(dy * xhat, axis=0, keepdims=True)
    dxh = dy * g
    dh = r * (dxh - xhat * jnp.mean(dxh * xhat, axis=-1, keepdims=True))
    return dh, dshift, dscale, dg


def _acc_partials(part_ref, first, rows):
    @pl.when(first)
    def _():
        part_ref[...] = jnp.zeros_like(part_ref)

    for r, val in rows.items():
        part_ref[0, r : r + 1, :] += val


def _mod_spec(n_lat):
    return pl.BlockSpec((1, 16, D), lambda i: (i // n_lat, 0, 0))


def _part_spec(n_lat):
    return pl.BlockSpec((1, 8, D), lambda i: (i // n_lat, 0, 0))


def _load_weights(pairs, sem):
    copies = [pltpu.make_async_copy(src, dst, sem.at[k]) for k, (src, dst) in enumerate(pairs)]
    for cp in copies:
        cp.start()
    for cp in copies:
        cp.wait()


def _ffn_fwd(h, modv, gvec, win, wout, *, T, mrow, grow, ctx_active, name):
    R = h.shape[0]
    n_lat, n_tiles = T // TM, R // TM

    def body(h_ref, mod_ref, g_ref, win_hbm, wout_hbm, ho_ref, a_ref, b_ref, f_ref, win_v, wout_v, sem):
        i = pl.program_id(0)

        @pl.when(i == 0)
        def _():
            _load_weights([(win_hbm, win_v), (wout_hbm, wout_v)], sem)

        def compute():
            h = h_ref[...]
            shift, scale, gate = (mod_ref[0, mrow + k : mrow + k + 1, :] for k in range(3))
            _, _, _, n = _norm_mod(h, g_ref[grow : grow + 1, :], shift, scale)
            n_bf = n.astype(BF16)
            acc = jnp.zeros((TM, D), F32)
            for c0 in range(0, DFF, FCH):
                a = _dot_nt(n_bf, win_v[c0 : c0 + FCH, :])
                b = _dot_nt(n_bf, win_v[DFF + c0 : DFF + c0 + FCH, :])
                a_ref[:, c0 : c0 + FCH] = a.astype(BF16)
                b_ref[:, c0 : c0 + FCH] = b.astype(BF16)
                s = a * _sigmoid(a) * b
                acc = acc + _dot(s.astype(BF16), wout_v[c0 : c0 + FCH, :])
            f_ref[...] = acc.astype(BF16)
            ho_ref[...] = h + (0.5 * gate) * acc

        if ctx_active:
            compute()
        else:
            pl.when(i < n_lat)(compute)

            @pl.when(i >= n_lat)
            def _():
                ho_ref[...] = h_ref[...]
                a_ref[...] = jnp.zeros_like(a_ref)
                b_ref[...] = jnp.zeros_like(b_ref)
                f_ref[...] = jnp.zeros_like(f_ref)

    return pl.pallas_call(
        body,
        name=name,
        grid=(n_tiles,),
        in_specs=[_rows(TM, D), _mod_spec(n_lat), _full((8, D)), ANY, ANY],
        out_specs=[_rows(TM, D), _rows(TM, DFF), _rows(TM, DFF), _rows(TM, D)],
        out_shape=[_sds((R, D), F32), _sds((R, DFF), BF16), _sds((R, DFF), BF16), _sds((R, D), BF16)],
        scratch_shapes=[pltpu.VMEM((2 * DFF, D), BF16), pltpu.VMEM((DFF, D), BF16), pltpu.SemaphoreType.DMA((2,))],
        compiler_params=_params(),
    )(h, modv, gvec, win, wout)


def _ffn_bwd(h, dho, a, b, f, modv, gvec, win, wout, *, T, mrow, grow, ctx_active, name):
    R = h.shape[0]
    n_lat, n_tiles = T // TM, R // TM

    def body(h_ref, dho_ref, a_ref, b_ref, f_ref, mod_ref, g_ref, win_hbm, wout_hbm,
             dh_ref, dab_ref, s_ref, n_ref, df_ref, part_ref, win_v, wout_v, sem):
        i = pl.program_id(0)

        @pl.when(i == 0)
        def _():
            _load_weights([(win_hbm, win_v), (wout_hbm, wout_v)], sem)

        first = jnp.logical_or(i == 0, i == n_lat)

        def compute():
            h = h_ref[...]
            dho = dho_ref[...]
            shift, scale, gate = (mod_ref[0, mrow + k : mrow + k + 1, :] for k in range(3))
            g = g_ref[grow : grow + 1, :]
            r, xhat, y, n = _norm_mod(h, g, shift, scale)
            dgate = 0.5 * jnp.sum(dho * f_ref[...].astype(F32), axis=0, keepdims=True)
            df_bf = ((0.5 * gate) * dho).astype(BF16)
            df_ref[...] = df_bf
            n_ref[...] = n.astype(BF16)
            dn = jnp.zeros((TM, D), F32)
            for c0 in range(0, DFF, FCH):
                ds = _dot_nt(df_bf, wout_v[c0 : c0 + FCH, :])
                av = a_ref[:, c0 : c0 + FCH].astype(F32)
                bv = b_ref[:, c0 : c0 + FCH].astype(F32)
                sig = _sigmoid(av)
                sa = av * sig
                s_ref[:, c0 : c0 + FCH] = (sa * bv).astype(BF16)
                da = (ds * bv * (sig * (1.0 + av * (1.0 - sig)))).astype(BF16)
                db = (ds * sa).astype(BF16)
                dab_ref[:, c0 : c0 + FCH] = da
                dab_ref[:, DFF + c0 : DFF + c0 + FCH] = db
                dn = dn + _dot(da, win_v[c0 : c0 + FCH, :]) + _dot(db, win_v[DFF + c0 : DFF + c0 + FCH, :])
            dh, dshift, dscale, dg = _norm_mod_bwd(dn, r, xhat, y, g, scale)
            dh_ref[...] = dho + dh
            _acc_partials(part_ref, first, {0: dshift, 1: dscale, 2: dgate, 3: dg})

        if ctx_active:
            compute()
        else:
            pl.when(i < n_lat)(compute)

            @pl.when(i >= n_lat)
            def _():
                dh_ref[...] = dho_ref[...]
                dab_ref[...] = jnp.zeros_like(dab_ref)
                s_ref[...] = jnp.zeros_like(s_ref)
                n_ref[...] = jnp.zeros_like(n_ref)
                df_ref[...] = jnp.zeros_like(df_ref)
                part_ref[...] = jnp.zeros_like(part_ref)

    return pl.pallas_call(
        body,
        name=name,
        grid=(n_tiles,),
        in_specs=[_rows(TM, D), _rows(TM, D), _rows(TM, DFF), _rows(TM, DFF), _rows(TM, D),
                  _mod_spec(n_lat), _full((8, D)), ANY, ANY],
        out_specs=[_rows(TM, D), _rows(TM, 2 * DFF), _rows(TM, DFF), _rows(TM, D), _rows(TM, D), _part_spec(n_lat)],
        out_shape=[_sds((R, D), F32), _sds((R, 2 * DFF), BF16), _sds((R, DFF), BF16), _sds((R, D), BF16),
                   _sds((R, D), BF16), _sds((2, 8, D), F32)],
        scratch_shapes=[pltpu.VMEM((2 * DFF, D), BF16), pltpu.VMEM((DFF, D), BF16), pltpu.SemaphoreType.DMA((2,))],
        compiler_params=_params(),
    )(h, dho, a, b, f, modv, gvec, win, wout)


def _wgrad(x, y, *, bk, sh, name):
    R, kx = x.shape
    n = y.shape[1]
    tr = R // 2
    nr, nsh = R // tr, bk // sh

    def body(x_ref, y_ref, o_ref, acc):
        r = pl.program_id(1)

        @pl.when(r == 0)
        def _():
            acc[...] = jnp.zeros_like(acc)

        acc[...] += _dot_tn(x_ref[...], y_ref[...])

        @pl.when(r == nr - 1)
        def _():
            for s in range(nsh):
                o_ref[s] = acc[s * sh : (s + 1) * sh, :].astype(BF16)

    return pl.pallas_call(
        body,
        name=name,
        grid=(kx // bk, nr),
        in_specs=[pl.BlockSpec((tr, bk), lambda k, r: (r, k)), pl.BlockSpec((tr, n), lambda k, r: (r, 0))],
        out_specs=pl.BlockSpec((nsh, sh, n), lambda k, r: (k, 0, 0)),
        out_shape=_sds((kx // sh, sh, n), BF16),
        scratch_shapes=[pltpu.VMEM((bk, n), F32)],
        compiler_params=_params(2),
    )(x, y)


def _rot_half(x):
    lane = lax.broadcasted_iota(jnp.int32, x.shape, 1)
    return jnp.where((lane & (HD - 1)) < HD // 2, -pltpu.roll(x, 128 - HD // 2, 1), pltpu.roll(x, HD // 2, 1))


def _tile_sel():
    i = lax.broadcasted_iota(jnp.int32, (KVW, AW), 0)
    j = lax.broadcasted_iota(jnp.int32, (KVW, AW), 1)
    return jnp.where(i == (j // 256) * HD + (j & (HD - 1)), 1.0, 0.0).astype(BF16)


def _mixproj_fwd(h, modv, gvec, win, cos, sin, *, T, name):
    R = h.shape[0]
    n_lat, n_tiles = T // TM, R // TM

    def body(h_ref, mod_ref, g_ref, win_ref, cos_ref, sin_ref, u_ref, q_ref, k4_ref, v4_ref):
        shift, scale = mod_ref[0, 3:4, :], mod_ref[0, 4:5, :]
        _, _, _, n = _norm_mod(h_ref[...], g_ref[1:2, :], shift, scale)
        proj = _dot_nt(n.astype(BF16), win_ref[...])
        u_ref[...] = proj[:, :PW]
        cs, sn = cos_ref[...], sin_ref[...]
        for s in range(AW // 128):
            x = proj[:, PW + 128 * s : PW + 128 * (s + 1)]
            q_ref[:, 128 * s : 128 * (s + 1)] = (x * cs + _rot_half(x) * sn).astype(BF16)
        k = proj[:, PW + AW : PW + AW + KVW]
        k = (k * cs + _rot_half(k) * sn).astype(BF16)
        v = proj[:, PW + AW + KVW :].astype(BF16)
        sel = _tile_sel()
        k4_ref[...] = _dot(k, sel).astype(BF16)
        v4_ref[...] = _dot(v, sel).astype(BF16)

    return pl.pallas_call(
        body,
        name=name,
        grid=(n_tiles,),
        in_specs=[_rows(TM, D), _mod_spec(n_lat), _full((8, D)), _full((PROJ, D)), _rows(TM, 128), _rows(TM, 128)],
        out_specs=[_rows(TM, PW), _rows(TM, AW), _rows(TM, AW), _rows(TM, AW)],
        out_shape=[_sds((R, PW), F32), _sds((R, AW), BF16), _sds((R, AW), BF16), _sds((R, AW), BF16)],
        compiler_params=_params(),
    )(h, modv, gvec, win, cos, sin)


def _win_start(j, hi):
    return pl.multiple_of(jnp.clip((j - 1) * BLK, 0, hi - 3 * BLK), BLK)


def _hi_lo(x):
    hi = x.astype(BF16)
    return hi, (x - hi.astype(F32)).astype(BF16)


def _pool_bounds(t, w, T, R):
    is_ctx = t >= T
    lo = jnp.maximum(t - w // 2, jnp.where(is_ctx, T, 0))
    hi = jnp.minimum(t + w // 2, jnp.where(is_ctx, R, T))
    return lo, hi


def _pooled(u_v, j, T, R):
    start = _win_start(j, R)
    u3_hi, u3_lo = _hi_lo(u_v[pl.ds(start, 3 * BLK), :])
    ub = u_v[pl.ds(pl.multiple_of(j * BLK, BLK), BLK), :]
    t = j * BLK + lax.broadcasted_iota(jnp.int32, (BLK, 1), 0)
    pos = start + lax.broadcasted_iota(jnp.int32, (1, 3 * BLK), 1)
    pooled, counts = [], []
    for g, w in enumerate(POOL_WINDOWS):
        lo, hi = _pool_bounds(t, w, T, R)
        band = jnp.where(pos >= lo, jnp.where(pos < hi, 1.0, 0.0), 0.0).astype(BF16)
        sl = slice(g * 128, (g + 1) * 128)
        sums = _dot(band, u3_hi[:, sl]) + _dot(band, u3_lo[:, sl])
        cnt = (hi - lo).astype(F32)
        pooled.append(sums / cnt - ub[:, sl])
        counts.append(cnt)
    return pooled, counts


def _stack_heads(x):
    lane_h = lax.broadcasted_iota(jnp.int32, x.shape, 1) // HD
    return jnp.concatenate([jnp.where(lane_h == h, x, jnp.zeros_like(x)) for h in range(4)], axis=0)


def _unstack_heads(x):
    lane_h = lax.broadcasted_iota(jnp.int32, (BLK, 256), 1) // HD
    out = jnp.zeros((BLK, 256), F32)
    for h in range(4):
        out = out + jnp.where(lane_h == h, x[h * BLK : (h + 1) * BLK, :], 0.0)
    return out


def _attn_probs(qs, kl, kc, sink_ref, g, j, start_l, nbl):
    s_l = _dot_nt(qs, kl) * (HD ** -0.5)
    s_c = _dot_nt(qs, kc) * (HD ** -0.5)
    rowi = lax.broadcasted_iota(jnp.int32, (4 * BLK, 1), 0)
    qpos = j * BLK + (rowi & (BLK - 1))
    kpos = start_l + lax.broadcasted_iota(jnp.int32, (1, 3 * BLK), 1)
    reach = jnp.where(j < nbl, BLK, -1)
    s_l = jnp.where(jnp.abs(kpos - qpos) <= reach, s_l, NEG)
    rb = rowi // BLK
    sk = jnp.where(rb == 0, sink_ref[4 * g], jnp.where(rb == 1, sink_ref[4 * g + 1],
                   jnp.where(rb == 2, sink_ref[4 * g + 2], sink_ref[4 * g + 3])))
    m = jnp.maximum(jnp.maximum(jnp.max(s_l, axis=1, keepdims=True), jnp.max(s_c, axis=1, keepdims=True)), sk)
    e_l, e_c, e_s = jnp.exp(s_l - m), jnp.exp(s_c - m), jnp.exp(sk - m)
    inv = 1.0 / (jnp.sum(e_l, axis=1, keepdims=True) + jnp.sum(e_c, axis=1, keepdims=True) + e_s)
    return e_l * inv, e_c * inv, e_s * inv


def _attnpool_fwd(u, q, k4, v4, sink, w_pool, pool_scale, *, T, name):
    R = u.shape[0]
    nb, nbl = R // BLK, T // BLK

    def body(q_ref, sink_ref, wp_ref, ps_ref, u_hbm, k4_hbm, v4_hbm, cat_ref, u_v, k4_v, v4_v, sem):
        j = pl.program_id(0)

        @pl.when(j == 0)
        def _():
            _load_weights([(u_hbm, u_v), (k4_hbm, k4_v), (v4_hbm, v4_v)], sem)

        pooled, _ = _pooled(u_v, j, T, R)
        for g in range(4):
            mixed = _dot(pooled[g].astype(BF16), wp_ref[g].astype(BF16)) * ps_ref[:, g * 128 : (g + 1) * 128]
            cat_ref[:, g * 128 : (g + 1) * 128] = mixed.astype(BF16)

        start_l = _win_start(j, T)
        for g in range(2):
            gl = slice(g * 256, (g + 1) * 256)
            qs = _stack_heads(q_ref[:, gl])
            p_l, p_c, _ = _attn_probs(qs, k4_v[pl.ds(start_l, 3 * BLK), gl], k4_v[T:R, gl], sink_ref, g, j, start_l, nbl)
            o = _dot(p_l.astype(BF16), v4_v[pl.ds(start_l, 3 * BLK), gl]) + _dot(p_c.astype(BF16), v4_v[T:R, gl])
            cat_ref[:, PW + g * 256 : PW + (g + 1) * 256] = _unstack_heads(o).astype(BF16)

    return pl.pallas_call(
        body,
        name=name,
        grid=(nb,),
        in_specs=[_rows(BLK, AW), SMEM, _full((4, 128, 128)), _full((1, PW)), ANY, ANY, ANY],
        out_specs=_rows(BLK, D),
        out_shape=_sds((R, D), BF16),
        scratch_shapes=[pltpu.VMEM((R, PW), F32), pltpu.VMEM((R, AW), BF16), pltpu.VMEM((R, AW), BF16),
                        pltpu.SemaphoreType.DMA((3,))],
        compiler_params=_params(),
    )(q, sink, w_pool, pool_scale, u, k4, v4)


def _mixout_fwd(h, cat, modv, wout, *, T, ctx_active, name):
    R = h.shape[0]
    n_lat, n_tiles = T // TM, R // TM

    def body(h_ref, cat_ref, mod_ref, w_ref, ho_ref, mo_ref):
        i = pl.program_id(0)

        def compute():
            mo = _dot(cat_ref[...], w_ref[...])
            mo_ref[...] = mo.astype(BF16)
            ho_ref[...] = h_ref[...] + mod_ref[0, 5:6, :] * mo

        if ctx_active:
            compute()
        else:
            pl.when(i < n_lat)(compute)

            @pl.when(i >= n_lat)
            def _():
                ho_ref[...] = h_ref[...]
                mo_ref[...] = jnp.zeros_like(mo_ref)

    return pl.pallas_call(
        body,
        name=name,
        grid=(n_tiles,),
        in_specs=[_rows(TM, D), _rows(TM, D), _mod_spec(n_lat), _full((D, D))],
        out_specs=[_rows(TM, D), _rows(TM, D)],
        out_shape=[_sds((R, D), F32), _sds((R, D), BF16)],
        compiler_params=_params(),
    )(h, cat, modv, wout)


def _mixout_bwd(dho, mo, modv, wout, *, T, ctx_active, name):
    R = dho.shape[0]
    n_lat, n_tiles = T // TM, R // TM

    def body(dho_ref, mo_ref, mod_ref, w_ref, dcat_ref, dmix_ref, part_ref):
        i = pl.program_id(0)
        first = jnp.logical_or(i == 0, i == n_lat)

        def compute():
            dho = dho_ref[...]
            dmix = (mod_ref[0, 5:6, :] * dho).astype(BF16)
            dmix_ref[...] = dmix
            dcat_ref[...] = _dot_nt(dmix, w_ref[...])
            dgate = jnp.sum(dho * mo_ref[...].astype(F32), axis=0, keepdims=True)
            _acc_partials(part_ref, first, {2: dgate})

        if ctx_active:
            compute()
        else:
            pl.when(i < n_lat)(compute)

            @pl.when(i >= n_lat)
            def _():
                dcat_ref[...] = jnp.zeros_like(dcat_ref)
                dmix_ref[...] = jnp.zeros_like(dmix_ref)
                part_ref[...] = jnp.zeros_like(part_ref)

    return pl.pallas_call(
        body,
        name=name,
        grid=(n_tiles,),
        in_specs=[_rows(TM, D), _rows(TM, D), _mod_spec(n_lat), _full((D, D))],
        out_specs=[_rows(TM, D), _rows(TM, D), _part_spec(n_lat)],
        out_shape=[_sds((R, D), F32), _sds((R, D), BF16), _sds((2, 8, D), F32)],
        compiler_params=_params(),
    )(dho, mo, modv, wout)


def _pool_bwd(u, dcat, w_pool, pool_scale, *, T, name):
    R = u.shape[0]
    nb = R // BLK

    def body(dcat_ref, wp_ref, ps_ref, u_hbm, dps_ref, dwp_ref, dsc_ref, u_v, sem):
        j = pl.program_id(0)

        @pl.when(j == 0)
        def _():
            _load_weights([(u_hbm, u_v)], sem)
            dwp_ref[...] = jnp.zeros_like(dwp_ref)
            dsc_ref[...] = jnp.zeros_like(dsc_ref)

        pooled, counts = _pooled(u_v, j, T, R)
        for g in range(4):
            sl = slice(g * 128, (g + 1) * 128)
            p_bf = pooled[g].astype(BF16)
            w_bf = wp_ref[g].astype(BF16)
            dmixed = dcat_ref[:, sl]
            dsc_ref[0:1, sl] += jnp.sum(dmixed * _dot(p_bf, w_bf), axis=0, keepdims=True)
            dmp = (dmixed * ps_ref[:, sl]).astype(BF16)
            dwp_ref[sl, :] += _dot_tn(p_bf, dmp)
            dps_ref[:, sl] = _dot_nt(dmp, w_bf) / counts[g]

    return pl.pallas_call(
        body,
        name=name,
        grid=(nb,),
        in_specs=[_rows(BLK, D), _full((4, 128, 128)), _full((1, PW)), ANY],
        out_specs=[_rows(BLK, PW), _full((PW, 128)), _full((8, PW))],
        out_shape=[_sds((R, PW), F32), _sds((PW, 128), F32), _sds((8, PW), F32)],
        scratch_shapes=[pltpu.VMEM((R, PW), F32), pltpu.SemaphoreType.DMA((1,))],
        compiler_params=_params(),
    )(dcat, w_pool, pool_scale, u)


def _fold_heads(x):
    y = x[:, :128] + x[:, 128:]
    return y + pltpu.roll(y, HD, 1)


def _attn_bwd(q, k4, v4, dcat, dps, sink, *, T, name):
    R = q.shape[0]
    nb, nbl = R // BLK, T // BLK

    def body(q_ref, dcat_ref, sink_ref, k4_hbm, v4_hbm, dps_hbm, du_ref, dq_ref, dk_ref, dv_ref, dsk_ref,
             k4_v, v4_v, dps_v, sem):
        j = pl.program_id(0)

        @pl.when(j == 0)
        def _():
            _load_weights([(k4_hbm, k4_v), (v4_hbm, v4_v), (dps_hbm, dps_v)], sem)
            dk_ref[...] = jnp.zeros_like(dk_ref)
            dv_ref[...] = jnp.zeros_like(dv_ref)
            dsk_ref[...] = jnp.zeros_like(dsk_ref)

        start = _win_start(j, R)
        d3_hi, d3_lo = _hi_lo(dps_v[pl.ds(start, 3 * BLK), :])
        db = dps_v[pl.ds(pl.multiple_of(j * BLK, BLK), BLK), :]
        pos = j * BLK + lax.broadcasted_iota(jnp.int32, (BLK, 1), 0)
        t_r = start + lax.broadcasted_iota(jnp.int32, (1, 3 * BLK), 1)
        for g, w in enumerate(POOL_WINDOWS):
            sl = slice(g * 128, (g + 1) * 128)
            lo_r, hi_r = _pool_bounds(t_r, w, T, R)
            band_t = jnp.where(pos >= lo_r, jnp.where(pos < hi_r, 1.0, 0.0), 0.0).astype(BF16)
            lo_c, hi_c = _pool_bounds(pos, w, T, R)
            du_ref[:, sl] = _dot(band_t, d3_hi[:, sl]) + _dot(band_t, d3_lo[:, sl]) - db[:, sl] * (hi_c - lo_c).astype(F32)

        start_l = _win_start(j, T)
        rb = lax.broadcasted_iota(jnp.int32, (4 * BLK, 1), 0) // BLK
        lane = lax.broadcasted_iota(jnp.int32, (1, 128), 1)
        dk_l, dk_c, dv_l, dv_c = [], [], [], []
        for g in range(2):
            gl = slice(g * 256, (g + 1) * 256)
            qs = _stack_heads(q_ref[:, gl])
            kl, kc = k4_v[pl.ds(start_l, 3 * BLK), gl], k4_v[T:R, gl]
            vl, vc = v4_v[pl.ds(start_l, 3 * BLK), gl], v4_v[T:R, gl]
            p_l, p_c, p_s = _attn_probs(qs, kl, kc, sink_ref, g, j, start_l, nbl)
            dos = _stack_heads(dcat_ref[:, PW + g * 256 : PW + (g + 1) * 256]).astype(BF16)
            dp_l, dp_c = _dot_nt(dos, vl), _dot_nt(dos, vc)
            delta = jnp.sum(p_l * dp_l, axis=1, keepdims=True) + jnp.sum(p_c * dp_c, axis=1, keepdims=True)
            ds_l = (p_l * (dp_l - delta) * (HD ** -0.5)).astype(BF16)
            ds_c = (p_c * (dp_c - delta) * (HD ** -0.5)).astype(BF16)
            dq_ref[:, gl] = _unstack_heads(_dot(ds_l, kl) + _dot(ds_c, kc))
            dk_l.append(_fold_heads(_dot_tn(ds_l, qs)))
            dk_c.append(_fold_heads(_dot_tn(ds_c, qs)))
            dv_l.append(_fold_heads(_dot_tn(p_l.astype(BF16), dos)))
            dv_c.append(_fold_heads(_dot_tn(p_c.astype(BF16), dos)))
            dsink = -p_s * delta
            for h in range(4):
                tot = jnp.sum(jnp.where(rb == h, dsink, 0.0), axis=0, keepdims=True)
                dsk_ref[4 * g + h : 4 * g + h + 1, :] += jnp.broadcast_to(tot, (1, 128))
        first = lane < HD
        dk_ref[pl.ds(start_l, 3 * BLK), :] += jnp.where(first, dk_l[0], dk_l[1])
        dk_ref[T:R, :] += jnp.where(first, dk_c[0], dk_c[1])
        dv_ref[pl.ds(start_l, 3 * BLK), :] += jnp.where(first, dv_l[0], dv_l[1])
        dv_ref[T:R, :] += jnp.where(first, dv_c[0], dv_c[1])

    return pl.pallas_call(
        body,
        name=name,
        grid=(nb,),
        in_specs=[_rows(BLK, AW), _rows(BLK, D), SMEM, ANY, ANY, ANY],
        out_specs=[_rows(BLK, PW), _rows(BLK, AW), _full((R, KVW)), _full((R, KVW)), _full((8, 128))],
        out_shape=[_sds((R, PW), F32), _sds((R, AW), F32), _sds((R, KVW), F32), _sds((R, KVW), F32),
                   _sds((8, 128), F32)],
        scratch_shapes=[pltpu.VMEM((R, AW), BF16), pltpu.VMEM((R, AW), BF16), pltpu.VMEM((R, PW), F32),
                        pltpu.SemaphoreType.DMA((3,))],
        compiler_params=_params(),
    )(q, dcat, sink, k4, v4, dps)


def _mixproj_bwd(h, dho, du, dq, dk, dv, modv, gvec, win, cos, sin, *, T, name):
    R = h.shape[0]
    n_lat, n_tiles = T // TM, R // TM

    def body(h_ref, dho_ref, du_ref, dq_ref, dk_ref, dv_ref, mod_ref, g_ref, win_ref, cos_ref, sin_ref,
             dh_ref, dproj_ref, n_ref, part_ref):
        i = pl.program_id(0)
        first = jnp.logical_or(i == 0, i == n_lat)
        shift, scale = mod_ref[0, 3:4, :], mod_ref[0, 4:5, :]
        g = g_ref[1:2, :]
        r, xhat, y, n = _norm_mod(h_ref[...], g, shift, scale)
        n_ref[...] = n.astype(BF16)
        cs, sn = cos_ref[...], sin_ref[...]
        dproj_ref[:, :PW] = du_ref[...].astype(BF16)
        for s in range(AW // 128):
            x = dq_ref[:, 128 * s : 128 * (s + 1)]
            dproj_ref[:, PW + 128 * s : PW + 128 * (s + 1)] = (x * cs - _rot_half(x) * sn).astype(BF16)
        x = dk_ref[...]
        dproj_ref[:, PW + AW : PW + AW + KVW] = (x * cs - _rot_half(x) * sn).astype(BF16)
        dproj_ref[:, PW + AW + KVW :] = dv_ref[...].astype(BF16)
        dn = _dot(dproj_ref[...], win_ref[...])
        dh, dshift, dscale, dg = _norm_mod_bwd(dn, r, xhat, y, g, scale)
        dh_ref[...] = dho_ref[...] + dh
        _acc_partials(part_ref, first, {0: dshift, 1: dscale, 3: dg})

    return pl.pallas_call(
        body,
        name=name,
        grid=(n_tiles,),
        in_specs=[_rows(TM, D), _rows(TM, D), _rows(TM, PW), _rows(TM, AW), _rows(TM, KVW), _rows(TM, KVW),
                  _mod_spec(n_lat), _full((8, D)), _full((PROJ, D)), _rows(TM, 128), _rows(TM, 128)],
        out_specs=[_rows(TM, D), _rows(TM, PROJ), _rows(TM, D), _part_spec(n_lat)],
        out_shape=[_sds((R, D), F32), _sds((R, PROJ), BF16), _sds((R, D), BF16), _sds((2, 8, D), F32)],
        compiler_params=_params(),
    )(h, dho, du, dq, dk, dv, modv, gvec, win, cos, sin)


def _loss_head(h, target, g_final, *, T, name):
    R = h.shape[0]
    n_lat, n_tiles = T // TM, R // TM

    def body(h_ref, t_ref, g_ref, dh_ref, loss_ref, dg_ref):
        i = pl.program_id(0)

        @pl.when(i == 0)
        def _():
            loss_ref[...] = jnp.zeros_like(loss_ref)
            dg_ref[...] = jnp.zeros_like(dg_ref)

        @pl.when(i < n_lat)
        def _():
            h = h_ref[...]
            g = g_ref[...]
            r = lax.rsqrt(jnp.mean(h * h, axis=-1, keepdims=True) + EPS)
            xhat = h * r
            err = xhat * g - t_ref[...]
            tot = jnp.sum(jnp.sum(err * err, axis=1, keepdims=True), axis=0, keepdims=True)
            loss_ref[...] += jnp.broadcast_to(tot * (0.5 / D), loss_ref.shape)
            dy = err * (1.0 / D)
            dg_ref[0:1, :] += jnp.sum(dy * xhat, axis=0, keepdims=True)
            dxh = dy * g
            dh_ref[...] = r * (dxh - xhat * jnp.mean(dxh * xhat, axis=-1, keepdims=True))

        @pl.when(i >= n_lat)
        def _():
            dh_ref[...] = jnp.zeros_like(dh_ref)

    return pl.pallas_call(
        body,
        name=name,
        grid=(n_tiles,),
        in_specs=[_rows(TM, D), pl.BlockSpec((TM, D), lambda i: (jnp.minimum(i, n_lat - 1), 0)), _full((1, D))],
        out_specs=[_rows(TM, D), _full((8, 128)), _full((8, D))],
        out_shape=[_sds((R, D), F32), _sds((8, 128), F32), _sds((8, D), F32)],
        compiler_params=_params(),
    )(h, target, g_final)


def _mod_fwd(c16, w_mod, b_cols, *, name):
    nl, _, cols = w_mod.shape

    def body(c_ref, w_ref, b_ref, o_ref):
        c = c_ref[...]
        sc = (c * _sigmoid(c)).astype(BF16)
        o_ref[0] = _dot(sc, w_ref[0].astype(BF16)) + b_ref[0]

    return pl.pallas_call(
        body,
        name=name,
        grid=(nl,),
        in_specs=[_full((16, D)), pl.BlockSpec((1, D, cols), lambda l: (l, 0, 0)),
                  pl.BlockSpec((1, 1, cols), lambda l: (l, 0, 0))],
        out_specs=pl.BlockSpec((1, 16, cols), lambda l: (l, 0, 0)),
        out_shape=_sds((nl, 16, cols), F32),
        compiler_params=_params(),
    )(c16, w_mod, b_cols)


def _mod_bwd(c16, dm_cols, w_mod, *, name):
    nl, _, cols = w_mod.shape

    def body(c_ref, dm_ref, w_ref, gw_ref, dc_ref):
        c = c_ref[...]
        sc = (c * _sigmoid(c)).astype(BF16)
        dm = dm_ref[0].astype(BF16)
        gw_ref[0] = _dot_tn(sc, dm)
        dc_ref[0] = _dot_nt(dm, w_ref[0].astype(BF16))

    return pl.pallas_call(
        body,
        name=name,
        grid=(nl,),
        in_specs=[_full((16, D)), pl.BlockSpec((1, 16, cols), lambda l: (l, 0, 0)),
                  pl.BlockSpec((1, D, cols), lambda l: (l, 0, 0))],
        out_specs=[pl.BlockSpec((1, D, cols), lambda l: (l, 0, 0)), pl.BlockSpec((1, 16, D), lambda l: (l, 0, 0))],
        out_shape=[_sds((nl, D, cols), F32), _sds((nl, 16, D), F32)],
        compiler_params=_params(),
    )(c16, dm_cols, w_mod)


def _coords():
    return lax.axis_index("x"), lax.axis_index("y"), lax.axis_index("c")


def _peer(k, x, y, c):
    return (1 - x if k & 4 else x, 1 - y if k & 2 else y, 1 - c if k & 1 else c)


def _lin(p):
    return 4 * p[0] + 2 * p[1] + p[2]


def _view(ref, slot):
    return ref if slot is None else ref.at[slot]


def _exchange(name, ins, out_shapes, plan, local_plan):
    n_in, n_out, n = len(ins), len(out_shapes), len(plan)

    def body(*refs):
        in_refs, out_refs = refs[:n_in], refs[n_in : n_in + n_out]
        send_sems, recv_sems, loc_sems = refs[n_in + n_out :]
        x, y, c = _coords()
        me = _lin((x, y, c))

        def remote(idx, sender, receiver, peer):
            k, ii, sfn, oi, dfn = plan[idx]
            return pltpu.make_async_remote_copy(
                src_ref=_view(in_refs[ii], sfn(sender, receiver)), dst_ref=_view(out_refs[oi], dfn(sender, receiver)),
                send_sem=send_sems.at[idx], recv_sem=recv_sems.at[idx], device_id=peer, device_id_type=MESH)

        peers = [_peer(p[0], x, y, c) for p in plan]
        sends = [remote(idx, me, _lin(peers[idx]), peers[idx]) for idx in range(n)]
        for cp in sends:
            cp.start()
        locs = [pltpu.make_async_copy(_view(in_refs[ii], sfn(me)), _view(out_refs[oi], dfn(me)), loc_sems.at[idx])
                for idx, (ii, sfn, oi, dfn) in enumerate(local_plan)]
        for cp in locs:
            cp.start()
        for idx in range(n):
            remote(idx, _lin(peers[idx]), me, peers[idx]).wait_recv()
        for cp in sends:
            cp.wait_send()
        for cp in locs:
            cp.wait()

    return pl.pallas_call(
        body,
        name=name,
        in_specs=[ANY] * n_in,
        out_specs=[ANY] * n_out,
        out_shape=list(out_shapes),
        scratch_shapes=[pltpu.SemaphoreType.DMA((n,)), pltpu.SemaphoreType.DMA((n,)),
                        pltpu.SemaphoreType.DMA((max(len(local_plan), 1),))],
    )(*ins)


def _all_gather(name, arrays):
    outs = [_sds((NDEV,) + a.shape, a.dtype) for a in arrays]
    plan = [(k, i, lambda s, r: None, i, lambda s, r: s) for i in range(len(arrays)) for k in range(1, NDEV)]
    local = [(i, lambda m: None, i, lambda m: m) for i in range(len(arrays))]
    return _exchange(name, arrays, outs, plan, local)


def _all_gather_2level(name, arrays):
    na = len(arrays)

    def body(*refs):
        in_refs, out_refs = refs[:na], refs[na : 2 * na]
        send_sems, recv_sems, loc_sems = refs[2 * na :]
        x, y, c = _coords()
        me, sib = (x, y, c), (x, y, 1 - c)
        chips = [(1 - x, y), (x, 1 - y), (1 - x, 1 - y)]

        def copy(a, n, block, to, src=None):
            rows = out_refs[a].at[_lin(block)]
            return pltpu.make_async_remote_copy(
                src_ref=rows if src is None else src, dst_ref=rows, send_sem=send_sems.at[7 * a + n],
                recv_sem=recv_sems.at[7 * a + n], device_id=to, device_id_type=MESH)

        local, remote = [], []
        for a in range(na):
            mine = pltpu.make_async_copy(in_refs[a], out_refs[a].at[_lin(me)], loc_sems.at[a])
            mine.start()
            local.append(mine)
            first = [copy(a, 0, me, sib, src=in_refs[a])]
            first += [copy(a, 1 + j, me, (*chip, c), src=in_refs[a]) for j, chip in enumerate(chips)]
            for cp in first:
                cp.start()
            remote += first
        for a in range(na):
            for j, chip in enumerate(chips):
                copy(a, 1 + j, (*chip, c), me).wait_recv()
                passed = copy(a, 4 + j, (*chip, c), sib)
                passed.start()
                remote.append(passed)
        for a in range(na):
            copy(a, 0, sib, me).wait_recv()
            for j, chip in enumerate(chips):
                copy(a, 4 + j, (*chip, 1 - c), me).wait_recv()
        for cp in remote:
            cp.wait_send()
        for cp in local:
            cp.wait()

    return pl.pallas_call(
        body,
        name=name,
        in_specs=[ANY] * na,
        out_specs=[ANY] * na,
        out_shape=[_sds((NDEV,) + a.shape, a.dtype) for a in arrays],
        scratch_shapes=[pltpu.SemaphoreType.DMA((7 * na,)), pltpu.SemaphoreType.DMA((7 * na,)),
                        pltpu.SemaphoreType.DMA((na,))],
    )(*arrays)


def _reduce_scatter(tag, grads):
    na = len(grads)
    quarter = [_sds((4,) + g.shape[1:], g.dtype) for g in grads]
    plan = [(1, i, (lambda s, r, q=q: 2 * q + (r & 1)), i, (lambda s, r, q=q: q)) for i in range(na) for q in range(4)]
    local = [(i, (lambda m, q=q: 2 * q + (m & 1)), na + i, (lambda m, q=q: q)) for i in range(na) for q in range(4)]
    res = _exchange(f"rs1_{tag}", grads, quarter + quarter, plan, local)
    chip = [_add_bf16(res[i], res[na + i], name=f"rs_add_{tag}_{i}") for i in range(na)]
    plan = [(k, i, lambda s, r: r >> 1, i, lambda s, r: s >> 1) for i in range(na) for k in (2, 4, 6)]
    local = [(i, lambda m: m >> 1, i, lambda m: m >> 1) for i in range(na)]
    parts = _exchange(f"rs2_{tag}", chip, quarter, plan, local)
    return [_sum_slots(p, name=f"rs_sum_{tag}_{i}") for i, p in enumerate(parts)]


def _add_bf16(a, b, *, name):
    shape = a.shape
    a2, b2 = a.reshape(-1, shape[-1]), b.reshape(-1, shape[-1])
    rows = a2.shape[0]
    tr = rows // 4

    def body(a_ref, b_ref, o_ref):
        o_ref[...] = (a_ref[...].astype(F32) + b_ref[...].astype(F32)).astype(o_ref.dtype)

    out = pl.pallas_call(
        body, name=name, grid=(4,), in_specs=[_rows(tr, shape[-1])] * 2, out_specs=_rows(tr, shape[-1]),
        out_shape=_sds(a2.shape, a.dtype), compiler_params=_params())(a2, b2)
    return out.reshape(shape)


def _sum_slots(parts, *, name):
    ns, sh, w = parts.shape

    def body(p_ref, o_ref):
        acc = p_ref[0].astype(F32)
        for s in range(1, ns):
            acc = acc + p_ref[s].astype(F32)
        o_ref[...] = acc

    return pl.pallas_call(
        body, name=name, grid=(1,), in_specs=[_full((ns, sh, w))], out_specs=_full((sh, w)),
        out_shape=_sds((sh, w), F32), compiler_params=_params())(parts)


def _adamw_math(w, g, m, v):
    m2 = ADAM_B1 * m + (1.0 - ADAM_B1) * g
    v2 = ADAM_B2 * v + (1.0 - ADAM_B2) * (g * g)
    m_hat = m2 / (1.0 - ADAM_B1 ** ADAM_STEP)
    v_hat = v2 / (1.0 - ADAM_B2 ** ADAM_STEP)
    delta = -ADAM_LR * (m_hat / (jnp.sqrt(v_hat) + ADAM_EPS) + ADAM_WD * w)
    return delta, m2, v2


def _adamw(w, g, m, v, *, name):
    shape = w.shape
    flat = [t.reshape(-1, shape[-1]) for t in (w, g, m, v)]
    rows, cols = flat[0].shape
    tr = rows // 8 if rows % 64 == 0 else rows
    spec = _rows(tr, cols)

    def body(w_ref, g_ref, m_ref, v_ref, d_ref, m2_ref, v2_ref):
        d_ref[...], m2_ref[...], v2_ref[...] = _adamw_math(w_ref[...], g_ref[...], m_ref[...], v_ref[...])

    outs = pl.pallas_call(
        body, name=name, grid=(rows // tr,), in_specs=[spec] * 4, out_specs=[spec] * 3,
        out_shape=[_sds((rows, cols), F32)] * 3, compiler_params=_params())(*flat)
    return tuple(o.reshape(shape) for o in outs)


def _small_sums(packets, nf, dwp, dsc, dsk, *, name):
    flat = [p for layer in packets for p in layer]

    def total(ref, *idx):
        acc = ref[(0,) + idx]
        for dev in range(1, NDEV):
            acc = acc + ref[(dev,) + idx]
        return acc

    def body(*refs):
        pk = refs[:6]
        nf_ref, dwp0, dwp1, dsc0, dsc1, dsk0, dsk1 = refs[6:13]
        dm_ref, gb_ref, gn_ref, gnf_ref, gwp_ref, gps_ref, gsk_ref = refs[13:]
        dm_ref[...] = jnp.zeros_like(dm_ref)
        gn_ref[...] = jnp.zeros_like(gn_ref)
        for l in range(2):
            for sb in range(3):
                p = pk[3 * l + sb]
                for r in range(3):
                    col = slice((3 * sb + r) * D, (3 * sb + r + 1) * D)
                    lat = p[0, 0, r : r + 1, :]
                    dm_ref[l, 0:1, col] = lat
                    for dev in range(1, NDEV):
                        row = p[dev, 0, r : r + 1, :]
                        dm_ref[l, dev : dev + 1, col] = row
                        lat = lat + row
                    ctx = total(p, 1, slice(r, r + 1), slice(None))
                    dm_ref[l, 8:9, col] = ctx
                    gb_ref[l : l + 1, col] = lat + ctx
                gn_ref[l, sb : sb + 1, :] = total(p, 0, slice(3, 4), slice(None)) + total(p, 1, slice(3, 4), slice(None))
        gnf_ref[...] = total(nf_ref, slice(0, 1), slice(None))
        for l, (a, b, c) in enumerate(((dwp0, dsc0, dsk0), (dwp1, dsc1, dsk1))):
            gwp_ref[l] = total(a, slice(None), slice(None))
            gps_ref[l : l + 1, :] = total(b, slice(0, 1), slice(None))
            gsk_ref[l] = total(c, slice(None), slice(None))

    ins = flat + [nf, dwp[0], dwp[1], dsc[0], dsc[1], dsk[0], dsk[1]]
    return pl.pallas_call(
        body,
        name=name,
        out_shape=[_sds((2, 16, NMOD * D), F32), _sds((2, NMOD * D), F32), _sds((2, 8, D), F32), _sds((1, D), F32),
                   _sds((2, PW, 128), F32), _sds((2, PW), F32), _sds((2, 8, 128), F32)],
        compiler_params=pltpu.CompilerParams(vmem_limit_bytes=VMEM_LIMIT),
    )(*ins)


def _small_adamw(c_ctx, dc_all, triples, *, name):
    n = len(triples)

    def body(*refs):
        c_ref, dc_ref = refs[0], refs[1]
        ins = refs[2 : 2 + 4 * n - 1]
        outs = refs[2 + 4 * n - 1 :]
        acc = dc_ref[0, 0, 8:9, :] + dc_ref[0, 1, 8:9, :]
        for dev in range(1, NDEV):
            acc = acc + (dc_ref[dev, 0, 8:9, :] + dc_ref[dev, 1, 8:9, :])
        c = c_ref[...]
        sig = _sigmoid(c)
        g_c = acc * (sig * (1.0 + c * (1.0 - sig)))
        outs[0][...] = g_c
        pos = 0
        for k in range(n):
            if k == 0:
                w, g, m, v = ins[0][...], g_c, ins[1][...], ins[2][...]
                pos = 3
            else:
                w, g, m, v = (ins[pos + t][...] for t in range(4))
                pos += 4
            d, m2, v2 = _adamw_math(w, g, m, v)
            outs[1 + 3 * k][...], outs[2 + 3 * k][...], outs[3 + 3 * k][...] = d, m2, v2

    flat_in = [c_ctx, dc_all]
    out_shape = [_sds(c_ctx.shape, F32)]
    for k, (w, g, m, v) in enumerate(triples):
        flat_in += [w, m, v] if k == 0 else [w, g, m, v]
        out_shape += [_sds(w.shape, F32)] * 3
    return pl.pallas_call(body, name=name, out_shape=out_shape,
                          compiler_params=pltpu.CompilerParams(vmem_limit_bytes=VMEM_LIMIT))(*flat_in)


def _rope_tables(T, R):
    t = jnp.arange(T)
    inv = ROPE_BASE ** (-jnp.arange(0, HD // 2, 2, dtype=F32) / (HD // 2))
    ang = jnp.concatenate([(t // GRID_W).astype(F32)[:, None] * inv, (t % GRID_W).astype(F32)[:, None] * inv], axis=-1)
    cos = jnp.concatenate([jnp.tile(jnp.cos(ang), (1, 4)), jnp.ones((R - T, 128), F32)], axis=0)
    sin = jnp.concatenate([jnp.tile(jnp.sin(ang), (1, 4)), jnp.zeros((R - T, 128), F32)], axis=0)
    return cos, sin


def kernel(x, c, ctx, c_ctx, w_mod, b_mod, norm_ffn1, w_ffn1_in, w_ffn1_out, norm_mix, w_in, w_pool, pool_scale, sink, w_out, norm_ffn2, w_ffn2_in, w_ffn2_out, norm_final, loss_target, m_c_ctx, m_w_mod, m_b_mod, m_norm_ffn1, m_w_ffn1_in, m_w_ffn1_out, m_norm_mix, m_w_in, m_w_pool, m_pool_scale, m_sink, m_w_out, m_norm_ffn2, m_w_ffn2_in, m_w_ffn2_out, m_norm_final, v_c_ctx, v_w_mod, v_b_mod, v_norm_ffn1, v_w_ffn1_in, v_w_ffn1_out, v_norm_mix, v_w_in, v_w_pool, v_pool_scale, v_sink, v_w_out, v_norm_ffn2, v_w_ffn2_in, v_w_ffn2_out, v_norm_final):
    T = x.shape[1]
    R = T + LC
    nl = w_mod.shape[0]
    me = _lin(_coords())
    mcols = w_mod.shape[2]

    (c_all,) = _all_gather("ag_c", [c])
    c16 = jnp.concatenate([c_all.reshape(NDEV, D), c_ctx[None], jnp.zeros((16 - NDEV - 1, D), F32)], axis=0)
    b_cols = lax.dynamic_slice(b_mod, (0, me * mcols), (nl, mcols)).reshape(nl, 1, mcols)
    (mod_all,) = _all_gather("ag_mod", [_mod_fwd(c16, w_mod, b_cols, name="mod_fwd")])
    mod_all = jnp.transpose(mod_all, (1, 2, 0, 3)).reshape(nl, 16, NMOD, D)
    mine = lax.dynamic_index_in_dim(mod_all, me, axis=1, keepdims=False)
    pad = jnp.zeros((nl, 16 - NMOD, D), F32)
    modv = jnp.stack([jnp.concatenate([mine, pad], axis=1), jnp.concatenate([mod_all[:, 8], pad], axis=1)], axis=1)

    shards = []
    for l in range(nl):
        shards += [w_ffn1_in[l].T.astype(BF16), w_ffn1_out[l].astype(BF16), w_in[l].T.astype(BF16),
                   w_out[l].astype(BF16), w_ffn2_in[l].T.astype(BF16), w_ffn2_out[l].astype(BF16)]
    full = _all_gather_2level("ag_weights", shards)
    full = [f.reshape(-1, D) for f in full]
    gvec = [jnp.concatenate([norm_ffn1[l][None], norm_mix[l][None], norm_ffn2[l][None], jnp.zeros((5, D), F32)], axis=0)
            for l in range(nl)]
    cos, sin = _rope_tables(T, R)
    ps2 = [pool_scale[l][None] for l in range(nl)]

    h = jnp.concatenate([x[0], ctx[0]], axis=0)
    loss_part, dh, packets, dnf, dwp, dsc, dsk, big_parts = _forward_backward(
        h, loss_target[0], modv, gvec, full, cos, sin, sink, w_pool, ps2, norm_final, T=T)
    loss = lax.psum(loss_part[0, 0], ("x", "y", "c"))
    big = {l: _reduce_scatter(str(l), big_parts[l]) for l in reversed(range(nl))}
    grad_x = dh[:T][None]

    flat = [p for layer in packets for p in layer]
    small = _all_gather("ag_small", flat + [dnf, dwp[0], dwp[1], dsc[0], dsc[1], dsk[0], dsk[1]])
    pk_all = [small[0:3], small[3:6]]
    dm, g_b_mod, g_norms, g_nf, g_wp, g_ps, g_sk = _small_sums(
        pk_all, small[6], small[7:9], small[9:11], small[11:13], name="small_sums")
    dm_cols = lax.dynamic_slice(dm, (0, 0, me * mcols), (nl, 16, mcols))
    g_w_mod, dc_part = _mod_bwd(c16, dm_cols, w_mod, name="mod_bwd")
    (dc_all,) = _all_gather("ag_dc", [dc_part])

    grads = {
        "b_mod": g_b_mod, "norm_ffn1": g_norms[:, 0], "norm_mix": g_norms[:, 1], "norm_ffn2": g_norms[:, 2],
        "w_pool": g_wp.reshape(w_pool.shape), "pool_scale": g_ps, "sink": g_sk[:, :, 0], "norm_final": g_nf.reshape(D),
        "w_mod": g_w_mod,
        "w_ffn1_in": jnp.stack([big[l][0].T for l in range(nl)]), "w_ffn1_out": jnp.stack([big[l][1] for l in range(nl)]),
        "w_in": jnp.stack([big[l][2].T for l in range(nl)]), "w_out": jnp.stack([big[l][3] for l in range(nl)]),
        "w_ffn2_in": jnp.stack([big[l][4].T for l in range(nl)]), "w_ffn2_out": jnp.stack([big[l][5] for l in range(nl)]),
    }
    weights = dict(c_ctx=c_ctx, w_mod=w_mod, b_mod=b_mod, norm_ffn1=norm_ffn1, w_ffn1_in=w_ffn1_in, w_ffn1_out=w_ffn1_out,
                   norm_mix=norm_mix, w_in=w_in, w_pool=w_pool, pool_scale=pool_scale, sink=sink, w_out=w_out,
                   norm_ffn2=norm_ffn2, w_ffn2_in=w_ffn2_in, w_ffn2_out=w_ffn2_out, norm_final=norm_final)
    moms = dict(c_ctx=(m_c_ctx, v_c_ctx), w_mod=(m_w_mod, v_w_mod), b_mod=(m_b_mod, v_b_mod),
                norm_ffn1=(m_norm_ffn1, v_norm_ffn1), w_ffn1_in=(m_w_ffn1_in, v_w_ffn1_in),
                w_ffn1_out=(m_w_ffn1_out, v_w_ffn1_out), norm_mix=(m_norm_mix, v_norm_mix), w_in=(m_w_in, v_w_in),
                w_pool=(m_w_pool, v_w_pool), pool_scale=(m_pool_scale, v_pool_scale), sink=(m_sink, v_sink),
                w_out=(m_w_out, v_w_out), norm_ffn2=(m_norm_ffn2, v_norm_ffn2), w_ffn2_in=(m_w_ffn2_in, v_w_ffn2_in),
                w_ffn2_out=(m_w_ffn2_out, v_w_ffn2_out), norm_final=(m_norm_final, v_norm_final))
    order = list(weights)
    small_names = ["c_ctx", "b_mod", "norm_ffn1", "norm_mix", "w_pool", "pool_scale", "sink", "norm_ffn2", "norm_final"]

    def as2d(name, t):
        if name == "w_pool":
            return t.reshape(-1, 128)
        return t.reshape(1, -1) if t.ndim == 1 else t

    triples = [(as2d(n, weights[n]), None if n == "c_ctx" else as2d(n, grads[n]), as2d(n, moms[n][0]), as2d(n, moms[n][1]))
               for n in small_names]
    outs = _small_adamw(as2d("c_ctx", c_ctx), dc_all, triples, name="small_adamw")
    grads["c_ctx"] = outs[0].reshape(c_ctx.shape)
    delta, new_m, new_v = {}, {}, {}
    for k, n in enumerate(small_names):
        delta[n], new_m[n], new_v[n] = (o.reshape(weights[n].shape) for o in outs[1 + 3 * k : 4 + 3 * k])
    for n in order:
        if n not in small_names:
            delta[n], new_m[n], new_v[n] = _adamw(weights[n], grads[n], moms[n][0], moms[n][1], name=f"adamw_{n}")

    return (loss, grad_x, *[grads[n] for n in order], *[delta[n] for n in order],
            *[new_m[n] for n in order], *[new_v[n] for n in order])


def _forward_backward(h, target, modv, gvec, full, cos, sin, sink, w_pool, ps2, norm_final, *, T):
    nl = len(gvec)
    saved = []
    for l in range(nl):
        w1i, w1o, wi, wo, w2i, w2o = full[6 * l : 6 * l + 6]
        last = l == nl - 1
        h0 = h
        h1, a1, b1, f1 = _ffn_fwd(h0, modv[l], gvec[l], w1i, w1o, T=T, mrow=0, grow=0, ctx_active=True, name=f"ffn1_fwd_{l}")
        u, q, k4, v4 = _mixproj_fwd(h1, modv[l], gvec[l], wi, cos, sin, T=T, name=f"mixproj_fwd_{l}")
        cat = _attnpool_fwd(u, q, k4, v4, sink[l], w_pool[l], ps2[l], T=T, name=f"attnpool_fwd_{l}")
        h2, mo = _mixout_fwd(h1, cat, modv[l], wo, T=T, ctx_active=not last, name=f"mixout_fwd_{l}")
        h3, a2, b2, f2 = _ffn_fwd(h2, modv[l], gvec[l], w2i, w2o, T=T, mrow=6, grow=2, ctx_active=not last, name=f"ffn2_fwd_{l}")
        saved.append((h0, a1, b1, f1, h1, u, q, k4, v4, cat, mo, h2, a2, b2, f2))
        h = h3

    dh, loss_part, dnf = _loss_head(h, target, norm_final[None], T=T, name="loss_head")

    packets, dwp, dsc, dsk = [None] * nl, [None] * nl, [None] * nl, [None] * nl
    big = {}
    for l in reversed(range(nl)):
        w1i, w1o, wi, wo, w2i, w2o = full[6 * l : 6 * l + 6]
        last = l == nl - 1
        h0, a1, b1, f1, h1, u, q, k4, v4, cat, mo, h2, a2, b2, f2 = saved[l]
        dh, dab, s, n, df, pk2 = _ffn_bwd(h2, dh, a2, b2, f2, modv[l], gvec[l], w2i, w2o, T=T, mrow=6, grow=2,
                                          ctx_active=not last, name=f"ffn2_bwd_{l}")
        g_w2i = _wgrad(dab, n, bk=FCH, sh=2 * DFF // NDEV, name=f"wgrad_ffn2_in_{l}")
        g_w2o = _wgrad(s, df, bk=FCH, sh=DFF // NDEV, name=f"wgrad_ffn2_out_{l}")
        dcat, dmix, pko = _mixout_bwd(dh, mo, modv[l], wo, T=T, ctx_active=not last, name=f"mixout_bwd_{l}")
        g_wo = _wgrad(cat, dmix, bk=D, sh=D // NDEV, name=f"wgrad_out_{l}")
        dps, dwp[l], dsc[l] = _pool_bwd(u, dcat, w_pool[l], ps2[l], T=T, name=f"pool_bwd_{l}")
        du, dq, dk, dv, dsk[l] = _attn_bwd(q, k4, v4, dcat, dps, sink[l], T=T, name=f"attn_bwd_{l}")
        dh, dproj, n, pkm = _mixproj_bwd(h1, dh, du, dq, dk, dv, modv[l], gvec[l], wi, cos, sin, T=T, name=f"mixproj_bwd_{l}")
        g_wi = _wgrad(dproj, n, bk=PROJ, sh=PROJ // NDEV, name=f"wgrad_in_{l}")
        dh, dab, s, n, df, pk1 = _ffn_bwd(h0, dh, a1, b1, f1, modv[l], gvec[l], w1i, w1o, T=T, mrow=0, grow=0,
                                          ctx_active=True, name=f"ffn1_bwd_{l}")
        g_w1i = _wgrad(dab, n, bk=FCH, sh=2 * DFF // NDEV, name=f"wgrad_ffn1_in_{l}")
        g_w1o = _wgrad(s, df, bk=FCH, sh=DFF // NDEV, name=f"wgrad_ffn1_out_{l}")
        packets[l] = [pk1, pkm + pko, pk2]
        big[l] = [g_w1i, g_w1o, g_wi, g_wo, g_w2i, g_w2o]
    return loss_part, dh, packets, dnf, dwp, dsc, dsk, big
```

```python
import functools

import jax
import jax.numpy as jnp
from jax import lax
from jax.experimental import pallas as pl
from jax.experimental.pallas import tpu as pltpu

F32, BF16 = jnp.float32, jnp.bfloat16

D = 1024
LC = 256
DFF = 2816
NMOD = 9
PW = 512
AW = 512
KVW = 128
PROJ = PW + AW + 2 * KVW
HD = 64
BLK = 128
GRID_W = 64
POOL_WINDOWS = (2, 4, 8, 16)
EPS = 1e-6
NEG = -1e30
ROPE_BASE = 10000.0
NDEV = 8
MESH = pl.DeviceIdType.MESH

ADAM_LR, ADAM_B1, ADAM_B2, ADAM_EPS, ADAM_WD, ADAM_STEP = 0.001, 0.9, 0.999, 1e-08, 0.01, 10

VMEM_LIMIT = 56 * 1024 * 1024
TM = 256
FCH = 1408

ANY = pl.BlockSpec(memory_space=pl.ANY)
SMEM = pl.BlockSpec(memory_space=pltpu.SMEM)


def _params(ngrid=1):
    return pltpu.CompilerParams(dimension_semantics=("arbitrary",) * ngrid, vmem_limit_bytes=VMEM_LIMIT)


def _dot(a, b):
    return jnp.dot(a, b, preferred_element_type=F32)


def _dot_nt(a, b):
    return lax.dot_general(a, b, (((1,), (1,)), ((), ())), preferred_element_type=F32)


def _dot_tn(a, b):
    return lax.dot_general(a, b, (((0,), (0,)), ((), ())), preferred_element_type=F32)


def _sigmoid(x):
    return 1.0 / (1.0 + jnp.exp(-x))


def _rows(tm, w):
    return pl.BlockSpec((tm, w), lambda i: (i, 0))


def _full(shape):
    nd = len(shape)
    return pl.BlockSpec(shape, lambda *_: (0,) * nd)


def _sds(shape, dtype):
    return jax.ShapeDtypeStruct(shape, dtype)


def _norm_mod(h, g, shift, scale):
    r = lax.rsqrt(jnp.mean(h * h, axis=-1, keepdims=True) + EPS)
    xhat = h * r
    y = xhat * g
    return r, xhat, y, y * (1.0 + scale) + shift


def _norm_mod_bwd(dn, r, xhat, y, g, scale):
    dshift = jnp.sum(dn, axis=0, keepdims=True)
    dscale = jnp.sum(dn * y, axis=0, keepdims=True)
    dy = dn * (1.0 + scale)
    dg = jnp.sum(dy * xhat, axis=0, keepdims=True)
    dxh = dy * g
    dh = r * (dxh - xhat * jnp.mean(dxh * xhat, axis=-1, keepdims=True))
    return dh, dshift, dscale, dg


def _acc_partials(part_ref, first, rows):
    @pl.when(first)
    def _():
        part_ref[...] = jnp.zeros_like(part_ref)

    for r, val in rows.items():
        part_ref[0, r : r + 1, :] += val


def _mod_spec(n_lat):
    return pl.BlockSpec((1, 16, D), lambda i: (i // n_lat, 0, 0))


def _part_spec(n_lat):
    return pl.BlockSpec((1, 8, D), lambda i: (i // n_lat, 0, 0))


def _load_weights(pairs, sem):
    copies = [pltpu.make_async_copy(src, dst, sem.at[k]) for k, (src, dst) in enumerate(pairs)]
    for cp in copies:
        cp.start()
    for cp in copies:
        cp.wait()


def _ffn_fwd(h, modv, gvec, win, wout, *, T, mrow, grow, ctx_active, name):
    R = h.shape[0]
    n_lat, n_tiles = T // TM, R // TM

    def body(h_ref, mod_ref, g_ref, win_hbm, wout_hbm, ho_ref, a_ref, b_ref, f_ref, win_v, wout_v, sem):
        i = pl.program_id(0)

        @pl.when(i == 0)
        def _():
            _load_weights([(win_hbm, win_v), (wout_hbm, wout_v)], sem)

        def compute():
            h = h_ref[...]
            shift, scale, gate = (mod_ref[0, mrow + k : mrow + k + 1, :] for k in range(3))
            _, _, _, n = _norm_mod(h, g_ref[grow : grow + 1, :], shift, scale)
            n_bf = n.astype(BF16)
            acc = jnp.zeros((TM, D), F32)
            for c0 in range(0, DFF, FCH):
                a = _dot_nt(n_bf, win_v[c0 : c0 + FCH, :])
                b = _dot_nt(n_bf, win_v[DFF + c0 : DFF + c0 + FCH, :])
                a_ref[:, c0 : c0 + FCH] = a.astype(BF16)
                b_ref[:, c0 : c0 + FCH] = b.astype(BF16)
                s = a * _sigmoid(a) * b
                acc = acc + _dot(s.astype(BF16), wout_v[c0 : c0 + FCH, :])
            f_ref[...] = acc.astype(BF16)
            ho_ref[...] = h + (0.5 * gate) * acc

        if ctx_active:
            compute()
        else:
            pl.when(i < n_lat)(compute)

            @pl.when(i >= n_lat)
            def _():
                ho_ref[...] = h_ref[...]
                a_ref[...] = jnp.zeros_like(a_ref)
                b_ref[...] = jnp.zeros_like(b_ref)
                f_ref[...] = jnp.zeros_like(f_ref)

    return pl.pallas_call(
        body,
        name=name,
        grid=(n_tiles,),
        in_specs=[_rows(TM, D), _mod_spec(n_lat), _full((8, D)), ANY, ANY],
        out_specs=[_rows(TM, D), _rows(TM, DFF), _rows(TM, DFF), _rows(TM, D)],
        out_shape=[_sds((R, D), F32), _sds((R, DFF), BF16), _sds((R, DFF), BF16), _sds((R, D), BF16)],
        scratch_shapes=[pltpu.VMEM((2 * DFF, D), BF16), pltpu.VMEM((DFF, D), BF16), pltpu.SemaphoreType.DMA((2,))],
        compiler_params=_params(),
    )(h, modv, gvec, win, wout)


def _ffn_bwd(h, dho, a, b, f, modv, gvec, win, wout, *, T, mrow, grow, ctx_active, name):
    R = h.shape[0]
    n_lat, n_tiles = T // TM, R // TM

    def body(h_ref, dho_ref, a_ref, b_ref, f_ref, mod_ref, g_ref, win_hbm, wout_hbm,
             dh_ref, dab_ref, s_ref, n_ref, df_ref, part_ref, win_v, wout_v, sem):
        i = pl.program_id(0)

        @pl.when(i == 0)
        def _():
            _load_weights([(win_hbm, win_v), (wout_hbm, wout_v)], sem)

        first = jnp.logical_or(i == 0, i == n_lat)

        def compute():
            h = h_ref[...]
            dho = dho_ref[...]
            shift, scale, gate = (mod_ref[0, mrow + k : mrow + k + 1, :] for k in range(3))
            g = g_ref[grow : grow + 1, :]
            r, xhat, y, n = _norm_mod(h, g, shift, scale)
            dgate = 0.5 * jnp.sum(dho * f_ref[...].astype(F32), axis=0, keepdims=True)
            df_bf = ((0.5 * gate) * dho).astype(BF16)
            df_ref[...] = df_bf
            n_ref[...] = n.astype(BF16)
            dn = jnp.zeros((TM, D), F32)
            for c0 in range(0, DFF, FCH):
                ds = _dot_nt(df_bf, wout_v[c0 : c0 + FCH, :])
                av = a_ref[:, c0 : c0 + FCH].astype(F32)
                bv = b_ref[:, c0 : c0 + FCH].astype(F32)
                sig = _sigmoid(av)
                sa = av * sig
                s_ref[:, c0 : c0 + FCH] = (sa * bv).astype(BF16)
                da = (ds * bv * (sig * (1.0 + av * (1.0 - sig)))).astype(BF16)
                db = (ds * sa).astype(BF16)
                dab_ref[:, c0 : c0 + FCH] = da
                dab_ref[:, DFF + c0 : DFF + c0 + FCH] = db
                dn = dn + _dot(da, win_v[c0 : c0 + FCH, :]) + _dot(db, win_v[DFF + c0 : DFF + c0 + FCH, :])
            dh, dshift, dscale, dg = _norm_mod_bwd(dn, r, xhat, y, g, scale)
            dh_ref[...] = dho + dh
            _acc_partials(part_ref, first, {0: dshift, 1: dscale, 2: dgate, 3: dg})

        if ctx_active:
            compute()
        else:
            pl.when(i < n_lat)(compute)

            @pl.when(i >= n_lat)
            def _():
                dh_ref[...] = dho_ref[...]
                dab_ref[...] = jnp.zeros_like(dab_ref)
                s_ref[...] = jnp.zeros_like(s_ref)
                n_ref[...] = jnp.zeros_like(n_ref)
                df_ref[...] = jnp.zeros_like(df_ref)
                part_ref[...] = jnp.zeros_like(part_ref)

    return pl.pallas_call(
        body,
        name=name,
        grid=(n_tiles,),
        in_specs=[_rows(TM, D), _rows(TM, D), _rows(TM, DFF), _rows(TM, DFF), _rows(TM, D),
                  _mod_spec(n_lat), _full((8, D)), ANY, ANY],
        out_specs=[_rows(TM, D), _rows(TM, 2 * DFF), _rows(TM, DFF), _rows(TM, D), _rows(TM, D), _part_spec(n_lat)],
        out_shape=[_sds((R, D), F32), _sds((R, 2 * DFF), BF16), _sds((R, DFF), BF16), _sds((R, D), BF16),
                   _sds((R, D), BF16), _sds((2, 8, D), F32)],
        scratch_shapes=[pltpu.VMEM((2 * DFF, D), BF16), pltpu.VMEM((DFF, D), BF16), pltpu.SemaphoreType.DMA((2,))],
        compiler_params=_params(),
    )(h, dho, a, b, f, modv, gvec, win, wout)


def _wgrad(x, y, *, bk, sh, name):
    R, kx = x.shape
    n = y.shape[1]
    tr = R // 2
    nr, nsh = R // tr, bk // sh

    def body(x_ref, y_ref, o_ref, acc):
        r = pl.program_id(1)

        @pl.when(r == 0)
        def _():
            acc[...] = jnp.zeros_like(acc)

        acc[...] += _dot_tn(x_ref[...], y_ref[...])

        @pl.when(r == nr - 1)
        def _():
            for s in range(nsh):
                o_ref[s] = acc[s * sh : (s + 1) * sh, :].astype(BF16)

    return pl.pallas_call(
        body,
        name=name,
        grid=(kx // bk, nr),
        in_specs=[pl.BlockSpec((tr, bk), lambda k, r: (r, k)), pl.BlockSpec((tr, n), lambda k, r: (r, 0))],
        out_specs=pl.BlockSpec((nsh, sh, n), lambda k, r: (k, 0, 0)),
        out_shape=_sds((kx // sh, sh, n), BF16),
        scratch_shapes=[pltpu.VMEM((bk, n), F32)],
        compiler_params=_params(2),
    )(x, y)


def _rot_half(x):
    lane = lax.broadcasted_iota(jnp.int32, x.shape, 1)
    return jnp.where((lane & (HD - 1)) < HD // 2, -pltpu.roll(x, 128 - HD // 2, 1), pltpu.roll(x, HD // 2, 1))


def _tile_sel():
    i = lax.broadcasted_iota(jnp.int32, (KVW, AW), 0)
    j = lax.broadcasted_iota(jnp.int32, (KVW, AW), 1)
    return jnp.where(i == (j // 256) * HD + (j & (HD - 1)), 1.0, 0.0).astype(BF16)


def _mixproj_fwd(h, modv, gvec, win, cos, sin, *, T, name):
    R = h.shape[0]
    n_lat, n_tiles = T // TM, R // TM

    def body(h_ref, mod_ref, g_ref, win_ref, cos_ref, sin_ref, u_ref, q_ref, k4_ref, v4_ref):
        shift, scale = mod_ref[0, 3:4, :], mod_ref[0, 4:5, :]
        _, _, _, n = _norm_mod(h_ref[...], g_ref[1:2, :], shift, scale)
        proj = _dot_nt(n.astype(BF16), win_ref[...])
        u_ref[...] = proj[:, :PW]
        cs, sn = cos_ref[...], sin_ref[...]
        for s in range(AW // 128):
            x = proj[:, PW + 128 * s : PW + 128 * (s + 1)]
            q_ref[:, 128 * s : 128 * (s + 1)] = (x * cs + _rot_half(x) * sn).astype(BF16)
        k = proj[:, PW + AW : PW + AW + KVW]
        k = (k * cs + _rot_half(k) * sn).astype(BF16)
        v = proj[:, PW + AW + KVW :].astype(BF16)
        sel = _tile_sel()
        k4_ref[...] = _dot(k, sel).astype(BF16)
        v4_ref[...] = _dot(v, sel).astype(BF16)

    return pl.pallas_call(
        body,
        name=name,
        grid=(n_tiles,),
        in_specs=[_rows(TM, D), _mod_spec(n_lat), _full((8, D)), _full((PROJ, D)), _rows(TM, 128), _rows(TM, 128)],
        out_specs=[_rows(TM, PW), _rows(TM, AW), _rows(TM, AW), _rows(TM, AW)],
        out_shape=[_sds((R, PW), F32), _sds((R, AW), BF16), _sds((R, AW), BF16), _sds((R, AW), BF16)],
        compiler_params=_params(),
    )(h, modv, gvec, win, cos, sin)


def _win_start(j, hi):
    return pl.multiple_of(jnp.clip((j - 1) * BLK, 0, hi - 3 * BLK), BLK)


def _hi_lo(x):
    hi = x.astype(BF16)
    return hi, (x - hi.astype(F32)).astype(BF16)


def _pool_bounds(t, w, T, R):
    is_ctx = t >= T
    lo = jnp.maximum(t - w // 2, jnp.where(is_ctx, T, 0))
    hi = jnp.minimum(t + w // 2, jnp.where(is_ctx, R, T))
    return lo, hi


def _pooled(u_v, j, T, R):
    start = _win_start(j, R)
    u3_hi, u3_lo = _hi_lo(u_v[pl.ds(start, 3 * BLK), :])
    ub = u_v[pl.ds(pl.multiple_of(j * BLK, BLK), BLK), :]
    t = j * BLK + lax.broadcasted_iota(jnp.int32, (BLK, 1), 0)
    pos = start + lax.broadcasted_iota(jnp.int32, (1, 3 * BLK), 1)
    pooled, counts = [], []
    for g, w in enumerate(POOL_WINDOWS):
        lo, hi = _pool_bounds(t, w, T, R)
        band = jnp.where(pos >= lo, jnp.where(pos < hi, 1.0, 0.0), 0.0).astype(BF16)
        sl = slice(g * 128, (g + 1) * 128)
        sums = _dot(band, u3_hi[:, sl]) + _dot(band, u3_lo[:, sl])
        cnt = (hi - lo).astype(F32)
        pooled.append(sums / cnt - ub[:, sl])
        counts.append(cnt)
    return pooled, counts


def _stack_heads(x):
    lane_h = lax.broadcasted_iota(jnp.int32, x.shape, 1) // HD
    return jnp.concatenate([jnp.where(lane_h == h, x, jnp.zeros_like(x)) for h in range(4)], axis=0)


def _unstack_heads(x):
    lane_h = lax.broadcasted_iota(jnp.int32, (BLK, 256), 1) // HD
    out = jnp.zeros((BLK, 256), F32)
    for h in range(4):
        out = out + jnp.where(lane_h == h, x[h * BLK : (h + 1) * BLK, :], 0.0)
    return out


def _attn_probs(qs, kl, kc, sink_ref, g, j, start_l, nbl):
    s_l = _dot_nt(qs, kl) * (HD ** -0.5)
    s_c = _dot_nt(qs, kc) * (HD ** -0.5)
    rowi = lax.broadcasted_iota(jnp.int32, (4 * BLK, 1), 0)
    qpos = j * BLK + (rowi & (BLK - 1))
    kpos = start_l + lax.broadcasted_iota(jnp.int32, (1, 3 * BLK), 1)
    reach = jnp.where(j < nbl, BLK, -1)
    s_l = jnp.where(jnp.abs(kpos - qpos) <= reach, s_l, NEG)
    rb = rowi // BLK
    sk = jnp.where(rb == 0, sink_ref[4 * g], jnp.where(rb == 1, sink_ref[4 * g + 1],
                   jnp.where(rb == 2, sink_ref[4 * g + 2], sink_ref[4 * g + 3])))
    m = jnp.maximum(jnp.maximum(jnp.max(s_l, axis=1, keepdims=True), jnp.max(s_c, axis=1, keepdims=True)), sk)
    e_l, e_c, e_s = jnp.exp(s_l - m), jnp.exp(s_c - m), jnp.exp(sk - m)
    inv = 1.0 / (jnp.sum(e_l, axis=1, keepdims=True) + jnp.sum(e_c, axis=1, keepdims=True) + e_s)
    return e_l * inv, e_c * inv, e_s * inv


def _attnpool_fwd(u, q, k4, v4, sink, w_pool, pool_scale, *, T, name):
    R = u.shape[0]
    nb, nbl = R // BLK, T // BLK

    def body(q_ref, sink_ref, wp_ref, ps_ref, u_hbm, k4_hbm, v4_hbm, cat_ref, u_v, k4_v, v4_v, sem):
        j = pl.program_id(0)

        @pl.when(j == 0)
        def _():
            _load_weights([(u_hbm, u_v), (k4_hbm, k4_v), (v4_hbm, v4_v)], sem)

        pooled, _ = _pooled(u_v, j, T, R)
        for g in range(4):
            mixed = _dot(pooled[g].astype(BF16), wp_ref[g].astype(BF16)) * ps_ref[:, g * 128 : (g + 1) * 128]
            cat_ref[:, g * 128 : (g + 1) * 128] = mixed.astype(BF16)

        start_l = _win_start(j, T)
        for g in range(2):
            gl = slice(g * 256, (g + 1) * 256)
            qs = _stack_heads(q_ref[:, gl])
            p_l, p_c, _ = _attn_probs(qs, k4_v[pl.ds(start_l, 3 * BLK), gl], k4_v[T:R, gl], sink_ref, g, j, start_l, nbl)
            o = _dot(p_l.astype(BF16), v4_v[pl.ds(start_l, 3 * BLK), gl]) + _dot(p_c.astype(BF16), v4_v[T:R, gl])
            cat_ref[:, PW + g * 256 : PW + (g + 1) * 256] = _unstack_heads(o).astype(BF16)

    return pl.pallas_call(
        body,
        name=name,
        grid=(nb,),
        in_specs=[_rows(BLK, AW), SMEM, _full((4, 128, 128)), _full((1, PW)), ANY, ANY, ANY],
        out_specs=_rows(BLK, D),
        out_shape=_sds((R, D), BF16),
        scratch_shapes=[pltpu.VMEM((R, PW), F32), pltpu.VMEM((R, AW), BF16), pltpu.VMEM((R, AW), BF16),
                        pltpu.SemaphoreType.DMA((3,))],
        compiler_params=_params(),
    )(q, sink, w_pool, pool_scale, u, k4, v4)


def _mixout_fwd(h, cat, modv, wout, *, T, ctx_active, name):
    R = h.shape[0]
    n_lat, n_tiles = T // TM, R // TM

    def body(h_ref, cat_ref, mod_ref, w_ref, ho_ref, mo_ref):
        i = pl.program_id(0)

        def compute():
            mo = _dot(cat_ref[...], w_ref[...])
            mo_ref[...] = mo.astype(BF16)
            ho_ref[...] = h_ref[...] + mod_ref[0, 5:6, :] * mo

        if ctx_active:
            compute()
        else:
            pl.when(i < n_lat)(compute)

            @pl.when(i >= n_lat)
            def _():
                ho_ref[...] = h_ref[...]
                mo_ref[...] = jnp.zeros_like(mo_ref)

    return pl.pallas_call(
        body,
        name=name,
        grid=(n_tiles,),
        in_specs=[_rows(TM, D), _rows(TM, D), _mod_spec(n_lat), _full((D, D))],
        out_specs=[_rows(TM, D), _rows(TM, D)],
        out_shape=[_sds((R, D), F32), _sds((R, D), BF16)],
        compiler_params=_params(),
    )(h, cat, modv, wout)


def _mixout_bwd(dho, mo, modv, wout, *, T, ctx_active, name):
    R = dho.shape[0]
    n_lat, n_tiles = T // TM, R // TM

    def body(dho_ref, mo_ref, mod_ref, w_ref, dcat_ref, dmix_ref, part_ref):
        i = pl.program_id(0)
        first = jnp.logical_or(i == 0, i == n_lat)

        def compute():
            dho = dho_ref[...]
            dmix = (mod_ref[0, 5:6, :] * dho).astype(BF16)
            dmix_ref[...] = dmix
            dcat_ref[...] = _dot_nt(dmix, w_ref[...])
            dgate = jnp.sum(dho * mo_ref[...].astype(F32), axis=0, keepdims=True)
            _acc_partials(part_ref, first, {2: dgate})

        if ctx_active:
            compute()
        else:
            pl.when(i < n_lat)(compute)

            @pl.when(i >= n_lat)
            def _():
                dcat_ref[...] = jnp.zeros_like(dcat_ref)
                dmix_ref[...] = jnp.zeros_like(dmix_ref)
                part_ref[...] = jnp.zeros_like(part_ref)

    return pl.pallas_call(
        body,
        name=name,
        grid=(n_tiles,),
        in_specs=[_rows(TM, D), _rows(TM, D), _mod_spec(n_lat), _full((D, D))],
        out_specs=[_rows(TM, D), _rows(TM, D), _part_spec(n_lat)],
        out_shape=[_sds((R, D), F32), _sds((R, D), BF16), _sds((2, 8, D), F32)],
        compiler_params=_params(),
    )(dho, mo, modv, wout)


def _pool_bwd(u, dcat, w_pool, pool_scale, *, T, name):
    R = u.shape[0]
    nb = R // BLK

    def body(dcat_ref, wp_ref, ps_ref, u_hbm, dps_ref, dwp_ref, dsc_ref, u_v, sem):
        j = pl.program_id(0)

        @pl.when(j == 0)
        def _():
            _load_weights([(u_hbm, u_v)], sem)
            dwp_ref[...] = jnp.zeros_like(dwp_ref)
            dsc_ref[...] = jnp.zeros_like(dsc_ref)

        pooled, counts = _pooled(u_v, j, T, R)
        for g in range(4):
            sl = slice(g * 128, (g + 1) * 128)
            p_bf = pooled[g].astype(BF16)
            w_bf = wp_ref[g].astype(BF16)
            dmixed = dcat_ref[:, sl]
            dsc_ref[0:1, sl] += jnp.sum(dmixed * _dot(p_bf, w_bf), axis=0, keepdims=True)
            dmp = (dmixed * ps_ref[:, sl]).astype(BF16)
            dwp_ref[sl, :] += _dot_tn(p_bf, dmp)
            dps_ref[:, sl] = _dot_nt(dmp, w_bf) / counts[g]

    return pl.pallas_call(
        body,
        name=name,
        grid=(nb,),
        in_specs=[_rows(BLK, D), _full((4, 128, 128)), _full((1, PW)), ANY],
        out_specs=[_rows(BLK, PW), _full((PW, 128)), _full((8, PW))],
        out_shape=[_sds((R, PW), F32), _sds((PW, 128), F32), _sds((8, PW), F32)],
        scratch_shapes=[pltpu.VMEM((R, PW), F32), pltpu.SemaphoreType.DMA((1,))],
        compiler_params=_params(),
    )(dcat, w_pool, pool_scale, u)


def _fold_heads(x):
    y = x[:, :128] + x[:, 128:]
    return y + pltpu.roll(y, HD, 1)


def _attn_bwd(q, k4, v4, dcat, dps, sink, *, T, name):
    R = q.shape[0]
    nb, nbl = R // BLK, T // BLK

    def body(q_ref, dcat_ref, sink_ref, k4_hbm, v4_hbm, dps_hbm, du_ref, dq_ref, dk_ref, dv_ref, dsk_ref,
             k4_v, v4_v, dps_v, sem):
        j = pl.program_id(0)

        @pl.when(j == 0)
        def _():
            _load_weights([(k4_hbm, k4_v), (v4_hbm, v4_v), (dps_hbm, dps_v)], sem)
            dk_ref[...] = jnp.zeros_like(dk_ref)
            dv_ref[...] = jnp.zeros_like(dv_ref)
            dsk_ref[...] = jnp.zeros_like(dsk_ref)

        start = _win_start(j, R)
        d3_hi, d3_lo = _hi_lo(dps_v[pl.ds(start, 3 * BLK), :])
        db = dps_v[pl.ds(pl.multiple_of(j * BLK, BLK), BLK), :]
        pos = j * BLK + lax.broadcasted_iota(jnp.int32, (BLK, 1), 0)
        t_r = start + lax.broadcasted_iota(jnp.int32, (1, 3 * BLK), 1)
        for g, w in enumerate(POOL_WINDOWS):
            sl = slice(g * 128, (g + 1) * 128)
            lo_r, hi_r = _pool_bounds(t_r, w, T, R)
            band_t = jnp.where(pos >= lo_r, jnp.where(pos < hi_r, 1.0, 0.0), 0.0).astype(BF16)
            lo_c, hi_c = _pool_bounds(pos, w, T, R)
            du_ref[:, sl] = _dot(band_t, d3_hi[:, sl]) + _dot(band_t, d3_lo[:, sl]) - db[:, sl] * (hi_c - lo_c).astype(F32)

        start_l = _win_start(j, T)
        rb = lax.broadcasted_iota(jnp.int32, (4 * BLK, 1), 0) // BLK
        lane = lax.broadcasted_iota(jnp.int32, (1, 128), 1)
        dk_l, dk_c, dv_l, dv_c = [], [], [], []
        for g in range(2):
            gl = slice(g * 256, (g + 1) * 256)
            qs = _stack_heads(q_ref[:, gl])
            kl, kc = k4_v[pl.ds(start_l, 3 * BLK), gl], k4_v[T:R, gl]
            vl, vc = v4_v[pl.ds(start_l, 3 * BLK), gl], v4_v[T:R, gl]
            p_l, p_c, p_s = _attn_probs(qs, kl, kc, sink_ref, g, j, start_l, nbl)
            dos = _stack_heads(dcat_ref[:, PW + g * 256 : PW + (g + 1) * 256]).astype(BF16)
            dp_l, dp_c = _dot_nt(dos, vl), _dot_nt(dos, vc)
            delta = jnp.sum(p_l * dp_l, axis=1, keepdims=True) + jnp.sum(p_c * dp_c, axis=1, keepdims=True)
            ds_l = (p_l * (dp_l - delta) * (HD ** -0.5)).astype(BF16)
            ds_c = (p_c * (dp_c - delta) * (HD ** -0.5)).astype(BF16)
            dq_ref[:, gl] = _unstack_heads(_dot(ds_l, kl) + _dot(ds_c, kc))
            dk_l.append(_fold_heads(_dot_tn(ds_l, qs)))
            dk_c.append(_fold_heads(_dot_tn(ds_c, qs)))
            dv_l.append(_fold_heads(_dot_tn(p_l.astype(BF16), dos)))
            dv_c.append(_fold_heads(_dot_tn(p_c.astype(BF16), dos)))
            dsink = -p_s * delta
            for h in range(4):
                tot = jnp.sum(jnp.where(rb == h, dsink, 0.0), axis=0, keepdims=True)
                dsk_ref[4 * g + h : 4 * g + h + 1, :] += jnp.broadcast_to(tot, (1, 128))
        first = lane < HD
        dk_ref[pl.ds(start_l, 3 * BLK), :] += jnp.where(first, dk_l[0], dk_l[1])
        dk_ref[T:R, :] += jnp.where(first, dk_c[0], dk_c[1])
        dv_ref[pl.ds(start_l, 3 * BLK), :] += jnp.where(first, dv_l[0], dv_l[1])
        dv_ref[T:R, :] += jnp.where(first, dv_c[0], dv_c[1])

    return pl.pallas_call(
        body,
        name=name,
        grid=(nb,),
        in_specs=[_rows(BLK, AW), _rows(BLK, D), SMEM, ANY, ANY, ANY],
        out_specs=[_rows(BLK, PW), _rows(BLK, AW), _full((R, KVW)), _full((R, KVW)), _full((8, 128))],
        out_shape=[_sds((R, PW), F32), _sds((R, AW), F32), _sds((R, KVW), F32), _sds((R, KVW), F32),
                   _sds((8, 128), F32)],
        scratch_shapes=[pltpu.VMEM((R, AW), BF16), pltpu.VMEM((R, AW), BF16), pltpu.VMEM((R, PW), F32),
                        pltpu.SemaphoreType.DMA((3,))],
        compiler_params=_params(),
    )(q, dcat, sink, k4, v4, dps)


def _mixproj_bwd(h, dho, du, dq, dk, dv, modv, gvec, win, cos, sin, *, T, name):
    R = h.shape[0]
    n_lat, n_tiles = T // TM, R // TM

    def body(h_ref, dho_ref, du_ref, dq_ref, dk_ref, dv_ref, mod_ref, g_ref, win_ref, cos_ref, sin_ref,
             dh_ref, dproj_ref, n_ref, part_ref):
        i = pl.program_id(0)
        first = jnp.logical_or(i == 0, i == n_lat)
        shift, scale = mod_ref[0, 3:4, :], mod_ref[0, 4:5, :]
        g = g_ref[1:2, :]
        r, xhat, y, n = _norm_mod(h_ref[...], g, shift, scale)
        n_ref[...] = n.astype(BF16)
        cs, sn = cos_ref[...], sin_ref[...]
        dproj_ref[:, :PW] = du_ref[...].astype(BF16)
        for s in range(AW // 128):
            x = dq_ref[:, 128 * s : 128 * (s + 1)]
            dproj_ref[:, PW + 128 * s : PW + 128 * (s + 1)] = (x * cs - _rot_half(x) * sn).astype(BF16)
        x = dk_ref[...]
        dproj_ref[:, PW + AW : PW + AW + KVW] = (x * cs - _rot_half(x) * sn).astype(BF16)
        dproj_ref[:, PW + AW + KVW :] = dv_ref[...].astype(BF16)
        dn = _dot(dproj_ref[...], win_ref[...])
        dh, dshift, dscale, dg = _norm_mod_bwd(dn, r, xhat, y, g, scale)
        dh_ref[...] = dho_ref[...] + dh
        _acc_partials(part_ref, first, {0: dshift, 1: dscale, 3: dg})

    return pl.pallas_call(
        body,
        name=name,
        grid=(n_tiles,),
        in_specs=[_rows(TM, D), _rows(TM, D), _rows(TM, PW), _rows(TM, AW), _rows(TM, KVW), _rows(TM, KVW),
                  _mod_spec(n_lat), _full((8, D)), _full((PROJ, D)), _rows(TM, 128), _rows(TM, 128)],
        out_specs=[_rows(TM, D), _rows(TM, PROJ), _rows(TM, D), _part_spec(n_lat)],
        out_shape=[_sds((R, D), F32), _sds((R, PROJ), BF16), _sds((R, D), BF16), _sds((2, 8, D), F32)],
        compiler_params=_params(),
    )(h, dho, du, dq, dk, dv, modv, gvec, win, cos, sin)


def _loss_head(h, target, g_final, *, T, name):
    R = h.shape[0]
    n_lat, n_tiles = T // TM, R // TM

    def body(h_ref, t_ref, g_ref, dh_ref, loss_ref, dg_ref):
        i = pl.program_id(0)

        @pl.when(i == 0)
        def _():
            loss_ref[...] = jnp.zeros_like(loss_ref)
            dg_ref[...] = jnp.zeros_like(dg_ref)

        @pl.when(i < n_lat)
        def _():
            h = h_ref[...]
            g = g_ref[...]
            r = lax.rsqrt(jnp.mean(h * h, axis=-1, keepdims=True) + EPS)
            xhat = h * r
            err = xhat * g - t_ref[...]
            tot = jnp.sum(jnp.sum(err * err, axis=1, keepdims=True), axis=0, keepdims=True)
            loss_ref[...] += jnp.broadcast_to(tot * (0.5 / D), loss_ref.shape)
            dy = err * (1.0 / D)
            dg_ref[0:1, :] += jnp.sum(dy * xhat, axis=0, keepdims=True)
            dxh = dy * g
            dh_ref[...] = r * (dxh - xhat * jnp.mean(dxh * xhat, axis=-1, keepdims=True))

        @pl.when(i >= n_lat)
        def _():
            dh_ref[...] = jnp.zeros_like(dh_ref)

    return pl.pallas_call(
        body,
        name=name,
        grid=(n_tiles,),
        in_specs=[_rows(TM, D), pl.BlockSpec((TM, D), lambda i: (jnp.minimum(i, n_lat - 1), 0)), _full((1, D))],
        out_specs=[_rows(TM, D), _full((8, 128)), _full((8, D))],
        out_shape=[_sds((R, D), F32), _sds((8, 128), F32), _sds((8, D), F32)],
        compiler_params=_params(),
    )(h, target, g_final)


def _mod_fwd(c16, w_mod, b_cols, *, name):
    nl, _, cols = w_mod.shape

    def body(c_ref, w_ref, b_ref, o_ref):
        c = c_ref[...]
        sc = (c * _sigmoid(c)).astype(BF16)
        o_ref[0] = _dot(sc, w_ref[0].astype(BF16)) + b_ref[0]

    return pl.pallas_call(
        body,
        name=name,
        grid=(nl,),
        in_specs=[_full((16, D)), pl.BlockSpec((1, D, cols), lambda l: (l, 0, 0)),
                  pl.BlockSpec((1, 1, cols), lambda l: (l, 0, 0))],
        out_specs=pl.BlockSpec((1, 16, cols), lambda l: (l, 0, 0)),
        out_shape=_sds((nl, 16, cols), F32),
        compiler_params=_params(),
    )(c16, w_mod, b_cols)


def _mod_bwd(c16, dm_cols, w_mod, *, name):
    nl, _, cols = w_mod.shape

    def body(c_ref, dm_ref, w_ref, gw_ref, dc_ref):
        c = c_ref[...]
        sc = (c * _sigmoid(c)).astype(BF16)
        dm = dm_ref[0].astype(BF16)
        gw_ref[0] = _dot_tn(sc, dm)
        dc_ref[0] = _dot_nt(dm, w_ref[0].astype(BF16))

    return pl.pallas_call(
        body,
        name=name,
        grid=(nl,),
        in_specs=[_full((16, D)), pl.BlockSpec((1, 16, cols), lambda l: (l, 0, 0)),
                  pl.BlockSpec((1, D, cols), lambda l: (l, 0, 0))],
        out_specs=[pl.BlockSpec((1, D, cols), lambda l: (l, 0, 0)), pl.BlockSpec((1, 16, D), lambda l: (l, 0, 0))],
        out_shape=[_sds((nl, D, cols), F32), _sds((nl, 16, D), F32)],
        compiler_params=_params(),
    )(c16, dm_cols, w_mod)


def _coords():
    return lax.axis_index("x"), lax.axis_index("y"), lax.axis_index("c")


def _peer(k, x, y, c):
    return (1 - x if k & 4 else x, 1 - y if k & 2 else y, 1 - c if k & 1 else c)


def _lin(p):
    return 4 * p[0] + 2 * p[1] + p[2]


def _view(ref, slot):
    return ref if slot is None else ref.at[slot]


def _exchange(name, ins, out_shapes, plan, local_plan=(), inplace=False):
    n_in, n_out, n = len(ins), len(out_shapes), len(plan)

    def body(*refs):
        in_refs, out_refs = refs[:n_in], refs[n_in : n_in + n_out]
        if inplace:
            in_refs = out_refs
        send_sems, recv_sems, loc_sems = refs[n_in + n_out :]
        x, y, c = _coords()
        me = _lin((x, y, c))

        def remote(idx, sender, receiver, peer):
            k, ii, sfn, oi, dfn = plan[idx]
            return pltpu.make_async_remote_copy(
                src_ref=_view(in_refs[ii], sfn(sender, receiver)), dst_ref=_view(out_refs[oi], dfn(sender, receiver)),
                send_sem=send_sems.at[idx], recv_sem=recv_sems.at[idx], device_id=peer, device_id_type=MESH)

        peers = [_peer(p[0], x, y, c) for p in plan]
        sends = [remote(idx, me, _lin(peers[idx]), peers[idx]) for idx in range(n)]
        for cp in sends:
            cp.start()
        locs = [pltpu.make_async_copy(_view(in_refs[ii], sfn(me)), _view(out_refs[oi], dfn(me)), loc_sems.at[idx])
                for idx, (ii, sfn, oi, dfn) in enumerate(local_plan)]
        for cp in locs:
            cp.start()
        for idx in range(n):
            remote(idx, _lin(peers[idx]), me, peers[idx]).wait_recv()
        for cp in sends:
            cp.wait_send()
        for cp in locs:
            cp.wait()

    return pl.pallas_call(
        body,
        name=name,
        in_specs=[ANY] * n_in,
        out_specs=[ANY] * n_out,
        out_shape=list(out_shapes),
        scratch_shapes=[pltpu.SemaphoreType.DMA((n,)), pltpu.SemaphoreType.DMA((n,)),
                        pltpu.SemaphoreType.DMA((max(len(local_plan), 1),))],
        input_output_aliases={i: i for i in range(n_in)} if inplace else {},
    )(*ins)


def _all_gather(name, arrays, two_level=True):
    na = len(arrays)
    outs = [_sds((NDEV,) + a.shape, a.dtype) for a in arrays]
    local = [(i, lambda m: None, i, lambda m: m) for i in range(na)]
    if not two_level:
        plan = [(k, i, lambda s, r: None, i, lambda s, r: s) for i in range(na) for k in range(1, NDEV)]
        return _exchange(name, arrays, outs, plan, local)
    plan = [(k, i, lambda s, r: None, i, lambda s, r: s) for i in range(na) for k in (1, 2, 4, 6)]
    got = _exchange(name + "_a", arrays, outs, plan, local)
    plan = [(1, i, (lambda s, r, k=k: s ^ k), i, (lambda s, r, k=k: s ^ k)) for i in range(na) for k in (2, 4, 6)]
    return _exchange(name + "_b", got, outs, plan, inplace=True)


def _reduce_scatter(tag, grads, pos):
    na = len(grads)
    plan = [(1, i, (lambda s, r, q=q: 2 * q + (r & 1)), i, (lambda s, r, q=q: q)) for i in range(na) for q in range(4)]
    got = _exchange(f"rs1_{tag}", grads, [_sds((4,) + g.shape[1:], g.dtype) for g in grads], plan)
    chip = [_add_pairs(grads[i], got[i], pos, name=f"rs_add_{tag}_{i}") for i in range(na)]
    plan = [(k, i, lambda s, r: r >> 1, i, (lambda s, r, j=j: j)) for i in range(na) for j, k in enumerate((2, 4, 6))]
    got = _exchange(f"rs2_{tag}", chip, [_sds((3,) + g.shape[1:], g.dtype) for g in grads], plan)
    return [_sum_chips(chip[i], got[i], pos, name=f"rs_sum_{tag}_{i}") for i in range(na)]


def _add_pairs(g, got, pos, *, name):
    _, sh, w = g.shape

    def body(pos_ref, g_ref, r_ref, o_ref):
        o_ref[...] = (g_ref[...].astype(F32) + r_ref[...].astype(F32)).astype(o_ref.dtype)

    return pl.pallas_call(
        body,
        name=name,
        grid_spec=pltpu.PrefetchScalarGridSpec(
            num_scalar_prefetch=1, grid=(4,),
            in_specs=[pl.BlockSpec((1, sh, w), lambda q, p: (2 * q + p[0], 0, 0)),
                      pl.BlockSpec((1, sh, w), lambda q, p: (q, 0, 0))],
            out_specs=pl.BlockSpec((1, sh, w), lambda q, p: (q, 0, 0))),
        out_shape=_sds((4, sh, w), g.dtype),
        compiler_params=_params(),
    )(pos, g, got)


def _sum_chips(chip, got, pos, *, name):
    _, sh, w = chip.shape

    def body(pos_ref, c_ref, r_ref, o_ref):
        acc = c_ref[0].astype(F32)
        for s in range(3):
            acc = acc + r_ref[s].astype(F32)
        o_ref[...] = acc

    return pl.pallas_call(
        body,
        name=name,
        grid_spec=pltpu.PrefetchScalarGridSpec(
            num_scalar_prefetch=1, grid=(1,),
            in_specs=[pl.BlockSpec((1, sh, w), lambda i, p: (p[1], 0, 0)), pl.BlockSpec((3, sh, w), lambda i, p: (0, 0, 0))],
            out_specs=pl.BlockSpec((sh, w), lambda i, p: (0, 0))),
        out_shape=_sds((sh, w), F32),
        compiler_params=_params(),
    )(pos, chip, got)


def _adamw_math(w, g, m, v):
    m2 = ADAM_B1 * m + (1.0 - ADAM_B1) * g
    v2 = ADAM_B2 * v + (1.0 - ADAM_B2) * (g * g)
    m_hat = m2 / (1.0 - ADAM_B1 ** ADAM_STEP)
    v_hat = v2 / (1.0 - ADAM_B2 ** ADAM_STEP)
    delta = -ADAM_LR * (m_hat / (jnp.sqrt(v_hat) + ADAM_EPS) + ADAM_WD * w)
    return delta, m2, v2


def _adamw(w, g, m, v, *, name):
    shape = w.shape
    flat = [t.reshape(-1, shape[-1]) for t in (w, g, m, v)]
    rows, cols = flat[0].shape
    tr = rows // 8 if rows % 64 == 0 else rows
    spec = _rows(tr, cols)

    def body(w_ref, g_ref, m_ref, v_ref, d_ref, m2_ref, v2_ref):
        d_ref[...], m2_ref[...], v2_ref[...] = _adamw_math(w_ref[...], g_ref[...], m_ref[...], v_ref[...])

    outs = pl.pallas_call(
        body, name=name, grid=(rows // tr,), in_specs=[spec] * 4, out_specs=[spec] * 3,
        out_shape=[_sds((rows, cols), F32)] * 3, compiler_params=_params())(*flat)
    return tuple(o.reshape(shape) for o in outs)


def _small_sums(packets, nf, dwp, dsc, dsk, *, name):
    flat = [p for layer in packets for p in layer]

    def total(ref, *idx):
        acc = ref[(0,) + idx]
        for dev in range(1, NDEV):
            acc = acc + ref[(dev,) + idx]
        return acc

    def body(*refs):
        pk = refs[:6]
        nf_ref, dwp0, dwp1, dsc0, dsc1, dsk0, dsk1 = refs[6:13]
        dm_ref, gb_ref, gn_ref, gnf_ref, gwp_ref, gps_ref, gsk_ref = refs[13:]
        dm_ref[...] = jnp.zeros_like(dm_ref)
        gn_ref[...] = jnp.zeros_like(gn_ref)
        for l in range(2):
            for sb in range(3):
                p = pk[3 * l + sb]
                for r in range(3):
                    col = slice((3 * sb + r) * D, (3 * sb + r + 1) * D)
                    lat = p[0, 0, r : r + 1, :]
                    dm_ref[l, 0:1, col] = lat
                    for dev in range(1, NDEV):
                        row = p[dev, 0, r : r + 1, :]
                        dm_ref[l, dev : dev + 1, col] = row
                        lat = lat + row
                    ctx = total(p, 1, slice(r, r + 1), slice(None))
                    dm_ref[l, 8:9, col] = ctx
                    gb_ref[l : l + 1, col] = lat + ctx
                gn_ref[l, sb : sb + 1, :] = total(p, 0, slice(3, 4), slice(None)) + total(p, 1, slice(3, 4), slice(None))
        gnf_ref[...] = total(nf_ref, slice(0, 1), slice(None))
        for l, (a, b, c) in enumerate(((dwp0, dsc0, dsk0), (dwp1, dsc1, dsk1))):
            gwp_ref[l] = total(a, slice(None), slice(None))
            gps_ref[l : l + 1, :] = total(b, slice(0, 1), slice(None))
            gsk_ref[l] = total(c, slice(None), slice(None))

    ins = flat + [nf, dwp[0], dwp[1], dsc[0], dsc[1], dsk[0], dsk[1]]
    return pl.pallas_call(
        body,
        name=name,
        out_shape=[_sds((2, 16, NMOD * D), F32), _sds((2, NMOD * D), F32), _sds((2, 8, D), F32), _sds((1, D), F32),
                   _sds((2, PW, 128), F32), _sds((2, PW), F32), _sds((2, 8, 128), F32)],
        compiler_params=pltpu.CompilerParams(vmem_limit_bytes=VMEM_LIMIT),
    )(*ins)


def _small_adamw(c_ctx, dc_all, triples, *, name):
    n = len(triples)

    def body(*refs):
        c_ref, dc_ref = refs[0], refs[1]
        ins = refs[2 : 2 + 4 * n - 1]
        outs = refs[2 + 4 * n - 1 :]
        acc = dc_ref[0, 0, 8:9, :] + dc_ref[0, 1, 8:9, :]
        for dev in range(1, NDEV):
            acc = acc + (dc_ref[dev, 0, 8:9, :] + dc_ref[dev, 1, 8:9, :])
        c = c_ref[...]
        sig = _sigmoid(c)
        g_c = acc * (sig * (1.0 + c * (1.0 - sig)))
        outs[0][...] = g_c
        pos = 0
        for k in range(n):
            if k == 0:
                w, g, m, v = ins[0][...], g_c, ins[1][...], ins[2][...]
                pos = 3
            else:
                w, g, m, v = (ins[pos + t][...] for t in range(4))
                pos += 4
            d, m2, v2 = _adamw_math(w, g, m, v)
            outs[1 + 3 * k][...], outs[2 + 3 * k][...], outs[3 + 3 * k][...] = d, m2, v2

    flat_in = [c_ctx, dc_all]
    out_shape = [_sds(c_ctx.shape, F32)]
    for k, (w, g, m, v) in enumerate(triples):
        flat_in += [w, m, v] if k == 0 else [w, g, m, v]
        out_shape += [_sds(w.shape, F32)] * 3
    return pl.pallas_call(body, name=name, out_shape=out_shape,
                          compiler_params=pltpu.CompilerParams(vmem_limit_bytes=VMEM_LIMIT))(*flat_in)


def _rope_tables(T, R):
    t = jnp.arange(T)
    inv = ROPE_BASE ** (-jnp.arange(0, HD // 2, 2, dtype=F32) / (HD // 2))
    ang = jnp.concatenate([(t // GRID_W).astype(F32)[:, None] * inv, (t % GRID_W).astype(F32)[:, None] * inv], axis=-1)
    cos = jnp.concatenate([jnp.tile(jnp.cos(ang), (1, 4)), jnp.ones((R - T, 128), F32)], axis=0)
    sin = jnp.concatenate([jnp.tile(jnp.sin(ang), (1, 4)), jnp.zeros((R - T, 128), F32)], axis=0)
    return cos, sin


def kernel(x, c, ctx, c_ctx, w_mod, b_mod, norm_ffn1, w_ffn1_in, w_ffn1_out, norm_mix, w_in, w_pool, pool_scale, sink, w_out, norm_ffn2, w_ffn2_in, w_ffn2_out, norm_final, loss_target, m_c_ctx, m_w_mod, m_b_mod, m_norm_ffn1, m_w_ffn1_in, m_w_ffn1_out, m_norm_mix, m_w_in, m_w_pool, m_pool_scale, m_sink, m_w_out, m_norm_ffn2, m_w_ffn2_in, m_w_ffn2_out, m_norm_final, v_c_ctx, v_w_mod, v_b_mod, v_norm_ffn1, v_w_ffn1_in, v_w_ffn1_out, v_norm_mix, v_w_in, v_w_pool, v_pool_scale, v_sink, v_w_out, v_norm_ffn2, v_w_ffn2_in, v_w_ffn2_out, v_norm_final):
    T = x.shape[1]
    R = T + LC
    nl = w_mod.shape[0]
    cx, cy, cc = _coords()
    me = _lin((cx, cy, cc))
    pos = jnp.stack([cc, 2 * cx + cy]).astype(jnp.int32)
    mcols = w_mod.shape[2]

    (c_all,) = _all_gather("ag_c", [c], two_level=False)
    c16 = jnp.concatenate([c_all.reshape(NDEV, D), c_ctx[None], jnp.zeros((16 - NDEV - 1, D), F32)], axis=0)
    b_cols = lax.dynamic_slice(b_mod, (0, me * mcols), (nl, mcols)).reshape(nl, 1, mcols)
    (mod_all,) = _all_gather("ag_mod", [_mod_fwd(c16, w_mod, b_cols, name="mod_fwd")], two_level=False)
    mod_all = jnp.transpose(mod_all, (1, 2, 0, 3)).reshape(nl, 16, NMOD, D)
    mine = lax.dynamic_index_in_dim(mod_all, me, axis=1, keepdims=False)
    pad = jnp.zeros((nl, 16 - NMOD, D), F32)
    modv = jnp.stack([jnp.concatenate([mine, pad], axis=1), jnp.concatenate([mod_all[:, 8], pad], axis=1)], axis=1)

    shards = []
    for l in range(nl):
        shards += [w_ffn1_in[l].T.astype(BF16), w_ffn1_out[l].astype(BF16), w_in[l].T.astype(BF16),
                   w_out[l].astype(BF16), w_ffn2_in[l].T.astype(BF16), w_ffn2_out[l].astype(BF16)]
    full = _all_gather("ag_weights", shards)
    full = [f.reshape(-1, D) for f in full]
    gvec = [jnp.concatenate([norm_ffn1[l][None], norm_mix[l][None], norm_ffn2[l][None], jnp.zeros((5, D), F32)], axis=0)
            for l in range(nl)]
    cos, sin = _rope_tables(T, R)
    ps2 = [pool_scale[l][None] for l in range(nl)]

    h = jnp.concatenate([x[0], ctx[0]], axis=0)
    loss_part, dh, packets, dnf, dwp, dsc, dsk, big_parts = _forward_backward(
        h, loss_target[0], modv, gvec, full, cos, sin, sink, w_pool, ps2, norm_final, T=T)
    loss = lax.psum(loss_part[0, 0], ("x", "y", "c"))
    big = {l: _reduce_scatter(str(l), big_parts[l], pos) for l in reversed(range(nl))}
    grad_x = dh[:T][None]

    flat = [p for layer in packets for p in layer]
    small = _all_gather("ag_small", flat + [dnf, dwp[0], dwp[1], dsc[0], dsc[1], dsk[0], dsk[1]])
    pk_all = [small[0:3], small[3:6]]
    dm, g_b_mod, g_norms, g_nf, g_wp, g_ps, g_sk = _small_sums(
        pk_all, small[6], small[7:9], small[9:11], small[11:13], name="small_sums")
    dm_cols = lax.dynamic_slice(dm, (0, 0, me * mcols), (nl, 16, mcols))
    g_w_mod, dc_part = _mod_bwd(c16, dm_cols, w_mod, name="mod_bwd")
    (dc_all,) = _all_gather("ag_dc", [dc_part], two_level=False)

    grads = {
        "b_mod": g_b_mod, "norm_ffn1": g_norms[:, 0], "norm_mix": g_norms[:, 1], "norm_ffn2": g_norms[:, 2],
        "w_pool": g_wp.reshape(w_pool.shape), "pool_scale": g_ps, "sink": g_sk[:, :, 0], "norm_final": g_nf.reshape(D),
        "w_mod": g_w_mod,
        "w_ffn1_in": jnp.stack([big[l][0].T for l in range(nl)]), "w_ffn1_out": jnp.stack([big[l][1] for l in range(nl)]),
        "w_in": jnp.stack([big[l][2].T for l in range(nl)]), "w_out": jnp.stack([big[l][3] for l in range(nl)]),
        "w_ffn2_in": jnp.stack([big[l][4].T for l in range(nl)]), "w_ffn2_out": jnp.stack([big[l][5] for l in range(nl)]),
    }
    weights = dict(c_ctx=c_ctx, w_mod=w_mod, b_mod=b_mod, norm_ffn1=norm_ffn1, w_ffn1_in=w_ffn1_in, w_ffn1_out=w_ffn1_out,
                   norm_mix=norm_mix, w_in=w_in, w_pool=w_pool, pool_scale=pool_scale, sink=sink, w_out=w_out,
                   norm_ffn2=norm_ffn2, w_ffn2_in=w_ffn2_in, w_ffn2_out=w_ffn2_out, norm_final=norm_final)
    moms = dict(c_ctx=(m_c_ctx, v_c_ctx), w_mod=(m_w_mod, v_w_mod), b_mod=(m_b_mod, v_b_mod),
                norm_ffn1=(m_norm_ffn1, v_norm_ffn1), w_ffn1_in=(m_w_ffn1_in, v_w_ffn1_in),
                w_ffn1_out=(m_w_ffn1_out, v_w_ffn1_out), norm_mix=(m_norm_mix, v_norm_mix), w_in=(m_w_in, v_w_in),
                w_pool=(m_w_pool, v_w_pool), pool_scale=(m_pool_scale, v_pool_scale), sink=(m_sink, v_sink),
                w_out=(m_w_out, v_w_out), norm_ffn2=(m_norm_ffn2, v_norm_ffn2), w_ffn2_in=(m_w_ffn2_in, v_w_ffn2_in),
                w_ffn2_out=(m_w_ffn2_out, v_w_ffn2_out), norm_final=(m_norm_final, v_norm_final))
    order = list(weights)
    small_names = ["c_ctx", "b_mod", "norm_ffn1", "norm_mix", "w_pool", "pool_scale", "sink", "norm_ffn2", "norm_final"]

    def as2d(name, t):
        if name == "w_pool":
            return t.reshape(-1, 128)
        return t.reshape(1, -1) if t.ndim == 1 else t

    triples = [(as2d(n, weights[n]), None if n == "c_ctx" else as2d(n, grads[n]), as2d(n, moms[n][0]), as2d(n, moms[n][1]))
               for n in small_names]
    outs = _small_adamw(as2d("c_ctx", c_ctx), dc_all, triples, name="small_adamw")
    grads["c_ctx"] = outs[0].reshape(c_ctx.shape)
    delta, new_m, new_v = {}, {}, {}
    for k, n in enumerate(small_names):
        delta[n], new_m[n], new_v[n] = (o.reshape(weights[n].shape) for o in outs[1 + 3 * k : 4 + 3 * k])
    for n in order:
        if n not in small_names:
            delta[n], new_m[n], new_v[n] = _adamw(weights[n], grads[n], moms[n][0], moms[n][1], name=f"adamw_{n}")

    return (loss, grad_x, *[grads[n] for n in order], *[delta[n] for n in order],
            *[new_m[n] for n in order], *[new_v[n] for n in order])


def _forward_backward(h, target, modv, gvec, full, cos, sin, sink, w_pool, ps2, norm_final, *, T):
    nl = len(gvec)
    saved = []
    for l in range(nl):
        w1i, w1o, wi, wo, w2i, w2o = full[6 * l : 6 * l + 6]
        last = l == nl - 1
        h0 = h
        h1, a1, b1, f1 = _ffn_fwd(h0, modv[l], gvec[l], w1i, w1o, T=T, mrow=0, grow=0, ctx_active=True, name=f"ffn1_fwd_{l}")
        u, q, k4, v4 = _mixproj_fwd(h1, modv[l], gvec[l], wi, cos, sin, T=T, name=f"mixproj_fwd_{l}")
        cat = _attnpool_fwd(u, q, k4, v4, sink[l], w_pool[l], ps2[l], T=T, name=f"attnpool_fwd_{l}")
        h2, mo = _mixout_fwd(h1, cat, modv[l], wo, T=T, ctx_active=not last, name=f"mixout_fwd_{l}")
        h3, a2, b2, f2 = _ffn_fwd(h2, modv[l], gvec[l], w2i, w2o, T=T, mrow=6, grow=2, ctx_active=not last, name=f"ffn2_fwd_{l}")
        saved.append((h0, a1, b1, f1, h1, u, q, k4, v4, cat, mo, h2, a2, b2, f2))
        h = h3

    dh, loss_part, dnf = _loss_head(h, target, norm_final[None], T=T, name="loss_head")

    packets, dwp, dsc, dsk = [None] * nl, [None] * nl, [None] * nl, [None] * nl
    big = {}
    for l in reversed(range(nl)):
        w1i, w1o, wi, wo, w2i, w2o = full[6 * l : 6 * l + 6]
        last = l == nl - 1
        h0, a1, b1, f1, h1, u, q, k4, v4, cat, mo, h2, a2, b2, f2 = saved[l]
        dh, dab, s, n, df, pk2 = _ffn_bwd(h2, dh, a2, b2, f2, modv[l], gvec[l], w2i, w2o, T=T, mrow=6, grow=2,
                                          ctx_active=not last, name=f"ffn2_bwd_{l}")
        g_w2i = _wgrad(dab, n, bk=FCH, sh=2 * DFF // NDEV, name=f"wgrad_ffn2_in_{l}")
        g_w2o = _wgrad(s, df, bk=FCH, sh=DFF // NDEV, name=f"wgrad_ffn2_out_{l}")
        dcat, dmix, pko = _mixout_bwd(dh, mo, modv[l], wo, T=T, ctx_active=not last, name=f"mixout_bwd_{l}")
        g_wo = _wgrad(cat, dmix, bk=D, sh=D // NDEV, name=f"wgrad_out_{l}")
        dps, dwp[l], dsc[l] = _pool_bwd(u, dcat, w_pool[l], ps2[l], T=T, name=f"pool_bwd_{l}")
        du, dq, dk, dv, dsk[l] = _attn_bwd(q, k4, v4, dcat, dps, sink[l], T=T, name=f"attn_bwd_{l}")
        dh, dproj, n, pkm = _mixproj_bwd(h1, dh, du, dq, dk, dv, modv[l], gvec[l], wi, cos, sin, T=T, name=f"mixproj_bwd_{l}")
        g_wi = _wgrad(dproj, n, bk=PROJ, sh=PROJ // NDEV, name=f"wgrad_in_{l}")
        dh, dab, s, n, df, pk1 = _ffn_bwd(h0, dh, a1, b1, f1, modv[l], gvec[l], w1i, w1o, T=T, mrow=0, grow=0,
                                          ctx_active=True, name=f"ffn1_bwd_{l}")
        g_w1i = _wgrad(dab, n, bk=FCH, sh=2 * DFF // NDEV, name=f"wgrad_ffn1_in_{l}")
        g_w1o = _wgrad(s, df, bk=FCH, sh=DFF // NDEV, name=f"wgrad_ffn1_out_{l}")
        packets[l] = [pk1, pkm + pko, pk2]
        big[l] = [g_w1i, g_w1o, g_wi, g_wo, g_w2i, g_w2o]
    return loss_part, dh, packets, dnf, dwp, dsc, dsk, big
```

```python
import functools

import jax
import jax.numpy as jnp
from jax import lax
from jax.experimental import pallas as pl
from jax.experimental.pallas import tpu as pltpu

F32, BF16 = jnp.float32, jnp.bfloat16

D = 1024
LC = 256
DFF = 2816
NMOD = 9
PW = 512
AW = 512
KVW = 128
PROJ = PW + AW + 2 * KVW
HD = 64
BLK = 128
GRID_W = 64
POOL_WINDOWS = (2, 4, 8, 16)
EPS = 1e-6
NEG = -1e30
ROPE_BASE = 10000.0
NDEV = 8
MESH = pl.DeviceIdType.MESH

ADAM_LR, ADAM_B1, ADAM_B2, ADAM_EPS, ADAM_WD, ADAM_STEP = 0.001, 0.9, 0.999, 1e-08, 0.01, 10

VMEM_LIMIT = 56 * 1024 * 1024
TM = 256
FCH = 1408

ANY = pl.BlockSpec(memory_space=pl.ANY)
SMEM = pl.BlockSpec(memory_space=pltpu.SMEM)


def _params(ngrid=1):
    return pltpu.CompilerParams(dimension_semantics=("arbitrary",) * ngrid, vmem_limit_bytes=VMEM_LIMIT)


def _dot(a, b):
    return jnp.dot(a, b, preferred_element_type=F32)


def _dot_nt(a, b):
    return lax.dot_general(a, b, (((1,), (1,)), ((), ())), preferred_element_type=F32)


def _dot_tn(a, b):
    return lax.dot_general(a, b, (((0,), (0,)), ((), ())), preferred_element_type=F32)


def _sigmoid(x):
    return 1.0 / (1.0 + jnp.exp(-x))


def _rows(tm, w):
    return pl.BlockSpec((tm, w), lambda i: (i, 0))


def _full(shape):
    nd = len(shape)
    return pl.BlockSpec(shape, lambda *_: (0,) * nd)


def _sds(shape, dtype):
    return jax.ShapeDtypeStruct(shape, dtype)


def _norm_mod(h, g, shift, scale):
    r = lax.rsqrt(jnp.mean(h * h, axis=-1, keepdims=True) + EPS)
    xhat = h * r
    y = xhat * g
    return r, xhat, y, y * (1.0 + scale) + shift


def _norm_mod_bwd(dn, r, xhat, y, g, scale):
    dshift = jnp.sum(dn, axis=0, keepdims=True)
    dscale = jnp.sum(dn * y, axis=0, keepdims=True)
    dy = dn * (1.0 + scale)
    dg = jnp.sum(dy * xhat, axis=0, keepdims=True)
    dxh = dy * g
    dh = r * (dxh - xhat * jnp.mean(dxh * xhat, axis=-1, keepdims=True))
    return dh, dshift, dscale, dg


def _acc_partials(part_ref, first, rows):
    @pl.when(first)
    def _():
        part_ref[...] = jnp.zeros_like(part_ref)

    for r, val in rows.items():
        part_ref[0, r : r + 1, :] += val


def _mod_spec(n_lat):
    return pl.BlockSpec((1, 16, D), lambda i: (i // n_lat, 0, 0))


def _part_spec(n_lat):
    return pl.BlockSpec((1, 8, D), lambda i: (i // n_lat, 0, 0))


def _load_weights(pairs, sem):
    copies = [pltpu.make_async_copy(src, dst, sem.at[k]) for k, (src, dst) in enumerate(pairs)]
    for cp in copies:
        cp.start()
    for cp in copies:
        cp.wait()


def _ffn_fwd(h, modv, gvec, win, wout, *, T, mrow, grow, ctx_active, name, carry=None):
    R = h.shape[0]
    n_lat, n_tiles = T // TM, R // TM

    def body(h_ref, mod_ref, g_ref, win_hbm, wout_hbm, ho_ref, a_ref, b_ref, f_ref, win_v, wout_v, sem):
        i = pl.program_id(0)

        @pl.when(i == 0)
        def _():
            _load_weights([(win_hbm, win_v), (wout_hbm, wout_v)], sem)

        def compute():
            h = h_ref[...]
            shift, scale, gate = (mod_ref[0, mrow + k : mrow + k + 1, :] for k in range(3))
            _, _, _, n = _norm_mod(h, g_ref[grow : grow + 1, :], shift, scale)
            n_bf = n.astype(BF16)
            acc = jnp.zeros((TM, D), F32)
            for c0 in range(0, DFF, FCH):
                a = _dot_nt(n_bf, win_v[c0 : c0 + FCH, :])
                b = _dot_nt(n_bf, win_v[DFF + c0 : DFF + c0 + FCH, :])
                a_ref[:, c0 : c0 + FCH] = a.astype(BF16)
                b_ref[:, c0 : c0 + FCH] = b.astype(BF16)
                s = a * _sigmoid(a) * b
                acc = acc + _dot(s.astype(BF16), wout_v[c0 : c0 + FCH, :])
            f_ref[...] = acc.astype(BF16)
            ho_ref[...] = h + (0.5 * gate) * acc

        if ctx_active:
            compute()
        else:
            pl.when(i < n_lat)(compute)

            @pl.when(i >= n_lat)
            def _():
                ho_ref[...] = h_ref[...]
                a_ref[...] = jnp.zeros_like(a_ref)
                b_ref[...] = jnp.zeros_like(b_ref)
                f_ref[...] = jnp.zeros_like(f_ref)

    return _call(
        body,
        name=name,
        grid=(n_tiles,),
        in_specs=[_rows(TM, D), _mod_spec(n_lat), _full((8, D)), ANY, ANY],
        out_specs=[_rows(TM, D), _rows(TM, DFF), _rows(TM, DFF), _rows(TM, D)],
        out_shape=[_sds((R, D), F32), _sds((R, DFF), BF16), _sds((R, DFF), BF16), _sds((R, D), BF16)],
        scratch_shapes=[pltpu.VMEM((2 * DFF, D), BF16), pltpu.VMEM((DFF, D), BF16), pltpu.SemaphoreType.DMA((2,))],
        args=(h, modv, gvec, win, wout),
        carry=carry,
    )


def _ffn_bwd(h, dho, a, b, f, modv, gvec, win, wout, *, T, mrow, grow, ctx_active, name, carry=None):
    R = h.shape[0]
    n_lat, n_tiles = T // TM, R // TM

    def body(h_ref, dho_ref, a_ref, b_ref, f_ref, mod_ref, g_ref, win_hbm, wout_hbm,
             dh_ref, dab_ref, s_ref, n_ref, df_ref, part_ref, win_v, wout_v, sem):
        i = pl.program_id(0)

        @pl.when(i == 0)
        def _():
            _load_weights([(win_hbm, win_v), (wout_hbm, wout_v)], sem)

        first = jnp.logical_or(i == 0, i == n_lat)

        def compute():
            h = h_ref[...]
            dho = dho_ref[...]
            shift, scale, gate = (mod_ref[0, mrow + k : mrow + k + 1, :] for k in range(3))
            g = g_ref[grow : grow + 1, :]
            r, xhat, y, n = _norm_mod(h, g, shift, scale)
            dgate = 0.5 * jnp.sum(dho * f_ref[...].astype(F32), axis=0, keepdims=True)
            df_bf = ((0.5 * gate) * dho).astype(BF16)
            df_ref[...] = df_bf
            n_ref[...] = n.astype(BF16)
            dn = jnp.zeros((TM, D), F32)
            for c0 in range(0, DFF, FCH):
                ds = _dot_nt(df_bf, wout_v[c0 : c0 + FCH, :])
                av = a_ref[:, c0 : c0 + FCH].astype(F32)
                bv = b_ref[:, c0 : c0 + FCH].astype(F32)
                sig = _sigmoid(av)
                sa = av * sig
                s_ref[:, c0 : c0 + FCH] = (sa * bv).astype(BF16)
                da = (ds * bv * (sig * (1.0 + av * (1.0 - sig)))).astype(BF16)
                db = (ds * sa).astype(BF16)
                dab_ref[:, c0 : c0 + FCH] = da
                dab_ref[:, DFF + c0 : DFF + c0 + FCH] = db
                dn = dn + _dot(da, win_v[c0 : c0 + FCH, :]) + _dot(db, win_v[DFF + c0 : DFF + c0 + FCH, :])
            dh, dshift, dscale, dg = _norm_mod_bwd(dn, r, xhat, y, g, scale)
            dh_ref[...] = dho + dh
            _acc_partials(part_ref, first, {0: dshift, 1: dscale, 2: dgate, 3: dg})

        if ctx_active:
            compute()
        else:
            pl.when(i < n_lat)(compute)

            @pl.when(i >= n_lat)
            def _():
                dh_ref[...] = dho_ref[...]
                dab_ref[...] = jnp.zeros_like(dab_ref)
                s_ref[...] = jnp.zeros_like(s_ref)
                n_ref[...] = jnp.zeros_like(n_ref)
                df_ref[...] = jnp.zeros_like(df_ref)
                part_ref[...] = jnp.zeros_like(part_ref)

    return _call(
        body,
        name=name,
        grid=(n_tiles,),
        in_specs=[_rows(TM, D), _rows(TM, D), _rows(TM, DFF), _rows(TM, DFF), _rows(TM, D),
                  _mod_spec(n_lat), _full((8, D)), ANY, ANY],
        out_specs=[_rows(TM, D), _rows(TM, 2 * DFF), _rows(TM, DFF), _rows(TM, D), _rows(TM, D), _part_spec(n_lat)],
        out_shape=[_sds((R, D), F32), _sds((R, 2 * DFF), BF16), _sds((R, DFF), BF16), _sds((R, D), BF16),
                   _sds((R, D), BF16), _sds((2, 8, D), F32)],
        scratch_shapes=[pltpu.VMEM((2 * DFF, D), BF16), pltpu.VMEM((DFF, D), BF16), pltpu.SemaphoreType.DMA((2,))],
        args=(h, dho, a, b, f, modv, gvec, win, wout),
        carry=carry,
    )


def _wgrad(x, y, *, bk, sh, name):
    R, kx = x.shape
    n = y.shape[1]
    tr = R // 2
    nr, nsh = R // tr, bk // sh

    def body(x_ref, y_ref, o_ref, acc):
        r = pl.program_id(1)

        @pl.when(r == 0)
        def _():
            acc[...] = jnp.zeros_like(acc)

        acc[...] += _dot_tn(x_ref[...], y_ref[...])

        @pl.when(r == nr - 1)
        def _():
            for s in range(nsh):
                o_ref[s] = acc[s * sh : (s + 1) * sh, :].astype(BF16)

    return pl.pallas_call(
        body,
        name=name,
        grid=(kx // bk, nr),
        in_specs=[pl.BlockSpec((tr, bk), lambda k, r: (r, k)), pl.BlockSpec((tr, n), lambda k, r: (r, 0))],
        out_specs=pl.BlockSpec((nsh, sh, n), lambda k, r: (k, 0, 0)),
        out_shape=_sds((kx // sh, sh, n), BF16),
        scratch_shapes=[pltpu.VMEM((bk, n), F32)],
        compiler_params=_params(2),
    )(x, y)


def _rot_half(x):
    lane = lax.broadcasted_iota(jnp.int32, x.shape, 1)
    return jnp.where((lane & (HD - 1)) < HD // 2, -pltpu.roll(x, 128 - HD // 2, 1), pltpu.roll(x, HD // 2, 1))


def _tile_sel():
    i = lax.broadcasted_iota(jnp.int32, (KVW, AW), 0)
    j = lax.broadcasted_iota(jnp.int32, (KVW, AW), 1)
    return jnp.where(i == (j // 256) * HD + (j & (HD - 1)), 1.0, 0.0).astype(BF16)


def _mixproj_fwd(h, modv, gvec, win, cos, sin, *, T, name):
    R = h.shape[0]
    n_lat, n_tiles = T // TM, R // TM

    def body(h_ref, mod_ref, g_ref, win_ref, cos_ref, sin_ref, u_ref, q_ref, k4_ref, v4_ref):
        shift, scale = mod_ref[0, 3:4, :], mod_ref[0, 4:5, :]
        _, _, _, n = _norm_mod(h_ref[...], g_ref[1:2, :], shift, scale)
        proj = _dot_nt(n.astype(BF16), win_ref[...])
        u_ref[...] = proj[:, :PW]
        cs, sn = cos_ref[...], sin_ref[...]
        for s in range(AW // 128):
            x = proj[:, PW + 128 * s : PW + 128 * (s + 1)]
            q_ref[:, 128 * s : 128 * (s + 1)] = (x * cs + _rot_half(x) * sn).astype(BF16)
        k = proj[:, PW + AW : PW + AW + KVW]
        k = (k * cs + _rot_half(k) * sn).astype(BF16)
        v = proj[:, PW + AW + KVW :].astype(BF16)
        sel = _tile_sel()
        k4_ref[...] = _dot(k, sel).astype(BF16)
        v4_ref[...] = _dot(v, sel).astype(BF16)

    return pl.pallas_call(
        body,
        name=name,
        grid=(n_tiles,),
        in_specs=[_rows(TM, D), _mod_spec(n_lat), _full((8, D)), _full((PROJ, D)), _rows(TM, 128), _rows(TM, 128)],
        out_specs=[_rows(TM, PW), _rows(TM, AW), _rows(TM, AW), _rows(TM, AW)],
        out_shape=[_sds((R, PW), F32), _sds((R, AW), BF16), _sds((R, AW), BF16), _sds((R, AW), BF16)],
        compiler_params=_params(),
    )(h, modv, gvec, win, cos, sin)


def _win_start(j, hi):
    return pl.multiple_of(jnp.clip((j - 1) * BLK, 0, hi - 3 * BLK), BLK)


def _hi_lo(x):
    hi = x.astype(BF16)
    return hi, (x - hi.astype(F32)).astype(BF16)


def _pool_bounds(t, w, T, R):
    is_ctx = t >= T
    lo = jnp.maximum(t - w // 2, jnp.where(is_ctx, T, 0))
    hi = jnp.minimum(t + w // 2, jnp.where(is_ctx, R, T))
    return lo, hi


def _pooled(u_v, j, T, R):
    start = _win_start(j, R)
    u3_hi, u3_lo = _hi_lo(u_v[pl.ds(start, 3 * BLK), :])
    ub = u_v[pl.ds(pl.multiple_of(j * BLK, BLK), BLK), :]
    t = j * BLK + lax.broadcasted_iota(jnp.int32, (BLK, 1), 0)
    pos = start + lax.broadcasted_iota(jnp.int32, (1, 3 * BLK), 1)
    pooled, counts = [], []
    for g, w in enumerate(POOL_WINDOWS):
        lo, hi = _pool_bounds(t, w, T, R)
        band = jnp.where(pos >= lo, jnp.where(pos < hi, 1.0, 0.0), 0.0).astype(BF16)
        sl = slice(g * 128, (g + 1) * 128)
        sums = _dot(band, u3_hi[:, sl]) + _dot(band, u3_lo[:, sl])
        cnt = (hi - lo).astype(F32)
        pooled.append(sums / cnt - ub[:, sl])
        counts.append(cnt)
    return pooled, counts


def _stack_heads(x):
    lane_h = lax.broadcasted_iota(jnp.int32, x.shape, 1) // HD
    return jnp.concatenate([jnp.where(lane_h == h, x, jnp.zeros_like(x)) for h in range(4)], axis=0)


def _unstack_heads(x):
    lane_h = lax.broadcasted_iota(jnp.int32, (BLK, 256), 1) // HD
    out = jnp.zeros((BLK, 256), F32)
    for h in range(4):
        out = out + jnp.where(lane_h == h, x[h * BLK : (h + 1) * BLK, :], 0.0)
    return out


def _attn_probs(qs, kl, kc, sink_ref, g, j, start_l, nbl):
    s_l = _dot_nt(qs, kl) * (HD ** -0.5)
    s_c = _dot_nt(qs, kc) * (HD ** -0.5)
    rowi = lax.broadcasted_iota(jnp.int32, (4 * BLK, 1), 0)
    qpos = j * BLK + (rowi & (BLK - 1))
    kpos = start_l + lax.broadcasted_iota(jnp.int32, (1, 3 * BLK), 1)
    reach = jnp.where(j < nbl, BLK, -1)
    s_l = jnp.where(jnp.abs(kpos - qpos) <= reach, s_l, NEG)
    rb = rowi // BLK
    sk = jnp.where(rb == 0, sink_ref[4 * g], jnp.where(rb == 1, sink_ref[4 * g + 1],
                   jnp.where(rb == 2, sink_ref[4 * g + 2], sink_ref[4 * g + 3])))
    m = jnp.maximum(jnp.maximum(jnp.max(s_l, axis=1, keepdims=True), jnp.max(s_c, axis=1, keepdims=True)), sk)
    e_l, e_c, e_s = jnp.exp(s_l - m), jnp.exp(s_c - m), jnp.exp(sk - m)
    inv = 1.0 / (jnp.sum(e_l, axis=1, keepdims=True) + jnp.sum(e_c, axis=1, keepdims=True) + e_s)
    return e_l * inv, e_c * inv, e_s * inv


def _attnpool_fwd(u, q, k4, v4, sink, w_pool, pool_scale, *, T, name, carry=None):
    R = u.shape[0]
    nb, nbl = R // BLK, T // BLK

    def body(q_ref, sink_ref, wp_ref, ps_ref, u_hbm, k4_hbm, v4_hbm, cat_ref, u_v, k4_v, v4_v, sem):
        j = pl.program_id(0)

        @pl.when(j == 0)
        def _():
            _load_weights([(u_hbm, u_v), (k4_hbm, k4_v), (v4_hbm, v4_v)], sem)

        pooled, _ = _pooled(u_v, j, T, R)
        for g in range(4):
            mixed = _dot(pooled[g].astype(BF16), wp_ref[g].astype(BF16)) * ps_ref[:, g * 128 : (g + 1) * 128]
            cat_ref[:, g * 128 : (g + 1) * 128] = mixed.astype(BF16)

        start_l = _win_start(j, T)
        for g in range(2):
            gl = slice(g * 256, (g + 1) * 256)
            qs = _stack_heads(q_ref[:, gl])
            p_l, p_c, _ = _attn_probs(qs, k4_v[pl.ds(start_l, 3 * BLK), gl], k4_v[T:R, gl], sink_ref, g, j, start_l, nbl)
            o = _dot(p_l.astype(BF16), v4_v[pl.ds(start_l, 3 * BLK), gl]) + _dot(p_c.astype(BF16), v4_v[T:R, gl])
            cat_ref[:, PW + g * 256 : PW + (g + 1) * 256] = _unstack_heads(o).astype(BF16)

    return _call(
        body,
        name=name,
        grid=(nb,),
        in_specs=[_rows(BLK, AW), SMEM, _full((4, 128, 128)), _full((1, PW)), ANY, ANY, ANY],
        out_specs=[_rows(BLK, D)],
        out_shape=[_sds((R, D), BF16)],
        scratch_shapes=[pltpu.VMEM((R, PW), F32), pltpu.VMEM((R, AW), BF16), pltpu.VMEM((R, AW), BF16),
                        pltpu.SemaphoreType.DMA((3,))],
        args=(q, sink, w_pool, pool_scale, u, k4, v4),
        carry=carry,
    )


def _mixout_fwd(h, cat, modv, wout, *, T, ctx_active, name):
    R = h.shape[0]
    n_lat, n_tiles = T // TM, R // TM

    def body(h_ref, cat_ref, mod_ref, w_ref, ho_ref, mo_ref):
        i = pl.program_id(0)

        def compute():
            mo = _dot(cat_ref[...], w_ref[...])
            mo_ref[...] = mo.astype(BF16)
            ho_ref[...] = h_ref[...] + mod_ref[0, 5:6, :] * mo

        if ctx_active:
            compute()
        else:
            pl.when(i < n_lat)(compute)

            @pl.when(i >= n_lat)
            def _():
                ho_ref[...] = h_ref[...]
                mo_ref[...] = jnp.zeros_like(mo_ref)

    return pl.pallas_call(
        body,
        name=name,
        grid=(n_tiles,),
        in_specs=[_rows(TM, D), _rows(TM, D), _mod_spec(n_lat), _full((D, D))],
        out_specs=[_rows(TM, D), _rows(TM, D)],
        out_shape=[_sds((R, D), F32), _sds((R, D), BF16)],
        compiler_params=_params(),
    )(h, cat, modv, wout)


def _mixout_bwd(dho, mo, modv, wout, *, T, ctx_active, name):
    R = dho.shape[0]
    n_lat, n_tiles = T // TM, R // TM

    def body(dho_ref, mo_ref, mod_ref, w_ref, dcat_ref, dmix_ref, part_ref):
        i = pl.program_id(0)
        first = jnp.logical_or(i == 0, i == n_lat)

        def compute():
            dho = dho_ref[...]
            dmix = (mod_ref[0, 5:6, :] * dho).astype(BF16)
            dmix_ref[...] = dmix
            dcat_ref[...] = _dot_nt(dmix, w_ref[...])
            dgate = jnp.sum(dho * mo_ref[...].astype(F32), axis=0, keepdims=True)
            _acc_partials(part_ref, first, {2: dgate})

        if ctx_active:
            compute()
        else:
            pl.when(i < n_lat)(compute)

            @pl.when(i >= n_lat)
            def _():
                dcat_ref[...] = jnp.zeros_like(dcat_ref)
                dmix_ref[...] = jnp.zeros_like(dmix_ref)
                part_ref[...] = jnp.zeros_like(part_ref)

    return pl.pallas_call(
        body,
        name=name,
        grid=(n_tiles,),
        in_specs=[_rows(TM, D), _rows(TM, D), _mod_spec(n_lat), _full((D, D))],
        out_specs=[_rows(TM, D), _rows(TM, D), _part_spec(n_lat)],
        out_shape=[_sds((R, D), F32), _sds((R, D), BF16), _sds((2, 8, D), F32)],
        compiler_params=_params(),
    )(dho, mo, modv, wout)


def _pool_bwd(u, dcat, w_pool, pool_scale, *, T, name):
    R = u.shape[0]
    nb = R // BLK

    def body(dcat_ref, wp_ref, ps_ref, u_hbm, dps_ref, dwp_ref, dsc_ref, u_v, sem):
        j = pl.program_id(0)

        @pl.when(j == 0)
        def _():
            _load_weights([(u_hbm, u_v)], sem)
            dwp_ref[...] = jnp.zeros_like(dwp_ref)
            dsc_ref[...] = jnp.zeros_like(dsc_ref)

        pooled, counts = _pooled(u_v, j, T, R)
        for g in range(4):
            sl = slice(g * 128, (g + 1) * 128)
            p_bf = pooled[g].astype(BF16)
            w_bf = wp_ref[g].astype(BF16)
            dmixed = dcat_ref[:, sl]
            dsc_ref[0:1, sl] += jnp.sum(dmixed * _dot(p_bf, w_bf), axis=0, keepdims=True)
            dmp = (dmixed * ps_ref[:, sl]).astype(BF16)
            dwp_ref[sl, :] += _dot_tn(p_bf, dmp)
            dps_ref[:, sl] = _dot_nt(dmp, w_bf) / counts[g]

    return pl.pallas_call(
        body,
        name=name,
        grid=(nb,),
        in_specs=[_rows(BLK, D), _full((4, 128, 128)), _full((1, PW)), ANY],
        out_specs=[_rows(BLK, PW), _full((PW, 128)), _full((8, PW))],
        out_shape=[_sds((R, PW), F32), _sds((PW, 128), F32), _sds((8, PW), F32)],
        scratch_shapes=[pltpu.VMEM((R, PW), F32), pltpu.SemaphoreType.DMA((1,))],
        compiler_params=_params(),
    )(dcat, w_pool, pool_scale, u)


def _fold_heads(x):
    y = x[:, :128] + x[:, 128:]
    return y + pltpu.roll(y, HD, 1)


def _attn_bwd(q, k4, v4, dcat, dps, sink, *, T, name, carry=None):
    R = q.shape[0]
    nb, nbl = R // BLK, T // BLK

    def body(q_ref, dcat_ref, sink_ref, k4_hbm, v4_hbm, dps_hbm, du_ref, dq_ref, dk_ref, dv_ref, dsk_ref,
             k4_v, v4_v, dps_v, sem):
        j = pl.program_id(0)

        @pl.when(j == 0)
        def _():
            _load_weights([(k4_hbm, k4_v), (v4_hbm, v4_v), (dps_hbm, dps_v)], sem)
            dk_ref[...] = jnp.zeros_like(dk_ref)
            dv_ref[...] = jnp.zeros_like(dv_ref)
            dsk_ref[...] = jnp.zeros_like(dsk_ref)

        start = _win_start(j, R)
        d3_hi, d3_lo = _hi_lo(dps_v[pl.ds(start, 3 * BLK), :])
        db = dps_v[pl.ds(pl.multiple_of(j * BLK, BLK), BLK), :]
        pos = j * BLK + lax.broadcasted_iota(jnp.int32, (BLK, 1), 0)
        t_r = start + lax.broadcasted_iota(jnp.int32, (1, 3 * BLK), 1)
        for g, w in enumerate(POOL_WINDOWS):
            sl = slice(g * 128, (g + 1) * 128)
            lo_r, hi_r = _pool_bounds(t_r, w, T, R)
            band_t = jnp.where(pos >= lo_r, jnp.where(pos < hi_r, 1.0, 0.0), 0.0).astype(BF16)
            lo_c, hi_c = _pool_bounds(pos, w, T, R)
            du_ref[:, sl] = _dot(band_t, d3_hi[:, sl]) + _dot(band_t, d3_lo[:, sl]) - db[:, sl] * (hi_c - lo_c).astype(F32)

        start_l = _win_start(j, T)
        rb = lax.broadcasted_iota(jnp.int32, (4 * BLK, 1), 0) // BLK
        lane = lax.broadcasted_iota(jnp.int32, (1, 128), 1)
        dk_l, dk_c, dv_l, dv_c = [], [], [], []
        for g in range(2):
            gl = slice(g * 256, (g + 1) * 256)
            qs = _stack_heads(q_ref[:, gl])
            kl, kc = k4_v[pl.ds(start_l, 3 * BLK), gl], k4_v[T:R, gl]
            vl, vc = v4_v[pl.ds(start_l, 3 * BLK), gl], v4_v[T:R, gl]
            p_l, p_c, p_s = _attn_probs(qs, kl, kc, sink_ref, g, j, start_l, nbl)
            dos = _stack_heads(dcat_ref[:, PW + g * 256 : PW + (g + 1) * 256]).astype(BF16)
            dp_l, dp_c = _dot_nt(dos, vl), _dot_nt(dos, vc)
            delta = jnp.sum(p_l * dp_l, axis=1, keepdims=True) + jnp.sum(p_c * dp_c, axis=1, keepdims=True)
            ds_l = (p_l * (dp_l - delta) * (HD ** -0.5)).astype(BF16)
            ds_c = (p_c * (dp_c - delta) * (HD ** -0.5)).astype(BF16)
            dq_ref[:, gl] = _unstack_heads(_dot(ds_l, kl) + _dot(ds_c, kc))
            dk_l.append(_fold_heads(_dot_tn(ds_l, qs)))
            dk_c.append(_fold_heads(_dot_tn(ds_c, qs)))
            dv_l.append(_fold_heads(_dot_tn(p_l.astype(BF16), dos)))
            dv_c.append(_fold_heads(_dot_tn(p_c.astype(BF16), dos)))
            dsink = -p_s * delta
            for h in range(4):
                tot = jnp.sum(jnp.where(rb == h, dsink, 0.0), axis=0, keepdims=True)
                dsk_ref[4 * g + h : 4 * g + h + 1, :] += jnp.broadcast_to(tot, (1, 128))
        first = lane < HD
        dk_ref[pl.ds(start_l, 3 * BLK), :] += jnp.where(first, dk_l[0], dk_l[1])
        dk_ref[T:R, :] += jnp.where(first, dk_c[0], dk_c[1])
        dv_ref[pl.ds(start_l, 3 * BLK), :] += jnp.where(first, dv_l[0], dv_l[1])
        dv_ref[T:R, :] += jnp.where(first, dv_c[0], dv_c[1])

    return _call(
        body,
        name=name,
        grid=(nb,),
        in_specs=[_rows(BLK, AW), _rows(BLK, D), SMEM, ANY, ANY, ANY],
        out_specs=[_rows(BLK, PW), _rows(BLK, AW), _full((R, KVW)), _full((R, KVW)), _full((8, 128))],
        out_shape=[_sds((R, PW), F32), _sds((R, AW), F32), _sds((R, KVW), F32), _sds((R, KVW), F32),
                   _sds((8, 128), F32)],
        scratch_shapes=[pltpu.VMEM((R, AW), BF16), pltpu.VMEM((R, AW), BF16), pltpu.VMEM((R, PW), F32),
                        pltpu.SemaphoreType.DMA((3,))],
        args=(q, dcat, sink, k4, v4, dps),
        carry=carry,
    )


def _mixproj_bwd(h, dho, du, dq, dk, dv, modv, gvec, win, cos, sin, *, T, name):
    R = h.shape[0]
    n_lat, n_tiles = T // TM, R // TM

    def body(h_ref, dho_ref, du_ref, dq_ref, dk_ref, dv_ref, mod_ref, g_ref, win_ref, cos_ref, sin_ref,
             dh_ref, dproj_ref, n_ref, part_ref):
        i = pl.program_id(0)
        first = jnp.logical_or(i == 0, i == n_lat)
        shift, scale = mod_ref[0, 3:4, :], mod_ref[0, 4:5, :]
        g = g_ref[1:2, :]
        r, xhat, y, n = _norm_mod(h_ref[...], g, shift, scale)
        n_ref[...] = n.astype(BF16)
        cs, sn = cos_ref[...], sin_ref[...]
        dproj_ref[:, :PW] = du_ref[...].astype(BF16)
        for s in range(AW // 128):
            x = dq_ref[:, 128 * s : 128 * (s + 1)]
            dproj_ref[:, PW + 128 * s : PW + 128 * (s + 1)] = (x * cs - _rot_half(x) * sn).astype(BF16)
        x = dk_ref[...]
        dproj_ref[:, PW + AW : PW + AW + KVW] = (x * cs - _rot_half(x) * sn).astype(BF16)
        dproj_ref[:, PW + AW + KVW :] = dv_ref[...].astype(BF16)
        dn = _dot(dproj_ref[...], win_ref[...])
        dh, dshift, dscale, dg = _norm_mod_bwd(dn, r, xhat, y, g, scale)
        dh_ref[...] = dho_ref[...] + dh
        _acc_partials(part_ref, first, {0: dshift, 1: dscale, 3: dg})

    return pl.pallas_call(
        body,
        name=name,
        grid=(n_tiles,),
        in_specs=[_rows(TM, D), _rows(TM, D), _rows(TM, PW), _rows(TM, AW), _rows(TM, KVW), _rows(TM, KVW),
                  _mod_spec(n_lat), _full((8, D)), _full((PROJ, D)), _rows(TM, 128), _rows(TM, 128)],
        out_specs=[_rows(TM, D), _rows(TM, PROJ), _rows(TM, D), _part_spec(n_lat)],
        out_shape=[_sds((R, D), F32), _sds((R, PROJ), BF16), _sds((R, D), BF16), _sds((2, 8, D), F32)],
        compiler_params=_params(),
    )(h, dho, du, dq, dk, dv, modv, gvec, win, cos, sin)


def _loss_head(h, target, g_final, *, T, name):
    R = h.shape[0]
    n_lat, n_tiles = T // TM, R // TM

    def body(h_ref, t_ref, g_ref, dh_ref, loss_ref, dg_ref):
        i = pl.program_id(0)

        @pl.when(i == 0)
        def _():
            loss_ref[...] = jnp.zeros_like(loss_ref)
            dg_ref[...] = jnp.zeros_like(dg_ref)

        @pl.when(i < n_lat)
        def _():
            h = h_ref[...]
            g = g_ref[...]
            r = lax.rsqrt(jnp.mean(h * h, axis=-1, keepdims=True) + EPS)
            xhat = h * r
            err = xhat * g - t_ref[...]
            tot = jnp.sum(jnp.sum(err * err, axis=1, keepdims=True), axis=0, keepdims=True)
            loss_ref[...] += jnp.broadcast_to(tot * (0.5 / D), loss_ref.shape)
            dy = err * (1.0 / D)
            dg_ref[0:1, :] += jnp.sum(dy * xhat, axis=0, keepdims=True)
            dxh = dy * g
            dh_ref[...] = r * (dxh - xhat * jnp.mean(dxh * xhat, axis=-1, keepdims=True))

        @pl.when(i >= n_lat)
        def _():
            dh_ref[...] = jnp.zeros_like(dh_ref)

    return pl.pallas_call(
        body,
        name=name,
        grid=(n_tiles,),
        in_specs=[_rows(TM, D), pl.BlockSpec((TM, D), lambda i: (jnp.minimum(i, n_lat - 1), 0)), _full((1, D))],
        out_specs=[_rows(TM, D), _full((8, 128)), _full((8, D))],
        out_shape=[_sds((R, D), F32), _sds((8, 128), F32), _sds((8, D), F32)],
        compiler_params=_params(),
    )(h, target, g_final)


def _mod_fwd(c16, w_mod, b_cols, *, name):
    nl, _, cols = w_mod.shape

    def body(c_ref, w_ref, b_ref, o_ref):
        c = c_ref[...]
        sc = (c * _sigmoid(c)).astype(BF16)
        o_ref[0] = _dot(sc, w_ref[0].astype(BF16)) + b_ref[0]

    return pl.pallas_call(
        body,
        name=name,
        grid=(nl,),
        in_specs=[_full((16, D)), pl.BlockSpec((1, D, cols), lambda l: (l, 0, 0)),
                  pl.BlockSpec((1, 1, cols), lambda l: (l, 0, 0))],
        out_specs=pl.BlockSpec((1, 16, cols), lambda l: (l, 0, 0)),
        out_shape=_sds((nl, 16, cols), F32),
        compiler_params=_params(),
    )(c16, w_mod, b_cols)


def _mod_bwd(c16, dm_cols, w_mod, *, name):
    nl, _, cols = w_mod.shape

    def body(c_ref, dm_ref, w_ref, gw_ref, dc_ref):
        c = c_ref[...]
        sc = (c * _sigmoid(c)).astype(BF16)
        dm = dm_ref[0].astype(BF16)
        gw_ref[0] = _dot_tn(sc, dm)
        dc_ref[0] = _dot_nt(dm, w_ref[0].astype(BF16))

    return pl.pallas_call(
        body,
        name=name,
        grid=(nl,),
        in_specs=[_full((16, D)), pl.BlockSpec((1, 16, cols), lambda l: (l, 0, 0)),
                  pl.BlockSpec((1, D, cols), lambda l: (l, 0, 0))],
        out_specs=[pl.BlockSpec((1, D, cols), lambda l: (l, 0, 0)), pl.BlockSpec((1, 16, D), lambda l: (l, 0, 0))],
        out_shape=[_sds((nl, D, cols), F32), _sds((nl, 16, D), F32)],
        compiler_params=_params(),
    )(c16, dm_cols, w_mod)


def _coords():
    return lax.axis_index("x"), lax.axis_index("y"), lax.axis_index("c")


def _peer(k, x, y, c):
    return (1 - x if k & 4 else x, 1 - y if k & 2 else y, 1 - c if k & 1 else c)


def _lin(p):
    return 4 * p[0] + 2 * p[1] + p[2]


def _view(ref, slot):
    return ref if slot is None else ref.at[slot]


class _Round:
    def __init__(self, ins, out_shapes, plan, local_plan=(), n_alias=0):
        self.ins, self.out_shapes = list(ins), list(out_shapes)
        self.plan, self.local_plan, self.n_alias = list(plan), list(local_plan), n_alias

    def sems(self):
        return [pltpu.SemaphoreType.DMA((len(self.plan),)), pltpu.SemaphoreType.DMA((len(self.plan),)),
                pltpu.SemaphoreType.DMA((max(len(self.local_plan), 1),))]

    def _copies(self, in_refs, out_refs, sems):
        in_refs = list(out_refs[: self.n_alias]) + list(in_refs[self.n_alias :])
        send_sems, recv_sems, loc_sems = sems
        x, y, c = _coords()
        me = _lin((x, y, c))
        sends, recvs = [], []
        for idx, (k, ii, sfn, oi, dfn) in enumerate(self.plan):
            peer = _peer(k, x, y, c)
            for sender, receiver, group in ((me, _lin(peer), sends), (_lin(peer), me, recvs)):
                group.append(pltpu.make_async_remote_copy(
                    src_ref=_view(in_refs[ii], sfn(sender, receiver)), dst_ref=_view(out_refs[oi], dfn(sender, receiver)),
                    send_sem=send_sems.at[idx], recv_sem=recv_sems.at[idx], device_id=peer, device_id_type=MESH))
        locs = [pltpu.make_async_copy(_view(in_refs[ii], sfn(me)), _view(out_refs[oi], dfn(me)), loc_sems.at[idx])
                for idx, (ii, sfn, oi, dfn) in enumerate(self.local_plan)]
        return sends, recvs, locs

    def start(self, in_refs, out_refs, sems):
        sends, _, locs = self._copies(in_refs, out_refs, sems)
        for cp in sends + locs:
            cp.start()

    def finish(self, in_refs, out_refs, sems):
        sends, recvs, locs = self._copies(in_refs, out_refs, sems)
        for cp in recvs:
            cp.wait_recv()
        for cp in sends:
            cp.wait_send()
        for cp in locs:
            cp.wait()


def _exchange(name, rnd):
    n_in, n_out = len(rnd.ins), len(rnd.out_shapes)

    def body(*refs):
        in_refs, out_refs, sems = refs[:n_in], refs[n_in : n_in + n_out], refs[n_in + n_out :]
        rnd.start(in_refs, out_refs, sems)
        rnd.finish(in_refs, out_refs, sems)

    return pl.pallas_call(
        body, name=name, in_specs=[ANY] * n_in, out_specs=[ANY] * n_out, out_shape=rnd.out_shapes,
        scratch_shapes=rnd.sems(), input_output_aliases={i: i for i in range(rnd.n_alias)})(*rnd.ins)


def _call(body, *, name, grid, in_specs, out_specs, out_shape, scratch_shapes, args, carry=None):
    params = _params(len(grid))
    if carry is None:
        outs = pl.pallas_call(body, name=name, grid=grid, in_specs=in_specs, out_specs=out_specs, out_shape=out_shape,
                              scratch_shapes=scratch_shapes, compiler_params=params)(*args)
        return list(outs), []
    n_ci, n_co, n_cs = len(in_specs), len(out_shape), len(scratch_shapes)
    n_xi, n_xo = len(carry.ins), len(carry.out_shapes)

    def wrapped(*refs):
        ci, xi = refs[:n_ci], refs[n_ci : n_ci + n_xi]
        o0 = n_ci + n_xi
        co, xo = refs[o0 : o0 + n_co], refs[o0 + n_co : o0 + n_co + n_xo]
        s0 = o0 + n_co + n_xo
        cs, sems = refs[s0 : s0 + n_cs], refs[s0 + n_cs :]
        ids = [pl.program_id(a) for a in range(len(grid))]
        first = functools.reduce(jnp.logical_and, [i == 0 for i in ids])
        last = functools.reduce(jnp.logical_and, [i == g - 1 for i, g in zip(ids, grid)])

        @pl.when(first)
        def _():
            carry.start(xi, xo, sems)

        body(*ci, *co, *cs)

        @pl.when(last)
        def _():
            carry.finish(xi, xo, sems)

    outs = pl.pallas_call(
        wrapped, name=name, grid=grid, in_specs=list(in_specs) + [ANY] * n_xi, out_specs=list(out_specs) + [ANY] * n_xo,
        out_shape=list(out_shape) + carry.out_shapes, scratch_shapes=list(scratch_shapes) + carry.sems(),
        input_output_aliases={n_ci + i: n_co + i for i in range(carry.n_alias)}, compiler_params=params,
    )(*args, *carry.ins)
    return list(outs[:n_co]), list(outs[n_co:])


def _gather_direct(arrays):
    na = len(arrays)
    outs = [_sds((NDEV,) + a.shape, a.dtype) for a in arrays]
    plan = [(k, i, lambda s, r: None, i, lambda s, r: s) for i in range(na) for k in range(1, NDEV)]
    return _Round(arrays, outs, plan, [(i, lambda m: None, i, lambda m: m) for i in range(na)])


def _gather_a(arrays):
    na = len(arrays)
    outs = [_sds((NDEV,) + a.shape, a.dtype) for a in arrays]
    plan = [(k, i, lambda s, r: None, i, lambda s, r: s) for i in range(na) for k in (1, 2, 4, 6)]
    return _Round(arrays, outs, plan, [(i, lambda m: None, i, lambda m: m) for i in range(na)])


def _gather_b(got):
    na = len(got)
    plan = [(1, i, (lambda s, r, k=k: s ^ k), i, (lambda s, r, k=k: s ^ k)) for i in range(na) for k in (2, 4, 6)]
    return _Round(got, [_sds(g.shape, g.dtype) for g in got], plan, n_alias=na)


def _all_gather(name, arrays):
    return _exchange(name + "_b", _gather_b(_exchange(name + "_a", _gather_a(arrays))))


def _scatter_1(grads):
    plan = [(1, i, (lambda s, r, q=q: 2 * q + (r & 1)), i, (lambda s, r, q=q: q))
            for i in range(len(grads)) for q in range(4)]
    return _Round(grads, [_sds((4,) + g.shape[1:], g.dtype) for g in grads], plan)


def _scatter_2(chip):
    plan = [(k, i, lambda s, r: r >> 1, i, (lambda s, r, j=j: j)) for i in range(len(chip)) for j, k in enumerate((2, 4, 6))]
    return _Round(chip, [_sds((3,) + g.shape[1:], g.dtype) for g in chip], plan)


def _add_pairs(g, got, pos, *, name):
    _, sh, w = g.shape

    def body(pos_ref, g_ref, r_ref, o_ref):
        o_ref[...] = (g_ref[...].astype(F32) + r_ref[...].astype(F32)).astype(o_ref.dtype)

    return pl.pallas_call(
        body,
        name=name,
        grid_spec=pltpu.PrefetchScalarGridSpec(
            num_scalar_prefetch=1, grid=(4,),
            in_specs=[pl.BlockSpec((1, sh, w), lambda q, p: (2 * q + p[0], 0, 0)),
                      pl.BlockSpec((1, sh, w), lambda q, p: (q, 0, 0))],
            out_specs=pl.BlockSpec((1, sh, w), lambda q, p: (q, 0, 0))),
        out_shape=_sds((4, sh, w), g.dtype),
        compiler_params=_params(),
    )(pos, g, got)


def _sum_chips(chip, got, pos, *, name):
    _, sh, w = chip.shape

    def body(pos_ref, c_ref, r_ref, o_ref):
        acc = c_ref[0].astype(F32)
        for s in range(3):
            acc = acc + r_ref[s].astype(F32)
        o_ref[...] = acc

    return pl.pallas_call(
        body,
        name=name,
        grid_spec=pltpu.PrefetchScalarGridSpec(
            num_scalar_prefetch=1, grid=(1,),
            in_specs=[pl.BlockSpec((1, sh, w), lambda i, p: (p[1], 0, 0)), pl.BlockSpec((3, sh, w), lambda i, p: (0, 0, 0))],
            out_specs=pl.BlockSpec((sh, w), lambda i, p: (0, 0))),
        out_shape=_sds((sh, w), F32),
        compiler_params=_params(),
    )(pos, chip, got)


def _adamw_math(w, g, m, v):
    m2 = ADAM_B1 * m + (1.0 - ADAM_B1) * g
    v2 = ADAM_B2 * v + (1.0 - ADAM_B2) * (g * g)
    m_hat = m2 / (1.0 - ADAM_B1 ** ADAM_STEP)
    v_hat = v2 / (1.0 - ADAM_B2 ** ADAM_STEP)
    delta = -ADAM_LR * (m_hat / (jnp.sqrt(v_hat) + ADAM_EPS) + ADAM_WD * w)
    return delta, m2, v2


def _adamw(w, g, m, v, *, name):
    shape = w.shape
    flat = [t.reshape(-1, shape[-1]) for t in (w, g, m, v)]
    rows, cols = flat[0].shape
    tr = rows // 8 if rows % 64 == 0 else rows
    spec = _rows(tr, cols)

    def body(w_ref, g_ref, m_ref, v_ref, d_ref, m2_ref, v2_ref):
        d_ref[...], m2_ref[...], v2_ref[...] = _adamw_math(w_ref[...], g_ref[...], m_ref[...], v_ref[...])

    outs = pl.pallas_call(
        body, name=name, grid=(rows // tr,), in_specs=[spec] * 4, out_specs=[spec] * 3,
        out_shape=[_sds((rows, cols), F32)] * 3, compiler_params=_params())(*flat)
    return tuple(o.reshape(shape) for o in outs)


def _small_sums(packets, nf, dwp, dsc, dsk, *, name):
    flat = [p for layer in packets for p in layer]

    def total(ref, *idx):
        acc = ref[(0,) + idx]
        for dev in range(1, NDEV):
            acc = acc + ref[(dev,) + idx]
        return acc

    def body(*refs):
        pk = refs[:6]
        nf_ref, dwp0, dwp1, dsc0, dsc1, dsk0, dsk1 = refs[6:13]
        dm_ref, gb_ref, gn_ref, gnf_ref, gwp_ref, gps_ref, gsk_ref = refs[13:]
        dm_ref[...] = jnp.zeros_like(dm_ref)
        gn_ref[...] = jnp.zeros_like(gn_ref)
        for l in range(2):
            for sb in range(3):
                p = pk[3 * l + sb]
                for r in range(3):
                    col = slice((3 * sb + r) * D, (3 * sb + r + 1) * D)
                    lat = p[0, 0, r : r + 1, :]
                    dm_ref[l, 0:1, col] = lat
                    for dev in range(1, NDEV):
                        row = p[dev, 0, r : r + 1, :]
                        dm_ref[l, dev : dev + 1, col] = row
                        lat = lat + row
                    ctx = total(p, 1, slice(r, r + 1), slice(None))
                    dm_ref[l, 8:9, col] = ctx
                    gb_ref[l : l + 1, col] = lat + ctx
                gn_ref[l, sb : sb + 1, :] = total(p, 0, slice(3, 4), slice(None)) + total(p, 1, slice(3, 4), slice(None))
        gnf_ref[...] = total(nf_ref, slice(0, 1), slice(None))
        for l, (a, b, c) in enumerate(((dwp0, dsc0, dsk0), (dwp1, dsc1, dsk1))):
            gwp_ref[l] = total(a, slice(None), slice(None))
            gps_ref[l : l + 1, :] = total(b, slice(0, 1), slice(None))
            gsk_ref[l] = total(c, slice(None), slice(None))

    ins = flat + [nf, dwp[0], dwp[1], dsc[0], dsc[1], dsk[0], dsk[1]]
    return pl.pallas_call(
        body,
        name=name,
        out_shape=[_sds((2, 16, NMOD * D), F32), _sds((2, NMOD * D), F32), _sds((2, 8, D), F32), _sds((1, D), F32),
                   _sds((2, PW, 128), F32), _sds((2, PW), F32), _sds((2, 8, 128), F32)],
        compiler_params=pltpu.CompilerParams(vmem_limit_bytes=VMEM_LIMIT),
    )(*ins)


def _small_adamw(c_ctx, dc_all, triples, *, name):
    n = len(triples)

    def body(*refs):
        c_ref, dc_ref = refs[0], refs[1]
        ins = refs[2 : 2 + 4 * n - 1]
        outs = refs[2 + 4 * n - 1 :]
        acc = dc_ref[0, 0, 8:9, :] + dc_ref[0, 1, 8:9, :]
        for dev in range(1, NDEV):
            acc = acc + (dc_ref[dev, 0, 8:9, :] + dc_ref[dev, 1, 8:9, :])
        c = c_ref[...]
        sig = _sigmoid(c)
        g_c = acc * (sig * (1.0 + c * (1.0 - sig)))
        outs[0][...] = g_c
        pos = 0
        for k in range(n):
            if k == 0:
                w, g, m, v = ins[0][...], g_c, ins[1][...], ins[2][...]
                pos = 3
            else:
                w, g, m, v = (ins[pos + t][...] for t in range(4))
                pos += 4
            d, m2, v2 = _adamw_math(w, g, m, v)
            outs[1 + 3 * k][...], outs[2 + 3 * k][...], outs[3 + 3 * k][...] = d, m2, v2

    flat_in = [c_ctx, dc_all]
    out_shape = [_sds(c_ctx.shape, F32)]
    for k, (w, g, m, v) in enumerate(triples):
        flat_in += [w, m, v] if k == 0 else [w, g, m, v]
        out_shape += [_sds(w.shape, F32)] * 3
    return pl.pallas_call(body, name=name, out_shape=out_shape,
                          compiler_params=pltpu.CompilerParams(vmem_limit_bytes=VMEM_LIMIT))(*flat_in)


def _rope_tables(T, R):
    t = jnp.arange(T)
    inv = ROPE_BASE ** (-jnp.arange(0, HD // 2, 2, dtype=F32) / (HD // 2))
    ang = jnp.concatenate([(t // GRID_W).astype(F32)[:, None] * inv, (t % GRID_W).astype(F32)[:, None] * inv], axis=-1)
    cos = jnp.concatenate([jnp.tile(jnp.cos(ang), (1, 4)), jnp.ones((R - T, 128), F32)], axis=0)
    sin = jnp.concatenate([jnp.tile(jnp.sin(ang), (1, 4)), jnp.zeros((R - T, 128), F32)], axis=0)
    return cos, sin


def kernel(x, c, ctx, c_ctx, w_mod, b_mod, norm_ffn1, w_ffn1_in, w_ffn1_out, norm_mix, w_in, w_pool, pool_scale, sink, w_out, norm_ffn2, w_ffn2_in, w_ffn2_out, norm_final, loss_target, m_c_ctx, m_w_mod, m_b_mod, m_norm_ffn1, m_w_ffn1_in, m_w_ffn1_out, m_norm_mix, m_w_in, m_w_pool, m_pool_scale, m_sink, m_w_out, m_norm_ffn2, m_w_ffn2_in, m_w_ffn2_out, m_norm_final, v_c_ctx, v_w_mod, v_b_mod, v_norm_ffn1, v_w_ffn1_in, v_w_ffn1_out, v_norm_mix, v_w_in, v_w_pool, v_pool_scale, v_sink, v_w_out, v_norm_ffn2, v_w_ffn2_in, v_w_ffn2_out, v_norm_final):
    T = x.shape[1]
    R = T + LC
    nl = w_mod.shape[0]
    cx, cy, cc = _coords()
    me = _lin((cx, cy, cc))
    pos = jnp.stack([cc, 2 * cx + cy]).astype(jnp.int32)
    mcols = w_mod.shape[2]

    shards = [([w_ffn1_in[l].T.astype(BF16), w_ffn1_out[l].astype(BF16)],
               [w_in[l].T.astype(BF16), w_out[l].astype(BF16)],
               [w_ffn2_in[l].T.astype(BF16), w_ffn2_out[l].astype(BF16)]) for l in range(nl)]

    got = _exchange("ag_c_w", _merge(_gather_direct([c]), _gather_a(shards[0][0])))
    c_all, w_first = got[0], got[1:]
    c16 = jnp.concatenate([c_all.reshape(NDEV, D), c_ctx[None], jnp.zeros((16 - NDEV - 1, D), F32)], axis=0)
    b_cols = lax.dynamic_slice(b_mod, (0, me * mcols), (nl, mcols)).reshape(nl, 1, mcols)
    got = _exchange("ag_mod_w", _merge(_gather_b(w_first), _gather_direct([_mod_fwd(c16, w_mod, b_cols, name="mod_fwd")])))
    w_first, mod_all = got[:2], got[2]
    mod_all = jnp.transpose(mod_all, (1, 2, 0, 3)).reshape(nl, 16, NMOD, D)
    mine = lax.dynamic_index_in_dim(mod_all, me, axis=1, keepdims=False)
    pad = jnp.zeros((nl, 16 - NMOD, D), F32)
    modv = jnp.stack([jnp.concatenate([mine, pad], axis=1), jnp.concatenate([mod_all[:, 8], pad], axis=1)], axis=1)

    gvec = [jnp.concatenate([norm_ffn1[l][None], norm_mix[l][None], norm_ffn2[l][None], jnp.zeros((5, D), F32)], axis=0)
            for l in range(nl)]
    cos, sin = _rope_tables(T, R)
    ps2 = [pool_scale[l][None] for l in range(nl)]

    h = jnp.concatenate([x[0], ctx[0]], axis=0)
    loss_part, dh, packets, dnf, dwp, dsc, dsk, big = _forward_backward(
        h, loss_target[0], modv, gvec, shards, w_first, cos, sin, sink, w_pool, ps2, norm_final, pos, T=T)
    loss = lax.psum(loss_part[0, 0], ("x", "y", "c"))
    grad_x = dh[:T][None]

    flat = [p for layer in packets for p in layer]
    small = _all_gather("ag_small", flat + [dnf, dwp[0], dwp[1], dsc[0], dsc[1], dsk[0], dsk[1]])
    pk_all = [small[0:3], small[3:6]]
    dm, g_b_mod, g_norms, g_nf, g_wp, g_ps, g_sk = _small_sums(
        pk_all, small[6], small[7:9], small[9:11], small[11:13], name="small_sums")
    dm_cols = lax.dynamic_slice(dm, (0, 0, me * mcols), (nl, 16, mcols))
    g_w_mod, dc_part = _mod_bwd(c16, dm_cols, w_mod, name="mod_bwd")
    (dc_all,) = _exchange("ag_dc", _gather_direct([dc_part]))

    grads = {
        "b_mod": g_b_mod, "norm_ffn1": g_norms[:, 0], "norm_mix": g_norms[:, 1], "norm_ffn2": g_norms[:, 2],
        "w_pool": g_wp.reshape(w_pool.shape), "pool_scale": g_ps, "sink": g_sk[:, :, 0], "norm_final": g_nf.reshape(D),
        "w_mod": g_w_mod,
        "w_ffn1_in": jnp.stack([big[l][0].T for l in range(nl)]), "w_ffn1_out": jnp.stack([big[l][1] for l in range(nl)]),
        "w_in": jnp.stack([big[l][2].T for l in range(nl)]), "w_out": jnp.stack([big[l][3] for l in range(nl)]),
        "w_ffn2_in": jnp.stack([big[l][4].T for l in range(nl)]), "w_ffn2_out": jnp.stack([big[l][5] for l in range(nl)]),
    }
    weights = dict(c_ctx=c_ctx, w_mod=w_mod, b_mod=b_mod, norm_ffn1=norm_ffn1, w_ffn1_in=w_ffn1_in, w_ffn1_out=w_ffn1_out,
                   norm_mix=norm_mix, w_in=w_in, w_pool=w_pool, pool_scale=pool_scale, sink=sink, w_out=w_out,
                   norm_ffn2=norm_ffn2, w_ffn2_in=w_ffn2_in, w_ffn2_out=w_ffn2_out, norm_final=norm_final)
    moms = dict(c_ctx=(m_c_ctx, v_c_ctx), w_mod=(m_w_mod, v_w_mod), b_mod=(m_b_mod, v_b_mod),
                norm_ffn1=(m_norm_ffn1, v_norm_ffn1), w_ffn1_in=(m_w_ffn1_in, v_w_ffn1_in),
                w_ffn1_out=(m_w_ffn1_out, v_w_ffn1_out), norm_mix=(m_norm_mix, v_norm_mix), w_in=(m_w_in, v_w_in),
                w_pool=(m_w_pool, v_w_pool), pool_scale=(m_pool_scale, v_pool_scale), sink=(m_sink, v_sink),
                w_out=(m_w_out, v_w_out), norm_ffn2=(m_norm_ffn2, v_norm_ffn2), w_ffn2_in=(m_w_ffn2_in, v_w_ffn2_in),
                w_ffn2_out=(m_w_ffn2_out, v_w_ffn2_out), norm_final=(m_norm_final, v_norm_final))
    order = list(weights)
    small_names = ["c_ctx", "b_mod", "norm_ffn1", "norm_mix", "w_pool", "pool_scale", "sink", "norm_ffn2", "norm_final"]

    def as2d(name, t):
        if name == "w_pool":
            return t.reshape(-1, 128)
        return t.reshape(1, -1) if t.ndim == 1 else t

    triples = [(as2d(n, weights[n]), None if n == "c_ctx" else as2d(n, grads[n]), as2d(n, moms[n][0]), as2d(n, moms[n][1]))
               for n in small_names]
    outs = _small_adamw(as2d("c_ctx", c_ctx), dc_all, triples, name="small_adamw")
    grads["c_ctx"] = outs[0].reshape(c_ctx.shape)
    delta, new_m, new_v = {}, {}, {}
    for k, n in enumerate(small_names):
        delta[n], new_m[n], new_v[n] = (o.reshape(weights[n].shape) for o in outs[1 + 3 * k : 4 + 3 * k])
    for n in order:
        if n not in small_names:
            delta[n], new_m[n], new_v[n] = _adamw(weights[n], grads[n], moms[n][0], moms[n][1], name=f"adamw_{n}")

    return (loss, grad_x, *[grads[n] for n in order], *[delta[n] for n in order],
            *[new_m[n] for n in order], *[new_v[n] for n in order])


def _merge(*rounds):
    ins, outs, plan, local, n_alias = [], [], [], [], 0
    for r in rounds:
        assert r.n_alias == 0 or (not ins and r.n_alias == len(r.ins) == len(r.out_shapes))
        oi, oo = len(ins), len(outs)
        plan += [(k, i + oi, sf, o + oo, df) for k, i, sf, o, df in r.plan]
        local += [(i + oi, sf, o + oo, df) for i, sf, o, df in r.local_plan]
        ins += r.ins
        outs += r.out_shapes
        n_alias += r.n_alias
    return _Round(ins, outs, plan, local, n_alias)


def _forward_backward(h, target, modv, gvec, shards, w_first, cos, sin, sink, w_pool, ps2, norm_final, pos, *, T):
    nl = len(gvec)
    flat = lambda ws: [w.reshape(-1, D) for w in ws]
    saved = []
    w1 = flat(w_first)
    wm = w2 = None
    for l in range(nl):
        last = l == nl - 1
        h0 = h
        if l == 0:
            (h1, a1, b1, f1), got = _ffn_fwd(h0, modv[l], gvec[l], *w1, T=T, mrow=0, grow=0, ctx_active=True,
                                             name=f"ffn1_fwd_{l}", carry=_gather_a(shards[l][1] + shards[l][2]))
            got = _exchange(f"ag_rest_{l}", _gather_b(got))
            wm, w2 = flat(got[:2]), flat(got[2:])
        else:
            (h1, a1, b1, f1), _ = _ffn_fwd(h0, modv[l], gvec[l], *w1, T=T, mrow=0, grow=0, ctx_active=True, name=f"ffn1_fwd_{l}")
        u, q, k4, v4 = _mixproj_fwd(h1, modv[l], gvec[l], wm[0], cos, sin, T=T, name=f"mixproj_fwd_{l}")
        (cat,), got1 = _attnpool_fwd(u, q, k4, v4, sink[l], w_pool[l], ps2[l], T=T, name=f"attnpool_fwd_{l}",
                                     carry=None if last else _gather_a(shards[l + 1][0]))
        h2, mo = _mixout_fwd(h1, cat, modv[l], wm[1], T=T, ctx_active=not last, name=f"mixout_fwd_{l}")
        (h3, a2, b2, f2), got2 = _ffn_fwd(h2, modv[l], gvec[l], *w2, T=T, mrow=6, grow=2, ctx_active=not last,
                                          name=f"ffn2_fwd_{l}",
                                          carry=None if last else _gather_a(shards[l + 1][1] + shards[l + 1][2]))
        saved.append((h0, a1, b1, f1, h1, u, q, k4, v4, cat, mo, h2, a2, b2, f2, w1, wm, w2))
        h = h3
        if not last:
            got = _exchange(f"ag_next_{l + 1}", _gather_b(got1 + got2))
            w1, wm, w2 = flat(got[:2]), flat(got[2:4]), flat(got[4:])

    dh, loss_part, dnf = _loss_head(h, target, norm_final[None], T=T, name="loss_head")

    def chip_sums(tag, grads):
        got = _exchange(f"rs1_{tag}", _scatter_1(grads))
        return [_add_pairs(g, r, pos, name=f"rs_add_{tag}_{i}") for i, (g, r) in enumerate(zip(grads, got))]

    def totals(tag, chip, got):
        return [_sum_chips(c_, r, pos, name=f"rs_sum_{tag}_{i}") for i, (c_, r) in enumerate(zip(chip, got))]

    packets, dwp, dsc, dsk = [None] * nl, [None] * nl, [None] * nl, [None] * nl
    big = {}
    waiting = None
    for l in reversed(range(nl)):
        last = l == nl - 1
        h0, a1, b1, f1, h1, u, q, k4, v4, cat, mo, h2, a2, b2, f2, w1, wm, w2 = saved[l]
        (dh, dab, s, n, df, pk2), got = _ffn_bwd(h2, dh, a2, b2, f2, modv[l], gvec[l], *w2, T=T, mrow=6, grow=2,
                                                 ctx_active=not last, name=f"ffn2_bwd_{l}",
                                                 carry=_scatter_2(waiting[1]) if waiting else None)
        if waiting:
            big[l + 1][0:2] = totals(*waiting, got)
        g2 = chip_sums(f"ffn2_{l}", [_wgrad(dab, n, bk=FCH, sh=2 * DFF // NDEV, name=f"wgrad_ffn2_in_{l}"),
                                     _wgrad(s, df, bk=FCH, sh=DFF // NDEV, name=f"wgrad_ffn2_out_{l}")])
        dcat, dmix, pko = _mixout_bwd(dh, mo, modv[l], wm[1], T=T, ctx_active=not last, name=f"mixout_bwd_{l}")
        g_wo = _wgrad(cat, dmix, bk=D, sh=D // NDEV, name=f"wgrad_out_{l}")
        dps, dwp[l], dsc[l] = _pool_bwd(u, dcat, w_pool[l], ps2[l], T=T, name=f"pool_bwd_{l}")
        (du, dq, dk, dv, dsk[l]), got = _attn_bwd(q, k4, v4, dcat, dps, sink[l], T=T, name=f"attn_bwd_{l}", carry=_scatter_2(g2))
        big[l] = [None, None, None, None] + totals(f"ffn2_{l}", g2, got)
        dh, dproj, n, pkm = _mixproj_bwd(h1, dh, du, dq, dk, dv, modv[l], gvec[l], wm[0], cos, sin, T=T, name=f"mixproj_bwd_{l}")
        gm = chip_sums(f"mix_{l}", [_wgrad(dproj, n, bk=PROJ, sh=PROJ // NDEV, name=f"wgrad_in_{l}"), g_wo])
        (dh, dab, s, n, df, pk1), got = _ffn_bwd(h0, dh, a1, b1, f1, modv[l], gvec[l], *w1, T=T, mrow=0, grow=0,
                                                 ctx_active=True, name=f"ffn1_bwd_{l}", carry=_scatter_2(gm))
        big[l][2:4] = totals(f"mix_{l}", gm, got)
        waiting = (f"ffn1_{l}", chip_sums(f"ffn1_{l}", [_wgrad(dab, n, bk=FCH, sh=2 * DFF // NDEV, name=f"wgrad_ffn1_in_{l}"),
                                                       _wgrad(s, df, bk=FCH, sh=DFF // NDEV, name=f"wgrad_ffn1_out_{l}")]))
        packets[l] = [pk1, pkm + pko, pk2]
    big[0][0:2] = totals(*waiting, _exchange("rs2_" + waiting[0], _scatter_2(waiting[1])))
    return loss_part, dh, packets, dnf, dwp, dsc, dsk, big
```

```python
import functools

import jax
import jax.numpy as jnp
from jax import lax
from jax.experimental import pallas as pl
from jax.experimental.pallas import tpu as pltpu

F32, BF16 = jnp.float32, jnp.bfloat16

D = 1024
LC = 256
DFF = 2816
NMOD = 9
PW = 512
AW = 512
KVW = 128
PROJ = PW + AW + 2 * KVW
HD = 64
BLK = 128
GRID_W = 64
POOL_WINDOWS = (2, 4, 8, 16)
EPS = 1e-6
NEG = -1e30
ROPE_BASE = 10000.0
NDEV = 8
MESH = pl.DeviceIdType.MESH

ADAM_LR, ADAM_B1, ADAM_B2, ADAM_EPS, ADAM_WD, ADAM_STEP = 0.001, 0.9, 0.999, 1e-08, 0.01, 10

VMEM_LIMIT = 56 * 1024 * 1024
TM = 256
FCH = 1408

ANY = pl.BlockSpec(memory_space=pl.ANY)
SMEM = pl.BlockSpec(memory_space=pltpu.SMEM)


def _params(ngrid=1):
    return pltpu.CompilerParams(dimension_semantics=("arbitrary",) * ngrid, vmem_limit_bytes=VMEM_LIMIT)


def _dot(a, b):
    return jnp.dot(a, b, preferred_element_type=F32)


def _dot_nt(a, b):
    return lax.dot_general(a, b, (((1,), (1,)), ((), ())), preferred_element_type=F32)


def _dot_tn(a, b):
    return lax.dot_general(a, b, (((0,), (0,)), ((), ())), preferred_element_type=F32)


def _sigmoid(x):
    return 1.0 / (1.0 + jnp.exp(-x))


def _rows(tm, w):
    return pl.BlockSpec((tm, w), lambda i: (i, 0))


def _full(shape):
    nd = len(shape)
    return pl.BlockSpec(shape, lambda *_: (0,) * nd)


def _sds(shape, dtype):
    return jax.ShapeDtypeStruct(shape, dtype)


def _norm_mod(h, g, shift, scale):
    r = lax.rsqrt(jnp.mean(h * h, axis=-1, keepdims=True) + EPS)
    xhat = h * r
    y = xhat * g
    return r, xhat, y, y * (1.0 + scale) + shift


def _norm_mod_bwd(dn, r, xhat, y, g, scale):
    dshift = jnp.sum(dn, axis=0, keepdims=True)
    dscale = jnp.sum(dn * y, axis=0, keepdims=True)
    dy = dn * (1.0 + scale)
    dg = jnp.sum(dy * xhat, axis=0, keepdims=True)
    dxh = dy * g
    dh = r * (dxh - xhat * jnp.mean(dxh * xhat, axis=-1, keepdims=True))
    return dh, dshift, dscale, dg


def _acc_partials(part_ref, first, rows):
    @pl.when(first)
    def _():
        part_ref[...] = jnp.zeros_like(part_ref)

    for r, val in rows.items():
        part_ref[0, r : r + 1, :] += val


def _mod_spec(n_lat):
    return pl.BlockSpec((1, 16, D), lambda i: (i // n_lat, 0, 0))


def _part_spec(n_lat):
    return pl.BlockSpec((1, 8, D), lambda i: (i // n_lat, 0, 0))


def _load_weights(pairs, sem):
    copies = [pltpu.make_async_copy(src, dst, sem.at[k]) for k, (src, dst) in enumerate(pairs)]
    for cp in copies:
        cp.start()
    for cp in copies:
        cp.wait()


def _ffn_weight_copies(win_hbm, wout_hbm, win_v, wout_v, sem):
    loads = []
    for k, c0 in enumerate(range(0, DFF, FCH)):
        slabs = [(win_hbm, win_v, c0), (win_hbm, win_v, DFF + c0), (wout_hbm, wout_v, c0)]
        loads.append([pltpu.make_async_copy(src.at[pl.ds(r0, FCH)], dst.at[pl.ds(r0, FCH)], sem.at[3 * k + j])
                      for j, (src, dst, r0) in enumerate(slabs)])
    return loads


def _wait_first_step(i, copies):
    @pl.when(i == 0)
    def _():
        for cp in copies:
            cp.wait()


def _ffn_fwd(h, modv, gvec, win, wout, *, T, mrow, grow, ctx_active, name, carry=None):
    R = h.shape[0]
    n_lat, n_tiles = T // TM, R // TM

    def body(h_ref, mod_ref, g_ref, win_hbm, wout_hbm, ho_ref, a_ref, b_ref, f_ref, win_v, wout_v, sem):
        i = pl.program_id(0)

        loads = _ffn_weight_copies(win_hbm, wout_hbm, win_v, wout_v, sem)

        @pl.when(i == 0)
        def _():
            for cp in sum(loads, []):
                cp.start()

        def compute():
            h = h_ref[...]
            shift, scale, gate = (mod_ref[0, mrow + k : mrow + k + 1, :] for k in range(3))
            _, _, _, n = _norm_mod(h, g_ref[grow : grow + 1, :], shift, scale)
            n_bf = n.astype(BF16)
            acc = jnp.zeros((TM, D), F32)
            for c0 in range(0, DFF, FCH):
                _wait_first_step(i, loads[c0 // FCH])
                a = _dot_nt(n_bf, win_v[c0 : c0 + FCH, :])
                b = _dot_nt(n_bf, win_v[DFF + c0 : DFF + c0 + FCH, :])
                a_ref[:, c0 : c0 + FCH] = a.astype(BF16)
                b_ref[:, c0 : c0 + FCH] = b.astype(BF16)
                s = a * _sigmoid(a) * b
                acc = acc + _dot(s.astype(BF16), wout_v[c0 : c0 + FCH, :])
            f_ref[...] = acc.astype(BF16)
            ho_ref[...] = h + (0.5 * gate) * acc

        if ctx_active:
            compute()
        else:
            pl.when(i < n_lat)(compute)

            @pl.when(i >= n_lat)
            def _():
                ho_ref[...] = h_ref[...]
                a_ref[...] = jnp.zeros_like(a_ref)
                b_ref[...] = jnp.zeros_like(b_ref)
                f_ref[...] = jnp.zeros_like(f_ref)

    return _call(
        body,
        name=name,
        grid=(n_tiles,),
        in_specs=[_rows(TM, D), _mod_spec(n_lat), _full((8, D)), ANY, ANY],
        out_specs=[_rows(TM, D), _rows(TM, DFF), _rows(TM, DFF), _rows(TM, D)],
        out_shape=[_sds((R, D), F32), _sds((R, DFF), BF16), _sds((R, DFF), BF16), _sds((R, D), BF16)],
        scratch_shapes=[pltpu.VMEM((2 * DFF, D), BF16), pltpu.VMEM((DFF, D), BF16), pltpu.SemaphoreType.DMA((3 * DFF // FCH,))],
        args=(h, modv, gvec, win, wout),
        carry=carry,
    )


def _ffn_bwd(h, dho, a, b, f, modv, gvec, win, wout, *, T, mrow, grow, ctx_active, name, carry=None):
    R = h.shape[0]
    n_lat, n_tiles = T // TM, R // TM

    def body(h_ref, dho_ref, a_ref, b_ref, f_ref, mod_ref, g_ref, win_hbm, wout_hbm,
             dh_ref, dab_ref, s_ref, n_ref, df_ref, part_ref, win_v, wout_v, sem):
        i = pl.program_id(0)

        loads = _ffn_weight_copies(win_hbm, wout_hbm, win_v, wout_v, sem)

        @pl.when(i == 0)
        def _():
            for cp in sum(loads, []):
                cp.start()

        first = jnp.logical_or(i == 0, i == n_lat)

        def compute():
            h = h_ref[...]
            dho = dho_ref[...]
            shift, scale, gate = (mod_ref[0, mrow + k : mrow + k + 1, :] for k in range(3))
            g = g_ref[grow : grow + 1, :]
            r, xhat, y, n = _norm_mod(h, g, shift, scale)
            dgate = 0.5 * jnp.sum(dho * f_ref[...].astype(F32), axis=0, keepdims=True)
            df_bf = ((0.5 * gate) * dho).astype(BF16)
            df_ref[...] = df_bf
            n_ref[...] = n.astype(BF16)
            dn = jnp.zeros((TM, D), F32)
            for c0 in range(0, DFF, FCH):
                _wait_first_step(i, loads[c0 // FCH])
                ds = _dot_nt(df_bf, wout_v[c0 : c0 + FCH, :])
                av = a_ref[:, c0 : c0 + FCH].astype(F32)
                bv = b_ref[:, c0 : c0 + FCH].astype(F32)
                sig = _sigmoid(av)
                sa = av * sig
                s_ref[:, c0 : c0 + FCH] = (sa * bv).astype(BF16)
                da = (ds * bv * (sig * (1.0 + av * (1.0 - sig)))).astype(BF16)
                db = (ds * sa).astype(BF16)
                dab_ref[:, c0 : c0 + FCH] = da
                dab_ref[:, DFF + c0 : DFF + c0 + FCH] = db
                dn = dn + _dot(da, win_v[c0 : c0 + FCH, :]) + _dot(db, win_v[DFF + c0 : DFF + c0 + FCH, :])
            dh, dshift, dscale, dg = _norm_mod_bwd(dn, r, xhat, y, g, scale)
            dh_ref[...] = dho + dh
            _acc_partials(part_ref, first, {0: dshift, 1: dscale, 2: dgate, 3: dg})

        if ctx_active:
            compute()
        else:
            pl.when(i < n_lat)(compute)

            @pl.when(i >= n_lat)
            def _():
                dh_ref[...] = dho_ref[...]
                dab_ref[...] = jnp.zeros_like(dab_ref)
                s_ref[...] = jnp.zeros_like(s_ref)
                n_ref[...] = jnp.zeros_like(n_ref)
                df_ref[...] = jnp.zeros_like(df_ref)
                part_ref[...] = jnp.zeros_like(part_ref)

    return _call(
        body,
        name=name,
        grid=(n_tiles,),
        in_specs=[_rows(TM, D), _rows(TM, D), _rows(TM, DFF), _rows(TM, DFF), _rows(TM, D),
                  _mod_spec(n_lat), _full((8, D)), ANY, ANY],
        out_specs=[_rows(TM, D), _rows(TM, 2 * DFF), _rows(TM, DFF), _rows(TM, D), _rows(TM, D), _part_spec(n_lat)],
        out_shape=[_sds((R, D), F32), _sds((R, 2 * DFF), BF16), _sds((R, DFF), BF16), _sds((R, D), BF16),
                   _sds((R, D), BF16), _sds((2, 8, D), F32)],
        scratch_shapes=[pltpu.VMEM((2 * DFF, D), BF16), pltpu.VMEM((DFF, D), BF16), pltpu.SemaphoreType.DMA((3 * DFF // FCH,))],
        args=(h, dho, a, b, f, modv, gvec, win, wout),
        carry=carry,
    )


def _wgrad(x, y, *, bk, sh, name, carry=None):
    R, kx = x.shape
    n = y.shape[1]
    tr = R // 2
    nr, nsh = R // tr, bk // sh

    def body(x_ref, y_ref, o_ref, acc):
        r = pl.program_id(1)

        @pl.when(r == 0)
        def _():
            acc[...] = jnp.zeros_like(acc)

        acc[...] += _dot_tn(x_ref[...], y_ref[...])

        @pl.when(r == nr - 1)
        def _():
            for s in range(nsh):
                o_ref[s] = acc[s * sh : (s + 1) * sh, :].astype(BF16)

    (out,), got = _call(
        body,
        name=name,
        grid=(kx // bk, nr),
        in_specs=[pl.BlockSpec((tr, bk), lambda k, r: (r, k)), pl.BlockSpec((tr, n), lambda k, r: (r, 0))],
        out_specs=[pl.BlockSpec((nsh, sh, n), lambda k, r: (k, 0, 0))],
        out_shape=[_sds((kx // sh, sh, n), BF16)],
        scratch_shapes=[pltpu.VMEM((bk, n), F32)],
        args=(x, y),
        carry=carry,
    )
    return out, got


def _rot_half(x):
    lane = lax.broadcasted_iota(jnp.int32, x.shape, 1)
    return jnp.where((lane & (HD - 1)) < HD // 2, -pltpu.roll(x, 128 - HD // 2, 1), pltpu.roll(x, HD // 2, 1))


def _tile_sel():
    i = lax.broadcasted_iota(jnp.int32, (KVW, AW), 0)
    j = lax.broadcasted_iota(jnp.int32, (KVW, AW), 1)
    return jnp.where(i == (j // 256) * HD + (j & (HD - 1)), 1.0, 0.0).astype(BF16)


def _mixproj_fwd(h, modv, gvec, win, cos, sin, *, T, name, carry=None):
    R = h.shape[0]
    n_lat, n_tiles = T // TM, R // TM

    def body(h_ref, mod_ref, g_ref, win_ref, cos_ref, sin_ref, u_ref, q_ref, k4_ref, v4_ref):
        shift, scale = mod_ref[0, 3:4, :], mod_ref[0, 4:5, :]
        _, _, _, n = _norm_mod(h_ref[...], g_ref[1:2, :], shift, scale)
        proj = _dot_nt(n.astype(BF16), win_ref[...])
        u_ref[...] = proj[:, :PW]
        cs, sn = cos_ref[...], sin_ref[...]
        for s in range(AW // 128):
            x = proj[:, PW + 128 * s : PW + 128 * (s + 1)]
            q_ref[:, 128 * s : 128 * (s + 1)] = (x * cs + _rot_half(x) * sn).astype(BF16)
        k = proj[:, PW + AW : PW + AW + KVW]
        k = (k * cs + _rot_half(k) * sn).astype(BF16)
        v = proj[:, PW + AW + KVW :].astype(BF16)
        sel = _tile_sel()
        k4_ref[...] = _dot(k, sel).astype(BF16)
        v4_ref[...] = _dot(v, sel).astype(BF16)

    return _call(
        body,
        name=name,
        grid=(n_tiles,),
        in_specs=[_rows(TM, D), _mod_spec(n_lat), _full((8, D)), _full((PROJ, D)), _rows(TM, 128), _rows(TM, 128)],
        out_specs=[_rows(TM, PW), _rows(TM, AW), _rows(TM, AW), _rows(TM, AW)],
        out_shape=[_sds((R, PW), F32), _sds((R, AW), BF16), _sds((R, AW), BF16), _sds((R, AW), BF16)],
        scratch_shapes=[],
        args=(h, modv, gvec, win, cos, sin),
        carry=carry,
    )


def _win_start(j, hi):
    return pl.multiple_of(jnp.clip((j - 1) * BLK, 0, hi - 3 * BLK), BLK)


def _hi_lo(x):
    hi = x.astype(BF16)
    return hi, (x - hi.astype(F32)).astype(BF16)


def _pool_bounds(t, w, T, R):
    is_ctx = t >= T
    lo = jnp.maximum(t - w // 2, jnp.where(is_ctx, T, 0))
    hi = jnp.minimum(t + w // 2, jnp.where(is_ctx, R, T))
    return lo, hi


def _pooled(u_v, j, T, R):
    start = _win_start(j, R)
    u3_hi, u3_lo = _hi_lo(u_v[pl.ds(start, 3 * BLK), :])
    ub = u_v[pl.ds(pl.multiple_of(j * BLK, BLK), BLK), :]
    t = j * BLK + lax.broadcasted_iota(jnp.int32, (BLK, 1), 0)
    pos = start + lax.broadcasted_iota(jnp.int32, (1, 3 * BLK), 1)
    pooled, counts = [], []
    for g, w in enumerate(POOL_WINDOWS):
        lo, hi = _pool_bounds(t, w, T, R)
        band = jnp.where(pos >= lo, jnp.where(pos < hi, 1.0, 0.0), 0.0).astype(BF16)
        sl = slice(g * 128, (g + 1) * 128)
        sums = _dot(band, u3_hi[:, sl]) + _dot(band, u3_lo[:, sl])
        cnt = (hi - lo).astype(F32)
        pooled.append(sums / cnt - ub[:, sl])
        counts.append(cnt)
    return pooled, counts


def _stack_heads(x):
    lane_h = lax.broadcasted_iota(jnp.int32, x.shape, 1) // HD
    return jnp.concatenate([jnp.where(lane_h == h, x, jnp.zeros_like(x)) for h in range(4)], axis=0)


def _unstack_heads(x):
    lane_h = lax.broadcasted_iota(jnp.int32, (BLK, 256), 1) // HD
    out = jnp.zeros((BLK, 256), F32)
    for h in range(4):
        out = out + jnp.where(lane_h == h, x[h * BLK : (h + 1) * BLK, :], 0.0)
    return out


def _attn_probs(qs, kl, kc, sink_ref, g, j, start_l, nbl):
    s_l = _dot_nt(qs, kl) * (HD ** -0.5)
    s_c = _dot_nt(qs, kc) * (HD ** -0.5)
    rowi = lax.broadcasted_iota(jnp.int32, (4 * BLK, 1), 0)
    qpos = j * BLK + (rowi & (BLK - 1))
    kpos = start_l + lax.broadcasted_iota(jnp.int32, (1, 3 * BLK), 1)
    reach = jnp.where(j < nbl, BLK, -1)
    s_l = jnp.where(jnp.abs(kpos - qpos) <= reach, s_l, NEG)
    rb = rowi // BLK
    sk = jnp.where(rb == 0, sink_ref[4 * g], jnp.where(rb == 1, sink_ref[4 * g + 1],
                   jnp.where(rb == 2, sink_ref[4 * g + 2], sink_ref[4 * g + 3])))
    m = jnp.maximum(jnp.maximum(jnp.max(s_l, axis=1, keepdims=True), jnp.max(s_c, axis=1, keepdims=True)), sk)
    e_l, e_c, e_s = jnp.exp(s_l - m), jnp.exp(s_c - m), jnp.exp(sk - m)
    inv = 1.0 / (jnp.sum(e_l, axis=1, keepdims=True) + jnp.sum(e_c, axis=1, keepdims=True) + e_s)
    return e_l * inv, e_c * inv, e_s * inv


def _attnpool_fwd(u, q, k4, v4, sink, w_pool, pool_scale, *, T, name, carry=None):
    R = u.shape[0]
    nb, nbl = R // BLK, T // BLK

    def body(q_ref, sink_ref, wp_ref, ps_ref, u_hbm, k4_hbm, v4_hbm, cat_ref, u_v, k4_v, v4_v, sem):
        j = pl.program_id(0)

        @pl.when(j == 0)
        def _():
            _load_weights([(u_hbm, u_v), (k4_hbm, k4_v), (v4_hbm, v4_v)], sem)

        pooled, _ = _pooled(u_v, j, T, R)
        for g in range(4):
            mixed = _dot(pooled[g].astype(BF16), wp_ref[g].astype(BF16)) * ps_ref[:, g * 128 : (g + 1) * 128]
            cat_ref[:, g * 128 : (g + 1) * 128] = mixed.astype(BF16)

        start_l = _win_start(j, T)
        for g in range(2):
            gl = slice(g * 256, (g + 1) * 256)
            qs = _stack_heads(q_ref[:, gl])
            p_l, p_c, _ = _attn_probs(qs, k4_v[pl.ds(start_l, 3 * BLK), gl], k4_v[T:R, gl], sink_ref, g, j, start_l, nbl)
            o = _dot(p_l.astype(BF16), v4_v[pl.ds(start_l, 3 * BLK), gl]) + _dot(p_c.astype(BF16), v4_v[T:R, gl])
            cat_ref[:, PW + g * 256 : PW + (g + 1) * 256] = _unstack_heads(o).astype(BF16)

    return _call(
        body,
        name=name,
        grid=(nb,),
        in_specs=[_rows(BLK, AW), SMEM, _full((4, 128, 128)), _full((1, PW)), ANY, ANY, ANY],
        out_specs=[_rows(BLK, D)],
        out_shape=[_sds((R, D), BF16)],
        scratch_shapes=[pltpu.VMEM((R, PW), F32), pltpu.VMEM((R, AW), BF16), pltpu.VMEM((R, AW), BF16),
                        pltpu.SemaphoreType.DMA((3,))],
        args=(q, sink, w_pool, pool_scale, u, k4, v4),
        carry=carry,
    )


def _mixout_fwd(h, cat, modv, wout, *, T, ctx_active, name, carry=None):
    R = h.shape[0]
    n_lat, n_tiles = T // TM, R // TM

    def body(h_ref, cat_ref, mod_ref, w_ref, ho_ref, mo_ref):
        i = pl.program_id(0)

        def compute():
            mo = _dot(cat_ref[...], w_ref[...])
            mo_ref[...] = mo.astype(BF16)
            ho_ref[...] = h_ref[...] + mod_ref[0, 5:6, :] * mo

        if ctx_active:
            compute()
        else:
            pl.when(i < n_lat)(compute)

            @pl.when(i >= n_lat)
            def _():
                ho_ref[...] = h_ref[...]
                mo_ref[...] = jnp.zeros_like(mo_ref)

    return _call(
        body,
        name=name,
        grid=(n_tiles,),
        in_specs=[_rows(TM, D), _rows(TM, D), _mod_spec(n_lat), _full((D, D))],
        out_specs=[_rows(TM, D), _rows(TM, D)],
        out_shape=[_sds((R, D), F32), _sds((R, D), BF16)],
        scratch_shapes=[],
        args=(h, cat, modv, wout),
        carry=carry,
    )


def _mixout_bwd(dho, mo, modv, wout, *, T, ctx_active, name, carry=None):
    R = dho.shape[0]
    n_lat, n_tiles = T // TM, R // TM

    def body(dho_ref, mo_ref, mod_ref, w_ref, dcat_ref, dmix_ref, part_ref):
        i = pl.program_id(0)
        first = jnp.logical_or(i == 0, i == n_lat)

        def compute():
            dho = dho_ref[...]
            dmix = (mod_ref[0, 5:6, :] * dho).astype(BF16)
            dmix_ref[...] = dmix
            dcat_ref[...] = _dot_nt(dmix, w_ref[...])
            dgate = jnp.sum(dho * mo_ref[...].astype(F32), axis=0, keepdims=True)
            _acc_partials(part_ref, first, {2: dgate})

        if ctx_active:
            compute()
        else:
            pl.when(i < n_lat)(compute)

            @pl.when(i >= n_lat)
            def _():
                dcat_ref[...] = jnp.zeros_like(dcat_ref)
                dmix_ref[...] = jnp.zeros_like(dmix_ref)
                part_ref[...] = jnp.zeros_like(part_ref)

    return _call(
        body,
        name=name,
        grid=(n_tiles,),
        in_specs=[_rows(TM, D), _rows(TM, D), _mod_spec(n_lat), _full((D, D))],
        out_specs=[_rows(TM, D), _rows(TM, D), _part_spec(n_lat)],
        out_shape=[_sds((R, D), F32), _sds((R, D), BF16), _sds((2, 8, D), F32)],
        scratch_shapes=[],
        args=(dho, mo, modv, wout),
        carry=carry,
    )


def _pool_bwd(u, dcat, w_pool, pool_scale, *, T, name):
    R = u.shape[0]
    nb = R // BLK

    def body(dcat_ref, wp_ref, ps_ref, u_hbm, dps_ref, dwp_ref, dsc_ref, u_v, sem):
        j = pl.program_id(0)

        @pl.when(j == 0)
        def _():
            _load_weights([(u_hbm, u_v)], sem)
            dwp_ref[...] = jnp.zeros_like(dwp_ref)
            dsc_ref[...] = jnp.zeros_like(dsc_ref)

        pooled, counts = _pooled(u_v, j, T, R)
        for g in range(4):
            sl = slice(g * 128, (g + 1) * 128)
            p_bf = pooled[g].astype(BF16)
            w_bf = wp_ref[g].astype(BF16)
            dmixed = dcat_ref[:, sl]
            dsc_ref[0:1, sl] += jnp.sum(dmixed * _dot(p_bf, w_bf), axis=0, keepdims=True)
            dmp = (dmixed * ps_ref[:, sl]).astype(BF16)
            dwp_ref[sl, :] += _dot_tn(p_bf, dmp)
            dps_ref[:, sl] = _dot_nt(dmp, w_bf) / counts[g]

    return pl.pallas_call(
        body,
        name=name,
        grid=(nb,),
        in_specs=[_rows(BLK, D), _full((4, 128, 128)), _full((1, PW)), ANY],
        out_specs=[_rows(BLK, PW), _full((PW, 128)), _full((8, PW))],
        out_shape=[_sds((R, PW), F32), _sds((PW, 128), F32), _sds((8, PW), F32)],
        scratch_shapes=[pltpu.VMEM((R, PW), F32), pltpu.SemaphoreType.DMA((1,))],
        compiler_params=_params(),
    )(dcat, w_pool, pool_scale, u)


def _fold_heads(x):
    y = x[:, :128] + x[:, 128:]
    return y + pltpu.roll(y, HD, 1)


def _attn_bwd(q, k4, v4, dcat, dps, sink, *, T, name, carry=None):
    R = q.shape[0]
    nb, nbl = R // BLK, T // BLK

    def body(q_ref, dcat_ref, sink_ref, k4_hbm, v4_hbm, dps_hbm, du_ref, dq_ref, dk_ref, dv_ref, dsk_ref,
             k4_v, v4_v, dps_v, sem):
        j = pl.program_id(0)

        @pl.when(j == 0)
        def _():
            _load_weights([(k4_hbm, k4_v), (v4_hbm, v4_v), (dps_hbm, dps_v)], sem)
            dk_ref[...] = jnp.zeros_like(dk_ref)
            dv_ref[...] = jnp.zeros_like(dv_ref)
            dsk_ref[...] = jnp.zeros_like(dsk_ref)

        start = _win_start(j, R)
        d3_hi, d3_lo = _hi_lo(dps_v[pl.ds(start, 3 * BLK), :])
        db = dps_v[pl.ds(pl.multiple_of(j * BLK, BLK), BLK), :]
        pos = j * BLK + lax.broadcasted_iota(jnp.int32, (BLK, 1), 0)
        t_r = start + lax.broadcasted_iota(jnp.int32, (1, 3 * BLK), 1)
        for g, w in enumerate(POOL_WINDOWS):
            sl = slice(g * 128, (g + 1) * 128)
            lo_r, hi_r = _pool_bounds(t_r, w, T, R)
            band_t = jnp.where(pos >= lo_r, jnp.where(pos < hi_r, 1.0, 0.0), 0.0).astype(BF16)
            lo_c, hi_c = _pool_bounds(pos, w, T, R)
            du_ref[:, sl] = _dot(band_t, d3_hi[:, sl]) + _dot(band_t, d3_lo[:, sl]) - db[:, sl] * (hi_c - lo_c).astype(F32)

        start_l = _win_start(j, T)
        rb = lax.broadcasted_iota(jnp.int32, (4 * BLK, 1), 0) // BLK
        lane = lax.broadcasted_iota(jnp.int32, (1, 128), 1)
        dk_l, dk_c, dv_l, dv_c = [], [], [], []
        for g in range(2):
            gl = slice(g * 256, (g + 1) * 256)
            qs = _stack_heads(q_ref[:, gl])
            kl, kc = k4_v[pl.ds(start_l, 3 * BLK), gl], k4_v[T:R, gl]
            vl, vc = v4_v[pl.ds(start_l, 3 * BLK), gl], v4_v[T:R, gl]
            p_l, p_c, p_s = _attn_probs(qs, kl, kc, sink_ref, g, j, start_l, nbl)
            dos = _stack_heads(dcat_ref[:, PW + g * 256 : PW + (g + 1) * 256]).astype(BF16)
            dp_l, dp_c = _dot_nt(dos, vl), _dot_nt(dos, vc)
            delta = jnp.sum(p_l * dp_l, axis=1, keepdims=True) + jnp.sum(p_c * dp_c, axis=1, keepdims=True)
            ds_l = (p_l * (dp_l - delta) * (HD ** -0.5)).astype(BF16)
            ds_c = (p_c * (dp_c - delta) * (HD ** -0.5)).astype(BF16)
            dq_ref[:, gl] = _unstack_heads(_dot(ds_l, kl) + _dot(ds_c, kc))
            dk_l.append(_fold_heads(_dot_tn(ds_l, qs)))
            dk_c.append(_fold_heads(_dot_tn(ds_c, qs)))
            dv_l.append(_fold_heads(_dot_tn(p_l.astype(BF16), dos)))
            dv_c.append(_fold_heads(_dot_tn(p_c.astype(BF16), dos)))
            dsink = -p_s * delta
            for h in range(4):
                tot = jnp.sum(jnp.where(rb == h, dsink, 0.0), axis=0, keepdims=True)
                dsk_ref[4 * g + h : 4 * g + h + 1, :] += jnp.broadcast_to(tot, (1, 128))
        first = lane < HD
        dk_ref[pl.ds(start_l, 3 * BLK), :] += jnp.where(first, dk_l[0], dk_l[1])
        dk_ref[T:R, :] += jnp.where(first, dk_c[0], dk_c[1])
        dv_ref[pl.ds(start_l, 3 * BLK), :] += jnp.where(first, dv_l[0], dv_l[1])
        dv_ref[T:R, :] += jnp.where(first, dv_c[0], dv_c[1])

    return _call(
        body,
        name=name,
        grid=(nb,),
        in_specs=[_rows(BLK, AW), _rows(BLK, D), SMEM, ANY, ANY, ANY],
        out_specs=[_rows(BLK, PW), _rows(BLK, AW), _full((R, KVW)), _full((R, KVW)), _full((8, 128))],
        out_shape=[_sds((R, PW), F32), _sds((R, AW), F32), _sds((R, KVW), F32), _sds((R, KVW), F32),
                   _sds((8, 128), F32)],
        scratch_shapes=[pltpu.VMEM((R, AW), BF16), pltpu.VMEM((R, AW), BF16), pltpu.VMEM((R, PW), F32),
                        pltpu.SemaphoreType.DMA((3,))],
        args=(q, dcat, sink, k4, v4, dps),
        carry=carry,
    )


def _mixproj_bwd(h, dho, du, dq, dk, dv, modv, gvec, win, cos, sin, *, T, name):
    R = h.shape[0]
    n_lat, n_tiles = T // TM, R // TM

    def body(h_ref, dho_ref, du_ref, dq_ref, dk_ref, dv_ref, mod_ref, g_ref, win_ref, cos_ref, sin_ref,
             dh_ref, dproj_ref, n_ref, part_ref):
        i = pl.program_id(0)
        first = jnp.logical_or(i == 0, i == n_lat)
        shift, scale = mod_ref[0, 3:4, :], mod_ref[0, 4:5, :]
        g = g_ref[1:2, :]
        r, xhat, y, n = _norm_mod(h_ref[...], g, shift, scale)
        n_ref[...] = n.astype(BF16)
        cs, sn = cos_ref[...], sin_ref[...]
        dproj_ref[:, :PW] = du_ref[...].astype(BF16)
        for s in range(AW // 128):
            x = dq_ref[:, 128 * s : 128 * (s + 1)]
            dproj_ref[:, PW + 128 * s : PW + 128 * (s + 1)] = (x * cs - _rot_half(x) * sn).astype(BF16)
        x = dk_ref[...]
        dproj_ref[:, PW + AW : PW + AW + KVW] = (x * cs - _rot_half(x) * sn).astype(BF16)
        dproj_ref[:, PW + AW + KVW :] = dv_ref[...].astype(BF16)
        dn = _dot(dproj_ref[...], win_ref[...])
        dh, dshift, dscale, dg = _norm_mod_bwd(dn, r, xhat, y, g, scale)
        dh_ref[...] = dho_ref[...] + dh
        _acc_partials(part_ref, first, {0: dshift, 1: dscale, 3: dg})

    return pl.pallas_call(
        body,
        name=name,
        grid=(n_tiles,),
        in_specs=[_rows(TM, D), _rows(TM, D), _rows(TM, PW), _rows(TM, AW), _rows(TM, KVW), _rows(TM, KVW),
                  _mod_spec(n_lat), _full((8, D)), _full((PROJ, D)), _rows(TM, 128), _rows(TM, 128)],
        out_specs=[_rows(TM, D), _rows(TM, PROJ), _rows(TM, D), _part_spec(n_lat)],
        out_shape=[_sds((R, D), F32), _sds((R, PROJ), BF16), _sds((R, D), BF16), _sds((2, 8, D), F32)],
        compiler_params=_params(),
    )(h, dho, du, dq, dk, dv, modv, gvec, win, cos, sin)


def _loss_head(h, target, g_final, *, T, name):
    R = h.shape[0]
    n_lat, n_tiles = T // TM, R // TM

    def body(h_ref, t_ref, g_ref, dh_ref, loss_ref, dg_ref):
        i = pl.program_id(0)

        @pl.when(i == 0)
        def _():
            loss_ref[...] = jnp.zeros_like(loss_ref)
            dg_ref[...] = jnp.zeros_like(dg_ref)

        @pl.when(i < n_lat)
        def _():
            h = h_ref[...]
            g = g_ref[...]
            r = lax.rsqrt(jnp.mean(h * h, axis=-1, keepdims=True) + EPS)
            xhat = h * r
            err = xhat * g - t_ref[...]
            tot = jnp.sum(jnp.sum(err * err, axis=1, keepdims=True), axis=0, keepdims=True)
            loss_ref[...] += jnp.broadcast_to(tot * (0.5 / D), loss_ref.shape)
            dy = err * (1.0 / D)
            dg_ref[0:1, :] += jnp.sum(dy * xhat, axis=0, keepdims=True)
            dxh = dy * g
            dh_ref[...] = r * (dxh - xhat * jnp.mean(dxh * xhat, axis=-1, keepdims=True))

        @pl.when(i >= n_lat)
        def _():
            dh_ref[...] = jnp.zeros_like(dh_ref)

    return pl.pallas_call(
        body,
        name=name,
        grid=(n_tiles,),
        in_specs=[_rows(TM, D), pl.BlockSpec((TM, D), lambda i: (jnp.minimum(i, n_lat - 1), 0)), _full((1, D))],
        out_specs=[_rows(TM, D), _full((8, 128)), _full((8, D))],
        out_shape=[_sds((R, D), F32), _sds((8, 128), F32), _sds((8, D), F32)],
        compiler_params=_params(),
    )(h, target, g_final)


def _mod_fwd(c16, w_mod, b_cols, *, name):
    nl, _, cols = w_mod.shape

    def body(c_ref, w_ref, b_ref, o_ref):
        c = c_ref[...]
        sc = (c * _sigmoid(c)).astype(BF16)
        o_ref[0] = _dot(sc, w_ref[0].astype(BF16)) + b_ref[0]

    return pl.pallas_call(
        body,
        name=name,
        grid=(nl,),
        in_specs=[_full((16, D)), pl.BlockSpec((1, D, cols), lambda l: (l, 0, 0)),
                  pl.BlockSpec((1, 1, cols), lambda l: (l, 0, 0))],
        out_specs=pl.BlockSpec((1, 16, cols), lambda l: (l, 0, 0)),
        out_shape=_sds((nl, 16, cols), F32),
        compiler_params=_params(),
    )(c16, w_mod, b_cols)


def _mod_bwd(c16, dm_cols, w_mod, *, name):
    nl, _, cols = w_mod.shape

    def body(c_ref, dm_ref, w_ref, gw_ref, dc_ref):
        c = c_ref[...]
        sc = (c * _sigmoid(c)).astype(BF16)
        dm = dm_ref[0].astype(BF16)
        gw_ref[0] = _dot_tn(sc, dm)
        dc_ref[0] = _dot_nt(dm, w_ref[0].astype(BF16))

    return pl.pallas_call(
        body,
        name=name,
        grid=(nl,),
        in_specs=[_full((16, D)), pl.BlockSpec((1, 16, cols), lambda l: (l, 0, 0)),
                  pl.BlockSpec((1, D, cols), lambda l: (l, 0, 0))],
        out_specs=[pl.BlockSpec((1, D, cols), lambda l: (l, 0, 0)), pl.BlockSpec((1, 16, D), lambda l: (l, 0, 0))],
        out_shape=[_sds((nl, D, cols), F32), _sds((nl, 16, D), F32)],
        compiler_params=_params(),
    )(c16, dm_cols, w_mod)


def _coords():
    return lax.axis_index("x"), lax.axis_index("y"), lax.axis_index("c")


def _peer(k, x, y, c):
    return (1 - x if k & 4 else x, 1 - y if k & 2 else y, 1 - c if k & 1 else c)


def _lin(p):
    return 4 * p[0] + 2 * p[1] + p[2]


def _view(ref, slot):
    return ref if slot is None else ref.at[slot]


class _Round:
    def __init__(self, ins, out_shapes, plan, local_plan=(), n_alias=0):
        self.ins, self.out_shapes = list(ins), list(out_shapes)
        self.plan, self.local_plan, self.n_alias = list(plan), list(local_plan), n_alias

    def sems(self):
        return [pltpu.SemaphoreType.DMA((len(self.plan),)), pltpu.SemaphoreType.DMA((len(self.plan),)),
                pltpu.SemaphoreType.DMA((max(len(self.local_plan), 1),))]

    def _copies(self, in_refs, out_refs, sems, incoming):
        in_refs = list(out_refs[: self.n_alias]) + list(in_refs[self.n_alias :])
        send_sems, recv_sems, loc_sems = sems
        x, y, c = _coords()
        me = _lin((x, y, c))
        remote = []
        for idx, (k, ii, sfn, oi, dfn) in enumerate(self.plan):
            peer = _peer(k, x, y, c)
            sender, receiver = (_lin(peer), me) if incoming else (me, _lin(peer))
            remote.append(pltpu.make_async_remote_copy(
                src_ref=_view(in_refs[ii], sfn(sender, receiver)), dst_ref=_view(out_refs[oi], dfn(sender, receiver)),
                send_sem=send_sems.at[idx], recv_sem=recv_sems.at[idx], device_id=peer, device_id_type=MESH))
        locs = [pltpu.make_async_copy(_view(in_refs[ii], sfn(me)), _view(out_refs[oi], dfn(me)), loc_sems.at[idx])
                for idx, (ii, sfn, oi, dfn) in enumerate(() if incoming else self.local_plan)]
        return remote, locs

    def start(self, in_refs, out_refs, sems):
        sends, locs = self._copies(in_refs, out_refs, sems, incoming=False)
        for cp in sends + locs:
            cp.start()

    def finish(self, in_refs, out_refs, sems):
        for cp in self._copies(in_refs, out_refs, sems, incoming=True)[0]:
            cp.wait_recv()
        sends, locs = self._copies(in_refs, out_refs, sems, incoming=False)
        for cp in sends:
            cp.wait_send()
        for cp in locs:
            cp.wait()


def _exchange(name, rnd):
    n_in, n_out = len(rnd.ins), len(rnd.out_shapes)

    def body(*refs):
        in_refs, out_refs, sems = refs[:n_in], refs[n_in : n_in + n_out], refs[n_in + n_out :]
        rnd.start(in_refs, out_refs, sems)
        rnd.finish(in_refs, out_refs, sems)

    return pl.pallas_call(
        body, name=name, in_specs=[ANY] * n_in, out_specs=[ANY] * n_out, out_shape=rnd.out_shapes,
        scratch_shapes=rnd.sems(), input_output_aliases={i: i for i in range(rnd.n_alias)})(*rnd.ins)


def _call(body, *, name, grid, in_specs, out_specs, out_shape, scratch_shapes, args, carry=None):
    params = _params(len(grid))
    if carry is None:
        outs = pl.pallas_call(body, name=name, grid=grid, in_specs=in_specs, out_specs=out_specs, out_shape=out_shape,
                              scratch_shapes=scratch_shapes, compiler_params=params)(*args)
        return list(outs), []
    n_ci, n_co, n_cs = len(in_specs), len(out_shape), len(scratch_shapes)
    n_xi, n_xo = len(carry.ins), len(carry.out_shapes)

    def wrapped(*refs):
        ci, xi = refs[:n_ci], refs[n_ci : n_ci + n_xi]
        o0 = n_ci + n_xi
        co, xo = refs[o0 : o0 + n_co], refs[o0 + n_co : o0 + n_co + n_xo]
        s0 = o0 + n_co + n_xo
        cs, sems = refs[s0 : s0 + n_cs], refs[s0 + n_cs :]
        ids = [pl.program_id(a) for a in range(len(grid))]
        first = functools.reduce(jnp.logical_and, [i == 0 for i in ids])
        last = functools.reduce(jnp.logical_and, [i == g - 1 for i, g in zip(ids, grid)])

        @pl.when(first)
        def _():
            carry.start(xi, xo, sems)

        body(*ci, *co, *cs)

        @pl.when(last)
        def _():
            carry.finish(xi, xo, sems)

    outs = pl.pallas_call(
        wrapped, name=name, grid=grid, in_specs=list(in_specs) + [ANY] * n_xi, out_specs=list(out_specs) + [ANY] * n_xo,
        out_shape=list(out_shape) + carry.out_shapes, scratch_shapes=list(scratch_shapes) + carry.sems(),
        input_output_aliases={n_ci + i: n_co + i for i in range(carry.n_alias)}, compiler_params=params,
    )(*args, *carry.ins)
    return list(outs[:n_co]), list(outs[n_co:])


def _gather_direct(arrays):
    na = len(arrays)
    outs = [_sds((NDEV,) + a.shape, a.dtype) for a in arrays]
    plan = [(k, i, lambda s, r: None, i, lambda s, r: s) for i in range(na) for k in range(1, NDEV)]
    return _Round(arrays, outs, plan, [(i, lambda m: None, i, lambda m: m) for i in range(na)])


def _gather_a(arrays):
    na = len(arrays)
    outs = [_sds((NDEV,) + a.shape, a.dtype) for a in arrays]
    plan = [(k, i, lambda s, r: None, i, lambda s, r: s) for i in range(na) for k in (1, 2, 4, 6)]
    return _Round(arrays, outs, plan, [(i, lambda m: None, i, lambda m: m) for i in range(na)])


def _gather_b(got):
    na = len(got)
    plan = [(1, i, (lambda s, r, k=k: s ^ k), i, (lambda s, r, k=k: s ^ k)) for i in range(na) for k in (2, 4, 6)]
    return _Round(got, [_sds(g.shape, g.dtype) for g in got], plan, n_alias=na)


def _scatter_1(grads):
    plan = [(1, i, (lambda s, r, q=q: 2 * q + (r & 1)), i, (lambda s, r, q=q: q))
            for i in range(len(grads)) for q in range(4)]
    return _Round(grads, [_sds((4,) + g.shape[1:], g.dtype) for g in grads], plan)


def _scatter_2(chip):
    plan = [(k, i, lambda s, r: r >> 1, i, (lambda s, r, j=j: j)) for i in range(len(chip)) for j, k in enumerate((2, 4, 6))]
    return _Round(chip, [_sds((3,) + g.shape[1:], g.dtype) for g in chip], plan)


def _add_pairs(g, got, pos, *, name):
    _, sh, w = g.shape

    def body(pos_ref, g_ref, r_ref, o_ref):
        o_ref[...] = (g_ref[...].astype(F32) + r_ref[...].astype(F32)).astype(o_ref.dtype)

    return pl.pallas_call(
        body,
        name=name,
        grid_spec=pltpu.PrefetchScalarGridSpec(
            num_scalar_prefetch=1, grid=(4,),
            in_specs=[pl.BlockSpec((1, sh, w), lambda q, p: (2 * q + p[0], 0, 0)),
                      pl.BlockSpec((1, sh, w), lambda q, p: (q, 0, 0))],
            out_specs=pl.BlockSpec((1, sh, w), lambda q, p: (q, 0, 0))),
        out_shape=_sds((4, sh, w), g.dtype),
        compiler_params=_params(),
    )(pos, g, got)


def _sum_chips(chip, got, pos, *, name):
    _, sh, w = chip.shape

    def body(pos_ref, c_ref, r_ref, o_ref):
        acc = c_ref[0].astype(F32)
        for s in range(3):
            acc = acc + r_ref[s].astype(F32)
        o_ref[...] = acc

    return pl.pallas_call(
        body,
        name=name,
        grid_spec=pltpu.PrefetchScalarGridSpec(
            num_scalar_prefetch=1, grid=(1,),
            in_specs=[pl.BlockSpec((1, sh, w), lambda i, p: (p[1], 0, 0)), pl.BlockSpec((3, sh, w), lambda i, p: (0, 0, 0))],
            out_specs=pl.BlockSpec((sh, w), lambda i, p: (0, 0))),
        out_shape=_sds((sh, w), F32),
        compiler_params=_params(),
    )(pos, chip, got)


def _adamw_math(w, g, m, v):
    m2 = ADAM_B1 * m + (1.0 - ADAM_B1) * g
    v2 = ADAM_B2 * v + (1.0 - ADAM_B2) * (g * g)
    m_hat = m2 / (1.0 - ADAM_B1 ** ADAM_STEP)
    v_hat = v2 / (1.0 - ADAM_B2 ** ADAM_STEP)
    delta = -ADAM_LR * (m_hat / (jnp.sqrt(v_hat) + ADAM_EPS) + ADAM_WD * w)
    return delta, m2, v2


def _adamw(w, g, m, v, *, name):
    shape = w.shape
    flat = [t.reshape(-1, shape[-1]) for t in (w, g, m, v)]
    rows, cols = flat[0].shape
    tr = rows // 8 if rows % 64 == 0 else rows
    spec = _rows(tr, cols)

    def body(w_ref, g_ref, m_ref, v_ref, d_ref, m2_ref, v2_ref):
        d_ref[...], m2_ref[...], v2_ref[...] = _adamw_math(w_ref[...], g_ref[...], m_ref[...], v_ref[...])

    outs = pl.pallas_call(
        body, name=name, grid=(rows // tr,), in_specs=[spec] * 4, out_specs=[spec] * 3,
        out_shape=[_sds((rows, cols), F32)] * 3, compiler_params=_params())(*flat)
    return tuple(o.reshape(shape) for o in outs)


def _small_sums(packets, nf, dwp, dsc, dsk, *, name):
    flat = [p for layer in packets for p in layer]

    def total(ref, *idx):
        acc = ref[(0,) + idx]
        for dev in range(1, NDEV):
            acc = acc + ref[(dev,) + idx]
        return acc

    def body(*refs):
        pk = refs[:6]
        nf_ref, dwp0, dwp1, dsc0, dsc1, dsk0, dsk1 = refs[6:13]
        dm_ref, gb_ref, gn_ref, gnf_ref, gwp_ref, gps_ref, gsk_ref = refs[13:]
        dm_ref[...] = jnp.zeros_like(dm_ref)
        gn_ref[...] = jnp.zeros_like(gn_ref)
        for l in range(2):
            for sb in range(3):
                p = pk[3 * l + sb]
                for r in range(3):
                    col = slice((3 * sb + r) * D, (3 * sb + r + 1) * D)
                    lat = p[0, 0, r : r + 1, :]
                    dm_ref[l, 0:1, col] = lat
                    for dev in range(1, NDEV):
                        row = p[dev, 0, r : r + 1, :]
                        dm_ref[l, dev : dev + 1, col] = row
                        lat = lat + row
                    ctx = total(p, 1, slice(r, r + 1), slice(None))
                    dm_ref[l, 8:9, col] = ctx
                    gb_ref[l : l + 1, col] = lat + ctx
                gn_ref[l, sb : sb + 1, :] = total(p, 0, slice(3, 4), slice(None)) + total(p, 1, slice(3, 4), slice(None))
        gnf_ref[...] = total(nf_ref, slice(0, 1), slice(None))
        for l, (a, b, c) in enumerate(((dwp0, dsc0, dsk0), (dwp1, dsc1, dsk1))):
            gwp_ref[l] = total(a, slice(None), slice(None))
            gps_ref[l : l + 1, :] = total(b, slice(0, 1), slice(None))
            gsk_ref[l] = total(c, slice(None), slice(None))

    ins = flat + [nf, dwp[0], dwp[1], dsc[0], dsc[1], dsk[0], dsk[1]]
    return pl.pallas_call(
        body,
        name=name,
        out_shape=[_sds((2, 16, NMOD * D), F32), _sds((2, NMOD * D), F32), _sds((2, 8, D), F32), _sds((1, D), F32),
                   _sds((2, PW, 128), F32), _sds((2, PW), F32), _sds((2, 8, 128), F32)],
        compiler_params=pltpu.CompilerParams(vmem_limit_bytes=VMEM_LIMIT),
    )(*ins)


def _small_adamw(c_ctx, dc_all, triples, *, name):
    n = len(triples)

    def body(*refs):
        c_ref, dc_ref = refs[0], refs[1]
        ins = refs[2 : 2 + 4 * n - 1]
        outs = refs[2 + 4 * n - 1 :]
        acc = dc_ref[0, 0, 8:9, :] + dc_ref[0, 1, 8:9, :]
        for dev in range(1, NDEV):
            acc = acc + (dc_ref[dev, 0, 8:9, :] + dc_ref[dev, 1, 8:9, :])
        c = c_ref[...]
        sig = _sigmoid(c)
        g_c = acc * (sig * (1.0 + c * (1.0 - sig)))
        outs[0][...] = g_c
        pos = 0
        for k in range(n):
            if k == 0:
                w, g, m, v = ins[0][...], g_c, ins[1][...], ins[2][...]
                pos = 3
            else:
                w, g, m, v = (ins[pos + t][...] for t in range(4))
                pos += 4
            d, m2, v2 = _adamw_math(w, g, m, v)
            outs[1 + 3 * k][...], outs[2 + 3 * k][...], outs[3 + 3 * k][...] = d, m2, v2

    flat_in = [c_ctx, dc_all]
    out_shape = [_sds(c_ctx.shape, F32)]
    for k, (w, g, m, v) in enumerate(triples):
        flat_in += [w, m, v] if k == 0 else [w, g, m, v]
        out_shape += [_sds(w.shape, F32)] * 3
    return pl.pallas_call(body, name=name, out_shape=out_shape,
                          compiler_params=pltpu.CompilerParams(vmem_limit_bytes=VMEM_LIMIT))(*flat_in)


def _rope_tables(T, R):
    t = jnp.arange(T)
    inv = ROPE_BASE ** (-jnp.arange(0, HD // 2, 2, dtype=F32) / (HD // 2))
    ang = jnp.concatenate([(t // GRID_W).astype(F32)[:, None] * inv, (t % GRID_W).astype(F32)[:, None] * inv], axis=-1)
    cos = jnp.concatenate([jnp.tile(jnp.cos(ang), (1, 4)), jnp.ones((R - T, 128), F32)], axis=0)
    sin = jnp.concatenate([jnp.tile(jnp.sin(ang), (1, 4)), jnp.zeros((R - T, 128), F32)], axis=0)
    return cos, sin


def kernel(x, c, ctx, c_ctx, w_mod, b_mod, norm_ffn1, w_ffn1_in, w_ffn1_out, norm_mix, w_in, w_pool, pool_scale, sink, w_out, norm_ffn2, w_ffn2_in, w_ffn2_out, norm_final, loss_target, m_c_ctx, m_w_mod, m_b_mod, m_norm_ffn1, m_w_ffn1_in, m_w_ffn1_out, m_norm_mix, m_w_in, m_w_pool, m_pool_scale, m_sink, m_w_out, m_norm_ffn2, m_w_ffn2_in, m_w_ffn2_out, m_norm_final, v_c_ctx, v_w_mod, v_b_mod, v_norm_ffn1, v_w_ffn1_in, v_w_ffn1_out, v_norm_mix, v_w_in, v_w_pool, v_pool_scale, v_sink, v_w_out, v_norm_ffn2, v_w_ffn2_in, v_w_ffn2_out, v_norm_final):
    T = x.shape[1]
    R = T + LC
    nl = w_mod.shape[0]
    cx, cy, cc = _coords()
    me = _lin((cx, cy, cc))
    pos = jnp.stack([cc, 2 * cx + cy]).astype(jnp.int32)
    mcols = w_mod.shape[2]

    shards = [([w_ffn1_in[l].T.astype(BF16), w_ffn1_out[l].astype(BF16)],
               [w_in[l].T.astype(BF16), w_out[l].astype(BF16)],
               [w_ffn2_in[l].T.astype(BF16), w_ffn2_out[l].astype(BF16)]) for l in range(nl)]

    got = _exchange("ag_c_w", _merge(_gather_direct([c]), _gather_a(shards[0][0] + shards[0][1])))
    c_all, w_first = got[0], got[1:]
    c16 = jnp.concatenate([c_all.reshape(NDEV, D), c_ctx[None], jnp.zeros((16 - NDEV - 1, D), F32)], axis=0)
    b_cols = lax.dynamic_slice(b_mod, (0, me * mcols), (nl, mcols)).reshape(nl, 1, mcols)
    got = _exchange("ag_mod_w", _merge(_gather_b(w_first), _gather_direct([_mod_fwd(c16, w_mod, b_cols, name="mod_fwd")])))
    w_first, mod_all = got[:4], got[4]
    mod_all = jnp.transpose(mod_all, (1, 2, 0, 3)).reshape(nl, 16, NMOD, D)
    mine = lax.dynamic_index_in_dim(mod_all, me, axis=1, keepdims=False)
    pad = jnp.zeros((nl, 16 - NMOD, D), F32)
    modv = jnp.stack([jnp.concatenate([mine, pad], axis=1), jnp.concatenate([mod_all[:, 8], pad], axis=1)], axis=1)

    gvec = [jnp.concatenate([norm_ffn1[l][None], norm_mix[l][None], norm_ffn2[l][None], jnp.zeros((5, D), F32)], axis=0)
            for l in range(nl)]
    cos, sin = _rope_tables(T, R)
    ps2 = [pool_scale[l][None] for l in range(nl)]

    h = jnp.concatenate([x[0], ctx[0]], axis=0)
    loss_part, dh, small, nf_all, big = _forward_backward(
        h, loss_target[0], modv, gvec, shards, w_first, cos, sin, sink, w_pool, ps2, norm_final, pos, T=T)
    loss = lax.psum(loss_part[0, 0], ("x", "y", "c"))
    grad_x = dh[:T][None]

    dm, g_b_mod, g_norms, g_nf, g_wp, g_ps, g_sk = _small_sums(
        [small[l][0:3] for l in range(nl)], nf_all, *[[small[l][k] for l in range(nl)] for k in (3, 4, 5)],
        name="small_sums")
    dm_cols = lax.dynamic_slice(dm, (0, 0, me * mcols), (nl, 16, mcols))
    g_w_mod, dc_part = _mod_bwd(c16, dm_cols, w_mod, name="mod_bwd")
    (dc_all,) = _exchange("ag_dc", _gather_direct([dc_part]))

    grads = {
        "b_mod": g_b_mod, "norm_ffn1": g_norms[:, 0], "norm_mix": g_norms[:, 1], "norm_ffn2": g_norms[:, 2],
        "w_pool": g_wp.reshape(w_pool.shape), "pool_scale": g_ps, "sink": g_sk[:, :, 0], "norm_final": g_nf.reshape(D),
        "w_mod": g_w_mod,
        "w_ffn1_in": jnp.stack([big[l][0].T for l in range(nl)]), "w_ffn1_out": jnp.stack([big[l][1] for l in range(nl)]),
        "w_in": jnp.stack([big[l][2].T for l in range(nl)]), "w_out": jnp.stack([big[l][3] for l in range(nl)]),
        "w_ffn2_in": jnp.stack([big[l][4].T for l in range(nl)]), "w_ffn2_out": jnp.stack([big[l][5] for l in range(nl)]),
    }
    weights = dict(c_ctx=c_ctx, w_mod=w_mod, b_mod=b_mod, norm_ffn1=norm_ffn1, w_ffn1_in=w_ffn1_in, w_ffn1_out=w_ffn1_out,
                   norm_mix=norm_mix, w_in=w_in, w_pool=w_pool, pool_scale=pool_scale, sink=sink, w_out=w_out,
                   norm_ffn2=norm_ffn2, w_ffn2_in=w_ffn2_in, w_ffn2_out=w_ffn2_out, norm_final=norm_final)
    moms = dict(c_ctx=(m_c_ctx, v_c_ctx), w_mod=(m_w_mod, v_w_mod), b_mod=(m_b_mod, v_b_mod),
                norm_ffn1=(m_norm_ffn1, v_norm_ffn1), w_ffn1_in=(m_w_ffn1_in, v_w_ffn1_in),
                w_ffn1_out=(m_w_ffn1_out, v_w_ffn1_out), norm_mix=(m_norm_mix, v_norm_mix), w_in=(m_w_in, v_w_in),
                w_pool=(m_w_pool, v_w_pool), pool_scale=(m_pool_scale, v_pool_scale), sink=(m_sink, v_sink),
                w_out=(m_w_out, v_w_out), norm_ffn2=(m_norm_ffn2, v_norm_ffn2), w_ffn2_in=(m_w_ffn2_in, v_w_ffn2_in),
                w_ffn2_out=(m_w_ffn2_out, v_w_ffn2_out), norm_final=(m_norm_final, v_norm_final))
    order = list(weights)
    small_names = ["c_ctx", "b_mod", "norm_ffn1", "norm_mix", "w_pool", "pool_scale", "sink", "norm_ffn2", "norm_final"]

    def as2d(name, t):
        if name == "w_pool":
            return t.reshape(-1, 128)
        return t.reshape(1, -1) if t.ndim == 1 else t

    triples = [(as2d(n, weights[n]), None if n == "c_ctx" else as2d(n, grads[n]), as2d(n, moms[n][0]), as2d(n, moms[n][1]))
               for n in small_names]
    outs = _small_adamw(as2d("c_ctx", c_ctx), dc_all, triples, name="small_adamw")
    grads["c_ctx"] = outs[0].reshape(c_ctx.shape)
    delta, new_m, new_v = {}, {}, {}
    for k, n in enumerate(small_names):
        delta[n], new_m[n], new_v[n] = (o.reshape(weights[n].shape) for o in outs[1 + 3 * k : 4 + 3 * k])
    for n in order:
        if n not in small_names:
            delta[n], new_m[n], new_v[n] = _adamw(weights[n], grads[n], moms[n][0], moms[n][1], name=f"adamw_{n}")

    return (loss, grad_x, *[grads[n] for n in order], *[delta[n] for n in order],
            *[new_m[n] for n in order], *[new_v[n] for n in order])


def _merge(*rounds):
    ins, outs, plan, local, n_alias = [], [], [], [], 0
    for r in rounds:
        assert r.n_alias == 0 or (not ins and r.n_alias == len(r.ins) == len(r.out_shapes))
        oi, oo = len(ins), len(outs)
        plan += [(k, i + oi, sf, o + oo, df) for k, i, sf, o, df in r.plan]
        local += [(i + oi, sf, o + oo, df) for i, sf, o, df in r.local_plan]
        ins += r.ins
        outs += r.out_shapes
        n_alias += r.n_alias
    return _Round(ins, outs, plan, local, n_alias)


def _forward_backward(h, target, modv, gvec, shards, w_first, cos, sin, sink, w_pool, ps2, norm_final, pos, *, T):
    nl = len(gvec)
    flat = lambda ws: [w.reshape(-1, D) for w in ws]
    saved = []
    w1, wm = flat(w_first[:2]), flat(w_first[2:])
    for l in range(nl):
        last = l == nl - 1
        h0 = h
        if l == 0:
            (h1, a1, b1, f1), got = _ffn_fwd(h0, modv[l], gvec[l], *w1, T=T, mrow=0, grow=0, ctx_active=True,
                                             name=f"ffn1_fwd_{l}", carry=_gather_a(shards[l][2]))
            (u, q, k4, v4), got = _mixproj_fwd(h1, modv[l], gvec[l], wm[0], cos, sin, T=T, name=f"mixproj_fwd_{l}",
                                               carry=_gather_b(got))
            w2 = flat(got)
        else:
            (h1, a1, b1, f1), got = _ffn_fwd(h0, modv[l], gvec[l], *w1, T=T, mrow=0, grow=0, ctx_active=True,
                                             name=f"ffn1_fwd_{l}", carry=_gather_b(nxt_m + nxt_2))
            wm, w2 = flat(got[:2]), flat(got[2:])
            (u, q, k4, v4), _ = _mixproj_fwd(h1, modv[l], gvec[l], wm[0], cos, sin, T=T, name=f"mixproj_fwd_{l}")
        (cat,), nxt_1 = _attnpool_fwd(u, q, k4, v4, sink[l], w_pool[l], ps2[l], T=T, name=f"attnpool_fwd_{l}",
                                      carry=None if last else _gather_a(shards[l + 1][0]))
        (h2, mo), nxt_m = _mixout_fwd(h1, cat, modv[l], wm[1], T=T, ctx_active=not last, name=f"mixout_fwd_{l}",
                                      carry=None if last else _gather_a(shards[l + 1][1]))
        (h3, a2, b2, f2), got = _ffn_fwd(h2, modv[l], gvec[l], *w2, T=T, mrow=6, grow=2, ctx_active=not last,
                                         name=f"ffn2_fwd_{l}",
                                         carry=None if last else _merge(_gather_b(nxt_1), _gather_a(shards[l + 1][2])))
        saved.append((h0, a1, b1, f1, h1, u, q, k4, v4, cat, mo, h2, a2, b2, f2, w1, wm, w2))
        h = h3
        if not last:
            w1, nxt_2 = flat(got[:2]), got[2:]

    dh, loss_part, dnf = _loss_head(h, target, norm_final[None], T=T, name="loss_head")

    def adds(tag, grads, got):
        return [_add_pairs(g, r, pos, name=f"rs_add_{tag}_{i}") for i, (g, r) in enumerate(zip(grads, got))]

    def totals(tag, chip, got):
        return [_sum_chips(c_, r, pos, name=f"rs_sum_{tag}_{i}") for i, (c_, r) in enumerate(zip(chip, got))]

    small, big = [None] * nl, {}
    prev = None
    for l in reversed(range(nl)):
        last = l == nl - 1
        h0, a1, b1, f1, h1, u, q, k4, v4, cat, mo, h2, a2, b2, f2, w1, wm, w2 = saved[l]
        (dh, dab, s, n, df, pk2), got = _ffn_bwd(
            h2, dh, a2, b2, f2, modv[l], gvec[l], *w2, T=T, mrow=6, grow=2, ctx_active=not last, name=f"ffn2_bwd_{l}",
            carry=_merge(_scatter_1(prev[0]), _gather_a(prev[1])) if prev else None)
        if prev:
            c1, small_a = adds(f"ffn1_{l + 1}", prev[0], got[:2]), got[2:]
        g_w2i, got = _wgrad(dab, n, bk=FCH, sh=2 * DFF // NDEV, name=f"wgrad_ffn2_in_{l}",
                            carry=_scatter_2(c1[:1]) if prev else None)
        if prev:
            big[l + 1][0:1] = totals(f"ffn1_in_{l + 1}", c1[:1], got)
        g_w2o, got = _wgrad(s, df, bk=FCH, sh=DFF // NDEV, name=f"wgrad_ffn2_out_{l}",
                            carry=_scatter_2(c1[1:]) if prev else None)
        if prev:
            big[l + 1][1:2] = totals(f"ffn1_out_{l + 1}", c1[1:], got)
        rnd = _scatter_1([g_w2i, g_w2o])
        (dcat, dmix, pko), got = _mixout_bwd(dh, mo, modv[l], wm[1], T=T, ctx_active=not last, name=f"mixout_bwd_{l}",
                                             carry=_merge(_gather_b(small_a), rnd) if prev else rnd)
        if prev:
            small[l + 1], got = got[: len(small_a)], got[len(small_a) :]
        c2 = adds(f"ffn2_{l}", [g_w2i, g_w2o], got)
        g_wo, _ = _wgrad(cat, dmix, bk=D, sh=D // NDEV, name=f"wgrad_out_{l}")
        dps, dwp, dsc = _pool_bwd(u, dcat, w_pool[l], ps2[l], T=T, name=f"pool_bwd_{l}")
        (du, dq, dk, dv, dsk), got = _attn_bwd(q, k4, v4, dcat, dps, sink[l], T=T, name=f"attn_bwd_{l}", carry=_scatter_2(c2))
        big[l] = [None, None, None, None] + totals(f"ffn2_{l}", c2, got)
        dh, dproj, n, pkm = _mixproj_bwd(h1, dh, du, dq, dk, dv, modv[l], gvec[l], wm[0], cos, sin, T=T, name=f"mixproj_bwd_{l}")
        g_wi, _ = _wgrad(dproj, n, bk=PROJ, sh=PROJ // NDEV, name=f"wgrad_in_{l}")
        (dh, dab, s, n, df, pk1), got = _ffn_bwd(h0, dh, a1, b1, f1, modv[l], gvec[l], *w1, T=T, mrow=0, grow=0,
                                                 ctx_active=True, name=f"ffn1_bwd_{l}", carry=_scatter_1([g_wi, g_wo]))
        cm = adds(f"mix_{l}", [g_wi, g_wo], got)
        g_w1i, got = _wgrad(dab, n, bk=FCH, sh=2 * DFF // NDEV, name=f"wgrad_ffn1_in_{l}", carry=_scatter_2(cm))
        big[l][2:4] = totals(f"mix_{l}", cm, got)
        g_w1o, _ = _wgrad(s, df, bk=FCH, sh=DFF // NDEV, name=f"wgrad_ffn1_out_{l}")
        prev = ([g_w1i, g_w1o], [pk1, pkm + pko, pk2, dwp, dsc, dsk])
    got = _exchange("rs1_tail", _merge(_scatter_1(prev[0]), _gather_a(prev[1] + [dnf])))
    c1, small_a = adds("ffn1_0", prev[0], got[:2]), got[2:]
    got = _exchange("rs2_tail", _merge(_gather_b(small_a), _scatter_2(c1)))
    small[0], nf_all = got[: len(small_a) - 1], got[len(small_a) - 1]
    big[0][0:2] = totals("ffn1_0", c1, got[len(small_a) :])
    return loss_part, dh, small, nf_all, big
```

```python
import functools

import jax
import jax.numpy as jnp
from jax import lax
from jax.experimental import pallas as pl
from jax.experimental.pallas import tpu as pltpu

F32, BF16 = jnp.float32, jnp.bfloat16

D = 1024
LC = 256
DFF = 2816
NMOD = 9
PW = 512
AW = 512
KVW = 128
PROJ = PW + AW + 2 * KVW
HD = 64
BLK = 128
GRID_W = 64
POOL_WINDOWS = (2, 4, 8, 16)
EPS = 1e-6
NEG = -1e30
ROPE_BASE = 10000.0
NDEV = 8
MESH = pl.DeviceIdType.MESH

ADAM_LR, ADAM_B1, ADAM_B2, ADAM_EPS, ADAM_WD, ADAM_STEP = 0.001, 0.9, 0.999, 1e-08, 0.01, 10

VMEM_LIMIT = 56 * 1024 * 1024
TM = 256
FFN_CHUNKS = ((0, 1536), (1536, 1280))
WG_BK = 1408

ANY = pl.BlockSpec(memory_space=pl.ANY)
SMEM = pl.BlockSpec(memory_space=pltpu.SMEM)


def _params(ngrid=1):
    return pltpu.CompilerParams(dimension_semantics=("arbitrary",) * ngrid, vmem_limit_bytes=VMEM_LIMIT)


def _dot(a, b):
    return jnp.dot(a, b, preferred_element_type=F32)


def _dot_nt(a, b):
    return lax.dot_general(a, b, (((1,), (1,)), ((), ())), preferred_element_type=F32)


def _dot_tn(a, b):
    return lax.dot_general(a, b, (((0,), (0,)), ((), ())), preferred_element_type=F32)


def _sigmoid(x):
    return 1.0 / (1.0 + jnp.exp(-x))


def _rows(tm, w):
    return pl.BlockSpec((tm, w), lambda i: (i, 0))


def _full(shape):
    nd = len(shape)
    return pl.BlockSpec(shape, lambda *_: (0,) * nd)


def _sds(shape, dtype):
    return jax.ShapeDtypeStruct(shape, dtype)


def _norm_mod(h, g, shift, scale):
    r = lax.rsqrt(jnp.mean(h * h, axis=-1, keepdims=True) + EPS)
    xhat = h * r
    y = xhat * g
    return r, xhat, y, y * (1.0 + scale) + shift


def _norm_mod_bwd(dn, r, xhat, y, g, scale):
    dshift = jnp.sum(dn, axis=0, keepdims=True)
    dscale = jnp.sum(dn * y, axis=0, keepdims=True)
    dy = dn * (1.0 + scale)
    dg = jnp.sum(dy * xhat, axis=0, keepdims=True)
    dxh = dy * g
    dh = r * (dxh - xhat * jnp.mean(dxh * xhat, axis=-1, keepdims=True))
    return dh, dshift, dscale, dg


def _acc_partials(part_ref, first, rows):
    @pl.when(first)
    def _():
        part_ref[...] = jnp.zeros_like(part_ref)

    for r, val in rows.items():
        part_ref[0, r : r + 1, :] += val


def _mod_spec(n_lat):
    return pl.BlockSpec((1, 16, D), lambda i: (i // n_lat, 0, 0))


def _part_spec(n_lat):
    return pl.BlockSpec((1, 8, D), lambda i: (i // n_lat, 0, 0))


def _load_weights(pairs, sem):
    copies = [pltpu.make_async_copy(src, dst, sem.at[k]) for k, (src, dst) in enumerate(pairs)]
    for cp in copies:
        cp.start()
    for cp in copies:
        cp.wait()


def _ffn_weight_copies(win_hbm, wout_hbm, win_v, wout_v, sem):
    loads = []
    for k, (c0, cw) in enumerate(FFN_CHUNKS):
        slabs = [(win_hbm, win_v, c0), (win_hbm, win_v, DFF + c0), (wout_hbm, wout_v, c0)]
        loads.append([pltpu.make_async_copy(src.at[pl.ds(r0, cw)], dst.at[pl.ds(r0, cw)], sem.at[3 * k + j])
                      for j, (src, dst, r0) in enumerate(slabs)])
    return loads


def _ffn_steps(i, n_active, loads, compute):
    @pl.when(i == 0)
    def _():
        for cp in sum(loads, []):
            cp.start()
        compute(loads)

    @pl.when(jnp.logical_and(i > 0, i < n_active))
    def _():
        compute(None)


def _wait_chunk(loads, k):
    if loads is not None:
        for cp in loads[k]:
            cp.wait()


def _ffn_fwd(h, modv, gvec, win, wout, *, T, mrow, grow, ctx_active, name, carry=None):
    R = h.shape[0]
    n_lat, n_tiles = T // TM, R // TM
    n_active = n_tiles if ctx_active else n_lat

    def body(h_ref, mod_ref, g_ref, win_hbm, wout_hbm, ho_ref, a_ref, b_ref, f_ref, win_v, wout_v, sem):
        i = pl.program_id(0)

        def compute(loads):
            h = h_ref[...]
            shift, scale, gate = (mod_ref[0, mrow + k : mrow + k + 1, :] for k in range(3))
            _, _, _, n = _norm_mod(h, g_ref[grow : grow + 1, :], shift, scale)
            n_bf = n.astype(BF16)
            acc = jnp.zeros((TM, D), F32)
            for k, (c0, cw) in enumerate(FFN_CHUNKS):
                _wait_chunk(loads, k)
                a = _dot_nt(n_bf, win_v[c0 : c0 + cw, :])
                b = _dot_nt(n_bf, win_v[DFF + c0 : DFF + c0 + cw, :])
                a_ref[:, c0 : c0 + cw] = a.astype(BF16)
                b_ref[:, c0 : c0 + cw] = b.astype(BF16)
                s = a * _sigmoid(a) * b
                acc = acc + _dot(s.astype(BF16), wout_v[c0 : c0 + cw, :])
            f_ref[...] = acc.astype(BF16)
            ho_ref[...] = h + (0.5 * gate) * acc

        _ffn_steps(i, n_active, _ffn_weight_copies(win_hbm, wout_hbm, win_v, wout_v, sem), compute)

        @pl.when(i >= n_active)
        def _():
            ho_ref[...] = h_ref[...]
            a_ref[...] = jnp.zeros_like(a_ref)
            b_ref[...] = jnp.zeros_like(b_ref)
            f_ref[...] = jnp.zeros_like(f_ref)

    return _call(
        body,
        name=name,
        grid=(n_tiles,),
        in_specs=[_rows(TM, D), _mod_spec(n_lat), _full((8, D)), ANY, ANY],
        out_specs=[_rows(TM, D), _rows(TM, DFF), _rows(TM, DFF), _rows(TM, D)],
        out_shape=[_sds((R, D), F32), _sds((R, DFF), BF16), _sds((R, DFF), BF16), _sds((R, D), BF16)],
        scratch_shapes=[pltpu.VMEM((2 * DFF, D), BF16), pltpu.VMEM((DFF, D), BF16),
                        pltpu.SemaphoreType.DMA((3 * len(FFN_CHUNKS),))],
        args=(h, modv, gvec, win, wout),
        carry=carry,
    )


def _ffn_bwd(h, dho, a, b, f, modv, gvec, win, wout, *, T, mrow, grow, ctx_active, name, carry=None):
    R = h.shape[0]
    n_lat, n_tiles = T // TM, R // TM
    n_active = n_tiles if ctx_active else n_lat

    def body(h_ref, dho_ref, a_ref, b_ref, f_ref, mod_ref, g_ref, win_hbm, wout_hbm,
             dh_ref, dab_ref, s_ref, n_ref, df_ref, part_ref, win_v, wout_v, sem):
        i = pl.program_id(0)
        first = jnp.logical_or(i == 0, i == n_lat)

        def compute(loads):
            h = h_ref[...]
            dho = dho_ref[...]
            shift, scale, gate = (mod_ref[0, mrow + k : mrow + k + 1, :] for k in range(3))
            g = g_ref[grow : grow + 1, :]
            r, xhat, y, n = _norm_mod(h, g, shift, scale)
            dgate = 0.5 * jnp.sum(dho * f_ref[...].astype(F32), axis=0, keepdims=True)
            df_bf = ((0.5 * gate) * dho).astype(BF16)
            df_ref[...] = df_bf
            n_ref[...] = n.astype(BF16)
            dn = jnp.zeros((TM, D), F32)
            for k, (c0, cw) in enumerate(FFN_CHUNKS):
                _wait_chunk(loads, k)
                ds = _dot_nt(df_bf, wout_v[c0 : c0 + cw, :])
                av = a_ref[:, c0 : c0 + cw].astype(F32)
                bv = b_ref[:, c0 : c0 + cw].astype(F32)
                sig = _sigmoid(av)
                sa = av * sig
                s_ref[:, c0 : c0 + cw] = (sa * bv).astype(BF16)
                da = (ds * bv * (sig * (1.0 + av * (1.0 - sig)))).astype(BF16)
                db = (ds * sa).astype(BF16)
                dab_ref[:, c0 : c0 + cw] = da
                dab_ref[:, DFF + c0 : DFF + c0 + cw] = db
                dn = dn + _dot(da, win_v[c0 : c0 + cw, :]) + _dot(db, win_v[DFF + c0 : DFF + c0 + cw, :])
            dh, dshift, dscale, dg = _norm_mod_bwd(dn, r, xhat, y, g, scale)
            dh_ref[...] = dho + dh
            _acc_partials(part_ref, first, {0: dshift, 1: dscale, 2: dgate, 3: dg})

        _ffn_steps(i, n_active, _ffn_weight_copies(win_hbm, wout_hbm, win_v, wout_v, sem), compute)

        @pl.when(i >= n_active)
        def _():
            dh_ref[...] = dho_ref[...]
            dab_ref[...] = jnp.zeros_like(dab_ref)
            s_ref[...] = jnp.zeros_like(s_ref)
            n_ref[...] = jnp.zeros_like(n_ref)
            df_ref[...] = jnp.zeros_like(df_ref)
            part_ref[...] = jnp.zeros_like(part_ref)

    return _call(
        body,
        name=name,
        grid=(n_tiles,),
        in_specs=[_rows(TM, D), _rows(TM, D), _rows(TM, DFF), _rows(TM, DFF), _rows(TM, D),
                  _mod_spec(n_lat), _full((8, D)), ANY, ANY],
        out_specs=[_rows(TM, D), _rows(TM, 2 * DFF), _rows(TM, DFF), _rows(TM, D), _rows(TM, D), _part_spec(n_lat)],
        out_shape=[_sds((R, D), F32), _sds((R, 2 * DFF), BF16), _sds((R, DFF), BF16), _sds((R, D), BF16),
                   _sds((R, D), BF16), _sds((2, 8, D), F32)],
        scratch_shapes=[pltpu.VMEM((2 * DFF, D), BF16), pltpu.VMEM((DFF, D), BF16),
                        pltpu.SemaphoreType.DMA((3 * len(FFN_CHUNKS),))],
        args=(h, dho, a, b, f, modv, gvec, win, wout),
        carry=carry,
    )


def _wgrad(x, y, *, bk, sh, name, carry=None):
    R, kx = x.shape
    n = y.shape[1]
    tr = R // 2
    nr, nsh = R // tr, bk // sh

    def body(x_ref, y_ref, o_ref, acc):
        r = pl.program_id(1)

        @pl.when(r == 0)
        def _():
            acc[...] = jnp.zeros_like(acc)

        acc[...] += _dot_tn(x_ref[...], y_ref[...])

        @pl.when(r == nr - 1)
        def _():
            for s in range(nsh):
                o_ref[s] = acc[s * sh : (s + 1) * sh, :].astype(BF16)

    (out,), got = _call(
        body,
        name=name,
        grid=(kx // bk, nr),
        in_specs=[pl.BlockSpec((tr, bk), lambda k, r: (r, k)), pl.BlockSpec((tr, n), lambda k, r: (r, 0))],
        out_specs=[pl.BlockSpec((nsh, sh, n), lambda k, r: (k, 0, 0))],
        out_shape=[_sds((kx // sh, sh, n), BF16)],
        scratch_shapes=[pltpu.VMEM((bk, n), F32)],
        args=(x, y),
        carry=carry,
    )
    return out, got


def _rot_half(x):
    lane = lax.broadcasted_iota(jnp.int32, x.shape, 1)
    return jnp.where((lane & (HD - 1)) < HD // 2, -pltpu.roll(x, 128 - HD // 2, 1), pltpu.roll(x, HD // 2, 1))


def _tile_sel():
    i = lax.broadcasted_iota(jnp.int32, (KVW, AW), 0)
    j = lax.broadcasted_iota(jnp.int32, (KVW, AW), 1)
    return jnp.where(i == (j // 256) * HD + (j & (HD - 1)), 1.0, 0.0).astype(BF16)


def _mixproj_fwd(h, modv, gvec, win, cos, sin, *, T, name, carry=None):
    R = h.shape[0]
    n_lat, n_tiles = T // TM, R // TM

    def body(h_ref, mod_ref, g_ref, win_ref, cos_ref, sin_ref, u_ref, q_ref, k4_ref, v4_ref):
        shift, scale = mod_ref[0, 3:4, :], mod_ref[0, 4:5, :]
        _, _, _, n = _norm_mod(h_ref[...], g_ref[1:2, :], shift, scale)
        proj = _dot_nt(n.astype(BF16), win_ref[...])
        u_ref[...] = proj[:, :PW]
        cs, sn = cos_ref[...], sin_ref[...]
        for s in range(AW // 128):
            x = proj[:, PW + 128 * s : PW + 128 * (s + 1)]
            q_ref[:, 128 * s : 128 * (s + 1)] = (x * cs + _rot_half(x) * sn).astype(BF16)
        k = proj[:, PW + AW : PW + AW + KVW]
        k = (k * cs + _rot_half(k) * sn).astype(BF16)
        v = proj[:, PW + AW + KVW :].astype(BF16)
        sel = _tile_sel()
        k4_ref[...] = _dot(k, sel).astype(BF16)
        v4_ref[...] = _dot(v, sel).astype(BF16)

    return _call(
        body,
        name=name,
        grid=(n_tiles,),
        in_specs=[_rows(TM, D), _mod_spec(n_lat), _full((8, D)), _full((PROJ, D)), _rows(TM, 128), _rows(TM, 128)],
        out_specs=[_rows(TM, PW), _rows(TM, AW), _rows(TM, AW), _rows(TM, AW)],
        out_shape=[_sds((R, PW), F32), _sds((R, AW), BF16), _sds((R, AW), BF16), _sds((R, AW), BF16)],
        scratch_shapes=[],
        args=(h, modv, gvec, win, cos, sin),
        carry=carry,
    )


def _win_start(j, hi):
    return pl.multiple_of(jnp.clip((j - 1) * BLK, 0, hi - 3 * BLK), BLK)


def _hi_lo(x):
    hi = x.astype(BF16)
    return hi, (x - hi.astype(F32)).astype(BF16)


def _pool_bounds(t, w, T, R):
    is_ctx = t >= T
    lo = jnp.maximum(t - w // 2, jnp.where(is_ctx, T, 0))
    hi = jnp.minimum(t + w // 2, jnp.where(is_ctx, R, T))
    return lo, hi


def _pooled(u_v, j, T, R):
    start = _win_start(j, R)
    u3_hi, u3_lo = _hi_lo(u_v[pl.ds(start, 3 * BLK), :])
    ub = u_v[pl.ds(pl.multiple_of(j * BLK, BLK), BLK), :]
    t = j * BLK + lax.broadcasted_iota(jnp.int32, (BLK, 1), 0)
    pos = start + lax.broadcasted_iota(jnp.int32, (1, 3 * BLK), 1)
    pooled, counts = [], []
    for g, w in enumerate(POOL_WINDOWS):
        lo, hi = _pool_bounds(t, w, T, R)
        band = jnp.where(pos >= lo, jnp.where(pos < hi, 1.0, 0.0), 0.0).astype(BF16)
        sl = slice(g * 128, (g + 1) * 128)
        sums = _dot(band, u3_hi[:, sl]) + _dot(band, u3_lo[:, sl])
        cnt = (hi - lo).astype(F32)
        pooled.append(sums / cnt - ub[:, sl])
        counts.append(cnt)
    return pooled, counts


def _stack_heads(x):
    lane_h = lax.broadcasted_iota(jnp.int32, x.shape, 1) // HD
    return jnp.concatenate([jnp.where(lane_h == h, x, jnp.zeros_like(x)) for h in range(4)], axis=0)


def _unstack_heads(x):
    lane_h = lax.broadcasted_iota(jnp.int32, (BLK, 256), 1) // HD
    out = jnp.zeros((BLK, 256), F32)
    for h in range(4):
        out = out + jnp.where(lane_h == h, x[h * BLK : (h + 1) * BLK, :], 0.0)
    return out


def _attn_probs(qs, kl, kc, sink_ref, g, j, start_l, nbl):
    s_l = _dot_nt(qs, kl) * (HD ** -0.5)
    s_c = _dot_nt(qs, kc) * (HD ** -0.5)
    rowi = lax.broadcasted_iota(jnp.int32, (4 * BLK, 1), 0)
    qpos = j * BLK + (rowi & (BLK - 1))
    kpos = start_l + lax.broadcasted_iota(jnp.int32, (1, 3 * BLK), 1)
    reach = jnp.where(j < nbl, BLK, -1)
    s_l = jnp.where(jnp.abs(kpos - qpos) <= reach, s_l, NEG)
    rb = rowi // BLK
    sk = jnp.where(rb == 0, sink_ref[4 * g], jnp.where(rb == 1, sink_ref[4 * g + 1],
                   jnp.where(rb == 2, sink_ref[4 * g + 2], sink_ref[4 * g + 3])))
    m = jnp.maximum(jnp.maximum(jnp.max(s_l, axis=1, keepdims=True), jnp.max(s_c, axis=1, keepdims=True)), sk)
    e_l, e_c, e_s = jnp.exp(s_l - m), jnp.exp(s_c - m), jnp.exp(sk - m)
    inv = 1.0 / (jnp.sum(e_l, axis=1, keepdims=True) + jnp.sum(e_c, axis=1, keepdims=True) + e_s)
    return e_l * inv, e_c * inv, e_s * inv


def _attnpool_fwd(u, q, k4, v4, sink, w_pool, pool_scale, *, T, name, carry=None):
    R = u.shape[0]
    nb, nbl = R // BLK, T // BLK

    def body(q_ref, sink_ref, wp_ref, ps_ref, u_hbm, k4_hbm, v4_hbm, cat_ref, u_v, k4_v, v4_v, sem):
        j = pl.program_id(0)

        @pl.when(j == 0)
        def _():
            _load_weights([(u_hbm, u_v), (k4_hbm, k4_v), (v4_hbm, v4_v)], sem)

        pooled, _ = _pooled(u_v, j, T, R)
        for g in range(4):
            mixed = _dot(pooled[g].astype(BF16), wp_ref[g].astype(BF16)) * ps_ref[:, g * 128 : (g + 1) * 128]
            cat_ref[:, g * 128 : (g + 1) * 128] = mixed.astype(BF16)

        start_l = _win_start(j, T)
        for g in range(2):
            gl = slice(g * 256, (g + 1) * 256)
            qs = _stack_heads(q_ref[:, gl])
            p_l, p_c, _ = _attn_probs(qs, k4_v[pl.ds(start_l, 3 * BLK), gl], k4_v[T:R, gl], sink_ref, g, j, start_l, nbl)
            o = _dot(p_l.astype(BF16), v4_v[pl.ds(start_l, 3 * BLK), gl]) + _dot(p_c.astype(BF16), v4_v[T:R, gl])
            cat_ref[:, PW + g * 256 : PW + (g + 1) * 256] = _unstack_heads(o).astype(BF16)

    return _call(
        body,
        name=name,
        grid=(nb,),
        in_specs=[_rows(BLK, AW), SMEM, _full((4, 128, 128)), _full((1, PW)), ANY, ANY, ANY],
        out_specs=[_rows(BLK, D)],
        out_shape=[_sds((R, D), BF16)],
        scratch_shapes=[pltpu.VMEM((R, PW), F32), pltpu.VMEM((R, AW), BF16), pltpu.VMEM((R, AW), BF16),
                        pltpu.SemaphoreType.DMA((3,))],
        args=(q, sink, w_pool, pool_scale, u, k4, v4),
        carry=carry,
    )


def _mixout_fwd(h, cat, modv, wout, *, T, ctx_active, name, carry=None):
    R = h.shape[0]
    n_lat, n_tiles = T // TM, R // TM

    def body(h_ref, cat_ref, mod_ref, w_ref, ho_ref, mo_ref):
        i = pl.program_id(0)

        def compute():
            mo = _dot(cat_ref[...], w_ref[...])
            mo_ref[...] = mo.astype(BF16)
            ho_ref[...] = h_ref[...] + mod_ref[0, 5:6, :] * mo

        if ctx_active:
            compute()
        else:
            pl.when(i < n_lat)(compute)

            @pl.when(i >= n_lat)
            def _():
                ho_ref[...] = h_ref[...]
                mo_ref[...] = jnp.zeros_like(mo_ref)

    return _call(
        body,
        name=name,
        grid=(n_tiles,),
        in_specs=[_rows(TM, D), _rows(TM, D), _mod_spec(n_lat), _full((D, D))],
        out_specs=[_rows(TM, D), _rows(TM, D)],
        out_shape=[_sds((R, D), F32), _sds((R, D), BF16)],
        scratch_shapes=[],
        args=(h, cat, modv, wout),
        carry=carry,
    )


def _mixout_bwd(dho, mo, modv, wout, *, T, ctx_active, name, carry=None):
    R = dho.shape[0]
    n_lat, n_tiles = T // TM, R // TM

    def body(dho_ref, mo_ref, mod_ref, w_ref, dcat_ref, dmix_ref, part_ref):
        i = pl.program_id(0)
        first = jnp.logical_or(i == 0, i == n_lat)

        def compute():
            dho = dho_ref[...]
            dmix = (mod_ref[0, 5:6, :] * dho).astype(BF16)
            dmix_ref[...] = dmix
            dcat_ref[...] = _dot_nt(dmix, w_ref[...])
            dgate = jnp.sum(dho * mo_ref[...].astype(F32), axis=0, keepdims=True)
            _acc_partials(part_ref, first, {2: dgate})

        if ctx_active:
            compute()
        else:
            pl.when(i < n_lat)(compute)

            @pl.when(i >= n_lat)
            def _():
                dcat_ref[...] = jnp.zeros_like(dcat_ref)
                dmix_ref[...] = jnp.zeros_like(dmix_ref)
                part_ref[...] = jnp.zeros_like(part_ref)

    return _call(
        body,
        name=name,
        grid=(n_tiles,),
        in_specs=[_rows(TM, D), _rows(TM, D), _mod_spec(n_lat), _full((D, D))],
        out_specs=[_rows(TM, D), _rows(TM, D), _part_spec(n_lat)],
        out_shape=[_sds((R, D), F32), _sds((R, D), BF16), _sds((2, 8, D), F32)],
        scratch_shapes=[],
        args=(dho, mo, modv, wout),
        carry=carry,
    )


def _pool_bwd(u, dcat, w_pool, pool_scale, *, T, name):
    R = u.shape[0]
    nb = R // BLK

    def body(dcat_ref, wp_ref, ps_ref, u_hbm, dps_ref, dwp_ref, dsc_ref, u_v, sem):
        j = pl.program_id(0)

        @pl.when(j == 0)
        def _():
            _load_weights([(u_hbm, u_v)], sem)
            dwp_ref[...] = jnp.zeros_like(dwp_ref)
            dsc_ref[...] = jnp.zeros_like(dsc_ref)

        pooled, counts = _pooled(u_v, j, T, R)
        for g in range(4):
            sl = slice(g * 128, (g + 1) * 128)
            p_bf = pooled[g].astype(BF16)
            w_bf = wp_ref[g].astype(BF16)
            dmixed = dcat_ref[:, sl]
            dsc_ref[0:1, sl] += jnp.sum(dmixed * _dot(p_bf, w_bf), axis=0, keepdims=True)
            dmp = (dmixed * ps_ref[:, sl]).astype(BF16)
            dwp_ref[sl, :] += _dot_tn(p_bf, dmp)
            dps_ref[:, sl] = _dot_nt(dmp, w_bf) / counts[g]

    return pl.pallas_call(
        body,
        name=name,
        grid=(nb,),
        in_specs=[_rows(BLK, D), _full((4, 128, 128)), _full((1, PW)), ANY],
        out_specs=[_rows(BLK, PW), _full((PW, 128)), _full((8, PW))],
        out_shape=[_sds((R, PW), F32), _sds((PW, 128), F32), _sds((8, PW), F32)],
        scratch_shapes=[pltpu.VMEM((R, PW), F32), pltpu.SemaphoreType.DMA((1,))],
        compiler_params=_params(),
    )(dcat, w_pool, pool_scale, u)


def _fold_heads(x):
    y = x[:, :128] + x[:, 128:]
    return y + pltpu.roll(y, HD, 1)


def _attn_bwd(q, k4, v4, dcat, dps, sink, *, T, name, carry=None):
    R = q.shape[0]
    nb, nbl = R // BLK, T // BLK

    def body(q_ref, dcat_ref, sink_ref, k4_hbm, v4_hbm, dps_hbm, du_ref, dq_ref, dk_ref, dv_ref, dsk_ref,
             k4_v, v4_v, dps_v, sem):
        j = pl.program_id(0)

        @pl.when(j == 0)
        def _():
            _load_weights([(k4_hbm, k4_v), (v4_hbm, v4_v), (dps_hbm, dps_v)], sem)
            dk_ref[...] = jnp.zeros_like(dk_ref)
            dv_ref[...] = jnp.zeros_like(dv_ref)
            dsk_ref[...] = jnp.zeros_like(dsk_ref)

        start = _win_start(j, R)
        d3_hi, d3_lo = _hi_lo(dps_v[pl.ds(start, 3 * BLK), :])
        db = dps_v[pl.ds(pl.multiple_of(j * BLK, BLK), BLK), :]
        pos = j * BLK + lax.broadcasted_iota(jnp.int32, (BLK, 1), 0)
        t_r = start + lax.broadcasted_iota(jnp.int32, (1, 3 * BLK), 1)
        for g, w in enumerate(POOL_WINDOWS):
            sl = slice(g * 128, (g + 1) * 128)
            lo_r, hi_r = _pool_bounds(t_r, w, T, R)
            band_t = jnp.where(pos >= lo_r, jnp.where(pos < hi_r, 1.0, 0.0), 0.0).astype(BF16)
            lo_c, hi_c = _pool_bounds(pos, w, T, R)
            du_ref[:, sl] = _dot(band_t, d3_hi[:, sl]) + _dot(band_t, d3_lo[:, sl]) - db[:, sl] * (hi_c - lo_c).astype(F32)

        start_l = _win_start(j, T)
        rb = lax.broadcasted_iota(jnp.int32, (4 * BLK, 1), 0) // BLK
        lane = lax.broadcasted_iota(jnp.int32, (1, 128), 1)
        dk_l, dk_c, dv_l, dv_c = [], [], [], []
        for g in range(2):
            gl = slice(g * 256, (g + 1) * 256)
            qs = _stack_heads(q_ref[:, gl])
            kl, kc = k4_v[pl.ds(start_l, 3 * BLK), gl], k4_v[T:R, gl]
            vl, vc = v4_v[pl.ds(start_l, 3 * BLK), gl], v4_v[T:R, gl]
            p_l, p_c, p_s = _attn_probs(qs, kl, kc, sink_ref, g, j, start_l, nbl)
            dos = _stack_heads(dcat_ref[:, PW + g * 256 : PW + (g + 1) * 256]).astype(BF16)
            dp_l, dp_c = _dot_nt(dos, vl), _dot_nt(dos, vc)
            delta = jnp.sum(p_l * dp_l, axis=1, keepdims=True) + jnp.sum(p_c * dp_c, axis=1, keepdims=True)
            ds_l = (p_l * (dp_l - delta) * (HD ** -0.5)).astype(BF16)
            ds_c = (p_c * (dp_c - delta) * (HD ** -0.5)).astype(BF16)
            dq_ref[:, gl] = _unstack_heads(_dot(ds_l, kl) + _dot(ds_c, kc))
            dk_l.append(_fold_heads(_dot_tn(ds_l, qs)))
            dk_c.append(_fold_heads(_dot_tn(ds_c, qs)))
            dv_l.append(_fold_heads(_dot_tn(p_l.astype(BF16), dos)))
            dv_c.append(_fold_heads(_dot_tn(p_c.astype(BF16), dos)))
            dsink = -p_s * delta
            for h in range(4):
                tot = jnp.sum(jnp.where(rb == h, dsink, 0.0), axis=0, keepdims=True)
                dsk_ref[4 * g + h : 4 * g + h + 1, :] += jnp.broadcast_to(tot, (1, 128))
        first = lane < HD
        dk_ref[pl.ds(start_l, 3 * BLK), :] += jnp.where(first, dk_l[0], dk_l[1])
        dk_ref[T:R, :] += jnp.where(first, dk_c[0], dk_c[1])
        dv_ref[pl.ds(start_l, 3 * BLK), :] += jnp.where(first, dv_l[0], dv_l[1])
        dv_ref[T:R, :] += jnp.where(first, dv_c[0], dv_c[1])

    return _call(
        body,
        name=name,
        grid=(nb,),
        in_specs=[_rows(BLK, AW), _rows(BLK, D), SMEM, ANY, ANY, ANY],
        out_specs=[_rows(BLK, PW), _rows(BLK, AW), _full((R, KVW)), _full((R, KVW)), _full((8, 128))],
        out_shape=[_sds((R, PW), F32), _sds((R, AW), F32), _sds((R, KVW), F32), _sds((R, KVW), F32),
                   _sds((8, 128), F32)],
        scratch_shapes=[pltpu.VMEM((R, AW), BF16), pltpu.VMEM((R, AW), BF16), pltpu.VMEM((R, PW), F32),
                        pltpu.SemaphoreType.DMA((3,))],
        args=(q, dcat, sink, k4, v4, dps),
        carry=carry,
    )


def _mixproj_bwd(h, dho, du, dq, dk, dv, modv, gvec, win, cos, sin, *, T, name):
    R = h.shape[0]
    n_lat, n_tiles = T // TM, R // TM

    def body(h_ref, dho_ref, du_ref, dq_ref, dk_ref, dv_ref, mod_ref, g_ref, win_ref, cos_ref, sin_ref,
             dh_ref, dproj_ref, n_ref, part_ref):
        i = pl.program_id(0)
        first = jnp.logical_or(i == 0, i == n_lat)
        shift, scale = mod_ref[0, 3:4, :], mod_ref[0, 4:5, :]
        g = g_ref[1:2, :]
        r, xhat, y, n = _norm_mod(h_ref[...], g, shift, scale)
        n_ref[...] = n.astype(BF16)
        cs, sn = cos_ref[...], sin_ref[...]
        dproj_ref[:, :PW] = du_ref[...].astype(BF16)
        for s in range(AW // 128):
            x = dq_ref[:, 128 * s : 128 * (s + 1)]
            dproj_ref[:, PW + 128 * s : PW + 128 * (s + 1)] = (x * cs - _rot_half(x) * sn).astype(BF16)
        x = dk_ref[...]
        dproj_ref[:, PW + AW : PW + AW + KVW] = (x * cs - _rot_half(x) * sn).astype(BF16)
        dproj_ref[:, PW + AW + KVW :] = dv_ref[...].astype(BF16)
        dn = _dot(dproj_ref[...], win_ref[...])
        dh, dshift, dscale, dg = _norm_mod_bwd(dn, r, xhat, y, g, scale)
        dh_ref[...] = dho_ref[...] + dh
        _acc_partials(part_ref, first, {0: dshift, 1: dscale, 3: dg})

    return pl.pallas_call(
        body,
        name=name,
        grid=(n_tiles,),
        in_specs=[_rows(TM, D), _rows(TM, D), _rows(TM, PW), _rows(TM, AW), _rows(TM, KVW), _rows(TM, KVW),
                  _mod_spec(n_lat), _full((8, D)), _full((PROJ, D)), _rows(TM, 128), _rows(TM, 128)],
        out_specs=[_rows(TM, D), _rows(TM, PROJ), _rows(TM, D), _part_spec(n_lat)],
        out_shape=[_sds((R, D), F32), _sds((R, PROJ), BF16), _sds((R, D), BF16), _sds((2, 8, D), F32)],
        compiler_params=_params(),
    )(h, dho, du, dq, dk, dv, modv, gvec, win, cos, sin)


def _loss_head(h, target, g_final, *, T, name):
    R = h.shape[0]
    n_lat, n_tiles = T // TM, R // TM

    def body(h_ref, t_ref, g_ref, dh_ref, loss_ref, dg_ref):
        i = pl.program_id(0)

        @pl.when(i == 0)
        def _():
            loss_ref[...] = jnp.zeros_like(loss_ref)
            dg_ref[...] = jnp.zeros_like(dg_ref)

        @pl.when(i < n_lat)
        def _():
            h = h_ref[...]
            g = g_ref[...]
            r = lax.rsqrt(jnp.mean(h * h, axis=-1, keepdims=True) + EPS)
            xhat = h * r
            err = xhat * g - t_ref[...]
            tot = jnp.sum(jnp.sum(err * err, axis=1, keepdims=True), axis=0, keepdims=True)
            loss_ref[...] += jnp.broadcast_to(tot * (0.5 / D), loss_ref.shape)
            dy = err * (1.0 / D)
            dg_ref[0:1, :] += jnp.sum(dy * xhat, axis=0, keepdims=True)
            dxh = dy * g
            dh_ref[...] = r * (dxh - xhat * jnp.mean(dxh * xhat, axis=-1, keepdims=True))

        @pl.when(i >= n_lat)
        def _():
            dh_ref[...] = jnp.zeros_like(dh_ref)

    return pl.pallas_call(
        body,
        name=name,
        grid=(n_tiles,),
        in_specs=[_rows(TM, D), pl.BlockSpec((TM, D), lambda i: (jnp.minimum(i, n_lat - 1), 0)), _full((1, D))],
        out_specs=[_rows(TM, D), _full((8, 128)), _full((8, D))],
        out_shape=[_sds((R, D), F32), _sds((8, 128), F32), _sds((8, D), F32)],
        compiler_params=_params(),
    )(h, target, g_final)


def _mod_fwd(c16, w_mod, b_cols, *, name):
    nl, _, cols = w_mod.shape

    def body(c_ref, w_ref, b_ref, o_ref):
        c = c_ref[...]
        sc = (c * _sigmoid(c)).astype(BF16)
        o_ref[0] = _dot(sc, w_ref[0].astype(BF16)) + b_ref[0]

    return pl.pallas_call(
        body,
        name=name,
        grid=(nl,),
        in_specs=[_full((16, D)), pl.BlockSpec((1, D, cols), lambda l: (l, 0, 0)),
                  pl.BlockSpec((1, 1, cols), lambda l: (l, 0, 0))],
        out_specs=pl.BlockSpec((1, 16, cols), lambda l: (l, 0, 0)),
        out_shape=_sds((nl, 16, cols), F32),
        compiler_params=_params(),
    )(c16, w_mod, b_cols)


def _mod_bwd(c16, dm_cols, w_mod, *, name):
    nl, _, cols = w_mod.shape

    def body(c_ref, dm_ref, w_ref, gw_ref, dc_ref):
        c = c_ref[...]
        sc = (c * _sigmoid(c)).astype(BF16)
        dm = dm_ref[0].astype(BF16)
        gw_ref[0] = _dot_tn(sc, dm)
        dc_ref[0] = _dot_nt(dm, w_ref[0].astype(BF16))

    return pl.pallas_call(
        body,
        name=name,
        grid=(nl,),
        in_specs=[_full((16, D)), pl.BlockSpec((1, 16, cols), lambda l: (l, 0, 0)),
                  pl.BlockSpec((1, D, cols), lambda l: (l, 0, 0))],
        out_specs=[pl.BlockSpec((1, D, cols), lambda l: (l, 0, 0)), pl.BlockSpec((1, 16, D), lambda l: (l, 0, 0))],
        out_shape=[_sds((nl, D, cols), F32), _sds((nl, 16, D), F32)],
        compiler_params=_params(),
    )(c16, dm_cols, w_mod)


def _coords():
    return lax.axis_index("x"), lax.axis_index("y"), lax.axis_index("c")


def _peer(k, x, y, c):
    return (1 - x if k & 4 else x, 1 - y if k & 2 else y, 1 - c if k & 1 else c)


def _lin(p):
    return 4 * p[0] + 2 * p[1] + p[2]


def _view(ref, slot):
    return ref if slot is None else ref.at[slot]


class _Round:
    def __init__(self, ins, out_shapes, plan, local_plan=(), n_alias=0):
        self.ins, self.out_shapes = list(ins), list(out_shapes)
        self.plan, self.local_plan, self.n_alias = list(plan), list(local_plan), n_alias

    def sems(self):
        return [pltpu.SemaphoreType.DMA((len(self.plan),)), pltpu.SemaphoreType.DMA((len(self.plan),)),
                pltpu.SemaphoreType.DMA((max(len(self.local_plan), 1),))]

    def _copies(self, in_refs, out_refs, sems, incoming):
        in_refs = list(out_refs[: self.n_alias]) + list(in_refs[self.n_alias :])
        send_sems, recv_sems, loc_sems = sems
        x, y, c = _coords()
        me = _lin((x, y, c))
        remote = []
        for idx, (k, ii, sfn, oi, dfn) in enumerate(self.plan):
            peer = _peer(k, x, y, c)
            sender, receiver = (_lin(peer), me) if incoming else (me, _lin(peer))
            remote.append(pltpu.make_async_remote_copy(
                src_ref=_view(in_refs[ii], sfn(sender, receiver)), dst_ref=_view(out_refs[oi], dfn(sender, receiver)),
                send_sem=send_sems.at[idx], recv_sem=recv_sems.at[idx], device_id=peer, device_id_type=MESH))
        locs = [pltpu.make_async_copy(_view(in_refs[ii], sfn(me)), _view(out_refs[oi], dfn(me)), loc_sems.at[idx])
                for idx, (ii, sfn, oi, dfn) in enumerate(() if incoming else self.local_plan)]
        return remote, locs

    def start(self, in_refs, out_refs, sems):
        sends, locs = self._copies(in_refs, out_refs, sems, incoming=False)
        for cp in sends + locs:
            cp.start()

    def finish(self, in_refs, out_refs, sems):
        for cp in self._copies(in_refs, out_refs, sems, incoming=True)[0]:
            cp.wait_recv()
        sends, locs = self._copies(in_refs, out_refs, sems, incoming=False)
        for cp in sends:
            cp.wait_send()
        for cp in locs:
            cp.wait()


def _exchange(name, rnd):
    n_in, n_out = len(rnd.ins), len(rnd.out_shapes)

    def body(*refs):
        in_refs, out_refs, sems = refs[:n_in], refs[n_in : n_in + n_out], refs[n_in + n_out :]
        rnd.start(in_refs, out_refs, sems)
        rnd.finish(in_refs, out_refs, sems)

    return pl.pallas_call(
        body, name=name, in_specs=[ANY] * n_in, out_specs=[ANY] * n_out, out_shape=rnd.out_shapes,
        scratch_shapes=rnd.sems(), input_output_aliases={i: i for i in range(rnd.n_alias)})(*rnd.ins)


def _call(body, *, name, grid, in_specs, out_specs, out_shape, scratch_shapes, args, carry=None):
    params = _params(len(grid))
    if carry is None:
        outs = pl.pallas_call(body, name=name, grid=grid, in_specs=in_specs, out_specs=out_specs, out_shape=out_shape,
                              scratch_shapes=scratch_shapes, compiler_params=params)(*args)
        return list(outs), []
    n_ci, n_co, n_cs = len(in_specs), len(out_shape), len(scratch_shapes)
    n_xi, n_xo = len(carry.ins), len(carry.out_shapes)

    def wrapped(*refs):
        ci, xi = refs[:n_ci], refs[n_ci : n_ci + n_xi]
        o0 = n_ci + n_xi
        co, xo = refs[o0 : o0 + n_co], refs[o0 + n_co : o0 + n_co + n_xo]
        s0 = o0 + n_co + n_xo
        cs, sems = refs[s0 : s0 + n_cs], refs[s0 + n_cs :]
        ids = [pl.program_id(a) for a in range(len(grid))]
        first = functools.reduce(jnp.logical_and, [i == 0 for i in ids])
        last = functools.reduce(jnp.logical_and, [i == g - 1 for i, g in zip(ids, grid)])

        body(*ci, *co, *cs)

        @pl.when(first)
        def _():
            carry.start(xi, xo, sems)

        @pl.when(last)
        def _():
            carry.finish(xi, xo, sems)

    outs = pl.pallas_call(
        wrapped, name=name, grid=grid, in_specs=list(in_specs) + [ANY] * n_xi, out_specs=list(out_specs) + [ANY] * n_xo,
        out_shape=list(out_shape) + carry.out_shapes, scratch_shapes=list(scratch_shapes) + carry.sems(),
        input_output_aliases={n_ci + i: n_co + i for i in range(carry.n_alias)}, compiler_params=params,
    )(*args, *carry.ins)
    return list(outs[:n_co]), list(outs[n_co:])


def _gather_direct(arrays):
    na = len(arrays)
    outs = [_sds((NDEV,) + a.shape, a.dtype) for a in arrays]
    plan = [(k, i, lambda s, r: None, i, lambda s, r: s) for i in range(na) for k in range(1, NDEV)]
    return _Round(arrays, outs, plan, [(i, lambda m: None, i, lambda m: m) for i in range(na)])


def _gather_a(arrays):
    na = len(arrays)
    outs = [_sds((NDEV,) + a.shape, a.dtype) for a in arrays]
    plan = [(k, i, lambda s, r: None, i, lambda s, r: s) for i in range(na) for k in (1, 2, 4, 6)]
    return _Round(arrays, outs, plan, [(i, lambda m: None, i, lambda m: m) for i in range(na)])


def _gather_b(got):
    na = len(got)
    plan = [(1, i, (lambda s, r, k=k: s ^ k), i, (lambda s, r, k=k: s ^ k)) for i in range(na) for k in (2, 4, 6)]
    return _Round(got, [_sds(g.shape, g.dtype) for g in got], plan, n_alias=na)


def _scatter_1(grads):
    plan = [(1, i, (lambda s, r, q=q: 2 * q + (r & 1)), i, (lambda s, r, q=q: q))
            for i in range(len(grads)) for q in range(4)]
    return _Round(grads, [_sds((4,) + g.shape[1:], g.dtype) for g in grads], plan)


def _scatter_2(chip):
    plan = [(k, i, lambda s, r: r >> 1, i, (lambda s, r, j=j: j)) for i in range(len(chip)) for j, k in enumerate((2, 4, 6))]
    return _Round(chip, [_sds((3,) + g.shape[1:], g.dtype) for g in chip], plan)


def _add_pairs(g, got, pos, *, name):
    _, sh, w = g.shape

    def body(pos_ref, g_ref, r_ref, o_ref):
        o_ref[...] = (g_ref[...].astype(F32) + r_ref[...].astype(F32)).astype(o_ref.dtype)

    return pl.pallas_call(
        body,
        name=name,
        grid_spec=pltpu.PrefetchScalarGridSpec(
            num_scalar_prefetch=1, grid=(4,),
            in_specs=[pl.BlockSpec((1, sh, w), lambda q, p: (2 * q + p[0], 0, 0)),
                      pl.BlockSpec((1, sh, w), lambda q, p: (q, 0, 0))],
            out_specs=pl.BlockSpec((1, sh, w), lambda q, p: (q, 0, 0))),
        out_shape=_sds((4, sh, w), g.dtype),
        compiler_params=_params(),
    )(pos, g, got)


def _sum_chips(chip, got, pos, *, name):
    _, sh, w = chip.shape

    def body(pos_ref, c_ref, r_ref, o_ref):
        acc = c_ref[0].astype(F32)
        for s in range(3):
            acc = acc + r_ref[s].astype(F32)
        o_ref[...] = acc

    return pl.pallas_call(
        body,
        name=name,
        grid_spec=pltpu.PrefetchScalarGridSpec(
            num_scalar_prefetch=1, grid=(1,),
            in_specs=[pl.BlockSpec((1, sh, w), lambda i, p: (p[1], 0, 0)), pl.BlockSpec((3, sh, w), lambda i, p: (0, 0, 0))],
            out_specs=pl.BlockSpec((sh, w), lambda i, p: (0, 0))),
        out_shape=_sds((sh, w), F32),
        compiler_params=_params(),
    )(pos, chip, got)


def _adamw_math(w, g, m, v):
    m2 = ADAM_B1 * m + (1.0 - ADAM_B1) * g
    v2 = ADAM_B2 * v + (1.0 - ADAM_B2) * (g * g)
    m_hat = m2 / (1.0 - ADAM_B1 ** ADAM_STEP)
    v_hat = v2 / (1.0 - ADAM_B2 ** ADAM_STEP)
    delta = -ADAM_LR * (m_hat / (jnp.sqrt(v_hat) + ADAM_EPS) + ADAM_WD * w)
    return delta, m2, v2


def _adamw(w, g, m, v, *, name):
    shape = w.shape
    flat = [t.reshape(-1, shape[-1]) for t in (w, g, m, v)]
    rows, cols = flat[0].shape
    tr = rows // 8 if rows % 64 == 0 else rows
    spec = _rows(tr, cols)

    def body(w_ref, g_ref, m_ref, v_ref, d_ref, m2_ref, v2_ref):
        d_ref[...], m2_ref[...], v2_ref[...] = _adamw_math(w_ref[...], g_ref[...], m_ref[...], v_ref[...])

    outs = pl.pallas_call(
        body, name=name, grid=(rows // tr,), in_specs=[spec] * 4, out_specs=[spec] * 3,
        out_shape=[_sds((rows, cols), F32)] * 3, compiler_params=_params())(*flat)
    return tuple(o.reshape(shape) for o in outs)


def _small_sums(packets, nf, dwp, dsc, dsk, *, name):
    flat = [p for layer in packets for p in layer]

    def total(ref, *idx):
        acc = ref[(0,) + idx]
        for dev in range(1, NDEV):
            acc = acc + ref[(dev,) + idx]
        return acc

    def body(*refs):
        pk = refs[:6]
        nf_ref, dwp0, dwp1, dsc0, dsc1, dsk0, dsk1 = refs[6:13]
        dm_ref, gb_ref, gn_ref, gnf_ref, gwp_ref, gps_ref, gsk_ref = refs[13:]
        dm_ref[...] = jnp.zeros_like(dm_ref)
        gn_ref[...] = jnp.zeros_like(gn_ref)
        for l in range(2):
            for sb in range(3):
                p = pk[3 * l + sb]
                for r in range(3):
                    col = slice((3 * sb + r) * D, (3 * sb + r + 1) * D)
                    lat = p[0, 0, r : r + 1, :]
                    dm_ref[l, 0:1, col] = lat
                    for dev in range(1, NDEV):
                        row = p[dev, 0, r : r + 1, :]
                        dm_ref[l, dev : dev + 1, col] = row
                        lat = lat + row
                    ctx = total(p, 1, slice(r, r + 1), slice(None))
                    dm_ref[l, 8:9, col] = ctx
                    gb_ref[l : l + 1, col] = lat + ctx
                gn_ref[l, sb : sb + 1, :] = total(p, 0, slice(3, 4), slice(None)) + total(p, 1, slice(3, 4), slice(None))
        gnf_ref[...] = total(nf_ref, slice(0, 1), slice(None))
        for l, (a, b, c) in enumerate(((dwp0, dsc0, dsk0), (dwp1, dsc1, dsk1))):
            gwp_ref[l] = total(a, slice(None), slice(None))
            gps_ref[l : l + 1, :] = total(b, slice(0, 1), slice(None))
            gsk_ref[l] = total(c, slice(None), slice(None))

    ins = flat + [nf, dwp[0], dwp[1], dsc[0], dsc[1], dsk[0], dsk[1]]
    return pl.pallas_call(
        body,
        name=name,
        out_shape=[_sds((2, 16, NMOD * D), F32), _sds((2, NMOD * D), F32), _sds((2, 8, D), F32), _sds((1, D), F32),
                   _sds((2, PW, 128), F32), _sds((2, PW), F32), _sds((2, 8, 128), F32)],
        compiler_params=pltpu.CompilerParams(vmem_limit_bytes=VMEM_LIMIT),
    )(*ins)


def _small_adamw(c_ctx, dc_all, triples, *, name):
    n = len(triples)

    def body(*refs):
        c_ref, dc_ref = refs[0], refs[1]
        ins = refs[2 : 2 + 4 * n - 1]
        outs = refs[2 + 4 * n - 1 :]
        acc = dc_ref[0, 0, 8:9, :] + dc_ref[0, 1, 8:9, :]
        for dev in range(1, NDEV):
            acc = acc + (dc_ref[dev, 0, 8:9, :] + dc_ref[dev, 1, 8:9, :])
        c = c_ref[...]
        sig = _sigmoid(c)
        g_c = acc * (sig * (1.0 + c * (1.0 - sig)))
        outs[0][...] = g_c
        pos = 0
        for k in range(n):
            if k == 0:
                w, g, m, v = ins[0][...], g_c, ins[1][...], ins[2][...]
                pos = 3
            else:
                w, g, m, v = (ins[pos + t][...] for t in range(4))
                pos += 4
            d, m2, v2 = _adamw_math(w, g, m, v)
            outs[1 + 3 * k][...], outs[2 + 3 * k][...], outs[3 + 3 * k][...] = d, m2, v2

    flat_in = [c_ctx, dc_all]
    out_shape = [_sds(c_ctx.shape, F32)]
    for k, (w, g, m, v) in enumerate(triples):
        flat_in += [w, m, v] if k == 0 else [w, g, m, v]
        out_shape += [_sds(w.shape, F32)] * 3
    return pl.pallas_call(body, name=name, out_shape=out_shape,
                          compiler_params=pltpu.CompilerParams(vmem_limit_bytes=VMEM_LIMIT))(*flat_in)


def _rope_tables(T, R):
    t = jnp.arange(T)
    inv = ROPE_BASE ** (-jnp.arange(0, HD // 2, 2, dtype=F32) / (HD // 2))
    ang = jnp.concatenate([(t // GRID_W).astype(F32)[:, None] * inv, (t % GRID_W).astype(F32)[:, None] * inv], axis=-1)
    cos = jnp.concatenate([jnp.tile(jnp.cos(ang), (1, 4)), jnp.ones((R - T, 128), F32)], axis=0)
    sin = jnp.concatenate([jnp.tile(jnp.sin(ang), (1, 4)), jnp.zeros((R - T, 128), F32)], axis=0)
    return cos, sin


def kernel(x, c, ctx, c_ctx, w_mod, b_mod, norm_ffn1, w_ffn1_in, w_ffn1_out, norm_mix, w_in, w_pool, pool_scale, sink, w_out, norm_ffn2, w_ffn2_in, w_ffn2_out, norm_final, loss_target, m_c_ctx, m_w_mod, m_b_mod, m_norm_ffn1, m_w_ffn1_in, m_w_ffn1_out, m_norm_mix, m_w_in, m_w_pool, m_pool_scale, m_sink, m_w_out, m_norm_ffn2, m_w_ffn2_in, m_w_ffn2_out, m_norm_final, v_c_ctx, v_w_mod, v_b_mod, v_norm_ffn1, v_w_ffn1_in, v_w_ffn1_out, v_norm_mix, v_w_in, v_w_pool, v_pool_scale, v_sink, v_w_out, v_norm_ffn2, v_w_ffn2_in, v_w_ffn2_out, v_norm_final):
    T = x.shape[1]
    R = T + LC
    nl = w_mod.shape[0]
    cx, cy, cc = _coords()
    me = _lin((cx, cy, cc))
    pos = jnp.stack([cc, 2 * cx + cy]).astype(jnp.int32)
    mcols = w_mod.shape[2]

    shards = [([w_ffn1_in[l].T.astype(BF16), w_ffn1_out[l].astype(BF16)],
               [w_in[l].T.astype(BF16), w_out[l].astype(BF16)],
               [w_ffn2_in[l].T.astype(BF16), w_ffn2_out[l].astype(BF16)]) for l in range(nl)]

    got = _exchange("ag_c_w", _merge(_gather_direct([c]), _gather_a(shards[0][0] + shards[0][1])))
    c_all, w_first = got[0], got[1:]
    c16 = jnp.concatenate([c_all.reshape(NDEV, D), c_ctx[None], jnp.zeros((16 - NDEV - 1, D), F32)], axis=0)
    b_cols = lax.dynamic_slice(b_mod, (0, me * mcols), (nl, mcols)).reshape(nl, 1, mcols)
    got = _exchange("ag_mod_w", _merge(_gather_b(w_first), _gather_direct([_mod_fwd(c16, w_mod, b_cols, name="mod_fwd")])))
    w_first, mod_all = got[:4], got[4]
    mod_all = jnp.transpose(mod_all, (1, 2, 0, 3)).reshape(nl, 16, NMOD, D)
    mine = lax.dynamic_index_in_dim(mod_all, me, axis=1, keepdims=False)
    pad = jnp.zeros((nl, 16 - NMOD, D), F32)
    modv = jnp.stack([jnp.concatenate([mine, pad], axis=1), jnp.concatenate([mod_all[:, 8], pad], axis=1)], axis=1)

    gvec = [jnp.concatenate([norm_ffn1[l][None], norm_mix[l][None], norm_ffn2[l][None], jnp.zeros((5, D), F32)], axis=0)
            for l in range(nl)]
    cos, sin = _rope_tables(T, R)
    ps2 = [pool_scale[l][None] for l in range(nl)]

    h = jnp.concatenate([x[0], ctx[0]], axis=0)
    loss_part, dh, small, nf_all, big = _forward_backward(
        h, loss_target[0], modv, gvec, shards, w_first, cos, sin, sink, w_pool, ps2, norm_final, pos, T=T)
    loss = lax.psum(loss_part[0, 0], ("x", "y", "c"))
    grad_x = dh[:T][None]

    dm, g_b_mod, g_norms, g_nf, g_wp, g_ps, g_sk = _small_sums(
        [small[l][0:3] for l in range(nl)], nf_all, *[[small[l][k] for l in range(nl)] for k in (3, 4, 5)],
        name="small_sums")
    dm_cols = lax.dynamic_slice(dm, (0, 0, me * mcols), (nl, 16, mcols))
    g_w_mod, dc_part = _mod_bwd(c16, dm_cols, w_mod, name="mod_bwd")
    (dc_all,) = _exchange("ag_dc", _gather_direct([dc_part]))

    grads = {
        "b_mod": g_b_mod, "norm_ffn1": g_norms[:, 0], "norm_mix": g_norms[:, 1], "norm_ffn2": g_norms[:, 2],
        "w_pool": g_wp.reshape(w_pool.shape), "pool_scale": g_ps, "sink": g_sk[:, :, 0], "norm_final": g_nf.reshape(D),
        "w_mod": g_w_mod,
        "w_ffn1_in": jnp.stack([big[l][0].T for l in range(nl)]), "w_ffn1_out": jnp.stack([big[l][1] for l in range(nl)]),
        "w_in": jnp.stack([big[l][2].T for l in range(nl)]), "w_out": jnp.stack([big[l][3] for l in range(nl)]),
        "w_ffn2_in": jnp.stack([big[l][4].T for l in range(nl)]), "w_ffn2_out": jnp.stack([big[l][5] for l in range(nl)]),
    }
    weights = dict(c_ctx=c_ctx, w_mod=w_mod, b_mod=b_mod, norm_ffn1=norm_ffn1, w_ffn1_in=w_ffn1_in, w_ffn1_out=w_ffn1_out,
                   norm_mix=norm_mix, w_in=w_in, w_pool=w_pool, pool_scale=pool_scale, sink=sink, w_out=w_out,
                   norm_ffn2=norm_ffn2, w_ffn2_in=w_ffn2_in, w_ffn2_out=w_ffn2_out, norm_final=norm_final)
    moms = dict(c_ctx=(m_c_ctx, v_c_ctx), w_mod=(m_w_mod, v_w_mod), b_mod=(m_b_mod, v_b_mod),
                norm_ffn1=(m_norm_ffn1, v_norm_ffn1), w_ffn1_in=(m_w_ffn1_in, v_w_ffn1_in),
                w_ffn1_out=(m_w_ffn1_out, v_w_ffn1_out), norm_mix=(m_norm_mix, v_norm_mix), w_in=(m_w_in, v_w_in),
                w_pool=(m_w_pool, v_w_pool), pool_scale=(m_pool_scale, v_pool_scale), sink=(m_sink, v_sink),
                w_out=(m_w_out, v_w_out), norm_ffn2=(m_norm_ffn2, v_norm_ffn2), w_ffn2_in=(m_w_ffn2_in, v_w_ffn2_in),
                w_ffn2_out=(m_w_ffn2_out, v_w_ffn2_out), norm_final=(m_norm_final, v_norm_final))
    order = list(weights)
    small_names = ["c_ctx", "b_mod", "norm_ffn1", "norm_mix", "w_pool", "pool_scale", "sink", "norm_ffn2", "norm_final"]

    def as2d(name, t):
        if name == "w_pool":
            return t.reshape(-1, 128)
        return t.reshape(1, -1) if t.ndim == 1 else t

    triples = [(as2d(n, weights[n]), None if n == "c_ctx" else as2d(n, grads[n]), as2d(n, moms[n][0]), as2d(n, moms[n][1]))
               for n in small_names]
    outs = _small_adamw(as2d("c_ctx", c_ctx), dc_all, triples, name="small_adamw")
    grads["c_ctx"] = outs[0].reshape(c_ctx.shape)
    delta, new_m, new_v = {}, {}, {}
    for k, n in enumerate(small_names):
        delta[n], new_m[n], new_v[n] = (o.reshape(weights[n].shape) for o in outs[1 + 3 * k : 4 + 3 * k])
    for n in order:
        if n not in small_names:
            delta[n], new_m[n], new_v[n] = _adamw(weights[n], grads[n], moms[n][0], moms[n][1], name=f"adamw_{n}")

    return (loss, grad_x, *[grads[n] for n in order], *[delta[n] for n in order],
            *[new_m[n] for n in order], *[new_v[n] for n in order])


def _merge(*rounds):
    ins, outs, plan, local, n_alias = [], [], [], [], 0
    for r in rounds:
        assert r.n_alias == 0 or (not ins and r.n_alias == len(r.ins) == len(r.out_shapes))
        oi, oo = len(ins), len(outs)
        plan += [(k, i + oi, sf, o + oo, df) for k, i, sf, o, df in r.plan]
        local += [(i + oi, sf, o + oo, df) for i, sf, o, df in r.local_plan]
        ins += r.ins
        outs += r.out_shapes
        n_alias += r.n_alias
    return _Round(ins, outs, plan, local, n_alias)


def _forward_backward(h, target, modv, gvec, shards, w_first, cos, sin, sink, w_pool, ps2, norm_final, pos, *, T):
    nl = len(gvec)
    flat = lambda ws: [w.reshape(-1, D) for w in ws]
    saved = []
    w1, wm = flat(w_first[:2]), flat(w_first[2:])
    for l in range(nl):
        last = l == nl - 1
        h0 = h
        if l == 0:
            (h1, a1, b1, f1), got = _ffn_fwd(h0, modv[l], gvec[l], *w1, T=T, mrow=0, grow=0, ctx_active=True,
                                             name=f"ffn1_fwd_{l}", carry=_gather_a(shards[l][2]))
            (u, q, k4, v4), got = _mixproj_fwd(h1, modv[l], gvec[l], wm[0], cos, sin, T=T, name=f"mixproj_fwd_{l}",
                                               carry=_gather_b(got))
            w2 = flat(got)
        else:
            (h1, a1, b1, f1), got = _ffn_fwd(h0, modv[l], gvec[l], *w1, T=T, mrow=0, grow=0, ctx_active=True,
                                             name=f"ffn1_fwd_{l}", carry=_gather_b(nxt_m + nxt_2))
            wm, w2 = flat(got[:2]), flat(got[2:])
            (u, q, k4, v4), _ = _mixproj_fwd(h1, modv[l], gvec[l], wm[0], cos, sin, T=T, name=f"mixproj_fwd_{l}")
        (cat,), nxt_1 = _attnpool_fwd(u, q, k4, v4, sink[l], w_pool[l], ps2[l], T=T, name=f"attnpool_fwd_{l}",
                                      carry=None if last else _gather_a(shards[l + 1][0]))
        (h2, mo), nxt_m = _mixout_fwd(h1, cat, modv[l], wm[1], T=T, ctx_active=not last, name=f"mixout_fwd_{l}",
                                      carry=None if last else _gather_a(shards[l + 1][1]))
        (h3, a2, b2, f2), got = _ffn_fwd(h2, modv[l], gvec[l], *w2, T=T, mrow=6, grow=2, ctx_active=not last,
                                         name=f"ffn2_fwd_{l}",
                                         carry=None if last else _merge(_gather_b(nxt_1), _gather_a(shards[l + 1][2])))
        saved.append((h0, a1, b1, f1, h1, u, q, k4, v4, cat, mo, h2, a2, b2, f2, w1, wm, w2))
        h = h3
        if not last:
            w1, nxt_2 = flat(got[:2]), got[2:]

    dh, loss_part, dnf = _loss_head(h, target, norm_final[None], T=T, name="loss_head")

    def adds(tag, grads, got):
        return [_add_pairs(g, r, pos, name=f"rs_add_{tag}_{i}") for i, (g, r) in enumerate(zip(grads, got))]

    def totals(tag, chip, got):
        return [_sum_chips(c_, r, pos, name=f"rs_sum_{tag}_{i}") for i, (c_, r) in enumerate(zip(chip, got))]

    small, big = [None] * nl, {}
    prev = None
    for l in reversed(range(nl)):
        last = l == nl - 1
        h0, a1, b1, f1, h1, u, q, k4, v4, cat, mo, h2, a2, b2, f2, w1, wm, w2 = saved[l]
        (dh, dab, s, n, df, pk2), got = _ffn_bwd(
            h2, dh, a2, b2, f2, modv[l], gvec[l], *w2, T=T, mrow=6, grow=2, ctx_active=not last, name=f"ffn2_bwd_{l}",
            carry=_merge(_scatter_1(prev[0]), _gather_a(prev[1])) if prev else None)
        if prev:
            c1, small_a = adds(f"ffn1_{l + 1}", prev[0], got[:2]), got[2:]
        g_w2i, got = _wgrad(dab, n, bk=WG_BK, sh=2 * DFF // NDEV, name=f"wgrad_ffn2_in_{l}",
                            carry=_scatter_2(c1[:1]) if prev else None)
        if prev:
            big[l + 1][0:1] = totals(f"ffn1_in_{l + 1}", c1[:1], got)
        g_w2o, got = _wgrad(s, df, bk=WG_BK, sh=DFF // NDEV, name=f"wgrad_ffn2_out_{l}",
                            carry=_scatter_2(c1[1:]) if prev else None)
        if prev:
            big[l + 1][1:2] = totals(f"ffn1_out_{l + 1}", c1[1:], got)
        rnd = _scatter_1([g_w2i, g_w2o])
        (dcat, dmix, pko), got = _mixout_bwd(dh, mo, modv[l], wm[1], T=T, ctx_active=not last, name=f"mixout_bwd_{l}",
                                             carry=_merge(_gather_b(small_a), rnd) if prev else rnd)
        if prev:
            small[l + 1], got = got[: len(small_a)], got[len(small_a) :]
        c2 = adds(f"ffn2_{l}", [g_w2i, g_w2o], got)
        g_wo, _ = _wgrad(cat, dmix, bk=D, sh=D // NDEV, name=f"wgrad_out_{l}")
        dps, dwp, dsc = _pool_bwd(u, dcat, w_pool[l], ps2[l], T=T, name=f"pool_bwd_{l}")
        (du, dq, dk, dv, dsk), got = _attn_bwd(q, k4, v4, dcat, dps, sink[l], T=T, name=f"attn_bwd_{l}", carry=_scatter_2(c2))
        big[l] = [None, None, None, None] + totals(f"ffn2_{l}", c2, got)
        dh, dproj, n, pkm = _mixproj_bwd(h1, dh, du, dq, dk, dv, modv[l], gvec[l], wm[0], cos, sin, T=T, name=f"mixproj_bwd_{l}")
        g_wi, _ = _wgrad(dproj, n, bk=PROJ, sh=PROJ // NDEV, name=f"wgrad_in_{l}")
        (dh, dab, s, n, df, pk1), got = _ffn_bwd(h0, dh, a1, b1, f1, modv[l], gvec[l], *w1, T=T, mrow=0, grow=0,
                                                 ctx_active=True, name=f"ffn1_bwd_{l}", carry=_scatter_1([g_wi, g_wo]))
        cm = adds(f"mix_{l}", [g_wi, g_wo], got)
        g_w1i, got = _wgrad(dab, n, bk=WG_BK, sh=2 * DFF // NDEV, name=f"wgrad_ffn1_in_{l}", carry=_scatter_2(cm))
        big[l][2:4] = totals(f"mix_{l}", cm, got)
        g_w1o, _ = _wgrad(s, df, bk=WG_BK, sh=DFF // NDEV, name=f"wgrad_ffn1_out_{l}")
        prev = ([g_w1i, g_w1o], [pk1, pkm + pko, pk2, dwp, dsc, dsk])
    got = _exchange("rs1_tail", _merge(_scatter_1(prev[0]), _gather_a(prev[1] + [dnf])))
    c1, small_a = adds("ffn1_0", prev[0], got[:2]), got[2:]
    got = _exchange("rs2_tail", _merge(_gather_b(small_a), _scatter_2(c1)))
    small[0], nf_all = got[: len(small_a) - 1], got[len(small_a) - 1]
    big[0][0:2] = totals("ffn1_0", c1, got[len(small_a) :])
    return loss_part, dh, small, nf_all, big
```

```python
import functools

import jax
import jax.numpy as jnp
from jax import lax
from jax.experimental import pallas as pl
from jax.experimental.pallas import tpu as pltpu

F32, BF16 = jnp.float32, jnp.bfloat16

D = 1024
LC = 256
DFF = 2816
NMOD = 9
PW = 512
AW = 512
KVW = 128
PROJ = PW + AW + 2 * KVW
HD = 64
BLK = 128
GRID_W = 64
POOL_WINDOWS = (2, 4, 8, 16)
EPS = 1e-6
NEG = -1e30
ROPE_BASE = 10000.0
NDEV = 8
MESH = pl.DeviceIdType.MESH

ADAM_LR, ADAM_B1, ADAM_B2, ADAM_EPS, ADAM_WD, ADAM_STEP = 0.001, 0.9, 0.999, 1e-08, 0.01, 10

VMEM_LIMIT = 56 * 1024 * 1024
TM = 256
FFN_CHUNKS = ((0, 1536), (1536, 1280))
WG_BK = 1408

ANY = pl.BlockSpec(memory_space=pl.ANY)
SMEM = pl.BlockSpec(memory_space=pltpu.SMEM)


def _params(ngrid=1):
    return pltpu.CompilerParams(dimension_semantics=("arbitrary",) * ngrid, vmem_limit_bytes=VMEM_LIMIT)


def _dot(a, b):
    return jnp.dot(a, b, preferred_element_type=F32)


def _dot_nt(a, b):
    return lax.dot_general(a, b, (((1,), (1,)), ((), ())), preferred_element_type=F32)


def _dot_tn(a, b):
    return lax.dot_general(a, b, (((0,), (0,)), ((), ())), preferred_element_type=F32)


def _sigmoid(x):
    return 1.0 / (1.0 + jnp.exp(-x))


def _rows(tm, w):
    return pl.BlockSpec((tm, w), lambda i: (i, 0))


def _full(shape):
    nd = len(shape)
    return pl.BlockSpec(shape, lambda *_: (0,) * nd)


def _sds(shape, dtype):
    return jax.ShapeDtypeStruct(shape, dtype)


def _norm_mod(h, g, shift, scale):
    r = lax.rsqrt(jnp.mean(h * h, axis=-1, keepdims=True) + EPS)
    xhat = h * r
    y = xhat * g
    return r, xhat, y, y * (1.0 + scale) + shift


def _norm_mod_bwd(dn, r, xhat, y, g, scale):
    dshift = jnp.sum(dn, axis=0, keepdims=True)
    dscale = jnp.sum(dn * y, axis=0, keepdims=True)
    dy = dn * (1.0 + scale)
    dg = jnp.sum(dy * xhat, axis=0, keepdims=True)
    dxh = dy * g
    dh = r * (dxh - xhat * jnp.mean(dxh * xhat, axis=-1, keepdims=True))
    return dh, dshift, dscale, dg


def _acc_partials(part_ref, first, rows):
    @pl.when(first)
    def _():
        part_ref[...] = jnp.zeros_like(part_ref)

    for r, val in rows.items():
        part_ref[0, r : r + 1, :] += val


def _mod_spec(n_lat):
    return pl.BlockSpec((1, 16, D), lambda i: (i // n_lat, 0, 0))


def _part_spec(n_lat):
    return pl.BlockSpec((1, 8, D), lambda i: (i // n_lat, 0, 0))


def _load_weights(pairs, sem):
    copies = [pltpu.make_async_copy(src, dst, sem.at[k]) for k, (src, dst) in enumerate(pairs)]
    for cp in copies:
        cp.start()
    for cp in copies:
        cp.wait()


def _ffn_weight_copies(win_hbm, wout_hbm, win_v, wout_v, sem):
    loads = []
    for k, (c0, cw) in enumerate(FFN_CHUNKS):
        slabs = [(win_hbm, win_v, c0), (win_hbm, win_v, DFF + c0), (wout_hbm, wout_v, c0)]
        loads.append([pltpu.make_async_copy(src.at[pl.ds(r0, cw)], dst.at[pl.ds(r0, cw)], sem.at[3 * k + j])
                      for j, (src, dst, r0) in enumerate(slabs)])
    return loads


def _ffn_steps(i, n_active, loads, compute):
    @pl.when(i == 0)
    def _():
        for cp in sum(loads, []):
            cp.start()
        compute(loads)

    @pl.when(jnp.logical_and(i > 0, i < n_active))
    def _():
        compute(None)


def _wait_chunk(loads, k):
    if loads is not None:
        for cp in loads[k]:
            cp.wait()


def _ffn_fwd(h, modv, gvec, win, wout, *, T, mrow, grow, ctx_active, name, carry=None):
    R = h.shape[0]
    n_lat, n_tiles = T // TM, R // TM
    n_active = n_tiles if ctx_active else n_lat

    def body(h_ref, mod_ref, g_ref, win_hbm, wout_hbm, ho_ref, a_ref, b_ref, f_ref, win_v, wout_v, sem):
        i = pl.program_id(0)

        def compute(loads):
            h = h_ref[...]
            shift, scale, gate = (mod_ref[0, mrow + k : mrow + k + 1, :] for k in range(3))
            _, _, _, n = _norm_mod(h, g_ref[grow : grow + 1, :], shift, scale)
            n_bf = n.astype(BF16)
            acc = jnp.zeros((TM, D), F32)
            for k, (c0, cw) in enumerate(FFN_CHUNKS):
                _wait_chunk(loads, k)
                a = _dot_nt(n_bf, win_v[c0 : c0 + cw, :])
                b = _dot_nt(n_bf, win_v[DFF + c0 : DFF + c0 + cw, :])
                a_ref[:, c0 : c0 + cw] = a.astype(BF16)
                b_ref[:, c0 : c0 + cw] = b.astype(BF16)
                s = a * _sigmoid(a) * b
                acc = acc + _dot(s.astype(BF16), wout_v[c0 : c0 + cw, :])
            f_ref[...] = acc.astype(BF16)
            ho_ref[...] = h + (0.5 * gate) * acc

        _ffn_steps(i, n_active, _ffn_weight_copies(win_hbm, wout_hbm, win_v, wout_v, sem), compute)

        @pl.when(i >= n_active)
        def _():
            ho_ref[...] = h_ref[...]
            a_ref[...] = jnp.zeros_like(a_ref)
            b_ref[...] = jnp.zeros_like(b_ref)
            f_ref[...] = jnp.zeros_like(f_ref)

    return _call(
        body,
        name=name,
        grid=(n_tiles,),
        in_specs=[_rows(TM, D), _mod_spec(n_lat), _full((8, D)), ANY, ANY],
        out_specs=[_rows(TM, D), _rows(TM, DFF), _rows(TM, DFF), _rows(TM, D)],
        out_shape=[_sds((R, D), F32), _sds((R, DFF), BF16), _sds((R, DFF), BF16), _sds((R, D), BF16)],
        scratch_shapes=[pltpu.VMEM((2 * DFF, D), BF16), pltpu.VMEM((DFF, D), BF16),
                        pltpu.SemaphoreType.DMA((3 * len(FFN_CHUNKS),))],
        args=(h, modv, gvec, win, wout),
        carry=carry,
    )


def _ffn_bwd(h, dho, a, b, f, modv, gvec, win, wout, *, T, mrow, grow, ctx_active, name, carry=None):
    R = h.shape[0]
    n_lat, n_tiles = T // TM, R // TM
    n_active = n_tiles if ctx_active else n_lat

    def body(h_ref, dho_ref, a_ref, b_ref, f_ref, mod_ref, g_ref, win_hbm, wout_hbm,
             dh_ref, dab_ref, s_ref, n_ref, df_ref, part_ref, win_v, wout_v, sem):
        i = pl.program_id(0)
        first = jnp.logical_or(i == 0, i == n_lat)

        def compute(loads):
            h = h_ref[...]
            dho = dho_ref[...]
            shift, scale, gate = (mod_ref[0, mrow + k : mrow + k + 1, :] for k in range(3))
            g = g_ref[grow : grow + 1, :]
            r, xhat, y, n = _norm_mod(h, g, shift, scale)
            dgate = 0.5 * jnp.sum(dho * f_ref[...].astype(F32), axis=0, keepdims=True)
            df_bf = ((0.5 * gate) * dho).astype(BF16)
            df_ref[...] = df_bf
            n_ref[...] = n.astype(BF16)
            dn = jnp.zeros((TM, D), F32)
            for k, (c0, cw) in enumerate(FFN_CHUNKS):
                _wait_chunk(loads, k)
                ds = _dot_nt(df_bf, wout_v[c0 : c0 + cw, :])
                av = a_ref[:, c0 : c0 + cw].astype(F32)
                bv = b_ref[:, c0 : c0 + cw].astype(F32)
                sig = _sigmoid(av)
                sa = av * sig
                s_ref[:, c0 : c0 + cw] = (sa * bv).astype(BF16)
                da = (ds * bv * (sig * (1.0 + av * (1.0 - sig)))).astype(BF16)
                db = (ds * sa).astype(BF16)
                dab_ref[:, c0 : c0 + cw] = da
                dab_ref[:, DFF + c0 : DFF + c0 + cw] = db
                dn = dn + _dot(da, win_v[c0 : c0 + cw, :]) + _dot(db, win_v[DFF + c0 : DFF + c0 + cw, :])
            dh, dshift, dscale, dg = _norm_mod_bwd(dn, r, xhat, y, g, scale)
            dh_ref[...] = dho + dh
            _acc_partials(part_ref, first, {0: dshift, 1: dscale, 2: dgate, 3: dg})

        _ffn_steps(i, n_active, _ffn_weight_copies(win_hbm, wout_hbm, win_v, wout_v, sem), compute)

        @pl.when(i >= n_active)
        def _():
            dh_ref[...] = dho_ref[...]
            dab_ref[...] = jnp.zeros_like(dab_ref)
            s_ref[...] = jnp.zeros_like(s_ref)
            n_ref[...] = jnp.zeros_like(n_ref)
            df_ref[...] = jnp.zeros_like(df_ref)
            part_ref[...] = jnp.zeros_like(part_ref)

    return _call(
        body,
        name=name,
        grid=(n_tiles,),
        in_specs=[_rows(TM, D), _rows(TM, D), _rows(TM, DFF), _rows(TM, DFF), _rows(TM, D),
                  _mod_spec(n_lat), _full((8, D)), ANY, ANY],
        out_specs=[_rows(TM, D), _rows(TM, 2 * DFF), _rows(TM, DFF), _rows(TM, D), _rows(TM, D), _part_spec(n_lat)],
        out_shape=[_sds((R, D), F32), _sds((R, 2 * DFF), BF16), _sds((R, DFF), BF16), _sds((R, D), BF16),
                   _sds((R, D), BF16), _sds((2, 8, D), F32)],
        scratch_shapes=[pltpu.VMEM((2 * DFF, D), BF16), pltpu.VMEM((DFF, D), BF16),
                        pltpu.SemaphoreType.DMA((3 * len(FFN_CHUNKS),))],
        args=(h, dho, a, b, f, modv, gvec, win, wout),
        carry=carry,
    )


def _wgrad(x, y, *, bk, sh, name, carry=None):
    R, kx = x.shape
    n = y.shape[1]
    tr = R // 2
    nr, nsh = R // tr, bk // sh

    def body(x_ref, y_ref, o_ref, acc):
        r = pl.program_id(1)

        @pl.when(r == 0)
        def _():
            acc[...] = jnp.zeros_like(acc)

        acc[...] += _dot_tn(x_ref[...], y_ref[...])

        @pl.when(r == nr - 1)
        def _():
            for s in range(nsh):
                o_ref[s] = acc[s * sh : (s + 1) * sh, :].astype(BF16)

    (out,), got = _call(
        body,
        name=name,
        grid=(kx // bk, nr),
        in_specs=[pl.BlockSpec((tr, bk), lambda k, r: (r, k)), pl.BlockSpec((tr, n), lambda k, r: (r, 0))],
        out_specs=[pl.BlockSpec((nsh, sh, n), lambda k, r: (k, 0, 0))],
        out_shape=[_sds((kx // sh, sh, n), BF16)],
        scratch_shapes=[pltpu.VMEM((bk, n), F32)],
        args=(x, y),
        carry=carry,
    )
    return out, got


def _rot_half(x):
    lane = lax.broadcasted_iota(jnp.int32, x.shape, 1)
    return jnp.where((lane & (HD - 1)) < HD // 2, -pltpu.roll(x, 128 - HD // 2, 1), pltpu.roll(x, HD // 2, 1))


def _tile_sel():
    i = lax.broadcasted_iota(jnp.int32, (KVW, AW), 0)
    j = lax.broadcasted_iota(jnp.int32, (KVW, AW), 1)
    return jnp.where(i == (j // 256) * HD + (j & (HD - 1)), 1.0, 0.0).astype(BF16)


def _mixproj_fwd(h, modv, gvec, win, cos, sin, *, T, name, carry=None):
    R = h.shape[0]
    n_lat, n_tiles = T // TM, R // TM

    def body(h_ref, mod_ref, g_ref, win_ref, cos_ref, sin_ref, u_ref, q_ref, k4_ref, v4_ref):
        shift, scale = mod_ref[0, 3:4, :], mod_ref[0, 4:5, :]
        _, _, _, n = _norm_mod(h_ref[...], g_ref[1:2, :], shift, scale)
        proj = _dot_nt(n.astype(BF16), win_ref[...])
        u_ref[...] = proj[:, :PW]
        cs, sn = cos_ref[...], sin_ref[...]
        for s in range(AW // 128):
            x = proj[:, PW + 128 * s : PW + 128 * (s + 1)]
            q_ref[:, 128 * s : 128 * (s + 1)] = (x * cs + _rot_half(x) * sn).astype(BF16)
        k = proj[:, PW + AW : PW + AW + KVW]
        k = (k * cs + _rot_half(k) * sn).astype(BF16)
        v = proj[:, PW + AW + KVW :].astype(BF16)
        sel = _tile_sel()
        k4_ref[...] = _dot(k, sel).astype(BF16)
        v4_ref[...] = _dot(v, sel).astype(BF16)

    return _call(
        body,
        name=name,
        grid=(n_tiles,),
        in_specs=[_rows(TM, D), _mod_spec(n_lat), _full((8, D)), _full((PROJ, D)), _rows(TM, 128), _rows(TM, 128)],
        out_specs=[_rows(TM, PW), _rows(TM, AW), _rows(TM, AW), _rows(TM, AW)],
        out_shape=[_sds((R, PW), F32), _sds((R, AW), BF16), _sds((R, AW), BF16), _sds((R, AW), BF16)],
        scratch_shapes=[],
        args=(h, modv, gvec, win, cos, sin),
        carry=carry,
    )


def _win_start(j, hi):
    return pl.multiple_of(jnp.clip((j - 1) * BLK, 0, hi - 3 * BLK), BLK)


def _hi_lo(x):
    hi = x.astype(BF16)
    return hi, (x - hi.astype(F32)).astype(BF16)


def _pool_bounds(t, w, T, R):
    is_ctx = t >= T
    lo = jnp.maximum(t - w // 2, jnp.where(is_ctx, T, 0))
    hi = jnp.minimum(t + w // 2, jnp.where(is_ctx, R, T))
    return lo, hi


def _pooled(u_v, j, T, R):
    start = _win_start(j, R)
    u3_hi, u3_lo = _hi_lo(u_v[pl.ds(start, 3 * BLK), :])
    ub = u_v[pl.ds(pl.multiple_of(j * BLK, BLK), BLK), :]
    t = j * BLK + lax.broadcasted_iota(jnp.int32, (BLK, 1), 0)
    pos = start + lax.broadcasted_iota(jnp.int32, (1, 3 * BLK), 1)
    pooled, counts = [], []
    for g, w in enumerate(POOL_WINDOWS):
        lo, hi = _pool_bounds(t, w, T, R)
        band = jnp.where(pos >= lo, jnp.where(pos < hi, 1.0, 0.0), 0.0).astype(BF16)
        sl = slice(g * 128, (g + 1) * 128)
        sums = _dot(band, u3_hi[:, sl]) + _dot(band, u3_lo[:, sl])
        cnt = (hi - lo).astype(F32)
        pooled.append(sums / cnt - ub[:, sl])
        counts.append(cnt)
    return pooled, counts


def _stack_heads(x):
    lane_h = lax.broadcasted_iota(jnp.int32, x.shape, 1) // HD
    return jnp.concatenate([jnp.where(lane_h == h, x, jnp.zeros_like(x)) for h in range(4)], axis=0)


def _unstack_heads(x):
    lane_h = lax.broadcasted_iota(jnp.int32, (BLK, 256), 1) // HD
    out = jnp.zeros((BLK, 256), F32)
    for h in range(4):
        out = out + jnp.where(lane_h == h, x[h * BLK : (h + 1) * BLK, :], 0.0)
    return out


def _attn_probs(qs, kl, kc, sink_ref, g, j, start_l, nbl):
    s_l = _dot_nt(qs, kl) * (HD ** -0.5)
    s_c = _dot_nt(qs, kc) * (HD ** -0.5)
    rowi = lax.broadcasted_iota(jnp.int32, (4 * BLK, 1), 0)
    qpos = j * BLK + (rowi & (BLK - 1))
    kpos = start_l + lax.broadcasted_iota(jnp.int32, (1, 3 * BLK), 1)
    reach = jnp.where(j < nbl, BLK, -1)
    s_l = jnp.where(jnp.abs(kpos - qpos) <= reach, s_l, NEG)
    rb = rowi // BLK
    sk = jnp.where(rb == 0, sink_ref[4 * g], jnp.where(rb == 1, sink_ref[4 * g + 1],
                   jnp.where(rb == 2, sink_ref[4 * g + 2], sink_ref[4 * g + 3])))
    m = jnp.maximum(jnp.maximum(jnp.max(s_l, axis=1, keepdims=True), jnp.max(s_c, axis=1, keepdims=True)), sk)
    e_l, e_c, e_s = jnp.exp(s_l - m), jnp.exp(s_c - m), jnp.exp(sk - m)
    inv = 1.0 / (jnp.sum(e_l, axis=1, keepdims=True) + jnp.sum(e_c, axis=1, keepdims=True) + e_s)
    return e_l * inv, e_c * inv, e_s * inv


def _attnpool_fwd(u, q, k4, v4, sink, w_pool, pool_scale, *, T, name, carry=None):
    R = u.shape[0]
    nb, nbl = R // BLK, T // BLK

    def body(q_ref, sink_ref, wp_ref, ps_ref, u_hbm, k4_hbm, v4_hbm, cat_ref, u_v, k4_v, v4_v, sem):
        j = pl.program_id(0)

        @pl.when(j == 0)
        def _():
            _load_weights([(u_hbm, u_v), (k4_hbm, k4_v), (v4_hbm, v4_v)], sem)

        pooled, _ = _pooled(u_v, j, T, R)
        for g in range(4):
            mixed = _dot(pooled[g].astype(BF16), wp_ref[g].astype(BF16)) * ps_ref[:, g * 128 : (g + 1) * 128]
            cat_ref[:, g * 128 : (g + 1) * 128] = mixed.astype(BF16)

        start_l = _win_start(j, T)
        for g in range(2):
            gl = slice(g * 256, (g + 1) * 256)
            qs = _stack_heads(q_ref[:, gl])
            p_l, p_c, _ = _attn_probs(qs, k4_v[pl.ds(start_l, 3 * BLK), gl], k4_v[T:R, gl], sink_ref, g, j, start_l, nbl)
            o = _dot(p_l.astype(BF16), v4_v[pl.ds(start_l, 3 * BLK), gl]) + _dot(p_c.astype(BF16), v4_v[T:R, gl])
            cat_ref[:, PW + g * 256 : PW + (g + 1) * 256] = _unstack_heads(o).astype(BF16)

    return _call(
        body,
        name=name,
        grid=(nb,),
        in_specs=[_rows(BLK, AW), SMEM, _full((4, 128, 128)), _full((1, PW)), ANY, ANY, ANY],
        out_specs=[_rows(BLK, D)],
        out_shape=[_sds((R, D), BF16)],
        scratch_shapes=[pltpu.VMEM((R, PW), F32), pltpu.VMEM((R, AW), BF16), pltpu.VMEM((R, AW), BF16),
                        pltpu.SemaphoreType.DMA((3,))],
        args=(q, sink, w_pool, pool_scale, u, k4, v4),
        carry=carry,
    )


def _mixout_fwd(h, cat, modv, wout, *, T, ctx_active, name, carry=None):
    R = h.shape[0]
    n_lat, n_tiles = T // TM, R // TM

    def body(h_ref, cat_ref, mod_ref, w_ref, ho_ref, mo_ref):
        i = pl.program_id(0)

        def compute():
            mo = _dot(cat_ref[...], w_ref[...])
            mo_ref[...] = mo.astype(BF16)
            ho_ref[...] = h_ref[...] + mod_ref[0, 5:6, :] * mo

        if ctx_active:
            compute()
        else:
            pl.when(i < n_lat)(compute)

            @pl.when(i >= n_lat)
            def _():
                ho_ref[...] = h_ref[...]
                mo_ref[...] = jnp.zeros_like(mo_ref)

    return _call(
        body,
        name=name,
        grid=(n_tiles,),
        in_specs=[_rows(TM, D), _rows(TM, D), _mod_spec(n_lat), _full((D, D))],
        out_specs=[_rows(TM, D), _rows(TM, D)],
        out_shape=[_sds((R, D), F32), _sds((R, D), BF16)],
        scratch_shapes=[],
        args=(h, cat, modv, wout),
        carry=carry,
    )


def _mixout_bwd(dho, mo, modv, wout, *, T, ctx_active, name, carry=None):
    R = dho.shape[0]
    n_lat, n_tiles = T // TM, R // TM

    def body(dho_ref, mo_ref, mod_ref, w_ref, dcat_ref, dmix_ref, part_ref):
        i = pl.program_id(0)
        first = jnp.logical_or(i == 0, i == n_lat)

        def compute():
            dho = dho_ref[...]
            dmix = (mod_ref[0, 5:6, :] * dho).astype(BF16)
            dmix_ref[...] = dmix
            dcat_ref[...] = _dot_nt(dmix, w_ref[...])
            dgate = jnp.sum(dho * mo_ref[...].astype(F32), axis=0, keepdims=True)
            _acc_partials(part_ref, first, {2: dgate})

        if ctx_active:
            compute()
        else:
            pl.when(i < n_lat)(compute)

            @pl.when(i >= n_lat)
            def _():
                dcat_ref[...] = jnp.zeros_like(dcat_ref)
                dmix_ref[...] = jnp.zeros_like(dmix_ref)
                part_ref[...] = jnp.zeros_like(part_ref)

    return _call(
        body,
        name=name,
        grid=(n_tiles,),
        in_specs=[_rows(TM, D), _rows(TM, D), _mod_spec(n_lat), _full((D, D))],
        out_specs=[_rows(TM, D), _rows(TM, D), _part_spec(n_lat)],
        out_shape=[_sds((R, D), F32), _sds((R, D), BF16), _sds((2, 8, D), F32)],
        scratch_shapes=[],
        args=(dho, mo, modv, wout),
        carry=carry,
    )


def _pool_bwd(u, dcat, w_pool, pool_scale, *, T, name):
    R = u.shape[0]
    nb = R // BLK

    def body(dcat_ref, wp_ref, ps_ref, u_hbm, dps_ref, dwp_ref, dsc_ref, u_v, sem):
        j = pl.program_id(0)

        @pl.when(j == 0)
        def _():
            _load_weights([(u_hbm, u_v)], sem)
            dwp_ref[...] = jnp.zeros_like(dwp_ref)
            dsc_ref[...] = jnp.zeros_like(dsc_ref)

        pooled, counts = _pooled(u_v, j, T, R)
        for g in range(4):
            sl = slice(g * 128, (g + 1) * 128)
            p_bf = pooled[g].astype(BF16)
            w_bf = wp_ref[g].astype(BF16)
            dmixed = dcat_ref[:, sl]
            dsc_ref[0:1, sl] += jnp.sum(dmixed * _dot(p_bf, w_bf), axis=0, keepdims=True)
            dmp = (dmixed * ps_ref[:, sl]).astype(BF16)
            dwp_ref[sl, :] += _dot_tn(p_bf, dmp)
            dps_ref[:, sl] = _dot_nt(dmp, w_bf) / counts[g]

    return pl.pallas_call(
        body,
        name=name,
        grid=(nb,),
        in_specs=[_rows(BLK, D), _full((4, 128, 128)), _full((1, PW)), ANY],
        out_specs=[_rows(BLK, PW), _full((PW, 128)), _full((8, PW))],
        out_shape=[_sds((R, PW), F32), _sds((PW, 128), F32), _sds((8, PW), F32)],
        scratch_shapes=[pltpu.VMEM((R, PW), F32), pltpu.SemaphoreType.DMA((1,))],
        compiler_params=_params(),
    )(dcat, w_pool, pool_scale, u)


def _fold_heads(x):
    y = x[:, :128] + x[:, 128:]
    return y + pltpu.roll(y, HD, 1)


def _attn_bwd(q, k4, v4, dcat, dps, sink, *, T, name, carry=None):
    R = q.shape[0]
    nb, nbl = R // BLK, T // BLK

    def body(q_ref, dcat_ref, sink_ref, k4_hbm, v4_hbm, dps_hbm, du_ref, dq_ref, dk_ref, dv_ref, dsk_ref,
             k4_v, v4_v, dps_v, sem):
        j = pl.program_id(0)

        @pl.when(j == 0)
        def _():
            _load_weights([(k4_hbm, k4_v), (v4_hbm, v4_v), (dps_hbm, dps_v)], sem)
            dk_ref[...] = jnp.zeros_like(dk_ref)
            dv_ref[...] = jnp.zeros_like(dv_ref)
            dsk_ref[...] = jnp.zeros_like(dsk_ref)

        start = _win_start(j, R)
        d3_hi, d3_lo = _hi_lo(dps_v[pl.ds(start, 3 * BLK), :])
        db = dps_v[pl.ds(pl.multiple_of(j * BLK, BLK), BLK), :]
        pos = j * BLK + lax.broadcasted_iota(jnp.int32, (BLK, 1), 0)
        t_r = start + lax.broadcasted_iota(jnp.int32, (1, 3 * BLK), 1)
        for g, w in enumerate(POOL_WINDOWS):
            sl = slice(g * 128, (g + 1) * 128)
            lo_r, hi_r = _pool_bounds(t_r, w, T, R)
            band_t = jnp.where(pos >= lo_r, jnp.where(pos < hi_r, 1.0, 0.0), 0.0).astype(BF16)
            lo_c, hi_c = _pool_bounds(pos, w, T, R)
            du_ref[:, sl] = _dot(band_t, d3_hi[:, sl]) + _dot(band_t, d3_lo[:, sl]) - db[:, sl] * (hi_c - lo_c).astype(F32)

        start_l = _win_start(j, T)
        rb = lax.broadcasted_iota(jnp.int32, (4 * BLK, 1), 0) // BLK
        lane = lax.broadcasted_iota(jnp.int32, (1, 128), 1)
        dk_l, dk_c, dv_l, dv_c = [], [], [], []
        for g in range(2):
            gl = slice(g * 256, (g + 1) * 256)
            qs = _stack_heads(q_ref[:, gl])
            kl, kc = k4_v[pl.ds(start_l, 3 * BLK), gl], k4_v[T:R, gl]
            vl, vc = v4_v[pl.ds(start_l, 3 * BLK), gl], v4_v[T:R, gl]
            p_l, p_c, p_s = _attn_probs(qs, kl, kc, sink_ref, g, j, start_l, nbl)
            dos = _stack_heads(dcat_ref[:, PW + g * 256 : PW + (g + 1) * 256]).astype(BF16)
            dp_l, dp_c = _dot_nt(dos, vl), _dot_nt(dos, vc)
            delta = jnp.sum(p_l * dp_l, axis=1, keepdims=True) + jnp.sum(p_c * dp_c, axis=1, keepdims=True)
            ds_l = (p_l * (dp_l - delta) * (HD ** -0.5)).astype(BF16)
            ds_c = (p_c * (dp_c - delta) * (HD ** -0.5)).astype(BF16)
            dq_ref[:, gl] = _unstack_heads(_dot(ds_l, kl) + _dot(ds_c, kc))
            dk_l.append(_fold_heads(_dot_tn(ds_l, qs)))
            dk_c.append(_fold_heads(_dot_tn(ds_c, qs)))
            dv_l.append(_fold_heads(_dot_tn(p_l.astype(BF16), dos)))
            dv_c.append(_fold_heads(_dot_tn(p_c.astype(BF16), dos)))
            dsink = -p_s * delta
            for h in range(4):
                tot = jnp.sum(jnp.where(rb == h, dsink, 0.0), axis=0, keepdims=True)
                dsk_ref[4 * g + h : 4 * g + h + 1, :] += jnp.broadcast_to(tot, (1, 128))
        first = lane < HD
        dk_ref[pl.ds(start_l, 3 * BLK), :] += jnp.where(first, dk_l[0], dk_l[1])
        dk_ref[T:R, :] += jnp.where(first, dk_c[0], dk_c[1])
        dv_ref[pl.ds(start_l, 3 * BLK), :] += jnp.where(first, dv_l[0], dv_l[1])
        dv_ref[T:R, :] += jnp.where(first, dv_c[0], dv_c[1])

    return _call(
        body,
        name=name,
        grid=(nb,),
        in_specs=[_rows(BLK, AW), _rows(BLK, D), SMEM, ANY, ANY, ANY],
        out_specs=[_rows(BLK, PW), _rows(BLK, AW), _full((R, KVW)), _full((R, KVW)), _full((8, 128))],
        out_shape=[_sds((R, PW), F32), _sds((R, AW), F32), _sds((R, KVW), F32), _sds((R, KVW), F32),
                   _sds((8, 128), F32)],
        scratch_shapes=[pltpu.VMEM((R, AW), BF16), pltpu.VMEM((R, AW), BF16), pltpu.VMEM((R, PW), F32),
                        pltpu.SemaphoreType.DMA((3,))],
        args=(q, dcat, sink, k4, v4, dps),
        carry=carry,
    )


def _mixproj_bwd(h, dho, du, dq, dk, dv, modv, gvec, win, cos, sin, *, T, name):
    R = h.shape[0]
    n_lat, n_tiles = T // TM, R // TM

    def body(h_ref, dho_ref, du_ref, dq_ref, dk_ref, dv_ref, mod_ref, g_ref, win_ref, cos_ref, sin_ref,
             dh_ref, dproj_ref, n_ref, part_ref):
        i = pl.program_id(0)
        first = jnp.logical_or(i == 0, i == n_lat)
        shift, scale = mod_ref[0, 3:4, :], mod_ref[0, 4:5, :]
        g = g_ref[1:2, :]
        r, xhat, y, n = _norm_mod(h_ref[...], g, shift, scale)
        n_ref[...] = n.astype(BF16)
        cs, sn = cos_ref[...], sin_ref[...]
        dproj_ref[:, :PW] = du_ref[...].astype(BF16)
        for s in range(AW // 128):
            x = dq_ref[:, 128 * s : 128 * (s + 1)]
            dproj_ref[:, PW + 128 * s : PW + 128 * (s + 1)] = (x * cs - _rot_half(x) * sn).astype(BF16)
        x = dk_ref[...]
        dproj_ref[:, PW + AW : PW + AW + KVW] = (x * cs - _rot_half(x) * sn).astype(BF16)
        dproj_ref[:, PW + AW + KVW :] = dv_ref[...].astype(BF16)
        dn = _dot(dproj_ref[...], win_ref[...])
        dh, dshift, dscale, dg = _norm_mod_bwd(dn, r, xhat, y, g, scale)
        dh_ref[...] = dho_ref[...] + dh
        _acc_partials(part_ref, first, {0: dshift, 1: dscale, 3: dg})

    return pl.pallas_call(
        body,
        name=name,
        grid=(n_tiles,),
        in_specs=[_rows(TM, D), _rows(TM, D), _rows(TM, PW), _rows(TM, AW), _rows(TM, KVW), _rows(TM, KVW),
                  _mod_spec(n_lat), _full((8, D)), _full((PROJ, D)), _rows(TM, 128), _rows(TM, 128)],
        out_specs=[_rows(TM, D), _rows(TM, PROJ), _rows(TM, D), _part_spec(n_lat)],
        out_shape=[_sds((R, D), F32), _sds((R, PROJ), BF16), _sds((R, D), BF16), _sds((2, 8, D), F32)],
        compiler_params=_params(),
    )(h, dho, du, dq, dk, dv, modv, gvec, win, cos, sin)


def _loss_head(h, target, g_final, *, T, name):
    R = h.shape[0]
    n_lat, n_tiles = T // TM, R // TM

    def body(h_ref, t_ref, g_ref, dh_ref, loss_ref, dg_ref):
        i = pl.program_id(0)

        @pl.when(i == 0)
        def _():
            loss_ref[...] = jnp.zeros_like(loss_ref)
            dg_ref[...] = jnp.zeros_like(dg_ref)

        @pl.when(i < n_lat)
        def _():
            h = h_ref[...]
            g = g_ref[...]
            r = lax.rsqrt(jnp.mean(h * h, axis=-1, keepdims=True) + EPS)
            xhat = h * r
            err = xhat * g - t_ref[...]
            tot = jnp.sum(jnp.sum(err * err, axis=1, keepdims=True), axis=0, keepdims=True)
            loss_ref[...] += jnp.broadcast_to(tot * (0.5 / D), loss_ref.shape)
            dy = err * (1.0 / D)
            dg_ref[0:1, :] += jnp.sum(dy * xhat, axis=0, keepdims=True)
            dxh = dy * g
            dh_ref[...] = r * (dxh - xhat * jnp.mean(dxh * xhat, axis=-1, keepdims=True))

        @pl.when(i >= n_lat)
        def _():
            dh_ref[...] = jnp.zeros_like(dh_ref)

    return pl.pallas_call(
        body,
        name=name,
        grid=(n_tiles,),
        in_specs=[_rows(TM, D), pl.BlockSpec((TM, D), lambda i: (jnp.minimum(i, n_lat - 1), 0)), _full((1, D))],
        out_specs=[_rows(TM, D), _full((8, 128)), _full((8, D))],
        out_shape=[_sds((R, D), F32), _sds((8, 128), F32), _sds((8, D), F32)],
        compiler_params=_params(),
    )(h, target, g_final)


def _mod_fwd(c16, w_mod, b_cols, *, name):
    nl, _, cols = w_mod.shape

    def body(c_ref, w_ref, b_ref, o_ref):
        c = c_ref[...]
        sc = (c * _sigmoid(c)).astype(BF16)
        o_ref[0] = _dot(sc, w_ref[0].astype(BF16)) + b_ref[0]

    return pl.pallas_call(
        body,
        name=name,
        grid=(nl,),
        in_specs=[_full((16, D)), pl.BlockSpec((1, D, cols), lambda l: (l, 0, 0)),
                  pl.BlockSpec((1, 1, cols), lambda l: (l, 0, 0))],
        out_specs=pl.BlockSpec((1, 16, cols), lambda l: (l, 0, 0)),
        out_shape=_sds((nl, 16, cols), F32),
        compiler_params=_params(),
    )(c16, w_mod, b_cols)


def _mod_bwd(c16, dm_cols, w_mod, *, name):
    nl, _, cols = w_mod.shape

    def body(c_ref, dm_ref, w_ref, gw_ref, dc_ref):
        c = c_ref[...]
        sc = (c * _sigmoid(c)).astype(BF16)
        dm = dm_ref[0].astype(BF16)
        gw_ref[0] = _dot_tn(sc, dm)
        dc_ref[0] = _dot_nt(dm, w_ref[0].astype(BF16))

    return pl.pallas_call(
        body,
        name=name,
        grid=(nl,),
        in_specs=[_full((16, D)), pl.BlockSpec((1, 16, cols), lambda l: (l, 0, 0)),
                  pl.BlockSpec((1, D, cols), lambda l: (l, 0, 0))],
        out_specs=[pl.BlockSpec((1, D, cols), lambda l: (l, 0, 0)), pl.BlockSpec((1, 16, D), lambda l: (l, 0, 0))],
        out_shape=[_sds((nl, D, cols), F32), _sds((nl, 16, D), F32)],
        compiler_params=_params(),
    )(c16, dm_cols, w_mod)


def _coords():
    return lax.axis_index("x"), lax.axis_index("y"), lax.axis_index("c")


def _peer(k, x, y, c):
    return (1 - x if k & 4 else x, 1 - y if k & 2 else y, 1 - c if k & 1 else c)


def _lin(p):
    return 4 * p[0] + 2 * p[1] + p[2]


def _view(ref, slot):
    return ref if slot is None else ref.at[slot]


class _Round:
    def __init__(self, ins, out_shapes, plan, local_plan=(), n_alias=0):
        self.ins, self.out_shapes = list(ins), list(out_shapes)
        self.plan, self.local_plan, self.n_alias = list(plan), list(local_plan), n_alias

    def sems(self):
        return [pltpu.SemaphoreType.DMA((len(self.plan),)), pltpu.SemaphoreType.DMA((len(self.plan),)),
                pltpu.SemaphoreType.DMA((max(len(self.local_plan), 1),))]

    def _copies(self, in_refs, out_refs, sems, incoming, links=("ici", "d2d")):
        in_refs = list(out_refs[: self.n_alias]) + list(in_refs[self.n_alias :])
        send_sems, recv_sems, loc_sems = sems
        x, y, c = _coords()
        me = _lin((x, y, c))
        remote = []
        for idx, (k, ii, sfn, oi, dfn) in enumerate(self.plan):
            if ("d2d" if k == 1 else "ici") not in links:
                continue
            peer = _peer(k, x, y, c)
            sender, receiver = (_lin(peer), me) if incoming else (me, _lin(peer))
            remote.append(pltpu.make_async_remote_copy(
                src_ref=_view(in_refs[ii], sfn(sender, receiver)), dst_ref=_view(out_refs[oi], dfn(sender, receiver)),
                send_sem=send_sems.at[idx], recv_sem=recv_sems.at[idx], device_id=peer, device_id_type=MESH))
        locs = [pltpu.make_async_copy(_view(in_refs[ii], sfn(me)), _view(out_refs[oi], dfn(me)), loc_sems.at[idx])
                for idx, (ii, sfn, oi, dfn) in enumerate(() if incoming or "ici" not in links else self.local_plan)]
        return remote, locs

    def start(self, in_refs, out_refs, sems, links=("ici", "d2d")):
        sends, locs = self._copies(in_refs, out_refs, sems, incoming=False, links=links)
        for cp in sends + locs:
            cp.start()

    def finish(self, in_refs, out_refs, sems):
        for cp in self._copies(in_refs, out_refs, sems, incoming=True)[0]:
            cp.wait_recv()
        sends, locs = self._copies(in_refs, out_refs, sems, incoming=False)
        for cp in sends:
            cp.wait_send()
        for cp in locs:
            cp.wait()


def _exchange(name, rnd):
    n_in, n_out = len(rnd.ins), len(rnd.out_shapes)

    def body(*refs):
        in_refs, out_refs, sems = refs[:n_in], refs[n_in : n_in + n_out], refs[n_in + n_out :]
        rnd.start(in_refs, out_refs, sems)
        rnd.finish(in_refs, out_refs, sems)

    return pl.pallas_call(
        body, name=name, in_specs=[ANY] * n_in, out_specs=[ANY] * n_out, out_shape=rnd.out_shapes,
        scratch_shapes=rnd.sems(), input_output_aliases={i: i for i in range(rnd.n_alias)})(*rnd.ins)


def _call(body, *, name, grid, in_specs, out_specs, out_shape, scratch_shapes, args, carry=None):
    params = _params(len(grid))
    if carry is None:
        outs = pl.pallas_call(body, name=name, grid=grid, in_specs=in_specs, out_specs=out_specs, out_shape=out_shape,
                              scratch_shapes=scratch_shapes, compiler_params=params)(*args)
        return list(outs), []
    n_ci, n_co, n_cs = len(in_specs), len(out_shape), len(scratch_shapes)
    n_xi, n_xo = len(carry.ins), len(carry.out_shapes)

    def wrapped(*refs):
        ci, xi = refs[:n_ci], refs[n_ci : n_ci + n_xi]
        o0 = n_ci + n_xi
        co, xo = refs[o0 : o0 + n_co], refs[o0 + n_co : o0 + n_co + n_xo]
        s0 = o0 + n_co + n_xo
        cs, sems = refs[s0 : s0 + n_cs], refs[s0 + n_cs :]
        ids = [pl.program_id(a) for a in range(len(grid))]
        first = functools.reduce(jnp.logical_and, [i == 0 for i in ids])
        last = functools.reduce(jnp.logical_and, [i == g - 1 for i, g in zip(ids, grid)])

        @pl.when(first)
        def _():
            carry.start(xi, xo, sems, links=("ici",))

        body(*ci, *co, *cs)

        @pl.when(first)
        def _():
            carry.start(xi, xo, sems, links=("d2d",))

        @pl.when(last)
        def _():
            carry.finish(xi, xo, sems)

    outs = pl.pallas_call(
        wrapped, name=name, grid=grid, in_specs=list(in_specs) + [ANY] * n_xi, out_specs=list(out_specs) + [ANY] * n_xo,
        out_shape=list(out_shape) + carry.out_shapes, scratch_shapes=list(scratch_shapes) + carry.sems(),
        input_output_aliases={n_ci + i: n_co + i for i in range(carry.n_alias)}, compiler_params=params,
    )(*args, *carry.ins)
    return list(outs[:n_co]), list(outs[n_co:])


def _gather_direct(arrays):
    na = len(arrays)
    outs = [_sds((NDEV,) + a.shape, a.dtype) for a in arrays]
    plan = [(k, i, lambda s, r: None, i, lambda s, r: s) for i in range(na) for k in range(1, NDEV)]
    return _Round(arrays, outs, plan, [(i, lambda m: None, i, lambda m: m) for i in range(na)])


def _gather_a(arrays):
    na = len(arrays)
    outs = [_sds((NDEV,) + a.shape, a.dtype) for a in arrays]
    plan = [(k, i, lambda s, r: None, i, lambda s, r: s) for i in range(na) for k in (2, 4, 6)]
    return _Round(arrays, outs, plan, [(i, lambda m: None, i, lambda m: m) for i in range(na)])


def _gather_b(got):
    na = len(got)
    plan = [(1, i, (lambda s, r, k=k: s ^ k), i, (lambda s, r, k=k: s ^ k)) for i in range(na) for k in (0, 2, 4, 6)]
    return _Round(got, [_sds(g.shape, g.dtype) for g in got], plan, n_alias=na)


def _scatter_1(grads):
    plan = [(1, i, (lambda s, r, q=q: 2 * q + (r & 1)), i, (lambda s, r, q=q: q))
            for i in range(len(grads)) for q in range(4)]
    return _Round(grads, [_sds((4,) + g.shape[1:], g.dtype) for g in grads], plan)


def _scatter_2(chip):
    plan = [(k, i, lambda s, r: r >> 1, i, (lambda s, r, j=j: j)) for i in range(len(chip)) for j, k in enumerate((2, 4, 6))]
    return _Round(chip, [_sds((3,) + g.shape[1:], g.dtype) for g in chip], plan)


def _add_pairs(g, got, pos, *, name):
    _, sh, w = g.shape

    def body(pos_ref, g_ref, r_ref, o_ref):
        o_ref[...] = (g_ref[...].astype(F32) + r_ref[...].astype(F32)).astype(o_ref.dtype)

    return pl.pallas_call(
        body,
        name=name,
        grid_spec=pltpu.PrefetchScalarGridSpec(
            num_scalar_prefetch=1, grid=(4,),
            in_specs=[pl.BlockSpec((1, sh, w), lambda q, p: (2 * q + p[0], 0, 0)),
                      pl.BlockSpec((1, sh, w), lambda q, p: (q, 0, 0))],
            out_specs=pl.BlockSpec((1, sh, w), lambda q, p: (q, 0, 0))),
        out_shape=_sds((4, sh, w), g.dtype),
        compiler_params=_params(),
    )(pos, g, got)


def _sum_chips(chip, got, pos, *, name):
    _, sh, w = chip.shape

    def body(pos_ref, c_ref, r_ref, o_ref):
        acc = c_ref[0].astype(F32)
        for s in range(3):
            acc = acc + r_ref[s].astype(F32)
        o_ref[...] = acc

    return pl.pallas_call(
        body,
        name=name,
        grid_spec=pltpu.PrefetchScalarGridSpec(
            num_scalar_prefetch=1, grid=(1,),
            in_specs=[pl.BlockSpec((1, sh, w), lambda i, p: (p[1], 0, 0)), pl.BlockSpec((3, sh, w), lambda i, p: (0, 0, 0))],
            out_specs=pl.BlockSpec((sh, w), lambda i, p: (0, 0))),
        out_shape=_sds((sh, w), F32),
        compiler_params=_params(),
    )(pos, chip, got)


def _adamw_math(w, g, m, v):
    m2 = ADAM_B1 * m + (1.0 - ADAM_B1) * g
    v2 = ADAM_B2 * v + (1.0 - ADAM_B2) * (g * g)
    m_hat = m2 / (1.0 - ADAM_B1 ** ADAM_STEP)
    v_hat = v2 / (1.0 - ADAM_B2 ** ADAM_STEP)
    delta = -ADAM_LR * (m_hat / (jnp.sqrt(v_hat) + ADAM_EPS) + ADAM_WD * w)
    return delta, m2, v2


def _adamw(w, g, m, v, *, name):
    shape = w.shape
    flat = [t.reshape(-1, shape[-1]) for t in (w, g, m, v)]
    rows, cols = flat[0].shape
    tr = rows // 8 if rows % 64 == 0 else rows
    spec = _rows(tr, cols)

    def body(w_ref, g_ref, m_ref, v_ref, d_ref, m2_ref, v2_ref):
        d_ref[...], m2_ref[...], v2_ref[...] = _adamw_math(w_ref[...], g_ref[...], m_ref[...], v_ref[...])

    outs = pl.pallas_call(
        body, name=name, grid=(rows // tr,), in_specs=[spec] * 4, out_specs=[spec] * 3,
        out_shape=[_sds((rows, cols), F32)] * 3, compiler_params=_params())(*flat)
    return tuple(o.reshape(shape) for o in outs)


def _small_sums(packets, nf, dwp, dsc, dsk, *, name):
    flat = [p for layer in packets for p in layer]

    def total(ref, *idx):
        acc = ref[(0,) + idx]
        for dev in range(1, NDEV):
            acc = acc + ref[(dev,) + idx]
        return acc

    def body(*refs):
        pk = refs[:6]
        nf_ref, dwp0, dwp1, dsc0, dsc1, dsk0, dsk1 = refs[6:13]
        dm_ref, gb_ref, gn_ref, gnf_ref, gwp_ref, gps_ref, gsk_ref = refs[13:]
        dm_ref[...] = jnp.zeros_like(dm_ref)
        gn_ref[...] = jnp.zeros_like(gn_ref)
        for l in range(2):
            for sb in range(3):
                p = pk[3 * l + sb]
                for r in range(3):
                    col = slice((3 * sb + r) * D, (3 * sb + r + 1) * D)
                    lat = p[0, 0, r : r + 1, :]
                    dm_ref[l, 0:1, col] = lat
                    for dev in range(1, NDEV):
                        row = p[dev, 0, r : r + 1, :]
                        dm_ref[l, dev : dev + 1, col] = row
                        lat = lat + row
                    ctx = total(p, 1, slice(r, r + 1), slice(None))
                    dm_ref[l, 8:9, col] = ctx
                    gb_ref[l : l + 1, col] = lat + ctx
                gn_ref[l, sb : sb + 1, :] = total(p, 0, slice(3, 4), slice(None)) + total(p, 1, slice(3, 4), slice(None))
        gnf_ref[...] = total(nf_ref, slice(0, 1), slice(None))
        for l, (a, b, c) in enumerate(((dwp0, dsc0, dsk0), (dwp1, dsc1, dsk1))):
            gwp_ref[l] = total(a, slice(None), slice(None))
            gps_ref[l : l + 1, :] = total(b, slice(0, 1), slice(None))
            gsk_ref[l] = total(c, slice(None), slice(None))

    ins = flat + [nf, dwp[0], dwp[1], dsc[0], dsc[1], dsk[0], dsk[1]]
    return pl.pallas_call(
        body,
        name=name,
        out_shape=[_sds((2, 16, NMOD * D), F32), _sds((2, NMOD * D), F32), _sds((2, 8, D), F32), _sds((1, D), F32),
                   _sds((2, PW, 128), F32), _sds((2, PW), F32), _sds((2, 8, 128), F32)],
        compiler_params=pltpu.CompilerParams(vmem_limit_bytes=VMEM_LIMIT),
    )(*ins)


def _small_adamw(c_ctx, dc_all, triples, *, name):
    n = len(triples)

    def body(*refs):
        c_ref, dc_ref = refs[0], refs[1]
        ins = refs[2 : 2 + 4 * n - 1]
        outs = refs[2 + 4 * n - 1 :]
        acc = dc_ref[0, 0, 8:9, :] + dc_ref[0, 1, 8:9, :]
        for dev in range(1, NDEV):
            acc = acc + (dc_ref[dev, 0, 8:9, :] + dc_ref[dev, 1, 8:9, :])
        c = c_ref[...]
        sig = _sigmoid(c)
        g_c = acc * (sig * (1.0 + c * (1.0 - sig)))
        outs[0][...] = g_c
        pos = 0
        for k in range(n):
            if k == 0:
                w, g, m, v = ins[0][...], g_c, ins[1][...], ins[2][...]
                pos = 3
            else:
                w, g, m, v = (ins[pos + t][...] for t in range(4))
                pos += 4
            d, m2, v2 = _adamw_math(w, g, m, v)
            outs[1 + 3 * k][...], outs[2 + 3 * k][...], outs[3 + 3 * k][...] = d, m2, v2

    flat_in = [c_ctx, dc_all]
    out_shape = [_sds(c_ctx.shape, F32)]
    for k, (w, g, m, v) in enumerate(triples):
        flat_in += [w, m, v] if k == 0 else [w, g, m, v]
        out_shape += [_sds(w.shape, F32)] * 3
    return pl.pallas_call(body, name=name, out_shape=out_shape,
                          compiler_params=pltpu.CompilerParams(vmem_limit_bytes=VMEM_LIMIT))(*flat_in)


def _rope_tables(T, R):
    t = jnp.arange(T)
    inv = ROPE_BASE ** (-jnp.arange(0, HD // 2, 2, dtype=F32) / (HD // 2))
    ang = jnp.concatenate([(t // GRID_W).astype(F32)[:, None] * inv, (t % GRID_W).astype(F32)[:, None] * inv], axis=-1)
    cos = jnp.concatenate([jnp.tile(jnp.cos(ang), (1, 4)), jnp.ones((R - T, 128), F32)], axis=0)
    sin = jnp.concatenate([jnp.tile(jnp.sin(ang), (1, 4)), jnp.zeros((R - T, 128), F32)], axis=0)
    return cos, sin


def kernel(x, c, ctx, c_ctx, w_mod, b_mod, norm_ffn1, w_ffn1_in, w_ffn1_out, norm_mix, w_in, w_pool, pool_scale, sink, w_out, norm_ffn2, w_ffn2_in, w_ffn2_out, norm_final, loss_target, m_c_ctx, m_w_mod, m_b_mod, m_norm_ffn1, m_w_ffn1_in, m_w_ffn1_out, m_norm_mix, m_w_in, m_w_pool, m_pool_scale, m_sink, m_w_out, m_norm_ffn2, m_w_ffn2_in, m_w_ffn2_out, m_norm_final, v_c_ctx, v_w_mod, v_b_mod, v_norm_ffn1, v_w_ffn1_in, v_w_ffn1_out, v_norm_mix, v_w_in, v_w_pool, v_pool_scale, v_sink, v_w_out, v_norm_ffn2, v_w_ffn2_in, v_w_ffn2_out, v_norm_final):
    T = x.shape[1]
    R = T + LC
    nl = w_mod.shape[0]
    cx, cy, cc = _coords()
    me = _lin((cx, cy, cc))
    pos = jnp.stack([cc, 2 * cx + cy]).astype(jnp.int32)
    mcols = w_mod.shape[2]

    shards = [([w_ffn1_in[l].T.astype(BF16), w_ffn1_out[l].astype(BF16)],
               [w_in[l].T.astype(BF16), w_out[l].astype(BF16)],
               [w_ffn2_in[l].T.astype(BF16), w_ffn2_out[l].astype(BF16)]) for l in range(nl)]

    got = _exchange("ag_c_w", _merge(_gather_direct([c]), _gather_a(shards[0][0] + shards[0][1])))
    c_all, w_first = got[0], got[1:]
    c16 = jnp.concatenate([c_all.reshape(NDEV, D), c_ctx[None], jnp.zeros((16 - NDEV - 1, D), F32)], axis=0)
    b_cols = lax.dynamic_slice(b_mod, (0, me * mcols), (nl, mcols)).reshape(nl, 1, mcols)
    got = _exchange("ag_mod_w", _merge(_gather_b(w_first), _gather_direct([_mod_fwd(c16, w_mod, b_cols, name="mod_fwd")])))
    w_first, mod_all = got[:4], got[4]
    mod_all = jnp.transpose(mod_all, (1, 2, 0, 3)).reshape(nl, 16, NMOD, D)
    mine = lax.dynamic_index_in_dim(mod_all, me, axis=1, keepdims=False)
    pad = jnp.zeros((nl, 16 - NMOD, D), F32)
    modv = jnp.stack([jnp.concatenate([mine, pad], axis=1), jnp.concatenate([mod_all[:, 8], pad], axis=1)], axis=1)

    gvec = [jnp.concatenate([norm_ffn1[l][None], norm_mix[l][None], norm_ffn2[l][None], jnp.zeros((5, D), F32)], axis=0)
            for l in range(nl)]
    cos, sin = _rope_tables(T, R)
    ps2 = [pool_scale[l][None] for l in range(nl)]

    h = jnp.concatenate([x[0], ctx[0]], axis=0)
    loss_part, dh, small, nf_all, big = _forward_backward(
        h, loss_target[0], modv, gvec, shards, w_first, cos, sin, sink, w_pool, ps2, norm_final, pos, T=T)
    loss = lax.psum(loss_part[0, 0], ("x", "y", "c"))
    grad_x = dh[:T][None]

    dm, g_b_mod, g_norms, g_nf, g_wp, g_ps, g_sk = _small_sums(
        [small[l][0:3] for l in range(nl)], nf_all, *[[small[l][k] for l in range(nl)] for k in (3, 4, 5)],
        name="small_sums")
    dm_cols = lax.dynamic_slice(dm, (0, 0, me * mcols), (nl, 16, mcols))
    g_w_mod, dc_part = _mod_bwd(c16, dm_cols, w_mod, name="mod_bwd")
    (dc_all,) = _exchange("ag_dc", _gather_direct([dc_part]))

    grads = {
        "b_mod": g_b_mod, "norm_ffn1": g_norms[:, 0], "norm_mix": g_norms[:, 1], "norm_ffn2": g_norms[:, 2],
        "w_pool": g_wp.reshape(w_pool.shape), "pool_scale": g_ps, "sink": g_sk[:, :, 0], "norm_final": g_nf.reshape(D),
        "w_mod": g_w_mod,
        "w_ffn1_in": jnp.stack([big[l][0].T for l in range(nl)]), "w_ffn1_out": jnp.stack([big[l][1] for l in range(nl)]),
        "w_in": jnp.stack([big[l][2].T for l in range(nl)]), "w_out": jnp.stack([big[l][3] for l in range(nl)]),
        "w_ffn2_in": jnp.stack([big[l][4].T for l in range(nl)]), "w_ffn2_out": jnp.stack([big[l][5] for l in range(nl)]),
    }
    weights = dict(c_ctx=c_ctx, w_mod=w_mod, b_mod=b_mod, norm_ffn1=norm_ffn1, w_ffn1_in=w_ffn1_in, w_ffn1_out=w_ffn1_out,
                   norm_mix=norm_mix, w_in=w_in, w_pool=w_pool, pool_scale=pool_scale, sink=sink, w_out=w_out,
                   norm_ffn2=norm_ffn2, w_ffn2_in=w_ffn2_in, w_ffn2_out=w_ffn2_out, norm_final=norm_final)
    moms = dict(c_ctx=(m_c_ctx, v_c_ctx), w_mod=(m_w_mod, v_w_mod), b_mod=(m_b_mod, v_b_mod),
                norm_ffn1=(m_norm_ffn1, v_norm_ffn1), w_ffn1_in=(m_w_ffn1_in, v_w_ffn1_in),
                w_ffn1_out=(m_w_ffn1_out, v_w_ffn1_out), norm_mix=(m_norm_mix, v_norm_mix), w_in=(m_w_in, v_w_in),
                w_pool=(m_w_pool, v_w_pool), pool_scale=(m_pool_scale, v_pool_scale), sink=(m_sink, v_sink),
                w_out=(m_w_out, v_w_out), norm_ffn2=(m_norm_ffn2, v_norm_ffn2), w_ffn2_in=(m_w_ffn2_in, v_w_ffn2_in),
                w_ffn2_out=(m_w_ffn2_out, v_w_ffn2_out), norm_final=(m_norm_final, v_norm_final))
    order = list(weights)
    small_names = ["c_ctx", "b_mod", "norm_ffn1", "norm_mix", "w_pool", "pool_scale", "sink", "norm_ffn2", "norm_final"]

    def as2d(name, t):
        if name == "w_pool":
            return t.reshape(-1, 128)
        return t.reshape(1, -1) if t.ndim == 1 else t

    triples = [(as2d(n, weights[n]), None if n == "c_ctx" else as2d(n, grads[n]), as2d(n, moms[n][0]), as2d(n, moms[n][1]))
               for n in small_names]
    outs = _small_adamw(as2d("c_ctx", c_ctx), dc_all, triples, name="small_adamw")
    grads["c_ctx"] = outs[0].reshape(c_ctx.shape)
    delta, new_m, new_v = {}, {}, {}
    for k, n in enumerate(small_names):
        delta[n], new_m[n], new_v[n] = (o.reshape(weights[n].shape) for o in outs[1 + 3 * k : 4 + 3 * k])
    for n in order:
        if n not in small_names:
            delta[n], new_m[n], new_v[n] = _adamw(weights[n], grads[n], moms[n][0], moms[n][1], name=f"adamw_{n}")

    return (loss, grad_x, *[grads[n] for n in order], *[delta[n] for n in order],
            *[new_m[n] for n in order], *[new_v[n] for n in order])


def _merge(*rounds):
    ins, outs, plan, local, n_alias = [], [], [], [], 0
    for r in rounds:
        assert r.n_alias == 0 or (not ins and r.n_alias == len(r.ins) == len(r.out_shapes))
        oi, oo = len(ins), len(outs)
        plan += [(k, i + oi, sf, o + oo, df) for k, i, sf, o, df in r.plan]
        local += [(i + oi, sf, o + oo, df) for i, sf, o, df in r.local_plan]
        ins += r.ins
        outs += r.out_shapes
        n_alias += r.n_alias
    return _Round(ins, outs, plan, local, n_alias)


def _forward_backward(h, target, modv, gvec, shards, w_first, cos, sin, sink, w_pool, ps2, norm_final, pos, *, T):
    nl = len(gvec)
    flat = lambda ws: [w.reshape(-1, D) for w in ws]
    saved = []
    w1, wm = flat(w_first[:2]), flat(w_first[2:])
    for l in range(nl):
        last = l == nl - 1
        h0 = h
        if l == 0:
            (h1, a1, b1, f1), got = _ffn_fwd(h0, modv[l], gvec[l], *w1, T=T, mrow=0, grow=0, ctx_active=True,
                                             name=f"ffn1_fwd_{l}", carry=_gather_a(shards[l][2]))
            (u, q, k4, v4), got = _mixproj_fwd(h1, modv[l], gvec[l], wm[0], cos, sin, T=T, name=f"mixproj_fwd_{l}",
                                               carry=_gather_b(got))
            w2 = flat(got)
        else:
            (h1, a1, b1, f1), got = _ffn_fwd(h0, modv[l], gvec[l], *w1, T=T, mrow=0, grow=0, ctx_active=True,
                                             name=f"ffn1_fwd_{l}", carry=_gather_b(nxt_m + nxt_2))
            wm, w2 = flat(got[:2]), flat(got[2:])
            (u, q, k4, v4), _ = _mixproj_fwd(h1, modv[l], gvec[l], wm[0], cos, sin, T=T, name=f"mixproj_fwd_{l}")
        (cat,), nxt_1 = _attnpool_fwd(u, q, k4, v4, sink[l], w_pool[l], ps2[l], T=T, name=f"attnpool_fwd_{l}",
                                      carry=None if last else _gather_a(shards[l + 1][0]))
        (h2, mo), nxt_m = _mixout_fwd(h1, cat, modv[l], wm[1], T=T, ctx_active=not last, name=f"mixout_fwd_{l}",
                                      carry=None if last else _gather_a(shards[l + 1][1]))
        (h3, a2, b2, f2), got = _ffn_fwd(h2, modv[l], gvec[l], *w2, T=T, mrow=6, grow=2, ctx_active=not last,
                                         name=f"ffn2_fwd_{l}",
                                         carry=None if last else _merge(_gather_b(nxt_1), _gather_a(shards[l + 1][2])))
        saved.append((h0, a1, b1, f1, h1, u, q, k4, v4, cat, mo, h2, a2, b2, f2, w1, wm, w2))
        h = h3
        if not last:
            w1, nxt_2 = flat(got[:2]), got[2:]

    dh, loss_part, dnf = _loss_head(h, target, norm_final[None], T=T, name="loss_head")

    def adds(tag, grads, got):
        return [_add_pairs(g, r, pos, name=f"rs_add_{tag}_{i}") for i, (g, r) in enumerate(zip(grads, got))]

    def totals(tag, chip, got):
        return [_sum_chips(c_, r, pos, name=f"rs_sum_{tag}_{i}") for i, (c_, r) in enumerate(zip(chip, got))]

    small, big = [None] * nl, {}
    prev = None
    for l in reversed(range(nl)):
        last = l == nl - 1
        h0, a1, b1, f1, h1, u, q, k4, v4, cat, mo, h2, a2, b2, f2, w1, wm, w2 = saved[l]
        (dh, dab, s, n, df, pk2), got = _ffn_bwd(
            h2, dh, a2, b2, f2, modv[l], gvec[l], *w2, T=T, mrow=6, grow=2, ctx_active=not last, name=f"ffn2_bwd_{l}",
            carry=_merge(_scatter_1(prev[0]), _gather_a(prev[1])) if prev else None)
        if prev:
            c1, small_a = adds(f"ffn1_{l + 1}", prev[0], got[:2]), got[2:]
        g_w2i, got = _wgrad(dab, n, bk=WG_BK, sh=2 * DFF // NDEV, name=f"wgrad_ffn2_in_{l}",
                            carry=_scatter_2(c1[:1]) if prev else None)
        if prev:
            big[l + 1][0:1] = totals(f"ffn1_in_{l + 1}", c1[:1], got)
        g_w2o, got = _wgrad(s, df, bk=WG_BK, sh=DFF // NDEV, name=f"wgrad_ffn2_out_{l}",
                            carry=_scatter_2(c1[1:]) if prev else None)
        if prev:
            big[l + 1][1:2] = totals(f"ffn1_out_{l + 1}", c1[1:], got)
        rnd = _scatter_1([g_w2i, g_w2o])
        (dcat, dmix, pko), got = _mixout_bwd(dh, mo, modv[l], wm[1], T=T, ctx_active=not last, name=f"mixout_bwd_{l}",
                                             carry=_merge(_gather_b(small_a), rnd) if prev else rnd)
        if prev:
            small[l + 1], got = got[: len(small_a)], got[len(small_a) :]
        c2 = adds(f"ffn2_{l}", [g_w2i, g_w2o], got)
        g_wo, _ = _wgrad(cat, dmix, bk=D, sh=D // NDEV, name=f"wgrad_out_{l}")
        dps, dwp, dsc = _pool_bwd(u, dcat, w_pool[l], ps2[l], T=T, name=f"pool_bwd_{l}")
        (du, dq, dk, dv, dsk), got = _attn_bwd(q, k4, v4, dcat, dps, sink[l], T=T, name=f"attn_bwd_{l}", carry=_scatter_2(c2))
        big[l] = [None, None, None, None] + totals(f"ffn2_{l}", c2, got)
        dh, dproj, n, pkm = _mixproj_bwd(h1, dh, du, dq, dk, dv, modv[l], gvec[l], wm[0], cos, sin, T=T, name=f"mixproj_bwd_{l}")
        g_wi, _ = _wgrad(dproj, n, bk=PROJ, sh=PROJ // NDEV, name=f"wgrad_in_{l}")
        (dh, dab, s, n, df, pk1), got = _ffn_bwd(h0, dh, a1, b1, f1, modv[l], gvec[l], *w1, T=T, mrow=0, grow=0,
                                                 ctx_active=True, name=f"ffn1_bwd_{l}", carry=_scatter_1([g_wi, g_wo]))
        cm = adds(f"mix_{l}", [g_wi, g_wo], got)
        g_w1i, got = _wgrad(dab, n, bk=WG_BK, sh=2 * DFF // NDEV, name=f"wgrad_ffn1_in_{l}", carry=_scatter_2(cm))
        big[l][2:4] = totals(f"mix_{l}", cm, got)
        g_w1o, _ = _wgrad(s, df, bk=WG_BK, sh=DFF // NDEV, name=f"wgrad_ffn1_out_{l}")
        prev = ([g_w1i, g_w1o], [pk1, pkm + pko, pk2, dwp, dsc, dsk])
    got = _exchange("rs1_tail", _merge(_scatter_1(prev[0]), _gather_a(prev[1] + [dnf])))
    c1, small_a = adds("ffn1_0", prev[0], got[:2]), got[2:]
    got = _exchange("rs2_tail", _merge(_gather_b(small_a), _scatter_2(c1)))
    small[0], nf_all = got[: len(small_a) - 1], got[len(small_a) - 1]
    big[0][0:2] = totals("ffn1_0", c1, got[len(small_a) :])
    return loss_part, dh, small, nf_all, big
```

```python
import functools

import jax
import jax.numpy as jnp
from jax import lax
from jax.experimental import pallas as pl
from jax.experimental.pallas import tpu as pltpu

F32, BF16 = jnp.float32, jnp.bfloat16

D = 1024
LC = 256
DFF = 2816
NMOD = 9
PW = 512
AW = 512
KVW = 128
PROJ = PW + AW + 2 * KVW
HD = 64
BLK = 128
GRID_W = 64
POOL_WINDOWS = (2, 4, 8, 16)
EPS = 1e-6
NEG = -1e30
ROPE_BASE = 10000.0
NDEV = 8
MESH = pl.DeviceIdType.MESH

ADAM_LR, ADAM_B1, ADAM_B2, ADAM_EPS, ADAM_WD, ADAM_STEP = 0.001, 0.9, 0.999, 1e-08, 0.01, 10

VMEM_LIMIT = 56 * 1024 * 1024
TM = 256
FFN_CHUNKS = ((0, 1536), (1536, 1280))
WG_BK = 1408

ANY = pl.BlockSpec(memory_space=pl.ANY)
SMEM = pl.BlockSpec(memory_space=pltpu.SMEM)


def _params(ngrid=1):
    return pltpu.CompilerParams(dimension_semantics=("arbitrary",) * ngrid, vmem_limit_bytes=VMEM_LIMIT)


def _dot(a, b):
    return jnp.dot(a, b, preferred_element_type=F32)


def _dot_nt(a, b):
    return lax.dot_general(a, b, (((1,), (1,)), ((), ())), preferred_element_type=F32)


def _dot_tn(a, b):
    return lax.dot_general(a, b, (((0,), (0,)), ((), ())), preferred_element_type=F32)


def _sigmoid(x):
    return 1.0 / (1.0 + jnp.exp(-x))


def _rows(tm, w):
    return pl.BlockSpec((tm, w), lambda i: (i, 0))


def _full(shape):
    nd = len(shape)
    return pl.BlockSpec(shape, lambda *_: (0,) * nd)


def _sds(shape, dtype):
    return jax.ShapeDtypeStruct(shape, dtype)


def _norm_mod(h, g, shift, scale):
    r = lax.rsqrt(jnp.mean(h * h, axis=-1, keepdims=True) + EPS)
    xhat = h * r
    y = xhat * g
    return r, xhat, y, y * (1.0 + scale) + shift


def _norm_mod_bwd(dn, r, xhat, y, g, scale):
    dshift = jnp.sum(dn, axis=0, keepdims=True)
    dscale = jnp.sum(dn * y, axis=0, keepdims=True)
    dy = dn * (1.0 + scale)
    dg = jnp.sum(dy * xhat, axis=0, keepdims=True)
    dxh = dy * g
    dh = r * (dxh - xhat * jnp.mean(dxh * xhat, axis=-1, keepdims=True))
    return dh, dshift, dscale, dg


def _acc_partials(part_ref, first, rows):
    @pl.when(first)
    def _():
        part_ref[...] = jnp.zeros_like(part_ref)

    for r, val in rows.items():
        part_ref[0, r : r + 1, :] += val


def _mod_spec(n_lat):
    return pl.BlockSpec((1, 16, D), lambda i: (i // n_lat, 0, 0))


def _part_spec(n_lat):
    return pl.BlockSpec((1, 8, D), lambda i: (i // n_lat, 0, 0))


def _load_weights(pairs, sem):
    copies = [pltpu.make_async_copy(src, dst, sem.at[k]) for k, (src, dst) in enumerate(pairs)]
    for cp in copies:
        cp.start()
    for cp in copies:
        cp.wait()


def _ffn_weight_copies(win_hbm, wout_hbm, win_v, wout_v, sem):
    loads = []
    for k, (c0, cw) in enumerate(FFN_CHUNKS):
        slabs = [(win_hbm, win_v, c0), (win_hbm, win_v, DFF + c0), (wout_hbm, wout_v, c0)]
        loads.append([pltpu.make_async_copy(src.at[pl.ds(r0, cw)], dst.at[pl.ds(r0, cw)], sem.at[3 * k + j])
                      for j, (src, dst, r0) in enumerate(slabs)])
    return loads


def _ffn_steps(i, n_active, loads, compute):
    @pl.when(i == 0)
    def _():
        for cp in sum(loads, []):
            cp.start()
        compute(loads)

    @pl.when(jnp.logical_and(i > 0, i < n_active))
    def _():
        compute(None)


def _wait_chunk(loads, k):
    if loads is not None:
        for cp in loads[k]:
            cp.wait()


def _ffn_fwd(h, modv, gvec, win, wout, *, T, mrow, grow, ctx_active, name, carry=None):
    R = h.shape[0]
    n_lat, n_tiles = T // TM, R // TM
    n_active = n_tiles if ctx_active else n_lat

    def body(h_ref, mod_ref, g_ref, win_hbm, wout_hbm, ho_ref, a_ref, b_ref, f_ref, win_v, wout_v, sem):
        i = pl.program_id(0)

        def compute(loads):
            h = h_ref[...]
            shift, scale, gate = (mod_ref[0, mrow + k : mrow + k + 1, :] for k in range(3))
            _, _, _, n = _norm_mod(h, g_ref[grow : grow + 1, :], shift, scale)
            n_bf = n.astype(BF16)
            acc = jnp.zeros((TM, D), F32)
            for k, (c0, cw) in enumerate(FFN_CHUNKS):
                _wait_chunk(loads, k)
                a = _dot_nt(n_bf, win_v[c0 : c0 + cw, :])
                b = _dot_nt(n_bf, win_v[DFF + c0 : DFF + c0 + cw, :])
                a_ref[:, c0 : c0 + cw] = a.astype(BF16)
                b_ref[:, c0 : c0 + cw] = b.astype(BF16)
                s = a * _sigmoid(a) * b
                acc = acc + _dot(s.astype(BF16), wout_v[c0 : c0 + cw, :])
            f_ref[...] = acc.astype(BF16)
            ho_ref[...] = h + (0.5 * gate) * acc

        _ffn_steps(i, n_active, _ffn_weight_copies(win_hbm, wout_hbm, win_v, wout_v, sem), compute)

        @pl.when(i >= n_active)
        def _():
            ho_ref[...] = h_ref[...]
            a_ref[...] = jnp.zeros_like(a_ref)
            b_ref[...] = jnp.zeros_like(b_ref)
            f_ref[...] = jnp.zeros_like(f_ref)

    return _call(
        body,
        name=name,
        grid=(n_tiles,),
        in_specs=[_rows(TM, D), _mod_spec(n_lat), _full((8, D)), ANY, ANY],
        out_specs=[_rows(TM, D), _rows(TM, DFF), _rows(TM, DFF), _rows(TM, D)],
        out_shape=[_sds((R, D), F32), _sds((R, DFF), BF16), _sds((R, DFF), BF16), _sds((R, D), BF16)],
        scratch_shapes=[pltpu.VMEM((2 * DFF, D), BF16), pltpu.VMEM((DFF, D), BF16),
                        pltpu.SemaphoreType.DMA((3 * len(FFN_CHUNKS),))],
        args=(h, modv, gvec, win, wout),
        carry=carry,
    )


def _ffn_bwd(h, dho, a, b, f, modv, gvec, win, wout, *, T, mrow, grow, ctx_active, name, carry=None):
    R = h.shape[0]
    n_lat, n_tiles = T // TM, R // TM
    n_active = n_tiles if ctx_active else n_lat

    def body(h_ref, dho_ref, a_ref, b_ref, f_ref, mod_ref, g_ref, win_hbm, wout_hbm,
             dh_ref, dab_ref, s_ref, n_ref, df_ref, part_ref, win_v, wout_v, sem):
        i = pl.program_id(0)
        first = jnp.logical_or(i == 0, i == n_lat)

        def compute(loads):
            h = h_ref[...]
            dho = dho_ref[...]
            shift, scale, gate = (mod_ref[0, mrow + k : mrow + k + 1, :] for k in range(3))
            g = g_ref[grow : grow + 1, :]
            r, xhat, y, n = _norm_mod(h, g, shift, scale)
            dgate = 0.5 * jnp.sum(dho * f_ref[...].astype(F32), axis=0, keepdims=True)
            df_bf = ((0.5 * gate) * dho).astype(BF16)
            df_ref[...] = df_bf
            n_ref[...] = n.astype(BF16)
            dn = jnp.zeros((TM, D), F32)
            for k, (c0, cw) in enumerate(FFN_CHUNKS):
                _wait_chunk(loads, k)
                ds = _dot_nt(df_bf, wout_v[c0 : c0 + cw, :])
                av = a_ref[:, c0 : c0 + cw].astype(F32)
                bv = b_ref[:, c0 : c0 + cw].astype(F32)
                sig = _sigmoid(av)
                sa = av * sig
                s_ref[:, c0 : c0 + cw] = (sa * bv).astype(BF16)
                da = (ds * bv * (sig * (1.0 + av * (1.0 - sig)))).astype(BF16)
                db = (ds * sa).astype(BF16)
                dab_ref[:, c0 : c0 + cw] = da
                dab_ref[:, DFF + c0 : DFF + c0 + cw] = db
                dn = dn + _dot(da, win_v[c0 : c0 + cw, :]) + _dot(db, win_v[DFF + c0 : DFF + c0 + cw, :])
            dh, dshift, dscale, dg = _norm_mod_bwd(dn, r, xhat, y, g, scale)
            dh_ref[...] = dho + dh
            _acc_partials(part_ref, first, {0: dshift, 1: dscale, 2: dgate, 3: dg})

        _ffn_steps(i, n_active, _ffn_weight_copies(win_hbm, wout_hbm, win_v, wout_v, sem), compute)

        @pl.when(i >= n_active)
        def _():
            dh_ref[...] = dho_ref[...]
            dab_ref[...] = jnp.zeros_like(dab_ref)
            s_ref[...] = jnp.zeros_like(s_ref)
            n_ref[...] = jnp.zeros_like(n_ref)
            df_ref[...] = jnp.zeros_like(df_ref)
            part_ref[...] = jnp.zeros_like(part_ref)

    return _call(
        body,
        name=name,
        grid=(n_tiles,),
        in_specs=[_rows(TM, D), _rows(TM, D), _rows(TM, DFF), _rows(TM, DFF), _rows(TM, D),
                  _mod_spec(n_lat), _full((8, D)), ANY, ANY],
        out_specs=[_rows(TM, D), _rows(TM, 2 * DFF), _rows(TM, DFF), _rows(TM, D), _rows(TM, D), _part_spec(n_lat)],
        out_shape=[_sds((R, D), F32), _sds((R, 2 * DFF), BF16), _sds((R, DFF), BF16), _sds((R, D), BF16),
                   _sds((R, D), BF16), _sds((2, 8, D), F32)],
        scratch_shapes=[pltpu.VMEM((2 * DFF, D), BF16), pltpu.VMEM((DFF, D), BF16),
                        pltpu.SemaphoreType.DMA((3 * len(FFN_CHUNKS),))],
        args=(h, dho, a, b, f, modv, gvec, win, wout),
        carry=carry,
    )


def _wgrad(x, y, *, bk, sh, name, carry=None):
    R, kx = x.shape
    n = y.shape[1]
    tr = R // 2
    nr, nsh = R // tr, bk // sh

    def body(x_ref, y_ref, o_ref, acc):
        r = pl.program_id(1)

        @pl.when(r == 0)
        def _():
            acc[...] = jnp.zeros_like(acc)

        acc[...] += _dot_tn(x_ref[...], y_ref[...])

        @pl.when(r == nr - 1)
        def _():
            for s in range(nsh):
                o_ref[s] = acc[s * sh : (s + 1) * sh, :].astype(BF16)

    (out,), got = _call(
        body,
        name=name,
        grid=(kx // bk, nr),
        in_specs=[pl.BlockSpec((tr, bk), lambda k, r: (r, k)), pl.BlockSpec((tr, n), lambda k, r: (r, 0))],
        out_specs=[pl.BlockSpec((nsh, sh, n), lambda k, r: (k, 0, 0))],
        out_shape=[_sds((kx // sh, sh, n), BF16)],
        scratch_shapes=[pltpu.VMEM((bk, n), F32)],
        args=(x, y),
        carry=carry,
    )
    return out, got


def _rot_half(x):
    lane = lax.broadcasted_iota(jnp.int32, x.shape, 1)
    return jnp.where((lane & (HD - 1)) < HD // 2, -pltpu.roll(x, 128 - HD // 2, 1), pltpu.roll(x, HD // 2, 1))


def _tile_sel():
    i = lax.broadcasted_iota(jnp.int32, (KVW, AW), 0)
    j = lax.broadcasted_iota(jnp.int32, (KVW, AW), 1)
    return jnp.where(i == (j // 256) * HD + (j & (HD - 1)), 1.0, 0.0).astype(BF16)


def _mixproj_fwd(h, modv, gvec, win, cos, sin, *, T, name, carry=None):
    R = h.shape[0]
    n_lat, n_tiles = T // TM, R // TM

    def body(h_ref, mod_ref, g_ref, win_ref, cos_ref, sin_ref, u_ref, q_ref, k4_ref, v4_ref):
        shift, scale = mod_ref[0, 3:4, :], mod_ref[0, 4:5, :]
        _, _, _, n = _norm_mod(h_ref[...], g_ref[1:2, :], shift, scale)
        proj = _dot_nt(n.astype(BF16), win_ref[...])
        u_ref[...] = proj[:, :PW]
        cs, sn = cos_ref[...], sin_ref[...]
        for s in range(AW // 128):
            x = proj[:, PW + 128 * s : PW + 128 * (s + 1)]
            q_ref[:, 128 * s : 128 * (s + 1)] = ((x * cs + _rot_half(x) * sn) * (HD ** -0.5)).astype(BF16)
        k = proj[:, PW + AW : PW + AW + KVW]
        k = (k * cs + _rot_half(k) * sn).astype(BF16)
        v = proj[:, PW + AW + KVW :].astype(BF16)
        sel = _tile_sel()
        k4_ref[...] = _dot(k, sel).astype(BF16)
        v4_ref[...] = _dot(v, sel).astype(BF16)

    return _call(
        body,
        name=name,
        grid=(n_tiles,),
        in_specs=[_rows(TM, D), _mod_spec(n_lat), _full((8, D)), _full((PROJ, D)), _rows(TM, 128), _rows(TM, 128)],
        out_specs=[_rows(TM, PW), _rows(TM, AW), _rows(TM, AW), _rows(TM, AW)],
        out_shape=[_sds((R, PW), F32), _sds((R, AW), BF16), _sds((R, AW), BF16), _sds((R, AW), BF16)],
        scratch_shapes=[],
        args=(h, modv, gvec, win, cos, sin),
        carry=carry,
    )


def _win_start(j, hi):
    return pl.multiple_of(jnp.clip((j - 1) * BLK, 0, hi - 3 * BLK), BLK)


def _hi_lo(x):
    hi = x.astype(BF16)
    return hi, (x - hi.astype(F32)).astype(BF16)


def _pool_bounds(t, w, T, R):
    is_ctx = t >= T
    lo = jnp.maximum(t - w // 2, jnp.where(is_ctx, T, 0))
    hi = jnp.minimum(t + w // 2, jnp.where(is_ctx, R, T))
    return lo, hi


def _pooled(u_v, j, T, R):
    start = _win_start(j, R)
    u3_hi, u3_lo = _hi_lo(u_v[pl.ds(start, 3 * BLK), :])
    ub = u_v[pl.ds(pl.multiple_of(j * BLK, BLK), BLK), :]
    t = j * BLK + lax.broadcasted_iota(jnp.int32, (BLK, 1), 0)
    pos = start + lax.broadcasted_iota(jnp.int32, (1, 3 * BLK), 1)
    pooled, counts = [], []
    for g, w in enumerate(POOL_WINDOWS):
        lo, hi = _pool_bounds(t, w, T, R)
        band = jnp.where(pos >= lo, jnp.where(pos < hi, 1.0, 0.0), 0.0).astype(BF16)
        sl = slice(g * 128, (g + 1) * 128)
        sums = _dot(band, u3_hi[:, sl]) + _dot(band, u3_lo[:, sl])
        cnt = (hi - lo).astype(F32)
        pooled.append(sums / cnt - ub[:, sl])
        counts.append(cnt)
    return pooled, counts


def _stack_heads(x):
    lane_h = lax.broadcasted_iota(jnp.int32, x.shape, 1) // HD
    return jnp.concatenate([jnp.where(lane_h == h, x, jnp.zeros_like(x)) for h in range(4)], axis=0)


def _unstack_heads(x):
    lane_h = lax.broadcasted_iota(jnp.int32, (BLK, 256), 1) // HD
    out = jnp.zeros((BLK, 256), F32)
    for h in range(4):
        out = out + jnp.where(lane_h == h, x[h * BLK : (h + 1) * BLK, :], 0.0)
    return out


def _window_mask(j, start_l, nbl):
    rowi = lax.broadcasted_iota(jnp.int32, (4 * BLK, 1), 0)
    qpos = j * BLK + (rowi & (BLK - 1))
    kpos = start_l + lax.broadcasted_iota(jnp.int32, (1, 3 * BLK), 1)
    reach = jnp.where(j < nbl, BLK, -1)
    return jnp.abs(kpos - qpos) <= reach


def _attn_exps(qs, kl, kc, sink_ref, g, valid):
    s_l = jnp.where(valid, _dot_nt(qs, kl), NEG)
    s_c = _dot_nt(qs, kc)
    rb = lax.broadcasted_iota(jnp.int32, (4 * BLK, 1), 0) // BLK
    sk = jnp.where(rb == 0, sink_ref[4 * g], jnp.where(rb == 1, sink_ref[4 * g + 1],
                   jnp.where(rb == 2, sink_ref[4 * g + 2], sink_ref[4 * g + 3])))
    m = jnp.maximum(jnp.maximum(jnp.max(s_l, axis=1, keepdims=True), jnp.max(s_c, axis=1, keepdims=True)), sk)
    e_l, e_c, e_s = jnp.exp(s_l - m), jnp.exp(s_c - m), jnp.exp(sk - m)
    inv = 1.0 / (jnp.sum(e_l, axis=1, keepdims=True) + jnp.sum(e_c, axis=1, keepdims=True) + e_s)
    return e_l, e_c, e_s, inv


def _attnpool_fwd(u, q, k4, v4, sink, w_pool, pool_scale, *, T, name, carry=None):
    R = u.shape[0]
    nb, nbl = R // BLK, T // BLK

    def body(q_ref, sink_ref, wp_ref, ps_ref, u_hbm, k4_hbm, v4_hbm, cat_ref, u_v, k4_v, v4_v, sem):
        j = pl.program_id(0)

        @pl.when(j == 0)
        def _():
            _load_weights([(u_hbm, u_v), (k4_hbm, k4_v), (v4_hbm, v4_v)], sem)

        pooled, _ = _pooled(u_v, j, T, R)
        for g in range(4):
            mixed = _dot(pooled[g].astype(BF16), wp_ref[g].astype(BF16)) * ps_ref[:, g * 128 : (g + 1) * 128]
            cat_ref[:, g * 128 : (g + 1) * 128] = mixed.astype(BF16)

        start_l = _win_start(j, T)
        valid = _window_mask(j, start_l, nbl)
        for g in range(2):
            gl = slice(g * 256, (g + 1) * 256)
            qs = _stack_heads(q_ref[:, gl])
            e_l, e_c, _, inv = _attn_exps(qs, k4_v[pl.ds(start_l, 3 * BLK), gl], k4_v[T:R, gl], sink_ref, g, valid)
            o = _dot(e_l.astype(BF16), v4_v[pl.ds(start_l, 3 * BLK), gl]) + _dot(e_c.astype(BF16), v4_v[T:R, gl])
            cat_ref[:, PW + g * 256 : PW + (g + 1) * 256] = _unstack_heads(o * inv).astype(BF16)

    return _call(
        body,
        name=name,
        grid=(nb,),
        in_specs=[_rows(BLK, AW), SMEM, _full((4, 128, 128)), _full((1, PW)), ANY, ANY, ANY],
        out_specs=[_rows(BLK, D)],
        out_shape=[_sds((R, D), BF16)],
        scratch_shapes=[pltpu.VMEM((R, PW), F32), pltpu.VMEM((R, AW), BF16), pltpu.VMEM((R, AW), BF16),
                        pltpu.SemaphoreType.DMA((3,))],
        args=(q, sink, w_pool, pool_scale, u, k4, v4),
        carry=carry,
    )


def _mixout_fwd(h, cat, modv, wout, *, T, ctx_active, name, carry=None):
    R = h.shape[0]
    n_lat, n_tiles = T // TM, R // TM

    def body(h_ref, cat_ref, mod_ref, w_ref, ho_ref, mo_ref):
        i = pl.program_id(0)

        def compute():
            mo = _dot(cat_ref[...], w_ref[...])
            mo_ref[...] = mo.astype(BF16)
            ho_ref[...] = h_ref[...] + mod_ref[0, 5:6, :] * mo

        if ctx_active:
            compute()
        else:
            pl.when(i < n_lat)(compute)

            @pl.when(i >= n_lat)
            def _():
                ho_ref[...] = h_ref[...]
                mo_ref[...] = jnp.zeros_like(mo_ref)

    return _call(
        body,
        name=name,
        grid=(n_tiles,),
        in_specs=[_rows(TM, D), _rows(TM, D), _mod_spec(n_lat), _full((D, D))],
        out_specs=[_rows(TM, D), _rows(TM, D)],
        out_shape=[_sds((R, D), F32), _sds((R, D), BF16)],
        scratch_shapes=[],
        args=(h, cat, modv, wout),
        carry=carry,
    )


def _mixout_bwd(dho, mo, modv, wout, *, T, ctx_active, name, carry=None):
    R = dho.shape[0]
    n_lat, n_tiles = T // TM, R // TM

    def body(dho_ref, mo_ref, mod_ref, w_ref, dcat_ref, dmix_ref, part_ref):
        i = pl.program_id(0)
        first = jnp.logical_or(i == 0, i == n_lat)

        def compute():
            dho = dho_ref[...]
            dmix = (mod_ref[0, 5:6, :] * dho).astype(BF16)
            dmix_ref[...] = dmix
            dcat_ref[...] = _dot_nt(dmix, w_ref[...])
            dgate = jnp.sum(dho * mo_ref[...].astype(F32), axis=0, keepdims=True)
            _acc_partials(part_ref, first, {2: dgate})

        if ctx_active:
            compute()
        else:
            pl.when(i < n_lat)(compute)

            @pl.when(i >= n_lat)
            def _():
                dcat_ref[...] = jnp.zeros_like(dcat_ref)
                dmix_ref[...] = jnp.zeros_like(dmix_ref)
                part_ref[...] = jnp.zeros_like(part_ref)

    return _call(
        body,
        name=name,
        grid=(n_tiles,),
        in_specs=[_rows(TM, D), _rows(TM, D), _mod_spec(n_lat), _full((D, D))],
        out_specs=[_rows(TM, D), _rows(TM, D), _part_spec(n_lat)],
        out_shape=[_sds((R, D), F32), _sds((R, D), BF16), _sds((2, 8, D), F32)],
        scratch_shapes=[],
        args=(dho, mo, modv, wout),
        carry=carry,
    )


def _pool_bwd(u, dcat, w_pool, pool_scale, *, T, name):
    R = u.shape[0]
    nb = R // BLK

    def body(dcat_ref, wp_ref, ps_ref, u_hbm, dps_ref, dwp_ref, dsc_ref, u_v, sem):
        j = pl.program_id(0)

        @pl.when(j == 0)
        def _():
            _load_weights([(u_hbm, u_v)], sem)
            dwp_ref[...] = jnp.zeros_like(dwp_ref)
            dsc_ref[...] = jnp.zeros_like(dsc_ref)

        pooled, counts = _pooled(u_v, j, T, R)
        for g in range(4):
            sl = slice(g * 128, (g + 1) * 128)
            p_bf = pooled[g].astype(BF16)
            w_bf = wp_ref[g].astype(BF16)
            dmixed = dcat_ref[:, sl]
            dsc_ref[0:1, sl] += jnp.sum(dmixed * _dot(p_bf, w_bf), axis=0, keepdims=True)
            dmp = (dmixed * ps_ref[:, sl]).astype(BF16)
            dwp_ref[sl, :] += _dot_tn(p_bf, dmp)
            dps_ref[:, sl] = _dot_nt(dmp, w_bf) / counts[g]

    return pl.pallas_call(
        body,
        name=name,
        grid=(nb,),
        in_specs=[_rows(BLK, D), _full((4, 128, 128)), _full((1, PW)), ANY],
        out_specs=[_rows(BLK, PW), _full((PW, 128)), _full((8, PW))],
        out_shape=[_sds((R, PW), F32), _sds((PW, 128), F32), _sds((8, PW), F32)],
        scratch_shapes=[pltpu.VMEM((R, PW), F32), pltpu.SemaphoreType.DMA((1,))],
        compiler_params=_params(),
    )(dcat, w_pool, pool_scale, u)


def _fold_heads(x):
    y = x[:, :128] + x[:, 128:]
    return y + pltpu.roll(y, HD, 1)


def _attn_bwd(q, k4, v4, dcat, dps, sink, *, T, name, carry=None):
    R = q.shape[0]
    nb, nbl = R // BLK, T // BLK

    def body(q_ref, dcat_ref, sink_ref, k4_hbm, v4_hbm, dps_hbm, du_ref, dq_ref, dk_ref, dv_ref, dsk_ref,
             k4_v, v4_v, dps_v, sem):
        j = pl.program_id(0)

        @pl.when(j == 0)
        def _():
            _load_weights([(k4_hbm, k4_v), (v4_hbm, v4_v), (dps_hbm, dps_v)], sem)
            dk_ref[...] = jnp.zeros_like(dk_ref)
            dv_ref[...] = jnp.zeros_like(dv_ref)
            dsk_ref[...] = jnp.zeros_like(dsk_ref)

        start = _win_start(j, R)
        d3_hi, d3_lo = _hi_lo(dps_v[pl.ds(start, 3 * BLK), :])
        db = dps_v[pl.ds(pl.multiple_of(j * BLK, BLK), BLK), :]
        pos = j * BLK + lax.broadcasted_iota(jnp.int32, (BLK, 1), 0)
        t_r = start + lax.broadcasted_iota(jnp.int32, (1, 3 * BLK), 1)
        for g, w in enumerate(POOL_WINDOWS):
            sl = slice(g * 128, (g + 1) * 128)
            lo_r, hi_r = _pool_bounds(t_r, w, T, R)
            band_t = jnp.where(pos >= lo_r, jnp.where(pos < hi_r, 1.0, 0.0), 0.0).astype(BF16)
            lo_c, hi_c = _pool_bounds(pos, w, T, R)
            du_ref[:, sl] = _dot(band_t, d3_hi[:, sl]) + _dot(band_t, d3_lo[:, sl]) - db[:, sl] * (hi_c - lo_c).astype(F32)

        start_l = _win_start(j, T)
        valid = _window_mask(j, start_l, nbl)
        rb = lax.broadcasted_iota(jnp.int32, (4 * BLK, 1), 0) // BLK
        lane = lax.broadcasted_iota(jnp.int32, (1, 128), 1)
        dk_l, dk_c, dv_l, dv_c = [], [], [], []
        for g in range(2):
            gl = slice(g * 256, (g + 1) * 256)
            qs = _stack_heads(q_ref[:, gl])
            kl, kc = k4_v[pl.ds(start_l, 3 * BLK), gl], k4_v[T:R, gl]
            vl, vc = v4_v[pl.ds(start_l, 3 * BLK), gl], v4_v[T:R, gl]
            e_l, e_c, e_s, inv = _attn_exps(qs, kl, kc, sink_ref, g, valid)
            p_l, p_c, p_s = e_l * inv, e_c * inv, e_s * inv
            dos = _stack_heads(dcat_ref[:, PW + g * 256 : PW + (g + 1) * 256]).astype(BF16)
            dp_l, dp_c = _dot_nt(dos, vl), _dot_nt(dos, vc)
            delta = jnp.sum(p_l * dp_l, axis=1, keepdims=True) + jnp.sum(p_c * dp_c, axis=1, keepdims=True)
            ds_l = (p_l * (dp_l - delta)).astype(BF16)
            ds_c = (p_c * (dp_c - delta)).astype(BF16)
            dq_ref[:, gl] = _unstack_heads(_dot(ds_l, kl) + _dot(ds_c, kc)) * (HD ** -0.5)
            dk_l.append(_fold_heads(_dot_tn(ds_l, qs)))
            dk_c.append(_fold_heads(_dot_tn(ds_c, qs)))
            dv_l.append(_fold_heads(_dot_tn(p_l.astype(BF16), dos)))
            dv_c.append(_fold_heads(_dot_tn(p_c.astype(BF16), dos)))
            dsink = -p_s * delta
            for h in range(4):
                tot = jnp.sum(jnp.where(rb == h, dsink, 0.0), axis=0, keepdims=True)
                dsk_ref[4 * g + h : 4 * g + h + 1, :] += jnp.broadcast_to(tot, (1, 128))
        first = lane < HD
        dk_ref[pl.ds(start_l, 3 * BLK), :] += jnp.where(first, dk_l[0], dk_l[1])
        dk_ref[T:R, :] += jnp.where(first, dk_c[0], dk_c[1])
        dv_ref[pl.ds(start_l, 3 * BLK), :] += jnp.where(first, dv_l[0], dv_l[1])
        dv_ref[T:R, :] += jnp.where(first, dv_c[0], dv_c[1])

    return _call(
        body,
        name=name,
        grid=(nb,),
        in_specs=[_rows(BLK, AW), _rows(BLK, D), SMEM, ANY, ANY, ANY],
        out_specs=[_rows(BLK, PW), _rows(BLK, AW), _full((R, KVW)), _full((R, KVW)), _full((8, 128))],
        out_shape=[_sds((R, PW), F32), _sds((R, AW), F32), _sds((R, KVW), F32), _sds((R, KVW), F32),
                   _sds((8, 128), F32)],
        scratch_shapes=[pltpu.VMEM((R, AW), BF16), pltpu.VMEM((R, AW), BF16), pltpu.VMEM((R, PW), F32),
                        pltpu.SemaphoreType.DMA((3,))],
        args=(q, dcat, sink, k4, v4, dps),
        carry=carry,
    )


def _mixproj_bwd(h, dho, du, dq, dk, dv, modv, gvec, win, cos, sin, *, T, name):
    R = h.shape[0]
    n_lat, n_tiles = T // TM, R // TM

    def body(h_ref, dho_ref, du_ref, dq_ref, dk_ref, dv_ref, mod_ref, g_ref, win_ref, cos_ref, sin_ref,
             dh_ref, dproj_ref, n_ref, part_ref):
        i = pl.program_id(0)
        first = jnp.logical_or(i == 0, i == n_lat)
        shift, scale = mod_ref[0, 3:4, :], mod_ref[0, 4:5, :]
        g = g_ref[1:2, :]
        r, xhat, y, n = _norm_mod(h_ref[...], g, shift, scale)
        n_ref[...] = n.astype(BF16)
        cs, sn = cos_ref[...], sin_ref[...]
        dproj_ref[:, :PW] = du_ref[...].astype(BF16)
        for s in range(AW // 128):
            x = dq_ref[:, 128 * s : 128 * (s + 1)]
            dproj_ref[:, PW + 128 * s : PW + 128 * (s + 1)] = (x * cs - _rot_half(x) * sn).astype(BF16)
        x = dk_ref[...]
        dproj_ref[:, PW + AW : PW + AW + KVW] = (x * cs - _rot_half(x) * sn).astype(BF16)
        dproj_ref[:, PW + AW + KVW :] = dv_ref[...].astype(BF16)
        dn = _dot(dproj_ref[...], win_ref[...])
        dh, dshift, dscale, dg = _norm_mod_bwd(dn, r, xhat, y, g, scale)
        dh_ref[...] = dho_ref[...] + dh
        _acc_partials(part_ref, first, {0: dshift, 1: dscale, 3: dg})

    return pl.pallas_call(
        body,
        name=name,
        grid=(n_tiles,),
        in_specs=[_rows(TM, D), _rows(TM, D), _rows(TM, PW), _rows(TM, AW), _rows(TM, KVW), _rows(TM, KVW),
                  _mod_spec(n_lat), _full((8, D)), _full((PROJ, D)), _rows(TM, 128), _rows(TM, 128)],
        out_specs=[_rows(TM, D), _rows(TM, PROJ), _rows(TM, D), _part_spec(n_lat)],
        out_shape=[_sds((R, D), F32), _sds((R, PROJ), BF16), _sds((R, D), BF16), _sds((2, 8, D), F32)],
        compiler_params=_params(),
    )(h, dho, du, dq, dk, dv, modv, gvec, win, cos, sin)


def _loss_head(h, target, g_final, *, T, name):
    R = h.shape[0]
    n_lat, n_tiles = T // TM, R // TM

    def body(h_ref, t_ref, g_ref, dh_ref, loss_ref, dg_ref):
        i = pl.program_id(0)

        @pl.when(i == 0)
        def _():
            loss_ref[...] = jnp.zeros_like(loss_ref)
            dg_ref[...] = jnp.zeros_like(dg_ref)

        @pl.when(i < n_lat)
        def _():
            h = h_ref[...]
            g = g_ref[...]
            r = lax.rsqrt(jnp.mean(h * h, axis=-1, keepdims=True) + EPS)
            xhat = h * r
            err = xhat * g - t_ref[...]
            tot = jnp.sum(jnp.sum(err * err, axis=1, keepdims=True), axis=0, keepdims=True)
            loss_ref[...] += jnp.broadcast_to(tot * (0.5 / D), loss_ref.shape)
            dy = err * (1.0 / D)
            dg_ref[0:1, :] += jnp.sum(dy * xhat, axis=0, keepdims=True)
            dxh = dy * g
            dh_ref[...] = r * (dxh - xhat * jnp.mean(dxh * xhat, axis=-1, keepdims=True))

        @pl.when(i >= n_lat)
        def _():
            dh_ref[...] = jnp.zeros_like(dh_ref)

    return pl.pallas_call(
        body,
        name=name,
        grid=(n_tiles,),
        in_specs=[_rows(TM, D), pl.BlockSpec((TM, D), lambda i: (jnp.minimum(i, n_lat - 1), 0)), _full((1, D))],
        out_specs=[_rows(TM, D), _full((8, 128)), _full((8, D))],
        out_shape=[_sds((R, D), F32), _sds((8, 128), F32), _sds((8, D), F32)],
        compiler_params=_params(),
    )(h, target, g_final)


def _mod_fwd(c16, w_mod, b_cols, *, name):
    nl, _, cols = w_mod.shape

    def body(c_ref, w_ref, b_ref, o_ref):
        c = c_ref[...]
        sc = (c * _sigmoid(c)).astype(BF16)
        o_ref[0] = _dot(sc, w_ref[0].astype(BF16)) + b_ref[0]

    return pl.pallas_call(
        body,
        name=name,
        grid=(nl,),
        in_specs=[_full((16, D)), pl.BlockSpec((1, D, cols), lambda l: (l, 0, 0)),
                  pl.BlockSpec((1, 1, cols), lambda l: (l, 0, 0))],
        out_specs=pl.BlockSpec((1, 16, cols), lambda l: (l, 0, 0)),
        out_shape=_sds((nl, 16, cols), F32),
        compiler_params=_params(),
    )(c16, w_mod, b_cols)


def _mod_bwd(c16, dm_cols, w_mod, *, name):
    nl, _, cols = w_mod.shape

    def body(c_ref, dm_ref, w_ref, gw_ref, dc_ref):
        c = c_ref[...]
        sc = (c * _sigmoid(c)).astype(BF16)
        dm = dm_ref[0].astype(BF16)
        gw_ref[0] = _dot_tn(sc, dm)
        dc_ref[0] = _dot_nt(dm, w_ref[0].astype(BF16))

    return pl.pallas_call(
        body,
        name=name,
        grid=(nl,),
        in_specs=[_full((16, D)), pl.BlockSpec((1, 16, cols), lambda l: (l, 0, 0)),
                  pl.BlockSpec((1, D, cols), lambda l: (l, 0, 0))],
        out_specs=[pl.BlockSpec((1, D, cols), lambda l: (l, 0, 0)), pl.BlockSpec((1, 16, D), lambda l: (l, 0, 0))],
        out_shape=[_sds((nl, D, cols), F32), _sds((nl, 16, D), F32)],
        compiler_params=_params(),
    )(c16, dm_cols, w_mod)


def _coords():
    return lax.axis_index("x"), lax.axis_index("y"), lax.axis_index("c")


def _peer(k, x, y, c):
    return (1 - x if k & 4 else x, 1 - y if k & 2 else y, 1 - c if k & 1 else c)


def _lin(p):
    return 4 * p[0] + 2 * p[1] + p[2]


def _view(ref, slot):
    return ref if slot is None else ref.at[slot]


class _Round:
    def __init__(self, ins, out_shapes, plan, local_plan=(), n_alias=0):
        self.ins, self.out_shapes = list(ins), list(out_shapes)
        self.plan, self.local_plan, self.n_alias = list(plan), list(local_plan), n_alias

    def sems(self):
        return [pltpu.SemaphoreType.DMA((len(self.plan),)), pltpu.SemaphoreType.DMA((len(self.plan),)),
                pltpu.SemaphoreType.DMA((max(len(self.local_plan), 1),))]

    def _copies(self, in_refs, out_refs, sems, incoming, links=("ici", "d2d")):
        in_refs = list(out_refs[: self.n_alias]) + list(in_refs[self.n_alias :])
        send_sems, recv_sems, loc_sems = sems
        x, y, c = _coords()
        me = _lin((x, y, c))
        remote = []
        for idx, (k, ii, sfn, oi, dfn) in enumerate(self.plan):
            if ("d2d" if k == 1 else "ici") not in links:
                continue
            peer = _peer(k, x, y, c)
            sender, receiver = (_lin(peer), me) if incoming else (me, _lin(peer))
            remote.append(pltpu.make_async_remote_copy(
                src_ref=_view(in_refs[ii], sfn(sender, receiver)), dst_ref=_view(out_refs[oi], dfn(sender, receiver)),
                send_sem=send_sems.at[idx], recv_sem=recv_sems.at[idx], device_id=peer, device_id_type=MESH))
        locs = [pltpu.make_async_copy(_view(in_refs[ii], sfn(me)), _view(out_refs[oi], dfn(me)), loc_sems.at[idx])
                for idx, (ii, sfn, oi, dfn) in enumerate(() if incoming or "ici" not in links else self.local_plan)]
        return remote, locs

    def start(self, in_refs, out_refs, sems, links=("ici", "d2d")):
        sends, locs = self._copies(in_refs, out_refs, sems, incoming=False, links=links)
        for cp in sends + locs:
            cp.start()

    def finish(self, in_refs, out_refs, sems):
        for cp in self._copies(in_refs, out_refs, sems, incoming=True)[0]:
            cp.wait_recv()
        sends, locs = self._copies(in_refs, out_refs, sems, incoming=False)
        for cp in sends:
            cp.wait_send()
        for cp in locs:
            cp.wait()


def _exchange(name, rnd):
    n_in, n_out = len(rnd.ins), len(rnd.out_shapes)

    def body(*refs):
        in_refs, out_refs, sems = refs[:n_in], refs[n_in : n_in + n_out], refs[n_in + n_out :]
        rnd.start(in_refs, out_refs, sems)
        rnd.finish(in_refs, out_refs, sems)

    return pl.pallas_call(
        body, name=name, in_specs=[ANY] * n_in, out_specs=[ANY] * n_out, out_shape=rnd.out_shapes,
        scratch_shapes=rnd.sems(), input_output_aliases={i: i for i in range(rnd.n_alias)})(*rnd.ins)


def _call(body, *, name, grid, in_specs, out_specs, out_shape, scratch_shapes, args, carry=None):
    params = _params(len(grid))
    if carry is None:
        outs = pl.pallas_call(body, name=name, grid=grid, in_specs=in_specs, out_specs=out_specs, out_shape=out_shape,
                              scratch_shapes=scratch_shapes, compiler_params=params)(*args)
        return list(outs), []
    n_ci, n_co, n_cs = len(in_specs), len(out_shape), len(scratch_shapes)
    n_xi, n_xo = len(carry.ins), len(carry.out_shapes)

    def wrapped(*refs):
        ci, xi = refs[:n_ci], refs[n_ci : n_ci + n_xi]
        o0 = n_ci + n_xi
        co, xo = refs[o0 : o0 + n_co], refs[o0 + n_co : o0 + n_co + n_xo]
        s0 = o0 + n_co + n_xo
        cs, sems = refs[s0 : s0 + n_cs], refs[s0 + n_cs :]
        ids = [pl.program_id(a) for a in range(len(grid))]
        first = functools.reduce(jnp.logical_and, [i == 0 for i in ids])
        last = functools.reduce(jnp.logical_and, [i == g - 1 for i, g in zip(ids, grid)])

        @pl.when(first)
        def _():
            carry.start(xi, xo, sems, links=("ici",))

        body(*ci, *co, *cs)

        @pl.when(first)
        def _():
            carry.start(xi, xo, sems, links=("d2d",))

        @pl.when(last)
        def _():
            carry.finish(xi, xo, sems)

    outs = pl.pallas_call(
        wrapped, name=name, grid=grid, in_specs=list(in_specs) + [ANY] * n_xi, out_specs=list(out_specs) + [ANY] * n_xo,
        out_shape=list(out_shape) + carry.out_shapes, scratch_shapes=list(scratch_shapes) + carry.sems(),
        input_output_aliases={n_ci + i: n_co + i for i in range(carry.n_alias)}, compiler_params=params,
    )(*args, *carry.ins)
    return list(outs[:n_co]), list(outs[n_co:])


def _gather_direct(arrays):
    na = len(arrays)
    outs = [_sds((NDEV,) + a.shape, a.dtype) for a in arrays]
    plan = [(k, i, lambda s, r: None, i, lambda s, r: s) for i in range(na) for k in range(1, NDEV)]
    return _Round(arrays, outs, plan, [(i, lambda m: None, i, lambda m: m) for i in range(na)])


def _gather_a(arrays):
    na = len(arrays)
    outs = [_sds((NDEV,) + a.shape, a.dtype) for a in arrays]
    plan = [(k, i, lambda s, r: None, i, lambda s, r: s) for i in range(na) for k in (2, 4, 6)]
    return _Round(arrays, outs, plan, [(i, lambda m: None, i, lambda m: m) for i in range(na)])


def _gather_b(got):
    na = len(got)
    plan = [(1, i, (lambda s, r, k=k: s ^ k), i, (lambda s, r, k=k: s ^ k)) for i in range(na) for k in (0, 2, 4, 6)]
    return _Round(got, [_sds(g.shape, g.dtype) for g in got], plan, n_alias=na)


def _scatter_1(grads):
    plan = [(1, i, (lambda s, r, q=q: 2 * q + (r & 1)), i, (lambda s, r, q=q: q))
            for i in range(len(grads)) for q in range(4)]
    return _Round(grads, [_sds((4,) + g.shape[1:], g.dtype) for g in grads], plan)


def _scatter_2(chip):
    plan = [(k, i, lambda s, r: r >> 1, i, (lambda s, r, j=j: j)) for i in range(len(chip)) for j, k in enumerate((2, 4, 6))]
    return _Round(chip, [_sds((3,) + g.shape[1:], g.dtype) for g in chip], plan)


def _add_pairs(g, got, pos, *, name):
    _, sh, w = g.shape

    def body(pos_ref, g_ref, r_ref, o_ref):
        o_ref[...] = (g_ref[...].astype(F32) + r_ref[...].astype(F32)).astype(o_ref.dtype)

    return pl.pallas_call(
        body,
        name=name,
        grid_spec=pltpu.PrefetchScalarGridSpec(
            num_scalar_prefetch=1, grid=(4,),
            in_specs=[pl.BlockSpec((1, sh, w), lambda q, p: (2 * q + p[0], 0, 0)),
                      pl.BlockSpec((1, sh, w), lambda q, p: (q, 0, 0))],
            out_specs=pl.BlockSpec((1, sh, w), lambda q, p: (q, 0, 0))),
        out_shape=_sds((4, sh, w), g.dtype),
        compiler_params=_params(),
    )(pos, g, got)


def _sum_chips(chip, got, pos, *, transpose, name):
    _, sh, w = chip.shape
    out = (w, sh) if transpose else (sh, w)

    def body(pos_ref, c_ref, r_ref, o_ref):
        acc = c_ref[0].astype(F32)
        for s in range(3):
            acc = acc + r_ref[s].astype(F32)
        o_ref[...] = acc.T if transpose else acc

    return pl.pallas_call(
        body,
        name=name,
        grid_spec=pltpu.PrefetchScalarGridSpec(
            num_scalar_prefetch=1, grid=(1,),
            in_specs=[pl.BlockSpec((1, sh, w), lambda i, p: (p[1], 0, 0)), pl.BlockSpec((3, sh, w), lambda i, p: (0, 0, 0))],
            out_specs=pl.BlockSpec(out, lambda i, p: (0, 0))),
        out_shape=_sds(out, F32),
        compiler_params=_params(),
    )(pos, chip, got)


def _adamw_math(w, g, m, v):
    m2 = ADAM_B1 * m + (1.0 - ADAM_B1) * g
    v2 = ADAM_B2 * v + (1.0 - ADAM_B2) * (g * g)
    m_hat = m2 / (1.0 - ADAM_B1 ** ADAM_STEP)
    v_hat = v2 / (1.0 - ADAM_B2 ** ADAM_STEP)
    delta = -ADAM_LR * (m_hat / (jnp.sqrt(v_hat) + ADAM_EPS) + ADAM_WD * w)
    return delta, m2, v2


def _adamw(w, g, m, v, *, name):
    shape = w.shape
    flat = [t.reshape(-1, shape[-1]) for t in (w, g, m, v)]
    rows, cols = flat[0].shape
    tr = rows // 8 if rows % 64 == 0 else rows
    spec = _rows(tr, cols)

    def body(w_ref, g_ref, m_ref, v_ref, d_ref, m2_ref, v2_ref):
        d_ref[...], m2_ref[...], v2_ref[...] = _adamw_math(w_ref[...], g_ref[...], m_ref[...], v_ref[...])

    outs = pl.pallas_call(
        body, name=name, grid=(rows // tr,), in_specs=[spec] * 4, out_specs=[spec] * 3,
        out_shape=[_sds((rows, cols), F32)] * 3, compiler_params=_params())(*flat)
    return tuple(o.reshape(shape) for o in outs)


def _small_sums(packets, nf, dwp, dsc, dsk, *, name):
    flat = [p for layer in packets for p in layer]

    def total(ref, *idx):
        acc = ref[(0,) + idx]
        for dev in range(1, NDEV):
            acc = acc + ref[(dev,) + idx]
        return acc

    def body(*refs):
        pk = refs[:6]
        nf_ref, dwp0, dwp1, dsc0, dsc1, dsk0, dsk1 = refs[6:13]
        dm_ref, gb_ref, gn_ref, gnf_ref, gwp_ref, gps_ref, gsk_ref = refs[13:]
        dm_ref[...] = jnp.zeros_like(dm_ref)
        gn_ref[...] = jnp.zeros_like(gn_ref)
        for l in range(2):
            for sb in range(3):
                p = pk[3 * l + sb]
                for r in range(3):
                    col = slice((3 * sb + r) * D, (3 * sb + r + 1) * D)
                    lat = p[0, 0, r : r + 1, :]
                    dm_ref[l, 0:1, col] = lat
                    for dev in range(1, NDEV):
                        row = p[dev, 0, r : r + 1, :]
                        dm_ref[l, dev : dev + 1, col] = row
                        lat = lat + row
                    ctx = total(p, 1, slice(r, r + 1), slice(None))
                    dm_ref[l, 8:9, col] = ctx
                    gb_ref[l : l + 1, col] = lat + ctx
                gn_ref[l, sb : sb + 1, :] = total(p, 0, slice(3, 4), slice(None)) + total(p, 1, slice(3, 4), slice(None))
        gnf_ref[...] = total(nf_ref, slice(0, 1), slice(None))
        for l, (a, b, c) in enumerate(((dwp0, dsc0, dsk0), (dwp1, dsc1, dsk1))):
            gwp_ref[l] = total(a, slice(None), slice(None))
            gps_ref[l : l + 1, :] = total(b, slice(0, 1), slice(None))
            gsk_ref[l] = total(c, slice(None), slice(None))

    ins = flat + [nf, dwp[0], dwp[1], dsc[0], dsc[1], dsk[0], dsk[1]]
    return pl.pallas_call(
        body,
        name=name,
        out_shape=[_sds((2, 16, NMOD * D), F32), _sds((2, NMOD * D), F32), _sds((2, 8, D), F32), _sds((1, D), F32),
                   _sds((2, PW, 128), F32), _sds((2, PW), F32), _sds((2, 8, 128), F32)],
        compiler_params=pltpu.CompilerParams(vmem_limit_bytes=VMEM_LIMIT),
    )(*ins)


def _small_adamw(c_ctx, dc_all, triples, *, name):
    n = len(triples)

    def body(*refs):
        c_ref, dc_ref = refs[0], refs[1]
        ins = refs[2 : 2 + 4 * n - 1]
        outs = refs[2 + 4 * n - 1 :]
        acc = dc_ref[0, 0, 8:9, :] + dc_ref[0, 1, 8:9, :]
        for dev in range(1, NDEV):
            acc = acc + (dc_ref[dev, 0, 8:9, :] + dc_ref[dev, 1, 8:9, :])
        c = c_ref[...]
        sig = _sigmoid(c)
        g_c = acc * (sig * (1.0 + c * (1.0 - sig)))
        outs[0][...] = g_c
        pos = 0
        for k in range(n):
            if k == 0:
                w, g, m, v = ins[0][...], g_c, ins[1][...], ins[2][...]
                pos = 3
            else:
                w, g, m, v = (ins[pos + t][...] for t in range(4))
                pos += 4
            d, m2, v2 = _adamw_math(w, g, m, v)
            outs[1 + 3 * k][...], outs[2 + 3 * k][...], outs[3 + 3 * k][...] = d, m2, v2

    flat_in = [c_ctx, dc_all]
    out_shape = [_sds(c_ctx.shape, F32)]
    for k, (w, g, m, v) in enumerate(triples):
        flat_in += [w, m, v] if k == 0 else [w, g, m, v]
        out_shape += [_sds(w.shape, F32)] * 3
    return pl.pallas_call(body, name=name, out_shape=out_shape,
                          compiler_params=pltpu.CompilerParams(vmem_limit_bytes=VMEM_LIMIT))(*flat_in)


def _rope_tables(T, R):
    t = jnp.arange(T)
    inv = ROPE_BASE ** (-jnp.arange(0, HD // 2, 2, dtype=F32) / (HD // 2))
    ang = jnp.concatenate([(t // GRID_W).astype(F32)[:, None] * inv, (t % GRID_W).astype(F32)[:, None] * inv], axis=-1)
    cos = jnp.concatenate([jnp.tile(jnp.cos(ang), (1, 4)), jnp.ones((R - T, 128), F32)], axis=0)
    sin = jnp.concatenate([jnp.tile(jnp.sin(ang), (1, 4)), jnp.zeros((R - T, 128), F32)], axis=0)
    return cos, sin


def kernel(x, c, ctx, c_ctx, w_mod, b_mod, norm_ffn1, w_ffn1_in, w_ffn1_out, norm_mix, w_in, w_pool, pool_scale, sink, w_out, norm_ffn2, w_ffn2_in, w_ffn2_out, norm_final, loss_target, m_c_ctx, m_w_mod, m_b_mod, m_norm_ffn1, m_w_ffn1_in, m_w_ffn1_out, m_norm_mix, m_w_in, m_w_pool, m_pool_scale, m_sink, m_w_out, m_norm_ffn2, m_w_ffn2_in, m_w_ffn2_out, m_norm_final, v_c_ctx, v_w_mod, v_b_mod, v_norm_ffn1, v_w_ffn1_in, v_w_ffn1_out, v_norm_mix, v_w_in, v_w_pool, v_pool_scale, v_sink, v_w_out, v_norm_ffn2, v_w_ffn2_in, v_w_ffn2_out, v_norm_final):
    T = x.shape[1]
    R = T + LC
    nl = w_mod.shape[0]
    cx, cy, cc = _coords()
    me = _lin((cx, cy, cc))
    pos = jnp.stack([cc, 2 * cx + cy]).astype(jnp.int32)
    mcols = w_mod.shape[2]

    shards = [([w_ffn1_in[l].T.astype(BF16), w_ffn1_out[l].astype(BF16)],
               [w_in[l].T.astype(BF16), w_out[l].astype(BF16)],
               [w_ffn2_in[l].T.astype(BF16), w_ffn2_out[l].astype(BF16)]) for l in range(nl)]

    got = _exchange("ag_c_w", _merge(_gather_direct([c]), _gather_a(shards[0][0] + shards[0][1])))
    c_all, w_first = got[0], got[1:]
    c16 = jnp.concatenate([c_all.reshape(NDEV, D), c_ctx[None], jnp.zeros((16 - NDEV - 1, D), F32)], axis=0)
    b_cols = lax.dynamic_slice(b_mod, (0, me * mcols), (nl, mcols)).reshape(nl, 1, mcols)
    got = _exchange("ag_mod_w", _merge(_gather_b(w_first), _gather_direct([_mod_fwd(c16, w_mod, b_cols, name="mod_fwd")])))
    w_first, mod_all = got[:4], got[4]
    mod_all = jnp.transpose(mod_all, (1, 2, 0, 3)).reshape(nl, 16, NMOD, D)
    mine = lax.dynamic_index_in_dim(mod_all, me, axis=1, keepdims=False)
    pad = jnp.zeros((nl, 16 - NMOD, D), F32)
    modv = jnp.stack([jnp.concatenate([mine, pad], axis=1), jnp.concatenate([mod_all[:, 8], pad], axis=1)], axis=1)

    gvec = [jnp.concatenate([norm_ffn1[l][None], norm_mix[l][None], norm_ffn2[l][None], jnp.zeros((5, D), F32)], axis=0)
            for l in range(nl)]
    cos, sin = _rope_tables(T, R)
    ps2 = [pool_scale[l][None] for l in range(nl)]

    h = jnp.concatenate([x[0], ctx[0]], axis=0)
    loss_part, dh, small, nf_all, big = _forward_backward(
        h, loss_target[0], modv, gvec, shards, w_first, cos, sin, sink, w_pool, ps2, norm_final, pos, T=T)
    loss = lax.psum(loss_part[0, 0], ("x", "y", "c"))
    grad_x = dh[:T][None]

    dm, g_b_mod, g_norms, g_nf, g_wp, g_ps, g_sk = _small_sums(
        [small[l][0:3] for l in range(nl)], nf_all, *[[small[l][k] for l in range(nl)] for k in (3, 4, 5)],
        name="small_sums")
    dm_cols = lax.dynamic_slice(dm, (0, 0, me * mcols), (nl, 16, mcols))
    g_w_mod, dc_part = _mod_bwd(c16, dm_cols, w_mod, name="mod_bwd")
    (dc_all,) = _exchange("ag_dc", _gather_direct([dc_part]))

    grads = {
        "b_mod": g_b_mod, "norm_ffn1": g_norms[:, 0], "norm_mix": g_norms[:, 1], "norm_ffn2": g_norms[:, 2],
        "w_pool": g_wp.reshape(w_pool.shape), "pool_scale": g_ps, "sink": g_sk[:, :, 0], "norm_final": g_nf.reshape(D),
        "w_mod": g_w_mod,
        "w_ffn1_in": jnp.stack([big[l][0] for l in range(nl)]), "w_ffn1_out": jnp.stack([big[l][1] for l in range(nl)]),
        "w_in": jnp.stack([big[l][2] for l in range(nl)]), "w_out": jnp.stack([big[l][3] for l in range(nl)]),
        "w_ffn2_in": jnp.stack([big[l][4] for l in range(nl)]), "w_ffn2_out": jnp.stack([big[l][5] for l in range(nl)]),
    }
    weights = dict(c_ctx=c_ctx, w_mod=w_mod, b_mod=b_mod, norm_ffn1=norm_ffn1, w_ffn1_in=w_ffn1_in, w_ffn1_out=w_ffn1_out,
                   norm_mix=norm_mix, w_in=w_in, w_pool=w_pool, pool_scale=pool_scale, sink=sink, w_out=w_out,
                   norm_ffn2=norm_ffn2, w_ffn2_in=w_ffn2_in, w_ffn2_out=w_ffn2_out, norm_final=norm_final)
    moms = dict(c_ctx=(m_c_ctx, v_c_ctx), w_mod=(m_w_mod, v_w_mod), b_mod=(m_b_mod, v_b_mod),
                norm_ffn1=(m_norm_ffn1, v_norm_ffn1), w_ffn1_in=(m_w_ffn1_in, v_w_ffn1_in),
                w_ffn1_out=(m_w_ffn1_out, v_w_ffn1_out), norm_mix=(m_norm_mix, v_norm_mix), w_in=(m_w_in, v_w_in),
                w_pool=(m_w_pool, v_w_pool), pool_scale=(m_pool_scale, v_pool_scale), sink=(m_sink, v_sink),
                w_out=(m_w_out, v_w_out), norm_ffn2=(m_norm_ffn2, v_norm_ffn2), w_ffn2_in=(m_w_ffn2_in, v_w_ffn2_in),
                w_ffn2_out=(m_w_ffn2_out, v_w_ffn2_out), norm_final=(m_norm_final, v_norm_final))
    order = list(weights)
    small_names = ["c_ctx", "b_mod", "norm_ffn1", "norm_mix", "w_pool", "pool_scale", "sink", "norm_ffn2", "norm_final"]

    def as2d(name, t):
        if name == "w_pool":
            return t.reshape(-1, 128)
        return t.reshape(1, -1) if t.ndim == 1 else t

    triples = [(as2d(n, weights[n]), None if n == "c_ctx" else as2d(n, grads[n]), as2d(n, moms[n][0]), as2d(n, moms[n][1]))
               for n in small_names]
    outs = _small_adamw(as2d("c_ctx", c_ctx), dc_all, triples, name="small_adamw")
    grads["c_ctx"] = outs[0].reshape(c_ctx.shape)
    delta, new_m, new_v = {}, {}, {}
    for k, n in enumerate(small_names):
        delta[n], new_m[n], new_v[n] = (o.reshape(weights[n].shape) for o in outs[1 + 3 * k : 4 + 3 * k])
    for n in order:
        if n not in small_names:
            delta[n], new_m[n], new_v[n] = _adamw(weights[n], grads[n], moms[n][0], moms[n][1], name=f"adamw_{n}")

    return (loss, grad_x, *[grads[n] for n in order], *[delta[n] for n in order],
            *[new_m[n] for n in order], *[new_v[n] for n in order])


def _merge(*rounds):
    ins, outs, plan, local, n_alias = [], [], [], [], 0
    for r in rounds:
        assert r.n_alias == 0 or (not ins and r.n_alias == len(r.ins) == len(r.out_shapes))
        oi, oo = len(ins), len(outs)
        plan += [(k, i + oi, sf, o + oo, df) for k, i, sf, o, df in r.plan]
        local += [(i + oi, sf, o + oo, df) for i, sf, o, df in r.local_plan]
        ins += r.ins
        outs += r.out_shapes
        n_alias += r.n_alias
    return _Round(ins, outs, plan, local, n_alias)


def _forward_backward(h, target, modv, gvec, shards, w_first, cos, sin, sink, w_pool, ps2, norm_final, pos, *, T):
    nl = len(gvec)
    flat = lambda ws: [w.reshape(-1, D) for w in ws]
    saved = []
    w1, wm = flat(w_first[:2]), flat(w_first[2:])
    for l in range(nl):
        last = l == nl - 1
        h0 = h
        if l == 0:
            (h1, a1, b1, f1), got = _ffn_fwd(h0, modv[l], gvec[l], *w1, T=T, mrow=0, grow=0, ctx_active=True,
                                             name=f"ffn1_fwd_{l}", carry=_gather_a(shards[l][2]))
            (u, q, k4, v4), got = _mixproj_fwd(h1, modv[l], gvec[l], wm[0], cos, sin, T=T, name=f"mixproj_fwd_{l}",
                                               carry=_gather_b(got))
            w2 = flat(got)
        else:
            (h1, a1, b1, f1), got = _ffn_fwd(h0, modv[l], gvec[l], *w1, T=T, mrow=0, grow=0, ctx_active=True,
                                             name=f"ffn1_fwd_{l}", carry=_gather_b(nxt_m + nxt_2))
            wm, w2 = flat(got[:2]), flat(got[2:])
            (u, q, k4, v4), _ = _mixproj_fwd(h1, modv[l], gvec[l], wm[0], cos, sin, T=T, name=f"mixproj_fwd_{l}")
        (cat,), nxt_1 = _attnpool_fwd(u, q, k4, v4, sink[l], w_pool[l], ps2[l], T=T, name=f"attnpool_fwd_{l}",
                                      carry=None if last else _gather_a(shards[l + 1][0]))
        (h2, mo), nxt_m = _mixout_fwd(h1, cat, modv[l], wm[1], T=T, ctx_active=not last, name=f"mixout_fwd_{l}",
                                      carry=None if last else _gather_a(shards[l + 1][1]))
        (h3, a2, b2, f2), got = _ffn_fwd(h2, modv[l], gvec[l], *w2, T=T, mrow=6, grow=2, ctx_active=not last,
                                         name=f"ffn2_fwd_{l}",
                                         carry=None if last else _merge(_gather_b(nxt_1), _gather_a(shards[l + 1][2])))
        saved.append((h0, a1, b1, f1, h1, u, q, k4, v4, cat, mo, h2, a2, b2, f2, w1, wm, w2))
        h = h3
        if not last:
            w1, nxt_2 = flat(got[:2]), got[2:]

    dh, loss_part, dnf = _loss_head(h, target, norm_final[None], T=T, name="loss_head")

    def adds(tag, grads, got):
        return [_add_pairs(g, r, pos, name=f"rs_add_{tag}_{i}") for i, (g, r) in enumerate(zip(grads, got))]

    def totals(tag, chip, got, transposed=(True, False)):
        return [_sum_chips(c_, r, pos, transpose=t, name=f"rs_sum_{tag}_{i}")
                for i, (c_, r, t) in enumerate(zip(chip, got, transposed))]

    small, big = [None] * nl, {}
    prev = None
    for l in reversed(range(nl)):
        last = l == nl - 1
        h0, a1, b1, f1, h1, u, q, k4, v4, cat, mo, h2, a2, b2, f2, w1, wm, w2 = saved[l]
        (dh, dab, s, n, df, pk2), got = _ffn_bwd(
            h2, dh, a2, b2, f2, modv[l], gvec[l], *w2, T=T, mrow=6, grow=2, ctx_active=not last, name=f"ffn2_bwd_{l}",
            carry=_merge(_scatter_1(prev[0]), _gather_a(prev[1])) if prev else None)
        if prev:
            c1, small_a = adds(f"ffn1_{l + 1}", prev[0], got[:2]), got[2:]
        g_w2i, got = _wgrad(dab, n, bk=WG_BK, sh=2 * DFF // NDEV, name=f"wgrad_ffn2_in_{l}",
                            carry=_scatter_2(c1[:1]) if prev else None)
        if prev:
            big[l + 1][0:1] = totals(f"ffn1_in_{l + 1}", c1[:1], got)
        g_w2o, got = _wgrad(s, df, bk=WG_BK, sh=DFF // NDEV, name=f"wgrad_ffn2_out_{l}",
                            carry=_scatter_2(c1[1:]) if prev else None)
        if prev:
            big[l + 1][1:2] = totals(f"ffn1_out_{l + 1}", c1[1:], got, transposed=(False,))
        rnd = _scatter_1([g_w2i, g_w2o])
        (dcat, dmix, pko), got = _mixout_bwd(dh, mo, modv[l], wm[1], T=T, ctx_active=not last, name=f"mixout_bwd_{l}",
                                             carry=_merge(_gather_b(small_a), rnd) if prev else rnd)
        if prev:
            small[l + 1], got = got[: len(small_a)], got[len(small_a) :]
        c2 = adds(f"ffn2_{l}", [g_w2i, g_w2o], got)
        g_wo, _ = _wgrad(cat, dmix, bk=D, sh=D // NDEV, name=f"wgrad_out_{l}")
        dps, dwp, dsc = _pool_bwd(u, dcat, w_pool[l], ps2[l], T=T, name=f"pool_bwd_{l}")
        (du, dq, dk, dv, dsk), got = _attn_bwd(q, k4, v4, dcat, dps, sink[l], T=T, name=f"attn_bwd_{l}", carry=_scatter_2(c2))
        big[l] = [None, None, None, None] + totals(f"ffn2_{l}", c2, got)
        dh, dproj, n, pkm = _mixproj_bwd(h1, dh, du, dq, dk, dv, modv[l], gvec[l], wm[0], cos, sin, T=T, name=f"mixproj_bwd_{l}")
        g_wi, _ = _wgrad(dproj, n, bk=PROJ, sh=PROJ // NDEV, name=f"wgrad_in_{l}")
        (dh, dab, s, n, df, pk1), got = _ffn_bwd(h0, dh, a1, b1, f1, modv[l], gvec[l], *w1, T=T, mrow=0, grow=0,
                                                 ctx_active=True, name=f"ffn1_bwd_{l}", carry=_scatter_1([g_wi, g_wo]))
        cm = adds(f"mix_{l}", [g_wi, g_wo], got)
        g_w1i, got = _wgrad(dab, n, bk=WG_BK, sh=2 * DFF // NDEV, name=f"wgrad_ffn1_in_{l}", carry=_scatter_2(cm))
        big[l][2:4] = totals(f"mix_{l}", cm, got)
        g_w1o, _ = _wgrad(s, df, bk=WG_BK, sh=DFF // NDEV, name=f"wgrad_ffn1_out_{l}")
        prev = ([g_w1i, g_w1o], [pk1, pkm + pko, pk2, dwp, dsc, dsk])
    got = _exchange("rs1_tail", _merge(_scatter_1(prev[0]), _gather_a(prev[1] + [dnf])))
    c1, small_a = adds("ffn1_0", prev[0], got[:2]), got[2:]
    got = _exchange("rs2_tail", _merge(_gather_b(small_a), _scatter_2(c1)))
    small[0], nf_all = got[: len(small_a) - 1], got[len(small_a) - 1]
    big[0][0:2] = totals("ffn1_0", c1, got[len(small_a) :])
    return loss_part, dh, small, nf_all, big
```

```python
import functools

import jax
import jax.numpy as jnp
from jax import lax
from jax.experimental import pallas as pl
from jax.experimental.pallas import tpu as pltpu

F32, BF16 = jnp.float32, jnp.bfloat16

D = 1024
LC = 256
DFF = 2816
NMOD = 9
PW = 512
AW = 512
KVW = 128
PROJ = PW + AW + 2 * KVW
HD = 64
BLK = 128
GRID_W = 64
POOL_WINDOWS = (2, 4, 8, 16)
EPS = 1e-6
NEG = -1e30
ROPE_BASE = 10000.0
NDEV = 8
MESH = pl.DeviceIdType.MESH

ADAM_LR, ADAM_B1, ADAM_B2, ADAM_EPS, ADAM_WD, ADAM_STEP = 0.001, 0.9, 0.999, 1e-08, 0.01, 10

VMEM_LIMIT = 56 * 1024 * 1024
TM = 256
FFN_CHUNKS = ((0, 1536), (1536, 1280))
WG_BK = 1408

ANY = pl.BlockSpec(memory_space=pl.ANY)
SMEM = pl.BlockSpec(memory_space=pltpu.SMEM)


def _params(ngrid=1):
    return pltpu.CompilerParams(dimension_semantics=("arbitrary",) * ngrid, vmem_limit_bytes=VMEM_LIMIT)


def _dot(a, b):
    return jnp.dot(a, b, preferred_element_type=F32)


def _dot_nt(a, b):
    return lax.dot_general(a, b, (((1,), (1,)), ((), ())), preferred_element_type=F32)


def _dot_tn(a, b):
    return lax.dot_general(a, b, (((0,), (0,)), ((), ())), preferred_element_type=F32)


def _sigmoid(x):
    return 1.0 / (1.0 + jnp.exp(-x))


def _rows(tm, w):
    return pl.BlockSpec((tm, w), lambda i: (i, 0))


def _full(shape):
    nd = len(shape)
    return pl.BlockSpec(shape, lambda *_: (0,) * nd)


def _sds(shape, dtype):
    return jax.ShapeDtypeStruct(shape, dtype)


def _norm_mod(h, g, shift, scale):
    r = lax.rsqrt(jnp.mean(h * h, axis=-1, keepdims=True) + EPS)
    xhat = h * r
    y = xhat * g
    return r, xhat, y, y * (1.0 + scale) + shift


def _norm_mod_bwd(dn, r, xhat, y, g, scale):
    dshift = jnp.sum(dn, axis=0, keepdims=True)
    dscale = jnp.sum(dn * y, axis=0, keepdims=True)
    dy = dn * (1.0 + scale)
    dg = jnp.sum(dy * xhat, axis=0, keepdims=True)
    dxh = dy * g
    dh = r * (dxh - xhat * jnp.mean(dxh * xhat, axis=-1, keepdims=True))
    return dh, dshift, dscale, dg


def _acc_partials(part_ref, first, rows):
    @pl.when(first)
    def _():
        part_ref[...] = jnp.zeros_like(part_ref)

    for r, val in rows.items():
        part_ref[0, r : r + 1, :] += val


def _mod_spec(n_lat):
    return pl.BlockSpec((1, 16, D), lambda i: (i // n_lat, 0, 0))


def _part_spec(n_lat):
    return pl.BlockSpec((1, 8, D), lambda i: (i // n_lat, 0, 0))


def _load_weights(pairs, sem):
    copies = [pltpu.make_async_copy(src, dst, sem.at[k]) for k, (src, dst) in enumerate(pairs)]
    for cp in copies:
        cp.start()
    for cp in copies:
        cp.wait()


def _ffn_weight_copies(win_hbm, wout_hbm, win_v, wout_v, sem):
    loads = []
    for k, (c0, cw) in enumerate(FFN_CHUNKS):
        slabs = [(win_hbm, win_v, c0), (win_hbm, win_v, DFF + c0), (wout_hbm, wout_v, c0)]
        loads.append([pltpu.make_async_copy(src.at[pl.ds(r0, cw)], dst.at[pl.ds(r0, cw)], sem.at[3 * k + j])
                      for j, (src, dst, r0) in enumerate(slabs)])
    return loads


def _ffn_steps(i, n_active, loads, compute):
    @pl.when(i == 0)
    def _():
        for cp in sum(loads, []):
            cp.start()
        compute(loads)

    @pl.when(jnp.logical_and(i > 0, i < n_active))
    def _():
        compute(None)


def _wait_chunk(loads, k):
    if loads is not None:
        for cp in loads[k]:
            cp.wait()


def _ffn_fwd(h, modv, gvec, win, wout, *, T, mrow, grow, ctx_active, name, carry=None):
    R = h.shape[0]
    n_lat, n_tiles = T // TM, R // TM
    n_active = n_tiles if ctx_active else n_lat

    def body(h_ref, mod_ref, g_ref, win_hbm, wout_hbm, ho_ref, a_ref, b_ref, f_ref, win_v, wout_v, sem):
        i = pl.program_id(0)

        def compute(loads):
            h = h_ref[...]
            shift, scale, gate = (mod_ref[0, mrow + k : mrow + k + 1, :] for k in range(3))
            _, _, _, n = _norm_mod(h, g_ref[grow : grow + 1, :], shift, scale)
            n_bf = n.astype(BF16)
            acc = jnp.zeros((TM, D), F32)
            for k, (c0, cw) in enumerate(FFN_CHUNKS):
                _wait_chunk(loads, k)
                a = _dot_nt(n_bf, win_v[c0 : c0 + cw, :])
                b = _dot_nt(n_bf, win_v[DFF + c0 : DFF + c0 + cw, :])
                a_ref[:, c0 : c0 + cw] = a.astype(BF16)
                b_ref[:, c0 : c0 + cw] = b.astype(BF16)
                s = a * _sigmoid(a) * b
                acc = acc + _dot(s.astype(BF16), wout_v[c0 : c0 + cw, :])
            f_ref[...] = acc.astype(BF16)
            ho_ref[...] = h + (0.5 * gate) * acc

        _ffn_steps(i, n_active, _ffn_weight_copies(win_hbm, wout_hbm, win_v, wout_v, sem), compute)

        @pl.when(i >= n_active)
        def _():
            ho_ref[...] = h_ref[...]
            a_ref[...] = jnp.zeros_like(a_ref)
            b_ref[...] = jnp.zeros_like(b_ref)
            f_ref[...] = jnp.zeros_like(f_ref)

    return _call(
        body,
        name=name,
        grid=(n_tiles,),
        in_specs=[_rows(TM, D), _mod_spec(n_lat), _full((8, D)), ANY, ANY],
        out_specs=[_rows(TM, D), _rows(TM, DFF), _rows(TM, DFF), _rows(TM, D)],
        out_shape=[_sds((R, D), F32), _sds((R, DFF), BF16), _sds((R, DFF), BF16), _sds((R, D), BF16)],
        scratch_shapes=[pltpu.VMEM((2 * DFF, D), BF16), pltpu.VMEM((DFF, D), BF16),
                        pltpu.SemaphoreType.DMA((3 * len(FFN_CHUNKS),))],
        args=(h, modv, gvec, win, wout),
        carry=carry,
    )


def _ffn_bwd(h, dho, a, b, f, modv, gvec, win, wout, *, T, mrow, grow, ctx_active, name, carry=None):
    R = h.shape[0]
    n_lat, n_tiles = T // TM, R // TM
    n_active = n_tiles if ctx_active else n_lat

    def body(h_ref, dho_ref, a_ref, b_ref, f_ref, mod_ref, g_ref, win_hbm, wout_hbm,
             dh_ref, dab_ref, s_ref, n_ref, df_ref, part_ref, win_v, wout_v, sem):
        i = pl.program_id(0)
        first = jnp.logical_or(i == 0, i == n_lat)

        def compute(loads):
            h = h_ref[...]
            dho = dho_ref[...]
            shift, scale, gate = (mod_ref[0, mrow + k : mrow + k + 1, :] for k in range(3))
            g = g_ref[grow : grow + 1, :]
            r, xhat, y, n = _norm_mod(h, g, shift, scale)
            dgate = 0.5 * jnp.sum(dho * f_ref[...].astype(F32), axis=0, keepdims=True)
            df_bf = ((0.5 * gate) * dho).astype(BF16)
            df_ref[...] = df_bf
            n_ref[...] = n.astype(BF16)
            dn = jnp.zeros((TM, D), F32)
            for k, (c0, cw) in enumerate(FFN_CHUNKS):
                _wait_chunk(loads, k)
                ds = _dot_nt(df_bf, wout_v[c0 : c0 + cw, :])
                av = a_ref[:, c0 : c0 + cw].astype(F32)
                bv = b_ref[:, c0 : c0 + cw].astype(F32)
                sig = _sigmoid(av)
                sa = av * sig
                s_ref[:, c0 : c0 + cw] = (sa * bv).astype(BF16)
                da = (ds * bv * (sig * (1.0 + av * (1.0 - sig)))).astype(BF16)
                db = (ds * sa).astype(BF16)
                dab_ref[:, c0 : c0 + cw] = da
                dab_ref[:, DFF + c0 : DFF + c0 + cw] = db
                dn = dn + _dot(da, win_v[c0 : c0 + cw, :]) + _dot(db, win_v[DFF + c0 : DFF + c0 + cw, :])
            dh, dshift, dscale, dg = _norm_mod_bwd(dn, r, xhat, y, g, scale)
            dh_ref[...] = dho + dh
            _acc_partials(part_ref, first, {0: dshift, 1: dscale, 2: dgate, 3: dg})

        _ffn_steps(i, n_active, _ffn_weight_copies(win_hbm, wout_hbm, win_v, wout_v, sem), compute)

        @pl.when(i >= n_active)
        def _():
            dh_ref[...] = dho_ref[...]
            dab_ref[...] = jnp.zeros_like(dab_ref)
            s_ref[...] = jnp.zeros_like(s_ref)
            n_ref[...] = jnp.zeros_like(n_ref)
            df_ref[...] = jnp.zeros_like(df_ref)
            part_ref[...] = jnp.zeros_like(part_ref)

    return _call(
        body,
        name=name,
        grid=(n_tiles,),
        in_specs=[_rows(TM, D), _rows(TM, D), _rows(TM, DFF), _rows(TM, DFF), _rows(TM, D),
                  _mod_spec(n_lat), _full((8, D)), ANY, ANY],
        out_specs=[_rows(TM, D), _rows(TM, 2 * DFF), _rows(TM, DFF), _rows(TM, D), _rows(TM, D), _part_spec(n_lat)],
        out_shape=[_sds((R, D), F32), _sds((R, 2 * DFF), BF16), _sds((R, DFF), BF16), _sds((R, D), BF16),
                   _sds((R, D), BF16), _sds((2, 8, D), F32)],
        scratch_shapes=[pltpu.VMEM((2 * DFF, D), BF16), pltpu.VMEM((DFF, D), BF16),
                        pltpu.SemaphoreType.DMA((3 * len(FFN_CHUNKS),))],
        args=(h, dho, a, b, f, modv, gvec, win, wout),
        carry=carry,
    )


def _wgrad(x, y, *, bk, sh, name, carry=None):
    R, kx = x.shape
    n = y.shape[1]
    tr = R // 2
    nr, nsh = R // tr, bk // sh

    def body(x_ref, y_ref, o_ref, acc):
        r = pl.program_id(1)

        @pl.when(r == 0)
        def _():
            acc[...] = jnp.zeros_like(acc)

        acc[...] += _dot_tn(x_ref[...], y_ref[...])

        @pl.when(r == nr - 1)
        def _():
            for s in range(nsh):
                o_ref[s] = acc[s * sh : (s + 1) * sh, :].astype(BF16)

    (out,), got = _call(
        body,
        name=name,
        grid=(kx // bk, nr),
        in_specs=[pl.BlockSpec((tr, bk), lambda k, r: (r, k)), pl.BlockSpec((tr, n), lambda k, r: (r, 0))],
        out_specs=[pl.BlockSpec((nsh, sh, n), lambda k, r: (k, 0, 0))],
        out_shape=[_sds((kx // sh, sh, n), BF16)],
        scratch_shapes=[pltpu.VMEM((bk, n), F32)],
        args=(x, y),
        carry=carry,
    )
    return out, got


def _rot_half(x):
    lane = lax.broadcasted_iota(jnp.int32, x.shape, 1)
    return jnp.where((lane & (HD - 1)) < HD // 2, -pltpu.roll(x, 128 - HD // 2, 1), pltpu.roll(x, HD // 2, 1))


def _tile_sel():
    i = lax.broadcasted_iota(jnp.int32, (KVW, AW), 0)
    j = lax.broadcasted_iota(jnp.int32, (KVW, AW), 1)
    return jnp.where(i == (j // 256) * HD + (j & (HD - 1)), 1.0, 0.0).astype(BF16)


def _mixproj_fwd(h, modv, gvec, win, cos, sin, *, T, name, carry=None):
    R = h.shape[0]
    n_lat, n_tiles = T // TM, R // TM

    def body(h_ref, mod_ref, g_ref, win_ref, cos_ref, sin_ref, u_ref, q_ref, k4_ref, v4_ref):
        shift, scale = mod_ref[0, 3:4, :], mod_ref[0, 4:5, :]
        _, _, _, n = _norm_mod(h_ref[...], g_ref[1:2, :], shift, scale)
        proj = _dot_nt(n.astype(BF16), win_ref[...])
        u_ref[...] = proj[:, :PW]
        cs, sn = cos_ref[...], sin_ref[...]
        for s in range(AW // 128):
            x = proj[:, PW + 128 * s : PW + 128 * (s + 1)]
            q_ref[:, 128 * s : 128 * (s + 1)] = ((x * cs + _rot_half(x) * sn) * (HD ** -0.5)).astype(BF16)
        k = proj[:, PW + AW : PW + AW + KVW]
        k = (k * cs + _rot_half(k) * sn).astype(BF16)
        v = proj[:, PW + AW + KVW :].astype(BF16)
        sel = _tile_sel()
        k4_ref[...] = _dot(k, sel).astype(BF16)
        v4_ref[...] = _dot(v, sel).astype(BF16)

    return _call(
        body,
        name=name,
        grid=(n_tiles,),
        in_specs=[_rows(TM, D), _mod_spec(n_lat), _full((8, D)), _full((PROJ, D)), _rows(TM, 128), _rows(TM, 128)],
        out_specs=[_rows(TM, PW), _rows(TM, AW), _rows(TM, AW), _rows(TM, AW)],
        out_shape=[_sds((R, PW), F32), _sds((R, AW), BF16), _sds((R, AW), BF16), _sds((R, AW), BF16)],
        scratch_shapes=[],
        args=(h, modv, gvec, win, cos, sin),
        carry=carry,
    )


def _win_start(j, hi):
    return pl.multiple_of(jnp.clip((j - 1) * BLK, 0, hi - 3 * BLK), BLK)


def _hi_lo(x):
    hi = x.astype(BF16)
    return hi, (x - hi.astype(F32)).astype(BF16)


def _pool_bounds(t, w, T, R):
    is_ctx = t >= T
    lo = jnp.maximum(t - w // 2, jnp.where(is_ctx, T, 0))
    hi = jnp.minimum(t + w // 2, jnp.where(is_ctx, R, T))
    return lo, hi


def _pooled(u_v, j, T, R):
    start = _win_start(j, R)
    u3_hi, u3_lo = _hi_lo(u_v[pl.ds(start, 3 * BLK), :])
    ub = u_v[pl.ds(pl.multiple_of(j * BLK, BLK), BLK), :]
    t = j * BLK + lax.broadcasted_iota(jnp.int32, (BLK, 1), 0)
    pos = start + lax.broadcasted_iota(jnp.int32, (1, 3 * BLK), 1)
    pooled, counts = [], []
    for g, w in enumerate(POOL_WINDOWS):
        lo, hi = _pool_bounds(t, w, T, R)
        band = jnp.where(pos >= lo, jnp.where(pos < hi, 1.0, 0.0), 0.0).astype(BF16)
        sl = slice(g * 128, (g + 1) * 128)
        sums = _dot(band, u3_hi[:, sl]) + _dot(band, u3_lo[:, sl])
        cnt = (hi - lo).astype(F32)
        pooled.append(sums / cnt - ub[:, sl])
        counts.append(cnt)
    return pooled, counts


def _stack_heads(x):
    lane_h = lax.broadcasted_iota(jnp.int32, x.shape, 1) // HD
    return jnp.concatenate([jnp.where(lane_h == h, x, jnp.zeros_like(x)) for h in range(4)], axis=0)


def _unstack_heads(x):
    lane_h = lax.broadcasted_iota(jnp.int32, (BLK, 256), 1) // HD
    out = jnp.zeros((BLK, 256), F32)
    for h in range(4):
        out = out + jnp.where(lane_h == h, x[h * BLK : (h + 1) * BLK, :], 0.0)
    return out


def _window_mask(j, start_l, nbl):
    rowi = lax.broadcasted_iota(jnp.int32, (4 * BLK, 1), 0)
    qpos = j * BLK + (rowi & (BLK - 1))
    kpos = start_l + lax.broadcasted_iota(jnp.int32, (1, 3 * BLK), 1)
    reach = jnp.where(j < nbl, BLK, -1)
    return jnp.abs(kpos - qpos) <= reach


def _attn_exps(qs, kl, kc, sink_ref, g, valid):
    s_l = jnp.where(valid, _dot_nt(qs, kl), NEG)
    s_c = _dot_nt(qs, kc)
    rb = lax.broadcasted_iota(jnp.int32, (4 * BLK, 1), 0) // BLK
    sk = jnp.where(rb == 0, sink_ref[4 * g], jnp.where(rb == 1, sink_ref[4 * g + 1],
                   jnp.where(rb == 2, sink_ref[4 * g + 2], sink_ref[4 * g + 3])))
    m = jnp.maximum(jnp.maximum(jnp.max(s_l, axis=1, keepdims=True), jnp.max(s_c, axis=1, keepdims=True)), sk)
    e_l, e_c, e_s = jnp.exp(s_l - m), jnp.exp(s_c - m), jnp.exp(sk - m)
    inv = 1.0 / (jnp.sum(e_l, axis=1, keepdims=True) + jnp.sum(e_c, axis=1, keepdims=True) + e_s)
    return e_l, e_c, e_s, inv


def _attnpool_fwd(u, q, k4, v4, sink, w_pool, pool_scale, *, T, name, carry=None):
    R = u.shape[0]
    nb, nbl = R // BLK, T // BLK

    def body(q_ref, sink_ref, wp_ref, ps_ref, u_hbm, k4_hbm, v4_hbm, cat_ref, u_v, k4_v, v4_v, sem):
        j = pl.program_id(0)

        @pl.when(j == 0)
        def _():
            _load_weights([(u_hbm, u_v), (k4_hbm, k4_v), (v4_hbm, v4_v)], sem)

        pooled, _ = _pooled(u_v, j, T, R)
        for g in range(4):
            mixed = _dot(pooled[g].astype(BF16), wp_ref[g].astype(BF16)) * ps_ref[:, g * 128 : (g + 1) * 128]
            cat_ref[:, g * 128 : (g + 1) * 128] = mixed.astype(BF16)

        start_l = _win_start(j, T)
        valid = _window_mask(j, start_l, nbl)
        for g in range(2):
            gl = slice(g * 256, (g + 1) * 256)
            qs = _stack_heads(q_ref[:, gl])
            e_l, e_c, _, inv = _attn_exps(qs, k4_v[pl.ds(start_l, 3 * BLK), gl], k4_v[T:R, gl], sink_ref, g, valid)
            o = _dot(e_l.astype(BF16), v4_v[pl.ds(start_l, 3 * BLK), gl]) + _dot(e_c.astype(BF16), v4_v[T:R, gl])
            cat_ref[:, PW + g * 256 : PW + (g + 1) * 256] = _unstack_heads(o * inv).astype(BF16)

    return _call(
        body,
        name=name,
        grid=(nb,),
        in_specs=[_rows(BLK, AW), SMEM, _full((4, 128, 128)), _full((1, PW)), ANY, ANY, ANY],
        out_specs=[_rows(BLK, D)],
        out_shape=[_sds((R, D), BF16)],
        scratch_shapes=[pltpu.VMEM((R, PW), F32), pltpu.VMEM((R, AW), BF16), pltpu.VMEM((R, AW), BF16),
                        pltpu.SemaphoreType.DMA((3,))],
        args=(q, sink, w_pool, pool_scale, u, k4, v4),
        carry=carry,
    )


def _mixout_fwd(h, cat, modv, wout, *, T, ctx_active, name, carry=None):
    R = h.shape[0]
    n_lat, n_tiles = T // TM, R // TM

    def body(h_ref, cat_ref, mod_ref, w_ref, ho_ref, mo_ref):
        i = pl.program_id(0)

        def compute():
            mo = _dot(cat_ref[...], w_ref[...])
            mo_ref[...] = mo.astype(BF16)
            ho_ref[...] = h_ref[...] + mod_ref[0, 5:6, :] * mo

        if ctx_active:
            compute()
        else:
            pl.when(i < n_lat)(compute)

            @pl.when(i >= n_lat)
            def _():
                ho_ref[...] = h_ref[...]
                mo_ref[...] = jnp.zeros_like(mo_ref)

    return _call(
        body,
        name=name,
        grid=(n_tiles,),
        in_specs=[_rows(TM, D), _rows(TM, D), _mod_spec(n_lat), _full((D, D))],
        out_specs=[_rows(TM, D), _rows(TM, D)],
        out_shape=[_sds((R, D), F32), _sds((R, D), BF16)],
        scratch_shapes=[],
        args=(h, cat, modv, wout),
        carry=carry,
    )


def _mixout_bwd(dho, mo, modv, wout, *, T, ctx_active, name, carry=None):
    R = dho.shape[0]
    n_lat, n_tiles = T // TM, R // TM

    def body(dho_ref, mo_ref, mod_ref, w_ref, dcat_ref, dmix_ref, part_ref):
        i = pl.program_id(0)
        first = jnp.logical_or(i == 0, i == n_lat)

        def compute():
            dho = dho_ref[...]
            dmix = (mod_ref[0, 5:6, :] * dho).astype(BF16)
            dmix_ref[...] = dmix
            dcat_ref[...] = _dot_nt(dmix, w_ref[...])
            dgate = jnp.sum(dho * mo_ref[...].astype(F32), axis=0, keepdims=True)
            _acc_partials(part_ref, first, {2: dgate})

        if ctx_active:
            compute()
        else:
            pl.when(i < n_lat)(compute)

            @pl.when(i >= n_lat)
            def _():
                dcat_ref[...] = jnp.zeros_like(dcat_ref)
                dmix_ref[...] = jnp.zeros_like(dmix_ref)
                part_ref[...] = jnp.zeros_like(part_ref)

    return _call(
        body,
        name=name,
        grid=(n_tiles,),
        in_specs=[_rows(TM, D), _rows(TM, D), _mod_spec(n_lat), _full((D, D))],
        out_specs=[_rows(TM, D), _rows(TM, D), _part_spec(n_lat)],
        out_shape=[_sds((R, D), F32), _sds((R, D), BF16), _sds((2, 8, D), F32)],
        scratch_shapes=[],
        args=(dho, mo, modv, wout),
        carry=carry,
    )


def _pool_bwd(u, dcat, w_pool, pool_scale, *, T, name):
    R = u.shape[0]
    nb = R // BLK

    def body(dcat_ref, wp_ref, ps_ref, u_hbm, dps_ref, dwp_ref, dsc_ref, u_v, sem):
        j = pl.program_id(0)

        @pl.when(j == 0)
        def _():
            _load_weights([(u_hbm, u_v)], sem)
            dwp_ref[...] = jnp.zeros_like(dwp_ref)
            dsc_ref[...] = jnp.zeros_like(dsc_ref)

        pooled, counts = _pooled(u_v, j, T, R)
        for g in range(4):
            sl = slice(g * 128, (g + 1) * 128)
            p_bf = pooled[g].astype(BF16)
            w_bf = wp_ref[g].astype(BF16)
            dmixed = dcat_ref[:, sl]
            dsc_ref[0:1, sl] += jnp.sum(dmixed * _dot(p_bf, w_bf), axis=0, keepdims=True)
            dmp = (dmixed * ps_ref[:, sl]).astype(BF16)
            dwp_ref[sl, :] += _dot_tn(p_bf, dmp)
            dps_ref[:, sl] = _dot_nt(dmp, w_bf) / counts[g]

    return pl.pallas_call(
        body,
        name=name,
        grid=(nb,),
        in_specs=[_rows(BLK, D), _full((4, 128, 128)), _full((1, PW)), ANY],
        out_specs=[_rows(BLK, PW), _full((PW, 128)), _full((8, PW))],
        out_shape=[_sds((R, PW), F32), _sds((PW, 128), F32), _sds((8, PW), F32)],
        scratch_shapes=[pltpu.VMEM((R, PW), F32), pltpu.SemaphoreType.DMA((1,))],
        compiler_params=_params(),
    )(dcat, w_pool, pool_scale, u)


def _fold_heads(x):
    y = x[:, :128] + x[:, 128:]
    return y + pltpu.roll(y, HD, 1)


def _attn_bwd(q, k4, v4, dcat, dps, sink, *, T, name, carry=None):
    R = q.shape[0]
    nb, nbl = R // BLK, T // BLK

    def body(q_ref, dcat_ref, sink_ref, k4_hbm, v4_hbm, dps_hbm, du_ref, dq_ref, dk_ref, dv_ref, dsk_ref,
             k4_v, v4_v, dps_v, sem):
        j = pl.program_id(0)

        @pl.when(j == 0)
        def _():
            _load_weights([(k4_hbm, k4_v), (v4_hbm, v4_v), (dps_hbm, dps_v)], sem)
            dk_ref[...] = jnp.zeros_like(dk_ref)
            dv_ref[...] = jnp.zeros_like(dv_ref)
            dsk_ref[...] = jnp.zeros_like(dsk_ref)

        start = _win_start(j, R)
        d3_hi, d3_lo = _hi_lo(dps_v[pl.ds(start, 3 * BLK), :])
        db = dps_v[pl.ds(pl.multiple_of(j * BLK, BLK), BLK), :]
        pos = j * BLK + lax.broadcasted_iota(jnp.int32, (BLK, 1), 0)
        t_r = start + lax.broadcasted_iota(jnp.int32, (1, 3 * BLK), 1)
        for g, w in enumerate(POOL_WINDOWS):
            sl = slice(g * 128, (g + 1) * 128)
            lo_r, hi_r = _pool_bounds(t_r, w, T, R)
            band_t = jnp.where(pos >= lo_r, jnp.where(pos < hi_r, 1.0, 0.0), 0.0).astype(BF16)
            lo_c, hi_c = _pool_bounds(pos, w, T, R)
            du_ref[:, sl] = _dot(band_t, d3_hi[:, sl]) + _dot(band_t, d3_lo[:, sl]) - db[:, sl] * (hi_c - lo_c).astype(F32)

        start_l = _win_start(j, T)
        valid = _window_mask(j, start_l, nbl)
        rb = lax.broadcasted_iota(jnp.int32, (4 * BLK, 1), 0) // BLK
        lane = lax.broadcasted_iota(jnp.int32, (1, 128), 1)
        dk_l, dk_c, dv_l, dv_c = [], [], [], []
        for g in range(2):
            gl = slice(g * 256, (g + 1) * 256)
            qs = _stack_heads(q_ref[:, gl])
            kl, kc = k4_v[pl.ds(start_l, 3 * BLK), gl], k4_v[T:R, gl]
            vl, vc = v4_v[pl.ds(start_l, 3 * BLK), gl], v4_v[T:R, gl]
            e_l, e_c, e_s, inv = _attn_exps(qs, kl, kc, sink_ref, g, valid)
            p_l, p_c, p_s = e_l * inv, e_c * inv, e_s * inv
            dos = _stack_heads(dcat_ref[:, PW + g * 256 : PW + (g + 1) * 256]).astype(BF16)
            dp_l, dp_c = _dot_nt(dos, vl), _dot_nt(dos, vc)
            delta = jnp.sum(p_l * dp_l, axis=1, keepdims=True) + jnp.sum(p_c * dp_c, axis=1, keepdims=True)
            ds_l = (p_l * (dp_l - delta)).astype(BF16)
            ds_c = (p_c * (dp_c - delta)).astype(BF16)
            dq_ref[:, gl] = _unstack_heads(_dot(ds_l, kl) + _dot(ds_c, kc)) * (HD ** -0.5)
            dk_l.append(_fold_heads(_dot_tn(ds_l, qs)))
            dk_c.append(_fold_heads(_dot_tn(ds_c, qs)))
            dv_l.append(_fold_heads(_dot_tn(p_l.astype(BF16), dos)))
            dv_c.append(_fold_heads(_dot_tn(p_c.astype(BF16), dos)))
            dsink = -p_s * delta
            for h in range(4):
                tot = jnp.sum(jnp.where(rb == h, dsink, 0.0), axis=0, keepdims=True)
                dsk_ref[4 * g + h : 4 * g + h + 1, :] += jnp.broadcast_to(tot, (1, 128))
        first = lane < HD
        dk_ref[pl.ds(start_l, 3 * BLK), :] += jnp.where(first, dk_l[0], dk_l[1])
        dk_ref[T:R, :] += jnp.where(first, dk_c[0], dk_c[1])
        dv_ref[pl.ds(start_l, 3 * BLK), :] += jnp.where(first, dv_l[0], dv_l[1])
        dv_ref[T:R, :] += jnp.where(first, dv_c[0], dv_c[1])

    return _call(
        body,
        name=name,
        grid=(nb,),
        in_specs=[_rows(BLK, AW), _rows(BLK, D), SMEM, ANY, ANY, ANY],
        out_specs=[_rows(BLK, PW), _rows(BLK, AW), _full((R, KVW)), _full((R, KVW)), _full((8, 128))],
        out_shape=[_sds((R, PW), F32), _sds((R, AW), F32), _sds((R, KVW), F32), _sds((R, KVW), F32),
                   _sds((8, 128), F32)],
        scratch_shapes=[pltpu.VMEM((R, AW), BF16), pltpu.VMEM((R, AW), BF16), pltpu.VMEM((R, PW), F32),
                        pltpu.SemaphoreType.DMA((3,))],
        args=(q, dcat, sink, k4, v4, dps),
        carry=carry,
    )


def _mixproj_bwd(h, dho, du, dq, dk, dv, modv, gvec, win, cos, sin, *, T, name):
    R = h.shape[0]
    n_lat, n_tiles = T // TM, R // TM

    def body(h_ref, dho_ref, du_ref, dq_ref, dk_ref, dv_ref, mod_ref, g_ref, win_ref, cos_ref, sin_ref,
             dh_ref, dproj_ref, n_ref, part_ref):
        i = pl.program_id(0)
        first = jnp.logical_or(i == 0, i == n_lat)
        shift, scale = mod_ref[0, 3:4, :], mod_ref[0, 4:5, :]
        g = g_ref[1:2, :]
        r, xhat, y, n = _norm_mod(h_ref[...], g, shift, scale)
        n_ref[...] = n.astype(BF16)
        cs, sn = cos_ref[...], sin_ref[...]
        dproj_ref[:, :PW] = du_ref[...].astype(BF16)
        for s in range(AW // 128):
            x = dq_ref[:, 128 * s : 128 * (s + 1)]
            dproj_ref[:, PW + 128 * s : PW + 128 * (s + 1)] = (x * cs - _rot_half(x) * sn).astype(BF16)
        x = dk_ref[...]
        dproj_ref[:, PW + AW : PW + AW + KVW] = (x * cs - _rot_half(x) * sn).astype(BF16)
        dproj_ref[:, PW + AW + KVW :] = dv_ref[...].astype(BF16)
        dn = _dot(dproj_ref[...], win_ref[...])
        dh, dshift, dscale, dg = _norm_mod_bwd(dn, r, xhat, y, g, scale)
        dh_ref[...] = dho_ref[...] + dh
        _acc_partials(part_ref, first, {0: dshift, 1: dscale, 3: dg})

    return pl.pallas_call(
        body,
        name=name,
        grid=(n_tiles,),
        in_specs=[_rows(TM, D), _rows(TM, D), _rows(TM, PW), _rows(TM, AW), _rows(TM, KVW), _rows(TM, KVW),
                  _mod_spec(n_lat), _full((8, D)), _full((PROJ, D)), _rows(TM, 128), _rows(TM, 128)],
        out_specs=[_rows(TM, D), _rows(TM, PROJ), _rows(TM, D), _part_spec(n_lat)],
        out_shape=[_sds((R, D), F32), _sds((R, PROJ), BF16), _sds((R, D), BF16), _sds((2, 8, D), F32)],
        compiler_params=_params(),
    )(h, dho, du, dq, dk, dv, modv, gvec, win, cos, sin)


def _loss_head(h, target, g_final, *, T, name):
    R = h.shape[0]
    n_lat, n_tiles = T // TM, R // TM

    def body(h_ref, t_ref, g_ref, dh_ref, loss_ref, dg_ref):
        i = pl.program_id(0)

        @pl.when(i == 0)
        def _():
            loss_ref[...] = jnp.zeros_like(loss_ref)
            dg_ref[...] = jnp.zeros_like(dg_ref)

        @pl.when(i < n_lat)
        def _():
            h = h_ref[...]
            g = g_ref[...]
            r = lax.rsqrt(jnp.mean(h * h, axis=-1, keepdims=True) + EPS)
            xhat = h * r
            err = xhat * g - t_ref[...]
            tot = jnp.sum(jnp.sum(err * err, axis=1, keepdims=True), axis=0, keepdims=True)
            loss_ref[...] += jnp.broadcast_to(tot * (0.5 / D), loss_ref.shape)
            dy = err * (1.0 / D)
            dg_ref[0:1, :] += jnp.sum(dy * xhat, axis=0, keepdims=True)
            dxh = dy * g
            dh_ref[...] = r * (dxh - xhat * jnp.mean(dxh * xhat, axis=-1, keepdims=True))

        @pl.when(i >= n_lat)
        def _():
            dh_ref[...] = jnp.zeros_like(dh_ref)

    return pl.pallas_call(
        body,
        name=name,
        grid=(n_tiles,),
        in_specs=[_rows(TM, D), pl.BlockSpec((TM, D), lambda i: (jnp.minimum(i, n_lat - 1), 0)), _full((1, D))],
        out_specs=[_rows(TM, D), _full((8, 128)), _full((8, D))],
        out_shape=[_sds((R, D), F32), _sds((8, 128), F32), _sds((8, D), F32)],
        compiler_params=_params(),
    )(h, target, g_final)


def _mod_fwd(c16, w_mod, b_cols, *, name):
    nl, _, cols = w_mod.shape

    def body(c_ref, w_ref, b_ref, o_ref):
        c = c_ref[...]
        sc = (c * _sigmoid(c)).astype(BF16)
        o_ref[0] = _dot(sc, w_ref[0].astype(BF16)) + b_ref[0]

    return pl.pallas_call(
        body,
        name=name,
        grid=(nl,),
        in_specs=[_full((16, D)), pl.BlockSpec((1, D, cols), lambda l: (l, 0, 0)),
                  pl.BlockSpec((1, 1, cols), lambda l: (l, 0, 0))],
        out_specs=pl.BlockSpec((1, 16, cols), lambda l: (l, 0, 0)),
        out_shape=_sds((nl, 16, cols), F32),
        compiler_params=_params(),
    )(c16, w_mod, b_cols)


def _mod_bwd(c16, dm_cols, w_mod, *, name):
    nl, _, cols = w_mod.shape

    def body(c_ref, dm_ref, w_ref, gw_ref, dc_ref):
        c = c_ref[...]
        sc = (c * _sigmoid(c)).astype(BF16)
        dm = dm_ref[0].astype(BF16)
        gw_ref[0] = _dot_tn(sc, dm)
        dc_ref[0] = _dot_nt(dm, w_ref[0].astype(BF16))

    return pl.pallas_call(
        body,
        name=name,
        grid=(nl,),
        in_specs=[_full((16, D)), pl.BlockSpec((1, 16, cols), lambda l: (l, 0, 0)),
                  pl.BlockSpec((1, D, cols), lambda l: (l, 0, 0))],
        out_specs=[pl.BlockSpec((1, D, cols), lambda l: (l, 0, 0)), pl.BlockSpec((1, 16, D), lambda l: (l, 0, 0))],
        out_shape=[_sds((nl, D, cols), F32), _sds((nl, 16, D), F32)],
        compiler_params=_params(),
    )(c16, dm_cols, w_mod)


def _coords():
    return lax.axis_index("x"), lax.axis_index("y"), lax.axis_index("c")


def _peer(k, x, y, c):
    return (1 - x if k & 4 else x, 1 - y if k & 2 else y, 1 - c if k & 1 else c)


def _lin(p):
    return 4 * p[0] + 2 * p[1] + p[2]


def _view(ref, slot):
    return ref if slot is None else ref.at[slot]


class _Round:
    def __init__(self, ins, out_shapes, plan, local_plan=(), n_alias=0):
        self.ins, self.out_shapes = list(ins), list(out_shapes)
        self.plan, self.local_plan, self.n_alias = list(plan), list(local_plan), n_alias

    def sems(self):
        return [pltpu.SemaphoreType.DMA((len(self.plan),)), pltpu.SemaphoreType.DMA((len(self.plan),)),
                pltpu.SemaphoreType.DMA((max(len(self.local_plan), 1),))]

    def _copies(self, in_refs, out_refs, sems, incoming, links=("ici", "d2d")):
        in_refs = list(out_refs[: self.n_alias]) + list(in_refs[self.n_alias :])
        send_sems, recv_sems, loc_sems = sems
        x, y, c = _coords()
        me = _lin((x, y, c))
        remote = []
        for idx, (k, ii, sfn, oi, dfn) in enumerate(self.plan):
            if ("d2d" if k == 1 else "ici") not in links:
                continue
            peer = _peer(k, x, y, c)
            sender, receiver = (_lin(peer), me) if incoming else (me, _lin(peer))
            remote.append(pltpu.make_async_remote_copy(
                src_ref=_view(in_refs[ii], sfn(sender, receiver)), dst_ref=_view(out_refs[oi], dfn(sender, receiver)),
                send_sem=send_sems.at[idx], recv_sem=recv_sems.at[idx], device_id=peer, device_id_type=MESH))
        locs = [pltpu.make_async_copy(_view(in_refs[ii], sfn(me)), _view(out_refs[oi], dfn(me)), loc_sems.at[idx])
                for idx, (ii, sfn, oi, dfn) in enumerate(() if incoming or "ici" not in links else self.local_plan)]
        return remote, locs

    def start(self, in_refs, out_refs, sems, links=("ici", "d2d")):
        sends, locs = self._copies(in_refs, out_refs, sems, incoming=False, links=links)
        for cp in sends + locs:
            cp.start()

    def finish(self, in_refs, out_refs, sems):
        for cp in self._copies(in_refs, out_refs, sems, incoming=True)[0]:
            cp.wait_recv()
        sends, locs = self._copies(in_refs, out_refs, sems, incoming=False)
        for cp in sends:
            cp.wait_send()
        for cp in locs:
            cp.wait()


def _exchange(name, rnd):
    n_in, n_out = len(rnd.ins), len(rnd.out_shapes)

    def body(*refs):
        in_refs, out_refs, sems = refs[:n_in], refs[n_in : n_in + n_out], refs[n_in + n_out :]
        rnd.start(in_refs, out_refs, sems)
        rnd.finish(in_refs, out_refs, sems)

    return pl.pallas_call(
        body, name=name, in_specs=[ANY] * n_in, out_specs=[ANY] * n_out, out_shape=rnd.out_shapes,
        scratch_shapes=rnd.sems(), input_output_aliases={i: i for i in range(rnd.n_alias)})(*rnd.ins)


def _call(body, *, name, grid, in_specs, out_specs, out_shape, scratch_shapes, args, carry=None):
    params = _params(len(grid))
    if carry is None:
        outs = pl.pallas_call(body, name=name, grid=grid, in_specs=in_specs, out_specs=out_specs, out_shape=out_shape,
                              scratch_shapes=scratch_shapes, compiler_params=params)(*args)
        return list(outs), []
    n_ci, n_co, n_cs = len(in_specs), len(out_shape), len(scratch_shapes)
    n_xi, n_xo = len(carry.ins), len(carry.out_shapes)

    def wrapped(*refs):
        ci, xi = refs[:n_ci], refs[n_ci : n_ci + n_xi]
        o0 = n_ci + n_xi
        co, xo = refs[o0 : o0 + n_co], refs[o0 + n_co : o0 + n_co + n_xo]
        s0 = o0 + n_co + n_xo
        cs, sems = refs[s0 : s0 + n_cs], refs[s0 + n_cs :]
        ids = [pl.program_id(a) for a in range(len(grid))]
        first = functools.reduce(jnp.logical_and, [i == 0 for i in ids])
        last = functools.reduce(jnp.logical_and, [i == g - 1 for i, g in zip(ids, grid)])

        @pl.when(first)
        def _():
            carry.start(xi, xo, sems, links=("ici",))

        body(*ci, *co, *cs)

        @pl.when(first)
        def _():
            carry.start(xi, xo, sems, links=("d2d",))

        @pl.when(last)
        def _():
            carry.finish(xi, xo, sems)

    outs = pl.pallas_call(
        wrapped, name=name, grid=grid, in_specs=list(in_specs) + [ANY] * n_xi, out_specs=list(out_specs) + [ANY] * n_xo,
        out_shape=list(out_shape) + carry.out_shapes, scratch_shapes=list(scratch_shapes) + carry.sems(),
        input_output_aliases={n_ci + i: n_co + i for i in range(carry.n_alias)}, compiler_params=params,
    )(*args, *carry.ins)
    return list(outs[:n_co]), list(outs[n_co:])


def _gather_direct(arrays):
    na = len(arrays)
    outs = [_sds((NDEV,) + a.shape, a.dtype) for a in arrays]
    plan = [(k, i, lambda s, r: None, i, lambda s, r: s) for i in range(na) for k in range(1, NDEV)]
    return _Round(arrays, outs, plan, [(i, lambda m: None, i, lambda m: m) for i in range(na)])


def _gather_a(arrays):
    na = len(arrays)
    outs = [_sds((NDEV,) + a.shape, a.dtype) for a in arrays]
    plan = [(k, i, lambda s, r: None, i, lambda s, r: s) for i in range(na) for k in (2, 4, 6)]
    return _Round(arrays, outs, plan, [(i, lambda m: None, i, lambda m: m) for i in range(na)])


def _gather_b(got):
    na = len(got)
    plan = [(1, i, (lambda s, r, k=k: s ^ k), i, (lambda s, r, k=k: s ^ k)) for i in range(na) for k in (0, 2, 4, 6)]
    return _Round(got, [_sds(g.shape, g.dtype) for g in got], plan, n_alias=na)


def _scatter_1(grads):
    plan = [(1, i, (lambda s, r, q=q: 2 * q + (r & 1)), i, (lambda s, r, q=q: q))
            for i in range(len(grads)) for q in range(4)]
    return _Round(grads, [_sds((4,) + g.shape[1:], g.dtype) for g in grads], plan)


def _scatter_2(chip):
    plan = [(k, i, lambda s, r: r >> 1, i, (lambda s, r, j=j: j)) for i in range(len(chip)) for j, k in enumerate((2, 4, 6))]
    return _Round(chip, [_sds((3,) + g.shape[1:], g.dtype) for g in chip], plan)


def _add_pairs(g, got, pos, *, name):
    _, sh, w = g.shape

    def body(pos_ref, g_ref, r_ref, o_ref):
        o_ref[...] = (g_ref[...].astype(F32) + r_ref[...].astype(F32)).astype(o_ref.dtype)

    return pl.pallas_call(
        body,
        name=name,
        grid_spec=pltpu.PrefetchScalarGridSpec(
            num_scalar_prefetch=1, grid=(4,),
            in_specs=[pl.BlockSpec((1, sh, w), lambda q, p: (2 * q + p[0], 0, 0)),
                      pl.BlockSpec((1, sh, w), lambda q, p: (q, 0, 0))],
            out_specs=pl.BlockSpec((1, sh, w), lambda q, p: (q, 0, 0))),
        out_shape=_sds((4, sh, w), g.dtype),
        compiler_params=_params(),
    )(pos, g, got)


def _sum_chips(chip, got, pos, *, transpose, name):
    _, sh, w = chip.shape
    out = (w, sh) if transpose else (sh, w)

    def body(pos_ref, c_ref, r_ref, o_ref):
        acc = c_ref[0].astype(F32)
        for s in range(3):
            acc = acc + r_ref[s].astype(F32)
        o_ref[...] = acc.T if transpose else acc

    return pl.pallas_call(
        body,
        name=name,
        grid_spec=pltpu.PrefetchScalarGridSpec(
            num_scalar_prefetch=1, grid=(1,),
            in_specs=[pl.BlockSpec((1, sh, w), lambda i, p: (p[1], 0, 0)), pl.BlockSpec((3, sh, w), lambda i, p: (0, 0, 0))],
            out_specs=pl.BlockSpec(out, lambda i, p: (0, 0))),
        out_shape=_sds(out, F32),
        compiler_params=_params(),
    )(pos, chip, got)


def _adamw_math(w, g, m, v):
    m2 = ADAM_B1 * m + (1.0 - ADAM_B1) * g
    v2 = ADAM_B2 * v + (1.0 - ADAM_B2) * (g * g)
    m_hat = m2 / (1.0 - ADAM_B1 ** ADAM_STEP)
    v_hat = v2 / (1.0 - ADAM_B2 ** ADAM_STEP)
    delta = -ADAM_LR * (m_hat / (jnp.sqrt(v_hat) + ADAM_EPS) + ADAM_WD * w)
    return delta, m2, v2


def _adamw(w, g, m, v, *, name, carry=None):
    shape = w.shape
    flat = [t.reshape(-1, shape[-1]) for t in (w, g, m, v)]
    rows, cols = flat[0].shape
    tr = rows // 8 if rows % 64 == 0 else rows
    spec = _rows(tr, cols)

    def body(w_ref, g_ref, m_ref, v_ref, d_ref, m2_ref, v2_ref):
        d_ref[...], m2_ref[...], v2_ref[...] = _adamw_math(w_ref[...], g_ref[...], m_ref[...], v_ref[...])

    outs, got = _call(body, name=name, grid=(rows // tr,), in_specs=[spec] * 4, out_specs=[spec] * 3,
                      out_shape=[_sds((rows, cols), F32)] * 3, scratch_shapes=[], args=flat, carry=carry)
    return tuple(o.reshape(shape) for o in outs), got


def _adds(tag, grads, got, *, pos):
    return [_add_pairs(g, r, pos, name=f"rs_add_{tag}_{i}") for i, (g, r) in enumerate(zip(grads, got))]


def _totals(tag, chip, got, transposed=(True, False), *, pos):
    return [_sum_chips(c_, r, pos, transpose=t, name=f"rs_sum_{tag}_{i}")
            for i, (c_, r, t) in enumerate(zip(chip, got, transposed))]


def _small_sums(packets, nf, dwp, dsc, dsk, *, name):
    flat = [p for layer in packets for p in layer]

    def total(ref, *idx):
        acc = ref[(0,) + idx]
        for dev in range(1, NDEV):
            acc = acc + ref[(dev,) + idx]
        return acc

    def body(*refs):
        pk = refs[:6]
        nf_ref, dwp0, dwp1, dsc0, dsc1, dsk0, dsk1 = refs[6:13]
        dm_ref, gb_ref, gn_ref, gnf_ref, gwp_ref, gps_ref, gsk_ref = refs[13:]
        dm_ref[...] = jnp.zeros_like(dm_ref)
        gn_ref[...] = jnp.zeros_like(gn_ref)
        for l in range(2):
            for sb in range(3):
                p = pk[3 * l + sb]
                for r in range(3):
                    col = slice((3 * sb + r) * D, (3 * sb + r + 1) * D)
                    lat = p[0, 0, r : r + 1, :]
                    dm_ref[l, 0:1, col] = lat
                    for dev in range(1, NDEV):
                        row = p[dev, 0, r : r + 1, :]
                        dm_ref[l, dev : dev + 1, col] = row
                        lat = lat + row
                    ctx = total(p, 1, slice(r, r + 1), slice(None))
                    dm_ref[l, 8:9, col] = ctx
                    gb_ref[l : l + 1, col] = lat + ctx
                gn_ref[l, sb : sb + 1, :] = total(p, 0, slice(3, 4), slice(None)) + total(p, 1, slice(3, 4), slice(None))
        gnf_ref[...] = total(nf_ref, slice(0, 1), slice(None))
        for l, (a, b, c) in enumerate(((dwp0, dsc0, dsk0), (dwp1, dsc1, dsk1))):
            gwp_ref[l] = total(a, slice(None), slice(None))
            gps_ref[l : l + 1, :] = total(b, slice(0, 1), slice(None))
            gsk_ref[l] = total(c, slice(None), slice(None))

    ins = flat + [nf, dwp[0], dwp[1], dsc[0], dsc[1], dsk[0], dsk[1]]
    return pl.pallas_call(
        body,
        name=name,
        out_shape=[_sds((2, 16, NMOD * D), F32), _sds((2, NMOD * D), F32), _sds((2, 8, D), F32), _sds((1, D), F32),
                   _sds((2, PW, 128), F32), _sds((2, PW), F32), _sds((2, 8, 128), F32)],
        compiler_params=pltpu.CompilerParams(vmem_limit_bytes=VMEM_LIMIT),
    )(*ins)


def _small_adamw(c_ctx, dc_all, triples, *, name):
    n = len(triples)

    def body(*refs):
        c_ref, dc_ref = refs[0], refs[1]
        ins = refs[2 : 2 + 4 * n - 1]
        outs = refs[2 + 4 * n - 1 :]
        acc = dc_ref[0, 0, 8:9, :] + dc_ref[0, 1, 8:9, :]
        for dev in range(1, NDEV):
            acc = acc + (dc_ref[dev, 0, 8:9, :] + dc_ref[dev, 1, 8:9, :])
        c = c_ref[...]
        sig = _sigmoid(c)
        g_c = acc * (sig * (1.0 + c * (1.0 - sig)))
        outs[0][...] = g_c
        pos = 0
        for k in range(n):
            if k == 0:
                w, g, m, v = ins[0][...], g_c, ins[1][...], ins[2][...]
                pos = 3
            else:
                w, g, m, v = (ins[pos + t][...] for t in range(4))
                pos += 4
            d, m2, v2 = _adamw_math(w, g, m, v)
            outs[1 + 3 * k][...], outs[2 + 3 * k][...], outs[3 + 3 * k][...] = d, m2, v2

    flat_in = [c_ctx, dc_all]
    out_shape = [_sds(c_ctx.shape, F32)]
    for k, (w, g, m, v) in enumerate(triples):
        flat_in += [w, m, v] if k == 0 else [w, g, m, v]
        out_shape += [_sds(w.shape, F32)] * 3
    return pl.pallas_call(body, name=name, out_shape=out_shape,
                          compiler_params=pltpu.CompilerParams(vmem_limit_bytes=VMEM_LIMIT))(*flat_in)


def _rope_tables(T, R):
    t = jnp.arange(T)
    inv = ROPE_BASE ** (-jnp.arange(0, HD // 2, 2, dtype=F32) / (HD // 2))
    ang = jnp.concatenate([(t // GRID_W).astype(F32)[:, None] * inv, (t % GRID_W).astype(F32)[:, None] * inv], axis=-1)
    cos = jnp.concatenate([jnp.tile(jnp.cos(ang), (1, 4)), jnp.ones((R - T, 128), F32)], axis=0)
    sin = jnp.concatenate([jnp.tile(jnp.sin(ang), (1, 4)), jnp.zeros((R - T, 128), F32)], axis=0)
    return cos, sin


def kernel(x, c, ctx, c_ctx, w_mod, b_mod, norm_ffn1, w_ffn1_in, w_ffn1_out, norm_mix, w_in, w_pool, pool_scale, sink, w_out, norm_ffn2, w_ffn2_in, w_ffn2_out, norm_final, loss_target, m_c_ctx, m_w_mod, m_b_mod, m_norm_ffn1, m_w_ffn1_in, m_w_ffn1_out, m_norm_mix, m_w_in, m_w_pool, m_pool_scale, m_sink, m_w_out, m_norm_ffn2, m_w_ffn2_in, m_w_ffn2_out, m_norm_final, v_c_ctx, v_w_mod, v_b_mod, v_norm_ffn1, v_w_ffn1_in, v_w_ffn1_out, v_norm_mix, v_w_in, v_w_pool, v_pool_scale, v_sink, v_w_out, v_norm_ffn2, v_w_ffn2_in, v_w_ffn2_out, v_norm_final):
    T = x.shape[1]
    R = T + LC
    nl = w_mod.shape[0]
    cx, cy, cc = _coords()
    me = _lin((cx, cy, cc))
    pos = jnp.stack([cc, 2 * cx + cy]).astype(jnp.int32)
    mcols = w_mod.shape[2]

    shards = [([w_ffn1_in[l].T.astype(BF16), w_ffn1_out[l].astype(BF16)],
               [w_in[l].T.astype(BF16), w_out[l].astype(BF16)],
               [w_ffn2_in[l].T.astype(BF16), w_ffn2_out[l].astype(BF16)]) for l in range(nl)]

    got = _exchange("ag_c_w", _merge(_gather_direct([c]), _gather_a(shards[0][0] + shards[0][1])))
    c_all, w_first = got[0], got[1:]
    c16 = jnp.concatenate([c_all.reshape(NDEV, D), c_ctx[None], jnp.zeros((16 - NDEV - 1, D), F32)], axis=0)
    b_cols = lax.dynamic_slice(b_mod, (0, me * mcols), (nl, mcols)).reshape(nl, 1, mcols)
    got = _exchange("ag_mod_w", _merge(_gather_b(w_first), _gather_direct([_mod_fwd(c16, w_mod, b_cols, name="mod_fwd")])))
    w_first, mod_all = got[:4], got[4]
    mod_all = jnp.transpose(mod_all, (1, 2, 0, 3)).reshape(nl, 16, NMOD, D)
    mine = lax.dynamic_index_in_dim(mod_all, me, axis=1, keepdims=False)
    pad = jnp.zeros((nl, 16 - NMOD, D), F32)
    modv = jnp.stack([jnp.concatenate([mine, pad], axis=1), jnp.concatenate([mod_all[:, 8], pad], axis=1)], axis=1)

    gvec = [jnp.concatenate([norm_ffn1[l][None], norm_mix[l][None], norm_ffn2[l][None], jnp.zeros((5, D), F32)], axis=0)
            for l in range(nl)]
    cos, sin = _rope_tables(T, R)
    ps2 = [pool_scale[l][None] for l in range(nl)]

    h = jnp.concatenate([x[0], ctx[0]], axis=0)
    loss_all, dh, small, nf_all, big, last_partials = _forward_backward(
        h, loss_target[0], modv, gvec, shards, w_first, cos, sin, sink, w_pool, ps2, norm_final, pos, T=T)
    loss = jnp.sum(loss_all[:, 0, 0])
    grad_x = dh[:T][None]

    dm, g_b_mod, g_norms, g_nf, g_wp, g_ps, g_sk = _small_sums(
        [small[l][0:3] for l in range(nl)], nf_all, *[[small[l][k] for l in range(nl)] for k in (3, 4, 5)],
        name="small_sums")
    dm_cols = lax.dynamic_slice(dm, (0, 0, me * mcols), (nl, 16, mcols))
    g_w_mod, dc_part = _mod_bwd(c16, dm_cols, w_mod, name="mod_bwd")
    got = _exchange("rs1_tail", _merge(_scatter_1(last_partials), _gather_direct([dc_part])))
    c1, dc_all = _adds("ffn1_0", last_partials, got[:2], pos=pos), got[2]

    delta, new_m, new_v = {}, {}, {}
    (delta["w_mod"], new_m["w_mod"], new_v["w_mod"]), got = _adamw(
        w_mod, g_w_mod, m_w_mod, v_w_mod, name="adamw_w_mod", carry=_scatter_2(c1[:1]))
    big[0][0:1] = _totals("ffn1_in_0", c1[:1], got, pos=pos)
    g_w_ffn2_in = jnp.stack([big[l][4] for l in range(nl)])
    (delta["w_ffn2_in"], new_m["w_ffn2_in"], new_v["w_ffn2_in"]), got = _adamw(
        w_ffn2_in, g_w_ffn2_in, m_w_ffn2_in, v_w_ffn2_in, name="adamw_w_ffn2_in", carry=_scatter_2(c1[1:]))
    big[0][1:2] = _totals("ffn1_out_0", c1[1:], got, transposed=(False,), pos=pos)

    grads = {
        "b_mod": g_b_mod, "norm_ffn1": g_norms[:, 0], "norm_mix": g_norms[:, 1], "norm_ffn2": g_norms[:, 2],
        "w_pool": g_wp.reshape(w_pool.shape), "pool_scale": g_ps, "sink": g_sk[:, :, 0], "norm_final": g_nf.reshape(D),
        "w_mod": g_w_mod,
        "w_ffn1_in": jnp.stack([big[l][0] for l in range(nl)]), "w_ffn1_out": jnp.stack([big[l][1] for l in range(nl)]),
        "w_in": jnp.stack([big[l][2] for l in range(nl)]), "w_out": jnp.stack([big[l][3] for l in range(nl)]),
        "w_ffn2_in": g_w_ffn2_in, "w_ffn2_out": jnp.stack([big[l][5] for l in range(nl)]),
    }
    weights = dict(c_ctx=c_ctx, w_mod=w_mod, b_mod=b_mod, norm_ffn1=norm_ffn1, w_ffn1_in=w_ffn1_in, w_ffn1_out=w_ffn1_out,
                   norm_mix=norm_mix, w_in=w_in, w_pool=w_pool, pool_scale=pool_scale, sink=sink, w_out=w_out,
                   norm_ffn2=norm_ffn2, w_ffn2_in=w_ffn2_in, w_ffn2_out=w_ffn2_out, norm_final=norm_final)
    moms = dict(c_ctx=(m_c_ctx, v_c_ctx), w_mod=(m_w_mod, v_w_mod), b_mod=(m_b_mod, v_b_mod),
                norm_ffn1=(m_norm_ffn1, v_norm_ffn1), w_ffn1_in=(m_w_ffn1_in, v_w_ffn1_in),
                w_ffn1_out=(m_w_ffn1_out, v_w_ffn1_out), norm_mix=(m_norm_mix, v_norm_mix), w_in=(m_w_in, v_w_in),
                w_pool=(m_w_pool, v_w_pool), pool_scale=(m_pool_scale, v_pool_scale), sink=(m_sink, v_sink),
                w_out=(m_w_out, v_w_out), norm_ffn2=(m_norm_ffn2, v_norm_ffn2), w_ffn2_in=(m_w_ffn2_in, v_w_ffn2_in),
                w_ffn2_out=(m_w_ffn2_out, v_w_ffn2_out), norm_final=(m_norm_final, v_norm_final))
    order = list(weights)
    small_names = ["c_ctx", "b_mod", "norm_ffn1", "norm_mix", "w_pool", "pool_scale", "sink", "norm_ffn2", "norm_final"]

    def as2d(name, t):
        if name == "w_pool":
            return t.reshape(-1, 128)
        return t.reshape(1, -1) if t.ndim == 1 else t

    triples = [(as2d(n, weights[n]), None if n == "c_ctx" else as2d(n, grads[n]), as2d(n, moms[n][0]), as2d(n, moms[n][1]))
               for n in small_names]
    outs = _small_adamw(as2d("c_ctx", c_ctx), dc_all, triples, name="small_adamw")
    grads["c_ctx"] = outs[0].reshape(c_ctx.shape)
    for k, n in enumerate(small_names):
        delta[n], new_m[n], new_v[n] = (o.reshape(weights[n].shape) for o in outs[1 + 3 * k : 4 + 3 * k])
    for n in order:
        if n not in delta:
            (delta[n], new_m[n], new_v[n]), _ = _adamw(weights[n], grads[n], moms[n][0], moms[n][1], name=f"adamw_{n}")

    return (loss, grad_x, *[grads[n] for n in order], *[delta[n] for n in order],
            *[new_m[n] for n in order], *[new_v[n] for n in order])


def _merge(*rounds):
    ins, outs, plan, local, n_alias = [], [], [], [], 0
    for r in rounds:
        assert r.n_alias == 0 or (not ins and r.n_alias == len(r.ins) == len(r.out_shapes))
        oi, oo = len(ins), len(outs)
        plan += [(k, i + oi, sf, o + oo, df) for k, i, sf, o, df in r.plan]
        local += [(i + oi, sf, o + oo, df) for i, sf, o, df in r.local_plan]
        ins += r.ins
        outs += r.out_shapes
        n_alias += r.n_alias
    return _Round(ins, outs, plan, local, n_alias)


def _forward_backward(h, target, modv, gvec, shards, w_first, cos, sin, sink, w_pool, ps2, norm_final, pos, *, T):
    nl = len(gvec)
    flat = lambda ws: [w.reshape(-1, D) for w in ws]
    saved = []
    w1, wm = flat(w_first[:2]), flat(w_first[2:])
    for l in range(nl):
        last = l == nl - 1
        h0 = h
        if l == 0:
            (h1, a1, b1, f1), got = _ffn_fwd(h0, modv[l], gvec[l], *w1, T=T, mrow=0, grow=0, ctx_active=True,
                                             name=f"ffn1_fwd_{l}", carry=_gather_a(shards[l][2]))
            (u, q, k4, v4), got = _mixproj_fwd(h1, modv[l], gvec[l], wm[0], cos, sin, T=T, name=f"mixproj_fwd_{l}",
                                               carry=_gather_b(got))
            w2 = flat(got)
        else:
            (h1, a1, b1, f1), got = _ffn_fwd(h0, modv[l], gvec[l], *w1, T=T, mrow=0, grow=0, ctx_active=True,
                                             name=f"ffn1_fwd_{l}", carry=_gather_b(nxt_m + nxt_2))
            wm, w2 = flat(got[:2]), flat(got[2:])
            (u, q, k4, v4), _ = _mixproj_fwd(h1, modv[l], gvec[l], wm[0], cos, sin, T=T, name=f"mixproj_fwd_{l}")
        (cat,), nxt_1 = _attnpool_fwd(u, q, k4, v4, sink[l], w_pool[l], ps2[l], T=T, name=f"attnpool_fwd_{l}",
                                      carry=None if last else _gather_a(shards[l + 1][0]))
        (h2, mo), nxt_m = _mixout_fwd(h1, cat, modv[l], wm[1], T=T, ctx_active=not last, name=f"mixout_fwd_{l}",
                                      carry=None if last else _gather_a(shards[l + 1][1]))
        (h3, a2, b2, f2), got = _ffn_fwd(h2, modv[l], gvec[l], *w2, T=T, mrow=6, grow=2, ctx_active=not last,
                                         name=f"ffn2_fwd_{l}",
                                         carry=None if last else _merge(_gather_b(nxt_1), _gather_a(shards[l + 1][2])))
        saved.append((h0, a1, b1, f1, h1, u, q, k4, v4, cat, mo, h2, a2, b2, f2, w1, wm, w2))
        h = h3
        if not last:
            w1, nxt_2 = flat(got[:2]), got[2:]

    dh, loss_part, dnf = _loss_head(h, target, norm_final[None], T=T, name="loss_head")

    adds = functools.partial(_adds, pos=pos)
    totals = functools.partial(_totals, pos=pos)
    small, big = [None] * nl, {}
    prev = None
    for l in reversed(range(nl)):
        last = l == nl - 1
        h0, a1, b1, f1, h1, u, q, k4, v4, cat, mo, h2, a2, b2, f2, w1, wm, w2 = saved[l]
        (dh, dab, s, n, df, pk2), got = _ffn_bwd(
            h2, dh, a2, b2, f2, modv[l], gvec[l], *w2, T=T, mrow=6, grow=2, ctx_active=not last, name=f"ffn2_bwd_{l}",
            carry=_merge(_scatter_1(prev[0]), _gather_a(prev[1])) if prev else None)
        if prev:
            c1, small_a = adds(f"ffn1_{l + 1}", prev[0], got[:2]), got[2:]
        g_w2i, got = _wgrad(dab, n, bk=WG_BK, sh=2 * DFF // NDEV, name=f"wgrad_ffn2_in_{l}",
                            carry=_scatter_2(c1[:1]) if prev else None)
        if prev:
            big[l + 1][0:1] = totals(f"ffn1_in_{l + 1}", c1[:1], got)
        g_w2o, got = _wgrad(s, df, bk=WG_BK, sh=DFF // NDEV, name=f"wgrad_ffn2_out_{l}",
                            carry=_scatter_2(c1[1:]) if prev else None)
        if prev:
            big[l + 1][1:2] = totals(f"ffn1_out_{l + 1}", c1[1:], got, transposed=(False,))
        rnd = _scatter_1([g_w2i, g_w2o])
        (dcat, dmix, pko), got = _mixout_bwd(dh, mo, modv[l], wm[1], T=T, ctx_active=not last, name=f"mixout_bwd_{l}",
                                             carry=_merge(_gather_b(small_a), rnd) if prev else rnd)
        if prev:
            small[l + 1], got = got[: len(small_a)], got[len(small_a) :]
        c2 = adds(f"ffn2_{l}", [g_w2i, g_w2o], got)
        g_wo, _ = _wgrad(cat, dmix, bk=D, sh=D // NDEV, name=f"wgrad_out_{l}")
        dps, dwp, dsc = _pool_bwd(u, dcat, w_pool[l], ps2[l], T=T, name=f"pool_bwd_{l}")
        (du, dq, dk, dv, dsk), got = _attn_bwd(q, k4, v4, dcat, dps, sink[l], T=T, name=f"attn_bwd_{l}", carry=_scatter_2(c2))
        big[l] = [None, None, None, None] + totals(f"ffn2_{l}", c2, got)
        dh, dproj, n, pkm = _mixproj_bwd(h1, dh, du, dq, dk, dv, modv[l], gvec[l], wm[0], cos, sin, T=T, name=f"mixproj_bwd_{l}")
        g_wi, _ = _wgrad(dproj, n, bk=PROJ, sh=PROJ // NDEV, name=f"wgrad_in_{l}")
        (dh, dab, s, n, df, pk1), got = _ffn_bwd(h0, dh, a1, b1, f1, modv[l], gvec[l], *w1, T=T, mrow=0, grow=0,
                                                 ctx_active=True, name=f"ffn1_bwd_{l}", carry=_scatter_1([g_wi, g_wo]))
        cm = adds(f"mix_{l}", [g_wi, g_wo], got)
        mine = [pk1, pkm + pko, pk2, dwp, dsc, dsk]
        rnd = _merge(_scatter_2(cm), _gather_a(mine + [dnf, loss_part])) if l == 0 else _scatter_2(cm)
        g_w1i, got = _wgrad(dab, n, bk=WG_BK, sh=2 * DFF // NDEV, name=f"wgrad_ffn1_in_{l}", carry=rnd)
        big[l][2:4] = totals(f"mix_{l}", cm, got[:2])
        g_w1o, got = _wgrad(s, df, bk=WG_BK, sh=DFF // NDEV, name=f"wgrad_ffn1_out_{l}",
                            carry=_gather_b(got[2:]) if l == 0 else None)
        prev = ([g_w1i, g_w1o], mine)
    small[0], nf_all, loss_all = got[:6], got[6], got[7]
    return loss_all, dh, small, nf_all, big, prev[0]
```

```python
import functools

import jax
import jax.numpy as jnp
from jax import lax
from jax.experimental import pallas as pl
from jax.experimental.pallas import tpu as pltpu

F32, BF16 = jnp.float32, jnp.bfloat16

D = 1024
LC = 256
DFF = 2816
NMOD = 9
PW = 512
AW = 512
KVW = 128
PROJ = PW + AW + 2 * KVW
HD = 64
BLK = 128
GRID_W = 64
POOL_WINDOWS = (2, 4, 8, 16)
EPS = 1e-6
NEG = -1e30
ROPE_BASE = 10000.0
NDEV = 8
MESH = pl.DeviceIdType.MESH

ADAM_LR, ADAM_B1, ADAM_B2, ADAM_EPS, ADAM_WD, ADAM_STEP = 0.001, 0.9, 0.999, 1e-08, 0.01, 10

VMEM_LIMIT = 56 * 1024 * 1024
TM = 256
FFN_CHUNKS = ((0, 1536), (1536, 1280))
WG_BK = 1408

ANY = pl.BlockSpec(memory_space=pl.ANY)
SMEM = pl.BlockSpec(memory_space=pltpu.SMEM)


def _params(ngrid=1):
    return pltpu.CompilerParams(dimension_semantics=("arbitrary",) * ngrid, vmem_limit_bytes=VMEM_LIMIT)


def _dot(a, b):
    return jnp.dot(a, b, preferred_element_type=F32)


def _dot_nt(a, b):
    return lax.dot_general(a, b, (((1,), (1,)), ((), ())), preferred_element_type=F32)


def _dot_tn(a, b):
    return lax.dot_general(a, b, (((0,), (0,)), ((), ())), preferred_element_type=F32)


def _sigmoid(x):
    return 1.0 / (1.0 + jnp.exp(-x))


def _rows(tm, w):
    return pl.BlockSpec((tm, w), lambda i: (i, 0))


def _full(shape):
    nd = len(shape)
    return pl.BlockSpec(shape, lambda *_: (0,) * nd)


def _sds(shape, dtype):
    return jax.ShapeDtypeStruct(shape, dtype)


def _norm_mod(h, g, shift, scale):
    r = lax.rsqrt(jnp.mean(h * h, axis=-1, keepdims=True) + EPS)
    xhat = h * r
    y = xhat * g
    return r, xhat, y, y * (1.0 + scale) + shift


def _norm_mod_bwd(dn, r, xhat, y, g, scale):
    dshift = jnp.sum(dn, axis=0, keepdims=True)
    dscale = jnp.sum(dn * y, axis=0, keepdims=True)
    dy = dn * (1.0 + scale)
    dg = jnp.sum(dy * xhat, axis=0, keepdims=True)
    dxh = dy * g
    dh = r * (dxh - xhat * jnp.mean(dxh * xhat, axis=-1, keepdims=True))
    return dh, dshift, dscale, dg


def _acc_partials(part_ref, first, rows):
    @pl.when(first)
    def _():
        part_ref[...] = jnp.zeros_like(part_ref)

    for r, val in rows.items():
        part_ref[0, r : r + 1, :] += val


def _mod_spec(n_lat):
    return pl.BlockSpec((1, 16, D), lambda i: (i // n_lat, 0, 0))


def _part_spec(n_lat):
    return pl.BlockSpec((1, 8, D), lambda i: (i // n_lat, 0, 0))


def _load_weights(pairs, sem):
    copies = [pltpu.make_async_copy(src, dst, sem.at[k]) for k, (src, dst) in enumerate(pairs)]
    for cp in copies:
        cp.start()
    for cp in copies:
        cp.wait()


def _ffn_weight_copies(win_hbm, wout_hbm, win_v, wout_v, sem):
    loads = []
    for k, (c0, cw) in enumerate(FFN_CHUNKS):
        slabs = [(win_hbm, win_v, c0), (win_hbm, win_v, DFF + c0), (wout_hbm, wout_v, c0)]
        loads.append([pltpu.make_async_copy(src.at[pl.ds(r0, cw)], dst.at[pl.ds(r0, cw)], sem.at[3 * k + j])
                      for j, (src, dst, r0) in enumerate(slabs)])
    return loads


def _ffn_steps(i, n_active, loads, compute):
    @pl.when(i == 0)
    def _():
        for cp in sum(loads, []):
            cp.start()
        compute(loads)

    @pl.when(jnp.logical_and(i > 0, i < n_active))
    def _():
        compute(None)


def _wait_chunk(loads, k):
    if loads is not None:
        for cp in loads[k]:
            cp.wait()


def _ffn_fwd(h, modv, gvec, win, wout, *, T, mrow, grow, ctx_active, name, carry=None):
    R = h.shape[0]
    n_lat, n_tiles = T // TM, R // TM
    n_active = n_tiles if ctx_active else n_lat

    def body(h_ref, mod_ref, g_ref, win_hbm, wout_hbm, ho_ref, a_ref, b_ref, f_ref, win_v, wout_v, sem):
        i = pl.program_id(0)

        def compute(loads):
            h = h_ref[...]
            shift, scale, gate = (mod_ref[0, mrow + k : mrow + k + 1, :] for k in range(3))
            _, _, _, n = _norm_mod(h, g_ref[grow : grow + 1, :], shift, scale)
            n_bf = n.astype(BF16)
            acc = jnp.zeros((TM, D), F32)
            for k, (c0, cw) in enumerate(FFN_CHUNKS):
                _wait_chunk(loads, k)
                a = _dot_nt(n_bf, win_v[c0 : c0 + cw, :])
                b = _dot_nt(n_bf, win_v[DFF + c0 : DFF + c0 + cw, :])
                a_ref[:, c0 : c0 + cw] = a.astype(BF16)
                b_ref[:, c0 : c0 + cw] = b.astype(BF16)
                s = a * _sigmoid(a) * b
                acc = acc + _dot(s.astype(BF16), wout_v[c0 : c0 + cw, :])
            f_ref[...] = acc.astype(BF16)
            ho_ref[...] = h + (0.5 * gate) * acc

        _ffn_steps(i, n_active, _ffn_weight_copies(win_hbm, wout_hbm, win_v, wout_v, sem), compute)

        @pl.when(i >= n_active)
        def _():
            ho_ref[...] = h_ref[...]
            a_ref[...] = jnp.zeros_like(a_ref)
            b_ref[...] = jnp.zeros_like(b_ref)
            f_ref[...] = jnp.zeros_like(f_ref)

    return _call(
        body,
        name=name,
        grid=(n_tiles,),
        in_specs=[_rows(TM, D), _mod_spec(n_lat), _full((8, D)), ANY, ANY],
        out_specs=[_rows(TM, D), _rows(TM, DFF), _rows(TM, DFF), _rows(TM, D)],
        out_shape=[_sds((R, D), F32), _sds((R, DFF), BF16), _sds((R, DFF), BF16), _sds((R, D), BF16)],
        scratch_shapes=[pltpu.VMEM((2 * DFF, D), BF16), pltpu.VMEM((DFF, D), BF16),
                        pltpu.SemaphoreType.DMA((3 * len(FFN_CHUNKS),))],
        args=(h, modv, gvec, win, wout),
        carry=carry,
    )


def _ffn_bwd(h, dho, a, b, f, modv, gvec, win, wout, *, T, mrow, grow, ctx_active, name, carry=None):
    R = h.shape[0]
    n_lat, n_tiles = T // TM, R // TM
    n_active = n_tiles if ctx_active else n_lat

    def body(h_ref, dho_ref, a_ref, b_ref, f_ref, mod_ref, g_ref, win_hbm, wout_hbm,
             dh_ref, dab_ref, s_ref, n_ref, df_ref, part_ref, win_v, wout_v, sem):
        i = pl.program_id(0)
        first = jnp.logical_or(i == 0, i == n_lat)

        def compute(loads):
            h = h_ref[...]
            dho = dho_ref[...]
            shift, scale, gate = (mod_ref[0, mrow + k : mrow + k + 1, :] for k in range(3))
            g = g_ref[grow : grow + 1, :]
            r, xhat, y, n = _norm_mod(h, g, shift, scale)
            dgate = 0.5 * jnp.sum(dho * f_ref[...].astype(F32), axis=0, keepdims=True)
            df_bf = ((0.5 * gate) * dho).astype(BF16)
            df_ref[...] = df_bf
            n_ref[...] = n.astype(BF16)
            dn = jnp.zeros((TM, D), F32)
            for k, (c0, cw) in enumerate(FFN_CHUNKS):
                _wait_chunk(loads, k)
                ds = _dot_nt(df_bf, wout_v[c0 : c0 + cw, :])
                av = a_ref[:, c0 : c0 + cw].astype(F32)
                bv = b_ref[:, c0 : c0 + cw].astype(F32)
                sig = _sigmoid(av)
                sa = av * sig
                s_ref[:, c0 : c0 + cw] = (sa * bv).astype(BF16)
                da = (ds * bv * (sig * (1.0 + av * (1.0 - sig)))).astype(BF16)
                db = (ds * sa).astype(BF16)
                dab_ref[:, c0 : c0 + cw] = da
                dab_ref[:, DFF + c0 : DFF + c0 + cw] = db
                dn = dn + _dot(da, win_v[c0 : c0 + cw, :]) + _dot(db, win_v[DFF + c0 : DFF + c0 + cw, :])
            dh, dshift, dscale, dg = _norm_mod_bwd(dn, r, xhat, y, g, scale)
            dh_ref[...] = dho + dh
            _acc_partials(part_ref, first, {0: dshift, 1: dscale, 2: dgate, 3: dg})

        _ffn_steps(i, n_active, _ffn_weight_copies(win_hbm, wout_hbm, win_v, wout_v, sem), compute)

        @pl.when(i >= n_active)
        def _():
            dh_ref[...] = dho_ref[...]
            dab_ref[...] = jnp.zeros_like(dab_ref)
            s_ref[...] = jnp.zeros_like(s_ref)
            n_ref[...] = jnp.zeros_like(n_ref)
            df_ref[...] = jnp.zeros_like(df_ref)
            part_ref[...] = jnp.zeros_like(part_ref)

    return _call(
        body,
        name=name,
        grid=(n_tiles,),
        in_specs=[_rows(TM, D), _rows(TM, D), _rows(TM, DFF), _rows(TM, DFF), _rows(TM, D),
                  _mod_spec(n_lat), _full((8, D)), ANY, ANY],
        out_specs=[_rows(TM, D), _rows(TM, 2 * DFF), _rows(TM, DFF), _rows(TM, D), _rows(TM, D), _part_spec(n_lat)],
        out_shape=[_sds((R, D), F32), _sds((R, 2 * DFF), BF16), _sds((R, DFF), BF16), _sds((R, D), BF16),
                   _sds((R, D), BF16), _sds((2, 8, D), F32)],
        scratch_shapes=[pltpu.VMEM((2 * DFF, D), BF16), pltpu.VMEM((DFF, D), BF16),
                        pltpu.SemaphoreType.DMA((3 * len(FFN_CHUNKS),))],
        args=(h, dho, a, b, f, modv, gvec, win, wout),
        carry=carry,
    )


def _wgrad(x, y, *, bk, sh, name, carry=None):
    R, kx = x.shape
    n = y.shape[1]
    tr = R // 2
    nr, nsh = R // tr, bk // sh

    def body(x_ref, y_ref, o_ref, acc):
        r = pl.program_id(1)

        @pl.when(r == 0)
        def _():
            acc[...] = jnp.zeros_like(acc)

        acc[...] += _dot_tn(x_ref[...], y_ref[...])

        @pl.when(r == nr - 1)
        def _():
            for s in range(nsh):
                o_ref[s] = acc[s * sh : (s + 1) * sh, :].astype(BF16)

    (out,), got = _call(
        body,
        name=name,
        grid=(kx // bk, nr),
        in_specs=[pl.BlockSpec((tr, bk), lambda k, r: (r, k)), pl.BlockSpec((tr, n), lambda k, r: (r, 0))],
        out_specs=[pl.BlockSpec((nsh, sh, n), lambda k, r: (k, 0, 0))],
        out_shape=[_sds((kx // sh, sh, n), BF16)],
        scratch_shapes=[pltpu.VMEM((bk, n), F32)],
        args=(x, y),
        carry=carry,
    )
    return out, got


def _rot_half(x):
    lane = lax.broadcasted_iota(jnp.int32, x.shape, 1)
    return jnp.where((lane & (HD - 1)) < HD // 2, -pltpu.roll(x, 128 - HD // 2, 1), pltpu.roll(x, HD // 2, 1))


def _tile_sel():
    i = lax.broadcasted_iota(jnp.int32, (KVW, AW), 0)
    j = lax.broadcasted_iota(jnp.int32, (KVW, AW), 1)
    return jnp.where(i == (j // 256) * HD + (j & (HD - 1)), 1.0, 0.0).astype(BF16)


def _mixproj_fwd(h, modv, gvec, win, cos, sin, *, T, name, carry=None):
    R = h.shape[0]
    n_lat, n_tiles = T // TM, R // TM

    def body(h_ref, mod_ref, g_ref, win_ref, cos_ref, sin_ref, u_ref, q_ref, k4_ref, v4_ref):
        shift, scale = mod_ref[0, 3:4, :], mod_ref[0, 4:5, :]
        _, _, _, n = _norm_mod(h_ref[...], g_ref[1:2, :], shift, scale)
        proj = _dot_nt(n.astype(BF16), win_ref[...])
        u_ref[...] = proj[:, :PW]
        cs, sn = cos_ref[...], sin_ref[...]
        for s in range(AW // 128):
            x = proj[:, PW + 128 * s : PW + 128 * (s + 1)]
            q_ref[:, 128 * s : 128 * (s + 1)] = ((x * cs + _rot_half(x) * sn) * (HD ** -0.5)).astype(BF16)
        k = proj[:, PW + AW : PW + AW + KVW]
        k = (k * cs + _rot_half(k) * sn).astype(BF16)
        v = proj[:, PW + AW + KVW :].astype(BF16)
        sel = _tile_sel()
        k4_ref[...] = _dot(k, sel).astype(BF16)
        v4_ref[...] = _dot(v, sel).astype(BF16)

    return _call(
        body,
        name=name,
        grid=(n_tiles,),
        in_specs=[_rows(TM, D), _mod_spec(n_lat), _full((8, D)), _full((PROJ, D)), _rows(TM, 128), _rows(TM, 128)],
        out_specs=[_rows(TM, PW), _rows(TM, AW), _rows(TM, AW), _rows(TM, AW)],
        out_shape=[_sds((R, PW), F32), _sds((R, AW), BF16), _sds((R, AW), BF16), _sds((R, AW), BF16)],
        scratch_shapes=[],
        args=(h, modv, gvec, win, cos, sin),
        carry=carry,
    )


def _win_start(j, hi):
    return pl.multiple_of(jnp.clip((j - 1) * BLK, 0, hi - 3 * BLK), BLK)


def _hi_lo(x):
    hi = x.astype(BF16)
    return hi, (x - hi.astype(F32)).astype(BF16)


def _pool_bounds(t, w, T, R):
    is_ctx = t >= T
    lo = jnp.maximum(t - w // 2, jnp.where(is_ctx, T, 0))
    hi = jnp.minimum(t + w // 2, jnp.where(is_ctx, R, T))
    return lo, hi


def _pooled(u_v, j, T, R):
    start = _win_start(j, R)
    u3_hi, u3_lo = _hi_lo(u_v[pl.ds(start, 3 * BLK), :])
    ub = u_v[pl.ds(pl.multiple_of(j * BLK, BLK), BLK), :]
    t = j * BLK + lax.broadcasted_iota(jnp.int32, (BLK, 1), 0)
    pos = start + lax.broadcasted_iota(jnp.int32, (1, 3 * BLK), 1)
    pooled, counts = [], []
    for g, w in enumerate(POOL_WINDOWS):
        lo, hi = _pool_bounds(t, w, T, R)
        band = jnp.where(pos >= lo, jnp.where(pos < hi, 1.0, 0.0), 0.0).astype(BF16)
        sl = slice(g * 128, (g + 1) * 128)
        sums = _dot(band, u3_hi[:, sl]) + _dot(band, u3_lo[:, sl])
        cnt = (hi - lo).astype(F32)
        pooled.append(sums / cnt - ub[:, sl])
        counts.append(cnt)
    return pooled, counts


def _stack_heads(x):
    lane_h = lax.broadcasted_iota(jnp.int32, x.shape, 1) // HD
    return jnp.concatenate([jnp.where(lane_h == h, x, jnp.zeros_like(x)) for h in range(4)], axis=0)


def _unstack_heads(x):
    lane_h = lax.broadcasted_iota(jnp.int32, (BLK, 256), 1) // HD
    out = jnp.zeros((BLK, 256), F32)
    for h in range(4):
        out = out + jnp.where(lane_h == h, x[h * BLK : (h + 1) * BLK, :], 0.0)
    return out


def _window_mask(j, start_l, nbl):
    rowi = lax.broadcasted_iota(jnp.int32, (4 * BLK, 1), 0)
    qpos = j * BLK + (rowi & (BLK - 1))
    kpos = start_l + lax.broadcasted_iota(jnp.int32, (1, 3 * BLK), 1)
    reach = jnp.where(j < nbl, BLK, -1)
    return jnp.abs(kpos - qpos) <= reach


def _attn_exps(qs, kl, kc, sink_ref, g, valid):
    s_l = jnp.where(valid, _dot_nt(qs, kl), NEG)
    s_c = _dot_nt(qs, kc)
    rb = lax.broadcasted_iota(jnp.int32, (4 * BLK, 1), 0) // BLK
    sk = jnp.where(rb == 0, sink_ref[4 * g], jnp.where(rb == 1, sink_ref[4 * g + 1],
                   jnp.where(rb == 2, sink_ref[4 * g + 2], sink_ref[4 * g + 3])))
    m = jnp.maximum(jnp.maximum(jnp.max(s_l, axis=1, keepdims=True), jnp.max(s_c, axis=1, keepdims=True)), sk)
    e_l, e_c, e_s = jnp.exp(s_l - m), jnp.exp(s_c - m), jnp.exp(sk - m)
    inv = 1.0 / (jnp.sum(e_l, axis=1, keepdims=True) + jnp.sum(e_c, axis=1, keepdims=True) + e_s)
    return e_l, e_c, e_s, inv


def _attnpool_fwd(u, q, k4, v4, sink, w_pool, pool_scale, *, T, name, carry=None):
    R = u.shape[0]
    nb, nbl = R // BLK, T // BLK

    def body(q_ref, sink_ref, wp_ref, ps_ref, u_hbm, k4_hbm, v4_hbm, cat_ref, u_v, k4_v, v4_v, sem):
        j = pl.program_id(0)

        @pl.when(j == 0)
        def _():
            _load_weights([(u_hbm, u_v), (k4_hbm, k4_v), (v4_hbm, v4_v)], sem)

        pooled, _ = _pooled(u_v, j, T, R)
        for g in range(4):
            mixed = _dot(pooled[g].astype(BF16), wp_ref[g].astype(BF16)) * ps_ref[:, g * 128 : (g + 1) * 128]
            cat_ref[:, g * 128 : (g + 1) * 128] = mixed.astype(BF16)

        start_l = _win_start(j, T)
        valid = _window_mask(j, start_l, nbl)
        for g in range(2):
            gl = slice(g * 256, (g + 1) * 256)
            qs = _stack_heads(q_ref[:, gl])
            e_l, e_c, _, inv = _attn_exps(qs, k4_v[pl.ds(start_l, 3 * BLK), gl], k4_v[T:R, gl], sink_ref, g, valid)
            o = _dot(e_l.astype(BF16), v4_v[pl.ds(start_l, 3 * BLK), gl]) + _dot(e_c.astype(BF16), v4_v[T:R, gl])
            cat_ref[:, PW + g * 256 : PW + (g + 1) * 256] = _unstack_heads(o * inv).astype(BF16)

    return _call(
        body,
        name=name,
        grid=(nb,),
        in_specs=[_rows(BLK, AW), SMEM, _full((4, 128, 128)), _full((1, PW)), ANY, ANY, ANY],
        out_specs=[_rows(BLK, D)],
        out_shape=[_sds((R, D), BF16)],
        scratch_shapes=[pltpu.VMEM((R, PW), F32), pltpu.VMEM((R, AW), BF16), pltpu.VMEM((R, AW), BF16),
                        pltpu.SemaphoreType.DMA((3,))],
        args=(q, sink, w_pool, pool_scale, u, k4, v4),
        carry=carry,
    )


def _mixout_fwd(h, cat, modv, wout, *, T, ctx_active, name, carry=None):
    R = h.shape[0]
    n_lat, n_tiles = T // TM, R // TM

    def body(h_ref, cat_ref, mod_ref, w_ref, ho_ref, mo_ref):
        i = pl.program_id(0)

        def compute():
            mo = _dot(cat_ref[...], w_ref[...])
            mo_ref[...] = mo.astype(BF16)
            ho_ref[...] = h_ref[...] + mod_ref[0, 5:6, :] * mo

        if ctx_active:
            compute()
        else:
            pl.when(i < n_lat)(compute)

            @pl.when(i >= n_lat)
            def _():
                ho_ref[...] = h_ref[...]
                mo_ref[...] = jnp.zeros_like(mo_ref)

    return _call(
        body,
        name=name,
        grid=(n_tiles,),
        in_specs=[_rows(TM, D), _rows(TM, D), _mod_spec(n_lat), _full((D, D))],
        out_specs=[_rows(TM, D), _rows(TM, D)],
        out_shape=[_sds((R, D), F32), _sds((R, D), BF16)],
        scratch_shapes=[],
        args=(h, cat, modv, wout),
        carry=carry,
    )


def _mixout_bwd(dho, mo, modv, wout, *, T, ctx_active, name, carry=None):
    R = dho.shape[0]
    n_lat, n_tiles = T // TM, R // TM

    def body(dho_ref, mo_ref, mod_ref, w_ref, dcat_ref, dmix_ref, part_ref):
        i = pl.program_id(0)
        first = jnp.logical_or(i == 0, i == n_lat)

        def compute():
            dho = dho_ref[...]
            dmix = (mod_ref[0, 5:6, :] * dho).astype(BF16)
            dmix_ref[...] = dmix
            dcat_ref[...] = _dot_nt(dmix, w_ref[...])
            dgate = jnp.sum(dho * mo_ref[...].astype(F32), axis=0, keepdims=True)
            _acc_partials(part_ref, first, {2: dgate})

        if ctx_active:
            compute()
        else:
            pl.when(i < n_lat)(compute)

            @pl.when(i >= n_lat)
            def _():
                dcat_ref[...] = jnp.zeros_like(dcat_ref)
                dmix_ref[...] = jnp.zeros_like(dmix_ref)
                part_ref[...] = jnp.zeros_like(part_ref)

    return _call(
        body,
        name=name,
        grid=(n_tiles,),
        in_specs=[_rows(TM, D), _rows(TM, D), _mod_spec(n_lat), _full((D, D))],
        out_specs=[_rows(TM, D), _rows(TM, D), _part_spec(n_lat)],
        out_shape=[_sds((R, D), F32), _sds((R, D), BF16), _sds((2, 8, D), F32)],
        scratch_shapes=[],
        args=(dho, mo, modv, wout),
        carry=carry,
    )


def _pool_bwd(u, dcat, w_pool, pool_scale, *, T, name):
    R = u.shape[0]
    nb = R // BLK

    def body(dcat_ref, wp_ref, ps_ref, u_hbm, dps_ref, dwp_ref, dsc_ref, u_v, sem):
        j = pl.program_id(0)

        @pl.when(j == 0)
        def _():
            _load_weights([(u_hbm, u_v)], sem)
            dwp_ref[...] = jnp.zeros_like(dwp_ref)
            dsc_ref[...] = jnp.zeros_like(dsc_ref)

        pooled, counts = _pooled(u_v, j, T, R)
        for g in range(4):
            sl = slice(g * 128, (g + 1) * 128)
            p_bf = pooled[g].astype(BF16)
            w_bf = wp_ref[g].astype(BF16)
            dmixed = dcat_ref[:, sl]
            dsc_ref[0:1, sl] += jnp.sum(dmixed * _dot(p_bf, w_bf), axis=0, keepdims=True)
            dmp = (dmixed * ps_ref[:, sl]).astype(BF16)
            dwp_ref[sl, :] += _dot_tn(p_bf, dmp)
            dps_ref[:, sl] = _dot_nt(dmp, w_bf) / counts[g]

    return pl.pallas_call(
        body,
        name=name,
        grid=(nb,),
        in_specs=[_rows(BLK, D), _full((4, 128, 128)), _full((1, PW)), ANY],
        out_specs=[_rows(BLK, PW), _full((PW, 128)), _full((8, PW))],
        out_shape=[_sds((R, PW), F32), _sds((PW, 128), F32), _sds((8, PW), F32)],
        scratch_shapes=[pltpu.VMEM((R, PW), F32), pltpu.SemaphoreType.DMA((1,))],
        compiler_params=_params(),
    )(dcat, w_pool, pool_scale, u)


def _fold_heads(x):
    y = x[:, :128] + x[:, 128:]
    return y + pltpu.roll(y, HD, 1)


def _attn_bwd(q, k4, v4, dcat, dps, sink, *, T, name, carry=None):
    R = q.shape[0]
    nb, nbl = R // BLK, T // BLK

    def body(q_ref, dcat_ref, sink_ref, k4_hbm, v4_hbm, dps_hbm, du_ref, dq_ref, dk_ref, dv_ref, dsk_ref,
             k4_v, v4_v, dps_v, sem):
        j = pl.program_id(0)

        @pl.when(j == 0)
        def _():
            _load_weights([(k4_hbm, k4_v), (v4_hbm, v4_v), (dps_hbm, dps_v)], sem)
            dk_ref[...] = jnp.zeros_like(dk_ref)
            dv_ref[...] = jnp.zeros_like(dv_ref)
            dsk_ref[...] = jnp.zeros_like(dsk_ref)

        start = _win_start(j, R)
        d3_hi, d3_lo = _hi_lo(dps_v[pl.ds(start, 3 * BLK), :])
        db = dps_v[pl.ds(pl.multiple_of(j * BLK, BLK), BLK), :]
        pos = j * BLK + lax.broadcasted_iota(jnp.int32, (BLK, 1), 0)
        t_r = start + lax.broadcasted_iota(jnp.int32, (1, 3 * BLK), 1)
        for g, w in enumerate(POOL_WINDOWS):
            sl = slice(g * 128, (g + 1) * 128)
            lo_r, hi_r = _pool_bounds(t_r, w, T, R)
            band_t = jnp.where(pos >= lo_r, jnp.where(pos < hi_r, 1.0, 0.0), 0.0).astype(BF16)
            lo_c, hi_c = _pool_bounds(pos, w, T, R)
            du_ref[:, sl] = _dot(band_t, d3_hi[:, sl]) + _dot(band_t, d3_lo[:, sl]) - db[:, sl] * (hi_c - lo_c).astype(F32)

        start_l = _win_start(j, T)
        valid = _window_mask(j, start_l, nbl)
        rb = lax.broadcasted_iota(jnp.int32, (4 * BLK, 1), 0) // BLK
        lane = lax.broadcasted_iota(jnp.int32, (1, 128), 1)
        dk_l, dk_c, dv_l, dv_c = [], [], [], []
        for g in range(2):
            gl = slice(g * 256, (g + 1) * 256)
            qs = _stack_heads(q_ref[:, gl])
            kl, kc = k4_v[pl.ds(start_l, 3 * BLK), gl], k4_v[T:R, gl]
            vl, vc = v4_v[pl.ds(start_l, 3 * BLK), gl], v4_v[T:R, gl]
            e_l, e_c, e_s, inv = _attn_exps(qs, kl, kc, sink_ref, g, valid)
            p_l, p_c, p_s = e_l * inv, e_c * inv, e_s * inv
            dos = _stack_heads(dcat_ref[:, PW + g * 256 : PW + (g + 1) * 256]).astype(BF16)
            dp_l, dp_c = _dot_nt(dos, vl), _dot_nt(dos, vc)
            delta = jnp.sum(p_l * dp_l, axis=1, keepdims=True) + jnp.sum(p_c * dp_c, axis=1, keepdims=True)
            ds_l = (p_l * (dp_l - delta)).astype(BF16)
            ds_c = (p_c * (dp_c - delta)).astype(BF16)
            dq_ref[:, gl] = _unstack_heads(_dot(ds_l, kl) + _dot(ds_c, kc)) * (HD ** -0.5)
            dk_l.append(_fold_heads(_dot_tn(ds_l, qs)))
            dk_c.append(_fold_heads(_dot_tn(ds_c, qs)))
            dv_l.append(_fold_heads(_dot_tn(p_l.astype(BF16), dos)))
            dv_c.append(_fold_heads(_dot_tn(p_c.astype(BF16), dos)))
            dsink = -p_s * delta
            for h in range(4):
                tot = jnp.sum(jnp.where(rb == h, dsink, 0.0), axis=0, keepdims=True)
                dsk_ref[4 * g + h : 4 * g + h + 1, :] += jnp.broadcast_to(tot, (1, 128))
        first = lane < HD
        dk_ref[pl.ds(start_l, 3 * BLK), :] += jnp.where(first, dk_l[0], dk_l[1])
        dk_ref[T:R, :] += jnp.where(first, dk_c[0], dk_c[1])
        dv_ref[pl.ds(start_l, 3 * BLK), :] += jnp.where(first, dv_l[0], dv_l[1])
        dv_ref[T:R, :] += jnp.where(first, dv_c[0], dv_c[1])

    return _call(
        body,
        name=name,
        grid=(nb,),
        in_specs=[_rows(BLK, AW), _rows(BLK, D), SMEM, ANY, ANY, ANY],
        out_specs=[_rows(BLK, PW), _rows(BLK, AW), _full((R, KVW)), _full((R, KVW)), _full((8, 128))],
        out_shape=[_sds((R, PW), F32), _sds((R, AW), F32), _sds((R, KVW), F32), _sds((R, KVW), F32),
                   _sds((8, 128), F32)],
        scratch_shapes=[pltpu.VMEM((R, AW), BF16), pltpu.VMEM((R, AW), BF16), pltpu.VMEM((R, PW), F32),
                        pltpu.SemaphoreType.DMA((3,))],
        args=(q, dcat, sink, k4, v4, dps),
        carry=carry,
    )


def _mixproj_bwd(h, dho, du, dq, dk, dv, modv, gvec, win, cos, sin, *, T, name):
    R = h.shape[0]
    n_lat, n_tiles = T // TM, R // TM

    def body(h_ref, dho_ref, du_ref, dq_ref, dk_ref, dv_ref, mod_ref, g_ref, win_ref, cos_ref, sin_ref,
             dh_ref, dproj_ref, n_ref, part_ref):
        i = pl.program_id(0)
        first = jnp.logical_or(i == 0, i == n_lat)
        shift, scale = mod_ref[0, 3:4, :], mod_ref[0, 4:5, :]
        g = g_ref[1:2, :]
        r, xhat, y, n = _norm_mod(h_ref[...], g, shift, scale)
        n_ref[...] = n.astype(BF16)
        cs, sn = cos_ref[...], sin_ref[...]
        dproj_ref[:, :PW] = du_ref[...].astype(BF16)
        for s in range(AW // 128):
            x = dq_ref[:, 128 * s : 128 * (s + 1)]
            dproj_ref[:, PW + 128 * s : PW + 128 * (s + 1)] = (x * cs - _rot_half(x) * sn).astype(BF16)
        x = dk_ref[...]
        dproj_ref[:, PW + AW : PW + AW + KVW] = (x * cs - _rot_half(x) * sn).astype(BF16)
        dproj_ref[:, PW + AW + KVW :] = dv_ref[...].astype(BF16)
        dn = _dot(dproj_ref[...], win_ref[...])
        dh, dshift, dscale, dg = _norm_mod_bwd(dn, r, xhat, y, g, scale)
        dh_ref[...] = dho_ref[...] + dh
        _acc_partials(part_ref, first, {0: dshift, 1: dscale, 3: dg})

    return pl.pallas_call(
        body,
        name=name,
        grid=(n_tiles,),
        in_specs=[_rows(TM, D), _rows(TM, D), _rows(TM, PW), _rows(TM, AW), _rows(TM, KVW), _rows(TM, KVW),
                  _mod_spec(n_lat), _full((8, D)), _full((PROJ, D)), _rows(TM, 128), _rows(TM, 128)],
        out_specs=[_rows(TM, D), _rows(TM, PROJ), _rows(TM, D), _part_spec(n_lat)],
        out_shape=[_sds((R, D), F32), _sds((R, PROJ), BF16), _sds((R, D), BF16), _sds((2, 8, D), F32)],
        compiler_params=_params(),
    )(h, dho, du, dq, dk, dv, modv, gvec, win, cos, sin)


def _loss_head(h, target, g_final, *, T, name):
    R = h.shape[0]
    n_lat, n_tiles = T // TM, R // TM

    def body(h_ref, t_ref, g_ref, dh_ref, loss_ref, dg_ref):
        i = pl.program_id(0)

        @pl.when(i == 0)
        def _():
            loss_ref[...] = jnp.zeros_like(loss_ref)
            dg_ref[...] = jnp.zeros_like(dg_ref)

        @pl.when(i < n_lat)
        def _():
            h = h_ref[...]
            g = g_ref[...]
            r = lax.rsqrt(jnp.mean(h * h, axis=-1, keepdims=True) + EPS)
            xhat = h * r
            err = xhat * g - t_ref[...]
            tot = jnp.sum(jnp.sum(err * err, axis=1, keepdims=True), axis=0, keepdims=True)
            loss_ref[...] += jnp.broadcast_to(tot * (0.5 / D), loss_ref.shape)
            dy = err * (1.0 / D)
            dg_ref[0:1, :] += jnp.sum(dy * xhat, axis=0, keepdims=True)
            dxh = dy * g
            dh_ref[...] = r * (dxh - xhat * jnp.mean(dxh * xhat, axis=-1, keepdims=True))

        @pl.when(i >= n_lat)
        def _():
            dh_ref[...] = jnp.zeros_like(dh_ref)

    return pl.pallas_call(
        body,
        name=name,
        grid=(n_tiles,),
        in_specs=[_rows(TM, D), pl.BlockSpec((TM, D), lambda i: (jnp.minimum(i, n_lat - 1), 0)), _full((1, D))],
        out_specs=[_rows(TM, D), _full((8, 128)), _full((8, D))],
        out_shape=[_sds((R, D), F32), _sds((8, 128), F32), _sds((8, D), F32)],
        compiler_params=_params(),
    )(h, target, g_final)


def _mod_fwd(c16, w_mod, b_cols, *, name):
    nl, _, cols = w_mod.shape

    def body(c_ref, w_ref, b_ref, o_ref):
        c = c_ref[...]
        sc = (c * _sigmoid(c)).astype(BF16)
        o_ref[0] = _dot(sc, w_ref[0].astype(BF16)) + b_ref[0]

    return pl.pallas_call(
        body,
        name=name,
        grid=(nl,),
        in_specs=[_full((16, D)), pl.BlockSpec((1, D, cols), lambda l: (l, 0, 0)),
                  pl.BlockSpec((1, 1, cols), lambda l: (l, 0, 0))],
        out_specs=pl.BlockSpec((1, 16, cols), lambda l: (l, 0, 0)),
        out_shape=_sds((nl, 16, cols), F32),
        compiler_params=_params(),
    )(c16, w_mod, b_cols)


def _mod_bwd(c16, dm_cols, w_mod, *, name):
    nl, _, cols = w_mod.shape

    def body(c_ref, dm_ref, w_ref, gw_ref, dc_ref):
        c = c_ref[...]
        sc = (c * _sigmoid(c)).astype(BF16)
        dm = dm_ref[0].astype(BF16)
        gw_ref[0] = _dot_tn(sc, dm)
        dc_ref[0] = _dot_nt(dm, w_ref[0].astype(BF16))

    return pl.pallas_call(
        body,
        name=name,
        grid=(nl,),
        in_specs=[_full((16, D)), pl.BlockSpec((1, 16, cols), lambda l: (l, 0, 0)),
                  pl.BlockSpec((1, D, cols), lambda l: (l, 0, 0))],
        out_specs=[pl.BlockSpec((1, D, cols), lambda l: (l, 0, 0)), pl.BlockSpec((1, 16, D), lambda l: (l, 0, 0))],
        out_shape=[_sds((nl, D, cols), F32), _sds((nl, 16, D), F32)],
        compiler_params=_params(),
    )(c16, dm_cols, w_mod)


def _coords():
    return lax.axis_index("x"), lax.axis_index("y"), lax.axis_index("c")


FWD = 8


def _peer(k, x, y, c):
    if k == FWD:
        return (x ^ (1 - c), y ^ c, c)
    return (1 - x if k & 4 else x, 1 - y if k & 2 else y, 1 - c if k & 1 else c)


def _lin(p):
    return 4 * p[0] + 2 * p[1] + p[2]


def _view(ref, slot):
    return ref if slot is None else ref.at[slot]


class _Round:
    def __init__(self, ins, out_shapes, plan, local_plan=(), n_alias=0):
        self.ins, self.out_shapes = list(ins), list(out_shapes)
        self.plan, self.local_plan, self.n_alias = list(plan), list(local_plan), n_alias
        fed = {p[3] for p in self.plan if p[0] == FWD}
        self.feeders = [n for n, p in enumerate(self.plan) if p[0] in (2, 4, 6) and p[3] in fed]

    def sems(self):
        return [pltpu.SemaphoreType.DMA((len(self.plan),)), pltpu.SemaphoreType.DMA((len(self.plan),)),
                pltpu.SemaphoreType.DMA((max(len(self.local_plan), 1),))]

    def _remote(self, in_refs, out_refs, sems, incoming, pick):
        in_refs = list(out_refs[: self.n_alias]) + list(in_refs[self.n_alias :])
        x, y, c = _coords()
        me = _lin((x, y, c))
        copies = {}
        for idx, (k, ii, sfn, oi, dfn) in enumerate(self.plan):
            if not pick(idx, "d2d" if k == 1 else "fwd" if k == FWD else "ici"):
                continue
            peer = _peer(k, x, y, c)
            sender, receiver = (_lin(peer), me) if incoming else (me, _lin(peer))
            src = out_refs[oi] if ii is None else in_refs[ii]
            copies[idx] = pltpu.make_async_remote_copy(
                src_ref=_view(src, sfn(sender, receiver)), dst_ref=_view(out_refs[oi], dfn(sender, receiver)),
                send_sem=sems[0].at[idx], recv_sem=sems[1].at[idx], device_id=peer, device_id_type=MESH)
        return copies

    def _local(self, in_refs, out_refs, sems):
        in_refs = list(out_refs[: self.n_alias]) + list(in_refs[self.n_alias :])
        me = _lin(_coords())
        return [pltpu.make_async_copy(_view(in_refs[ii], sfn(me)), _view(out_refs[oi], dfn(me)), sems[2].at[idx])
                for idx, (ii, sfn, oi, dfn) in enumerate(self.local_plan)]

    def start(self, in_refs, out_refs, sems, links=("ici", "d2d")):
        for cp in self._remote(in_refs, out_refs, sems, False, lambda n, link: link in links).values():
            cp.start()
        if "ici" in links:
            for cp in self._local(in_refs, out_refs, sems):
                cp.start()

    def mid(self, in_refs, out_refs, sems):
        if self.feeders:
            for cp in self._remote(in_refs, out_refs, sems, True, lambda n, link: n in self.feeders).values():
                cp.wait_recv()
            for cp in self._remote(in_refs, out_refs, sems, False, lambda n, link: link == "fwd").values():
                cp.start()

    def finish(self, in_refs, out_refs, sems):
        for cp in self._remote(in_refs, out_refs, sems, True, lambda n, link: n not in self.feeders).values():
            cp.wait_recv()
        for cp in self._remote(in_refs, out_refs, sems, False, lambda n, link: True).values():
            cp.wait_send()
        for cp in self._local(in_refs, out_refs, sems):
            cp.wait()


def _exchange(name, rnd):
    n_in, n_out = len(rnd.ins), len(rnd.out_shapes)

    def body(*refs):
        in_refs, out_refs, sems = refs[:n_in], refs[n_in : n_in + n_out], refs[n_in + n_out :]
        rnd.start(in_refs, out_refs, sems)
        rnd.mid(in_refs, out_refs, sems)
        rnd.finish(in_refs, out_refs, sems)

    return pl.pallas_call(
        body, name=name, in_specs=[ANY] * n_in, out_specs=[ANY] * n_out, out_shape=rnd.out_shapes,
        scratch_shapes=rnd.sems(), input_output_aliases={i: i for i in range(rnd.n_alias)})(*rnd.ins)


def _call(body, *, name, grid, in_specs, out_specs, out_shape, scratch_shapes, args, carry=None):
    params = _params(len(grid))
    if carry is None:
        outs = pl.pallas_call(body, name=name, grid=grid, in_specs=in_specs, out_specs=out_specs, out_shape=out_shape,
                              scratch_shapes=scratch_shapes, compiler_params=params)(*args)
        return list(outs), []
    n_ci, n_co, n_cs = len(in_specs), len(out_shape), len(scratch_shapes)
    n_xi, n_xo = len(carry.ins), len(carry.out_shapes)

    def wrapped(*refs):
        ci, xi = refs[:n_ci], refs[n_ci : n_ci + n_xi]
        o0 = n_ci + n_xi
        co, xo = refs[o0 : o0 + n_co], refs[o0 + n_co : o0 + n_co + n_xo]
        s0 = o0 + n_co + n_xo
        cs, sems = refs[s0 : s0 + n_cs], refs[s0 + n_cs :]
        ids = [pl.program_id(a) for a in range(len(grid))]
        first = functools.reduce(jnp.logical_and, [i == 0 for i in ids])
        last = functools.reduce(jnp.logical_and, [i == g - 1 for i, g in zip(ids, grid)])

        @pl.when(first)
        def _():
            carry.start(xi, xo, sems, links=("ici",))

        if carry.feeders:
            step = functools.reduce(lambda acc, ig: acc * ig[1] + ig[0], zip(ids, grid), 0)
            n_steps = functools.reduce(lambda a, b: a * b, grid)

            @pl.when(step == min(n_steps - 1, (3 * n_steps) // 5))
            def _():
                carry.mid(xi, xo, sems)

        body(*ci, *co, *cs)

        @pl.when(first)
        def _():
            carry.start(xi, xo, sems, links=("d2d",))

        @pl.when(last)
        def _():
            carry.finish(xi, xo, sems)

    outs = pl.pallas_call(
        wrapped, name=name, grid=grid, in_specs=list(in_specs) + [ANY] * n_xi, out_specs=list(out_specs) + [ANY] * n_xo,
        out_shape=list(out_shape) + carry.out_shapes, scratch_shapes=list(scratch_shapes) + carry.sems(),
        input_output_aliases={n_ci + i: n_co + i for i in range(carry.n_alias)}, compiler_params=params,
    )(*args, *carry.ins)
    return list(outs[:n_co]), list(outs[n_co:])


def _gather_direct(arrays):
    na = len(arrays)
    outs = [_sds((NDEV,) + a.shape, a.dtype) for a in arrays]
    plan = [(k, i, lambda s, r: None, i, lambda s, r: s) for i in range(na) for k in range(1, NDEV)]
    return _Round(arrays, outs, plan, [(i, lambda m: None, i, lambda m: m) for i in range(na)])


def _gather_a(arrays):
    na = len(arrays)
    outs = [_sds((NDEV,) + a.shape, a.dtype) for a in arrays]
    plan = [(k, i, lambda s, r: None, i, lambda s, r: s) for i in range(na) for k in (2, 4)]
    handed = lambda s, r: s ^ (2 << (s & 1))
    plan += [(FWD, None, handed, i, handed) for i in range(na)]
    return _Round(arrays, outs, plan, [(i, lambda m: None, i, lambda m: m) for i in range(na)])


def _gather_b(got):
    na = len(got)
    plan = [(1, i, (lambda s, r, k=k: s ^ k), i, (lambda s, r, k=k: s ^ k)) for i in range(na) for k in (0, 2, 4, 6)]
    return _Round(got, [_sds(g.shape, g.dtype) for g in got], plan, n_alias=na)


def _scatter_1(grads):
    plan = [(1, i, (lambda s, r, q=q: 2 * q + (r & 1)), i, (lambda s, r, q=q: q))
            for i in range(len(grads)) for q in range(4)]
    return _Round(grads, [_sds((4,) + g.shape[1:], g.dtype) for g in grads], plan)


def _scatter_2(chip):
    plan = [(k, i, lambda s, r: r >> 1, i, (lambda s, r, j=j: j)) for i in range(len(chip)) for j, k in enumerate((2, 4, 6))]
    return _Round(chip, [_sds((3,) + g.shape[1:], g.dtype) for g in chip], plan)


def _add_pairs(g, got, pos, *, name):
    _, sh, w = g.shape

    def body(pos_ref, g_ref, r_ref, o_ref):
        o_ref[...] = (g_ref[...].astype(F32) + r_ref[...].astype(F32)).astype(o_ref.dtype)

    return pl.pallas_call(
        body,
        name=name,
        grid_spec=pltpu.PrefetchScalarGridSpec(
            num_scalar_prefetch=1, grid=(4,),
            in_specs=[pl.BlockSpec((1, sh, w), lambda q, p: (2 * q + p[0], 0, 0)),
                      pl.BlockSpec((1, sh, w), lambda q, p: (q, 0, 0))],
            out_specs=pl.BlockSpec((1, sh, w), lambda q, p: (q, 0, 0))),
        out_shape=_sds((4, sh, w), g.dtype),
        compiler_params=_params(),
    )(pos, g, got)


def _sum_chips(chip, got, pos, *, transpose, name):
    _, sh, w = chip.shape
    out = (w, sh) if transpose else (sh, w)

    def body(pos_ref, c_ref, r_ref, o_ref):
        acc = c_ref[0].astype(F32)
        for s in range(3):
            acc = acc + r_ref[s].astype(F32)
        o_ref[...] = acc.T if transpose else acc

    return pl.pallas_call(
        body,
        name=name,
        grid_spec=pltpu.PrefetchScalarGridSpec(
            num_scalar_prefetch=1, grid=(1,),
            in_specs=[pl.BlockSpec((1, sh, w), lambda i, p: (p[1], 0, 0)), pl.BlockSpec((3, sh, w), lambda i, p: (0, 0, 0))],
            out_specs=pl.BlockSpec(out, lambda i, p: (0, 0))),
        out_shape=_sds(out, F32),
        compiler_params=_params(),
    )(pos, chip, got)


def _adamw_math(w, g, m, v):
    m2 = ADAM_B1 * m + (1.0 - ADAM_B1) * g
    v2 = ADAM_B2 * v + (1.0 - ADAM_B2) * (g * g)
    m_hat = m2 / (1.0 - ADAM_B1 ** ADAM_STEP)
    v_hat = v2 / (1.0 - ADAM_B2 ** ADAM_STEP)
    delta = -ADAM_LR * (m_hat / (jnp.sqrt(v_hat) + ADAM_EPS) + ADAM_WD * w)
    return delta, m2, v2


def _adamw(w, g, m, v, *, name, carry=None):
    shape = w.shape
    flat = [t.reshape(-1, shape[-1]) for t in (w, g, m, v)]
    rows, cols = flat[0].shape
    tr = rows // 8 if rows % 64 == 0 else rows
    spec = _rows(tr, cols)

    def body(w_ref, g_ref, m_ref, v_ref, d_ref, m2_ref, v2_ref):
        d_ref[...], m2_ref[...], v2_ref[...] = _adamw_math(w_ref[...], g_ref[...], m_ref[...], v_ref[...])

    outs, got = _call(body, name=name, grid=(rows // tr,), in_specs=[spec] * 4, out_specs=[spec] * 3,
                      out_shape=[_sds((rows, cols), F32)] * 3, scratch_shapes=[], args=flat, carry=carry)
    return tuple(o.reshape(shape) for o in outs), got


def _adds(tag, grads, got, *, pos):
    return [_add_pairs(g, r, pos, name=f"rs_add_{tag}_{i}") for i, (g, r) in enumerate(zip(grads, got))]


def _totals(tag, chip, got, transposed=(True, False), *, pos):
    return [_sum_chips(c_, r, pos, transpose=t, name=f"rs_sum_{tag}_{i}")
            for i, (c_, r, t) in enumerate(zip(chip, got, transposed))]


def _small_sums(packets, nf, dwp, dsc, dsk, *, name):
    flat = [p for layer in packets for p in layer]

    def total(ref, *idx):
        acc = ref[(0,) + idx]
        for dev in range(1, NDEV):
            acc = acc + ref[(dev,) + idx]
        return acc

    def body(*refs):
        pk = refs[:6]
        nf_ref, dwp0, dwp1, dsc0, dsc1, dsk0, dsk1 = refs[6:13]
        dm_ref, gb_ref, gn_ref, gnf_ref, gwp_ref, gps_ref, gsk_ref = refs[13:]
        dm_ref[...] = jnp.zeros_like(dm_ref)
        gn_ref[...] = jnp.zeros_like(gn_ref)
        for l in range(2):
            for sb in range(3):
                p = pk[3 * l + sb]
                for r in range(3):
                    col = slice((3 * sb + r) * D, (3 * sb + r + 1) * D)
                    lat = p[0, 0, r : r + 1, :]
                    dm_ref[l, 0:1, col] = lat
                    for dev in range(1, NDEV):
                        row = p[dev, 0, r : r + 1, :]
                        dm_ref[l, dev : dev + 1, col] = row
                        lat = lat + row
                    ctx = total(p, 1, slice(r, r + 1), slice(None))
                    dm_ref[l, 8:9, col] = ctx
                    gb_ref[l : l + 1, col] = lat + ctx
                gn_ref[l, sb : sb + 1, :] = total(p, 0, slice(3, 4), slice(None)) + total(p, 1, slice(3, 4), slice(None))
        gnf_ref[...] = total(nf_ref, slice(0, 1), slice(None))
        for l, (a, b, c) in enumerate(((dwp0, dsc0, dsk0), (dwp1, dsc1, dsk1))):
            gwp_ref[l] = total(a, slice(None), slice(None))
            gps_ref[l : l + 1, :] = total(b, slice(0, 1), slice(None))
            gsk_ref[l] = total(c, slice(None), slice(None))

    ins = flat + [nf, dwp[0], dwp[1], dsc[0], dsc[1], dsk[0], dsk[1]]
    return pl.pallas_call(
        body,
        name=name,
        out_shape=[_sds((2, 16, NMOD * D), F32), _sds((2, NMOD * D), F32), _sds((2, 8, D), F32), _sds((1, D), F32),
                   _sds((2, PW, 128), F32), _sds((2, PW), F32), _sds((2, 8, 128), F32)],
        compiler_params=pltpu.CompilerParams(vmem_limit_bytes=VMEM_LIMIT),
    )(*ins)


def _small_adamw(c_ctx, dc_all, triples, *, name):
    n = len(triples)

    def body(*refs):
        c_ref, dc_ref = refs[0], refs[1]
        ins = refs[2 : 2 + 4 * n - 1]
        outs = refs[2 + 4 * n - 1 :]
        acc = dc_ref[0, 0, 8:9, :] + dc_ref[0, 1, 8:9, :]
        for dev in range(1, NDEV):
            acc = acc + (dc_ref[dev, 0, 8:9, :] + dc_ref[dev, 1, 8:9, :])
        c = c_ref[...]
        sig = _sigmoid(c)
        g_c = acc * (sig * (1.0 + c * (1.0 - sig)))
        outs[0][...] = g_c
        pos = 0
        for k in range(n):
            if k == 0:
                w, g, m, v = ins[0][...], g_c, ins[1][...], ins[2][...]
                pos = 3
            else:
                w, g, m, v = (ins[pos + t][...] for t in range(4))
                pos += 4
            d, m2, v2 = _adamw_math(w, g, m, v)
            outs[1 + 3 * k][...], outs[2 + 3 * k][...], outs[3 + 3 * k][...] = d, m2, v2

    flat_in = [c_ctx, dc_all]
    out_shape = [_sds(c_ctx.shape, F32)]
    for k, (w, g, m, v) in enumerate(triples):
        flat_in += [w, m, v] if k == 0 else [w, g, m, v]
        out_shape += [_sds(w.shape, F32)] * 3
    return pl.pallas_call(body, name=name, out_shape=out_shape,
                          compiler_params=pltpu.CompilerParams(vmem_limit_bytes=VMEM_LIMIT))(*flat_in)


def _rope_tables(T, R):
    t = jnp.arange(T)
    inv = ROPE_BASE ** (-jnp.arange(0, HD // 2, 2, dtype=F32) / (HD // 2))
    ang = jnp.concatenate([(t // GRID_W).astype(F32)[:, None] * inv, (t % GRID_W).astype(F32)[:, None] * inv], axis=-1)
    cos = jnp.concatenate([jnp.tile(jnp.cos(ang), (1, 4)), jnp.ones((R - T, 128), F32)], axis=0)
    sin = jnp.concatenate([jnp.tile(jnp.sin(ang), (1, 4)), jnp.zeros((R - T, 128), F32)], axis=0)
    return cos, sin


def kernel(x, c, ctx, c_ctx, w_mod, b_mod, norm_ffn1, w_ffn1_in, w_ffn1_out, norm_mix, w_in, w_pool, pool_scale, sink, w_out, norm_ffn2, w_ffn2_in, w_ffn2_out, norm_final, loss_target, m_c_ctx, m_w_mod, m_b_mod, m_norm_ffn1, m_w_ffn1_in, m_w_ffn1_out, m_norm_mix, m_w_in, m_w_pool, m_pool_scale, m_sink, m_w_out, m_norm_ffn2, m_w_ffn2_in, m_w_ffn2_out, m_norm_final, v_c_ctx, v_w_mod, v_b_mod, v_norm_ffn1, v_w_ffn1_in, v_w_ffn1_out, v_norm_mix, v_w_in, v_w_pool, v_pool_scale, v_sink, v_w_out, v_norm_ffn2, v_w_ffn2_in, v_w_ffn2_out, v_norm_final):
    T = x.shape[1]
    R = T + LC
    nl = w_mod.shape[0]
    cx, cy, cc = _coords()
    me = _lin((cx, cy, cc))
    pos = jnp.stack([cc, 2 * cx + cy]).astype(jnp.int32)
    mcols = w_mod.shape[2]

    shards = [([w_ffn1_in[l].T.astype(BF16), w_ffn1_out[l].astype(BF16)],
               [w_in[l].T.astype(BF16), w_out[l].astype(BF16)],
               [w_ffn2_in[l].T.astype(BF16), w_ffn2_out[l].astype(BF16)]) for l in range(nl)]

    got = _exchange("ag_c_w", _merge(_gather_direct([c]), _gather_a(shards[0][0] + shards[0][1])))
    c_all, w_first = got[0], got[1:]
    c16 = jnp.concatenate([c_all.reshape(NDEV, D), c_ctx[None], jnp.zeros((16 - NDEV - 1, D), F32)], axis=0)
    b_cols = lax.dynamic_slice(b_mod, (0, me * mcols), (nl, mcols)).reshape(nl, 1, mcols)
    got = _exchange("ag_mod_w", _merge(_gather_b(w_first), _gather_direct([_mod_fwd(c16, w_mod, b_cols, name="mod_fwd")])))
    w_first, mod_all = got[:4], got[4]
    mod_all = jnp.transpose(mod_all, (1, 2, 0, 3)).reshape(nl, 16, NMOD, D)
    mine = lax.dynamic_index_in_dim(mod_all, me, axis=1, keepdims=False)
    pad = jnp.zeros((nl, 16 - NMOD, D), F32)
    modv = jnp.stack([jnp.concatenate([mine, pad], axis=1), jnp.concatenate([mod_all[:, 8], pad], axis=1)], axis=1)

    gvec = [jnp.concatenate([norm_ffn1[l][None], norm_mix[l][None], norm_ffn2[l][None], jnp.zeros((5, D), F32)], axis=0)
            for l in range(nl)]
    cos, sin = _rope_tables(T, R)
    ps2 = [pool_scale[l][None] for l in range(nl)]

    h = jnp.concatenate([x[0], ctx[0]], axis=0)
    loss_all, dh, small, nf_all, big, last_partials = _forward_backward(
        h, loss_target[0], modv, gvec, shards, w_first, cos, sin, sink, w_pool, ps2, norm_final, pos, T=T)
    loss = jnp.sum(loss_all[:, 0, 0])
    grad_x = dh[:T][None]

    dm, g_b_mod, g_norms, g_nf, g_wp, g_ps, g_sk = _small_sums(
        [small[l][0:3] for l in range(nl)], nf_all, *[[small[l][k] for l in range(nl)] for k in (3, 4, 5)],
        name="small_sums")
    dm_cols = lax.dynamic_slice(dm, (0, 0, me * mcols), (nl, 16, mcols))
    g_w_mod, dc_part = _mod_bwd(c16, dm_cols, w_mod, name="mod_bwd")
    got = _exchange("rs1_tail", _merge(_scatter_1(last_partials), _gather_direct([dc_part])))
    c1, dc_all = _adds("ffn1_0", last_partials, got[:2], pos=pos), got[2]

    delta, new_m, new_v = {}, {}, {}
    (delta["w_mod"], new_m["w_mod"], new_v["w_mod"]), got = _adamw(
        w_mod, g_w_mod, m_w_mod, v_w_mod, name="adamw_w_mod", carry=_scatter_2(c1[:1]))
    big[0][0:1] = _totals("ffn1_in_0", c1[:1], got, pos=pos)
    g_w_ffn2_in = jnp.stack([big[l][4] for l in range(nl)])
    (delta["w_ffn2_in"], new_m["w_ffn2_in"], new_v["w_ffn2_in"]), got = _adamw(
        w_ffn2_in, g_w_ffn2_in, m_w_ffn2_in, v_w_ffn2_in, name="adamw_w_ffn2_in", carry=_scatter_2(c1[1:]))
    big[0][1:2] = _totals("ffn1_out_0", c1[1:], got, transposed=(False,), pos=pos)

    grads = {
        "b_mod": g_b_mod, "norm_ffn1": g_norms[:, 0], "norm_mix": g_norms[:, 1], "norm_ffn2": g_norms[:, 2],
        "w_pool": g_wp.reshape(w_pool.shape), "pool_scale": g_ps, "sink": g_sk[:, :, 0], "norm_final": g_nf.reshape(D),
        "w_mod": g_w_mod,
        "w_ffn1_in": jnp.stack([big[l][0] for l in range(nl)]), "w_ffn1_out": jnp.stack([big[l][1] for l in range(nl)]),
        "w_in": jnp.stack([big[l][2] for l in range(nl)]), "w_out": jnp.stack([big[l][3] for l in range(nl)]),
        "w_ffn2_in": g_w_ffn2_in, "w_ffn2_out": jnp.stack([big[l][5] for l in range(nl)]),
    }
    weights = dict(c_ctx=c_ctx, w_mod=w_mod, b_mod=b_mod, norm_ffn1=norm_ffn1, w_ffn1_in=w_ffn1_in, w_ffn1_out=w_ffn1_out,
                   norm_mix=norm_mix, w_in=w_in, w_pool=w_pool, pool_scale=pool_scale, sink=sink, w_out=w_out,
                   norm_ffn2=norm_ffn2, w_ffn2_in=w_ffn2_in, w_ffn2_out=w_ffn2_out, norm_final=norm_final)
    moms = dict(c_ctx=(m_c_ctx, v_c_ctx), w_mod=(m_w_mod, v_w_mod), b_mod=(m_b_mod, v_b_mod),
                norm_ffn1=(m_norm_ffn1, v_norm_ffn1), w_ffn1_in=(m_w_ffn1_in, v_w_ffn1_in),
                w_ffn1_out=(m_w_ffn1_out, v_w_ffn1_out), norm_mix=(m_norm_mix, v_norm_mix), w_in=(m_w_in, v_w_in),
                w_pool=(m_w_pool, v_w_pool), pool_scale=(m_pool_scale, v_pool_scale), sink=(m_sink, v_sink),
                w_out=(m_w_out, v_w_out), norm_ffn2=(m_norm_ffn2, v_norm_ffn2), w_ffn2_in=(m_w_ffn2_in, v_w_ffn2_in),
                w_ffn2_out=(m_w_ffn2_out, v_w_ffn2_out), norm_final=(m_norm_final, v_norm_final))
    order = list(weights)
    small_names = ["c_ctx", "b_mod", "norm_ffn1", "norm_mix", "w_pool", "pool_scale", "sink", "norm_ffn2", "norm_final"]

    def as2d(name, t):
        if name == "w_pool":
            return t.reshape(-1, 128)
        return t.reshape(1, -1) if t.ndim == 1 else t

    triples = [(as2d(n, weights[n]), None if n == "c_ctx" else as2d(n, grads[n]), as2d(n, moms[n][0]), as2d(n, moms[n][1]))
               for n in small_names]
    outs = _small_adamw(as2d("c_ctx", c_ctx), dc_all, triples, name="small_adamw")
    grads["c_ctx"] = outs[0].reshape(c_ctx.shape)
    for k, n in enumerate(small_names):
        delta[n], new_m[n], new_v[n] = (o.reshape(weights[n].shape) for o in outs[1 + 3 * k : 4 + 3 * k])
    for n in order:
        if n not in delta:
            (delta[n], new_m[n], new_v[n]), _ = _adamw(weights[n], grads[n], moms[n][0], moms[n][1], name=f"adamw_{n}")

    return (loss, grad_x, *[grads[n] for n in order], *[delta[n] for n in order],
            *[new_m[n] for n in order], *[new_v[n] for n in order])


def _merge(*rounds):
    ins, outs, plan, local, n_alias = [], [], [], [], 0
    for r in rounds:
        assert r.n_alias == 0 or (not ins and r.n_alias == len(r.ins) == len(r.out_shapes))
        oi, oo = len(ins), len(outs)
        plan += [(k, None if i is None else i + oi, sf, o + oo, df) for k, i, sf, o, df in r.plan]
        local += [(i + oi, sf, o + oo, df) for i, sf, o, df in r.local_plan]
        ins += r.ins
        outs += r.out_shapes
        n_alias += r.n_alias
    return _Round(ins, outs, plan, local, n_alias)


def _forward_backward(h, target, modv, gvec, shards, w_first, cos, sin, sink, w_pool, ps2, norm_final, pos, *, T):
    nl = len(gvec)
    flat = lambda ws: [w.reshape(-1, D) for w in ws]
    saved = []
    w1, wm = flat(w_first[:2]), flat(w_first[2:])
    for l in range(nl):
        last = l == nl - 1
        h0 = h
        if l == 0:
            (h1, a1, b1, f1), got = _ffn_fwd(h0, modv[l], gvec[l], *w1, T=T, mrow=0, grow=0, ctx_active=True,
                                             name=f"ffn1_fwd_{l}", carry=_gather_a(shards[l][2]))
            (u, q, k4, v4), got = _mixproj_fwd(h1, modv[l], gvec[l], wm[0], cos, sin, T=T, name=f"mixproj_fwd_{l}",
                                               carry=_gather_b(got))
            w2 = flat(got)
        else:
            (h1, a1, b1, f1), got = _ffn_fwd(h0, modv[l], gvec[l], *w1, T=T, mrow=0, grow=0, ctx_active=True,
                                             name=f"ffn1_fwd_{l}", carry=_gather_b(nxt_m + nxt_2))
            wm, w2 = flat(got[:2]), flat(got[2:])
            (u, q, k4, v4), _ = _mixproj_fwd(h1, modv[l], gvec[l], wm[0], cos, sin, T=T, name=f"mixproj_fwd_{l}")
        (cat,), nxt_1 = _attnpool_fwd(u, q, k4, v4, sink[l], w_pool[l], ps2[l], T=T, name=f"attnpool_fwd_{l}",
                                      carry=None if last else _gather_a(shards[l + 1][0]))
        (h2, mo), nxt_m = _mixout_fwd(h1, cat, modv[l], wm[1], T=T, ctx_active=not last, name=f"mixout_fwd_{l}",
                                      carry=None if last else _gather_a(shards[l + 1][1]))
        (h3, a2, b2, f2), got = _ffn_fwd(h2, modv[l], gvec[l], *w2, T=T, mrow=6, grow=2, ctx_active=not last,
                                         name=f"ffn2_fwd_{l}",
                                         carry=None if last else _merge(_gather_b(nxt_1), _gather_a(shards[l + 1][2])))
        saved.append((h0, a1, b1, f1, h1, u, q, k4, v4, cat, mo, h2, a2, b2, f2, w1, wm, w2))
        h = h3
        if not last:
            w1, nxt_2 = flat(got[:2]), got[2:]

    dh, loss_part, dnf = _loss_head(h, target, norm_final[None], T=T, name="loss_head")

    adds = functools.partial(_adds, pos=pos)
    totals = functools.partial(_totals, pos=pos)
    small, big = [None] * nl, {}
    prev = None
    for l in reversed(range(nl)):
        last = l == nl - 1
        h0, a1, b1, f1, h1, u, q, k4, v4, cat, mo, h2, a2, b2, f2, w1, wm, w2 = saved[l]
        (dh, dab, s, n, df, pk2), got = _ffn_bwd(
            h2, dh, a2, b2, f2, modv[l], gvec[l], *w2, T=T, mrow=6, grow=2, ctx_active=not last, name=f"ffn2_bwd_{l}",
            carry=_merge(_scatter_1(prev[0]), _gather_a(prev[1])) if prev else None)
        if prev:
            c1, small_a = adds(f"ffn1_{l + 1}", prev[0], got[:2]), got[2:]
        g_w2i, got = _wgrad(dab, n, bk=WG_BK, sh=2 * DFF // NDEV, name=f"wgrad_ffn2_in_{l}",
                            carry=_scatter_2(c1[:1]) if prev else None)
        if prev:
            big[l + 1][0:1] = totals(f"ffn1_in_{l + 1}", c1[:1], got)
        g_w2o, got = _wgrad(s, df, bk=WG_BK, sh=DFF // NDEV, name=f"wgrad_ffn2_out_{l}",
                            carry=_scatter_2(c1[1:]) if prev else None)
        if prev:
            big[l + 1][1:2] = totals(f"ffn1_out_{l + 1}", c1[1:], got, transposed=(False,))
        rnd = _scatter_1([g_w2i, g_w2o])
        (dcat, dmix, pko), got = _mixout_bwd(dh, mo, modv[l], wm[1], T=T, ctx_active=not last, name=f"mixout_bwd_{l}",
                                             carry=_merge(_gather_b(small_a), rnd) if prev else rnd)
        if prev:
            small[l + 1], got = got[: len(small_a)], got[len(small_a) :]
        c2 = adds(f"ffn2_{l}", [g_w2i, g_w2o], got)
        g_wo, _ = _wgrad(cat, dmix, bk=D, sh=D // NDEV, name=f"wgrad_out_{l}")
        dps, dwp, dsc = _pool_bwd(u, dcat, w_pool[l], ps2[l], T=T, name=f"pool_bwd_{l}")
        (du, dq, dk, dv, dsk), got = _attn_bwd(q, k4, v4, dcat, dps, sink[l], T=T, name=f"attn_bwd_{l}", carry=_scatter_2(c2))
        big[l] = [None, None, None, None] + totals(f"ffn2_{l}", c2, got)
        dh, dproj, n, pkm = _mixproj_bwd(h1, dh, du, dq, dk, dv, modv[l], gvec[l], wm[0], cos, sin, T=T, name=f"mixproj_bwd_{l}")
        g_wi, _ = _wgrad(dproj, n, bk=PROJ, sh=PROJ // NDEV, name=f"wgrad_in_{l}")
        (dh, dab, s, n, df, pk1), got = _ffn_bwd(h0, dh, a1, b1, f1, modv[l], gvec[l], *w1, T=T, mrow=0, grow=0,
                                                 ctx_active=True, name=f"ffn1_bwd_{l}", carry=_scatter_1([g_wi, g_wo]))
        cm = adds(f"mix_{l}", [g_wi, g_wo], got)
        mine = [pk1, pkm + pko, pk2, dwp, dsc, dsk]
        rnd = _merge(_scatter_2(cm), _gather_a(mine + [dnf, loss_part])) if l == 0 else _scatter_2(cm)
        g_w1i, got = _wgrad(dab, n, bk=WG_BK, sh=2 * DFF // NDEV, name=f"wgrad_ffn1_in_{l}", carry=rnd)
        big[l][2:4] = totals(f"mix_{l}", cm, got[:2])
        g_w1o, got = _wgrad(s, df, bk=WG_BK, sh=DFF // NDEV, name=f"wgrad_ffn1_out_{l}",
                            carry=_gather_b(got[2:]) if l == 0 else None)
        prev = ([g_w1i, g_w1o], mine)
    small[0], nf_all, loss_all = got[:6], got[6], got[7]
    return loss_all, dh, small, nf_all, big, prev[0]
```

```python
import functools

import jax
import jax.numpy as jnp
from jax import lax
from jax.experimental import pallas as pl
from jax.experimental.pallas import tpu as pltpu

F32, BF16 = jnp.float32, jnp.bfloat16

D = 1024
LC = 256
DFF = 2816
NMOD = 9
PW = 512
AW = 512
KVW = 128
PROJ = PW + AW + 2 * KVW
HD = 64
BLK = 128
GRID_W = 64
POOL_WINDOWS = (2, 4, 8, 16)
EPS = 1e-6
NEG = -1e30
ROPE_BASE = 10000.0
NDEV = 8
MESH = pl.DeviceIdType.MESH

ADAM_LR, ADAM_B1, ADAM_B2, ADAM_EPS, ADAM_WD, ADAM_STEP = 0.001, 0.9, 0.999, 1e-08, 0.01, 10

VMEM_LIMIT = 56 * 1024 * 1024
TM = 256
FFN_CHUNKS = ((0, 1536), (1536, 1280))
WG_BK = 1408

ANY = pl.BlockSpec(memory_space=pl.ANY)
SMEM = pl.BlockSpec(memory_space=pltpu.SMEM)


def _params(ngrid=1):
    return pltpu.CompilerParams(dimension_semantics=("arbitrary",) * ngrid, vmem_limit_bytes=VMEM_LIMIT)


def _dot(a, b):
    return jnp.dot(a, b, preferred_element_type=F32)


def _dot_nt(a, b):
    return lax.dot_general(a, b, (((1,), (1,)), ((), ())), preferred_element_type=F32)


def _dot_tn(a, b):
    return lax.dot_general(a, b, (((0,), (0,)), ((), ())), preferred_element_type=F32)


def _sigmoid(x):
    return 1.0 / (1.0 + jnp.exp(-x))


def _rows(tm, w):
    return pl.BlockSpec((tm, w), lambda i: (i, 0))


def _full(shape):
    nd = len(shape)
    return pl.BlockSpec(shape, lambda *_: (0,) * nd)


def _sds(shape, dtype):
    return jax.ShapeDtypeStruct(shape, dtype)


def _norm_mod(h, g, shift, scale):
    r = lax.rsqrt(jnp.mean(h * h, axis=-1, keepdims=True) + EPS)
    xhat = h * r
    y = xhat * g
    return r, xhat, y, y * (1.0 + scale) + shift


def _norm_mod_bwd(dn, r, xhat, y, g, scale):
    dshift = jnp.sum(dn, axis=0, keepdims=True)
    dscale = jnp.sum(dn * y, axis=0, keepdims=True)
    dy = dn * (1.0 + scale)
    dg = jnp.sum(dy * xhat, axis=0, keepdims=True)
    dxh = dy * g
    dh = r * (dxh - xhat * jnp.mean(dxh * xhat, axis=-1, keepdims=True))
    return dh, dshift, dscale, dg


def _acc_partials(part_ref, first, rows):
    @pl.when(first)
    def _():
        part_ref[...] = jnp.zeros_like(part_ref)

    for r, val in rows.items():
        part_ref[0, r : r + 1, :] += val


def _mod_spec(n_lat):
    return pl.BlockSpec((1, 16, D), lambda i: (i // n_lat, 0, 0))


def _part_spec(n_lat):
    return pl.BlockSpec((1, 8, D), lambda i: (i // n_lat, 0, 0))


def _load_weights(pairs, sem):
    copies = [pltpu.make_async_copy(src, dst, sem.at[k]) for k, (src, dst) in enumerate(pairs)]
    for cp in copies:
        cp.start()
    for cp in copies:
        cp.wait()


def _ffn_weight_copies(win_hbm, wout_hbm, win_v, wout_v, sem):
    loads = []
    for k, (c0, cw) in enumerate(FFN_CHUNKS):
        slabs = [(win_hbm, win_v, c0), (win_hbm, win_v, DFF + c0), (wout_hbm, wout_v, c0)]
        loads.append([pltpu.make_async_copy(src.at[pl.ds(r0, cw)], dst.at[pl.ds(r0, cw)], sem.at[3 * k + j])
                      for j, (src, dst, r0) in enumerate(slabs)])
    return loads


def _ffn_steps(i, n_active, loads, compute):
    @pl.when(i == 0)
    def _():
        for cp in sum(loads, []):
            cp.start()
        compute(loads)

    @pl.when(jnp.logical_and(i > 0, i < n_active))
    def _():
        compute(None)


def _wait_chunk(loads, k):
    if loads is not None:
        for cp in loads[k]:
            cp.wait()


def _ffn_fwd(h, modv, gvec, win, wout, *, T, mrow, grow, ctx_active, name, carry=None):
    R = h.shape[0]
    n_lat, n_tiles = T // TM, R // TM
    n_active = n_tiles if ctx_active else n_lat

    def body(h_ref, mod_ref, g_ref, win_hbm, wout_hbm, ho_ref, a_ref, b_ref, f_ref, win_v, wout_v, sem):
        i = pl.program_id(0)

        def compute(loads):
            h = h_ref[...]
            shift, scale, gate = (mod_ref[0, mrow + k : mrow + k + 1, :] for k in range(3))
            _, _, _, n = _norm_mod(h, g_ref[grow : grow + 1, :], shift, scale)
            n_bf = n.astype(BF16)
            acc = jnp.zeros((TM, D), F32)
            for k, (c0, cw) in enumerate(FFN_CHUNKS):
                _wait_chunk(loads, k)
                a = _dot_nt(n_bf, win_v[c0 : c0 + cw, :])
                b = _dot_nt(n_bf, win_v[DFF + c0 : DFF + c0 + cw, :])
                a_ref[:, c0 : c0 + cw] = a.astype(BF16)
                b_ref[:, c0 : c0 + cw] = b.astype(BF16)
                s = a * _sigmoid(a) * b
                acc = acc + _dot(s.astype(BF16), wout_v[c0 : c0 + cw, :])
            f_ref[...] = acc.astype(BF16)
            ho_ref[...] = h + (0.5 * gate) * acc

        _ffn_steps(i, n_active, _ffn_weight_copies(win_hbm, wout_hbm, win_v, wout_v, sem), compute)

        @pl.when(i >= n_active)
        def _():
            ho_ref[...] = h_ref[...]
            a_ref[...] = jnp.zeros_like(a_ref)
            b_ref[...] = jnp.zeros_like(b_ref)
            f_ref[...] = jnp.zeros_like(f_ref)

    return _call(
        body,
        name=name,
        grid=(n_tiles,),
        in_specs=[_rows(TM, D), _mod_spec(n_lat), _full((8, D)), ANY, ANY],
        out_specs=[_rows(TM, D), _rows(TM, DFF), _rows(TM, DFF), _rows(TM, D)],
        out_shape=[_sds((R, D), F32), _sds((R, DFF), BF16), _sds((R, DFF), BF16), _sds((R, D), BF16)],
        scratch_shapes=[pltpu.VMEM((2 * DFF, D), BF16), pltpu.VMEM((DFF, D), BF16),
                        pltpu.SemaphoreType.DMA((3 * len(FFN_CHUNKS),))],
        args=(h, modv, gvec, win, wout),
        carry=carry,
    )


def _ffn_bwd(h, dho, a, b, f, modv, gvec, win, wout, *, T, mrow, grow, ctx_active, name, carry=None):
    R = h.shape[0]
    n_lat, n_tiles = T // TM, R // TM
    n_active = n_tiles if ctx_active else n_lat

    def body(h_ref, dho_ref, a_ref, b_ref, f_ref, mod_ref, g_ref, win_hbm, wout_hbm,
             dh_ref, dab_ref, s_ref, n_ref, df_ref, part_ref, win_v, wout_v, sem):
        i = pl.program_id(0)
        first = jnp.logical_or(i == 0, i == n_lat)

        def compute(loads):
            h = h_ref[...]
            dho = dho_ref[...]
            shift, scale, gate = (mod_ref[0, mrow + k : mrow + k + 1, :] for k in range(3))
            g = g_ref[grow : grow + 1, :]
            r, xhat, y, n = _norm_mod(h, g, shift, scale)
            dgate = 0.5 * jnp.sum(dho * f_ref[...].astype(F32), axis=0, keepdims=True)
            df_bf = ((0.5 * gate) * dho).astype(BF16)
            df_ref[...] = df_bf
            n_ref[...] = n.astype(BF16)
            dn = jnp.zeros((TM, D), F32)
            for k, (c0, cw) in enumerate(FFN_CHUNKS):
                _wait_chunk(loads, k)
                ds = _dot_nt(df_bf, wout_v[c0 : c0 + cw, :])
                av = a_ref[:, c0 : c0 + cw].astype(F32)
                bv = b_ref[:, c0 : c0 + cw].astype(F32)
                sig = _sigmoid(av)
                sa = av * sig
                s_ref[:, c0 : c0 + cw] = (sa * bv).astype(BF16)
                da = (ds * bv * (sig * (1.0 + av * (1.0 - sig)))).astype(BF16)
                db = (ds * sa).astype(BF16)
                dab_ref[:, c0 : c0 + cw] = da
                dab_ref[:, DFF + c0 : DFF + c0 + cw] = db
                dn = dn + _dot(da, win_v[c0 : c0 + cw, :]) + _dot(db, win_v[DFF + c0 : DFF + c0 + cw, :])
            dh, dshift, dscale, dg = _norm_mod_bwd(dn, r, xhat, y, g, scale)
            dh_ref[...] = dho + dh
            _acc_partials(part_ref, first, {0: dshift, 1: dscale, 2: dgate, 3: dg})

        _ffn_steps(i, n_active, _ffn_weight_copies(win_hbm, wout_hbm, win_v, wout_v, sem), compute)

        @pl.when(i >= n_active)
        def _():
            dh_ref[...] = dho_ref[...]
            dab_ref[...] = jnp.zeros_like(dab_ref)
            s_ref[...] = jnp.zeros_like(s_ref)
            n_ref[...] = jnp.zeros_like(n_ref)
            df_ref[...] = jnp.zeros_like(df_ref)
            part_ref[...] = jnp.zeros_like(part_ref)

    return _call(
        body,
        name=name,
        grid=(n_tiles,),
        in_specs=[_rows(TM, D), _rows(TM, D), _rows(TM, DFF), _rows(TM, DFF), _rows(TM, D),
                  _mod_spec(n_lat), _full((8, D)), ANY, ANY],
        out_specs=[_rows(TM, D), _rows(TM, 2 * DFF), _rows(TM, DFF), _rows(TM, D), _rows(TM, D), _part_spec(n_lat)],
        out_shape=[_sds((R, D), F32), _sds((R, 2 * DFF), BF16), _sds((R, DFF), BF16), _sds((R, D), BF16),
                   _sds((R, D), BF16), _sds((2, 8, D), F32)],
        scratch_shapes=[pltpu.VMEM((2 * DFF, D), BF16), pltpu.VMEM((DFF, D), BF16),
                        pltpu.SemaphoreType.DMA((3 * len(FFN_CHUNKS),))],
        args=(h, dho, a, b, f, modv, gvec, win, wout),
        carry=carry,
    )


def _wgrad(x, y, *, bk, sh, name, carry=None):
    R, kx = x.shape
    n = y.shape[1]
    tr = R // 2
    nr, nsh = R // tr, bk // sh

    def body(x_ref, y_ref, o_ref, acc):
        r = pl.program_id(1)

        @pl.when(r == 0)
        def _():
            acc[...] = jnp.zeros_like(acc)

        acc[...] += _dot_tn(x_ref[...], y_ref[...])

        @pl.when(r == nr - 1)
        def _():
            for s in range(nsh):
                o_ref[s] = acc[s * sh : (s + 1) * sh, :].astype(BF16)

    (out,), got = _call(
        body,
        name=name,
        grid=(kx // bk, nr),
        in_specs=[pl.BlockSpec((tr, bk), lambda k, r: (r, k)), pl.BlockSpec((tr, n), lambda k, r: (r, 0))],
        out_specs=[pl.BlockSpec((nsh, sh, n), lambda k, r: (k, 0, 0))],
        out_shape=[_sds((kx // sh, sh, n), BF16)],
        scratch_shapes=[pltpu.VMEM((bk, n), F32)],
        args=(x, y),
        carry=carry,
    )
    return out, got


def _rot_half(x):
    lane = lax.broadcasted_iota(jnp.int32, x.shape, 1)
    return jnp.where((lane & (HD - 1)) < HD // 2, -pltpu.roll(x, 128 - HD // 2, 1), pltpu.roll(x, HD // 2, 1))


def _tile_sel():
    i = lax.broadcasted_iota(jnp.int32, (KVW, AW), 0)
    j = lax.broadcasted_iota(jnp.int32, (KVW, AW), 1)
    return jnp.where(i == (j // 256) * HD + (j & (HD - 1)), 1.0, 0.0).astype(BF16)


def _mixproj_fwd(h, modv, gvec, win, cos, sin, *, T, name, carry=None):
    R = h.shape[0]
    n_lat, n_tiles = T // TM, R // TM

    def body(h_ref, mod_ref, g_ref, win_ref, cos_ref, sin_ref, u_ref, q_ref, k4_ref, v4_ref):
        shift, scale = mod_ref[0, 3:4, :], mod_ref[0, 4:5, :]
        _, _, _, n = _norm_mod(h_ref[...], g_ref[1:2, :], shift, scale)
        proj = _dot_nt(n.astype(BF16), win_ref[...])
        u_ref[...] = proj[:, :PW]
        cs, sn = cos_ref[...], sin_ref[...]
        for s in range(AW // 128):
            x = proj[:, PW + 128 * s : PW + 128 * (s + 1)]
            q_ref[:, 128 * s : 128 * (s + 1)] = ((x * cs + _rot_half(x) * sn) * (HD ** -0.5)).astype(BF16)
        k = proj[:, PW + AW : PW + AW + KVW]
        k = (k * cs + _rot_half(k) * sn).astype(BF16)
        v = proj[:, PW + AW + KVW :].astype(BF16)
        sel = _tile_sel()
        k4_ref[...] = _dot(k, sel).astype(BF16)
        v4_ref[...] = _dot(v, sel).astype(BF16)

    return _call(
        body,
        name=name,
        grid=(n_tiles,),
        in_specs=[_rows(TM, D), _mod_spec(n_lat), _full((8, D)), _full((PROJ, D)), _rows(TM, 128), _rows(TM, 128)],
        out_specs=[_rows(TM, PW), _rows(TM, AW), _rows(TM, AW), _rows(TM, AW)],
        out_shape=[_sds((R, PW), F32), _sds((R, AW), BF16), _sds((R, AW), BF16), _sds((R, AW), BF16)],
        scratch_shapes=[],
        args=(h, modv, gvec, win, cos, sin),
        carry=carry,
    )


def _win_start(j, hi):
    return pl.multiple_of(jnp.clip((j - 1) * BLK, 0, hi - 3 * BLK), BLK)


def _hi_lo(x):
    hi = x.astype(BF16)
    return hi, (x - hi.astype(F32)).astype(BF16)


def _pool_bounds(t, w, T, R):
    is_ctx = t >= T
    lo = jnp.maximum(t - w // 2, jnp.where(is_ctx, T, 0))
    hi = jnp.minimum(t + w // 2, jnp.where(is_ctx, R, T))
    return lo, hi


def _pooled(u_v, j, T, R):
    start = _win_start(j, R)
    u3_hi, u3_lo = _hi_lo(u_v[pl.ds(start, 3 * BLK), :])
    ub = u_v[pl.ds(pl.multiple_of(j * BLK, BLK), BLK), :]
    t = j * BLK + lax.broadcasted_iota(jnp.int32, (BLK, 1), 0)
    pos = start + lax.broadcasted_iota(jnp.int32, (1, 3 * BLK), 1)
    pooled, counts = [], []
    for g, w in enumerate(POOL_WINDOWS):
        lo, hi = _pool_bounds(t, w, T, R)
        band = jnp.where(pos >= lo, jnp.where(pos < hi, 1.0, 0.0), 0.0).astype(BF16)
        sl = slice(g * 128, (g + 1) * 128)
        sums = _dot(band, u3_hi[:, sl]) + _dot(band, u3_lo[:, sl])
        cnt = (hi - lo).astype(F32)
        pooled.append(sums / cnt - ub[:, sl])
        counts.append(cnt)
    return pooled, counts


def _stack_heads(x):
    lane_h = lax.broadcasted_iota(jnp.int32, x.shape, 1) // HD
    return jnp.concatenate([jnp.where(lane_h == h, x, jnp.zeros_like(x)) for h in range(4)], axis=0)


def _unstack_heads(x):
    lane_h = lax.broadcasted_iota(jnp.int32, (BLK, 256), 1) // HD
    out = jnp.zeros((BLK, 256), F32)
    for h in range(4):
        out = out + jnp.where(lane_h == h, x[h * BLK : (h + 1) * BLK, :], 0.0)
    return out


def _window_mask(j, start_l, nbl):
    rowi = lax.broadcasted_iota(jnp.int32, (4 * BLK, 1), 0)
    qpos = j * BLK + (rowi & (BLK - 1))
    kpos = start_l + lax.broadcasted_iota(jnp.int32, (1, 3 * BLK), 1)
    reach = jnp.where(j < nbl, BLK, -1)
    return jnp.abs(kpos - qpos) <= reach


def _attn_exps(qs, kl, kc, sink_ref, g, valid):
    s_l = jnp.where(valid, _dot_nt(qs, kl), NEG)
    s_c = _dot_nt(qs, kc)
    rb = lax.broadcasted_iota(jnp.int32, (4 * BLK, 1), 0) // BLK
    sk = jnp.where(rb == 0, sink_ref[4 * g], jnp.where(rb == 1, sink_ref[4 * g + 1],
                   jnp.where(rb == 2, sink_ref[4 * g + 2], sink_ref[4 * g + 3])))
    m = jnp.maximum(jnp.maximum(jnp.max(s_l, axis=1, keepdims=True), jnp.max(s_c, axis=1, keepdims=True)), sk)
    e_l, e_c, e_s = jnp.exp(s_l - m), jnp.exp(s_c - m), jnp.exp(sk - m)
    inv = 1.0 / (jnp.sum(e_l, axis=1, keepdims=True) + jnp.sum(e_c, axis=1, keepdims=True) + e_s)
    return e_l, e_c, e_s, inv


def _attnpool_fwd(u, q, k4, v4, sink, w_pool, pool_scale, *, T, name, carry=None):
    R = u.shape[0]
    nb, nbl = R // BLK, T // BLK

    def body(q_ref, sink_ref, wp_ref, ps_ref, u_hbm, k4_hbm, v4_hbm, cat_ref, u_v, k4_v, v4_v, sem):
        j = pl.program_id(0)

        @pl.when(j == 0)
        def _():
            _load_weights([(u_hbm, u_v), (k4_hbm, k4_v), (v4_hbm, v4_v)], sem)

        pooled, _ = _pooled(u_v, j, T, R)
        for g in range(4):
            mixed = _dot(pooled[g].astype(BF16), wp_ref[g].astype(BF16)) * ps_ref[:, g * 128 : (g + 1) * 128]
            cat_ref[:, g * 128 : (g + 1) * 128] = mixed.astype(BF16)

        start_l = _win_start(j, T)
        valid = _window_mask(j, start_l, nbl)
        for g in range(2):
            gl = slice(g * 256, (g + 1) * 256)
            qs = _stack_heads(q_ref[:, gl])
            e_l, e_c, _, inv = _attn_exps(qs, k4_v[pl.ds(start_l, 3 * BLK), gl], k4_v[T:R, gl], sink_ref, g, valid)
            o = _dot(e_l.astype(BF16), v4_v[pl.ds(start_l, 3 * BLK), gl]) + _dot(e_c.astype(BF16), v4_v[T:R, gl])
            cat_ref[:, PW + g * 256 : PW + (g + 1) * 256] = _unstack_heads(o * inv).astype(BF16)

    return _call(
        body,
        name=name,
        grid=(nb,),
        in_specs=[_rows(BLK, AW), SMEM, _full((4, 128, 128)), _full((1, PW)), ANY, ANY, ANY],
        out_specs=[_rows(BLK, D)],
        out_shape=[_sds((R, D), BF16)],
        scratch_shapes=[pltpu.VMEM((R, PW), F32), pltpu.VMEM((R, AW), BF16), pltpu.VMEM((R, AW), BF16),
                        pltpu.SemaphoreType.DMA((3,))],
        args=(q, sink, w_pool, pool_scale, u, k4, v4),
        carry=carry,
    )


def _mixout_fwd(h, cat, modv, wout, *, T, ctx_active, name, carry=None):
    R = h.shape[0]
    n_lat, n_tiles = T // TM, R // TM

    def body(h_ref, cat_ref, mod_ref, w_ref, ho_ref, mo_ref):
        i = pl.program_id(0)

        def compute():
            mo = _dot(cat_ref[...], w_ref[...])
            mo_ref[...] = mo.astype(BF16)
            ho_ref[...] = h_ref[...] + mod_ref[0, 5:6, :] * mo

        if ctx_active:
            compute()
        else:
            pl.when(i < n_lat)(compute)

            @pl.when(i >= n_lat)
            def _():
                ho_ref[...] = h_ref[...]
                mo_ref[...] = jnp.zeros_like(mo_ref)

    return _call(
        body,
        name=name,
        grid=(n_tiles,),
        in_specs=[_rows(TM, D), _rows(TM, D), _mod_spec(n_lat), _full((D, D))],
        out_specs=[_rows(TM, D), _rows(TM, D)],
        out_shape=[_sds((R, D), F32), _sds((R, D), BF16)],
        scratch_shapes=[],
        args=(h, cat, modv, wout),
        carry=carry,
    )


def _mixout_bwd(dho, mo, modv, wout, *, T, ctx_active, name, carry=None):
    R = dho.shape[0]
    n_lat, n_tiles = T // TM, R // TM

    def body(dho_ref, mo_ref, mod_ref, w_ref, dcat_ref, dmix_ref, part_ref):
        i = pl.program_id(0)
        first = jnp.logical_or(i == 0, i == n_lat)

        def compute():
            dho = dho_ref[...]
            dmix = (mod_ref[0, 5:6, :] * dho).astype(BF16)
            dmix_ref[...] = dmix
            dcat_ref[...] = _dot_nt(dmix, w_ref[...])
            dgate = jnp.sum(dho * mo_ref[...].astype(F32), axis=0, keepdims=True)
            _acc_partials(part_ref, first, {2: dgate})

        if ctx_active:
            compute()
        else:
            pl.when(i < n_lat)(compute)

            @pl.when(i >= n_lat)
            def _():
                dcat_ref[...] = jnp.zeros_like(dcat_ref)
                dmix_ref[...] = jnp.zeros_like(dmix_ref)
                part_ref[...] = jnp.zeros_like(part_ref)

    return _call(
        body,
        name=name,
        grid=(n_tiles,),
        in_specs=[_rows(TM, D), _rows(TM, D), _mod_spec(n_lat), _full((D, D))],
        out_specs=[_rows(TM, D), _rows(TM, D), _part_spec(n_lat)],
        out_shape=[_sds((R, D), F32), _sds((R, D), BF16), _sds((2, 8, D), F32)],
        scratch_shapes=[],
        args=(dho, mo, modv, wout),
        carry=carry,
    )


def _pool_bwd(u, dcat, w_pool, pool_scale, *, T, name):
    R = u.shape[0]
    nb = R // BLK

    def body(dcat_ref, wp_ref, ps_ref, u_hbm, dps_ref, dwp_ref, dsc_ref, u_v, sem):
        j = pl.program_id(0)

        @pl.when(j == 0)
        def _():
            _load_weights([(u_hbm, u_v)], sem)
            dwp_ref[...] = jnp.zeros_like(dwp_ref)
            dsc_ref[...] = jnp.zeros_like(dsc_ref)

        pooled, counts = _pooled(u_v, j, T, R)
        for g in range(4):
            sl = slice(g * 128, (g + 1) * 128)
            p_bf = pooled[g].astype(BF16)
            w_bf = wp_ref[g].astype(BF16)
            dmixed = dcat_ref[:, sl]
            dsc_ref[0:1, sl] += jnp.sum(dmixed * _dot(p_bf, w_bf), axis=0, keepdims=True)
            dmp = (dmixed * ps_ref[:, sl]).astype(BF16)
            dwp_ref[sl, :] += _dot_tn(p_bf, dmp)
            dps_ref[:, sl] = _dot_nt(dmp, w_bf) / counts[g]

    return pl.pallas_call(
        body,
        name=name,
        grid=(nb,),
        in_specs=[_rows(BLK, D), _full((4, 128, 128)), _full((1, PW)), ANY],
        out_specs=[_rows(BLK, PW), _full((PW, 128)), _full((8, PW))],
        out_shape=[_sds((R, PW), F32), _sds((PW, 128), F32), _sds((8, PW), F32)],
        scratch_shapes=[pltpu.VMEM((R, PW), F32), pltpu.SemaphoreType.DMA((1,))],
        compiler_params=_params(),
    )(dcat, w_pool, pool_scale, u)


def _fold_heads(x):
    y = x[:, :128] + x[:, 128:]
    return y + pltpu.roll(y, HD, 1)


def _attn_bwd(q, k4, v4, dcat, dps, sink, *, T, name, carry=None):
    R = q.shape[0]
    nb, nbl = R // BLK, T // BLK

    def body(q_ref, dcat_ref, sink_ref, k4_hbm, v4_hbm, dps_hbm, du_ref, dq_ref, dk_ref, dv_ref, dsk_ref,
             k4_v, v4_v, dps_v, sem):
        j = pl.program_id(0)

        @pl.when(j == 0)
        def _():
            _load_weights([(k4_hbm, k4_v), (v4_hbm, v4_v), (dps_hbm, dps_v)], sem)
            dk_ref[...] = jnp.zeros_like(dk_ref)
            dv_ref[...] = jnp.zeros_like(dv_ref)
            dsk_ref[...] = jnp.zeros_like(dsk_ref)

        start = _win_start(j, R)
        d3_hi, d3_lo = _hi_lo(dps_v[pl.ds(start, 3 * BLK), :])
        db = dps_v[pl.ds(pl.multiple_of(j * BLK, BLK), BLK), :]
        pos = j * BLK + lax.broadcasted_iota(jnp.int32, (BLK, 1), 0)
        t_r = start + lax.broadcasted_iota(jnp.int32, (1, 3 * BLK), 1)
        for g, w in enumerate(POOL_WINDOWS):
            sl = slice(g * 128, (g + 1) * 128)
            lo_r, hi_r = _pool_bounds(t_r, w, T, R)
            band_t = jnp.where(pos >= lo_r, jnp.where(pos < hi_r, 1.0, 0.0), 0.0).astype(BF16)
            lo_c, hi_c = _pool_bounds(pos, w, T, R)
            du_ref[:, sl] = _dot(band_t, d3_hi[:, sl]) + _dot(band_t, d3_lo[:, sl]) - db[:, sl] * (hi_c - lo_c).astype(F32)

        start_l = _win_start(j, T)
        valid = _window_mask(j, start_l, nbl)
        rb = lax.broadcasted_iota(jnp.int32, (4 * BLK, 1), 0) // BLK
        lane = lax.broadcasted_iota(jnp.int32, (1, 128), 1)
        dk_l, dk_c, dv_l, dv_c = [], [], [], []
        for g in range(2):
            gl = slice(g * 256, (g + 1) * 256)
            qs = _stack_heads(q_ref[:, gl])
            kl, kc = k4_v[pl.ds(start_l, 3 * BLK), gl], k4_v[T:R, gl]
            vl, vc = v4_v[pl.ds(start_l, 3 * BLK), gl], v4_v[T:R, gl]
            e_l, e_c, e_s, inv = _attn_exps(qs, kl, kc, sink_ref, g, valid)
            p_l, p_c, p_s = e_l * inv, e_c * inv, e_s * inv
            dos = _stack_heads(dcat_ref[:, PW + g * 256 : PW + (g + 1) * 256]).astype(BF16)
            dp_l, dp_c = _dot_nt(dos, vl), _dot_nt(dos, vc)
            delta = jnp.sum(p_l * dp_l, axis=1, keepdims=True) + jnp.sum(p_c * dp_c, axis=1, keepdims=True)
            ds_l = (p_l * (dp_l - delta)).astype(BF16)
            ds_c = (p_c * (dp_c - delta)).astype(BF16)
            dq_ref[:, gl] = _unstack_heads(_dot(ds_l, kl) + _dot(ds_c, kc)) * (HD ** -0.5)
            dk_l.append(_fold_heads(_dot_tn(ds_l, qs)))
            dk_c.append(_fold_heads(_dot_tn(ds_c, qs)))
            dv_l.append(_fold_heads(_dot_tn(p_l.astype(BF16), dos)))
            dv_c.append(_fold_heads(_dot_tn(p_c.astype(BF16), dos)))
            dsink = -p_s * delta
            for h in range(4):
                tot = jnp.sum(jnp.where(rb == h, dsink, 0.0), axis=0, keepdims=True)
                dsk_ref[4 * g + h : 4 * g + h + 1, :] += jnp.broadcast_to(tot, (1, 128))
        first = lane < HD
        dk_ref[pl.ds(start_l, 3 * BLK), :] += jnp.where(first, dk_l[0], dk_l[1])
        dk_ref[T:R, :] += jnp.where(first, dk_c[0], dk_c[1])
        dv_ref[pl.ds(start_l, 3 * BLK), :] += jnp.where(first, dv_l[0], dv_l[1])
        dv_ref[T:R, :] += jnp.where(first, dv_c[0], dv_c[1])

    return _call(
        body,
        name=name,
        grid=(nb,),
        in_specs=[_rows(BLK, AW), _rows(BLK, D), SMEM, ANY, ANY, ANY],
        out_specs=[_rows(BLK, PW), _rows(BLK, AW), _full((R, KVW)), _full((R, KVW)), _full((8, 128))],
        out_shape=[_sds((R, PW), F32), _sds((R, AW), F32), _sds((R, KVW), F32), _sds((R, KVW), F32),
                   _sds((8, 128), F32)],
        scratch_shapes=[pltpu.VMEM((R, AW), BF16), pltpu.VMEM((R, AW), BF16), pltpu.VMEM((R, PW), F32),
                        pltpu.SemaphoreType.DMA((3,))],
        args=(q, dcat, sink, k4, v4, dps),
        carry=carry,
    )


def _mixproj_bwd(h, dho, du, dq, dk, dv, modv, gvec, win, cos, sin, *, T, name):
    R = h.shape[0]
    n_lat, n_tiles = T // TM, R // TM

    def body(h_ref, dho_ref, du_ref, dq_ref, dk_ref, dv_ref, mod_ref, g_ref, win_ref, cos_ref, sin_ref,
             dh_ref, dproj_ref, n_ref, part_ref):
        i = pl.program_id(0)
        first = jnp.logical_or(i == 0, i == n_lat)
        shift, scale = mod_ref[0, 3:4, :], mod_ref[0, 4:5, :]
        g = g_ref[1:2, :]
        r, xhat, y, n = _norm_mod(h_ref[...], g, shift, scale)
        n_ref[...] = n.astype(BF16)
        cs, sn = cos_ref[...], sin_ref[...]
        dproj_ref[:, :PW] = du_ref[...].astype(BF16)
        for s in range(AW // 128):
            x = dq_ref[:, 128 * s : 128 * (s + 1)]
            dproj_ref[:, PW + 128 * s : PW + 128 * (s + 1)] = (x * cs - _rot_half(x) * sn).astype(BF16)
        x = dk_ref[...]
        dproj_ref[:, PW + AW : PW + AW + KVW] = (x * cs - _rot_half(x) * sn).astype(BF16)
        dproj_ref[:, PW + AW + KVW :] = dv_ref[...].astype(BF16)
        dn = _dot(dproj_ref[...], win_ref[...])
        dh, dshift, dscale, dg = _norm_mod_bwd(dn, r, xhat, y, g, scale)
        dh_ref[...] = dho_ref[...] + dh
        _acc_partials(part_ref, first, {0: dshift, 1: dscale, 3: dg})

    return pl.pallas_call(
        body,
        name=name,
        grid=(n_tiles,),
        in_specs=[_rows(TM, D), _rows(TM, D), _rows(TM, PW), _rows(TM, AW), _rows(TM, KVW), _rows(TM, KVW),
                  _mod_spec(n_lat), _full((8, D)), _full((PROJ, D)), _rows(TM, 128), _rows(TM, 128)],
        out_specs=[_rows(TM, D), _rows(TM, PROJ), _rows(TM, D), _part_spec(n_lat)],
        out_shape=[_sds((R, D), F32), _sds((R, PROJ), BF16), _sds((R, D), BF16), _sds((2, 8, D), F32)],
        compiler_params=_params(),
    )(h, dho, du, dq, dk, dv, modv, gvec, win, cos, sin)


def _loss_head(h, target, g_final, *, T, name):
    R = h.shape[0]
    n_lat, n_tiles = T // TM, R // TM

    def body(h_ref, t_ref, g_ref, dh_ref, loss_ref, dg_ref):
        i = pl.program_id(0)

        @pl.when(i == 0)
        def _():
            loss_ref[...] = jnp.zeros_like(loss_ref)
            dg_ref[...] = jnp.zeros_like(dg_ref)

        @pl.when(i < n_lat)
        def _():
            h = h_ref[...]
            g = g_ref[...]
            r = lax.rsqrt(jnp.mean(h * h, axis=-1, keepdims=True) + EPS)
            xhat = h * r
            err = xhat * g - t_ref[...]
            tot = jnp.sum(jnp.sum(err * err, axis=1, keepdims=True), axis=0, keepdims=True)
            loss_ref[...] += jnp.broadcast_to(tot * (0.5 / D), loss_ref.shape)
            dy = err * (1.0 / D)
            dg_ref[0:1, :] += jnp.sum(dy * xhat, axis=0, keepdims=True)
            dxh = dy * g
            dh_ref[...] = r * (dxh - xhat * jnp.mean(dxh * xhat, axis=-1, keepdims=True))

        @pl.when(i >= n_lat)
        def _():
            dh_ref[...] = jnp.zeros_like(dh_ref)

    return pl.pallas_call(
        body,
        name=name,
        grid=(n_tiles,),
        in_specs=[_rows(TM, D), pl.BlockSpec((TM, D), lambda i: (jnp.minimum(i, n_lat - 1), 0)), _full((1, D))],
        out_specs=[_rows(TM, D), _full((8, 128)), _full((8, D))],
        out_shape=[_sds((R, D), F32), _sds((8, 128), F32), _sds((8, D), F32)],
        compiler_params=_params(),
    )(h, target, g_final)


def _mod_fwd(c16, w_mod, b_cols, *, name):
    nl, _, cols = w_mod.shape

    def body(c_ref, w_ref, b_ref, o_ref):
        c = c_ref[...]
        sc = (c * _sigmoid(c)).astype(BF16)
        o_ref[0] = _dot(sc, w_ref[0].astype(BF16)) + b_ref[0]

    return pl.pallas_call(
        body,
        name=name,
        grid=(nl,),
        in_specs=[_full((16, D)), pl.BlockSpec((1, D, cols), lambda l: (l, 0, 0)),
                  pl.BlockSpec((1, 1, cols), lambda l: (l, 0, 0))],
        out_specs=pl.BlockSpec((1, 16, cols), lambda l: (l, 0, 0)),
        out_shape=_sds((nl, 16, cols), F32),
        compiler_params=_params(),
    )(c16, w_mod, b_cols)


def _mod_bwd(c16, dm_cols, w_mod, *, name):
    nl, _, cols = w_mod.shape

    def body(c_ref, dm_ref, w_ref, gw_ref, dc_ref):
        c = c_ref[...]
        sc = (c * _sigmoid(c)).astype(BF16)
        dm = dm_ref[0].astype(BF16)
        gw_ref[0] = _dot_tn(sc, dm)
        dc_ref[0] = _dot_nt(dm, w_ref[0].astype(BF16))

    return pl.pallas_call(
        body,
        name=name,
        grid=(nl,),
        in_specs=[_full((16, D)), pl.BlockSpec((1, 16, cols), lambda l: (l, 0, 0)),
                  pl.BlockSpec((1, D, cols), lambda l: (l, 0, 0))],
        out_specs=[pl.BlockSpec((1, D, cols), lambda l: (l, 0, 0)), pl.BlockSpec((1, 16, D), lambda l: (l, 0, 0))],
        out_shape=[_sds((nl, D, cols), F32), _sds((nl, 16, D), F32)],
        compiler_params=_params(),
    )(c16, dm_cols, w_mod)


def _coords():
    return lax.axis_index("x"), lax.axis_index("y"), lax.axis_index("c")


FWD = 8


def _peer(k, x, y, c):
    if k == FWD:
        return (x ^ (1 - c), y ^ c, c)
    return (1 - x if k & 4 else x, 1 - y if k & 2 else y, 1 - c if k & 1 else c)


def _lin(p):
    return 4 * p[0] + 2 * p[1] + p[2]


def _view(ref, slot):
    return ref if slot is None else ref.at[slot]


class _Round:
    def __init__(self, ins, out_shapes, plan, local_plan=(), n_alias=0):
        self.ins, self.out_shapes = list(ins), list(out_shapes)
        self.plan, self.local_plan, self.n_alias = list(plan), list(local_plan), n_alias
        fed = {p[3] for p in self.plan if p[0] == FWD}
        self.feeders = [n for n, p in enumerate(self.plan) if p[0] in (2, 4, 6) and p[3] in fed]

    def sems(self):
        return [pltpu.SemaphoreType.DMA((len(self.plan),)), pltpu.SemaphoreType.DMA((len(self.plan),)),
                pltpu.SemaphoreType.DMA((max(len(self.local_plan), 1),))]

    def _remote(self, in_refs, out_refs, sems, incoming, pick):
        in_refs = list(out_refs[: self.n_alias]) + list(in_refs[self.n_alias :])
        x, y, c = _coords()
        me = _lin((x, y, c))
        copies = {}
        for idx, (k, ii, sfn, oi, dfn) in enumerate(self.plan):
            if not pick(idx, "d2d" if k == 1 else "fwd" if k == FWD else "ici"):
                continue
            peer = _peer(k, x, y, c)
            sender, receiver = (_lin(peer), me) if incoming else (me, _lin(peer))
            src = out_refs[oi] if ii is None else in_refs[ii]
            copies[idx] = pltpu.make_async_remote_copy(
                src_ref=_view(src, sfn(sender, receiver)), dst_ref=_view(out_refs[oi], dfn(sender, receiver)),
                send_sem=sems[0].at[idx], recv_sem=sems[1].at[idx], device_id=peer, device_id_type=MESH)
        return copies

    def _local(self, in_refs, out_refs, sems):
        in_refs = list(out_refs[: self.n_alias]) + list(in_refs[self.n_alias :])
        me = _lin(_coords())
        return [pltpu.make_async_copy(_view(in_refs[ii], sfn(me)), _view(out_refs[oi], dfn(me)), sems[2].at[idx])
                for idx, (ii, sfn, oi, dfn) in enumerate(self.local_plan)]

    def start(self, in_refs, out_refs, sems, links=("ici", "d2d")):
        for cp in self._remote(in_refs, out_refs, sems, False, lambda n, link: link in links).values():
            cp.start()
        if "ici" in links:
            for cp in self._local(in_refs, out_refs, sems):
                cp.start()

    def mid(self, in_refs, out_refs, sems):
        if self.feeders:
            for cp in self._remote(in_refs, out_refs, sems, True, lambda n, link: n in self.feeders).values():
                cp.wait_recv()
            for cp in self._remote(in_refs, out_refs, sems, False, lambda n, link: link == "fwd").values():
                cp.start()

    def finish(self, in_refs, out_refs, sems):
        for cp in self._remote(in_refs, out_refs, sems, True, lambda n, link: n not in self.feeders).values():
            cp.wait_recv()
        for cp in self._remote(in_refs, out_refs, sems, False, lambda n, link: True).values():
            cp.wait_send()
        for cp in self._local(in_refs, out_refs, sems):
            cp.wait()


def _exchange(name, rnd):
    n_in, n_out = len(rnd.ins), len(rnd.out_shapes)

    def body(*refs):
        in_refs, out_refs, sems = refs[:n_in], refs[n_in : n_in + n_out], refs[n_in + n_out :]
        rnd.start(in_refs, out_refs, sems)
        rnd.mid(in_refs, out_refs, sems)
        rnd.finish(in_refs, out_refs, sems)

    return pl.pallas_call(
        body, name=name, in_specs=[ANY] * n_in, out_specs=[ANY] * n_out, out_shape=rnd.out_shapes,
        scratch_shapes=rnd.sems(), input_output_aliases={i: i for i in range(rnd.n_alias)})(*rnd.ins)


def _call(body, *, name, grid, in_specs, out_specs, out_shape, scratch_shapes, args, carry=None):
    params = _params(len(grid))
    if carry is None:
        outs = pl.pallas_call(body, name=name, grid=grid, in_specs=in_specs, out_specs=out_specs, out_shape=out_shape,
                              scratch_shapes=scratch_shapes, compiler_params=params)(*args)
        return list(outs), []
    n_ci, n_co, n_cs = len(in_specs), len(out_shape), len(scratch_shapes)
    n_xi, n_xo = len(carry.ins), len(carry.out_shapes)

    def wrapped(*refs):
        ci, xi = refs[:n_ci], refs[n_ci : n_ci + n_xi]
        o0 = n_ci + n_xi
        co, xo = refs[o0 : o0 + n_co], refs[o0 + n_co : o0 + n_co + n_xo]
        s0 = o0 + n_co + n_xo
        cs, sems = refs[s0 : s0 + n_cs], refs[s0 + n_cs :]
        ids = [pl.program_id(a) for a in range(len(grid))]
        first = functools.reduce(jnp.logical_and, [i == 0 for i in ids])
        last = functools.reduce(jnp.logical_and, [i == g - 1 for i, g in zip(ids, grid)])

        @pl.when(first)
        def _():
            carry.start(xi, xo, sems, links=("ici",))

        if carry.feeders:
            step = functools.reduce(lambda acc, ig: acc * ig[1] + ig[0], zip(ids, grid), 0)
            n_steps = functools.reduce(lambda a, b: a * b, grid)

            @pl.when(step == min(n_steps - 1, (3 * n_steps) // 5))
            def _():
                carry.mid(xi, xo, sems)

        body(*ci, *co, *cs)

        @pl.when(first)
        def _():
            carry.start(xi, xo, sems, links=("d2d",))

        @pl.when(last)
        def _():
            carry.finish(xi, xo, sems)

    outs = pl.pallas_call(
        wrapped, name=name, grid=grid, in_specs=list(in_specs) + [ANY] * n_xi, out_specs=list(out_specs) + [ANY] * n_xo,
        out_shape=list(out_shape) + carry.out_shapes, scratch_shapes=list(scratch_shapes) + carry.sems(),
        input_output_aliases={n_ci + i: n_co + i for i in range(carry.n_alias)}, compiler_params=params,
    )(*args, *carry.ins)
    return list(outs[:n_co]), list(outs[n_co:])


def _gather_direct(arrays):
    na = len(arrays)
    outs = [_sds((NDEV,) + a.shape, a.dtype) for a in arrays]
    plan = [(k, i, lambda s, r: None, i, lambda s, r: s) for i in range(na) for k in range(1, NDEV)]
    return _Round(arrays, outs, plan, [(i, lambda m: None, i, lambda m: m) for i in range(na)])


def _gather_a(arrays):
    na = len(arrays)
    outs = [_sds((NDEV,) + a.shape, a.dtype) for a in arrays]
    plan = [(k, i, lambda s, r: None, i, lambda s, r: s) for i in range(na) for k in (2, 4)]
    handed = lambda s, r: s ^ (2 << (s & 1))
    plan += [(FWD, None, handed, i, handed) for i in range(na)]
    return _Round(arrays, outs, plan, [(i, lambda m: None, i, lambda m: m) for i in range(na)])


def _gather_b(got):
    na = len(got)
    plan = [(1, i, (lambda s, r, k=k: s ^ k), i, (lambda s, r, k=k: s ^ k)) for i in range(na) for k in (0, 2, 4, 6)]
    return _Round(got, [_sds(g.shape, g.dtype) for g in got], plan, n_alias=na)


def _scatter_1(grads):
    plan = [(1, i, (lambda s, r, q=q: 2 * q + (r & 1)), i, (lambda s, r, q=q: q))
            for i in range(len(grads)) for q in range(4)]
    return _Round(grads, [_sds((4,) + g.shape[1:], g.dtype) for g in grads], plan)


def _scatter_2(chip):
    plan = [(k, i, lambda s, r: r >> 1, i, (lambda s, r, j=j: j)) for i in range(len(chip)) for j, k in enumerate((2, 4, 6))]
    return _Round(chip, [_sds((3,) + g.shape[1:], g.dtype) for g in chip], plan)


def _add_pairs(g, got, pos, *, name):
    _, sh, w = g.shape

    def body(pos_ref, g_ref, r_ref, o_ref):
        o_ref[...] = (g_ref[...].astype(F32) + r_ref[...].astype(F32)).astype(o_ref.dtype)

    return pl.pallas_call(
        body,
        name=name,
        grid_spec=pltpu.PrefetchScalarGridSpec(
            num_scalar_prefetch=1, grid=(4,),
            in_specs=[pl.BlockSpec((1, sh, w), lambda q, p: (2 * q + p[0], 0, 0)),
                      pl.BlockSpec((1, sh, w), lambda q, p: (q, 0, 0))],
            out_specs=pl.BlockSpec((1, sh, w), lambda q, p: (q, 0, 0))),
        out_shape=_sds((4, sh, w), g.dtype),
        compiler_params=_params(),
    )(pos, g, got)


def _sum_adamw(chip, got, pos, w, m, v, layer, prior, *, transpose, name):
    _, sh, wd = chip.shape
    nl, rows, cols = w.shape
    if transpose:
        nb, blk = 4, (1, wd // 4, cols)
        part = lambda n: pl.BlockSpec((n, sh, wd // 4), lambda i, p: ((p[1] if n == 1 else 0), 0, i))
    else:
        nb, blk = 2, (1, sh // 2, wd)
        part = lambda n: pl.BlockSpec((n, sh // 2, wd), lambda i, p: ((p[1] if n == 1 else 0), i, 0))
    mine = pl.BlockSpec(blk, lambda i, p: (layer, i, 0))
    n_prior = 0 if prior is None else 4

    def body(pos_ref, c_ref, r_ref, w_ref, m_ref, v_ref, *refs):
        g_ref, d_ref, m2_ref, v2_ref = refs[n_prior:]
        acc = c_ref[0].astype(F32)
        for s in range(3):
            acc = acc + r_ref[s].astype(F32)
        g = acc.T if transpose else acc
        g_ref[0] = g
        d_ref[0], m2_ref[0], v2_ref[0] = _adamw_math(w_ref[0], g, m_ref[0], v_ref[0])

    return pl.pallas_call(
        body,
        name=name,
        grid_spec=pltpu.PrefetchScalarGridSpec(
            num_scalar_prefetch=1, grid=(nb,),
            in_specs=[part(1), part(3), mine, mine, mine] + [ANY] * n_prior,
            out_specs=[mine] * 4),
        out_shape=[_sds(w.shape, F32)] * 4,
        input_output_aliases={6 + k: k for k in range(n_prior)},
        compiler_params=_params(),
    )(pos, chip, got, w, m, v, *(prior or ()))


def _adamw_math(w, g, m, v):
    m2 = ADAM_B1 * m + (1.0 - ADAM_B1) * g
    v2 = ADAM_B2 * v + (1.0 - ADAM_B2) * (g * g)
    m_hat = m2 / (1.0 - ADAM_B1 ** ADAM_STEP)
    v_hat = v2 / (1.0 - ADAM_B2 ** ADAM_STEP)
    delta = -ADAM_LR * (m_hat / (jnp.sqrt(v_hat) + ADAM_EPS) + ADAM_WD * w)
    return delta, m2, v2


def _adamw(w, g, m, v, *, name, carry=None):
    shape = w.shape
    flat = [t.reshape(-1, shape[-1]) for t in (w, g, m, v)]
    rows, cols = flat[0].shape
    tr = rows // 8 if rows % 64 == 0 else rows
    spec = _rows(tr, cols)

    def body(w_ref, g_ref, m_ref, v_ref, d_ref, m2_ref, v2_ref):
        d_ref[...], m2_ref[...], v2_ref[...] = _adamw_math(w_ref[...], g_ref[...], m_ref[...], v_ref[...])

    outs, got = _call(body, name=name, grid=(rows // tr,), in_specs=[spec] * 4, out_specs=[spec] * 3,
                      out_shape=[_sds((rows, cols), F32)] * 3, scratch_shapes=[], args=flat, carry=carry)
    return tuple(o.reshape(shape) for o in outs), got


def _adds(tag, grads, got, *, pos):
    return [_add_pairs(g, r, pos, name=f"rs_add_{tag}_{i}") for i, (g, r) in enumerate(zip(grads, got))]


def _small_sums(packets, nf, dwp, dsc, dsk, *, name):
    flat = [p for layer in packets for p in layer]

    def total(ref, *idx):
        acc = ref[(0,) + idx]
        for dev in range(1, NDEV):
            acc = acc + ref[(dev,) + idx]
        return acc

    def body(*refs):
        pk = refs[:6]
        nf_ref, dwp0, dwp1, dsc0, dsc1, dsk0, dsk1 = refs[6:13]
        dm_ref, gb_ref, gn_ref, gnf_ref, gwp_ref, gps_ref, gsk_ref = refs[13:]
        dm_ref[...] = jnp.zeros_like(dm_ref)
        gn_ref[...] = jnp.zeros_like(gn_ref)
        for l in range(2):
            for sb in range(3):
                p = pk[3 * l + sb]
                for r in range(3):
                    col = slice((3 * sb + r) * D, (3 * sb + r + 1) * D)
                    lat = p[0, 0, r : r + 1, :]
                    dm_ref[l, 0:1, col] = lat
                    for dev in range(1, NDEV):
                        row = p[dev, 0, r : r + 1, :]
                        dm_ref[l, dev : dev + 1, col] = row
                        lat = lat + row
                    ctx = total(p, 1, slice(r, r + 1), slice(None))
                    dm_ref[l, 8:9, col] = ctx
                    gb_ref[l : l + 1, col] = lat + ctx
                gn_ref[l, sb : sb + 1, :] = total(p, 0, slice(3, 4), slice(None)) + total(p, 1, slice(3, 4), slice(None))
        gnf_ref[...] = total(nf_ref, slice(0, 1), slice(None))
        for l, (a, b, c) in enumerate(((dwp0, dsc0, dsk0), (dwp1, dsc1, dsk1))):
            gwp_ref[l] = total(a, slice(None), slice(None))
            gps_ref[l : l + 1, :] = total(b, slice(0, 1), slice(None))
            gsk_ref[l] = total(c, slice(None), slice(None))

    ins = flat + [nf, dwp[0], dwp[1], dsc[0], dsc[1], dsk[0], dsk[1]]
    return pl.pallas_call(
        body,
        name=name,
        out_shape=[_sds((2, 16, NMOD * D), F32), _sds((2, NMOD * D), F32), _sds((2, 8, D), F32), _sds((1, D), F32),
                   _sds((2, PW, 128), F32), _sds((2, PW), F32), _sds((2, 8, 128), F32)],
        compiler_params=pltpu.CompilerParams(vmem_limit_bytes=VMEM_LIMIT),
    )(*ins)


def _small_adamw(c_ctx, dc_all, triples, *, name):
    n = len(triples)

    def body(*refs):
        c_ref, dc_ref = refs[0], refs[1]
        ins = refs[2 : 2 + 4 * n - 1]
        outs = refs[2 + 4 * n - 1 :]
        acc = dc_ref[0, 0, 8:9, :] + dc_ref[0, 1, 8:9, :]
        for dev in range(1, NDEV):
            acc = acc + (dc_ref[dev, 0, 8:9, :] + dc_ref[dev, 1, 8:9, :])
        c = c_ref[...]
        sig = _sigmoid(c)
        g_c = acc * (sig * (1.0 + c * (1.0 - sig)))
        outs[0][...] = g_c
        pos = 0
        for k in range(n):
            if k == 0:
                w, g, m, v = ins[0][...], g_c, ins[1][...], ins[2][...]
                pos = 3
            else:
                w, g, m, v = (ins[pos + t][...] for t in range(4))
                pos += 4
            d, m2, v2 = _adamw_math(w, g, m, v)
            outs[1 + 3 * k][...], outs[2 + 3 * k][...], outs[3 + 3 * k][...] = d, m2, v2

    flat_in = [c_ctx, dc_all]
    out_shape = [_sds(c_ctx.shape, F32)]
    for k, (w, g, m, v) in enumerate(triples):
        flat_in += [w, m, v] if k == 0 else [w, g, m, v]
        out_shape += [_sds(w.shape, F32)] * 3
    return pl.pallas_call(body, name=name, out_shape=out_shape,
                          compiler_params=pltpu.CompilerParams(vmem_limit_bytes=VMEM_LIMIT))(*flat_in)


def _rope_tables(T, R):
    t = jnp.arange(T)
    inv = ROPE_BASE ** (-jnp.arange(0, HD // 2, 2, dtype=F32) / (HD // 2))
    ang = jnp.concatenate([(t // GRID_W).astype(F32)[:, None] * inv, (t % GRID_W).astype(F32)[:, None] * inv], axis=-1)
    cos = jnp.concatenate([jnp.tile(jnp.cos(ang), (1, 4)), jnp.ones((R - T, 128), F32)], axis=0)
    sin = jnp.concatenate([jnp.tile(jnp.sin(ang), (1, 4)), jnp.zeros((R - T, 128), F32)], axis=0)
    return cos, sin


def kernel(x, c, ctx, c_ctx, w_mod, b_mod, norm_ffn1, w_ffn1_in, w_ffn1_out, norm_mix, w_in, w_pool, pool_scale, sink, w_out, norm_ffn2, w_ffn2_in, w_ffn2_out, norm_final, loss_target, m_c_ctx, m_w_mod, m_b_mod, m_norm_ffn1, m_w_ffn1_in, m_w_ffn1_out, m_norm_mix, m_w_in, m_w_pool, m_pool_scale, m_sink, m_w_out, m_norm_ffn2, m_w_ffn2_in, m_w_ffn2_out, m_norm_final, v_c_ctx, v_w_mod, v_b_mod, v_norm_ffn1, v_w_ffn1_in, v_w_ffn1_out, v_norm_mix, v_w_in, v_w_pool, v_pool_scale, v_sink, v_w_out, v_norm_ffn2, v_w_ffn2_in, v_w_ffn2_out, v_norm_final):
    T = x.shape[1]
    R = T + LC
    nl = w_mod.shape[0]
    cx, cy, cc = _coords()
    me = _lin((cx, cy, cc))
    pos = jnp.stack([cc, 2 * cx + cy]).astype(jnp.int32)
    mcols = w_mod.shape[2]

    shards = [([w_ffn1_in[l].T.astype(BF16), w_ffn1_out[l].astype(BF16)],
               [w_in[l].T.astype(BF16), w_out[l].astype(BF16)],
               [w_ffn2_in[l].T.astype(BF16), w_ffn2_out[l].astype(BF16)]) for l in range(nl)]

    got = _exchange("ag_c_w", _merge(_gather_direct([c]), _gather_a(shards[0][0] + shards[0][1])))
    c_all, w_first = got[0], got[1:]
    c16 = jnp.concatenate([c_all.reshape(NDEV, D), c_ctx[None], jnp.zeros((16 - NDEV - 1, D), F32)], axis=0)
    b_cols = lax.dynamic_slice(b_mod, (0, me * mcols), (nl, mcols)).reshape(nl, 1, mcols)
    got = _exchange("ag_mod_w", _merge(_gather_b(w_first), _gather_direct([_mod_fwd(c16, w_mod, b_cols, name="mod_fwd")])))
    w_first, mod_all = got[:4], got[4]
    mod_all = jnp.transpose(mod_all, (1, 2, 0, 3)).reshape(nl, 16, NMOD, D)
    mine = lax.dynamic_index_in_dim(mod_all, me, axis=1, keepdims=False)
    pad = jnp.zeros((nl, 16 - NMOD, D), F32)
    modv = jnp.stack([jnp.concatenate([mine, pad], axis=1), jnp.concatenate([mod_all[:, 8], pad], axis=1)], axis=1)

    gvec = [jnp.concatenate([norm_ffn1[l][None], norm_mix[l][None], norm_ffn2[l][None], jnp.zeros((5, D), F32)], axis=0)
            for l in range(nl)]
    cos, sin = _rope_tables(T, R)
    ps2 = [pool_scale[l][None] for l in range(nl)]

    h = jnp.concatenate([x[0], ctx[0]], axis=0)
    loss_all, dh, small, nf_all, big, last_partials = _forward_backward(
        h, loss_target[0], modv, gvec, shards, w_first, cos, sin, sink, w_pool, ps2, norm_final, pos, T=T)
    loss = jnp.sum(loss_all[:, 0, 0])
    grad_x = dh[:T][None]

    dm, g_b_mod, g_norms, g_nf, g_wp, g_ps, g_sk = _small_sums(
        [small[l][0:3] for l in range(nl)], nf_all, *[[small[l][k] for l in range(nl)] for k in (3, 4, 5)],
        name="small_sums")
    dm_cols = lax.dynamic_slice(dm, (0, 0, me * mcols), (nl, 16, mcols))
    g_w_mod, dc_part = _mod_bwd(c16, dm_cols, w_mod, name="mod_bwd")
    got = _exchange("rs1_tail", _merge(_scatter_1(last_partials), _gather_direct([dc_part])))
    c1, dc_all = _adds("ffn1_0", last_partials, got[:2], pos=pos), got[2]

    delta, new_m, new_v = {}, {}, {}
    (delta["w_mod"], new_m["w_mod"], new_v["w_mod"]), got = _adamw(
        w_mod, g_w_mod, m_w_mod, v_w_mod, name="adamw_w_mod", carry=_scatter_2(c1))
    big[0][0:2] = [(c1[0], got[0]), (c1[1], got[1])]

    grads = {
        "b_mod": g_b_mod, "norm_ffn1": g_norms[:, 0], "norm_mix": g_norms[:, 1], "norm_ffn2": g_norms[:, 2],
        "w_pool": g_wp.reshape(w_pool.shape), "pool_scale": g_ps, "sink": g_sk[:, :, 0], "norm_final": g_nf.reshape(D),
        "w_mod": g_w_mod,
    }
    weights = dict(c_ctx=c_ctx, w_mod=w_mod, b_mod=b_mod, norm_ffn1=norm_ffn1, w_ffn1_in=w_ffn1_in, w_ffn1_out=w_ffn1_out,
                   norm_mix=norm_mix, w_in=w_in, w_pool=w_pool, pool_scale=pool_scale, sink=sink, w_out=w_out,
                   norm_ffn2=norm_ffn2, w_ffn2_in=w_ffn2_in, w_ffn2_out=w_ffn2_out, norm_final=norm_final)
    moms = dict(c_ctx=(m_c_ctx, v_c_ctx), w_mod=(m_w_mod, v_w_mod), b_mod=(m_b_mod, v_b_mod),
                norm_ffn1=(m_norm_ffn1, v_norm_ffn1), w_ffn1_in=(m_w_ffn1_in, v_w_ffn1_in),
                w_ffn1_out=(m_w_ffn1_out, v_w_ffn1_out), norm_mix=(m_norm_mix, v_norm_mix), w_in=(m_w_in, v_w_in),
                w_pool=(m_w_pool, v_w_pool), pool_scale=(m_pool_scale, v_pool_scale), sink=(m_sink, v_sink),
                w_out=(m_w_out, v_w_out), norm_ffn2=(m_norm_ffn2, v_norm_ffn2), w_ffn2_in=(m_w_ffn2_in, v_w_ffn2_in),
                w_ffn2_out=(m_w_ffn2_out, v_w_ffn2_out), norm_final=(m_norm_final, v_norm_final))
    order = list(weights)
    small_names = ["c_ctx", "b_mod", "norm_ffn1", "norm_mix", "w_pool", "pool_scale", "sink", "norm_ffn2", "norm_final"]

    def as2d(name, t):
        if name == "w_pool":
            return t.reshape(-1, 128)
        return t.reshape(1, -1) if t.ndim == 1 else t

    triples = [(as2d(n, weights[n]), None if n == "c_ctx" else as2d(n, grads[n]), as2d(n, moms[n][0]), as2d(n, moms[n][1]))
               for n in small_names]
    outs = _small_adamw(as2d("c_ctx", c_ctx), dc_all, triples, name="small_adamw")
    grads["c_ctx"] = outs[0].reshape(c_ctx.shape)
    for k, n in enumerate(small_names):
        delta[n], new_m[n], new_v[n] = (o.reshape(weights[n].shape) for o in outs[1 + 3 * k : 4 + 3 * k])
    for k, n in enumerate(["w_ffn1_in", "w_ffn1_out", "w_in", "w_out", "w_ffn2_in", "w_ffn2_out"]):
        outs = None
        for l in reversed(range(nl)):
            outs = _sum_adamw(*big[l][k], pos, weights[n], *moms[n], l, outs, transpose=k % 2 == 0, name=f"adamw_{n}_{l}")
        grads[n], delta[n], new_m[n], new_v[n] = outs

    return (loss, grad_x, *[grads[n] for n in order], *[delta[n] for n in order],
            *[new_m[n] for n in order], *[new_v[n] for n in order])


def _merge(*rounds):
    ins, outs, plan, local, n_alias = [], [], [], [], 0
    for r in rounds:
        assert r.n_alias == 0 or (not ins and r.n_alias == len(r.ins) == len(r.out_shapes))
        oi, oo = len(ins), len(outs)
        plan += [(k, None if i is None else i + oi, sf, o + oo, df) for k, i, sf, o, df in r.plan]
        local += [(i + oi, sf, o + oo, df) for i, sf, o, df in r.local_plan]
        ins += r.ins
        outs += r.out_shapes
        n_alias += r.n_alias
    return _Round(ins, outs, plan, local, n_alias)


def _forward_backward(h, target, modv, gvec, shards, w_first, cos, sin, sink, w_pool, ps2, norm_final, pos, *, T):
    nl = len(gvec)
    flat = lambda ws: [w.reshape(-1, D) for w in ws]
    saved = []
    w1, wm = flat(w_first[:2]), flat(w_first[2:])
    for l in range(nl):
        last = l == nl - 1
        h0 = h
        if l == 0:
            (h1, a1, b1, f1), got = _ffn_fwd(h0, modv[l], gvec[l], *w1, T=T, mrow=0, grow=0, ctx_active=True,
                                             name=f"ffn1_fwd_{l}", carry=_gather_a(shards[l][2]))
            (u, q, k4, v4), got = _mixproj_fwd(h1, modv[l], gvec[l], wm[0], cos, sin, T=T, name=f"mixproj_fwd_{l}",
                                               carry=_gather_b(got))
            w2 = flat(got)
        else:
            (h1, a1, b1, f1), got = _ffn_fwd(h0, modv[l], gvec[l], *w1, T=T, mrow=0, grow=0, ctx_active=True,
                                             name=f"ffn1_fwd_{l}", carry=_gather_b(nxt_m + nxt_2))
            wm, w2 = flat(got[:2]), flat(got[2:])
            (u, q, k4, v4), _ = _mixproj_fwd(h1, modv[l], gvec[l], wm[0], cos, sin, T=T, name=f"mixproj_fwd_{l}")
        (cat,), nxt_1 = _attnpool_fwd(u, q, k4, v4, sink[l], w_pool[l], ps2[l], T=T, name=f"attnpool_fwd_{l}",
                                      carry=None if last else _gather_a(shards[l + 1][0]))
        (h2, mo), nxt_m = _mixout_fwd(h1, cat, modv[l], wm[1], T=T, ctx_active=not last, name=f"mixout_fwd_{l}",
                                      carry=None if last else _gather_a(shards[l + 1][1]))
        (h3, a2, b2, f2), got = _ffn_fwd(h2, modv[l], gvec[l], *w2, T=T, mrow=6, grow=2, ctx_active=not last,
                                         name=f"ffn2_fwd_{l}",
                                         carry=None if last else _merge(_gather_b(nxt_1), _gather_a(shards[l + 1][2])))
        saved.append((h0, a1, b1, f1, h1, u, q, k4, v4, cat, mo, h2, a2, b2, f2, w1, wm, w2))
        h = h3
        if not last:
            w1, nxt_2 = flat(got[:2]), got[2:]

    dh, loss_part, dnf = _loss_head(h, target, norm_final[None], T=T, name="loss_head")

    adds = functools.partial(_adds, pos=pos)
    small, big = [None] * nl, {}
    prev = None
    for l in reversed(range(nl)):
        last = l == nl - 1
        h0, a1, b1, f1, h1, u, q, k4, v4, cat, mo, h2, a2, b2, f2, w1, wm, w2 = saved[l]
        (dh, dab, s, n, df, pk2), got = _ffn_bwd(
            h2, dh, a2, b2, f2, modv[l], gvec[l], *w2, T=T, mrow=6, grow=2, ctx_active=not last, name=f"ffn2_bwd_{l}",
            carry=_merge(_scatter_1(prev[0]), _gather_a(prev[1])) if prev else None)
        if prev:
            c1, small_a = adds(f"ffn1_{l + 1}", prev[0], got[:2]), got[2:]
        g_w2i, got = _wgrad(dab, n, bk=WG_BK, sh=2 * DFF // NDEV, name=f"wgrad_ffn2_in_{l}",
                            carry=_scatter_2(c1[:1]) if prev else None)
        if prev:
            big[l + 1][0] = (c1[0], got[0])
        g_w2o, got = _wgrad(s, df, bk=WG_BK, sh=DFF // NDEV, name=f"wgrad_ffn2_out_{l}",
                            carry=_scatter_2(c1[1:]) if prev else None)
        if prev:
            big[l + 1][1] = (c1[1], got[0])
        rnd = _scatter_1([g_w2i, g_w2o])
        (dcat, dmix, pko), got = _mixout_bwd(dh, mo, modv[l], wm[1], T=T, ctx_active=not last, name=f"mixout_bwd_{l}",
                                             carry=_merge(_gather_b(small_a), rnd) if prev else rnd)
        if prev:
            small[l + 1], got = got[: len(small_a)], got[len(small_a) :]
        c2 = adds(f"ffn2_{l}", [g_w2i, g_w2o], got)
        g_wo, _ = _wgrad(cat, dmix, bk=D, sh=D // NDEV, name=f"wgrad_out_{l}")
        dps, dwp, dsc = _pool_bwd(u, dcat, w_pool[l], ps2[l], T=T, name=f"pool_bwd_{l}")
        (du, dq, dk, dv, dsk), got = _attn_bwd(q, k4, v4, dcat, dps, sink[l], T=T, name=f"attn_bwd_{l}", carry=_scatter_2(c2))
        big[l] = [None, None, None, None, (c2[0], got[0]), (c2[1], got[1])]
        dh, dproj, n, pkm = _mixproj_bwd(h1, dh, du, dq, dk, dv, modv[l], gvec[l], wm[0], cos, sin, T=T, name=f"mixproj_bwd_{l}")
        g_wi, _ = _wgrad(dproj, n, bk=PROJ, sh=PROJ // NDEV, name=f"wgrad_in_{l}")
        (dh, dab, s, n, df, pk1), got = _ffn_bwd(h0, dh, a1, b1, f1, modv[l], gvec[l], *w1, T=T, mrow=0, grow=0,
                                                 ctx_active=True, name=f"ffn1_bwd_{l}", carry=_scatter_1([g_wi, g_wo]))
        cm = adds(f"mix_{l}", [g_wi, g_wo], got)
        mine = [pk1, pkm + pko, pk2, dwp, dsc, dsk]
        rnd = _merge(_scatter_2(cm), _gather_a(mine + [dnf, loss_part])) if l == 0 else _scatter_2(cm)
        g_w1i, got = _wgrad(dab, n, bk=WG_BK, sh=2 * DFF // NDEV, name=f"wgrad_ffn1_in_{l}", carry=rnd)
        big[l][2:4] = [(cm[0], got[0]), (cm[1], got[1])]
        g_w1o, got = _wgrad(s, df, bk=WG_BK, sh=DFF // NDEV, name=f"wgrad_ffn1_out_{l}",
                            carry=_gather_b(got[2:]) if l == 0 else None)
        prev = ([g_w1i, g_w1o], mine)
    small[0], nf_all, loss_all = got[:6], got[6], got[7]
    return loss_all, dh, small, nf_all, big, prev[0]
```

```python
import functools

import jax
import jax.numpy as jnp
from jax import lax
from jax.experimental import pallas as pl
from jax.experimental.pallas import tpu as pltpu

F32, BF16 = jnp.float32, jnp.bfloat16

D = 1024
LC = 256
DFF = 2816
NMOD = 9
PW = 512
AW = 512
KVW = 128
PROJ = PW + AW + 2 * KVW
HD = 64
BLK = 128
GRID_W = 64
POOL_WINDOWS = (2, 4, 8, 16)
EPS = 1e-6
NEG = -1e30
ROPE_BASE = 10000.0
NDEV = 8
MESH = pl.DeviceIdType.MESH

ADAM_LR, ADAM_B1, ADAM_B2, ADAM_EPS, ADAM_WD, ADAM_STEP = 0.001, 0.9, 0.999, 1e-08, 0.01, 10

VMEM_LIMIT = 56 * 1024 * 1024
TM = 256
FFN_CHUNKS = ((0, 1536), (1536, 1280))
WG_BK = 1408

ANY = pl.BlockSpec(memory_space=pl.ANY)
SMEM = pl.BlockSpec(memory_space=pltpu.SMEM)


def _params(ngrid=1):
    return pltpu.CompilerParams(dimension_semantics=("arbitrary",) * ngrid, vmem_limit_bytes=VMEM_LIMIT)


def _dot(a, b):
    return jnp.dot(a, b, preferred_element_type=F32)


def _dot_nt(a, b):
    return lax.dot_general(a, b, (((1,), (1,)), ((), ())), preferred_element_type=F32)


def _dot_tn(a, b):
    return lax.dot_general(a, b, (((0,), (0,)), ((), ())), preferred_element_type=F32)


def _sigmoid(x):
    return 1.0 / (1.0 + jnp.exp(-x))


def _rows(tm, w):
    return pl.BlockSpec((tm, w), lambda i: (i, 0))


def _full(shape):
    nd = len(shape)
    return pl.BlockSpec(shape, lambda *_: (0,) * nd)


def _sds(shape, dtype):
    return jax.ShapeDtypeStruct(shape, dtype)


def _norm_mod(h, g, shift, scale):
    r = lax.rsqrt(jnp.mean(h * h, axis=-1, keepdims=True) + EPS)
    xhat = h * r
    y = xhat * g
    return r, xhat, y, y * (1.0 + scale) + shift


def _norm_mod_bwd(dn, r, xhat, y, g, scale):
    dshift = jnp.sum(dn, axis=0, keepdims=True)
    dscale = jnp.sum(dn * y, axis=0, keepdims=True)
    dy = dn * (1.0 + scale)
    dg = jnp.sum(dy * xhat, axis=0, keepdims=True)
    dxh = dy * g
    dh = r * (dxh - xhat * jnp.mean(dxh * xhat, axis=-1, keepdims=True))
    return dh, dshift, dscale, dg


def _acc_partials(part_ref, first, rows):
    @pl.when(first)
    def _():
        part_ref[...] = jnp.zeros_like(part_ref)

    for r, val in rows.items():
        part_ref[0, r : r + 1, :] += val


def _mod_spec(n_lat):
    return pl.BlockSpec((1, 16, D), lambda i: (i // n_lat, 0, 0))


def _part_spec(n_lat):
    return pl.BlockSpec((1, 8, D), lambda i: (i // n_lat, 0, 0))


def _load_weights(pairs, sem):
    copies = [pltpu.make_async_copy(src, dst, sem.at[k]) for k, (src, dst) in enumerate(pairs)]
    for cp in copies:
        cp.start()
    for cp in copies:
        cp.wait()


def _ffn_weight_copies(win_hbm, wout_hbm, win_v, wout_v, sem):
    loads = []
    for k, (c0, cw) in enumerate(FFN_CHUNKS):
        slabs = [(win_hbm, win_v, c0), (win_hbm, win_v, DFF + c0), (wout_hbm, wout_v, c0)]
        loads.append([pltpu.make_async_copy(src.at[pl.ds(r0, cw)], dst.at[pl.ds(r0, cw)], sem.at[3 * k + j])
                      for j, (src, dst, r0) in enumerate(slabs)])
    return loads


def _ffn_steps(i, n_active, loads, compute):
    @pl.when(i == 0)
    def _():
        for cp in sum(loads, []):
            cp.start()
        compute(loads)

    @pl.when(jnp.logical_and(i > 0, i < n_active))
    def _():
        compute(None)


def _wait_chunk(loads, k):
    if loads is not None:
        for cp in loads[k]:
            cp.wait()


def _ffn_fwd(h, modv, gvec, win, wout, *, T, mrow, grow, ctx_active, name, carry=None):
    R = h.shape[0]
    n_lat, n_tiles = T // TM, R // TM
    n_active = n_tiles if ctx_active else n_lat

    def body(h_ref, mod_ref, g_ref, win_hbm, wout_hbm, ho_ref, a_ref, b_ref, f_ref, win_v, wout_v, sem):
        i = pl.program_id(0)

        def compute(loads):
            h = h_ref[...]
            shift, scale, gate = (mod_ref[0, mrow + k : mrow + k + 1, :] for k in range(3))
            _, _, _, n = _norm_mod(h, g_ref[grow : grow + 1, :], shift, scale)
            n_bf = n.astype(BF16)
            acc = jnp.zeros((TM, D), F32)
            for k, (c0, cw) in enumerate(FFN_CHUNKS):
                _wait_chunk(loads, k)
                a = _dot_nt(n_bf, win_v[c0 : c0 + cw, :])
                b = _dot_nt(n_bf, win_v[DFF + c0 : DFF + c0 + cw, :])
                a_ref[:, c0 : c0 + cw] = a.astype(BF16)
                b_ref[:, c0 : c0 + cw] = b.astype(BF16)
                s = a * _sigmoid(a) * b
                acc = acc + _dot(s.astype(BF16), wout_v[c0 : c0 + cw, :])
            f_ref[...] = acc.astype(BF16)
            ho_ref[...] = h + (0.5 * gate) * acc

        _ffn_steps(i, n_active, _ffn_weight_copies(win_hbm, wout_hbm, win_v, wout_v, sem), compute)

        @pl.when(i >= n_active)
        def _():
            ho_ref[...] = h_ref[...]
            a_ref[...] = jnp.zeros_like(a_ref)
            b_ref[...] = jnp.zeros_like(b_ref)
            f_ref[...] = jnp.zeros_like(f_ref)

    return _call(
        body,
        name=name,
        grid=(n_tiles,),
        in_specs=[_rows(TM, D), _mod_spec(n_lat), _full((8, D)), ANY, ANY],
        out_specs=[_rows(TM, D), _rows(TM, DFF), _rows(TM, DFF), _rows(TM, D)],
        out_shape=[_sds((R, D), F32), _sds((R, DFF), BF16), _sds((R, DFF), BF16), _sds((R, D), BF16)],
        scratch_shapes=[pltpu.VMEM((2 * DFF, D), BF16), pltpu.VMEM((DFF, D), BF16),
                        pltpu.SemaphoreType.DMA((3 * len(FFN_CHUNKS),))],
        args=(h, modv, gvec, win, wout),
        carry=carry,
    )


def _ffn_bwd(h, dho, a, b, f, modv, gvec, win, wout, *, T, mrow, grow, ctx_active, name, carry=None):
    R = h.shape[0]
    n_lat, n_tiles = T // TM, R // TM
    n_active = n_tiles if ctx_active else n_lat

    def body(h_ref, dho_ref, a_ref, b_ref, f_ref, mod_ref, g_ref, win_hbm, wout_hbm,
             dh_ref, dab_ref, s_ref, n_ref, df_ref, part_ref, win_v, wout_v, sem):
        i = pl.program_id(0)
        first = jnp.logical_or(i == 0, i == n_lat)

        def compute(loads):
            h = h_ref[...]
            dho = dho_ref[...]
            shift, scale, gate = (mod_ref[0, mrow + k : mrow + k + 1, :] for k in range(3))
            g = g_ref[grow : grow + 1, :]
            r, xhat, y, n = _norm_mod(h, g, shift, scale)
            dgate = 0.5 * jnp.sum(dho * f_ref[...].astype(F32), axis=0, keepdims=True)
            df_bf = ((0.5 * gate) * dho).astype(BF16)
            df_ref[...] = df_bf
            n_ref[...] = n.astype(BF16)
            dn = jnp.zeros((TM, D), F32)
            for k, (c0, cw) in enumerate(FFN_CHUNKS):
                _wait_chunk(loads, k)
                ds = _dot_nt(df_bf, wout_v[c0 : c0 + cw, :])
                av = a_ref[:, c0 : c0 + cw].astype(F32)
                bv = b_ref[:, c0 : c0 + cw].astype(F32)
                sig = _sigmoid(av)
                sa = av * sig
                s_ref[:, c0 : c0 + cw] = (sa * bv).astype(BF16)
                da = (ds * bv * (sig * (1.0 + av * (1.0 - sig)))).astype(BF16)
                db = (ds * sa).astype(BF16)
                dab_ref[:, c0 : c0 + cw] = da
                dab_ref[:, DFF + c0 : DFF + c0 + cw] = db
                dn = dn + _dot(da, win_v[c0 : c0 + cw, :]) + _dot(db, win_v[DFF + c0 : DFF + c0 + cw, :])
            dh, dshift, dscale, dg = _norm_mod_bwd(dn, r, xhat, y, g, scale)
            dh_ref[...] = dho + dh
            _acc_partials(part_ref, first, {0: dshift, 1: dscale, 2: dgate, 3: dg})

        _ffn_steps(i, n_active, _ffn_weight_copies(win_hbm, wout_hbm, win_v, wout_v, sem), compute)

        @pl.when(i >= n_active)
        def _():
            dh_ref[...] = dho_ref[...]
            dab_ref[...] = jnp.zeros_like(dab_ref)
            s_ref[...] = jnp.zeros_like(s_ref)
            n_ref[...] = jnp.zeros_like(n_ref)
            df_ref[...] = jnp.zeros_like(df_ref)
            part_ref[...] = jnp.zeros_like(part_ref)

    return _call(
        body,
        name=name,
        grid=(n_tiles,),
        in_specs=[_rows(TM, D), _rows(TM, D), _rows(TM, DFF), _rows(TM, DFF), _rows(TM, D),
                  _mod_spec(n_lat), _full((8, D)), ANY, ANY],
        out_specs=[_rows(TM, D), _rows(TM, 2 * DFF), _rows(TM, DFF), _rows(TM, D), _rows(TM, D), _part_spec(n_lat)],
        out_shape=[_sds((R, D), F32), _sds((R, 2 * DFF), BF16), _sds((R, DFF), BF16), _sds((R, D), BF16),
                   _sds((R, D), BF16), _sds((2, 8, D), F32)],
        scratch_shapes=[pltpu.VMEM((2 * DFF, D), BF16), pltpu.VMEM((DFF, D), BF16),
                        pltpu.SemaphoreType.DMA((3 * len(FFN_CHUNKS),))],
        args=(h, dho, a, b, f, modv, gvec, win, wout),
        carry=carry,
    )


def _wgrad(x, y, *, bk, sh, name, carry=None):
    R, kx = x.shape
    n = y.shape[1]
    tr = R // 2
    nr, nsh = R // tr, bk // sh

    def body(x_ref, y_ref, o_ref, acc):
        r = pl.program_id(1)

        @pl.when(r == 0)
        def _():
            acc[...] = jnp.zeros_like(acc)

        acc[...] += _dot_tn(x_ref[...], y_ref[...])

        @pl.when(r == nr - 1)
        def _():
            for s in range(nsh):
                o_ref[s] = acc[s * sh : (s + 1) * sh, :].astype(BF16)

    (out,), got = _call(
        body,
        name=name,
        grid=(kx // bk, nr),
        in_specs=[pl.BlockSpec((tr, bk), lambda k, r: (r, k)), pl.BlockSpec((tr, n), lambda k, r: (r, 0))],
        out_specs=[pl.BlockSpec((nsh, sh, n), lambda k, r: (k, 0, 0))],
        out_shape=[_sds((kx // sh, sh, n), BF16)],
        scratch_shapes=[pltpu.VMEM((bk, n), F32)],
        args=(x, y),
        carry=carry,
    )
    return out, got


def _rot_half(x):
    lane = lax.broadcasted_iota(jnp.int32, x.shape, 1)
    return jnp.where((lane & (HD - 1)) < HD // 2, -pltpu.roll(x, 128 - HD // 2, 1), pltpu.roll(x, HD // 2, 1))


def _tile_sel():
    i = lax.broadcasted_iota(jnp.int32, (KVW, AW), 0)
    j = lax.broadcasted_iota(jnp.int32, (KVW, AW), 1)
    return jnp.where(i == (j // 256) * HD + (j & (HD - 1)), 1.0, 0.0).astype(BF16)


def _mixproj_fwd(h, modv, gvec, win, cos, sin, *, T, name, carry=None):
    R = h.shape[0]
    n_lat, n_tiles = T // TM, R // TM

    def body(h_ref, mod_ref, g_ref, win_ref, cos_ref, sin_ref, u_ref, q_ref, k4_ref, v4_ref):
        shift, scale = mod_ref[0, 3:4, :], mod_ref[0, 4:5, :]
        _, _, _, n = _norm_mod(h_ref[...], g_ref[1:2, :], shift, scale)
        proj = _dot_nt(n.astype(BF16), win_ref[...])
        u_ref[...] = proj[:, :PW]
        cs, sn = cos_ref[...], sin_ref[...]
        for s in range(AW // 128):
            x = proj[:, PW + 128 * s : PW + 128 * (s + 1)]
            q_ref[:, 128 * s : 128 * (s + 1)] = ((x * cs + _rot_half(x) * sn) * (HD ** -0.5)).astype(BF16)
        k = proj[:, PW + AW : PW + AW + KVW]
        k = (k * cs + _rot_half(k) * sn).astype(BF16)
        v = proj[:, PW + AW + KVW :].astype(BF16)
        sel = _tile_sel()
        k4_ref[...] = _dot(k, sel).astype(BF16)
        v4_ref[...] = _dot(v, sel).astype(BF16)

    return _call(
        body,
        name=name,
        grid=(n_tiles,),
        in_specs=[_rows(TM, D), _mod_spec(n_lat), _full((8, D)), _full((PROJ, D)), _rows(TM, 128), _rows(TM, 128)],
        out_specs=[_rows(TM, PW), _rows(TM, AW), _rows(TM, AW), _rows(TM, AW)],
        out_shape=[_sds((R, PW), F32), _sds((R, AW), BF16), _sds((R, AW), BF16), _sds((R, AW), BF16)],
        scratch_shapes=[],
        args=(h, modv, gvec, win, cos, sin),
        carry=carry,
    )


def _win_start(j, hi):
    return pl.multiple_of(jnp.clip((j - 1) * BLK, 0, hi - 3 * BLK), BLK)


def _hi_lo(x):
    hi = x.astype(BF16)
    return hi, (x - hi.astype(F32)).astype(BF16)


def _pool_bounds(t, w, T, R):
    is_ctx = t >= T
    lo = jnp.maximum(t - w // 2, jnp.where(is_ctx, T, 0))
    hi = jnp.minimum(t + w // 2, jnp.where(is_ctx, R, T))
    return lo, hi


def _pooled(u_v, j, T, R):
    start = _win_start(j, R)
    u3_hi, u3_lo = _hi_lo(u_v[pl.ds(start, 3 * BLK), :])
    ub = u_v[pl.ds(pl.multiple_of(j * BLK, BLK), BLK), :]
    t = j * BLK + lax.broadcasted_iota(jnp.int32, (BLK, 1), 0)
    pos = start + lax.broadcasted_iota(jnp.int32, (1, 3 * BLK), 1)
    pooled, counts = [], []
    for g, w in enumerate(POOL_WINDOWS):
        lo, hi = _pool_bounds(t, w, T, R)
        band = jnp.where(pos >= lo, jnp.where(pos < hi, 1.0, 0.0), 0.0).astype(BF16)
        sl = slice(g * 128, (g + 1) * 128)
        sums = _dot(band, u3_hi[:, sl]) + _dot(band, u3_lo[:, sl])
        cnt = (hi - lo).astype(F32)
        pooled.append(sums / cnt - ub[:, sl])
        counts.append(cnt)
    return pooled, counts


def _stack_heads(x):
    lane_h = lax.broadcasted_iota(jnp.int32, x.shape, 1) // HD
    return jnp.concatenate([jnp.where(lane_h == h, x, jnp.zeros_like(x)) for h in range(4)], axis=0)


def _unstack_heads(x):
    lane_h = lax.broadcasted_iota(jnp.int32, (BLK, 256), 1) // HD
    out = jnp.zeros((BLK, 256), F32)
    for h in range(4):
        out = out + jnp.where(lane_h == h, x[h * BLK : (h + 1) * BLK, :], 0.0)
    return out


def _window_mask(j, start_l, nbl):
    rowi = lax.broadcasted_iota(jnp.int32, (4 * BLK, 1), 0)
    qpos = j * BLK + (rowi & (BLK - 1))
    kpos = start_l + lax.broadcasted_iota(jnp.int32, (1, 3 * BLK), 1)
    reach = jnp.where(j < nbl, BLK, -1)
    return jnp.abs(kpos - qpos) <= reach


def _attn_exps(qs, kl, kc, sink_ref, g, valid):
    s_l = jnp.where(valid, _dot_nt(qs, kl), NEG)
    s_c = _dot_nt(qs, kc)
    rb = lax.broadcasted_iota(jnp.int32, (4 * BLK, 1), 0) // BLK
    sk = jnp.where(rb == 0, sink_ref[4 * g], jnp.where(rb == 1, sink_ref[4 * g + 1],
                   jnp.where(rb == 2, sink_ref[4 * g + 2], sink_ref[4 * g + 3])))
    m = jnp.maximum(jnp.maximum(jnp.max(s_l, axis=1, keepdims=True), jnp.max(s_c, axis=1, keepdims=True)), sk)
    e_l, e_c, e_s = jnp.exp(s_l - m), jnp.exp(s_c - m), jnp.exp(sk - m)
    inv = 1.0 / (jnp.sum(e_l, axis=1, keepdims=True) + jnp.sum(e_c, axis=1, keepdims=True) + e_s)
    return e_l, e_c, e_s, inv


def _attnpool_fwd(u, q, k4, v4, sink, w_pool, pool_scale, *, T, name, carry=None):
    R = u.shape[0]
    nb, nbl = R // BLK, T // BLK

    def body(q_ref, sink_ref, wp_ref, ps_ref, u_hbm, k4_hbm, v4_hbm, cat_ref, u_v, k4_v, v4_v, sem):
        j = pl.program_id(0)

        @pl.when(j == 0)
        def _():
            _load_weights([(u_hbm, u_v), (k4_hbm, k4_v), (v4_hbm, v4_v)], sem)

        pooled, _ = _pooled(u_v, j, T, R)
        for g in range(4):
            mixed = _dot(pooled[g].astype(BF16), wp_ref[g].astype(BF16)) * ps_ref[:, g * 128 : (g + 1) * 128]
            cat_ref[:, g * 128 : (g + 1) * 128] = mixed.astype(BF16)

        start_l = _win_start(j, T)
        valid = _window_mask(j, start_l, nbl)
        for g in range(2):
            gl = slice(g * 256, (g + 1) * 256)
            qs = _stack_heads(q_ref[:, gl])
            e_l, e_c, _, inv = _attn_exps(qs, k4_v[pl.ds(start_l, 3 * BLK), gl], k4_v[T:R, gl], sink_ref, g, valid)
            o = _dot(e_l.astype(BF16), v4_v[pl.ds(start_l, 3 * BLK), gl]) + _dot(e_c.astype(BF16), v4_v[T:R, gl])
            cat_ref[:, PW + g * 256 : PW + (g + 1) * 256] = _unstack_heads(o * inv).astype(BF16)

    return _call(
        body,
        name=name,
        grid=(nb,),
        in_specs=[_rows(BLK, AW), SMEM, _full((4, 128, 128)), _full((1, PW)), ANY, ANY, ANY],
        out_specs=[_rows(BLK, D)],
        out_shape=[_sds((R, D), BF16)],
        scratch_shapes=[pltpu.VMEM((R, PW), F32), pltpu.VMEM((R, AW), BF16), pltpu.VMEM((R, AW), BF16),
                        pltpu.SemaphoreType.DMA((3,))],
        args=(q, sink, w_pool, pool_scale, u, k4, v4),
        carry=carry,
    )


def _mixout_fwd(h, cat, modv, wout, *, T, ctx_active, name, carry=None):
    R = h.shape[0]
    n_lat, n_tiles = T // TM, R // TM

    def body(h_ref, cat_ref, mod_ref, w_ref, ho_ref, mo_ref):
        i = pl.program_id(0)

        def compute():
            mo = _dot(cat_ref[...], w_ref[...])
            mo_ref[...] = mo.astype(BF16)
            ho_ref[...] = h_ref[...] + mod_ref[0, 5:6, :] * mo

        if ctx_active:
            compute()
        else:
            pl.when(i < n_lat)(compute)

            @pl.when(i >= n_lat)
            def _():
                ho_ref[...] = h_ref[...]
                mo_ref[...] = jnp.zeros_like(mo_ref)

    return _call(
        body,
        name=name,
        grid=(n_tiles,),
        in_specs=[_rows(TM, D), _rows(TM, D), _mod_spec(n_lat), _full((D, D))],
        out_specs=[_rows(TM, D), _rows(TM, D)],
        out_shape=[_sds((R, D), F32), _sds((R, D), BF16)],
        scratch_shapes=[],
        args=(h, cat, modv, wout),
        carry=carry,
    )


def _mixout_bwd(dho, mo, modv, wout, *, T, ctx_active, name, carry=None):
    R = dho.shape[0]
    n_lat, n_tiles = T // TM, R // TM

    def body(dho_ref, mo_ref, mod_ref, w_ref, dcat_ref, dmix_ref, part_ref):
        i = pl.program_id(0)
        first = jnp.logical_or(i == 0, i == n_lat)

        def compute():
            dho = dho_ref[...]
            dmix = (mod_ref[0, 5:6, :] * dho).astype(BF16)
            dmix_ref[...] = dmix
            dcat_ref[...] = _dot_nt(dmix, w_ref[...])
            dgate = jnp.sum(dho * mo_ref[...].astype(F32), axis=0, keepdims=True)
            _acc_partials(part_ref, first, {2: dgate})

        if ctx_active:
            compute()
        else:
            pl.when(i < n_lat)(compute)

            @pl.when(i >= n_lat)
            def _():
                dcat_ref[...] = jnp.zeros_like(dcat_ref)
                dmix_ref[...] = jnp.zeros_like(dmix_ref)
                part_ref[...] = jnp.zeros_like(part_ref)

    return _call(
        body,
        name=name,
        grid=(n_tiles,),
        in_specs=[_rows(TM, D), _rows(TM, D), _mod_spec(n_lat), _full((D, D))],
        out_specs=[_rows(TM, D), _rows(TM, D), _part_spec(n_lat)],
        out_shape=[_sds((R, D), F32), _sds((R, D), BF16), _sds((2, 8, D), F32)],
        scratch_shapes=[],
        args=(dho, mo, modv, wout),
        carry=carry,
    )


def _pool_bwd(u, dcat, w_pool, pool_scale, *, T, name):
    R = u.shape[0]
    nb = R // BLK

    def body(dcat_ref, wp_ref, ps_ref, u_hbm, dps_ref, dwp_ref, dsc_ref, u_v, sem):
        j = pl.program_id(0)

        @pl.when(j == 0)
        def _():
            _load_weights([(u_hbm, u_v)], sem)
            dwp_ref[...] = jnp.zeros_like(dwp_ref)
            dsc_ref[...] = jnp.zeros_like(dsc_ref)

        pooled, counts = _pooled(u_v, j, T, R)
        for g in range(4):
            sl = slice(g * 128, (g + 1) * 128)
            p_bf = pooled[g].astype(BF16)
            w_bf = wp_ref[g].astype(BF16)
            dmixed = dcat_ref[:, sl]
            dsc_ref[0:1, sl] += jnp.sum(dmixed * _dot(p_bf, w_bf), axis=0, keepdims=True)
            dmp = (dmixed * ps_ref[:, sl]).astype(BF16)
            dwp_ref[sl, :] += _dot_tn(p_bf, dmp)
            dps_ref[:, sl] = _dot_nt(dmp, w_bf) / counts[g]

    return pl.pallas_call(
        body,
        name=name,
        grid=(nb,),
        in_specs=[_rows(BLK, D), _full((4, 128, 128)), _full((1, PW)), ANY],
        out_specs=[_rows(BLK, PW), _full((PW, 128)), _full((8, PW))],
        out_shape=[_sds((R, PW), F32), _sds((PW, 128), F32), _sds((8, PW), F32)],
        scratch_shapes=[pltpu.VMEM((R, PW), F32), pltpu.SemaphoreType.DMA((1,))],
        compiler_params=_params(),
    )(dcat, w_pool, pool_scale, u)


def _fold_heads(x):
    y = x[:, :128] + x[:, 128:]
    return y + pltpu.roll(y, HD, 1)


def _attn_bwd(q, k4, v4, dcat, dps, sink, *, T, name, carry=None):
    R = q.shape[0]
    nb, nbl = R // BLK, T // BLK

    def body(q_ref, dcat_ref, sink_ref, k4_hbm, v4_hbm, dps_hbm, du_ref, dq_ref, dk_ref, dv_ref, dsk_ref,
             k4_v, v4_v, dps_v, sem):
        j = pl.program_id(0)

        @pl.when(j == 0)
        def _():
            _load_weights([(k4_hbm, k4_v), (v4_hbm, v4_v), (dps_hbm, dps_v)], sem)
            dk_ref[...] = jnp.zeros_like(dk_ref)
            dv_ref[...] = jnp.zeros_like(dv_ref)
            dsk_ref[...] = jnp.zeros_like(dsk_ref)

        start = _win_start(j, R)
        d3_hi, d3_lo = _hi_lo(dps_v[pl.ds(start, 3 * BLK), :])
        db = dps_v[pl.ds(pl.multiple_of(j * BLK, BLK), BLK), :]
        pos = j * BLK + lax.broadcasted_iota(jnp.int32, (BLK, 1), 0)
        t_r = start + lax.broadcasted_iota(jnp.int32, (1, 3 * BLK), 1)
        for g, w in enumerate(POOL_WINDOWS):
            sl = slice(g * 128, (g + 1) * 128)
            lo_r, hi_r = _pool_bounds(t_r, w, T, R)
            band_t = jnp.where(pos >= lo_r, jnp.where(pos < hi_r, 1.0, 0.0), 0.0).astype(BF16)
            lo_c, hi_c = _pool_bounds(pos, w, T, R)
            du_ref[:, sl] = _dot(band_t, d3_hi[:, sl]) + _dot(band_t, d3_lo[:, sl]) - db[:, sl] * (hi_c - lo_c).astype(F32)

        start_l = _win_start(j, T)
        valid = _window_mask(j, start_l, nbl)
        rb = lax.broadcasted_iota(jnp.int32, (4 * BLK, 1), 0) // BLK
        lane = lax.broadcasted_iota(jnp.int32, (1, 128), 1)
        dk_l, dk_c, dv_l, dv_c = [], [], [], []
        for g in range(2):
            gl = slice(g * 256, (g + 1) * 256)
            qs = _stack_heads(q_ref[:, gl])
            kl, kc = k4_v[pl.ds(start_l, 3 * BLK), gl], k4_v[T:R, gl]
            vl, vc = v4_v[pl.ds(start_l, 3 * BLK), gl], v4_v[T:R, gl]
            e_l, e_c, e_s, inv = _attn_exps(qs, kl, kc, sink_ref, g, valid)
            p_l, p_c, p_s = e_l * inv, e_c * inv, e_s * inv
            dos = _stack_heads(dcat_ref[:, PW + g * 256 : PW + (g + 1) * 256]).astype(BF16)
            dp_l, dp_c = _dot_nt(dos, vl), _dot_nt(dos, vc)
            delta = jnp.sum(p_l * dp_l, axis=1, keepdims=True) + jnp.sum(p_c * dp_c, axis=1, keepdims=True)
            ds_l = (p_l * (dp_l - delta)).astype(BF16)
            ds_c = (p_c * (dp_c - delta)).astype(BF16)
            dq_ref[:, gl] = _unstack_heads(_dot(ds_l, kl) + _dot(ds_c, kc)) * (HD ** -0.5)
            dk_l.append(_fold_heads(_dot_tn(ds_l, qs)))
            dk_c.append(_fold_heads(_dot_tn(ds_c, qs)))
            dv_l.append(_fold_heads(_dot_tn(p_l.astype(BF16), dos)))
            dv_c.append(_fold_heads(_dot_tn(p_c.astype(BF16), dos)))
            dsink = -p_s * delta
            for h in range(4):
                tot = jnp.sum(jnp.where(rb == h, dsink, 0.0), axis=0, keepdims=True)
                dsk_ref[4 * g + h : 4 * g + h + 1, :] += jnp.broadcast_to(tot, (1, 128))
        first = lane < HD
        dk_ref[pl.ds(start_l, 3 * BLK), :] += jnp.where(first, dk_l[0], dk_l[1])
        dk_ref[T:R, :] += jnp.where(first, dk_c[0], dk_c[1])
        dv_ref[pl.ds(start_l, 3 * BLK), :] += jnp.where(first, dv_l[0], dv_l[1])
        dv_ref[T:R, :] += jnp.where(first, dv_c[0], dv_c[1])

    return _call(
        body,
        name=name,
        grid=(nb,),
        in_specs=[_rows(BLK, AW), _rows(BLK, D), SMEM, ANY, ANY, ANY],
        out_specs=[_rows(BLK, PW), _rows(BLK, AW), _full((R, KVW)), _full((R, KVW)), _full((8, 128))],
        out_shape=[_sds((R, PW), F32), _sds((R, AW), F32), _sds((R, KVW), F32), _sds((R, KVW), F32),
                   _sds((8, 128), F32)],
        scratch_shapes=[pltpu.VMEM((R, AW), BF16), pltpu.VMEM((R, AW), BF16), pltpu.VMEM((R, PW), F32),
                        pltpu.SemaphoreType.DMA((3,))],
        args=(q, dcat, sink, k4, v4, dps),
        carry=carry,
    )


def _mixproj_bwd(h, dho, du, dq, dk, dv, modv, gvec, win, cos, sin, *, T, name):
    R = h.shape[0]
    n_lat, n_tiles = T // TM, R // TM

    def body(h_ref, dho_ref, du_ref, dq_ref, dk_ref, dv_ref, mod_ref, g_ref, win_ref, cos_ref, sin_ref,
             dh_ref, dproj_ref, n_ref, part_ref):
        i = pl.program_id(0)
        first = jnp.logical_or(i == 0, i == n_lat)
        shift, scale = mod_ref[0, 3:4, :], mod_ref[0, 4:5, :]
        g = g_ref[1:2, :]
        r, xhat, y, n = _norm_mod(h_ref[...], g, shift, scale)
        n_ref[...] = n.astype(BF16)
        cs, sn = cos_ref[...], sin_ref[...]
        dproj_ref[:, :PW] = du_ref[...].astype(BF16)
        for s in range(AW // 128):
            x = dq_ref[:, 128 * s : 128 * (s + 1)]
            dproj_ref[:, PW + 128 * s : PW + 128 * (s + 1)] = (x * cs - _rot_half(x) * sn).astype(BF16)
        x = dk_ref[...]
        dproj_ref[:, PW + AW : PW + AW + KVW] = (x * cs - _rot_half(x) * sn).astype(BF16)
        dproj_ref[:, PW + AW + KVW :] = dv_ref[...].astype(BF16)
        dn = _dot(dproj_ref[...], win_ref[...])
        dh, dshift, dscale, dg = _norm_mod_bwd(dn, r, xhat, y, g, scale)
        dh_ref[...] = dho_ref[...] + dh
        _acc_partials(part_ref, first, {0: dshift, 1: dscale, 3: dg})

    return pl.pallas_call(
        body,
        name=name,
        grid=(n_tiles,),
        in_specs=[_rows(TM, D), _rows(TM, D), _rows(TM, PW), _rows(TM, AW), _rows(TM, KVW), _rows(TM, KVW),
                  _mod_spec(n_lat), _full((8, D)), _full((PROJ, D)), _rows(TM, 128), _rows(TM, 128)],
        out_specs=[_rows(TM, D), _rows(TM, PROJ), _rows(TM, D), _part_spec(n_lat)],
        out_shape=[_sds((R, D), F32), _sds((R, PROJ), BF16), _sds((R, D), BF16), _sds((2, 8, D), F32)],
        compiler_params=_params(),
    )(h, dho, du, dq, dk, dv, modv, gvec, win, cos, sin)


def _loss_head(h, target, g_final, *, T, name):
    R = h.shape[0]
    n_lat, n_tiles = T // TM, R // TM

    def body(h_ref, t_ref, g_ref, dh_ref, loss_ref, dg_ref):
        i = pl.program_id(0)

        @pl.when(i == 0)
        def _():
            loss_ref[...] = jnp.zeros_like(loss_ref)
            dg_ref[...] = jnp.zeros_like(dg_ref)

        @pl.when(i < n_lat)
        def _():
            h = h_ref[...]
            g = g_ref[...]
            r = lax.rsqrt(jnp.mean(h * h, axis=-1, keepdims=True) + EPS)
            xhat = h * r
            err = xhat * g - t_ref[...]
            tot = jnp.sum(jnp.sum(err * err, axis=1, keepdims=True), axis=0, keepdims=True)
            loss_ref[...] += jnp.broadcast_to(tot * (0.5 / D), loss_ref.shape)
            dy = err * (1.0 / D)
            dg_ref[0:1, :] += jnp.sum(dy * xhat, axis=0, keepdims=True)
            dxh = dy * g
            dh_ref[...] = r * (dxh - xhat * jnp.mean(dxh * xhat, axis=-1, keepdims=True))

        @pl.when(i >= n_lat)
        def _():
            dh_ref[...] = jnp.zeros_like(dh_ref)

    return pl.pallas_call(
        body,
        name=name,
        grid=(n_tiles,),
        in_specs=[_rows(TM, D), pl.BlockSpec((TM, D), lambda i: (jnp.minimum(i, n_lat - 1), 0)), _full((1, D))],
        out_specs=[_rows(TM, D), _full((8, 128)), _full((8, D))],
        out_shape=[_sds((R, D), F32), _sds((8, 128), F32), _sds((8, D), F32)],
        compiler_params=_params(),
    )(h, target, g_final)


def _mod_fwd(c16, w_mod, b_cols, *, name):
    nl, _, cols = w_mod.shape

    def body(c_ref, w_ref, b_ref, o_ref):
        c = c_ref[...]
        sc = (c * _sigmoid(c)).astype(BF16)
        o_ref[0] = _dot(sc, w_ref[0].astype(BF16)) + b_ref[0]

    return pl.pallas_call(
        body,
        name=name,
        grid=(nl,),
        in_specs=[_full((16, D)), pl.BlockSpec((1, D, cols), lambda l: (l, 0, 0)),
                  pl.BlockSpec((1, 1, cols), lambda l: (l, 0, 0))],
        out_specs=pl.BlockSpec((1, 16, cols), lambda l: (l, 0, 0)),
        out_shape=_sds((nl, 16, cols), F32),
        compiler_params=_params(),
    )(c16, w_mod, b_cols)


def _mod_bwd(c16, dm_cols, w_mod, *, name):
    nl, _, cols = w_mod.shape

    def body(c_ref, dm_ref, w_ref, gw_ref, dc_ref):
        c = c_ref[...]
        sc = (c * _sigmoid(c)).astype(BF16)
        dm = dm_ref[0].astype(BF16)
        gw_ref[0] = _dot_tn(sc, dm)
        dc_ref[0] = _dot_nt(dm, w_ref[0].astype(BF16))

    return pl.pallas_call(
        body,
        name=name,
        grid=(nl,),
        in_specs=[_full((16, D)), pl.BlockSpec((1, 16, cols), lambda l: (l, 0, 0)),
                  pl.BlockSpec((1, D, cols), lambda l: (l, 0, 0))],
        out_specs=[pl.BlockSpec((1, D, cols), lambda l: (l, 0, 0)), pl.BlockSpec((1, 16, D), lambda l: (l, 0, 0))],
        out_shape=[_sds((nl, D, cols), F32), _sds((nl, 16, D), F32)],
        compiler_params=_params(),
    )(c16, dm_cols, w_mod)


def _coords():
    return lax.axis_index("x"), lax.axis_index("y"), lax.axis_index("c")


FWD = 8


def _peer(k, x, y, c):
    if k == FWD:
        return (x ^ (1 - c), y ^ c, c)
    return (1 - x if k & 4 else x, 1 - y if k & 2 else y, 1 - c if k & 1 else c)


def _lin(p):
    return 4 * p[0] + 2 * p[1] + p[2]


def _view(ref, slot):
    return ref if slot is None else ref.at[slot]


class _Round:
    def __init__(self, ins, out_shapes, plan, local_plan=(), n_alias=0):
        self.ins, self.out_shapes = list(ins), list(out_shapes)
        self.plan, self.local_plan, self.n_alias = list(plan), list(local_plan), n_alias
        fed = {p[3] for p in self.plan if p[0] == FWD}
        self.feeders = [n for n, p in enumerate(self.plan) if p[0] in (2, 4, 6) and p[3] in fed]

    def sems(self):
        return [pltpu.SemaphoreType.DMA((len(self.plan),)), pltpu.SemaphoreType.DMA((len(self.plan),)),
                pltpu.SemaphoreType.DMA((max(len(self.local_plan), 1),))]

    def _remote(self, in_refs, out_refs, sems, incoming, pick):
        in_refs = list(out_refs[: self.n_alias]) + list(in_refs[self.n_alias :])
        x, y, c = _coords()
        me = _lin((x, y, c))
        copies = {}
        for idx, (k, ii, sfn, oi, dfn) in enumerate(self.plan):
            if not pick(idx, "d2d" if k == 1 else "fwd" if k == FWD else "ici"):
                continue
            peer = _peer(k, x, y, c)
            sender, receiver = (_lin(peer), me) if incoming else (me, _lin(peer))
            src = out_refs[oi] if ii is None else in_refs[ii]
            copies[idx] = pltpu.make_async_remote_copy(
                src_ref=_view(src, sfn(sender, receiver)), dst_ref=_view(out_refs[oi], dfn(sender, receiver)),
                send_sem=sems[0].at[idx], recv_sem=sems[1].at[idx], device_id=peer, device_id_type=MESH)
        return copies

    def _local(self, in_refs, out_refs, sems):
        in_refs = list(out_refs[: self.n_alias]) + list(in_refs[self.n_alias :])
        me = _lin(_coords())
        return [pltpu.make_async_copy(_view(in_refs[ii], sfn(me)), _view(out_refs[oi], dfn(me)), sems[2].at[idx])
                for idx, (ii, sfn, oi, dfn) in enumerate(self.local_plan)]

    def start(self, in_refs, out_refs, sems, links=("ici", "d2d")):
        for cp in self._remote(in_refs, out_refs, sems, False, lambda n, link: link in links).values():
            cp.start()
        if "ici" in links:
            for cp in self._local(in_refs, out_refs, sems):
                cp.start()

    def mid(self, in_refs, out_refs, sems):
        if self.feeders:
            for cp in self._remote(in_refs, out_refs, sems, True, lambda n, link: n in self.feeders).values():
                cp.wait_recv()
            for cp in self._remote(in_refs, out_refs, sems, False, lambda n, link: link == "fwd").values():
                cp.start()

    def finish(self, in_refs, out_refs, sems):
        for cp in self._remote(in_refs, out_refs, sems, True, lambda n, link: n not in self.feeders).values():
            cp.wait_recv()
        for cp in self._remote(in_refs, out_refs, sems, False, lambda n, link: True).values():
            cp.wait_send()
        for cp in self._local(in_refs, out_refs, sems):
            cp.wait()


def _exchange(name, rnd):
    n_in, n_out = len(rnd.ins), len(rnd.out_shapes)

    def body(*refs):
        in_refs, out_refs, sems = refs[:n_in], refs[n_in : n_in + n_out], refs[n_in + n_out :]
        rnd.start(in_refs, out_refs, sems)
        rnd.mid(in_refs, out_refs, sems)
        rnd.finish(in_refs, out_refs, sems)

    return pl.pallas_call(
        body, name=name, in_specs=[ANY] * n_in, out_specs=[ANY] * n_out, out_shape=rnd.out_shapes,
        scratch_shapes=rnd.sems(), input_output_aliases={i: i for i in range(rnd.n_alias)})(*rnd.ins)


def _call(body, *, name, grid, in_specs, out_specs, out_shape, scratch_shapes, args, carry=None):
    params = _params(len(grid))
    if carry is None:
        outs = pl.pallas_call(body, name=name, grid=grid, in_specs=in_specs, out_specs=out_specs, out_shape=out_shape,
                              scratch_shapes=scratch_shapes, compiler_params=params)(*args)
        return list(outs), []
    n_ci, n_co, n_cs = len(in_specs), len(out_shape), len(scratch_shapes)
    n_xi, n_xo = len(carry.ins), len(carry.out_shapes)

    def wrapped(*refs):
        ci, xi = refs[:n_ci], refs[n_ci : n_ci + n_xi]
        o0 = n_ci + n_xi
        co, xo = refs[o0 : o0 + n_co], refs[o0 + n_co : o0 + n_co + n_xo]
        s0 = o0 + n_co + n_xo
        cs, sems = refs[s0 : s0 + n_cs], refs[s0 + n_cs :]
        ids = [pl.program_id(a) for a in range(len(grid))]
        first = functools.reduce(jnp.logical_and, [i == 0 for i in ids])
        last = functools.reduce(jnp.logical_and, [i == g - 1 for i, g in zip(ids, grid)])

        @pl.when(first)
        def _():
            carry.start(xi, xo, sems, links=("ici",))

        if carry.feeders:
            step = functools.reduce(lambda acc, ig: acc * ig[1] + ig[0], zip(ids, grid), 0)
            n_steps = functools.reduce(lambda a, b: a * b, grid)

            @pl.when(step == min(n_steps - 1, (3 * n_steps) // 5))
            def _():
                carry.mid(xi, xo, sems)

        body(*ci, *co, *cs)

        @pl.when(first)
        def _():
            carry.start(xi, xo, sems, links=("d2d",))

        @pl.when(last)
        def _():
            carry.finish(xi, xo, sems)

    outs = pl.pallas_call(
        wrapped, name=name, grid=grid, in_specs=list(in_specs) + [ANY] * n_xi, out_specs=list(out_specs) + [ANY] * n_xo,
        out_shape=list(out_shape) + carry.out_shapes, scratch_shapes=list(scratch_shapes) + carry.sems(),
        input_output_aliases={n_ci + i: n_co + i for i in range(carry.n_alias)}, compiler_params=params,
    )(*args, *carry.ins)
    return list(outs[:n_co]), list(outs[n_co:])


def _gather_direct(arrays):
    na = len(arrays)
    outs = [_sds((NDEV,) + a.shape, a.dtype) for a in arrays]
    plan = [(k, i, lambda s, r: None, i, lambda s, r: s) for i in range(na) for k in range(1, NDEV)]
    return _Round(arrays, outs, plan, [(i, lambda m: None, i, lambda m: m) for i in range(na)])


def _gather_a(arrays):
    na = len(arrays)
    outs = [_sds((NDEV,) + a.shape, a.dtype) for a in arrays]
    plan = [(k, i, lambda s, r: None, i, lambda s, r: s) for i in range(na) for k in (2, 4)]
    handed = lambda s, r: s ^ (2 << (s & 1))
    plan += [(FWD, None, handed, i, handed) for i in range(na)]
    return _Round(arrays, outs, plan, [(i, lambda m: None, i, lambda m: m) for i in range(na)])


def _gather_b(got):
    na = len(got)
    plan = [(1, i, (lambda s, r, k=k: s ^ k), i, (lambda s, r, k=k: s ^ k)) for i in range(na) for k in (0, 2, 4, 6)]
    return _Round(got, [_sds(g.shape, g.dtype) for g in got], plan, n_alias=na)


def _scatter_1(grads):
    plan = [(1, i, (lambda s, r, q=q: 2 * q + (r & 1)), i, (lambda s, r, q=q: q))
            for i in range(len(grads)) for q in range(4)]
    return _Round(grads, [_sds((4,) + g.shape[1:], g.dtype) for g in grads], plan)


def _scatter_2(chip):
    plan = [(k, i, lambda s, r: r >> 1, i, (lambda s, r, j=j: j)) for i in range(len(chip)) for j, k in enumerate((2, 4, 6))]
    return _Round(chip, [_sds((3,) + g.shape[1:], g.dtype) for g in chip], plan)


def _add_pairs(g, got, pos, *, name):
    _, sh, w = g.shape

    def body(pos_ref, g_ref, r_ref, o_ref):
        o_ref[...] = (g_ref[...].astype(F32) + r_ref[...].astype(F32)).astype(o_ref.dtype)

    return pl.pallas_call(
        body,
        name=name,
        grid_spec=pltpu.PrefetchScalarGridSpec(
            num_scalar_prefetch=1, grid=(4,),
            in_specs=[pl.BlockSpec((1, sh, w), lambda q, p: (2 * q + p[0], 0, 0)),
                      pl.BlockSpec((1, sh, w), lambda q, p: (q, 0, 0))],
            out_specs=pl.BlockSpec((1, sh, w), lambda q, p: (q, 0, 0))),
        out_shape=_sds((4, sh, w), g.dtype),
        compiler_params=_params(),
    )(pos, g, got)


def _sum_adamw(chip, got, pos, w, m, v, layer, prior, *, transpose, name):
    _, sh, wd = chip.shape
    nl, rows, cols = w.shape
    if transpose:
        nb, blk = 4, (wd // 4, cols)
        part = lambda n: pl.BlockSpec((n, sh, wd // 4), lambda i, p: ((p[1] if n == 1 else 0), 0, i))
    else:
        nb, blk = 2, (sh // 2, wd)
        part = lambda n: pl.BlockSpec((n, sh // 2, wd), lambda i, p: ((p[1] if n == 1 else 0), i, 0))
    mine = pl.BlockSpec(blk, lambda i, p: (layer * nb + i, 0))
    flat = lambda t: t.reshape(nl * rows, cols)
    n_prior = 0 if prior is None else 4

    def body(pos_ref, c_ref, r_ref, w_ref, m_ref, v_ref, *refs):
        g_ref, d_ref, m2_ref, v2_ref = refs[n_prior:]
        acc = c_ref[0].astype(F32)
        for s in range(3):
            acc = acc + r_ref[s].astype(F32)
        g = acc.T if transpose else acc
        g_ref[...] = g
        d_ref[...], m2_ref[...], v2_ref[...] = _adamw_math(w_ref[...], g, m_ref[...], v_ref[...])

    outs = pl.pallas_call(
        body,
        name=name,
        grid_spec=pltpu.PrefetchScalarGridSpec(
            num_scalar_prefetch=1, grid=(nb,),
            in_specs=[part(1), part(3), mine, mine, mine] + [ANY] * n_prior,
            out_specs=[mine] * 4),
        out_shape=[_sds((nl * rows, cols), F32)] * 4,
        input_output_aliases={6 + k: k for k in range(n_prior)},
        compiler_params=_params(),
    )(pos, chip, got, flat(w), flat(m), flat(v), *(flat(t) for t in prior or ()))
    return [o.reshape(w.shape) for o in outs]


def _adamw_math(w, g, m, v):
    m2 = ADAM_B1 * m + (1.0 - ADAM_B1) * g
    v2 = ADAM_B2 * v + (1.0 - ADAM_B2) * (g * g)
    m_hat = m2 / (1.0 - ADAM_B1 ** ADAM_STEP)
    v_hat = v2 / (1.0 - ADAM_B2 ** ADAM_STEP)
    delta = -ADAM_LR * (m_hat / (jnp.sqrt(v_hat) + ADAM_EPS) + ADAM_WD * w)
    return delta, m2, v2


def _adamw(w, g, m, v, *, name, carry=None):
    shape = w.shape
    flat = [t.reshape(-1, shape[-1]) for t in (w, g, m, v)]
    rows, cols = flat[0].shape
    tr = rows // 8 if rows % 64 == 0 else rows
    spec = _rows(tr, cols)

    def body(w_ref, g_ref, m_ref, v_ref, d_ref, m2_ref, v2_ref):
        d_ref[...], m2_ref[...], v2_ref[...] = _adamw_math(w_ref[...], g_ref[...], m_ref[...], v_ref[...])

    outs, got = _call(body, name=name, grid=(rows // tr,), in_specs=[spec] * 4, out_specs=[spec] * 3,
                      out_shape=[_sds((rows, cols), F32)] * 3, scratch_shapes=[], args=flat, carry=carry)
    return tuple(o.reshape(shape) for o in outs), got


def _adds(tag, grads, got, *, pos):
    return [_add_pairs(g, r, pos, name=f"rs_add_{tag}_{i}") for i, (g, r) in enumerate(zip(grads, got))]


def _small_sums(packets, nf, dwp, dsc, dsk, *, name):
    flat = [p for layer in packets for p in layer]

    def total(ref, *idx):
        acc = ref[(0,) + idx]
        for dev in range(1, NDEV):
            acc = acc + ref[(dev,) + idx]
        return acc

    def body(*refs):
        pk = refs[:6]
        nf_ref, dwp0, dwp1, dsc0, dsc1, dsk0, dsk1 = refs[6:13]
        dm_ref, gb_ref, gn_ref, gnf_ref, gwp_ref, gps_ref, gsk_ref = refs[13:]
        dm_ref[...] = jnp.zeros_like(dm_ref)
        gn_ref[...] = jnp.zeros_like(gn_ref)
        for l in range(2):
            for sb in range(3):
                p = pk[3 * l + sb]
                for r in range(3):
                    col = slice((3 * sb + r) * D, (3 * sb + r + 1) * D)
                    lat = p[0, 0, r : r + 1, :]
                    dm_ref[l, 0:1, col] = lat
                    for dev in range(1, NDEV):
                        row = p[dev, 0, r : r + 1, :]
                        dm_ref[l, dev : dev + 1, col] = row
                        lat = lat + row
                    ctx = total(p, 1, slice(r, r + 1), slice(None))
                    dm_ref[l, 8:9, col] = ctx
                    gb_ref[l : l + 1, col] = lat + ctx
                gn_ref[l, sb : sb + 1, :] = total(p, 0, slice(3, 4), slice(None)) + total(p, 1, slice(3, 4), slice(None))
        gnf_ref[...] = total(nf_ref, slice(0, 1), slice(None))
        for l, (a, b, c) in enumerate(((dwp0, dsc0, dsk0), (dwp1, dsc1, dsk1))):
            gwp_ref[l] = total(a, slice(None), slice(None))
            gps_ref[l : l + 1, :] = total(b, slice(0, 1), slice(None))
            gsk_ref[l] = total(c, slice(None), slice(None))

    ins = flat + [nf, dwp[0], dwp[1], dsc[0], dsc[1], dsk[0], dsk[1]]
    return pl.pallas_call(
        body,
        name=name,
        out_shape=[_sds((2, 16, NMOD * D), F32), _sds((2, NMOD * D), F32), _sds((2, 8, D), F32), _sds((1, D), F32),
                   _sds((2, PW, 128), F32), _sds((2, PW), F32), _sds((2, 8, 128), F32)],
        compiler_params=pltpu.CompilerParams(vmem_limit_bytes=VMEM_LIMIT),
    )(*ins)


def _small_adamw(c_ctx, dc_all, triples, *, name):
    n = len(triples)

    def body(*refs):
        c_ref, dc_ref = refs[0], refs[1]
        ins = refs[2 : 2 + 4 * n - 1]
        outs = refs[2 + 4 * n - 1 :]
        acc = dc_ref[0, 0, 8:9, :] + dc_ref[0, 1, 8:9, :]
        for dev in range(1, NDEV):
            acc = acc + (dc_ref[dev, 0, 8:9, :] + dc_ref[dev, 1, 8:9, :])
        c = c_ref[...]
        sig = _sigmoid(c)
        g_c = acc * (sig * (1.0 + c * (1.0 - sig)))
        outs[0][...] = g_c
        pos = 0
        for k in range(n):
            if k == 0:
                w, g, m, v = ins[0][...], g_c, ins[1][...], ins[2][...]
                pos = 3
            else:
                w, g, m, v = (ins[pos + t][...] for t in range(4))
                pos += 4
            d, m2, v2 = _adamw_math(w, g, m, v)
            outs[1 + 3 * k][...], outs[2 + 3 * k][...], outs[3 + 3 * k][...] = d, m2, v2

    flat_in = [c_ctx, dc_all]
    out_shape = [_sds(c_ctx.shape, F32)]
    for k, (w, g, m, v) in enumerate(triples):
        flat_in += [w, m, v] if k == 0 else [w, g, m, v]
        out_shape += [_sds(w.shape, F32)] * 3
    return pl.pallas_call(body, name=name, out_shape=out_shape,
                          compiler_params=pltpu.CompilerParams(vmem_limit_bytes=VMEM_LIMIT))(*flat_in)


def _rope_tables(T, R):
    t = jnp.arange(T)
    inv = ROPE_BASE ** (-jnp.arange(0, HD // 2, 2, dtype=F32) / (HD // 2))
    ang = jnp.concatenate([(t // GRID_W).astype(F32)[:, None] * inv, (t % GRID_W).astype(F32)[:, None] * inv], axis=-1)
    cos = jnp.concatenate([jnp.tile(jnp.cos(ang), (1, 4)), jnp.ones((R - T, 128), F32)], axis=0)
    sin = jnp.concatenate([jnp.tile(jnp.sin(ang), (1, 4)), jnp.zeros((R - T, 128), F32)], axis=0)
    return cos, sin


def kernel(x, c, ctx, c_ctx, w_mod, b_mod, norm_ffn1, w_ffn1_in, w_ffn1_out, norm_mix, w_in, w_pool, pool_scale, sink, w_out, norm_ffn2, w_ffn2_in, w_ffn2_out, norm_final, loss_target, m_c_ctx, m_w_mod, m_b_mod, m_norm_ffn1, m_w_ffn1_in, m_w_ffn1_out, m_norm_mix, m_w_in, m_w_pool, m_pool_scale, m_sink, m_w_out, m_norm_ffn2, m_w_ffn2_in, m_w_ffn2_out, m_norm_final, v_c_ctx, v_w_mod, v_b_mod, v_norm_ffn1, v_w_ffn1_in, v_w_ffn1_out, v_norm_mix, v_w_in, v_w_pool, v_pool_scale, v_sink, v_w_out, v_norm_ffn2, v_w_ffn2_in, v_w_ffn2_out, v_norm_final):
    T = x.shape[1]
    R = T + LC
    nl = w_mod.shape[0]
    cx, cy, cc = _coords()
    me = _lin((cx, cy, cc))
    pos = jnp.stack([cc, 2 * cx + cy]).astype(jnp.int32)
    mcols = w_mod.shape[2]

    shards = [([w_ffn1_in[l].T.astype(BF16), w_ffn1_out[l].astype(BF16)],
               [w_in[l].T.astype(BF16), w_out[l].astype(BF16)],
               [w_ffn2_in[l].T.astype(BF16), w_ffn2_out[l].astype(BF16)]) for l in range(nl)]

    got = _exchange("ag_c_w", _merge(_gather_direct([c]), _gather_a(shards[0][0] + shards[0][1])))
    c_all, w_first = got[0], got[1:]
    c16 = jnp.concatenate([c_all.reshape(NDEV, D), c_ctx[None], jnp.zeros((16 - NDEV - 1, D), F32)], axis=0)
    b_cols = lax.dynamic_slice(b_mod, (0, me * mcols), (nl, mcols)).reshape(nl, 1, mcols)
    got = _exchange("ag_mod_w", _merge(_gather_b(w_first), _gather_direct([_mod_fwd(c16, w_mod, b_cols, name="mod_fwd")])))
    w_first, mod_all = got[:4], got[4]
    mod_all = jnp.transpose(mod_all, (1, 2, 0, 3)).reshape(nl, 16, NMOD, D)
    mine = lax.dynamic_index_in_dim(mod_all, me, axis=1, keepdims=False)
    pad = jnp.zeros((nl, 16 - NMOD, D), F32)
    modv = jnp.stack([jnp.concatenate([mine, pad], axis=1), jnp.concatenate([mod_all[:, 8], pad], axis=1)], axis=1)

    gvec = [jnp.concatenate([norm_ffn1[l][None], norm_mix[l][None], norm_ffn2[l][None], jnp.zeros((5, D), F32)], axis=0)
            for l in range(nl)]
    cos, sin = _rope_tables(T, R)
    ps2 = [pool_scale[l][None] for l in range(nl)]

    h = jnp.concatenate([x[0], ctx[0]], axis=0)
    loss_all, dh, small, nf_all, big, last_partials = _forward_backward(
        h, loss_target[0], modv, gvec, shards, w_first, cos, sin, sink, w_pool, ps2, norm_final, pos, T=T)
    loss = jnp.sum(loss_all[:, 0, 0])
    grad_x = dh[:T][None]

    dm, g_b_mod, g_norms, g_nf, g_wp, g_ps, g_sk = _small_sums(
        [small[l][0:3] for l in range(nl)], nf_all, *[[small[l][k] for l in range(nl)] for k in (3, 4, 5)],
        name="small_sums")
    dm_cols = lax.dynamic_slice(dm, (0, 0, me * mcols), (nl, 16, mcols))
    g_w_mod, dc_part = _mod_bwd(c16, dm_cols, w_mod, name="mod_bwd")
    got = _exchange("rs1_tail", _merge(_scatter_1(last_partials), _gather_direct([dc_part])))
    c1, dc_all = _adds("ffn1_0", last_partials, got[:2], pos=pos), got[2]

    delta, new_m, new_v = {}, {}, {}
    (delta["w_mod"], new_m["w_mod"], new_v["w_mod"]), got = _adamw(
        w_mod, g_w_mod, m_w_mod, v_w_mod, name="adamw_w_mod", carry=_scatter_2(c1))
    big[0][0:2] = [(c1[0], got[0]), (c1[1], got[1])]

    grads = {
        "b_mod": g_b_mod, "norm_ffn1": g_norms[:, 0], "norm_mix": g_norms[:, 1], "norm_ffn2": g_norms[:, 2],
        "w_pool": g_wp.reshape(w_pool.shape), "pool_scale": g_ps, "sink": g_sk[:, :, 0], "norm_final": g_nf.reshape(D),
        "w_mod": g_w_mod,
    }
    weights = dict(c_ctx=c_ctx, w_mod=w_mod, b_mod=b_mod, norm_ffn1=norm_ffn1, w_ffn1_in=w_ffn1_in, w_ffn1_out=w_ffn1_out,
                   norm_mix=norm_mix, w_in=w_in, w_pool=w_pool, pool_scale=pool_scale, sink=sink, w_out=w_out,
                   norm_ffn2=norm_ffn2, w_ffn2_in=w_ffn2_in, w_ffn2_out=w_ffn2_out, norm_final=norm_final)
    moms = dict(c_ctx=(m_c_ctx, v_c_ctx), w_mod=(m_w_mod, v_w_mod), b_mod=(m_b_mod, v_b_mod),
                norm_ffn1=(m_norm_ffn1, v_norm_ffn1), w_ffn1_in=(m_w_ffn1_in, v_w_ffn1_in),
                w_ffn1_out=(m_w_ffn1_out, v_w_ffn1_out), norm_mix=(m_norm_mix, v_norm_mix), w_in=(m_w_in, v_w_in),
                w_pool=(m_w_pool, v_w_pool), pool_scale=(m_pool_scale, v_pool_scale), sink=(m_sink, v_sink),
                w_out=(m_w_out, v_w_out), norm_ffn2=(m_norm_ffn2, v_norm_ffn2), w_ffn2_in=(m_w_ffn2_in, v_w_ffn2_in),
                w_ffn2_out=(m_w_ffn2_out, v_w_ffn2_out), norm_final=(m_norm_final, v_norm_final))
    order = list(weights)
    small_names = ["c_ctx", "b_mod", "norm_ffn1", "norm_mix", "w_pool", "pool_scale", "sink", "norm_ffn2", "norm_final"]

    def as2d(name, t):
        if name == "w_pool":
            return t.reshape(-1, 128)
        return t.reshape(1, -1) if t.ndim == 1 else t

    triples = [(as2d(n, weights[n]), None if n == "c_ctx" else as2d(n, grads[n]), as2d(n, moms[n][0]), as2d(n, moms[n][1]))
               for n in small_names]
    outs = _small_adamw(as2d("c_ctx", c_ctx), dc_all, triples, name="small_adamw")
    grads["c_ctx"] = outs[0].reshape(c_ctx.shape)
    for k, n in enumerate(small_names):
        delta[n], new_m[n], new_v[n] = (o.reshape(weights[n].shape) for o in outs[1 + 3 * k : 4 + 3 * k])
    for k, n in enumerate(["w_ffn1_in", "w_ffn1_out", "w_in", "w_out", "w_ffn2_in", "w_ffn2_out"]):
        outs = None
        for l in reversed(range(nl)):
            outs = _sum_adamw(*big[l][k], pos, weights[n], *moms[n], l, outs, transpose=k % 2 == 0, name=f"adamw_{n}_{l}")
        grads[n], delta[n], new_m[n], new_v[n] = outs

    return (loss, grad_x, *[grads[n] for n in order], *[delta[n] for n in order],
            *[new_m[n] for n in order], *[new_v[n] for n in order])


def _merge(*rounds):
    ins, outs, plan, local, n_alias = [], [], [], [], 0
    for r in rounds:
        assert r.n_alias == 0 or (not ins and r.n_alias == len(r.ins) == len(r.out_shapes))
        oi, oo = len(ins), len(outs)
        plan += [(k, None if i is None else i + oi, sf, o + oo, df) for k, i, sf, o, df in r.plan]
        local += [(i + oi, sf, o + oo, df) for i, sf, o, df in r.local_plan]
        ins += r.ins
        outs += r.out_shapes
        n_alias += r.n_alias
    return _Round(ins, outs, plan, local, n_alias)


def _forward_backward(h, target, modv, gvec, shards, w_first, cos, sin, sink, w_pool, ps2, norm_final, pos, *, T):
    nl = len(gvec)
    flat = lambda ws: [w.reshape(-1, D) for w in ws]
    saved = []
    w1, wm = flat(w_first[:2]), flat(w_first[2:])
    for l in range(nl):
        last = l == nl - 1
        h0 = h
        if l == 0:
            (h1, a1, b1, f1), got = _ffn_fwd(h0, modv[l], gvec[l], *w1, T=T, mrow=0, grow=0, ctx_active=True,
                                             name=f"ffn1_fwd_{l}", carry=_gather_a(shards[l][2]))
            (u, q, k4, v4), got = _mixproj_fwd(h1, modv[l], gvec[l], wm[0], cos, sin, T=T, name=f"mixproj_fwd_{l}",
                                               carry=_gather_b(got))
            w2 = flat(got)
        else:
            (h1, a1, b1, f1), got = _ffn_fwd(h0, modv[l], gvec[l], *w1, T=T, mrow=0, grow=0, ctx_active=True,
                                             name=f"ffn1_fwd_{l}", carry=_gather_b(nxt_m + nxt_2))
            wm, w2 = flat(got[:2]), flat(got[2:])
            (u, q, k4, v4), _ = _mixproj_fwd(h1, modv[l], gvec[l], wm[0], cos, sin, T=T, name=f"mixproj_fwd_{l}")
        (cat,), nxt_1 = _attnpool_fwd(u, q, k4, v4, sink[l], w_pool[l], ps2[l], T=T, name=f"attnpool_fwd_{l}",
                                      carry=None if last else _gather_a(shards[l + 1][0]))
        (h2, mo), nxt_m = _mixout_fwd(h1, cat, modv[l], wm[1], T=T, ctx_active=not last, name=f"mixout_fwd_{l}",
                                      carry=None if last else _gather_a(shards[l + 1][1]))
        (h3, a2, b2, f2), got = _ffn_fwd(h2, modv[l], gvec[l], *w2, T=T, mrow=6, grow=2, ctx_active=not last,
                                         name=f"ffn2_fwd_{l}",
                                         carry=None if last else _merge(_gather_b(nxt_1), _gather_a(shards[l + 1][2])))
        saved.append((h0, a1, b1, f1, h1, u, q, k4, v4, cat, mo, h2, a2, b2, f2, w1, wm, w2))
        h = h3
        if not last:
            w1, nxt_2 = flat(got[:2]), got[2:]

    dh, loss_part, dnf = _loss_head(h, target, norm_final[None], T=T, name="loss_head")

    adds = functools.partial(_adds, pos=pos)
    small, big = [None] * nl, {}
    prev = None
    for l in reversed(range(nl)):
        last = l == nl - 1
        h0, a1, b1, f1, h1, u, q, k4, v4, cat, mo, h2, a2, b2, f2, w1, wm, w2 = saved[l]
        (dh, dab, s, n, df, pk2), got = _ffn_bwd(
            h2, dh, a2, b2, f2, modv[l], gvec[l], *w2, T=T, mrow=6, grow=2, ctx_active=not last, name=f"ffn2_bwd_{l}",
            carry=_merge(_scatter_1(prev[0]), _gather_a(prev[1])) if prev else None)
        if prev:
            c1, small_a = adds(f"ffn1_{l + 1}", prev[0], got[:2]), got[2:]
        g_w2i, got = _wgrad(dab, n, bk=WG_BK, sh=2 * DFF // NDEV, name=f"wgrad_ffn2_in_{l}",
                            carry=_scatter_2(c1[:1]) if prev else None)
        if prev:
            big[l + 1][0] = (c1[0], got[0])
        g_w2o, got = _wgrad(s, df, bk=WG_BK, sh=DFF // NDEV, name=f"wgrad_ffn2_out_{l}",
                            carry=_scatter_2(c1[1:]) if prev else None)
        if prev:
            big[l + 1][1] = (c1[1], got[0])
        rnd = _scatter_1([g_w2i, g_w2o])
        (dcat, dmix, pko), got = _mixout_bwd(dh, mo, modv[l], wm[1], T=T, ctx_active=not last, name=f"mixout_bwd_{l}",
                                             carry=_merge(_gather_b(small_a), rnd) if prev else rnd)
        if prev:
            small[l + 1], got = got[: len(small_a)], got[len(small_a) :]
        c2 = adds(f"ffn2_{l}", [g_w2i, g_w2o], got)
        g_wo, _ = _wgrad(cat, dmix, bk=D, sh=D // NDEV, name=f"wgrad_out_{l}")
        dps, dwp, dsc = _pool_bwd(u, dcat, w_pool[l], ps2[l], T=T, name=f"pool_bwd_{l}")
        (du, dq, dk, dv, dsk), got = _attn_bwd(q, k4, v4, dcat, dps, sink[l], T=T, name=f"attn_bwd_{l}", carry=_scatter_2(c2))
        big[l] = [None, None, None, None, (c2[0], got[0]), (c2[1], got[1])]
        dh, dproj, n, pkm = _mixproj_bwd(h1, dh, du, dq, dk, dv, modv[l], gvec[l], wm[0], cos, sin, T=T, name=f"mixproj_bwd_{l}")
        g_wi, _ = _wgrad(dproj, n, bk=PROJ, sh=PROJ // NDEV, name=f"wgrad_in_{l}")
        (dh, dab, s, n, df, pk1), got = _ffn_bwd(h0, dh, a1, b1, f1, modv[l], gvec[l], *w1, T=T, mrow=0, grow=0,
                                                 ctx_active=True, name=f"ffn1_bwd_{l}", carry=_scatter_1([g_wi, g_wo]))
        cm = adds(f"mix_{l}", [g_wi, g_wo], got)
        mine = [pk1, pkm + pko, pk2, dwp, dsc, dsk]
        rnd = _merge(_scatter_2(cm), _gather_a(mine + [dnf, loss_part])) if l == 0 else _scatter_2(cm)
        g_w1i, got = _wgrad(dab, n, bk=WG_BK, sh=2 * DFF // NDEV, name=f"wgrad_ffn1_in_{l}", carry=rnd)
        big[l][2:4] = [(cm[0], got[0]), (cm[1], got[1])]
        g_w1o, got = _wgrad(s, df, bk=WG_BK, sh=DFF // NDEV, name=f"wgrad_ffn1_out_{l}",
                            carry=_gather_b(got[2:]) if l == 0 else None)
        prev = ([g_w1i, g_w1o], mine)
    small[0], nf_all, loss_all = got[:6], got[6], got[7]
    return loss_all, dh, small, nf_all, big, prev[0]
```

```python
import functools

import jax
import jax.numpy as jnp
from jax import lax
from jax.experimental import pallas as pl
from jax.experimental.pallas import tpu as pltpu

F32, BF16 = jnp.float32, jnp.bfloat16

D = 1024
LC = 256
DFF = 2816
NMOD = 9
PW = 512
AW = 512
KVW = 128
PROJ = PW + AW + 2 * KVW
HD = 64
BLK = 128
GRID_W = 64
POOL_WINDOWS = (2, 4, 8, 16)
EPS = 1e-6
NEG = -1e30
ROPE_BASE = 10000.0
NDEV = 8
MESH = pl.DeviceIdType.MESH

ADAM_LR, ADAM_B1, ADAM_B2, ADAM_EPS, ADAM_WD, ADAM_STEP = 0.001, 0.9, 0.999, 1e-08, 0.01, 10

VMEM_LIMIT = 56 * 1024 * 1024
TM = 256
FFN_CHUNKS = ((0, 1536), (1536, 1280))
WG_BK = 1408

ANY = pl.BlockSpec(memory_space=pl.ANY)
SMEM = pl.BlockSpec(memory_space=pltpu.SMEM)


def _params(ngrid=1):
    return pltpu.CompilerParams(dimension_semantics=("arbitrary",) * ngrid, vmem_limit_bytes=VMEM_LIMIT)


def _dot(a, b):
    return jnp.dot(a, b, preferred_element_type=F32)


def _dot_nt(a, b):
    return lax.dot_general(a, b, (((1,), (1,)), ((), ())), preferred_element_type=F32)


def _dot_tn(a, b):
    return lax.dot_general(a, b, (((0,), (0,)), ((), ())), preferred_element_type=F32)


def _sigmoid(x):
    return 1.0 / (1.0 + jnp.exp(-x))


def _rows(tm, w):
    return pl.BlockSpec((tm, w), lambda i: (i, 0))


def _full(shape):
    nd = len(shape)
    return pl.BlockSpec(shape, lambda *_: (0,) * nd)


def _sds(shape, dtype):
    return jax.ShapeDtypeStruct(shape, dtype)


def _norm_mod(h, g, shift, scale):
    r = lax.rsqrt(jnp.mean(h * h, axis=-1, keepdims=True) + EPS)
    xhat = h * r
    y = xhat * g
    return r, xhat, y, y * (1.0 + scale) + shift


def _norm_mod_bwd(dn, r, xhat, y, g, scale):
    dshift = jnp.sum(dn, axis=0, keepdims=True)
    dscale = jnp.sum(dn * y, axis=0, keepdims=True)
    dy = dn * (1.0 + scale)
    dg = jnp.sum(dy * xhat, axis=0, keepdims=True)
    dxh = dy * g
    dh = r * (dxh - xhat * jnp.mean(dxh * xhat, axis=-1, keepdims=True))
    return dh, dshift, dscale, dg


def _acc_partials(part_ref, first, rows):
    @pl.when(first)
    def _():
        part_ref[...] = jnp.zeros_like(part_ref)

    for r, val in rows.items():
        part_ref[0, r : r + 1, :] += val


def _mod_spec(n_lat):
    return pl.BlockSpec((1, 16, D), lambda i: (i // n_lat, 0, 0))


def _part_spec(n_lat):
    return pl.BlockSpec((1, 8, D), lambda i: (i // n_lat, 0, 0))


def _load_weights(pairs, sem):
    copies = [pltpu.make_async_copy(src, dst, sem.at[k]) for k, (src, dst) in enumerate(pairs)]
    for cp in copies:
        cp.start()
    for cp in copies:
        cp.wait()


def _ffn_weight_copies(win_hbm, wout_hbm, win_v, wout_v, sem):
    loads = []
    for k, (c0, cw) in enumerate(FFN_CHUNKS):
        slabs = [(win_hbm, win_v, c0), (win_hbm, win_v, DFF + c0), (wout_hbm, wout_v, c0)]
        loads.append([pltpu.make_async_copy(src.at[pl.ds(r0, cw)], dst.at[pl.ds(r0, cw)], sem.at[3 * k + j])
                      for j, (src, dst, r0) in enumerate(slabs)])
    return loads


def _ffn_steps(i, n_active, loads, compute):
    @pl.when(i == 0)
    def _():
        for cp in sum(loads, []):
            cp.start()
        compute(loads)

    @pl.when(jnp.logical_and(i > 0, i < n_active))
    def _():
        compute(None)


def _wait_chunk(loads, k):
    if loads is not None:
        for cp in loads[k]:
            cp.wait()


def _ffn_fwd(h, modv, gvec, win, wout, *, T, mrow, grow, ctx_active, name, carry=None):
    R = h.shape[0]
    n_lat, n_tiles = T // TM, R // TM
    n_active = n_tiles if ctx_active else n_lat

    def body(h_ref, mod_ref, g_ref, win_hbm, wout_hbm, ho_ref, a_ref, b_ref, f_ref, win_v, wout_v, sem):
        i = pl.program_id(0)

        def compute(loads):
            h = h_ref[...]
            shift, scale, gate = (mod_ref[0, mrow + k : mrow + k + 1, :] for k in range(3))
            _, _, _, n = _norm_mod(h, g_ref[grow : grow + 1, :], shift, scale)
            n_bf = n.astype(BF16)
            acc = jnp.zeros((TM, D), F32)
            for k, (c0, cw) in enumerate(FFN_CHUNKS):
                _wait_chunk(loads, k)
                a = _dot_nt(n_bf, win_v[c0 : c0 + cw, :])
                b = _dot_nt(n_bf, win_v[DFF + c0 : DFF + c0 + cw, :])
                a_ref[:, c0 : c0 + cw] = a.astype(BF16)
                b_ref[:, c0 : c0 + cw] = b.astype(BF16)
                s = a * _sigmoid(a) * b
                acc = acc + _dot(s.astype(BF16), wout_v[c0 : c0 + cw, :])
            f_ref[...] = acc.astype(BF16)
            ho_ref[...] = h + (0.5 * gate) * acc

        _ffn_steps(i, n_active, _ffn_weight_copies(win_hbm, wout_hbm, win_v, wout_v, sem), compute)

        @pl.when(i >= n_active)
        def _():
            ho_ref[...] = h_ref[...]
            a_ref[...] = jnp.zeros_like(a_ref)
            b_ref[...] = jnp.zeros_like(b_ref)
            f_ref[...] = jnp.zeros_like(f_ref)

    return _call(
        body,
        name=name,
        grid=(n_tiles,),
        in_specs=[_rows(TM, D), _mod_spec(n_lat), _full((8, D)), ANY, ANY],
        out_specs=[_rows(TM, D), _rows(TM, DFF), _rows(TM, DFF), _rows(TM, D)],
        out_shape=[_sds((R, D), F32), _sds((R, DFF), BF16), _sds((R, DFF), BF16), _sds((R, D), BF16)],
        scratch_shapes=[pltpu.VMEM((2 * DFF, D), BF16), pltpu.VMEM((DFF, D), BF16),
                        pltpu.SemaphoreType.DMA((3 * len(FFN_CHUNKS),))],
        args=(h, modv, gvec, win, wout),
        carry=carry,
    )


def _ffn_bwd(h, dho, a, b, f, modv, gvec, win, wout, *, T, mrow, grow, ctx_active, name, carry=None):
    R = h.shape[0]
    n_lat, n_tiles = T // TM, R // TM
    n_active = n_tiles if ctx_active else n_lat

    def body(h_ref, dho_ref, a_ref, b_ref, f_ref, mod_ref, g_ref, win_hbm, wout_hbm,
             dh_ref, dab_ref, s_ref, n_ref, df_ref, part_ref, win_v, wout_v, sem):
        i = pl.program_id(0)
        first = jnp.logical_or(i == 0, i == n_lat)

        def compute(loads):
            h = h_ref[...]
            dho = dho_ref[...]
            shift, scale, gate = (mod_ref[0, mrow + k : mrow + k + 1, :] for k in range(3))
            g = g_ref[grow : grow + 1, :]
            r, xhat, y, n = _norm_mod(h, g, shift, scale)
            dgate = 0.5 * jnp.sum(dho * f_ref[...].astype(F32), axis=0, keepdims=True)
            df_bf = ((0.5 * gate) * dho).astype(BF16)
            df_ref[...] = df_bf
            n_ref[...] = n.astype(BF16)
            dn = jnp.zeros((TM, D), F32)
            for k, (c0, cw) in enumerate(FFN_CHUNKS):
                _wait_chunk(loads, k)
                ds = _dot_nt(df_bf, wout_v[c0 : c0 + cw, :])
                av = a_ref[:, c0 : c0 + cw].astype(F32)
                bv = b_ref[:, c0 : c0 + cw].astype(F32)
                sig = _sigmoid(av)
                sa = av * sig
                s_ref[:, c0 : c0 + cw] = (sa * bv).astype(BF16)
                da = (ds * bv * (sig * (1.0 + av * (1.0 - sig)))).astype(BF16)
                db = (ds * sa).astype(BF16)
                dab_ref[:, c0 : c0 + cw] = da
                dab_ref[:, DFF + c0 : DFF + c0 + cw] = db
                dn = dn + _dot(da, win_v[c0 : c0 + cw, :]) + _dot(db, win_v[DFF + c0 : DFF + c0 + cw, :])
            dh, dshift, dscale, dg = _norm_mod_bwd(dn, r, xhat, y, g, scale)
            dh_ref[...] = dho + dh
            _acc_partials(part_ref, first, {0: dshift, 1: dscale, 2: dgate, 3: dg})

        _ffn_steps(i, n_active, _ffn_weight_copies(win_hbm, wout_hbm, win_v, wout_v, sem), compute)

        @pl.when(i >= n_active)
        def _():
            dh_ref[...] = dho_ref[...]
            dab_ref[...] = jnp.zeros_like(dab_ref)
            s_ref[...] = jnp.zeros_like(s_ref)
            n_ref[...] = jnp.zeros_like(n_ref)
            df_ref[...] = jnp.zeros_like(df_ref)
            part_ref[...] = jnp.zeros_like(part_ref)

    return _call(
        body,
        name=name,
        grid=(n_tiles,),
        in_specs=[_rows(TM, D), _rows(TM, D), _rows(TM, DFF), _rows(TM, DFF), _rows(TM, D),
                  _mod_spec(n_lat), _full((8, D)), ANY, ANY],
        out_specs=[_rows(TM, D), _rows(TM, 2 * DFF), _rows(TM, DFF), _rows(TM, D), _rows(TM, D), _part_spec(n_lat)],
        out_shape=[_sds((R, D), F32), _sds((R, 2 * DFF), BF16), _sds((R, DFF), BF16), _sds((R, D), BF16),
                   _sds((R, D), BF16), _sds((2, 8, D), F32)],
        scratch_shapes=[pltpu.VMEM((2 * DFF, D), BF16), pltpu.VMEM((DFF, D), BF16),
                        pltpu.SemaphoreType.DMA((3 * len(FFN_CHUNKS),))],
        args=(h, dho, a, b, f, modv, gvec, win, wout),
        carry=carry,
    )


def _wgrad(x, y, *, bk, sh, name, carry=None):
    R, kx = x.shape
    n = y.shape[1]
    tr = R // 2
    nr, nsh = R // tr, bk // sh

    def body(x_ref, y_ref, o_ref, acc):
        r = pl.program_id(1)

        @pl.when(r == 0)
        def _():
            acc[...] = jnp.zeros_like(acc)

        acc[...] += _dot_tn(x_ref[...], y_ref[...])

        @pl.when(r == nr - 1)
        def _():
            for s in range(nsh):
                o_ref[s] = acc[s * sh : (s + 1) * sh, :].astype(BF16)

    (out,), got = _call(
        body,
        name=name,
        grid=(kx // bk, nr),
        in_specs=[pl.BlockSpec((tr, bk), lambda k, r: (r, k)), pl.BlockSpec((tr, n), lambda k, r: (r, 0))],
        out_specs=[pl.BlockSpec((nsh, sh, n), lambda k, r: (k, 0, 0))],
        out_shape=[_sds((kx // sh, sh, n), BF16)],
        scratch_shapes=[pltpu.VMEM((bk, n), F32)],
        args=(x, y),
        carry=carry,
    )
    return out, got


def _rot_half(x):
    lane = lax.broadcasted_iota(jnp.int32, x.shape, 1)
    return jnp.where((lane & (HD - 1)) < HD // 2, -pltpu.roll(x, 128 - HD // 2, 1), pltpu.roll(x, HD // 2, 1))


def _tile_sel():
    i = lax.broadcasted_iota(jnp.int32, (KVW, AW), 0)
    j = lax.broadcasted_iota(jnp.int32, (KVW, AW), 1)
    return jnp.where(i == (j // 256) * HD + (j & (HD - 1)), 1.0, 0.0).astype(BF16)


def _mixproj_fwd(h, modv, gvec, win, cos, sin, *, T, name, carry=None):
    R = h.shape[0]
    n_lat, n_tiles = T // TM, R // TM

    def body(h_ref, mod_ref, g_ref, win_ref, cos_ref, sin_ref, u_ref, q_ref, k4_ref, v4_ref):
        shift, scale = mod_ref[0, 3:4, :], mod_ref[0, 4:5, :]
        _, _, _, n = _norm_mod(h_ref[...], g_ref[1:2, :], shift, scale)
        proj = _dot_nt(n.astype(BF16), win_ref[...])
        u_ref[...] = proj[:, :PW]
        cs, sn = cos_ref[...], sin_ref[...]
        for s in range(AW // 128):
            x = proj[:, PW + 128 * s : PW + 128 * (s + 1)]
            q_ref[:, 128 * s : 128 * (s + 1)] = ((x * cs + _rot_half(x) * sn) * (HD ** -0.5)).astype(BF16)
        k = proj[:, PW + AW : PW + AW + KVW]
        k = (k * cs + _rot_half(k) * sn).astype(BF16)
        v = proj[:, PW + AW + KVW :].astype(BF16)
        sel = _tile_sel()
        k4_ref[...] = _dot(k, sel).astype(BF16)
        v4_ref[...] = _dot(v, sel).astype(BF16)

    return _call(
        body,
        name=name,
        grid=(n_tiles,),
        in_specs=[_rows(TM, D), _mod_spec(n_lat), _full((8, D)), _full((PROJ, D)), _rows(TM, 128), _rows(TM, 128)],
        out_specs=[_rows(TM, PW), _rows(TM, AW), _rows(TM, AW), _rows(TM, AW)],
        out_shape=[_sds((R, PW), F32), _sds((R, AW), BF16), _sds((R, AW), BF16), _sds((R, AW), BF16)],
        scratch_shapes=[],
        args=(h, modv, gvec, win, cos, sin),
        carry=carry,
    )


def _win_start(j, hi):
    return pl.multiple_of(jnp.clip((j - 1) * BLK, 0, hi - 3 * BLK), BLK)


def _hi_lo(x):
    hi = x.astype(BF16)
    return hi, (x - hi.astype(F32)).astype(BF16)


def _pool_bounds(t, w, T, R):
    is_ctx = t >= T
    lo = jnp.maximum(t - w // 2, jnp.where(is_ctx, T, 0))
    hi = jnp.minimum(t + w // 2, jnp.where(is_ctx, R, T))
    return lo, hi


def _pooled(u_v, j, T, R):
    start = _win_start(j, R)
    u3_hi, u3_lo = _hi_lo(u_v[pl.ds(start, 3 * BLK), :])
    ub = u_v[pl.ds(pl.multiple_of(j * BLK, BLK), BLK), :]
    t = j * BLK + lax.broadcasted_iota(jnp.int32, (BLK, 1), 0)
    pos = start + lax.broadcasted_iota(jnp.int32, (1, 3 * BLK), 1)
    pooled, counts = [], []
    for g, w in enumerate(POOL_WINDOWS):
        lo, hi = _pool_bounds(t, w, T, R)
        band = jnp.where(pos >= lo, jnp.where(pos < hi, 1.0, 0.0), 0.0).astype(BF16)
        sl = slice(g * 128, (g + 1) * 128)
        sums = _dot(band, u3_hi[:, sl]) + _dot(band, u3_lo[:, sl])
        cnt = (hi - lo).astype(F32)
        pooled.append(sums / cnt - ub[:, sl])
        counts.append(cnt)
    return pooled, counts


def _stack_heads(x):
    lane_h = lax.broadcasted_iota(jnp.int32, x.shape, 1) // HD
    return jnp.concatenate([jnp.where(lane_h == h, x, jnp.zeros_like(x)) for h in range(4)], axis=0)


def _unstack_heads(x):
    lane_h = lax.broadcasted_iota(jnp.int32, (BLK, 256), 1) // HD
    out = jnp.zeros((BLK, 256), F32)
    for h in range(4):
        out = out + jnp.where(lane_h == h, x[h * BLK : (h + 1) * BLK, :], 0.0)
    return out


def _window_mask(j, start_l, nbl):
    rowi = lax.broadcasted_iota(jnp.int32, (4 * BLK, 1), 0)
    qpos = j * BLK + (rowi & (BLK - 1))
    kpos = start_l + lax.broadcasted_iota(jnp.int32, (1, 3 * BLK), 1)
    reach = jnp.where(j < nbl, BLK, -1)
    return jnp.abs(kpos - qpos) <= reach


def _attn_exps(qs, kl, kc, sink_ref, g, valid):
    s_l = jnp.where(valid, _dot_nt(qs, kl), NEG)
    s_c = _dot_nt(qs, kc)
    rb = lax.broadcasted_iota(jnp.int32, (4 * BLK, 1), 0) // BLK
    sk = jnp.where(rb == 0, sink_ref[4 * g], jnp.where(rb == 1, sink_ref[4 * g + 1],
                   jnp.where(rb == 2, sink_ref[4 * g + 2], sink_ref[4 * g + 3])))
    m = jnp.maximum(jnp.maximum(jnp.max(s_l, axis=1, keepdims=True), jnp.max(s_c, axis=1, keepdims=True)), sk)
    e_l, e_c, e_s = jnp.exp(s_l - m), jnp.exp(s_c - m), jnp.exp(sk - m)
    inv = 1.0 / (jnp.sum(e_l, axis=1, keepdims=True) + jnp.sum(e_c, axis=1, keepdims=True) + e_s)
    return e_l, e_c, e_s, inv


def _attnpool_fwd(u, q, k4, v4, sink, w_pool, pool_scale, *, T, name, carry=None):
    R = u.shape[0]
    nb, nbl = R // BLK, T // BLK

    def body(q_ref, sink_ref, wp_ref, ps_ref, u_hbm, k4_hbm, v4_hbm, cat_ref, u_v, k4_v, v4_v, sem):
        j = pl.program_id(0)

        @pl.when(j == 0)
        def _():
            _load_weights([(u_hbm, u_v), (k4_hbm, k4_v), (v4_hbm, v4_v)], sem)

        pooled, _ = _pooled(u_v, j, T, R)
        for g in range(4):
            mixed = _dot(pooled[g].astype(BF16), wp_ref[g].astype(BF16)) * ps_ref[:, g * 128 : (g + 1) * 128]
            cat_ref[:, g * 128 : (g + 1) * 128] = mixed.astype(BF16)

        start_l = _win_start(j, T)
        valid = _window_mask(j, start_l, nbl)
        for g in range(2):
            gl = slice(g * 256, (g + 1) * 256)
            qs = _stack_heads(q_ref[:, gl])
            e_l, e_c, _, inv = _attn_exps(qs, k4_v[pl.ds(start_l, 3 * BLK), gl], k4_v[T:R, gl], sink_ref, g, valid)
            o = _dot(e_l.astype(BF16), v4_v[pl.ds(start_l, 3 * BLK), gl]) + _dot(e_c.astype(BF16), v4_v[T:R, gl])
            cat_ref[:, PW + g * 256 : PW + (g + 1) * 256] = _unstack_heads(o * inv).astype(BF16)

    return _call(
        body,
        name=name,
        grid=(nb,),
        in_specs=[_rows(BLK, AW), SMEM, _full((4, 128, 128)), _full((1, PW)), ANY, ANY, ANY],
        out_specs=[_rows(BLK, D)],
        out_shape=[_sds((R, D), BF16)],
        scratch_shapes=[pltpu.VMEM((R, PW), F32), pltpu.VMEM((R, AW), BF16), pltpu.VMEM((R, AW), BF16),
                        pltpu.SemaphoreType.DMA((3,))],
        args=(q, sink, w_pool, pool_scale, u, k4, v4),
        carry=carry,
    )


def _mixout_fwd(h, cat, modv, wout, *, T, ctx_active, name, carry=None):
    R = h.shape[0]
    n_lat, n_tiles = T // TM, R // TM

    def body(h_ref, cat_ref, mod_ref, w_ref, ho_ref, mo_ref):
        i = pl.program_id(0)

        def compute():
            mo = _dot(cat_ref[...], w_ref[...])
            mo_ref[...] = mo.astype(BF16)
            ho_ref[...] = h_ref[...] + mod_ref[0, 5:6, :] * mo

        if ctx_active:
            compute()
        else:
            pl.when(i < n_lat)(compute)

            @pl.when(i >= n_lat)
            def _():
                ho_ref[...] = h_ref[...]
                mo_ref[...] = jnp.zeros_like(mo_ref)

    return _call(
        body,
        name=name,
        grid=(n_tiles,),
        in_specs=[_rows(TM, D), _rows(TM, D), _mod_spec(n_lat), _full((D, D))],
        out_specs=[_rows(TM, D), _rows(TM, D)],
        out_shape=[_sds((R, D), F32), _sds((R, D), BF16)],
        scratch_shapes=[],
        args=(h, cat, modv, wout),
        carry=carry,
    )


def _mixout_bwd(dho, mo, modv, wout, *, T, ctx_active, name, carry=None):
    R = dho.shape[0]
    n_lat, n_tiles = T // TM, R // TM

    def body(dho_ref, mo_ref, mod_ref, w_ref, dcat_ref, dmix_ref, part_ref):
        i = pl.program_id(0)
        first = jnp.logical_or(i == 0, i == n_lat)

        def compute():
            dho = dho_ref[...]
            dmix = (mod_ref[0, 5:6, :] * dho).astype(BF16)
            dmix_ref[...] = dmix
            dcat_ref[...] = _dot_nt(dmix, w_ref[...])
            dgate = jnp.sum(dho * mo_ref[...].astype(F32), axis=0, keepdims=True)
            _acc_partials(part_ref, first, {2: dgate})

        if ctx_active:
            compute()
        else:
            pl.when(i < n_lat)(compute)

            @pl.when(i >= n_lat)
            def _():
                dcat_ref[...] = jnp.zeros_like(dcat_ref)
                dmix_ref[...] = jnp.zeros_like(dmix_ref)
                part_ref[...] = jnp.zeros_like(part_ref)

    return _call(
        body,
        name=name,
        grid=(n_tiles,),
        in_specs=[_rows(TM, D), _rows(TM, D), _mod_spec(n_lat), _full((D, D))],
        out_specs=[_rows(TM, D), _rows(TM, D), _part_spec(n_lat)],
        out_shape=[_sds((R, D), F32), _sds((R, D), BF16), _sds((2, 8, D), F32)],
        scratch_shapes=[],
        args=(dho, mo, modv, wout),
        carry=carry,
    )


def _pool_bwd(u, dcat, w_pool, pool_scale, *, T, name):
    R = u.shape[0]
    nb = R // BLK

    def body(dcat_ref, wp_ref, ps_ref, u_hbm, dps_ref, dwp_ref, dsc_ref, u_v, sem):
        j = pl.program_id(0)

        @pl.when(j == 0)
        def _():
            _load_weights([(u_hbm, u_v)], sem)
            dwp_ref[...] = jnp.zeros_like(dwp_ref)
            dsc_ref[...] = jnp.zeros_like(dsc_ref)

        pooled, counts = _pooled(u_v, j, T, R)
        for g in range(4):
            sl = slice(g * 128, (g + 1) * 128)
            p_bf = pooled[g].astype(BF16)
            w_bf = wp_ref[g].astype(BF16)
            dmixed = dcat_ref[:, sl]
            dsc_ref[0:1, sl] += jnp.sum(dmixed * _dot(p_bf, w_bf), axis=0, keepdims=True)
            dmp = (dmixed * ps_ref[:, sl]).astype(BF16)
            dwp_ref[sl, :] += _dot_tn(p_bf, dmp)
            dps_ref[:, sl] = _dot_nt(dmp, w_bf) / counts[g]

    return pl.pallas_call(
        body,
        name=name,
        grid=(nb,),
        in_specs=[_rows(BLK, D), _full((4, 128, 128)), _full((1, PW)), ANY],
        out_specs=[_rows(BLK, PW), _full((PW, 128)), _full((8, PW))],
        out_shape=[_sds((R, PW), F32), _sds((PW, 128), F32), _sds((8, PW), F32)],
        scratch_shapes=[pltpu.VMEM((R, PW), F32), pltpu.SemaphoreType.DMA((1,))],
        compiler_params=_params(),
    )(dcat, w_pool, pool_scale, u)


def _fold_heads(x):
    y = x[:, :128] + x[:, 128:]
    return y + pltpu.roll(y, HD, 1)


def _attn_bwd(q, k4, v4, dcat, dps, sink, *, T, name, carry=None):
    R = q.shape[0]
    nb, nbl = R // BLK, T // BLK

    def body(q_ref, dcat_ref, sink_ref, k4_hbm, v4_hbm, dps_hbm, du_ref, dq_ref, dk_ref, dv_ref, dsk_ref,
             k4_v, v4_v, dps_v, sem):
        j = pl.program_id(0)

        @pl.when(j == 0)
        def _():
            _load_weights([(k4_hbm, k4_v), (v4_hbm, v4_v), (dps_hbm, dps_v)], sem)
            dk_ref[...] = jnp.zeros_like(dk_ref)
            dv_ref[...] = jnp.zeros_like(dv_ref)
            dsk_ref[...] = jnp.zeros_like(dsk_ref)

        start = _win_start(j, R)
        d3_hi, d3_lo = _hi_lo(dps_v[pl.ds(start, 3 * BLK), :])
        db = dps_v[pl.ds(pl.multiple_of(j * BLK, BLK), BLK), :]
        pos = j * BLK + lax.broadcasted_iota(jnp.int32, (BLK, 1), 0)
        t_r = start + lax.broadcasted_iota(jnp.int32, (1, 3 * BLK), 1)
        for g, w in enumerate(POOL_WINDOWS):
            sl = slice(g * 128, (g + 1) * 128)
            lo_r, hi_r = _pool_bounds(t_r, w, T, R)
            band_t = jnp.where(pos >= lo_r, jnp.where(pos < hi_r, 1.0, 0.0), 0.0).astype(BF16)
            lo_c, hi_c = _pool_bounds(pos, w, T, R)
            du_ref[:, sl] = _dot(band_t, d3_hi[:, sl]) + _dot(band_t, d3_lo[:, sl]) - db[:, sl] * (hi_c - lo_c).astype(F32)

        start_l = _win_start(j, T)
        valid = _window_mask(j, start_l, nbl)
        rb = lax.broadcasted_iota(jnp.int32, (4 * BLK, 1), 0) // BLK
        lane = lax.broadcasted_iota(jnp.int32, (1, 128), 1)
        dk_l, dk_c, dv_l, dv_c = [], [], [], []
        for g in range(2):
            gl = slice(g * 256, (g + 1) * 256)
            qs = _stack_heads(q_ref[:, gl])
            kl, kc = k4_v[pl.ds(start_l, 3 * BLK), gl], k4_v[T:R, gl]
            vl, vc = v4_v[pl.ds(start_l, 3 * BLK), gl], v4_v[T:R, gl]
            e_l, e_c, e_s, inv = _attn_exps(qs, kl, kc, sink_ref, g, valid)
            p_l, p_c, p_s = e_l * inv, e_c * inv, e_s * inv
            dos = _stack_heads(dcat_ref[:, PW + g * 256 : PW + (g + 1) * 256]).astype(BF16)
            dp_l, dp_c = _dot_nt(dos, vl), _dot_nt(dos, vc)
            delta = jnp.sum(p_l * dp_l, axis=1, keepdims=True) + jnp.sum(p_c * dp_c, axis=1, keepdims=True)
            ds_l = (p_l * (dp_l - delta)).astype(BF16)
            ds_c = (p_c * (dp_c - delta)).astype(BF16)
            dq_ref[:, gl] = _unstack_heads(_dot(ds_l, kl) + _dot(ds_c, kc)) * (HD ** -0.5)
            dk_l.append(_fold_heads(_dot_tn(ds_l, qs)))
            dk_c.append(_fold_heads(_dot_tn(ds_c, qs)))
            dv_l.append(_fold_heads(_dot_tn(p_l.astype(BF16), dos)))
            dv_c.append(_fold_heads(_dot_tn(p_c.astype(BF16), dos)))
            dsink = -p_s * delta
            for h in range(4):
                tot = jnp.sum(jnp.where(rb == h, dsink, 0.0), axis=0, keepdims=True)
                dsk_ref[4 * g + h : 4 * g + h + 1, :] += jnp.broadcast_to(tot, (1, 128))
        first = lane < HD
        dk_ref[pl.ds(start_l, 3 * BLK), :] += jnp.where(first, dk_l[0], dk_l[1])
        dk_ref[T:R, :] += jnp.where(first, dk_c[0], dk_c[1])
        dv_ref[pl.ds(start_l, 3 * BLK), :] += jnp.where(first, dv_l[0], dv_l[1])
        dv_ref[T:R, :] += jnp.where(first, dv_c[0], dv_c[1])

    return _call(
        body,
        name=name,
        grid=(nb,),
        in_specs=[_rows(BLK, AW), _rows(BLK, D), SMEM, ANY, ANY, ANY],
        out_specs=[_rows(BLK, PW), _rows(BLK, AW), _full((R, KVW)), _full((R, KVW)), _full((8, 128))],
        out_shape=[_sds((R, PW), F32), _sds((R, AW), F32), _sds((R, KVW), F32), _sds((R, KVW), F32),
                   _sds((8, 128), F32)],
        scratch_shapes=[pltpu.VMEM((R, AW), BF16), pltpu.VMEM((R, AW), BF16), pltpu.VMEM((R, PW), F32),
                        pltpu.SemaphoreType.DMA((3,))],
        args=(q, dcat, sink, k4, v4, dps),
        carry=carry,
    )


def _mixproj_bwd(h, dho, du, dq, dk, dv, modv, gvec, win, cos, sin, *, T, name):
    R = h.shape[0]
    n_lat, n_tiles = T // TM, R // TM

    def body(h_ref, dho_ref, du_ref, dq_ref, dk_ref, dv_ref, mod_ref, g_ref, win_ref, cos_ref, sin_ref,
             dh_ref, dproj_ref, n_ref, part_ref):
        i = pl.program_id(0)
        first = jnp.logical_or(i == 0, i == n_lat)
        shift, scale = mod_ref[0, 3:4, :], mod_ref[0, 4:5, :]
        g = g_ref[1:2, :]
        r, xhat, y, n = _norm_mod(h_ref[...], g, shift, scale)
        n_ref[...] = n.astype(BF16)
        cs, sn = cos_ref[...], sin_ref[...]
        dproj_ref[:, :PW] = du_ref[...].astype(BF16)
        for s in range(AW // 128):
            x = dq_ref[:, 128 * s : 128 * (s + 1)]
            dproj_ref[:, PW + 128 * s : PW + 128 * (s + 1)] = (x * cs - _rot_half(x) * sn).astype(BF16)
        x = dk_ref[...]
        dproj_ref[:, PW + AW : PW + AW + KVW] = (x * cs - _rot_half(x) * sn).astype(BF16)
        dproj_ref[:, PW + AW + KVW :] = dv_ref[...].astype(BF16)
        dn = _dot(dproj_ref[...], win_ref[...])
        dh, dshift, dscale, dg = _norm_mod_bwd(dn, r, xhat, y, g, scale)
        dh_ref[...] = dho_ref[...] + dh
        _acc_partials(part_ref, first, {0: dshift, 1: dscale, 3: dg})

    return pl.pallas_call(
        body,
        name=name,
        grid=(n_tiles,),
        in_specs=[_rows(TM, D), _rows(TM, D), _rows(TM, PW), _rows(TM, AW), _rows(TM, KVW), _rows(TM, KVW),
                  _mod_spec(n_lat), _full((8, D)), _full((PROJ, D)), _rows(TM, 128), _rows(TM, 128)],
        out_specs=[_rows(TM, D), _rows(TM, PROJ), _rows(TM, D), _part_spec(n_lat)],
        out_shape=[_sds((R, D), F32), _sds((R, PROJ), BF16), _sds((R, D), BF16), _sds((2, 8, D), F32)],
        compiler_params=_params(),
    )(h, dho, du, dq, dk, dv, modv, gvec, win, cos, sin)


def _loss_head(h, target, g_final, *, T, name):
    R = h.shape[0]
    n_lat, n_tiles = T // TM, R // TM

    def body(h_ref, t_ref, g_ref, dh_ref, loss_ref, dg_ref):
        i = pl.program_id(0)

        @pl.when(i == 0)
        def _():
            loss_ref[...] = jnp.zeros_like(loss_ref)
            dg_ref[...] = jnp.zeros_like(dg_ref)

        @pl.when(i < n_lat)
        def _():
            h = h_ref[...]
            g = g_ref[...]
            r = lax.rsqrt(jnp.mean(h * h, axis=-1, keepdims=True) + EPS)
            xhat = h * r
            err = xhat * g - t_ref[...]
            tot = jnp.sum(jnp.sum(err * err, axis=1, keepdims=True), axis=0, keepdims=True)
            loss_ref[...] += jnp.broadcast_to(tot * (0.5 / D), loss_ref.shape)
            dy = err * (1.0 / D)
            dg_ref[0:1, :] += jnp.sum(dy * xhat, axis=0, keepdims=True)
            dxh = dy * g
            dh_ref[...] = r * (dxh - xhat * jnp.mean(dxh * xhat, axis=-1, keepdims=True))

        @pl.when(i >= n_lat)
        def _():
            dh_ref[...] = jnp.zeros_like(dh_ref)

    return pl.pallas_call(
        body,
        name=name,
        grid=(n_tiles,),
        in_specs=[_rows(TM, D), pl.BlockSpec((TM, D), lambda i: (jnp.minimum(i, n_lat - 1), 0)), _full((1, D))],
        out_specs=[_rows(TM, D), _full((8, 128)), _full((8, D))],
        out_shape=[_sds((R, D), F32), _sds((8, 128), F32), _sds((8, D), F32)],
        compiler_params=_params(),
    )(h, target, g_final)


def _mod_fwd(c16, w_mod, b_cols, *, name):
    nl, _, cols = w_mod.shape

    def body(c_ref, w_ref, b_ref, o_ref):
        c = c_ref[...]
        sc = (c * _sigmoid(c)).astype(BF16)
        o_ref[0] = _dot(sc, w_ref[0].astype(BF16)) + b_ref[0]

    return pl.pallas_call(
        body,
        name=name,
        grid=(nl,),
        in_specs=[_full((16, D)), pl.BlockSpec((1, D, cols), lambda l: (l, 0, 0)),
                  pl.BlockSpec((1, 1, cols), lambda l: (l, 0, 0))],
        out_specs=pl.BlockSpec((1, 16, cols), lambda l: (l, 0, 0)),
        out_shape=_sds((nl, 16, cols), F32),
        compiler_params=_params(),
    )(c16, w_mod, b_cols)


def _mod_bwd(c16, dm_cols, w_mod, *, name):
    nl, _, cols = w_mod.shape

    def body(c_ref, dm_ref, w_ref, gw_ref, dc_ref):
        c = c_ref[...]
        sc = (c * _sigmoid(c)).astype(BF16)
        dm = dm_ref[0].astype(BF16)
        gw_ref[0] = _dot_tn(sc, dm)
        dc_ref[0] = _dot_nt(dm, w_ref[0].astype(BF16))

    return pl.pallas_call(
        body,
        name=name,
        grid=(nl,),
        in_specs=[_full((16, D)), pl.BlockSpec((1, 16, cols), lambda l: (l, 0, 0)),
                  pl.BlockSpec((1, D, cols), lambda l: (l, 0, 0))],
        out_specs=[pl.BlockSpec((1, D, cols), lambda l: (l, 0, 0)), pl.BlockSpec((1, 16, D), lambda l: (l, 0, 0))],
        out_shape=[_sds((nl, D, cols), F32), _sds((nl, 16, D), F32)],
        compiler_params=_params(),
    )(c16, dm_cols, w_mod)


def _coords():
    return lax.axis_index("x"), lax.axis_index("y"), lax.axis_index("c")


FWD = 8


def _peer(k, x, y, c):
    if k == FWD:
        return (x ^ (1 - c), y ^ c, c)
    return (1 - x if k & 4 else x, 1 - y if k & 2 else y, 1 - c if k & 1 else c)


def _lin(p):
    return 4 * p[0] + 2 * p[1] + p[2]


def _view(ref, slot):
    return ref if slot is None else ref.at[slot]


class _Round:
    def __init__(self, ins, out_shapes, plan, local_plan=(), n_alias=0):
        self.ins, self.out_shapes = list(ins), list(out_shapes)
        self.plan, self.local_plan, self.n_alias = list(plan), list(local_plan), n_alias
        fed = {p[3] for p in self.plan if p[0] == FWD}
        self.feeders = [n for n, p in enumerate(self.plan) if p[0] in (2, 4, 6) and p[3] in fed]

    def sems(self):
        return [pltpu.SemaphoreType.DMA((len(self.plan),)), pltpu.SemaphoreType.DMA((len(self.plan),)),
                pltpu.SemaphoreType.DMA((max(len(self.local_plan), 1),))]

    def _remote(self, in_refs, out_refs, sems, incoming, pick):
        in_refs = list(out_refs[: self.n_alias]) + list(in_refs[self.n_alias :])
        x, y, c = _coords()
        me = _lin((x, y, c))
        copies = {}
        for idx, (k, ii, sfn, oi, dfn) in enumerate(self.plan):
            if not pick(idx, "d2d" if k == 1 else "fwd" if k == FWD else "ici"):
                continue
            peer = _peer(k, x, y, c)
            sender, receiver = (_lin(peer), me) if incoming else (me, _lin(peer))
            src = out_refs[oi] if ii is None else in_refs[ii]
            copies[idx] = pltpu.make_async_remote_copy(
                src_ref=_view(src, sfn(sender, receiver)), dst_ref=_view(out_refs[oi], dfn(sender, receiver)),
                send_sem=sems[0].at[idx], recv_sem=sems[1].at[idx], device_id=peer, device_id_type=MESH)
        return copies

    def _local(self, in_refs, out_refs, sems):
        in_refs = list(out_refs[: self.n_alias]) + list(in_refs[self.n_alias :])
        me = _lin(_coords())
        return [pltpu.make_async_copy(_view(in_refs[ii], sfn(me)), _view(out_refs[oi], dfn(me)), sems[2].at[idx])
                for idx, (ii, sfn, oi, dfn) in enumerate(self.local_plan)]

    def start(self, in_refs, out_refs, sems, links=("ici", "d2d")):
        for cp in self._remote(in_refs, out_refs, sems, False, lambda n, link: link in links).values():
            cp.start()
        if "ici" in links:
            for cp in self._local(in_refs, out_refs, sems):
                cp.start()

    def mid(self, in_refs, out_refs, sems):
        if self.feeders:
            for cp in self._remote(in_refs, out_refs, sems, True, lambda n, link: n in self.feeders).values():
                cp.wait_recv()
            for cp in self._remote(in_refs, out_refs, sems, False, lambda n, link: link == "fwd").values():
                cp.start()

    def finish(self, in_refs, out_refs, sems):
        for cp in self._remote(in_refs, out_refs, sems, True, lambda n, link: n not in self.feeders).values():
            cp.wait_recv()
        for cp in self._remote(in_refs, out_refs, sems, False, lambda n, link: True).values():
            cp.wait_send()
        for cp in self._local(in_refs, out_refs, sems):
            cp.wait()


def _exchange(name, rnd):
    n_in, n_out = len(rnd.ins), len(rnd.out_shapes)

    def body(*refs):
        in_refs, out_refs, sems = refs[:n_in], refs[n_in : n_in + n_out], refs[n_in + n_out :]
        rnd.start(in_refs, out_refs, sems)
        rnd.mid(in_refs, out_refs, sems)
        rnd.finish(in_refs, out_refs, sems)

    return pl.pallas_call(
        body, name=name, in_specs=[ANY] * n_in, out_specs=[ANY] * n_out, out_shape=rnd.out_shapes,
        scratch_shapes=rnd.sems(), input_output_aliases={i: i for i in range(rnd.n_alias)})(*rnd.ins)


def _call(body, *, name, grid, in_specs, out_specs, out_shape, scratch_shapes, args, carry=None):
    params = _params(len(grid))
    if carry is None:
        outs = pl.pallas_call(body, name=name, grid=grid, in_specs=in_specs, out_specs=out_specs, out_shape=out_shape,
                              scratch_shapes=scratch_shapes, compiler_params=params)(*args)
        return list(outs), []
    n_ci, n_co, n_cs = len(in_specs), len(out_shape), len(scratch_shapes)
    n_xi, n_xo = len(carry.ins), len(carry.out_shapes)

    def wrapped(*refs):
        ci, xi = refs[:n_ci], refs[n_ci : n_ci + n_xi]
        o0 = n_ci + n_xi
        co, xo = refs[o0 : o0 + n_co], refs[o0 + n_co : o0 + n_co + n_xo]
        s0 = o0 + n_co + n_xo
        cs, sems = refs[s0 : s0 + n_cs], refs[s0 + n_cs :]
        ids = [pl.program_id(a) for a in range(len(grid))]
        first = functools.reduce(jnp.logical_and, [i == 0 for i in ids])
        last = functools.reduce(jnp.logical_and, [i == g - 1 for i, g in zip(ids, grid)])

        @pl.when(first)
        def _():
            carry.start(xi, xo, sems, links=("ici",))

        if carry.feeders:
            step = functools.reduce(lambda acc, ig: acc * ig[1] + ig[0], zip(ids, grid), 0)
            n_steps = functools.reduce(lambda a, b: a * b, grid)

            @pl.when(step == min(n_steps - 1, (3 * n_steps) // 5))
            def _():
                carry.mid(xi, xo, sems)

        body(*ci, *co, *cs)

        @pl.when(first)
        def _():
            carry.start(xi, xo, sems, links=("d2d",))

        @pl.when(last)
        def _():
            carry.finish(xi, xo, sems)

    outs = pl.pallas_call(
        wrapped, name=name, grid=grid, in_specs=list(in_specs) + [ANY] * n_xi, out_specs=list(out_specs) + [ANY] * n_xo,
        out_shape=list(out_shape) + carry.out_shapes, scratch_shapes=list(scratch_shapes) + carry.sems(),
        input_output_aliases={n_ci + i: n_co + i for i in range(carry.n_alias)}, compiler_params=params,
    )(*args, *carry.ins)
    return list(outs[:n_co]), list(outs[n_co:])


def _gather_direct(arrays):
    na = len(arrays)
    outs = [_sds((NDEV,) + a.shape, a.dtype) for a in arrays]
    plan = [(k, i, lambda s, r: None, i, lambda s, r: s) for i in range(na) for k in range(1, NDEV)]
    return _Round(arrays, outs, plan, [(i, lambda m: None, i, lambda m: m) for i in range(na)])


def _gather_a(arrays):
    na = len(arrays)
    outs = [_sds((NDEV,) + a.shape, a.dtype) for a in arrays]
    plan = [(k, i, lambda s, r: None, i, lambda s, r: s) for i in range(na) for k in (2, 4)]
    handed = lambda s, r: s ^ (2 << (s & 1))
    plan += [(FWD, None, handed, i, handed) for i in range(na)]
    return _Round(arrays, outs, plan, [(i, lambda m: None, i, lambda m: m) for i in range(na)])


def _gather_b(got):
    na = len(got)
    plan = [(1, i, (lambda s, r, k=k: s ^ k), i, (lambda s, r, k=k: s ^ k)) for i in range(na) for k in (0, 2, 4, 6)]
    return _Round(got, [_sds(g.shape, g.dtype) for g in got], plan, n_alias=na)


def _scatter_1(grads):
    plan = [(1, i, (lambda s, r, q=q: 2 * q + (r & 1)), i, (lambda s, r, q=q: q))
            for i in range(len(grads)) for q in range(4)]
    return _Round(grads, [_sds((4,) + g.shape[1:], g.dtype) for g in grads], plan)


def _scatter_2(chip):
    plan = [(k, i, lambda s, r: r >> 1, i, (lambda s, r, j=j: j)) for i in range(len(chip)) for j, k in enumerate((2, 4, 6))]
    return _Round(chip, [_sds((3,) + g.shape[1:], g.dtype) for g in chip], plan)


def _add_pairs(g, got, pos, *, name):
    _, sh, w = g.shape

    def body(pos_ref, g_ref, r_ref, o_ref):
        o_ref[...] = (g_ref[...].astype(F32) + r_ref[...].astype(F32)).astype(o_ref.dtype)

    return pl.pallas_call(
        body,
        name=name,
        grid_spec=pltpu.PrefetchScalarGridSpec(
            num_scalar_prefetch=1, grid=(4,),
            in_specs=[pl.BlockSpec((1, sh, w), lambda q, p: (2 * q + p[0], 0, 0)),
                      pl.BlockSpec((1, sh, w), lambda q, p: (q, 0, 0))],
            out_specs=pl.BlockSpec((1, sh, w), lambda q, p: (q, 0, 0))),
        out_shape=_sds((4, sh, w), g.dtype),
        compiler_params=_params(),
    )(pos, g, got)


def _sum_adamw(chip, got, pos, w, m, v, layer, prior, *, name):
    _, sh, wd = chip.shape
    nl, rows, cols = w.shape
    nb, blk = 2, (sh // 2, wd)
    part = lambda n: pl.BlockSpec((n, sh // 2, wd), lambda i, p: ((p[1] if n == 1 else 0), i, 0))
    mine = pl.BlockSpec(blk, lambda i, p: (layer * nb + i, 0))
    flat = lambda t: t.reshape(nl * rows, cols)
    n_prior = 0 if prior is None else 4

    def body(pos_ref, c_ref, r_ref, w_ref, m_ref, v_ref, *refs):
        g_ref, d_ref, m2_ref, v2_ref = refs[n_prior:]
        g = c_ref[0].astype(F32)
        for s in range(3):
            g = g + r_ref[s].astype(F32)
        g_ref[...] = g
        d_ref[...], m2_ref[...], v2_ref[...] = _adamw_math(w_ref[...], g, m_ref[...], v_ref[...])

    outs = pl.pallas_call(
        body,
        name=name,
        grid_spec=pltpu.PrefetchScalarGridSpec(
            num_scalar_prefetch=1, grid=(nb,),
            in_specs=[part(1), part(3), mine, mine, mine] + [ANY] * n_prior,
            out_specs=[mine] * 4),
        out_shape=[_sds((nl * rows, cols), F32)] * 4,
        input_output_aliases={6 + k: k for k in range(n_prior)},
        compiler_params=_params(),
    )(pos, chip, got, flat(w), flat(m), flat(v), *(flat(t) for t in prior or ()))
    return [o.reshape(w.shape) for o in outs]


def _adamw_math(w, g, m, v):
    m2 = ADAM_B1 * m + (1.0 - ADAM_B1) * g
    v2 = ADAM_B2 * v + (1.0 - ADAM_B2) * (g * g)
    m_hat = m2 / (1.0 - ADAM_B1 ** ADAM_STEP)
    v_hat = v2 / (1.0 - ADAM_B2 ** ADAM_STEP)
    delta = -ADAM_LR * (m_hat / (jnp.sqrt(v_hat) + ADAM_EPS) + ADAM_WD * w)
    return delta, m2, v2


def _adamw(w, g, m, v, *, name, carry=None):
    shape = w.shape
    flat = [t.reshape(-1, shape[-1]) for t in (w, g, m, v)]
    rows, cols = flat[0].shape
    tr = rows // 8 if rows % 64 == 0 else rows
    spec = _rows(tr, cols)

    def body(w_ref, g_ref, m_ref, v_ref, d_ref, m2_ref, v2_ref):
        d_ref[...], m2_ref[...], v2_ref[...] = _adamw_math(w_ref[...], g_ref[...], m_ref[...], v_ref[...])

    outs, got = _call(body, name=name, grid=(rows // tr,), in_specs=[spec] * 4, out_specs=[spec] * 3,
                      out_shape=[_sds((rows, cols), F32)] * 3, scratch_shapes=[], args=flat, carry=carry)
    return tuple(o.reshape(shape) for o in outs), got


def _adds(tag, grads, got, *, pos):
    return [_add_pairs(g, r, pos, name=f"rs_add_{tag}_{i}") for i, (g, r) in enumerate(zip(grads, got))]


def _small_sums(packets, nf, dwp, dsc, dsk, *, name):
    flat = [p for layer in packets for p in layer]

    def total(ref, *idx):
        acc = ref[(0,) + idx]
        for dev in range(1, NDEV):
            acc = acc + ref[(dev,) + idx]
        return acc

    def body(*refs):
        pk = refs[:6]
        nf_ref, dwp0, dwp1, dsc0, dsc1, dsk0, dsk1 = refs[6:13]
        dm_ref, gb_ref, gn_ref, gnf_ref, gwp_ref, gps_ref, gsk_ref = refs[13:]
        dm_ref[...] = jnp.zeros_like(dm_ref)
        gn_ref[...] = jnp.zeros_like(gn_ref)
        for l in range(2):
            for sb in range(3):
                p = pk[3 * l + sb]
                for r in range(3):
                    col = slice((3 * sb + r) * D, (3 * sb + r + 1) * D)
                    lat = p[0, 0, r : r + 1, :]
                    dm_ref[l, 0:1, col] = lat
                    for dev in range(1, NDEV):
                        row = p[dev, 0, r : r + 1, :]
                        dm_ref[l, dev : dev + 1, col] = row
                        lat = lat + row
                    ctx = total(p, 1, slice(r, r + 1), slice(None))
                    dm_ref[l, 8:9, col] = ctx
                    gb_ref[l : l + 1, col] = lat + ctx
                gn_ref[l, sb : sb + 1, :] = total(p, 0, slice(3, 4), slice(None)) + total(p, 1, slice(3, 4), slice(None))
        gnf_ref[...] = total(nf_ref, slice(0, 1), slice(None))
        for l, (a, b, c) in enumerate(((dwp0, dsc0, dsk0), (dwp1, dsc1, dsk1))):
            gwp_ref[l] = total(a, slice(None), slice(None))
            gps_ref[l : l + 1, :] = total(b, slice(0, 1), slice(None))
            gsk_ref[l] = total(c, slice(None), slice(None))

    ins = flat + [nf, dwp[0], dwp[1], dsc[0], dsc[1], dsk[0], dsk[1]]
    return pl.pallas_call(
        body,
        name=name,
        out_shape=[_sds((2, 16, NMOD * D), F32), _sds((2, NMOD * D), F32), _sds((2, 8, D), F32), _sds((1, D), F32),
                   _sds((2, PW, 128), F32), _sds((2, PW), F32), _sds((2, 8, 128), F32)],
        compiler_params=pltpu.CompilerParams(vmem_limit_bytes=VMEM_LIMIT),
    )(*ins)


def _small_adamw(c_ctx, dc_all, triples, *, name):
    n = len(triples)

    def body(*refs):
        c_ref, dc_ref = refs[0], refs[1]
        ins = refs[2 : 2 + 4 * n - 1]
        outs = refs[2 + 4 * n - 1 :]
        acc = dc_ref[0, 0, 8:9, :] + dc_ref[0, 1, 8:9, :]
        for dev in range(1, NDEV):
            acc = acc + (dc_ref[dev, 0, 8:9, :] + dc_ref[dev, 1, 8:9, :])
        c = c_ref[...]
        sig = _sigmoid(c)
        g_c = acc * (sig * (1.0 + c * (1.0 - sig)))
        outs[0][...] = g_c
        pos = 0
        for k in range(n):
            if k == 0:
                w, g, m, v = ins[0][...], g_c, ins[1][...], ins[2][...]
                pos = 3
            else:
                w, g, m, v = (ins[pos + t][...] for t in range(4))
                pos += 4
            d, m2, v2 = _adamw_math(w, g, m, v)
            outs[1 + 3 * k][...], outs[2 + 3 * k][...], outs[3 + 3 * k][...] = d, m2, v2

    flat_in = [c_ctx, dc_all]
    out_shape = [_sds(c_ctx.shape, F32)]
    for k, (w, g, m, v) in enumerate(triples):
        flat_in += [w, m, v] if k == 0 else [w, g, m, v]
        out_shape += [_sds(w.shape, F32)] * 3
    return pl.pallas_call(body, name=name, out_shape=out_shape,
                          compiler_params=pltpu.CompilerParams(vmem_limit_bytes=VMEM_LIMIT))(*flat_in)


def _rope_tables(T, R):
    t = jnp.arange(T)
    inv = ROPE_BASE ** (-jnp.arange(0, HD // 2, 2, dtype=F32) / (HD // 2))
    ang = jnp.concatenate([(t // GRID_W).astype(F32)[:, None] * inv, (t % GRID_W).astype(F32)[:, None] * inv], axis=-1)
    cos = jnp.concatenate([jnp.tile(jnp.cos(ang), (1, 4)), jnp.ones((R - T, 128), F32)], axis=0)
    sin = jnp.concatenate([jnp.tile(jnp.sin(ang), (1, 4)), jnp.zeros((R - T, 128), F32)], axis=0)
    return cos, sin


def kernel(x, c, ctx, c_ctx, w_mod, b_mod, norm_ffn1, w_ffn1_in, w_ffn1_out, norm_mix, w_in, w_pool, pool_scale, sink, w_out, norm_ffn2, w_ffn2_in, w_ffn2_out, norm_final, loss_target, m_c_ctx, m_w_mod, m_b_mod, m_norm_ffn1, m_w_ffn1_in, m_w_ffn1_out, m_norm_mix, m_w_in, m_w_pool, m_pool_scale, m_sink, m_w_out, m_norm_ffn2, m_w_ffn2_in, m_w_ffn2_out, m_norm_final, v_c_ctx, v_w_mod, v_b_mod, v_norm_ffn1, v_w_ffn1_in, v_w_ffn1_out, v_norm_mix, v_w_in, v_w_pool, v_pool_scale, v_sink, v_w_out, v_norm_ffn2, v_w_ffn2_in, v_w_ffn2_out, v_norm_final):
    T = x.shape[1]
    R = T + LC
    nl = w_mod.shape[0]
    cx, cy, cc = _coords()
    me = _lin((cx, cy, cc))
    pos = jnp.stack([cc, 2 * cx + cy]).astype(jnp.int32)
    mcols = w_mod.shape[2]

    shards = [([w_ffn1_in[l].T.astype(BF16), w_ffn1_out[l].astype(BF16)],
               [w_in[l].T.astype(BF16), w_out[l].astype(BF16)],
               [w_ffn2_in[l].T.astype(BF16), w_ffn2_out[l].astype(BF16)]) for l in range(nl)]

    got = _exchange("ag_c_w", _merge(_gather_direct([c]), _gather_a(shards[0][0] + shards[0][1])))
    c_all, w_first = got[0], got[1:]
    c16 = jnp.concatenate([c_all.reshape(NDEV, D), c_ctx[None], jnp.zeros((16 - NDEV - 1, D), F32)], axis=0)
    b_cols = lax.dynamic_slice(b_mod, (0, me * mcols), (nl, mcols)).reshape(nl, 1, mcols)
    got = _exchange("ag_mod_w", _merge(_gather_b(w_first), _gather_direct([_mod_fwd(c16, w_mod, b_cols, name="mod_fwd")])))
    w_first, mod_all = got[:4], got[4]
    mod_all = jnp.transpose(mod_all, (1, 2, 0, 3)).reshape(nl, 16, NMOD, D)
    mine = lax.dynamic_index_in_dim(mod_all, me, axis=1, keepdims=False)
    pad = jnp.zeros((nl, 16 - NMOD, D), F32)
    modv = jnp.stack([jnp.concatenate([mine, pad], axis=1), jnp.concatenate([mod_all[:, 8], pad], axis=1)], axis=1)

    gvec = [jnp.concatenate([norm_ffn1[l][None], norm_mix[l][None], norm_ffn2[l][None], jnp.zeros((5, D), F32)], axis=0)
            for l in range(nl)]
    cos, sin = _rope_tables(T, R)
    ps2 = [pool_scale[l][None] for l in range(nl)]

    h = jnp.concatenate([x[0], ctx[0]], axis=0)
    loss_all, dh, small, nf_all, big, last_partials = _forward_backward(
        h, loss_target[0], modv, gvec, shards, w_first, cos, sin, sink, w_pool, ps2, norm_final, pos, T=T)
    loss = jnp.sum(loss_all[:, 0, 0])
    grad_x = dh[:T][None]

    dm, g_b_mod, g_norms, g_nf, g_wp, g_ps, g_sk = _small_sums(
        [small[l][0:3] for l in range(nl)], nf_all, *[[small[l][k] for l in range(nl)] for k in (3, 4, 5)],
        name="small_sums")
    dm_cols = lax.dynamic_slice(dm, (0, 0, me * mcols), (nl, 16, mcols))
    g_w_mod, dc_part = _mod_bwd(c16, dm_cols, w_mod, name="mod_bwd")
    got = _exchange("rs1_tail", _merge(_scatter_1(last_partials), _gather_direct([dc_part])))
    c1, dc_all = _adds("ffn1_0", last_partials, got[:2], pos=pos), got[2]

    delta, new_m, new_v = {}, {}, {}
    (delta["w_mod"], new_m["w_mod"], new_v["w_mod"]), got = _adamw(
        w_mod, g_w_mod, m_w_mod, v_w_mod, name="adamw_w_mod", carry=_scatter_2(c1))
    big[0][0:2] = [(c1[0], got[0]), (c1[1], got[1])]

    grads = {
        "b_mod": g_b_mod, "norm_ffn1": g_norms[:, 0], "norm_mix": g_norms[:, 1], "norm_ffn2": g_norms[:, 2],
        "w_pool": g_wp.reshape(w_pool.shape), "pool_scale": g_ps, "sink": g_sk[:, :, 0], "norm_final": g_nf.reshape(D),
        "w_mod": g_w_mod,
    }
    weights = dict(c_ctx=c_ctx, w_mod=w_mod, b_mod=b_mod, norm_ffn1=norm_ffn1, w_ffn1_in=w_ffn1_in, w_ffn1_out=w_ffn1_out,
                   norm_mix=norm_mix, w_in=w_in, w_pool=w_pool, pool_scale=pool_scale, sink=sink, w_out=w_out,
                   norm_ffn2=norm_ffn2, w_ffn2_in=w_ffn2_in, w_ffn2_out=w_ffn2_out, norm_final=norm_final)
    moms = dict(c_ctx=(m_c_ctx, v_c_ctx), w_mod=(m_w_mod, v_w_mod), b_mod=(m_b_mod, v_b_mod),
                norm_ffn1=(m_norm_ffn1, v_norm_ffn1), w_ffn1_in=(m_w_ffn1_in, v_w_ffn1_in),
                w_ffn1_out=(m_w_ffn1_out, v_w_ffn1_out), norm_mix=(m_norm_mix, v_norm_mix), w_in=(m_w_in, v_w_in),
                w_pool=(m_w_pool, v_w_pool), pool_scale=(m_pool_scale, v_pool_scale), sink=(m_sink, v_sink),
                w_out=(m_w_out, v_w_out), norm_ffn2=(m_norm_ffn2, v_norm_ffn2), w_ffn2_in=(m_w_ffn2_in, v_w_ffn2_in),
                w_ffn2_out=(m_w_ffn2_out, v_w_ffn2_out), norm_final=(m_norm_final, v_norm_final))
    order = list(weights)
    small_names = ["c_ctx", "b_mod", "norm_ffn1", "norm_mix", "w_pool", "pool_scale", "sink", "norm_ffn2", "norm_final"]

    def as2d(name, t):
        if name == "w_pool":
            return t.reshape(-1, 128)
        return t.reshape(1, -1) if t.ndim == 1 else t

    triples = [(as2d(n, weights[n]), None if n == "c_ctx" else as2d(n, grads[n]), as2d(n, moms[n][0]), as2d(n, moms[n][1]))
               for n in small_names]
    outs = _small_adamw(as2d("c_ctx", c_ctx), dc_all, triples, name="small_adamw")
    grads["c_ctx"] = outs[0].reshape(c_ctx.shape)
    for k, n in enumerate(small_names):
        delta[n], new_m[n], new_v[n] = (o.reshape(weights[n].shape) for o in outs[1 + 3 * k : 4 + 3 * k])
    for k, n in enumerate(["w_ffn1_in", "w_ffn1_out", "w_in", "w_out", "w_ffn2_in", "w_ffn2_out"]):
        turn = (lambda t: jnp.swapaxes(t, 1, 2)) if k % 2 == 0 else (lambda t: t)
        wmv = [turn(t) for t in (weights[n], *moms[n])]
        outs = None
        for l in reversed(range(nl)):
            outs = _sum_adamw(*big[l][k], pos, *wmv, l, outs, name=f"adamw_{n}_{l}")
        grads[n], delta[n], new_m[n], new_v[n] = (turn(o) for o in outs)

    return (loss, grad_x, *[grads[n] for n in order], *[delta[n] for n in order],
            *[new_m[n] for n in order], *[new_v[n] for n in order])


def _merge(*rounds):
    ins, outs, plan, local, n_alias = [], [], [], [], 0
    for r in rounds:
        assert r.n_alias == 0 or (not ins and r.n_alias == len(r.ins) == len(r.out_shapes))
        oi, oo = len(ins), len(outs)
        plan += [(k, None if i is None else i + oi, sf, o + oo, df) for k, i, sf, o, df in r.plan]
        local += [(i + oi, sf, o + oo, df) for i, sf, o, df in r.local_plan]
        ins += r.ins
        outs += r.out_shapes
        n_alias += r.n_alias
    return _Round(ins, outs, plan, local, n_alias)


def _forward_backward(h, target, modv, gvec, shards, w_first, cos, sin, sink, w_pool, ps2, norm_final, pos, *, T):
    nl = len(gvec)
    flat = lambda ws: [w.reshape(-1, D) for w in ws]
    saved = []
    w1, wm = flat(w_first[:2]), flat(w_first[2:])
    for l in range(nl):
        last = l == nl - 1
        h0 = h
        if l == 0:
            (h1, a1, b1, f1), got = _ffn_fwd(h0, modv[l], gvec[l], *w1, T=T, mrow=0, grow=0, ctx_active=True,
                                             name=f"ffn1_fwd_{l}", carry=_gather_a(shards[l][2]))
            (u, q, k4, v4), got = _mixproj_fwd(h1, modv[l], gvec[l], wm[0], cos, sin, T=T, name=f"mixproj_fwd_{l}",
                                               carry=_gather_b(got))
            w2 = flat(got)
        else:
            (h1, a1, b1, f1), got = _ffn_fwd(h0, modv[l], gvec[l], *w1, T=T, mrow=0, grow=0, ctx_active=True,
                                             name=f"ffn1_fwd_{l}", carry=_gather_b(nxt_m + nxt_2))
            wm, w2 = flat(got[:2]), flat(got[2:])
            (u, q, k4, v4), _ = _mixproj_fwd(h1, modv[l], gvec[l], wm[0], cos, sin, T=T, name=f"mixproj_fwd_{l}")
        (cat,), nxt_1 = _attnpool_fwd(u, q, k4, v4, sink[l], w_pool[l], ps2[l], T=T, name=f"attnpool_fwd_{l}",
                                      carry=None if last else _gather_a(shards[l + 1][0]))
        (h2, mo), nxt_m = _mixout_fwd(h1, cat, modv[l], wm[1], T=T, ctx_active=not last, name=f"mixout_fwd_{l}",
                                      carry=None if last else _gather_a(shards[l + 1][1]))
        (h3, a2, b2, f2), got = _ffn_fwd(h2, modv[l], gvec[l], *w2, T=T, mrow=6, grow=2, ctx_active=not last,
                                         name=f"ffn2_fwd_{l}",
                                         carry=None if last else _merge(_gather_b(nxt_1), _gather_a(shards[l + 1][2])))
        saved.append((h0, a1, b1, f1, h1, u, q, k4, v4, cat, mo, h2, a2, b2, f2, w1, wm, w2))
        h = h3
        if not last:
            w1, nxt_2 = flat(got[:2]), got[2:]

    dh, loss_part, dnf = _loss_head(h, target, norm_final[None], T=T, name="loss_head")

    adds = functools.partial(_adds, pos=pos)
    small, big = [None] * nl, {}
    prev = None
    for l in reversed(range(nl)):
        last = l == nl - 1
        h0, a1, b1, f1, h1, u, q, k4, v4, cat, mo, h2, a2, b2, f2, w1, wm, w2 = saved[l]
        (dh, dab, s, n, df, pk2), got = _ffn_bwd(
            h2, dh, a2, b2, f2, modv[l], gvec[l], *w2, T=T, mrow=6, grow=2, ctx_active=not last, name=f"ffn2_bwd_{l}",
            carry=_merge(_scatter_1(prev[0]), _gather_a(prev[1])) if prev else None)
        if prev:
            c1, small_a = adds(f"ffn1_{l + 1}", prev[0], got[:2]), got[2:]
        g_w2i, got = _wgrad(dab, n, bk=WG_BK, sh=2 * DFF // NDEV, name=f"wgrad_ffn2_in_{l}",
                            carry=_scatter_2(c1[:1]) if prev else None)
        if prev:
            big[l + 1][0] = (c1[0], got[0])
        g_w2o, got = _wgrad(s, df, bk=WG_BK, sh=DFF // NDEV, name=f"wgrad_ffn2_out_{l}",
                            carry=_scatter_2(c1[1:]) if prev else None)
        if prev:
            big[l + 1][1] = (c1[1], got[0])
        rnd = _scatter_1([g_w2i, g_w2o])
        (dcat, dmix, pko), got = _mixout_bwd(dh, mo, modv[l], wm[1], T=T, ctx_active=not last, name=f"mixout_bwd_{l}",
                                             carry=_merge(_gather_b(small_a), rnd) if prev else rnd)
        if prev:
            small[l + 1], got = got[: len(small_a)], got[len(small_a) :]
        c2 = adds(f"ffn2_{l}", [g_w2i, g_w2o], got)
        g_wo, _ = _wgrad(cat, dmix, bk=D, sh=D // NDEV, name=f"wgrad_out_{l}")
        dps, dwp, dsc = _pool_bwd(u, dcat, w_pool[l], ps2[l], T=T, name=f"pool_bwd_{l}")
        (du, dq, dk, dv, dsk), got = _attn_bwd(q, k4, v4, dcat, dps, sink[l], T=T, name=f"attn_bwd_{l}", carry=_scatter_2(c2))
        big[l] = [None, None, None, None, (c2[0], got[0]), (c2[1], got[1])]
        dh, dproj, n, pkm = _mixproj_bwd(h1, dh, du, dq, dk, dv, modv[l], gvec[l], wm[0], cos, sin, T=T, name=f"mixproj_bwd_{l}")
        g_wi, _ = _wgrad(dproj, n, bk=PROJ, sh=PROJ // NDEV, name=f"wgrad_in_{l}")
        (dh, dab, s, n, df, pk1), got = _ffn_bwd(h0, dh, a1, b1, f1, modv[l], gvec[l], *w1, T=T, mrow=0, grow=0,
                                                 ctx_active=True, name=f"ffn1_bwd_{l}", carry=_scatter_1([g_wi, g_wo]))
        cm = adds(f"mix_{l}", [g_wi, g_wo], got)
        mine = [pk1, pkm + pko, pk2, dwp, dsc, dsk]
        rnd = _merge(_scatter_2(cm), _gather_a(mine + [dnf, loss_part])) if l == 0 else _scatter_2(cm)
        g_w1i, got = _wgrad(dab, n, bk=WG_BK, sh=2 * DFF // NDEV, name=f"wgrad_ffn1_in_{l}", carry=rnd)
        big[l][2:4] = [(cm[0], got[0]), (cm[1], got[1])]
        g_w1o, got = _wgrad(s, df, bk=WG_BK, sh=DFF // NDEV, name=f"wgrad_ffn1_out_{l}",
                            carry=_gather_b(got[2:]) if l == 0 else None)
        prev = ([g_w1i, g_w1o], mine)
    small[0], nf_all, loss_all = got[:6], got[6], got[7]
    return loss_all, dh, small, nf_all, big, prev[0]
```

```python
import functools

import jax
import jax.numpy as jnp
from jax import lax
from jax.experimental import pallas as pl
from jax.experimental.pallas import tpu as pltpu

F32, BF16 = jnp.float32, jnp.bfloat16

D = 1024
LC = 256
DFF = 2816
NMOD = 9
PW = 512
AW = 512
KVW = 128
PROJ = PW + AW + 2 * KVW
HD = 64
BLK = 128
GRID_W = 64
POOL_WINDOWS = (2, 4, 8, 16)
EPS = 1e-6
NEG = -1e30
ROPE_BASE = 10000.0
NDEV = 8
MESH = pl.DeviceIdType.MESH

ADAM_LR, ADAM_B1, ADAM_B2, ADAM_EPS, ADAM_WD, ADAM_STEP = 0.001, 0.9, 0.999, 1e-08, 0.01, 10

VMEM_LIMIT = 56 * 1024 * 1024
TM = 256
FFN_CHUNKS = ((0, 1536), (1536, 1280))
WG_BK = 1408

ANY = pl.BlockSpec(memory_space=pl.ANY)
SMEM = pl.BlockSpec(memory_space=pltpu.SMEM)


def _params(ngrid=1):
    return pltpu.CompilerParams(dimension_semantics=("arbitrary",) * ngrid, vmem_limit_bytes=VMEM_LIMIT)


def _dot(a, b):
    return jnp.dot(a, b, preferred_element_type=F32)


def _dot_nt(a, b):
    return lax.dot_general(a, b, (((1,), (1,)), ((), ())), preferred_element_type=F32)


def _dot_tn(a, b):
    return lax.dot_general(a, b, (((0,), (0,)), ((), ())), preferred_element_type=F32)


def _sigmoid(x):
    return 1.0 / (1.0 + jnp.exp(-x))


def _rows(tm, w):
    return pl.BlockSpec((tm, w), lambda i: (i, 0))


def _full(shape):
    nd = len(shape)
    return pl.BlockSpec(shape, lambda *_: (0,) * nd)


def _sds(shape, dtype):
    return jax.ShapeDtypeStruct(shape, dtype)


def _norm_mod(h, g, shift, scale):
    r = lax.rsqrt(jnp.mean(h * h, axis=-1, keepdims=True) + EPS)
    xhat = h * r
    y = xhat * g
    return r, xhat, y, y * (1.0 + scale) + shift


def _norm_mod_bwd(dn, r, xhat, y, g, scale):
    dshift = jnp.sum(dn, axis=0, keepdims=True)
    dscale = jnp.sum(dn * y, axis=0, keepdims=True)
    dy = dn * (1.0 + scale)
    dg = jnp.sum(dy * xhat, axis=0, keepdims=True)
    dxh = dy * g
    dh = r * (dxh - xhat * jnp.mean(dxh * xhat, axis=-1, keepdims=True))
    return dh, dshift, dscale, dg


def _acc_partials(part_ref, first, rows):
    @pl.when(first)
    def _():
        part_ref[...] = jnp.zeros_like(part_ref)

    for r, val in rows.items():
        part_ref[0, r : r + 1, :] += val


def _mod_spec(n_lat):
    return pl.BlockSpec((1, 16, D), lambda i: (i // n_lat, 0, 0))


def _part_spec(n_lat):
    return pl.BlockSpec((1, 8, D), lambda i: (i // n_lat, 0, 0))


def _load_weights(pairs, sem):
    copies = [pltpu.make_async_copy(src, dst, sem.at[k]) for k, (src, dst) in enumerate(pairs)]
    for cp in copies:
        cp.start()
    for cp in copies:
        cp.wait()


def _ffn_weight_copies(win_hbm, wout_hbm, win_v, wout_v, sem):
    loads = []
    for k, (c0, cw) in enumerate(FFN_CHUNKS):
        slabs = [(win_hbm, win_v, c0), (win_hbm, win_v, DFF + c0), (wout_hbm, wout_v, c0)]
        loads.append([pltpu.make_async_copy(src.at[pl.ds(r0, cw)], dst.at[pl.ds(r0, cw)], sem.at[3 * k + j])
                      for j, (src, dst, r0) in enumerate(slabs)])
    return loads


def _ffn_steps(i, n_active, loads, compute):
    @pl.when(i == 0)
    def _():
        for cp in sum(loads, []):
            cp.start()
        compute(loads)

    @pl.when(jnp.logical_and(i > 0, i < n_active))
    def _():
        compute(None)


def _wait_chunk(loads, k):
    if loads is not None:
        for cp in loads[k]:
            cp.wait()


def _ffn_fwd(h, modv, gvec, win, wout, *, T, mrow, grow, ctx_active, name, carry=None):
    R = h.shape[0]
    n_lat, n_tiles = T // TM, R // TM
    n_active = n_tiles if ctx_active else n_lat

    def body(h_ref, mod_ref, g_ref, win_hbm, wout_hbm, ho_ref, a_ref, b_ref, f_ref, win_v, wout_v, sem):
        i = pl.program_id(0)

        def compute(loads):
            h = h_ref[...]
            shift, scale, gate = (mod_ref[0, mrow + k : mrow + k + 1, :] for k in range(3))
            _, _, _, n = _norm_mod(h, g_ref[grow : grow + 1, :], shift, scale)
            n_bf = n.astype(BF16)
            acc = jnp.zeros((TM, D), F32)
            for k, (c0, cw) in enumerate(FFN_CHUNKS):
                _wait_chunk(loads, k)
                a = _dot_nt(n_bf, win_v[c0 : c0 + cw, :])
                b = _dot_nt(n_bf, win_v[DFF + c0 : DFF + c0 + cw, :])
                a_ref[:, c0 : c0 + cw] = a.astype(BF16)
                b_ref[:, c0 : c0 + cw] = b.astype(BF16)
                s = a * _sigmoid(a) * b
                acc = acc + _dot(s.astype(BF16), wout_v[c0 : c0 + cw, :])
            f_ref[...] = acc.astype(BF16)
            ho_ref[...] = h + (0.5 * gate) * acc

        _ffn_steps(i, n_active, _ffn_weight_copies(win_hbm, wout_hbm, win_v, wout_v, sem), compute)

        @pl.when(i >= n_active)
        def _():
            ho_ref[...] = h_ref[...]
            a_ref[...] = jnp.zeros_like(a_ref)
            b_ref[...] = jnp.zeros_like(b_ref)
            f_ref[...] = jnp.zeros_like(f_ref)

    return _call(
        body,
        name=name,
        grid=(n_tiles,),
        in_specs=[_rows(TM, D), _mod_spec(n_lat), _full((8, D)), ANY, ANY],
        out_specs=[_rows(TM, D), _rows(TM, DFF), _rows(TM, DFF), _rows(TM, D)],
        out_shape=[_sds((R, D), F32), _sds((R, DFF), BF16), _sds((R, DFF), BF16), _sds((R, D), BF16)],
        scratch_shapes=[pltpu.VMEM((2 * DFF, D), BF16), pltpu.VMEM((DFF, D), BF16),
                        pltpu.SemaphoreType.DMA((3 * len(FFN_CHUNKS),))],
        args=(h, modv, gvec, win, wout),
        carry=carry,
    )


def _ffn_bwd(h, dho, a, b, f, modv, gvec, win, wout, *, T, mrow, grow, ctx_active, name, carry=None):
    R = h.shape[0]
    n_lat, n_tiles = T // TM, R // TM
    n_active = n_tiles if ctx_active else n_lat

    def body(h_ref, dho_ref, a_ref, b_ref, f_ref, mod_ref, g_ref, win_hbm, wout_hbm,
             dh_ref, dab_ref, s_ref, n_ref, df_ref, part_ref, win_v, wout_v, sem):
        i = pl.program_id(0)
        first = jnp.logical_or(i == 0, i == n_lat)

        def compute(loads):
            h = h_ref[...]
            dho = dho_ref[...]
            shift, scale, gate = (mod_ref[0, mrow + k : mrow + k + 1, :] for k in range(3))
            g = g_ref[grow : grow + 1, :]
            r, xhat, y, n = _norm_mod(h, g, shift, scale)
            dgate = 0.5 * jnp.sum(dho * f_ref[...].astype(F32), axis=0, keepdims=True)
            df_bf = ((0.5 * gate) * dho).astype(BF16)
            df_ref[...] = df_bf
            n_ref[...] = n.astype(BF16)
            dn = jnp.zeros((TM, D), F32)
            for k, (c0, cw) in enumerate(FFN_CHUNKS):
                _wait_chunk(loads, k)
                ds = _dot_nt(df_bf, wout_v[c0 : c0 + cw, :])
                av = a_ref[:, c0 : c0 + cw].astype(F32)
                bv = b_ref[:, c0 : c0 + cw].astype(F32)
                sig = _sigmoid(av)
                sa = av * sig
                s_ref[:, c0 : c0 + cw] = (sa * bv).astype(BF16)
                da = (ds * bv * (sig * (1.0 + av * (1.0 - sig)))).astype(BF16)
                db = (ds * sa).astype(BF16)
                dab_ref[:, c0 : c0 + cw] = da
                dab_ref[:, DFF + c0 : DFF + c0 + cw] = db
                dn = dn + _dot(da, win_v[c0 : c0 + cw, :]) + _dot(db, win_v[DFF + c0 : DFF + c0 + cw, :])
            dh, dshift, dscale, dg = _norm_mod_bwd(dn, r, xhat, y, g, scale)
            dh_ref[...] = dho + dh
            _acc_partials(part_ref, first, {0: dshift, 1: dscale, 2: dgate, 3: dg})

        _ffn_steps(i, n_active, _ffn_weight_copies(win_hbm, wout_hbm, win_v, wout_v, sem), compute)

        @pl.when(i >= n_active)
        def _():
            dh_ref[...] = dho_ref[...]
            dab_ref[...] = jnp.zeros_like(dab_ref)
            s_ref[...] = jnp.zeros_like(s_ref)
            n_ref[...] = jnp.zeros_like(n_ref)
            df_ref[...] = jnp.zeros_like(df_ref)
            part_ref[...] = jnp.zeros_like(part_ref)

    return _call(
        body,
        name=name,
        grid=(n_tiles,),
        in_specs=[_rows(TM, D), _rows(TM, D), _rows(TM, DFF), _rows(TM, DFF), _rows(TM, D),
                  _mod_spec(n_lat), _full((8, D)), ANY, ANY],
        out_specs=[_rows(TM, D), _rows(TM, 2 * DFF), _rows(TM, DFF), _rows(TM, D), _rows(TM, D), _part_spec(n_lat)],
        out_shape=[_sds((R, D), F32), _sds((R, 2 * DFF), BF16), _sds((R, DFF), BF16), _sds((R, D), BF16),
                   _sds((R, D), BF16), _sds((2, 8, D), F32)],
        scratch_shapes=[pltpu.VMEM((2 * DFF, D), BF16), pltpu.VMEM((DFF, D), BF16),
                        pltpu.SemaphoreType.DMA((3 * len(FFN_CHUNKS),))],
        args=(h, dho, a, b, f, modv, gvec, win, wout),
        carry=carry,
    )


def _wgrad(x, y, *, bk, sh, name, carry=None):
    R, kx = x.shape
    n = y.shape[1]
    tr = R // 2
    nr, nsh = R // tr, bk // sh

    def body(x_ref, y_ref, o_ref, acc):
        r = pl.program_id(1)

        @pl.when(r == 0)
        def _():
            acc[...] = jnp.zeros_like(acc)

        acc[...] += _dot_tn(x_ref[...], y_ref[...])

        @pl.when(r == nr - 1)
        def _():
            for s in range(nsh):
                o_ref[s] = acc[s * sh : (s + 1) * sh, :].astype(BF16)

    (out,), got = _call(
        body,
        name=name,
        grid=(kx // bk, nr),
        in_specs=[pl.BlockSpec((tr, bk), lambda k, r: (r, k)), pl.BlockSpec((tr, n), lambda k, r: (r, 0))],
        out_specs=[pl.BlockSpec((nsh, sh, n), lambda k, r: (k, 0, 0))],
        out_shape=[_sds((kx // sh, sh, n), BF16)],
        scratch_shapes=[pltpu.VMEM((bk, n), F32)],
        args=(x, y),
        carry=carry,
    )
    return out, got


def _rot_half(x):
    lane = lax.broadcasted_iota(jnp.int32, x.shape, 1)
    return jnp.where((lane & (HD - 1)) < HD // 2, -pltpu.roll(x, 128 - HD // 2, 1), pltpu.roll(x, HD // 2, 1))


def _tile_sel():
    i = lax.broadcasted_iota(jnp.int32, (KVW, AW), 0)
    j = lax.broadcasted_iota(jnp.int32, (KVW, AW), 1)
    return jnp.where(i == (j // 256) * HD + (j & (HD - 1)), 1.0, 0.0).astype(BF16)


def _mixproj_fwd(h, modv, gvec, win, cos, sin, *, T, name, carry=None):
    R = h.shape[0]
    n_lat, n_tiles = T // TM, R // TM

    def body(h_ref, mod_ref, g_ref, win_ref, cos_ref, sin_ref, u_ref, q_ref, k4_ref, v4_ref):
        shift, scale = mod_ref[0, 3:4, :], mod_ref[0, 4:5, :]
        _, _, _, n = _norm_mod(h_ref[...], g_ref[1:2, :], shift, scale)
        proj = _dot_nt(n.astype(BF16), win_ref[...])
        u_ref[...] = proj[:, :PW]
        cs, sn = cos_ref[...], sin_ref[...]
        for s in range(AW // 128):
            x = proj[:, PW + 128 * s : PW + 128 * (s + 1)]
            q_ref[:, 128 * s : 128 * (s + 1)] = ((x * cs + _rot_half(x) * sn) * (HD ** -0.5)).astype(BF16)
        k = proj[:, PW + AW : PW + AW + KVW]
        k = (k * cs + _rot_half(k) * sn).astype(BF16)
        v = proj[:, PW + AW + KVW :].astype(BF16)
        sel = _tile_sel()
        k4_ref[...] = _dot(k, sel).astype(BF16)
        v4_ref[...] = _dot(v, sel).astype(BF16)

    return _call(
        body,
        name=name,
        grid=(n_tiles,),
        in_specs=[_rows(TM, D), _mod_spec(n_lat), _full((8, D)), _full((PROJ, D)), _rows(TM, 128), _rows(TM, 128)],
        out_specs=[_rows(TM, PW), _rows(TM, AW), _rows(TM, AW), _rows(TM, AW)],
        out_shape=[_sds((R, PW), F32), _sds((R, AW), BF16), _sds((R, AW), BF16), _sds((R, AW), BF16)],
        scratch_shapes=[],
        args=(h, modv, gvec, win, cos, sin),
        carry=carry,
    )


def _win_start(j, hi):
    return pl.multiple_of(jnp.clip((j - 1) * BLK, 0, hi - 3 * BLK), BLK)


def _hi_lo(x):
    hi = x.astype(BF16)
    return hi, (x - hi.astype(F32)).astype(BF16)


def _pool_bounds(t, w, T, R):
    is_ctx = t >= T
    lo = jnp.maximum(t - w // 2, jnp.where(is_ctx, T, 0))
    hi = jnp.minimum(t + w // 2, jnp.where(is_ctx, R, T))
    return lo, hi


def _pooled(u_v, j, T, R):
    start = _win_start(j, R)
    u3_hi, u3_lo = _hi_lo(u_v[pl.ds(start, 3 * BLK), :])
    ub = u_v[pl.ds(pl.multiple_of(j * BLK, BLK), BLK), :]
    t = j * BLK + lax.broadcasted_iota(jnp.int32, (BLK, 1), 0)
    pos = start + lax.broadcasted_iota(jnp.int32, (1, 3 * BLK), 1)
    pooled, counts = [], []
    for g, w in enumerate(POOL_WINDOWS):
        lo, hi = _pool_bounds(t, w, T, R)
        band = jnp.where(pos >= lo, jnp.where(pos < hi, 1.0, 0.0), 0.0).astype(BF16)
        sl = slice(g * 128, (g + 1) * 128)
        sums = _dot(band, u3_hi[:, sl]) + _dot(band, u3_lo[:, sl])
        cnt = (hi - lo).astype(F32)
        pooled.append(sums / cnt - ub[:, sl])
        counts.append(cnt)
    return pooled, counts


def _stack_heads(x):
    lane_h = lax.broadcasted_iota(jnp.int32, x.shape, 1) // HD
    return jnp.concatenate([jnp.where(lane_h == h, x, jnp.zeros_like(x)) for h in range(4)], axis=0)


def _unstack_heads(x):
    lane_h = lax.broadcasted_iota(jnp.int32, (BLK, 256), 1) // HD
    out = jnp.zeros((BLK, 256), F32)
    for h in range(4):
        out = out + jnp.where(lane_h == h, x[h * BLK : (h + 1) * BLK, :], 0.0)
    return out


def _window_mask(j, start_l, nbl):
    rowi = lax.broadcasted_iota(jnp.int32, (4 * BLK, 1), 0)
    qpos = j * BLK + (rowi & (BLK - 1))
    kpos = start_l + lax.broadcasted_iota(jnp.int32, (1, 3 * BLK), 1)
    reach = jnp.where(j < nbl, BLK, -1)
    return jnp.abs(kpos - qpos) <= reach


def _attn_exps(qs, kl, kc, sink_ref, g, valid):
    s_l = jnp.where(valid, _dot_nt(qs, kl), NEG)
    s_c = _dot_nt(qs, kc)
    rb = lax.broadcasted_iota(jnp.int32, (4 * BLK, 1), 0) // BLK
    sk = jnp.where(rb == 0, sink_ref[4 * g], jnp.where(rb == 1, sink_ref[4 * g + 1],
                   jnp.where(rb == 2, sink_ref[4 * g + 2], sink_ref[4 * g + 3])))
    m = jnp.maximum(jnp.maximum(jnp.max(s_l, axis=1, keepdims=True), jnp.max(s_c, axis=1, keepdims=True)), sk)
    e_l, e_c, e_s = jnp.exp(s_l - m), jnp.exp(s_c - m), jnp.exp(sk - m)
    inv = 1.0 / (jnp.sum(e_l, axis=1, keepdims=True) + jnp.sum(e_c, axis=1, keepdims=True) + e_s)
    return e_l, e_c, e_s, inv


def _attnpool_fwd(u, q, k4, v4, sink, w_pool, pool_scale, *, T, name, carry=None):
    R = u.shape[0]
    nb, nbl = R // BLK, T // BLK

    def body(q_ref, sink_ref, wp_ref, ps_ref, u_hbm, k4_hbm, v4_hbm, cat_ref, u_v, k4_v, v4_v, sem):
        j = pl.program_id(0)

        @pl.when(j == 0)
        def _():
            _load_weights([(u_hbm, u_v), (k4_hbm, k4_v), (v4_hbm, v4_v)], sem)

        pooled, _ = _pooled(u_v, j, T, R)
        for g in range(4):
            mixed = _dot(pooled[g].astype(BF16), wp_ref[g].astype(BF16)) * ps_ref[:, g * 128 : (g + 1) * 128]
            cat_ref[:, g * 128 : (g + 1) * 128] = mixed.astype(BF16)

        start_l = _win_start(j, T)
        valid = _window_mask(j, start_l, nbl)
        for g in range(2):
            gl = slice(g * 256, (g + 1) * 256)
            qs = _stack_heads(q_ref[:, gl])
            e_l, e_c, _, inv = _attn_exps(qs, k4_v[pl.ds(start_l, 3 * BLK), gl], k4_v[T:R, gl], sink_ref, g, valid)
            o = _dot(e_l.astype(BF16), v4_v[pl.ds(start_l, 3 * BLK), gl]) + _dot(e_c.astype(BF16), v4_v[T:R, gl])
            cat_ref[:, PW + g * 256 : PW + (g + 1) * 256] = _unstack_heads(o * inv).astype(BF16)

    return _call(
        body,
        name=name,
        grid=(nb,),
        in_specs=[_rows(BLK, AW), SMEM, _full((4, 128, 128)), _full((1, PW)), ANY, ANY, ANY],
        out_specs=[_rows(BLK, D)],
        out_shape=[_sds((R, D), BF16)],
        scratch_shapes=[pltpu.VMEM((R, PW), F32), pltpu.VMEM((R, AW), BF16), pltpu.VMEM((R, AW), BF16),
                        pltpu.SemaphoreType.DMA((3,))],
        args=(q, sink, w_pool, pool_scale, u, k4, v4),
        carry=carry,
    )


def _mixout_fwd(h, cat, modv, wout, *, T, ctx_active, name, carry=None):
    R = h.shape[0]
    n_lat, n_tiles = T // TM, R // TM

    def body(h_ref, cat_ref, mod_ref, w_ref, ho_ref, mo_ref):
        i = pl.program_id(0)

        def compute():
            mo = _dot(cat_ref[...], w_ref[...])
            mo_ref[...] = mo.astype(BF16)
            ho_ref[...] = h_ref[...] + mod_ref[0, 5:6, :] * mo

        if ctx_active:
            compute()
        else:
            pl.when(i < n_lat)(compute)

            @pl.when(i >= n_lat)
            def _():
                ho_ref[...] = h_ref[...]
                mo_ref[...] = jnp.zeros_like(mo_ref)

    return _call(
        body,
        name=name,
        grid=(n_tiles,),
        in_specs=[_rows(TM, D), _rows(TM, D), _mod_spec(n_lat), _full((D, D))],
        out_specs=[_rows(TM, D), _rows(TM, D)],
        out_shape=[_sds((R, D), F32), _sds((R, D), BF16)],
        scratch_shapes=[],
        args=(h, cat, modv, wout),
        carry=carry,
    )


def _mixout_bwd(dho, mo, modv, wout, *, T, ctx_active, name, carry=None):
    R = dho.shape[0]
    n_lat, n_tiles = T // TM, R // TM

    def body(dho_ref, mo_ref, mod_ref, w_ref, dcat_ref, dmix_ref, part_ref):
        i = pl.program_id(0)
        first = jnp.logical_or(i == 0, i == n_lat)

        def compute():
            dho = dho_ref[...]
            dmix = (mod_ref[0, 5:6, :] * dho).astype(BF16)
            dmix_ref[...] = dmix
            dcat_ref[...] = _dot_nt(dmix, w_ref[...])
            dgate = jnp.sum(dho * mo_ref[...].astype(F32), axis=0, keepdims=True)
            _acc_partials(part_ref, first, {2: dgate})

        if ctx_active:
            compute()
        else:
            pl.when(i < n_lat)(compute)

            @pl.when(i >= n_lat)
            def _():
                dcat_ref[...] = jnp.zeros_like(dcat_ref)
                dmix_ref[...] = jnp.zeros_like(dmix_ref)
                part_ref[...] = jnp.zeros_like(part_ref)

    return _call(
        body,
        name=name,
        grid=(n_tiles,),
        in_specs=[_rows(TM, D), _rows(TM, D), _mod_spec(n_lat), _full((D, D))],
        out_specs=[_rows(TM, D), _rows(TM, D), _part_spec(n_lat)],
        out_shape=[_sds((R, D), F32), _sds((R, D), BF16), _sds((2, 8, D), F32)],
        scratch_shapes=[],
        args=(dho, mo, modv, wout),
        carry=carry,
    )


def _pool_bwd(u, dcat, w_pool, pool_scale, *, T, name):
    R = u.shape[0]
    nb = R // BLK

    def body(dcat_ref, wp_ref, ps_ref, u_hbm, dps_ref, dwp_ref, dsc_ref, u_v, sem):
        j = pl.program_id(0)

        @pl.when(j == 0)
        def _():
            _load_weights([(u_hbm, u_v)], sem)
            dwp_ref[...] = jnp.zeros_like(dwp_ref)
            dsc_ref[...] = jnp.zeros_like(dsc_ref)

        pooled, counts = _pooled(u_v, j, T, R)
        for g in range(4):
            sl = slice(g * 128, (g + 1) * 128)
            p_bf = pooled[g].astype(BF16)
            w_bf = wp_ref[g].astype(BF16)
            dmixed = dcat_ref[:, sl]
            dsc_ref[0:1, sl] += jnp.sum(dmixed * _dot(p_bf, w_bf), axis=0, keepdims=True)
            dmp = (dmixed * ps_ref[:, sl]).astype(BF16)
            dwp_ref[sl, :] += _dot_tn(p_bf, dmp)
            dps_ref[:, sl] = _dot_nt(dmp, w_bf) / counts[g]

    return pl.pallas_call(
        body,
        name=name,
        grid=(nb,),
        in_specs=[_rows(BLK, D), _full((4, 128, 128)), _full((1, PW)), ANY],
        out_specs=[_rows(BLK, PW), _full((PW, 128)), _full((8, PW))],
        out_shape=[_sds((R, PW), F32), _sds((PW, 128), F32), _sds((8, PW), F32)],
        scratch_shapes=[pltpu.VMEM((R, PW), F32), pltpu.SemaphoreType.DMA((1,))],
        compiler_params=_params(),
    )(dcat, w_pool, pool_scale, u)


def _fold_heads(x):
    y = x[:, :128] + x[:, 128:]
    return y + pltpu.roll(y, HD, 1)


def _attn_bwd(q, k4, v4, dcat, dps, sink, *, T, name, carry=None):
    R = q.shape[0]
    nb, nbl = R // BLK, T // BLK

    def body(q_ref, dcat_ref, sink_ref, k4_hbm, v4_hbm, dps_hbm, du_ref, dq_ref, dk_ref, dv_ref, dsk_ref,
             k4_v, v4_v, dps_v, sem):
        j = pl.program_id(0)

        @pl.when(j == 0)
        def _():
            _load_weights([(k4_hbm, k4_v), (v4_hbm, v4_v), (dps_hbm, dps_v)], sem)
            dk_ref[...] = jnp.zeros_like(dk_ref)
            dv_ref[...] = jnp.zeros_like(dv_ref)
            dsk_ref[...] = jnp.zeros_like(dsk_ref)

        start = _win_start(j, R)
        d3_hi, d3_lo = _hi_lo(dps_v[pl.ds(start, 3 * BLK), :])
        db = dps_v[pl.ds(pl.multiple_of(j * BLK, BLK), BLK), :]
        pos = j * BLK + lax.broadcasted_iota(jnp.int32, (BLK, 1), 0)
        t_r = start + lax.broadcasted_iota(jnp.int32, (1, 3 * BLK), 1)
        for g, w in enumerate(POOL_WINDOWS):
            sl = slice(g * 128, (g + 1) * 128)
            lo_r, hi_r = _pool_bounds(t_r, w, T, R)
            band_t = jnp.where(pos >= lo_r, jnp.where(pos < hi_r, 1.0, 0.0), 0.0).astype(BF16)
            lo_c, hi_c = _pool_bounds(pos, w, T, R)
            du_ref[:, sl] = _dot(band_t, d3_hi[:, sl]) + _dot(band_t, d3_lo[:, sl]) - db[:, sl] * (hi_c - lo_c).astype(F32)

        start_l = _win_start(j, T)
        valid = _window_mask(j, start_l, nbl)
        rb = lax.broadcasted_iota(jnp.int32, (4 * BLK, 1), 0) // BLK
        lane = lax.broadcasted_iota(jnp.int32, (1, 128), 1)
        dk_l, dk_c, dv_l, dv_c = [], [], [], []
        for g in range(2):
            gl = slice(g * 256, (g + 1) * 256)
            qs = _stack_heads(q_ref[:, gl])
            kl, kc = k4_v[pl.ds(start_l, 3 * BLK), gl], k4_v[T:R, gl]
            vl, vc = v4_v[pl.ds(start_l, 3 * BLK), gl], v4_v[T:R, gl]
            e_l, e_c, e_s, inv = _attn_exps(qs, kl, kc, sink_ref, g, valid)
            p_l, p_c, p_s = e_l * inv, e_c * inv, e_s * inv
            dos = _stack_heads(dcat_ref[:, PW + g * 256 : PW + (g + 1) * 256]).astype(BF16)
            dp_l, dp_c = _dot_nt(dos, vl), _dot_nt(dos, vc)
            delta = jnp.sum(p_l * dp_l, axis=1, keepdims=True) + jnp.sum(p_c * dp_c, axis=1, keepdims=True)
            ds_l = (p_l * (dp_l - delta)).astype(BF16)
            ds_c = (p_c * (dp_c - delta)).astype(BF16)
            dq_ref[:, gl] = _unstack_heads(_dot(ds_l, kl) + _dot(ds_c, kc)) * (HD ** -0.5)
            dk_l.append(_fold_heads(_dot_tn(ds_l, qs)))
            dk_c.append(_fold_heads(_dot_tn(ds_c, qs)))
            dv_l.append(_fold_heads(_dot_tn(p_l.astype(BF16), dos)))
            dv_c.append(_fold_heads(_dot_tn(p_c.astype(BF16), dos)))
            dsink = -p_s * delta
            for h in range(4):
                tot = jnp.sum(jnp.where(rb == h, dsink, 0.0), axis=0, keepdims=True)
                dsk_ref[4 * g + h : 4 * g + h + 1, :] += jnp.broadcast_to(tot, (1, 128))
        first = lane < HD
        dk_ref[pl.ds(start_l, 3 * BLK), :] += jnp.where(first, dk_l[0], dk_l[1])
        dk_ref[T:R, :] += jnp.where(first, dk_c[0], dk_c[1])
        dv_ref[pl.ds(start_l, 3 * BLK), :] += jnp.where(first, dv_l[0], dv_l[1])
        dv_ref[T:R, :] += jnp.where(first, dv_c[0], dv_c[1])

    return _call(
        body,
        name=name,
        grid=(nb,),
        in_specs=[_rows(BLK, AW), _rows(BLK, D), SMEM, ANY, ANY, ANY],
        out_specs=[_rows(BLK, PW), _rows(BLK, AW), _full((R, KVW)), _full((R, KVW)), _full((8, 128))],
        out_shape=[_sds((R, PW), F32), _sds((R, AW), F32), _sds((R, KVW), F32), _sds((R, KVW), F32),
                   _sds((8, 128), F32)],
        scratch_shapes=[pltpu.VMEM((R, AW), BF16), pltpu.VMEM((R, AW), BF16), pltpu.VMEM((R, PW), F32),
                        pltpu.SemaphoreType.DMA((3,))],
        args=(q, dcat, sink, k4, v4, dps),
        carry=carry,
    )


def _mixproj_bwd(h, dho, du, dq, dk, dv, modv, gvec, win, cos, sin, *, T, name):
    R = h.shape[0]
    n_lat, n_tiles = T // TM, R // TM

    def body(h_ref, dho_ref, du_ref, dq_ref, dk_ref, dv_ref, mod_ref, g_ref, win_ref, cos_ref, sin_ref,
             dh_ref, dproj_ref, n_ref, part_ref):
        i = pl.program_id(0)
        first = jnp.logical_or(i == 0, i == n_lat)
        shift, scale = mod_ref[0, 3:4, :], mod_ref[0, 4:5, :]
        g = g_ref[1:2, :]
        r, xhat, y, n = _norm_mod(h_ref[...], g, shift, scale)
        n_ref[...] = n.astype(BF16)
        cs, sn = cos_ref[...], sin_ref[...]
        dproj_ref[:, :PW] = du_ref[...].astype(BF16)
        for s in range(AW // 128):
            x = dq_ref[:, 128 * s : 128 * (s + 1)]
            dproj_ref[:, PW + 128 * s : PW + 128 * (s + 1)] = (x * cs - _rot_half(x) * sn).astype(BF16)
        x = dk_ref[...]
        dproj_ref[:, PW + AW : PW + AW + KVW] = (x * cs - _rot_half(x) * sn).astype(BF16)
        dproj_ref[:, PW + AW + KVW :] = dv_ref[...].astype(BF16)
        dn = _dot(dproj_ref[...], win_ref[...])
        dh, dshift, dscale, dg = _norm_mod_bwd(dn, r, xhat, y, g, scale)
        dh_ref[...] = dho_ref[...] + dh
        _acc_partials(part_ref, first, {0: dshift, 1: dscale, 3: dg})

    return pl.pallas_call(
        body,
        name=name,
        grid=(n_tiles,),
        in_specs=[_rows(TM, D), _rows(TM, D), _rows(TM, PW), _rows(TM, AW), _rows(TM, KVW), _rows(TM, KVW),
                  _mod_spec(n_lat), _full((8, D)), _full((PROJ, D)), _rows(TM, 128), _rows(TM, 128)],
        out_specs=[_rows(TM, D), _rows(TM, PROJ), _rows(TM, D), _part_spec(n_lat)],
        out_shape=[_sds((R, D), F32), _sds((R, PROJ), BF16), _sds((R, D), BF16), _sds((2, 8, D), F32)],
        compiler_params=_params(),
    )(h, dho, du, dq, dk, dv, modv, gvec, win, cos, sin)


def _loss_head(h, target, g_final, *, T, name):
    R = h.shape[0]
    n_lat, n_tiles = T // TM, R // TM

    def body(h_ref, t_ref, g_ref, dh_ref, loss_ref, dg_ref):
        i = pl.program_id(0)

        @pl.when(i == 0)
        def _():
            loss_ref[...] = jnp.zeros_like(loss_ref)
            dg_ref[...] = jnp.zeros_like(dg_ref)

        @pl.when(i < n_lat)
        def _():
            h = h_ref[...]
            g = g_ref[...]
            r = lax.rsqrt(jnp.mean(h * h, axis=-1, keepdims=True) + EPS)
            xhat = h * r
            err = xhat * g - t_ref[...]
            tot = jnp.sum(jnp.sum(err * err, axis=1, keepdims=True), axis=0, keepdims=True)
            loss_ref[...] += jnp.broadcast_to(tot * (0.5 / D), loss_ref.shape)
            dy = err * (1.0 / D)
            dg_ref[0:1, :] += jnp.sum(dy * xhat, axis=0, keepdims=True)
            dxh = dy * g
            dh_ref[...] = r * (dxh - xhat * jnp.mean(dxh * xhat, axis=-1, keepdims=True))

        @pl.when(i >= n_lat)
        def _():
            dh_ref[...] = jnp.zeros_like(dh_ref)

    return pl.pallas_call(
        body,
        name=name,
        grid=(n_tiles,),
        in_specs=[_rows(TM, D), pl.BlockSpec((TM, D), lambda i: (jnp.minimum(i, n_lat - 1), 0)), _full((1, D))],
        out_specs=[_rows(TM, D), _full((8, 128)), _full((8, D))],
        out_shape=[_sds((R, D), F32), _sds((8, 128), F32), _sds((8, D), F32)],
        compiler_params=_params(),
    )(h, target, g_final)


def _mod_fwd(c16, w_mod, b_cols, *, name):
    nl, _, cols = w_mod.shape

    def body(c_ref, w_ref, b_ref, o_ref):
        c = c_ref[...]
        sc = (c * _sigmoid(c)).astype(BF16)
        o_ref[0] = _dot(sc, w_ref[0].astype(BF16)) + b_ref[0]

    return pl.pallas_call(
        body,
        name=name,
        grid=(nl,),
        in_specs=[_full((16, D)), pl.BlockSpec((1, D, cols), lambda l: (l, 0, 0)),
                  pl.BlockSpec((1, 1, cols), lambda l: (l, 0, 0))],
        out_specs=pl.BlockSpec((1, 16, cols), lambda l: (l, 0, 0)),
        out_shape=_sds((nl, 16, cols), F32),
        compiler_params=_params(),
    )(c16, w_mod, b_cols)


def _mod_bwd(c16, dm_cols, w_mod, *, name):
    nl, _, cols = w_mod.shape

    def body(c_ref, dm_ref, w_ref, gw_ref, dc_ref):
        c = c_ref[...]
        sc = (c * _sigmoid(c)).astype(BF16)
        dm = dm_ref[0].astype(BF16)
        gw_ref[0] = _dot_tn(sc, dm)
        dc_ref[0] = _dot_nt(dm, w_ref[0].astype(BF16))

    return pl.pallas_call(
        body,
        name=name,
        grid=(nl,),
        in_specs=[_full((16, D)), pl.BlockSpec((1, 16, cols), lambda l: (l, 0, 0)),
                  pl.BlockSpec((1, D, cols), lambda l: (l, 0, 0))],
        out_specs=[pl.BlockSpec((1, D, cols), lambda l: (l, 0, 0)), pl.BlockSpec((1, 16, D), lambda l: (l, 0, 0))],
        out_shape=[_sds((nl, D, cols), F32), _sds((nl, 16, D), F32)],
        compiler_params=_params(),
    )(c16, dm_cols, w_mod)


def _coords():
    return lax.axis_index("x"), lax.axis_index("y"), lax.axis_index("c")


FWD = 8


def _peer(k, x, y, c):
    if k == FWD:
        return (x ^ (1 - c), y ^ c, c)
    return (1 - x if k & 4 else x, 1 - y if k & 2 else y, 1 - c if k & 1 else c)


def _lin(p):
    return 4 * p[0] + 2 * p[1] + p[2]


def _view(ref, slot):
    return ref if slot is None else ref.at[slot]


class _Round:
    def __init__(self, ins, out_shapes, plan, local_plan=(), n_alias=0):
        self.ins, self.out_shapes = list(ins), list(out_shapes)
        self.plan, self.local_plan, self.n_alias = list(plan), list(local_plan), n_alias
        fed = {p[3] for p in self.plan if p[0] == FWD}
        self.feeders = [n for n, p in enumerate(self.plan) if p[0] in (2, 4, 6) and p[3] in fed]

    def sems(self):
        return [pltpu.SemaphoreType.DMA((len(self.plan),)), pltpu.SemaphoreType.DMA((len(self.plan),)),
                pltpu.SemaphoreType.DMA((max(len(self.local_plan), 1),))]

    def _remote(self, in_refs, out_refs, sems, incoming, pick):
        in_refs = list(out_refs[: self.n_alias]) + list(in_refs[self.n_alias :])
        x, y, c = _coords()
        me = _lin((x, y, c))
        copies = {}
        for idx, (k, ii, sfn, oi, dfn) in enumerate(self.plan):
            if not pick(idx, "d2d" if k == 1 else "fwd" if k == FWD else "ici"):
                continue
            peer = _peer(k, x, y, c)
            sender, receiver = (_lin(peer), me) if incoming else (me, _lin(peer))
            src = out_refs[oi] if ii is None else in_refs[ii]
            copies[idx] = pltpu.make_async_remote_copy(
                src_ref=_view(src, sfn(sender, receiver)), dst_ref=_view(out_refs[oi], dfn(sender, receiver)),
                send_sem=sems[0].at[idx], recv_sem=sems[1].at[idx], device_id=peer, device_id_type=MESH)
        return copies

    def _local(self, in_refs, out_refs, sems):
        in_refs = list(out_refs[: self.n_alias]) + list(in_refs[self.n_alias :])
        me = _lin(_coords())
        return [pltpu.make_async_copy(_view(in_refs[ii], sfn(me)), _view(out_refs[oi], dfn(me)), sems[2].at[idx])
                for idx, (ii, sfn, oi, dfn) in enumerate(self.local_plan)]

    def start(self, in_refs, out_refs, sems, links=("ici", "d2d")):
        for cp in self._remote(in_refs, out_refs, sems, False, lambda n, link: link in links).values():
            cp.start()
        if "ici" in links:
            for cp in self._local(in_refs, out_refs, sems):
                cp.start()

    def mid(self, in_refs, out_refs, sems):
        if self.feeders:
            for cp in self._remote(in_refs, out_refs, sems, True, lambda n, link: n in self.feeders).values():
                cp.wait_recv()
            for cp in self._remote(in_refs, out_refs, sems, False, lambda n, link: link == "fwd").values():
                cp.start()

    def finish(self, in_refs, out_refs, sems):
        for cp in self._remote(in_refs, out_refs, sems, True, lambda n, link: n not in self.feeders).values():
            cp.wait_recv()
        for cp in self._remote(in_refs, out_refs, sems, False, lambda n, link: True).values():
            cp.wait_send()
        for cp in self._local(in_refs, out_refs, sems):
            cp.wait()


def _exchange(name, rnd):
    n_in, n_out = len(rnd.ins), len(rnd.out_shapes)

    def body(*refs):
        in_refs, out_refs, sems = refs[:n_in], refs[n_in : n_in + n_out], refs[n_in + n_out :]
        rnd.start(in_refs, out_refs, sems)
        rnd.mid(in_refs, out_refs, sems)
        rnd.finish(in_refs, out_refs, sems)

    return pl.pallas_call(
        body, name=name, in_specs=[ANY] * n_in, out_specs=[ANY] * n_out, out_shape=rnd.out_shapes,
        scratch_shapes=rnd.sems(), input_output_aliases={i: i for i in range(rnd.n_alias)})(*rnd.ins)


def _call(body, *, name, grid, in_specs, out_specs, out_shape, scratch_shapes, args, carry=None):
    params = _params(len(grid))
    if carry is None:
        outs = pl.pallas_call(body, name=name, grid=grid, in_specs=in_specs, out_specs=out_specs, out_shape=out_shape,
                              scratch_shapes=scratch_shapes, compiler_params=params)(*args)
        return list(outs), []
    n_ci, n_co, n_cs = len(in_specs), len(out_shape), len(scratch_shapes)
    n_xi, n_xo = len(carry.ins), len(carry.out_shapes)

    def wrapped(*refs):
        ci, xi = refs[:n_ci], refs[n_ci : n_ci + n_xi]
        o0 = n_ci + n_xi
        co, xo = refs[o0 : o0 + n_co], refs[o0 + n_co : o0 + n_co + n_xo]
        s0 = o0 + n_co + n_xo
        cs, sems = refs[s0 : s0 + n_cs], refs[s0 + n_cs :]
        ids = [pl.program_id(a) for a in range(len(grid))]
        first = functools.reduce(jnp.logical_and, [i == 0 for i in ids])
        last = functools.reduce(jnp.logical_and, [i == g - 1 for i, g in zip(ids, grid)])

        @pl.when(first)
        def _():
            carry.start(xi, xo, sems, links=("ici",))

        if carry.feeders:
            step = functools.reduce(lambda acc, ig: acc * ig[1] + ig[0], zip(ids, grid), 0)
            n_steps = functools.reduce(lambda a, b: a * b, grid)

            @pl.when(step == min(n_steps - 1, (3 * n_steps) // 5))
            def _():
                carry.mid(xi, xo, sems)

        body(*ci, *co, *cs)

        @pl.when(first)
        def _():
            carry.start(xi, xo, sems, links=("d2d",))

        @pl.when(last)
        def _():
            carry.finish(xi, xo, sems)

    outs = pl.pallas_call(
        wrapped, name=name, grid=grid, in_specs=list(in_specs) + [ANY] * n_xi, out_specs=list(out_specs) + [ANY] * n_xo,
        out_shape=list(out_shape) + carry.out_shapes, scratch_shapes=list(scratch_shapes) + carry.sems(),
        input_output_aliases={n_ci + i: n_co + i for i in range(carry.n_alias)}, compiler_params=params,
    )(*args, *carry.ins)
    return list(outs[:n_co]), list(outs[n_co:])


def _gather_direct(arrays):
    na = len(arrays)
    outs = [_sds((NDEV,) + a.shape, a.dtype) for a in arrays]
    plan = [(k, i, lambda s, r: None, i, lambda s, r: s) for i in range(na) for k in range(1, NDEV)]
    return _Round(arrays, outs, plan, [(i, lambda m: None, i, lambda m: m) for i in range(na)])


def _gather_a(arrays):
    na = len(arrays)
    outs = [_sds((NDEV,) + a.shape, a.dtype) for a in arrays]
    plan = [(k, i, lambda s, r: None, i, lambda s, r: s) for i in range(na) for k in (2, 4)]
    handed = lambda s, r: s ^ (2 << (s & 1))
    plan += [(FWD, None, handed, i, handed) for i in range(na)]
    return _Round(arrays, outs, plan, [(i, lambda m: None, i, lambda m: m) for i in range(na)])


def _gather_b(got):
    na = len(got)
    plan = [(1, i, (lambda s, r, k=k: s ^ k), i, (lambda s, r, k=k: s ^ k)) for i in range(na) for k in (0, 2, 4, 6)]
    return _Round(got, [_sds(g.shape, g.dtype) for g in got], plan, n_alias=na)


def _scatter_1(grads):
    plan = [(1, i, (lambda s, r, q=q: 2 * q + (r & 1)), i, (lambda s, r, q=q: q))
            for i in range(len(grads)) for q in range(4)]
    return _Round(grads, [_sds((4,) + g.shape[1:], g.dtype) for g in grads], plan)


def _scatter_2(chip):
    plan = [(k, i, lambda s, r: r >> 1, i, (lambda s, r, j=j: j)) for i in range(len(chip)) for j, k in enumerate((2, 4, 6))]
    return _Round(chip, [_sds((3,) + g.shape[1:], g.dtype) for g in chip], plan)


def _add_pairs(g, got, pos, *, name):
    _, sh, w = g.shape

    def body(pos_ref, g_ref, r_ref, o_ref):
        o_ref[...] = (g_ref[...].astype(F32) + r_ref[...].astype(F32)).astype(o_ref.dtype)

    return pl.pallas_call(
        body,
        name=name,
        grid_spec=pltpu.PrefetchScalarGridSpec(
            num_scalar_prefetch=1, grid=(4,),
            in_specs=[pl.BlockSpec((1, sh, w), lambda q, p: (2 * q + p[0], 0, 0)),
                      pl.BlockSpec((1, sh, w), lambda q, p: (q, 0, 0))],
            out_specs=pl.BlockSpec((1, sh, w), lambda q, p: (q, 0, 0))),
        out_shape=_sds((4, sh, w), g.dtype),
        compiler_params=_params(),
    )(pos, g, got)


def _sum_adamw(chip, got, pos, w, m, v, layer, prior, *, name):
    _, sh, wd = chip.shape
    nl, rows, cols = w.shape
    nb, blk = 2, (sh // 2, wd)
    part = lambda n: pl.BlockSpec((n, sh // 2, wd), lambda i, p: ((p[1] if n == 1 else 0), i, 0))
    mine = pl.BlockSpec(blk, lambda i, p: (layer * nb + i, 0))
    flat = lambda t: t.reshape(nl * rows, cols)
    n_prior = 0 if prior is None else 4

    def body(pos_ref, c_ref, r_ref, w_ref, m_ref, v_ref, *refs):
        g_ref, d_ref, m2_ref, v2_ref = refs[n_prior:]
        g = c_ref[0].astype(F32)
        for s in range(3):
            g = g + r_ref[s].astype(F32)
        g_ref[...] = g
        d_ref[...], m2_ref[...], v2_ref[...] = _adamw_math(w_ref[...], g, m_ref[...], v_ref[...])

    outs = pl.pallas_call(
        body,
        name=name,
        grid_spec=pltpu.PrefetchScalarGridSpec(
            num_scalar_prefetch=1, grid=(nb,),
            in_specs=[part(1), part(3), mine, mine, mine] + [ANY] * n_prior,
            out_specs=[mine] * 4),
        out_shape=[_sds((nl * rows, cols), F32)] * 4,
        input_output_aliases={6 + k: k for k in range(n_prior)},
        compiler_params=_params(),
    )(pos, chip, got, flat(w), flat(m), flat(v), *(flat(t) for t in prior or ()))
    return [o.reshape(w.shape) for o in outs]


def _adamw_math(w, g, m, v):
    m2 = ADAM_B1 * m + (1.0 - ADAM_B1) * g
    v2 = ADAM_B2 * v + (1.0 - ADAM_B2) * (g * g)
    m_hat = m2 / (1.0 - ADAM_B1 ** ADAM_STEP)
    v_hat = v2 / (1.0 - ADAM_B2 ** ADAM_STEP)
    delta = -ADAM_LR * (m_hat / (jnp.sqrt(v_hat) + ADAM_EPS) + ADAM_WD * w)
    return delta, m2, v2


def _adamw(w, g, m, v, *, name, carry=None):
    shape = w.shape
    flat = [t.reshape(-1, shape[-1]) for t in (w, g, m, v)]
    rows, cols = flat[0].shape
    tr = rows // 8 if rows % 64 == 0 else rows
    spec = _rows(tr, cols)

    def body(w_ref, g_ref, m_ref, v_ref, d_ref, m2_ref, v2_ref):
        d_ref[...], m2_ref[...], v2_ref[...] = _adamw_math(w_ref[...], g_ref[...], m_ref[...], v_ref[...])

    outs, got = _call(body, name=name, grid=(rows // tr,), in_specs=[spec] * 4, out_specs=[spec] * 3,
                      out_shape=[_sds((rows, cols), F32)] * 3, scratch_shapes=[], args=flat, carry=carry)
    return tuple(o.reshape(shape) for o in outs), got


def _adds(tag, grads, got, *, pos):
    return [_add_pairs(g, r, pos, name=f"rs_add_{tag}_{i}") for i, (g, r) in enumerate(zip(grads, got))]


def _small_sums(packets, nf, dwp, dsc, dsk, *, name):
    flat = [p for layer in packets for p in layer]

    def total(ref, *idx):
        acc = ref[(0,) + idx]
        for dev in range(1, NDEV):
            acc = acc + ref[(dev,) + idx]
        return acc

    def body(*refs):
        pk = refs[:6]
        nf_ref, dwp0, dwp1, dsc0, dsc1, dsk0, dsk1 = refs[6:13]
        dm_ref, gb_ref, gn_ref, gnf_ref, gwp_ref, gps_ref, gsk_ref = refs[13:]
        dm_ref[...] = jnp.zeros_like(dm_ref)
        gn_ref[...] = jnp.zeros_like(gn_ref)
        for l in range(2):
            for sb in range(3):
                p = pk[3 * l + sb]
                for r in range(3):
                    col = slice((3 * sb + r) * D, (3 * sb + r + 1) * D)
                    lat = p[0, 0, r : r + 1, :]
                    dm_ref[l, 0:1, col] = lat
                    for dev in range(1, NDEV):
                        row = p[dev, 0, r : r + 1, :]
                        dm_ref[l, dev : dev + 1, col] = row
                        lat = lat + row
                    ctx = total(p, 1, slice(r, r + 1), slice(None))
                    dm_ref[l, 8:9, col] = ctx
                    gb_ref[l : l + 1, col] = lat + ctx
                gn_ref[l, sb : sb + 1, :] = total(p, 0, slice(3, 4), slice(None)) + total(p, 1, slice(3, 4), slice(None))
        gnf_ref[...] = total(nf_ref, slice(0, 1), slice(None))
        for l, (a, b, c) in enumerate(((dwp0, dsc0, dsk0), (dwp1, dsc1, dsk1))):
            gwp_ref[l] = total(a, slice(None), slice(None))
            gps_ref[l : l + 1, :] = total(b, slice(0, 1), slice(None))
            gsk_ref[l] = total(c, slice(None), slice(None))

    ins = flat + [nf, dwp[0], dwp[1], dsc[0], dsc[1], dsk[0], dsk[1]]
    return pl.pallas_call(
        body,
        name=name,
        out_shape=[_sds((2, 16, NMOD * D), F32), _sds((2, NMOD * D), F32), _sds((2, 8, D), F32), _sds((1, D), F32),
                   _sds((2, PW, 128), F32), _sds((2, PW), F32), _sds((2, 8, 128), F32)],
        compiler_params=pltpu.CompilerParams(vmem_limit_bytes=VMEM_LIMIT),
    )(*ins)


def _small_adamw(c_ctx, dc_all, triples, *, name):
    n = len(triples)

    def body(*refs):
        c_ref, dc_ref = refs[0], refs[1]
        ins = refs[2 : 2 + 4 * n - 1]
        outs = refs[2 + 4 * n - 1 :]
        acc = dc_ref[0, 0, 8:9, :] + dc_ref[0, 1, 8:9, :]
        for dev in range(1, NDEV):
            acc = acc + (dc_ref[dev, 0, 8:9, :] + dc_ref[dev, 1, 8:9, :])
        c = c_ref[...]
        sig = _sigmoid(c)
        g_c = acc * (sig * (1.0 + c * (1.0 - sig)))
        outs[0][...] = g_c
        pos = 0
        for k in range(n):
            if k == 0:
                w, g, m, v = ins[0][...], g_c, ins[1][...], ins[2][...]
                pos = 3
            else:
                w, g, m, v = (ins[pos + t][...] for t in range(4))
                pos += 4
            d, m2, v2 = _adamw_math(w, g, m, v)
            outs[1 + 3 * k][...], outs[2 + 3 * k][...], outs[3 + 3 * k][...] = d, m2, v2

    flat_in = [c_ctx, dc_all]
    out_shape = [_sds(c_ctx.shape, F32)]
    for k, (w, g, m, v) in enumerate(triples):
        flat_in += [w, m, v] if k == 0 else [w, g, m, v]
        out_shape += [_sds(w.shape, F32)] * 3
    return pl.pallas_call(body, name=name, out_shape=out_shape,
                          compiler_params=pltpu.CompilerParams(vmem_limit_bytes=VMEM_LIMIT))(*flat_in)


def _rope_tables(T, R):
    t = jnp.arange(T)
    inv = ROPE_BASE ** (-jnp.arange(0, HD // 2, 2, dtype=F32) / (HD // 2))
    ang = jnp.concatenate([(t // GRID_W).astype(F32)[:, None] * inv, (t % GRID_W).astype(F32)[:, None] * inv], axis=-1)
    cos = jnp.concatenate([jnp.tile(jnp.cos(ang), (1, 4)), jnp.ones((R - T, 128), F32)], axis=0)
    sin = jnp.concatenate([jnp.tile(jnp.sin(ang), (1, 4)), jnp.zeros((R - T, 128), F32)], axis=0)
    return cos, sin


def kernel(x, c, ctx, c_ctx, w_mod, b_mod, norm_ffn1, w_ffn1_in, w_ffn1_out, norm_mix, w_in, w_pool, pool_scale, sink, w_out, norm_ffn2, w_ffn2_in, w_ffn2_out, norm_final, loss_target, m_c_ctx, m_w_mod, m_b_mod, m_norm_ffn1, m_w_ffn1_in, m_w_ffn1_out, m_norm_mix, m_w_in, m_w_pool, m_pool_scale, m_sink, m_w_out, m_norm_ffn2, m_w_ffn2_in, m_w_ffn2_out, m_norm_final, v_c_ctx, v_w_mod, v_b_mod, v_norm_ffn1, v_w_ffn1_in, v_w_ffn1_out, v_norm_mix, v_w_in, v_w_pool, v_pool_scale, v_sink, v_w_out, v_norm_ffn2, v_w_ffn2_in, v_w_ffn2_out, v_norm_final):
    T = x.shape[1]
    R = T + LC
    nl = w_mod.shape[0]
    cx, cy, cc = _coords()
    me = _lin((cx, cy, cc))
    pos = jnp.stack([cc, 2 * cx + cy]).astype(jnp.int32)
    mcols = w_mod.shape[2]

    shards = [([w_ffn1_in[l].T.astype(BF16), w_ffn1_out[l].astype(BF16)],
               [w_in[l].T.astype(BF16), w_out[l].astype(BF16)],
               [w_ffn2_in[l].T.astype(BF16), w_ffn2_out[l].astype(BF16)]) for l in range(nl)]

    got = _exchange("ag_c_w", _merge(_gather_direct([c]), _gather_a(shards[0][0] + shards[0][1])))
    c_all, w_first = got[0], got[1:]
    c16 = jnp.concatenate([c_all.reshape(NDEV, D), c_ctx[None], jnp.zeros((16 - NDEV - 1, D), F32)], axis=0)
    b_cols = lax.dynamic_slice(b_mod, (0, me * mcols), (nl, mcols)).reshape(nl, 1, mcols)
    got = _exchange("ag_mod_w", _merge(_gather_b(w_first), _gather_direct([_mod_fwd(c16, w_mod, b_cols, name="mod_fwd")])))
    w_first, mod_all = got[:4], got[4]
    mod_all = jnp.transpose(mod_all, (1, 2, 0, 3)).reshape(nl, 16, NMOD, D)
    mine = lax.dynamic_index_in_dim(mod_all, me, axis=1, keepdims=False)
    pad = jnp.zeros((nl, 16 - NMOD, D), F32)
    modv = jnp.stack([jnp.concatenate([mine, pad], axis=1), jnp.concatenate([mod_all[:, 8], pad], axis=1)], axis=1)

    gvec = [jnp.concatenate([norm_ffn1[l][None], norm_mix[l][None], norm_ffn2[l][None], jnp.zeros((5, D), F32)], axis=0)
            for l in range(nl)]
    cos, sin = _rope_tables(T, R)
    ps2 = [pool_scale[l][None] for l in range(nl)]

    h = jnp.concatenate([x[0], ctx[0]], axis=0)
    loss_all, dh, small, nf_all, big, last_partial = _forward_backward(
        h, loss_target[0], modv, gvec, shards, w_first, cos, sin, sink, w_pool, ps2, norm_final, pos, T=T)
    loss = jnp.sum(loss_all[:, 0, 0])
    grad_x = dh[:T][None]

    dm, g_b_mod, g_norms, g_nf, g_wp, g_ps, g_sk = _small_sums(
        [small[l][0:3] for l in range(nl)], nf_all, *[[small[l][k] for l in range(nl)] for k in (3, 4, 5)],
        name="small_sums")
    dm_cols = lax.dynamic_slice(dm, (0, 0, me * mcols), (nl, 16, mcols))
    g_w_mod, dc_part = _mod_bwd(c16, dm_cols, w_mod, name="mod_bwd")
    got = _exchange("rs1_tail", _merge(_scatter_1([last_partial]), _gather_direct([dc_part])))
    (c1o,), dc_all = _adds("ffn1_out_0", [last_partial], got[:1], pos=pos), got[1]

    delta, new_m, new_v = {}, {}, {}
    (delta["w_mod"], new_m["w_mod"], new_v["w_mod"]), got = _adamw(
        w_mod, g_w_mod, m_w_mod, v_w_mod, name="adamw_w_mod", carry=_scatter_2([c1o]))
    big[0][1] = (c1o, got[0])

    grads = {
        "b_mod": g_b_mod, "norm_ffn1": g_norms[:, 0], "norm_mix": g_norms[:, 1], "norm_ffn2": g_norms[:, 2],
        "w_pool": g_wp.reshape(w_pool.shape), "pool_scale": g_ps, "sink": g_sk[:, :, 0], "norm_final": g_nf.reshape(D),
        "w_mod": g_w_mod,
    }
    weights = dict(c_ctx=c_ctx, w_mod=w_mod, b_mod=b_mod, norm_ffn1=norm_ffn1, w_ffn1_in=w_ffn1_in, w_ffn1_out=w_ffn1_out,
                   norm_mix=norm_mix, w_in=w_in, w_pool=w_pool, pool_scale=pool_scale, sink=sink, w_out=w_out,
                   norm_ffn2=norm_ffn2, w_ffn2_in=w_ffn2_in, w_ffn2_out=w_ffn2_out, norm_final=norm_final)
    moms = dict(c_ctx=(m_c_ctx, v_c_ctx), w_mod=(m_w_mod, v_w_mod), b_mod=(m_b_mod, v_b_mod),
                norm_ffn1=(m_norm_ffn1, v_norm_ffn1), w_ffn1_in=(m_w_ffn1_in, v_w_ffn1_in),
                w_ffn1_out=(m_w_ffn1_out, v_w_ffn1_out), norm_mix=(m_norm_mix, v_norm_mix), w_in=(m_w_in, v_w_in),
                w_pool=(m_w_pool, v_w_pool), pool_scale=(m_pool_scale, v_pool_scale), sink=(m_sink, v_sink),
                w_out=(m_w_out, v_w_out), norm_ffn2=(m_norm_ffn2, v_norm_ffn2), w_ffn2_in=(m_w_ffn2_in, v_w_ffn2_in),
                w_ffn2_out=(m_w_ffn2_out, v_w_ffn2_out), norm_final=(m_norm_final, v_norm_final))
    order = list(weights)
    small_names = ["c_ctx", "b_mod", "norm_ffn1", "norm_mix", "w_pool", "pool_scale", "sink", "norm_ffn2", "norm_final"]

    def as2d(name, t):
        if name == "w_pool":
            return t.reshape(-1, 128)
        return t.reshape(1, -1) if t.ndim == 1 else t

    triples = [(as2d(n, weights[n]), None if n == "c_ctx" else as2d(n, grads[n]), as2d(n, moms[n][0]), as2d(n, moms[n][1]))
               for n in small_names]
    outs = _small_adamw(as2d("c_ctx", c_ctx), dc_all, triples, name="small_adamw")
    grads["c_ctx"] = outs[0].reshape(c_ctx.shape)
    for k, n in enumerate(small_names):
        delta[n], new_m[n], new_v[n] = (o.reshape(weights[n].shape) for o in outs[1 + 3 * k : 4 + 3 * k])
    for k, n in enumerate(["w_ffn1_in", "w_ffn1_out", "w_in", "w_out", "w_ffn2_in", "w_ffn2_out"]):
        turn = (lambda t: jnp.swapaxes(t, 1, 2)) if k % 2 == 0 else (lambda t: t)
        wmv = [turn(t) for t in (weights[n], *moms[n])]
        outs = None
        for l in reversed(range(nl)):
            outs = _sum_adamw(*big[l][k], pos, *wmv, l, outs, name=f"adamw_{n}_{l}")
        grads[n], delta[n], new_m[n], new_v[n] = (turn(o) for o in outs)

    return (loss, grad_x, *[grads[n] for n in order], *[delta[n] for n in order],
            *[new_m[n] for n in order], *[new_v[n] for n in order])


def _merge(*rounds):
    ins, outs, plan, local, n_alias = [], [], [], [], 0
    for r in rounds:
        assert r.n_alias == 0 or (not ins and r.n_alias == len(r.ins) == len(r.out_shapes))
        oi, oo = len(ins), len(outs)
        plan += [(k, None if i is None else i + oi, sf, o + oo, df) for k, i, sf, o, df in r.plan]
        local += [(i + oi, sf, o + oo, df) for i, sf, o, df in r.local_plan]
        ins += r.ins
        outs += r.out_shapes
        n_alias += r.n_alias
    return _Round(ins, outs, plan, local, n_alias)


def _forward_backward(h, target, modv, gvec, shards, w_first, cos, sin, sink, w_pool, ps2, norm_final, pos, *, T):
    nl = len(gvec)
    flat = lambda ws: [w.reshape(-1, D) for w in ws]
    saved = []
    w1, wm = flat(w_first[:2]), flat(w_first[2:])
    for l in range(nl):
        last = l == nl - 1
        h0 = h
        if l == 0:
            (h1, a1, b1, f1), got = _ffn_fwd(h0, modv[l], gvec[l], *w1, T=T, mrow=0, grow=0, ctx_active=True,
                                             name=f"ffn1_fwd_{l}", carry=_gather_a(shards[l][2]))
            (u, q, k4, v4), got = _mixproj_fwd(h1, modv[l], gvec[l], wm[0], cos, sin, T=T, name=f"mixproj_fwd_{l}",
                                               carry=_gather_b(got))
            w2 = flat(got)
        else:
            (h1, a1, b1, f1), got = _ffn_fwd(h0, modv[l], gvec[l], *w1, T=T, mrow=0, grow=0, ctx_active=True,
                                             name=f"ffn1_fwd_{l}", carry=_gather_b(nxt_m + nxt_2))
            wm, w2 = flat(got[:2]), flat(got[2:])
            (u, q, k4, v4), _ = _mixproj_fwd(h1, modv[l], gvec[l], wm[0], cos, sin, T=T, name=f"mixproj_fwd_{l}")
        (cat,), nxt_1 = _attnpool_fwd(u, q, k4, v4, sink[l], w_pool[l], ps2[l], T=T, name=f"attnpool_fwd_{l}",
                                      carry=None if last else _gather_a(shards[l + 1][0]))
        (h2, mo), nxt_m = _mixout_fwd(h1, cat, modv[l], wm[1], T=T, ctx_active=not last, name=f"mixout_fwd_{l}",
                                      carry=None if last else _gather_a(shards[l + 1][1]))
        (h3, a2, b2, f2), got = _ffn_fwd(h2, modv[l], gvec[l], *w2, T=T, mrow=6, grow=2, ctx_active=not last,
                                         name=f"ffn2_fwd_{l}",
                                         carry=None if last else _merge(_gather_b(nxt_1), _gather_a(shards[l + 1][2])))
        saved.append((h0, a1, b1, f1, h1, u, q, k4, v4, cat, mo, h2, a2, b2, f2, w1, wm, w2))
        h = h3
        if not last:
            w1, nxt_2 = flat(got[:2]), got[2:]

    dh, loss_part, dnf = _loss_head(h, target, norm_final[None], T=T, name="loss_head")

    adds = functools.partial(_adds, pos=pos)
    small, big = [None] * nl, {}
    prev = None
    for l in reversed(range(nl)):
        last = l == nl - 1
        h0, a1, b1, f1, h1, u, q, k4, v4, cat, mo, h2, a2, b2, f2, w1, wm, w2 = saved[l]
        (dh, dab, s, n, df, pk2), got = _ffn_bwd(
            h2, dh, a2, b2, f2, modv[l], gvec[l], *w2, T=T, mrow=6, grow=2, ctx_active=not last, name=f"ffn2_bwd_{l}",
            carry=_merge(_scatter_1(prev[0]), _gather_a(prev[1])) if prev else None)
        if prev:
            c1, small_a = adds(f"ffn1_{l + 1}", prev[0], got[:2]), got[2:]
        g_w2i, got = _wgrad(dab, n, bk=WG_BK, sh=2 * DFF // NDEV, name=f"wgrad_ffn2_in_{l}",
                            carry=_scatter_2(c1[:1]) if prev else None)
        if prev:
            big[l + 1][0] = (c1[0], got[0])
        g_w2o, got = _wgrad(s, df, bk=WG_BK, sh=DFF // NDEV, name=f"wgrad_ffn2_out_{l}",
                            carry=_scatter_2(c1[1:]) if prev else None)
        if prev:
            big[l + 1][1] = (c1[1], got[0])
        rnd = _scatter_1([g_w2i, g_w2o])
        (dcat, dmix, pko), got = _mixout_bwd(dh, mo, modv[l], wm[1], T=T, ctx_active=not last, name=f"mixout_bwd_{l}",
                                             carry=_merge(_gather_b(small_a), rnd) if prev else rnd)
        if prev:
            small[l + 1], got = got[: len(small_a)], got[len(small_a) :]
        c2 = adds(f"ffn2_{l}", [g_w2i, g_w2o], got)
        g_wo, _ = _wgrad(cat, dmix, bk=D, sh=D // NDEV, name=f"wgrad_out_{l}")
        dps, dwp, dsc = _pool_bwd(u, dcat, w_pool[l], ps2[l], T=T, name=f"pool_bwd_{l}")
        (du, dq, dk, dv, dsk), got = _attn_bwd(q, k4, v4, dcat, dps, sink[l], T=T, name=f"attn_bwd_{l}", carry=_scatter_2(c2))
        big[l] = [None, None, None, None, (c2[0], got[0]), (c2[1], got[1])]
        dh, dproj, n, pkm = _mixproj_bwd(h1, dh, du, dq, dk, dv, modv[l], gvec[l], wm[0], cos, sin, T=T, name=f"mixproj_bwd_{l}")
        g_wi, _ = _wgrad(dproj, n, bk=PROJ, sh=PROJ // NDEV, name=f"wgrad_in_{l}")
        (dh, dab, s, n, df, pk1), got = _ffn_bwd(h0, dh, a1, b1, f1, modv[l], gvec[l], *w1, T=T, mrow=0, grow=0,
                                                 ctx_active=True, name=f"ffn1_bwd_{l}", carry=_scatter_1([g_wi, g_wo]))
        cm = adds(f"mix_{l}", [g_wi, g_wo], got)
        mine = [pk1, pkm + pko, pk2, dwp, dsc, dsk]
        rnd = _merge(_scatter_2(cm), _gather_a(mine + [dnf, loss_part])) if l == 0 else _scatter_2(cm)
        g_w1i, got = _wgrad(dab, n, bk=WG_BK, sh=2 * DFF // NDEV, name=f"wgrad_ffn1_in_{l}", carry=rnd)
        big[l][2:4] = [(cm[0], got[0]), (cm[1], got[1])]
        if l > 0:
            g_w1o, _ = _wgrad(s, df, bk=WG_BK, sh=DFF // NDEV, name=f"wgrad_ffn1_out_{l}")
            prev = ([g_w1i, g_w1o], mine)
    (c1i,) = adds("ffn1_in_0", [g_w1i], _exchange("rs1_ffn1_in_0", _scatter_1([g_w1i])))
    g_w1o, got = _wgrad(s, df, bk=WG_BK, sh=DFF // NDEV, name="wgrad_ffn1_out_0",
                        carry=_merge(_gather_b(got[2:]), _scatter_2([c1i])))
    small[0], nf_all, loss_all = got[:6], got[6], got[7]
    big[0][0] = (c1i, got[8])
    return loss_all, dh, small, nf_all, big, g_w1o
```

```python
import functools

import jax
import jax.numpy as jnp
from jax import lax
from jax.experimental import pallas as pl
from jax.experimental.pallas import tpu as pltpu

F32, BF16 = jnp.float32, jnp.bfloat16

D = 1024
LC = 256
DFF = 2816
NMOD = 9
PW = 512
AW = 512
KVW = 128
PROJ = PW + AW + 2 * KVW
HD = 64
BLK = 128
GRID_W = 64
POOL_WINDOWS = (2, 4, 8, 16)
EPS = 1e-6
NEG = -1e30
ROPE_BASE = 10000.0
NDEV = 8
MESH = pl.DeviceIdType.MESH

ADAM_LR, ADAM_B1, ADAM_B2, ADAM_EPS, ADAM_WD, ADAM_STEP = 0.001, 0.9, 0.999, 1e-08, 0.01, 10

VMEM_LIMIT = 56 * 1024 * 1024
TM = 256
FFN_CHUNKS = ((0, 1536), (1536, 1280))
WG_BK = 1408

ANY = pl.BlockSpec(memory_space=pl.ANY)
SMEM = pl.BlockSpec(memory_space=pltpu.SMEM)


def _params(ngrid=1):
    return pltpu.CompilerParams(dimension_semantics=("arbitrary",) * ngrid, vmem_limit_bytes=VMEM_LIMIT)


def _dot(a, b):
    return jnp.dot(a, b, preferred_element_type=F32)


def _dot_nt(a, b):
    return lax.dot_general(a, b, (((1,), (1,)), ((), ())), preferred_element_type=F32)


def _dot_tn(a, b):
    return lax.dot_general(a, b, (((0,), (0,)), ((), ())), preferred_element_type=F32)


def _sigmoid(x):
    return 1.0 / (1.0 + jnp.exp(-x))


def _rows(tm, w):
    return pl.BlockSpec((tm, w), lambda i: (i, 0))


def _full(shape):
    nd = len(shape)
    return pl.BlockSpec(shape, lambda *_: (0,) * nd)


def _sds(shape, dtype):
    return jax.ShapeDtypeStruct(shape, dtype)


def _norm_mod(h, g, shift, scale):
    r = lax.rsqrt(jnp.mean(h * h, axis=-1, keepdims=True) + EPS)
    xhat = h * r
    y = xhat * g
    return r, xhat, y, y * (1.0 + scale) + shift


def _norm_mod_bwd(dn, r, xhat, y, g, scale):
    dshift = jnp.sum(dn, axis=0, keepdims=True)
    dscale = jnp.sum(dn * y, axis=0, keepdims=True)
    dy = dn * (1.0 + scale)
    dg = jnp.sum(dy * xhat, axis=0, keepdims=True)
    dxh = dy * g
    dh = r * (dxh - xhat * jnp.mean(dxh * xhat, axis=-1, keepdims=True))
    return dh, dshift, dscale, dg


def _acc_partials(part_ref, first, rows):
    @pl.when(first)
    def _():
        part_ref[...] = jnp.zeros_like(part_ref)

    for r, val in rows.items():
        part_ref[0, r : r + 1, :] += val


def _mod_spec(n_lat):
    return pl.BlockSpec((1, 16, D), lambda i: (i // n_lat, 0, 0))


def _part_spec(n_lat):
    return pl.BlockSpec((1, 8, D), lambda i: (i // n_lat, 0, 0))


def _load_weights(pairs, sem):
    copies = [pltpu.make_async_copy(src, dst, sem.at[k]) for k, (src, dst) in enumerate(pairs)]
    for cp in copies:
        cp.start()
    for cp in copies:
        cp.wait()


def _ffn_weight_copies(win_hbm, wout_hbm, win_v, wout_v, sem):
    loads = []
    for k, (c0, cw) in enumerate(FFN_CHUNKS):
        slabs = [(win_hbm, win_v, c0), (win_hbm, win_v, DFF + c0), (wout_hbm, wout_v, c0)]
        loads.append([pltpu.make_async_copy(src.at[pl.ds(r0, cw)], dst.at[pl.ds(r0, cw)], sem.at[3 * k + j])
                      for j, (src, dst, r0) in enumerate(slabs)])
    return loads


def _ffn_steps(i, n_active, loads, compute):
    @pl.when(i == 0)
    def _():
        for cp in sum(loads, []):
            cp.start()
        compute(loads)

    @pl.when(jnp.logical_and(i > 0, i < n_active))
    def _():
        compute(None)


def _wait_chunk(loads, k):
    if loads is not None:
        for cp in loads[k]:
            cp.wait()


def _ffn_fwd(h, modv, gvec, win, wout, *, T, mrow, grow, ctx_active, name, carry=None):
    R = h.shape[0]
    n_lat, n_tiles = T // TM, R // TM
    n_active = n_tiles if ctx_active else n_lat

    def body(h_ref, mod_ref, g_ref, win_hbm, wout_hbm, ho_ref, a_ref, b_ref, f_ref, win_v, wout_v, sem):
        i = pl.program_id(0)

        def compute(loads):
            h = h_ref[...]
            shift, scale, gate = (mod_ref[0, mrow + k : mrow + k + 1, :] for k in range(3))
            _, _, _, n = _norm_mod(h, g_ref[grow : grow + 1, :], shift, scale)
            n_bf = n.astype(BF16)
            acc = jnp.zeros((TM, D), F32)
            for k, (c0, cw) in enumerate(FFN_CHUNKS):
                _wait_chunk(loads, k)
                a = _dot_nt(n_bf, win_v[c0 : c0 + cw, :])
                b = _dot_nt(n_bf, win_v[DFF + c0 : DFF + c0 + cw, :])
                a_ref[:, c0 : c0 + cw] = a.astype(BF16)
                b_ref[:, c0 : c0 + cw] = b.astype(BF16)
                s = a * _sigmoid(a) * b
                acc = acc + _dot(s.astype(BF16), wout_v[c0 : c0 + cw, :])
            f_ref[...] = acc.astype(BF16)
            ho_ref[...] = h + (0.5 * gate) * acc

        _ffn_steps(i, n_active, _ffn_weight_copies(win_hbm, wout_hbm, win_v, wout_v, sem), compute)

        @pl.when(i >= n_active)
        def _():
            ho_ref[...] = h_ref[...]
            a_ref[...] = jnp.zeros_like(a_ref)
            b_ref[...] = jnp.zeros_like(b_ref)
            f_ref[...] = jnp.zeros_like(f_ref)

    return _call(
        body,
        name=name,
        grid=(n_tiles,),
        in_specs=[_rows(TM, D), _mod_spec(n_lat), _full((8, D)), ANY, ANY],
        out_specs=[_rows(TM, D), _rows(TM, DFF), _rows(TM, DFF), _rows(TM, D)],
        out_shape=[_sds((R, D), F32), _sds((R, DFF), BF16), _sds((R, DFF), BF16), _sds((R, D), BF16)],
        scratch_shapes=[pltpu.VMEM((2 * DFF, D), BF16), pltpu.VMEM((DFF, D), BF16),
                        pltpu.SemaphoreType.DMA((3 * len(FFN_CHUNKS),))],
        args=(h, modv, gvec, win, wout),
        carry=carry,
    )


def _ffn_bwd(h, dho, a, b, f, modv, gvec, win, wout, *, T, mrow, grow, ctx_active, name, carry=None):
    R = h.shape[0]
    n_lat, n_tiles = T // TM, R // TM
    n_active = n_tiles if ctx_active else n_lat

    def body(h_ref, dho_ref, a_ref, b_ref, f_ref, mod_ref, g_ref, win_hbm, wout_hbm,
             dh_ref, dab_ref, s_ref, n_ref, df_ref, part_ref, win_v, wout_v, sem):
        i = pl.program_id(0)
        first = jnp.logical_or(i == 0, i == n_lat)

        def compute(loads):
            h = h_ref[...]
            dho = dho_ref[...]
            shift, scale, gate = (mod_ref[0, mrow + k : mrow + k + 1, :] for k in range(3))
            g = g_ref[grow : grow + 1, :]
            r, xhat, y, n = _norm_mod(h, g, shift, scale)
            dgate = 0.5 * jnp.sum(dho * f_ref[...].astype(F32), axis=0, keepdims=True)
            df_bf = ((0.5 * gate) * dho).astype(BF16)
            df_ref[...] = df_bf
            n_ref[...] = n.astype(BF16)
            dn = jnp.zeros((TM, D), F32)
            for k, (c0, cw) in enumerate(FFN_CHUNKS):
                _wait_chunk(loads, k)
                ds = _dot_nt(df_bf, wout_v[c0 : c0 + cw, :])
                av = a_ref[:, c0 : c0 + cw].astype(F32)
                bv = b_ref[:, c0 : c0 + cw].astype(F32)
                sig = _sigmoid(av)
                sa = av * sig
                s_ref[:, c0 : c0 + cw] = (sa * bv).astype(BF16)
                da = (ds * bv * (sig * (1.0 + av * (1.0 - sig)))).astype(BF16)
                db = (ds * sa).astype(BF16)
                dab_ref[:, c0 : c0 + cw] = da
                dab_ref[:, DFF + c0 : DFF + c0 + cw] = db
                dn = dn + _dot(da, win_v[c0 : c0 + cw, :]) + _dot(db, win_v[DFF + c0 : DFF + c0 + cw, :])
            dh, dshift, dscale, dg = _norm_mod_bwd(dn, r, xhat, y, g, scale)
            dh_ref[...] = dho + dh
            _acc_partials(part_ref, first, {0: dshift, 1: dscale, 2: dgate, 3: dg})

        _ffn_steps(i, n_active, _ffn_weight_copies(win_hbm, wout_hbm, win_v, wout_v, sem), compute)

        @pl.when(i >= n_active)
        def _():
            dh_ref[...] = dho_ref[...]
            dab_ref[...] = jnp.zeros_like(dab_ref)
            s_ref[...] = jnp.zeros_like(s_ref)
            n_ref[...] = jnp.zeros_like(n_ref)
            df_ref[...] = jnp.zeros_like(df_ref)
            part_ref[...] = jnp.zeros_like(part_ref)

    return _call(
        body,
        name=name,
        grid=(n_tiles,),
        in_specs=[_rows(TM, D), _rows(TM, D), _rows(TM, DFF), _rows(TM, DFF), _rows(TM, D),
                  _mod_spec(n_lat), _full((8, D)), ANY, ANY],
        out_specs=[_rows(TM, D), _rows(TM, 2 * DFF), _rows(TM, DFF), _rows(TM, D), _rows(TM, D), _part_spec(n_lat)],
        out_shape=[_sds((R, D), F32), _sds((R, 2 * DFF), BF16), _sds((R, DFF), BF16), _sds((R, D), BF16),
                   _sds((R, D), BF16), _sds((2, 8, D), F32)],
        scratch_shapes=[pltpu.VMEM((2 * DFF, D), BF16), pltpu.VMEM((DFF, D), BF16),
                        pltpu.SemaphoreType.DMA((3 * len(FFN_CHUNKS),))],
        args=(h, dho, a, b, f, modv, gvec, win, wout),
        carry=carry,
    )


def _wgrad(x, y, *, bk, sh, name, carry=None):
    R, kx = x.shape
    n = y.shape[1]
    tr = R // 2
    nr, nsh = R // tr, bk // sh

    def body(x_ref, y_ref, o_ref, acc):
        r = pl.program_id(1)

        @pl.when(r == 0)
        def _():
            acc[...] = jnp.zeros_like(acc)

        acc[...] += _dot_tn(x_ref[...], y_ref[...])

        @pl.when(r == nr - 1)
        def _():
            for s in range(nsh):
                o_ref[s] = acc[s * sh : (s + 1) * sh, :].astype(BF16)

    (out,), got = _call(
        body,
        name=name,
        grid=(kx // bk, nr),
        in_specs=[pl.BlockSpec((tr, bk), lambda k, r: (r, k)), pl.BlockSpec((tr, n), lambda k, r: (r, 0))],
        out_specs=[pl.BlockSpec((nsh, sh, n), lambda k, r: (k, 0, 0))],
        out_shape=[_sds((kx // sh, sh, n), BF16)],
        scratch_shapes=[pltpu.VMEM((bk, n), F32)],
        args=(x, y),
        carry=carry,
    )
    return out, got


def _rot_half(x):
    lane = lax.broadcasted_iota(jnp.int32, x.shape, 1)
    return jnp.where((lane & (HD - 1)) < HD // 2, -pltpu.roll(x, 128 - HD // 2, 1), pltpu.roll(x, HD // 2, 1))


def _tile_sel():
    i = lax.broadcasted_iota(jnp.int32, (KVW, AW), 0)
    j = lax.broadcasted_iota(jnp.int32, (KVW, AW), 1)
    return jnp.where(i == (j // 256) * HD + (j & (HD - 1)), 1.0, 0.0).astype(BF16)


def _mixproj_fwd(h, modv, gvec, win, cos, sin, *, T, name, carry=None):
    R = h.shape[0]
    n_lat, n_tiles = T // TM, R // TM

    def body(h_ref, mod_ref, g_ref, win_ref, cos_ref, sin_ref, u_ref, q_ref, k4_ref, v4_ref):
        shift, scale = mod_ref[0, 3:4, :], mod_ref[0, 4:5, :]
        _, _, _, n = _norm_mod(h_ref[...], g_ref[1:2, :], shift, scale)
        proj = _dot_nt(n.astype(BF16), win_ref[...])
        u_ref[...] = proj[:, :PW]
        cs, sn = cos_ref[...], sin_ref[...]
        for s in range(AW // 128):
            x = proj[:, PW + 128 * s : PW + 128 * (s + 1)]
            q_ref[:, 128 * s : 128 * (s + 1)] = ((x * cs + _rot_half(x) * sn) * (HD ** -0.5)).astype(BF16)
        k = proj[:, PW + AW : PW + AW + KVW]
        k = (k * cs + _rot_half(k) * sn).astype(BF16)
        v = proj[:, PW + AW + KVW :].astype(BF16)
        sel = _tile_sel()
        k4_ref[...] = _dot(k, sel).astype(BF16)
        v4_ref[...] = _dot(v, sel).astype(BF16)

    return _call(
        body,
        name=name,
        grid=(n_tiles,),
        in_specs=[_rows(TM, D), _mod_spec(n_lat), _full((8, D)), _full((PROJ, D)), _rows(TM, 128), _rows(TM, 128)],
        out_specs=[_rows(TM, PW), _rows(TM, AW), _rows(TM, AW), _rows(TM, AW)],
        out_shape=[_sds((R, PW), F32), _sds((R, AW), BF16), _sds((R, AW), BF16), _sds((R, AW), BF16)],
        scratch_shapes=[],
        args=(h, modv, gvec, win, cos, sin),
        carry=carry,
    )


def _win_start(j, hi):
    return pl.multiple_of(jnp.clip((j - 1) * BLK, 0, hi - 3 * BLK), BLK)


def _hi_lo(x):
    hi = x.astype(BF16)
    return hi, (x - hi.astype(F32)).astype(BF16)


def _pool_bounds(t, w, T, R):
    is_ctx = t >= T
    lo = jnp.maximum(t - w // 2, jnp.where(is_ctx, T, 0))
    hi = jnp.minimum(t + w // 2, jnp.where(is_ctx, R, T))
    return lo, hi


def _pooled(u_v, j, T, R):
    start = _win_start(j, R)
    u3_hi, u3_lo = _hi_lo(u_v[pl.ds(start, 3 * BLK), :])
    ub = u_v[pl.ds(pl.multiple_of(j * BLK, BLK), BLK), :]
    t = j * BLK + lax.broadcasted_iota(jnp.int32, (BLK, 1), 0)
    pos = start + lax.broadcasted_iota(jnp.int32, (1, 3 * BLK), 1)
    pooled, counts = [], []
    for g, w in enumerate(POOL_WINDOWS):
        lo, hi = _pool_bounds(t, w, T, R)
        band = jnp.where(pos >= lo, jnp.where(pos < hi, 1.0, 0.0), 0.0).astype(BF16)
        sl = slice(g * 128, (g + 1) * 128)
        sums = _dot(band, u3_hi[:, sl]) + _dot(band, u3_lo[:, sl])
        cnt = (hi - lo).astype(F32)
        pooled.append(sums / cnt - ub[:, sl])
        counts.append(cnt)
    return pooled, counts


def _stack_heads(x):
    lane_h = lax.broadcasted_iota(jnp.int32, x.shape, 1) // HD
    return jnp.concatenate([jnp.where(lane_h == h, x, jnp.zeros_like(x)) for h in range(4)], axis=0)


def _unstack_heads(x):
    lane_h = lax.broadcasted_iota(jnp.int32, (BLK, 256), 1) // HD
    out = jnp.zeros((BLK, 256), F32)
    for h in range(4):
        out = out + jnp.where(lane_h == h, x[h * BLK : (h + 1) * BLK, :], 0.0)
    return out


def _window_mask(j, start_l, nbl):
    rowi = lax.broadcasted_iota(jnp.int32, (4 * BLK, 1), 0)
    qpos = j * BLK + (rowi & (BLK - 1))
    kpos = start_l + lax.broadcasted_iota(jnp.int32, (1, 3 * BLK), 1)
    reach = jnp.where(j < nbl, BLK, -1)
    return jnp.abs(kpos - qpos) <= reach


def _attn_exps(qs, kl, kc, sink_ref, g, valid):
    s_l = jnp.where(valid, _dot_nt(qs, kl), NEG)
    s_c = _dot_nt(qs, kc)
    rb = lax.broadcasted_iota(jnp.int32, (4 * BLK, 1), 0) // BLK
    sk = jnp.where(rb == 0, sink_ref[4 * g], jnp.where(rb == 1, sink_ref[4 * g + 1],
                   jnp.where(rb == 2, sink_ref[4 * g + 2], sink_ref[4 * g + 3])))
    m = jnp.maximum(jnp.maximum(jnp.max(s_l, axis=1, keepdims=True), jnp.max(s_c, axis=1, keepdims=True)), sk)
    e_l, e_c, e_s = jnp.exp(s_l - m), jnp.exp(s_c - m), jnp.exp(sk - m)
    inv = 1.0 / (jnp.sum(e_l, axis=1, keepdims=True) + jnp.sum(e_c, axis=1, keepdims=True) + e_s)
    return e_l, e_c, e_s, inv


def _attnpool_fwd(u, q, k4, v4, sink, w_pool, pool_scale, *, T, name, carry=None):
    R = u.shape[0]
    nb, nbl = R // BLK, T // BLK

    def body(q_ref, sink_ref, wp_ref, ps_ref, u_hbm, k4_hbm, v4_hbm, cat_ref, u_v, k4_v, v4_v, sem):
        j = pl.program_id(0)

        @pl.when(j == 0)
        def _():
            _load_weights([(u_hbm, u_v), (k4_hbm, k4_v), (v4_hbm, v4_v)], sem)

        pooled, _ = _pooled(u_v, j, T, R)
        for g in range(4):
            mixed = _dot(pooled[g].astype(BF16), wp_ref[g].astype(BF16)) * ps_ref[:, g * 128 : (g + 1) * 128]
            cat_ref[:, g * 128 : (g + 1) * 128] = mixed.astype(BF16)

        start_l = _win_start(j, T)
        valid = _window_mask(j, start_l, nbl)
        for g in range(2):
            gl = slice(g * 256, (g + 1) * 256)
            qs = _stack_heads(q_ref[:, gl])
            e_l, e_c, _, inv = _attn_exps(qs, k4_v[pl.ds(start_l, 3 * BLK), gl], k4_v[T:R, gl], sink_ref, g, valid)
            o = _dot(e_l.astype(BF16), v4_v[pl.ds(start_l, 3 * BLK), gl]) + _dot(e_c.astype(BF16), v4_v[T:R, gl])
            cat_ref[:, PW + g * 256 : PW + (g + 1) * 256] = _unstack_heads(o * inv).astype(BF16)

    return _call(
        body,
        name=name,
        grid=(nb,),
        in_specs=[_rows(BLK, AW), SMEM, _full((4, 128, 128)), _full((1, PW)), ANY, ANY, ANY],
        out_specs=[_rows(BLK, D)],
        out_shape=[_sds((R, D), BF16)],
        scratch_shapes=[pltpu.VMEM((R, PW), F32), pltpu.VMEM((R, AW), BF16), pltpu.VMEM((R, AW), BF16),
                        pltpu.SemaphoreType.DMA((3,))],
        args=(q, sink, w_pool, pool_scale, u, k4, v4),
        carry=carry,
    )


def _mixout_fwd(h, cat, modv, wout, *, T, ctx_active, name, carry=None):
    R = h.shape[0]
    n_lat, n_tiles = T // TM, R // TM

    def body(h_ref, cat_ref, mod_ref, w_ref, ho_ref, mo_ref):
        i = pl.program_id(0)

        def compute():
            mo = _dot(cat_ref[...], w_ref[...])
            mo_ref[...] = mo.astype(BF16)
            ho_ref[...] = h_ref[...] + mod_ref[0, 5:6, :] * mo

        if ctx_active:
            compute()
        else:
            pl.when(i < n_lat)(compute)

            @pl.when(i >= n_lat)
            def _():
                ho_ref[...] = h_ref[...]
                mo_ref[...] = jnp.zeros_like(mo_ref)

    return _call(
        body,
        name=name,
        grid=(n_tiles,),
        in_specs=[_rows(TM, D), _rows(TM, D), _mod_spec(n_lat), _full((D, D))],
        out_specs=[_rows(TM, D), _rows(TM, D)],
        out_shape=[_sds((R, D), F32), _sds((R, D), BF16)],
        scratch_shapes=[],
        args=(h, cat, modv, wout),
        carry=carry,
    )


def _mixout_bwd(dho, mo, modv, wout, *, T, ctx_active, name, carry=None):
    R = dho.shape[0]
    n_lat, n_tiles = T // TM, R // TM

    def body(dho_ref, mo_ref, mod_ref, w_ref, dcat_ref, dmix_ref, part_ref):
        i = pl.program_id(0)
        first = jnp.logical_or(i == 0, i == n_lat)

        def compute():
            dho = dho_ref[...]
            dmix = (mod_ref[0, 5:6, :] * dho).astype(BF16)
            dmix_ref[...] = dmix
            dcat_ref[...] = _dot_nt(dmix, w_ref[...])
            dgate = jnp.sum(dho * mo_ref[...].astype(F32), axis=0, keepdims=True)
            _acc_partials(part_ref, first, {2: dgate})

        if ctx_active:
            compute()
        else:
            pl.when(i < n_lat)(compute)

            @pl.when(i >= n_lat)
            def _():
                dcat_ref[...] = jnp.zeros_like(dcat_ref)
                dmix_ref[...] = jnp.zeros_like(dmix_ref)
                part_ref[...] = jnp.zeros_like(part_ref)

    return _call(
        body,
        name=name,
        grid=(n_tiles,),
        in_specs=[_rows(TM, D), _rows(TM, D), _mod_spec(n_lat), _full((D, D))],
        out_specs=[_rows(TM, D), _rows(TM, D), _part_spec(n_lat)],
        out_shape=[_sds((R, D), F32), _sds((R, D), BF16), _sds((2, 8, D), F32)],
        scratch_shapes=[],
        args=(dho, mo, modv, wout),
        carry=carry,
    )


def _pool_bwd(u, dcat, w_pool, pool_scale, *, T, name):
    R = u.shape[0]
    nb = R // BLK

    def body(dcat_ref, wp_ref, ps_ref, u_hbm, dps_ref, dwp_ref, dsc_ref, u_v, sem):
        j = pl.program_id(0)

        @pl.when(j == 0)
        def _():
            _load_weights([(u_hbm, u_v)], sem)
            dwp_ref[...] = jnp.zeros_like(dwp_ref)
            dsc_ref[...] = jnp.zeros_like(dsc_ref)

        pooled, counts = _pooled(u_v, j, T, R)
        for g in range(4):
            sl = slice(g * 128, (g + 1) * 128)
            p_bf = pooled[g].astype(BF16)
            w_bf = wp_ref[g].astype(BF16)
            dmixed = dcat_ref[:, sl]
            dsc_ref[0:1, sl] += jnp.sum(dmixed * _dot(p_bf, w_bf), axis=0, keepdims=True)
            dmp = (dmixed * ps_ref[:, sl]).astype(BF16)
            dwp_ref[sl, :] += _dot_tn(p_bf, dmp)
            dps_ref[:, sl] = _dot_nt(dmp, w_bf) / counts[g]

    return pl.pallas_call(
        body,
        name=name,
        grid=(nb,),
        in_specs=[_rows(BLK, D), _full((4, 128, 128)), _full((1, PW)), ANY],
        out_specs=[_rows(BLK, PW), _full((PW, 128)), _full((8, PW))],
        out_shape=[_sds((R, PW), F32), _sds((PW, 128), F32), _sds((8, PW), F32)],
        scratch_shapes=[pltpu.VMEM((R, PW), F32), pltpu.SemaphoreType.DMA((1,))],
        compiler_params=_params(),
    )(dcat, w_pool, pool_scale, u)


def _fold_heads(x):
    y = x[:, :128] + x[:, 128:]
    return y + pltpu.roll(y, HD, 1)


def _attn_bwd(q, k4, v4, dcat, dps, sink, *, T, name, carry=None):
    R = q.shape[0]
    nb, nbl = R // BLK, T // BLK

    def body(q_ref, dcat_ref, sink_ref, k4_hbm, v4_hbm, dps_hbm, du_ref, dq_ref, dk_ref, dv_ref, dsk_ref,
             k4_v, v4_v, dps_v, sem):
        j = pl.program_id(0)

        @pl.when(j == 0)
        def _():
            _load_weights([(k4_hbm, k4_v), (v4_hbm, v4_v), (dps_hbm, dps_v)], sem)
            dk_ref[...] = jnp.zeros_like(dk_ref)
            dv_ref[...] = jnp.zeros_like(dv_ref)
            dsk_ref[...] = jnp.zeros_like(dsk_ref)

        start = _win_start(j, R)
        d3_hi, d3_lo = _hi_lo(dps_v[pl.ds(start, 3 * BLK), :])
        db = dps_v[pl.ds(pl.multiple_of(j * BLK, BLK), BLK), :]
        pos = j * BLK + lax.broadcasted_iota(jnp.int32, (BLK, 1), 0)
        t_r = start + lax.broadcasted_iota(jnp.int32, (1, 3 * BLK), 1)
        for g, w in enumerate(POOL_WINDOWS):
            sl = slice(g * 128, (g + 1) * 128)
            lo_r, hi_r = _pool_bounds(t_r, w, T, R)
            band_t = jnp.where(pos >= lo_r, jnp.where(pos < hi_r, 1.0, 0.0), 0.0).astype(BF16)
            lo_c, hi_c = _pool_bounds(pos, w, T, R)
            du_ref[:, sl] = _dot(band_t, d3_hi[:, sl]) + _dot(band_t, d3_lo[:, sl]) - db[:, sl] * (hi_c - lo_c).astype(F32)

        start_l = _win_start(j, T)
        valid = _window_mask(j, start_l, nbl)
        rb = lax.broadcasted_iota(jnp.int32, (4 * BLK, 1), 0) // BLK
        lane = lax.broadcasted_iota(jnp.int32, (1, 128), 1)
        dk_l, dk_c, dv_l, dv_c = [], [], [], []
        for g in range(2):
            gl = slice(g * 256, (g + 1) * 256)
            qs = _stack_heads(q_ref[:, gl])
            kl, kc = k4_v[pl.ds(start_l, 3 * BLK), gl], k4_v[T:R, gl]
            vl, vc = v4_v[pl.ds(start_l, 3 * BLK), gl], v4_v[T:R, gl]
            e_l, e_c, e_s, inv = _attn_exps(qs, kl, kc, sink_ref, g, valid)
            p_l, p_c, p_s = e_l * inv, e_c * inv, e_s * inv
            dos = _stack_heads(dcat_ref[:, PW + g * 256 : PW + (g + 1) * 256]).astype(BF16)
            dp_l, dp_c = _dot_nt(dos, vl), _dot_nt(dos, vc)
            delta = jnp.sum(p_l * dp_l, axis=1, keepdims=True) + jnp.sum(p_c * dp_c, axis=1, keepdims=True)
            ds_l = (p_l * (dp_l - delta)).astype(BF16)
            ds_c = (p_c * (dp_c - delta)).astype(BF16)
            dq_ref[:, gl] = _unstack_heads(_dot(ds_l, kl) + _dot(ds_c, kc)) * (HD ** -0.5)
            dk_l.append(_fold_heads(_dot_tn(ds_l, qs)))
            dk_c.append(_fold_heads(_dot_tn(ds_c, qs)))
            dv_l.append(_fold_heads(_dot_tn(p_l.astype(BF16), dos)))
            dv_c.append(_fold_heads(_dot_tn(p_c.astype(BF16), dos)))
            dsink = -p_s * delta
            for h in range(4):
                tot = jnp.sum(jnp.where(rb == h, dsink, 0.0), axis=0, keepdims=True)
                dsk_ref[4 * g + h : 4 * g + h + 1, :] += jnp.broadcast_to(tot, (1, 128))
        first = lane < HD
        dk_ref[pl.ds(start_l, 3 * BLK), :] += jnp.where(first, dk_l[0], dk_l[1])
        dk_ref[T:R, :] += jnp.where(first, dk_c[0], dk_c[1])
        dv_ref[pl.ds(start_l, 3 * BLK), :] += jnp.where(first, dv_l[0], dv_l[1])
        dv_ref[T:R, :] += jnp.where(first, dv_c[0], dv_c[1])

    return _call(
        body,
        name=name,
        grid=(nb,),
        in_specs=[_rows(BLK, AW), _rows(BLK, D), SMEM, ANY, ANY, ANY],
        out_specs=[_rows(BLK, PW), _rows(BLK, AW), _full((R, KVW)), _full((R, KVW)), _full((8, 128))],
        out_shape=[_sds((R, PW), F32), _sds((R, AW), F32), _sds((R, KVW), F32), _sds((R, KVW), F32),
                   _sds((8, 128), F32)],
        scratch_shapes=[pltpu.VMEM((R, AW), BF16), pltpu.VMEM((R, AW), BF16), pltpu.VMEM((R, PW), F32),
                        pltpu.SemaphoreType.DMA((3,))],
        args=(q, dcat, sink, k4, v4, dps),
        carry=carry,
    )


def _mixproj_bwd(h, dho, du, dq, dk, dv, modv, gvec, win, cos, sin, *, T, name):
    R = h.shape[0]
    n_lat, n_tiles = T // TM, R // TM

    def body(h_ref, dho_ref, du_ref, dq_ref, dk_ref, dv_ref, mod_ref, g_ref, win_ref, cos_ref, sin_ref,
             dh_ref, dproj_ref, n_ref, part_ref):
        i = pl.program_id(0)
        first = jnp.logical_or(i == 0, i == n_lat)
        shift, scale = mod_ref[0, 3:4, :], mod_ref[0, 4:5, :]
        g = g_ref[1:2, :]
        r, xhat, y, n = _norm_mod(h_ref[...], g, shift, scale)
        n_ref[...] = n.astype(BF16)
        cs, sn = cos_ref[...], sin_ref[...]
        dproj_ref[:, :PW] = du_ref[...].astype(BF16)
        for s in range(AW // 128):
            x = dq_ref[:, 128 * s : 128 * (s + 1)]
            dproj_ref[:, PW + 128 * s : PW + 128 * (s + 1)] = (x * cs - _rot_half(x) * sn).astype(BF16)
        x = dk_ref[...]
        dproj_ref[:, PW + AW : PW + AW + KVW] = (x * cs - _rot_half(x) * sn).astype(BF16)
        dproj_ref[:, PW + AW + KVW :] = dv_ref[...].astype(BF16)
        dn = _dot(dproj_ref[...], win_ref[...])
        dh, dshift, dscale, dg = _norm_mod_bwd(dn, r, xhat, y, g, scale)
        dh_ref[...] = dho_ref[...] + dh
        _acc_partials(part_ref, first, {0: dshift, 1: dscale, 3: dg})

    return pl.pallas_call(
        body,
        name=name,
        grid=(n_tiles,),
        in_specs=[_rows(TM, D), _rows(TM, D), _rows(TM, PW), _rows(TM, AW), _rows(TM, KVW), _rows(TM, KVW),
                  _mod_spec(n_lat), _full((8, D)), _full((PROJ, D)), _rows(TM, 128), _rows(TM, 128)],
        out_specs=[_rows(TM, D), _rows(TM, PROJ), _rows(TM, D), _part_spec(n_lat)],
        out_shape=[_sds((R, D), F32), _sds((R, PROJ), BF16), _sds((R, D), BF16), _sds((2, 8, D), F32)],
        compiler_params=_params(),
    )(h, dho, du, dq, dk, dv, modv, gvec, win, cos, sin)


def _loss_head(h, target, g_final, *, T, name):
    R = h.shape[0]
    n_lat, n_tiles = T // TM, R // TM

    def body(h_ref, t_ref, g_ref, dh_ref, loss_ref, dg_ref):
        i = pl.program_id(0)

        @pl.when(i == 0)
        def _():
            loss_ref[...] = jnp.zeros_like(loss_ref)
            dg_ref[...] = jnp.zeros_like(dg_ref)

        @pl.when(i < n_lat)
        def _():
            h = h_ref[...]
            g = g_ref[...]
            r = lax.rsqrt(jnp.mean(h * h, axis=-1, keepdims=True) + EPS)
            xhat = h * r
            err = xhat * g - t_ref[...]
            tot = jnp.sum(jnp.sum(err * err, axis=1, keepdims=True), axis=0, keepdims=True)
            loss_ref[...] += jnp.broadcast_to(tot * (0.5 / D), loss_ref.shape)
            dy = err * (1.0 / D)
            dg_ref[0:1, :] += jnp.sum(dy * xhat, axis=0, keepdims=True)
            dxh = dy * g
            dh_ref[...] = r * (dxh - xhat * jnp.mean(dxh * xhat, axis=-1, keepdims=True))

        @pl.when(i >= n_lat)
        def _():
            dh_ref[...] = jnp.zeros_like(dh_ref)

    return pl.pallas_call(
        body,
        name=name,
        grid=(n_tiles,),
        in_specs=[_rows(TM, D), pl.BlockSpec((TM, D), lambda i: (jnp.minimum(i, n_lat - 1), 0)), _full((1, D))],
        out_specs=[_rows(TM, D), _full((8, 128)), _full((8, D))],
        out_shape=[_sds((R, D), F32), _sds((8, 128), F32), _sds((8, D), F32)],
        compiler_params=_params(),
    )(h, target, g_final)


def _mod_fwd(c16, w_mod, b_cols, *, name):
    nl, _, cols = w_mod.shape

    def body(c_ref, w_ref, b_ref, o_ref):
        c = c_ref[...]
        sc = (c * _sigmoid(c)).astype(BF16)
        o_ref[0] = _dot(sc, w_ref[0].astype(BF16)) + b_ref[0]

    return pl.pallas_call(
        body,
        name=name,
        grid=(nl,),
        in_specs=[_full((16, D)), pl.BlockSpec((1, D, cols), lambda l: (l, 0, 0)),
                  pl.BlockSpec((1, 1, cols), lambda l: (l, 0, 0))],
        out_specs=pl.BlockSpec((1, 16, cols), lambda l: (l, 0, 0)),
        out_shape=_sds((nl, 16, cols), F32),
        compiler_params=_params(),
    )(c16, w_mod, b_cols)


def _mod_bwd(c16, dm_cols, w_mod, *, name):
    nl, _, cols = w_mod.shape

    def body(c_ref, dm_ref, w_ref, gw_ref, dc_ref):
        c = c_ref[...]
        sc = (c * _sigmoid(c)).astype(BF16)
        dm = dm_ref[0].astype(BF16)
        gw_ref[0] = _dot_tn(sc, dm)
        dc_ref[0] = _dot_nt(dm, w_ref[0].astype(BF16))

    return pl.pallas_call(
        body,
        name=name,
        grid=(nl,),
        in_specs=[_full((16, D)), pl.BlockSpec((1, 16, cols), lambda l: (l, 0, 0)),
                  pl.BlockSpec((1, D, cols), lambda l: (l, 0, 0))],
        out_specs=[pl.BlockSpec((1, D, cols), lambda l: (l, 0, 0)), pl.BlockSpec((1, 16, D), lambda l: (l, 0, 0))],
        out_shape=[_sds((nl, D, cols), F32), _sds((nl, 16, D), F32)],
        compiler_params=_params(),
    )(c16, dm_cols, w_mod)


def _coords():
    return lax.axis_index("x"), lax.axis_index("y"), lax.axis_index("c")


FWD = 8


def _peer(k, x, y, c):
    if k == FWD:
        return (x ^ (1 - c), y ^ c, c)
    return (1 - x if k & 4 else x, 1 - y if k & 2 else y, 1 - c if k & 1 else c)


def _lin(p):
    return 4 * p[0] + 2 * p[1] + p[2]


def _view(ref, slot):
    return ref if slot is None else ref.at[slot]


class _Round:
    def __init__(self, ins, out_shapes, plan, local_plan=(), n_alias=0):
        self.ins, self.out_shapes = list(ins), list(out_shapes)
        self.plan, self.local_plan, self.n_alias = list(plan), list(local_plan), n_alias
        fed = {p[3] for p in self.plan if p[0] == FWD}
        self.feeders = [n for n, p in enumerate(self.plan) if p[0] in (2, 4, 6) and p[3] in fed]

    def sems(self):
        return [pltpu.SemaphoreType.DMA((len(self.plan),)), pltpu.SemaphoreType.DMA((len(self.plan),)),
                pltpu.SemaphoreType.DMA((max(len(self.local_plan), 1),))]

    def _remote(self, in_refs, out_refs, sems, incoming, pick):
        in_refs = list(out_refs[: self.n_alias]) + list(in_refs[self.n_alias :])
        x, y, c = _coords()
        me = _lin((x, y, c))
        copies = {}
        for idx, (k, ii, sfn, oi, dfn) in enumerate(self.plan):
            if not pick(idx, "d2d" if k == 1 else "fwd" if k == FWD else "ici"):
                continue
            peer = _peer(k, x, y, c)
            sender, receiver = (_lin(peer), me) if incoming else (me, _lin(peer))
            src = out_refs[oi] if ii is None else in_refs[ii]
            copies[idx] = pltpu.make_async_remote_copy(
                src_ref=_view(src, sfn(sender, receiver)), dst_ref=_view(out_refs[oi], dfn(sender, receiver)),
                send_sem=sems[0].at[idx], recv_sem=sems[1].at[idx], device_id=peer, device_id_type=MESH)
        return copies

    def _local(self, in_refs, out_refs, sems):
        in_refs = list(out_refs[: self.n_alias]) + list(in_refs[self.n_alias :])
        me = _lin(_coords())
        return [pltpu.make_async_copy(_view(in_refs[ii], sfn(me)), _view(out_refs[oi], dfn(me)), sems[2].at[idx])
                for idx, (ii, sfn, oi, dfn) in enumerate(self.local_plan)]

    def start(self, in_refs, out_refs, sems, links=("ici", "d2d")):
        for cp in self._remote(in_refs, out_refs, sems, False, lambda n, link: link in links).values():
            cp.start()
        if "ici" in links:
            for cp in self._local(in_refs, out_refs, sems):
                cp.start()

    def mid(self, in_refs, out_refs, sems):
        if self.feeders:
            for cp in self._remote(in_refs, out_refs, sems, True, lambda n, link: n in self.feeders).values():
                cp.wait_recv()
            for cp in self._remote(in_refs, out_refs, sems, False, lambda n, link: link == "fwd").values():
                cp.start()

    def finish(self, in_refs, out_refs, sems):
        for cp in self._remote(in_refs, out_refs, sems, True, lambda n, link: n not in self.feeders).values():
            cp.wait_recv()
        for cp in self._remote(in_refs, out_refs, sems, False, lambda n, link: True).values():
            cp.wait_send()
        for cp in self._local(in_refs, out_refs, sems):
            cp.wait()


def _exchange(name, rnd):
    n_in, n_out = len(rnd.ins), len(rnd.out_shapes)

    def body(*refs):
        in_refs, out_refs, sems = refs[:n_in], refs[n_in : n_in + n_out], refs[n_in + n_out :]
        rnd.start(in_refs, out_refs, sems)
        rnd.mid(in_refs, out_refs, sems)
        rnd.finish(in_refs, out_refs, sems)

    return pl.pallas_call(
        body, name=name, in_specs=[ANY] * n_in, out_specs=[ANY] * n_out, out_shape=rnd.out_shapes,
        scratch_shapes=rnd.sems(), input_output_aliases={i: i for i in range(rnd.n_alias)})(*rnd.ins)


def _call(body, *, name, grid, in_specs, out_specs, out_shape, scratch_shapes, args, carry=None):
    params = _params(len(grid))
    if carry is None:
        outs = pl.pallas_call(body, name=name, grid=grid, in_specs=in_specs, out_specs=out_specs, out_shape=out_shape,
                              scratch_shapes=scratch_shapes, compiler_params=params)(*args)
        return list(outs), []
    n_ci, n_co, n_cs = len(in_specs), len(out_shape), len(scratch_shapes)
    n_xi, n_xo = len(carry.ins), len(carry.out_shapes)

    def wrapped(*refs):
        ci, xi = refs[:n_ci], refs[n_ci : n_ci + n_xi]
        o0 = n_ci + n_xi
        co, xo = refs[o0 : o0 + n_co], refs[o0 + n_co : o0 + n_co + n_xo]
        s0 = o0 + n_co + n_xo
        cs, sems = refs[s0 : s0 + n_cs], refs[s0 + n_cs :]
        ids = [pl.program_id(a) for a in range(len(grid))]
        first = functools.reduce(jnp.logical_and, [i == 0 for i in ids])
        last = functools.reduce(jnp.logical_and, [i == g - 1 for i, g in zip(ids, grid)])

        @pl.when(first)
        def _():
            carry.start(xi, xo, sems, links=("ici",))

        if carry.feeders:
            step = functools.reduce(lambda acc, ig: acc * ig[1] + ig[0], zip(ids, grid), 0)
            n_steps = functools.reduce(lambda a, b: a * b, grid)

            @pl.when(step == min(n_steps - 1, (3 * n_steps) // 5))
            def _():
                carry.mid(xi, xo, sems)

        body(*ci, *co, *cs)

        @pl.when(first)
        def _():
            carry.start(xi, xo, sems, links=("d2d",))

        @pl.when(last)
        def _():
            carry.finish(xi, xo, sems)

    outs = pl.pallas_call(
        wrapped, name=name, grid=grid, in_specs=list(in_specs) + [ANY] * n_xi, out_specs=list(out_specs) + [ANY] * n_xo,
        out_shape=list(out_shape) + carry.out_shapes, scratch_shapes=list(scratch_shapes) + carry.sems(),
        input_output_aliases={n_ci + i: n_co + i for i in range(carry.n_alias)}, compiler_params=params,
    )(*args, *carry.ins)
    return list(outs[:n_co]), list(outs[n_co:])


def _gather_direct(arrays):
    na = len(arrays)
    outs = [_sds((NDEV,) + a.shape, a.dtype) for a in arrays]
    plan = [(k, i, lambda s, r: None, i, lambda s, r: s) for i in range(na) for k in range(1, NDEV)]
    return _Round(arrays, outs, plan, [(i, lambda m: None, i, lambda m: m) for i in range(na)])


def _gather_a(arrays):
    na = len(arrays)
    outs = [_sds((NDEV,) + a.shape, a.dtype) for a in arrays]
    plan = [(k, i, lambda s, r: None, i, lambda s, r: s) for i in range(na) for k in (2, 4)]
    handed = lambda s, r: s ^ (2 << (s & 1))
    plan += [(FWD, None, handed, i, handed) for i in range(na)]
    return _Round(arrays, outs, plan, [(i, lambda m: None, i, lambda m: m) for i in range(na)])


def _gather_b(got):
    na = len(got)
    plan = [(1, i, (lambda s, r, k=k: s ^ k), i, (lambda s, r, k=k: s ^ k)) for i in range(na) for k in (0, 2, 4, 6)]
    return _Round(got, [_sds(g.shape, g.dtype) for g in got], plan, n_alias=na)


def _scatter_1(grads):
    plan = [(1, i, (lambda s, r, q=q: 2 * q + (r & 1)), i, (lambda s, r, q=q: q))
            for i in range(len(grads)) for q in range(4)]
    return _Round(grads, [_sds((4,) + g.shape[1:], g.dtype) for g in grads], plan)


def _scatter_2(chip):
    plan = [(k, i, lambda s, r: r >> 1, i, (lambda s, r, j=j: j)) for i in range(len(chip)) for j, k in enumerate((2, 4, 6))]
    return _Round(chip, [_sds((3,) + g.shape[1:], g.dtype) for g in chip], plan)


def _add_pairs(grads, got, pos, *, name):
    n = len(grads)
    mine = lambda a: pl.BlockSpec((1,) + a.shape[1:], lambda q, p: (2 * q + p[0], 0, 0))
    slot = lambda a: pl.BlockSpec((1,) + a.shape[1:], lambda q, p: (q, 0, 0))

    def body(pos_ref, *refs):
        for g_ref, r_ref, o_ref in zip(refs[:n], refs[n : 2 * n], refs[2 * n :]):
            o_ref[...] = (g_ref[...].astype(F32) + r_ref[...].astype(F32)).astype(o_ref.dtype)

    return pl.pallas_call(
        body,
        name=name,
        grid_spec=pltpu.PrefetchScalarGridSpec(
            num_scalar_prefetch=1, grid=(4,),
            in_specs=[mine(g) for g in grads] + [slot(g) for g in grads],
            out_specs=[slot(g) for g in grads]),
        out_shape=[_sds((4,) + g.shape[1:], g.dtype) for g in grads],
        compiler_params=_params(),
    )(pos, *grads, *got)


def _sum_adamw(chip, got, pos, w, m, v, layer, prior, *, name):
    _, sh, wd = chip.shape
    nl, rows, cols = w.shape
    nb, blk = 2, (sh // 2, wd)
    part = lambda n: pl.BlockSpec((n, sh // 2, wd), lambda i, p: ((p[1] if n == 1 else 0), i, 0))
    mine = pl.BlockSpec(blk, lambda i, p: (layer * nb + i, 0))
    flat = lambda t: t.reshape(nl * rows, cols)
    n_prior = 0 if prior is None else 4

    def body(pos_ref, c_ref, r_ref, w_ref, m_ref, v_ref, *refs):
        g_ref, d_ref, m2_ref, v2_ref = refs[n_prior:]
        g = c_ref[0].astype(F32)
        for s in range(3):
            g = g + r_ref[s].astype(F32)
        g_ref[...] = g
        d_ref[...], m2_ref[...], v2_ref[...] = _adamw_math(w_ref[...], g, m_ref[...], v_ref[...])

    outs = pl.pallas_call(
        body,
        name=name,
        grid_spec=pltpu.PrefetchScalarGridSpec(
            num_scalar_prefetch=1, grid=(nb,),
            in_specs=[part(1), part(3), mine, mine, mine] + [ANY] * n_prior,
            out_specs=[mine] * 4),
        out_shape=[_sds((nl * rows, cols), F32)] * 4,
        input_output_aliases={6 + k: k for k in range(n_prior)},
        compiler_params=_params(),
    )(pos, chip, got, flat(w), flat(m), flat(v), *(flat(t) for t in prior or ()))
    return [o.reshape(w.shape) for o in outs]


def _adamw_math(w, g, m, v):
    m2 = ADAM_B1 * m + (1.0 - ADAM_B1) * g
    v2 = ADAM_B2 * v + (1.0 - ADAM_B2) * (g * g)
    m_hat = m2 / (1.0 - ADAM_B1 ** ADAM_STEP)
    v_hat = v2 / (1.0 - ADAM_B2 ** ADAM_STEP)
    delta = -ADAM_LR * (m_hat / (jnp.sqrt(v_hat) + ADAM_EPS) + ADAM_WD * w)
    return delta, m2, v2


def _adamw(w, g, m, v, *, name, carry=None):
    shape = w.shape
    flat = [t.reshape(-1, shape[-1]) for t in (w, g, m, v)]
    rows, cols = flat[0].shape
    tr = rows // 8 if rows % 64 == 0 else rows
    spec = _rows(tr, cols)

    def body(w_ref, g_ref, m_ref, v_ref, d_ref, m2_ref, v2_ref):
        d_ref[...], m2_ref[...], v2_ref[...] = _adamw_math(w_ref[...], g_ref[...], m_ref[...], v_ref[...])

    outs, got = _call(body, name=name, grid=(rows // tr,), in_specs=[spec] * 4, out_specs=[spec] * 3,
                      out_shape=[_sds((rows, cols), F32)] * 3, scratch_shapes=[], args=flat, carry=carry)
    return tuple(o.reshape(shape) for o in outs), got


def _adds(tag, grads, got, *, pos):
    return _add_pairs(list(grads), list(got)[: len(grads)], pos, name=f"rs_add_{tag}")


def _small_sums(packets, nf, dwp, dsc, dsk, *, name):
    flat = [p for layer in packets for p in layer]

    def total(ref, *idx):
        acc = ref[(0,) + idx]
        for dev in range(1, NDEV):
            acc = acc + ref[(dev,) + idx]
        return acc

    def body(*refs):
        pk = refs[:6]
        nf_ref, dwp0, dwp1, dsc0, dsc1, dsk0, dsk1 = refs[6:13]
        dm_ref, gb_ref, gn_ref, gnf_ref, gwp_ref, gps_ref, gsk_ref = refs[13:]
        dm_ref[...] = jnp.zeros_like(dm_ref)
        gn_ref[...] = jnp.zeros_like(gn_ref)
        for l in range(2):
            for sb in range(3):
                p = pk[3 * l + sb]
                for r in range(3):
                    col = slice((3 * sb + r) * D, (3 * sb + r + 1) * D)
                    lat = p[0, 0, r : r + 1, :]
                    dm_ref[l, 0:1, col] = lat
                    for dev in range(1, NDEV):
                        row = p[dev, 0, r : r + 1, :]
                        dm_ref[l, dev : dev + 1, col] = row
                        lat = lat + row
                    ctx = total(p, 1, slice(r, r + 1), slice(None))
                    dm_ref[l, 8:9, col] = ctx
                    gb_ref[l : l + 1, col] = lat + ctx
                gn_ref[l, sb : sb + 1, :] = total(p, 0, slice(3, 4), slice(None)) + total(p, 1, slice(3, 4), slice(None))
        gnf_ref[...] = total(nf_ref, slice(0, 1), slice(None))
        for l, (a, b, c) in enumerate(((dwp0, dsc0, dsk0), (dwp1, dsc1, dsk1))):
            gwp_ref[l] = total(a, slice(None), slice(None))
            gps_ref[l : l + 1, :] = total(b, slice(0, 1), slice(None))
            gsk_ref[l] = total(c, slice(None), slice(None))

    ins = flat + [nf, dwp[0], dwp[1], dsc[0], dsc[1], dsk[0], dsk[1]]
    return pl.pallas_call(
        body,
        name=name,
        out_shape=[_sds((2, 16, NMOD * D), F32), _sds((2, NMOD * D), F32), _sds((2, 8, D), F32), _sds((1, D), F32),
                   _sds((2, PW, 128), F32), _sds((2, PW), F32), _sds((2, 8, 128), F32)],
        compiler_params=pltpu.CompilerParams(vmem_limit_bytes=VMEM_LIMIT),
    )(*ins)


def _small_adamw(c_ctx, dc_all, triples, *, name):
    n = len(triples)

    def body(*refs):
        c_ref, dc_ref = refs[0], refs[1]
        ins = refs[2 : 2 + 4 * n - 1]
        outs = refs[2 + 4 * n - 1 :]
        acc = dc_ref[0, 0, 8:9, :] + dc_ref[0, 1, 8:9, :]
        for dev in range(1, NDEV):
            acc = acc + (dc_ref[dev, 0, 8:9, :] + dc_ref[dev, 1, 8:9, :])
        c = c_ref[...]
        sig = _sigmoid(c)
        g_c = acc * (sig * (1.0 + c * (1.0 - sig)))
        outs[0][...] = g_c
        pos = 0
        for k in range(n):
            if k == 0:
                w, g, m, v = ins[0][...], g_c, ins[1][...], ins[2][...]
                pos = 3
            else:
                w, g, m, v = (ins[pos + t][...] for t in range(4))
                pos += 4
            d, m2, v2 = _adamw_math(w, g, m, v)
            outs[1 + 3 * k][...], outs[2 + 3 * k][...], outs[3 + 3 * k][...] = d, m2, v2

    flat_in = [c_ctx, dc_all]
    out_shape = [_sds(c_ctx.shape, F32)]
    for k, (w, g, m, v) in enumerate(triples):
        flat_in += [w, m, v] if k == 0 else [w, g, m, v]
        out_shape += [_sds(w.shape, F32)] * 3
    return pl.pallas_call(body, name=name, out_shape=out_shape,
                          compiler_params=pltpu.CompilerParams(vmem_limit_bytes=VMEM_LIMIT))(*flat_in)


def _rope_tables(T, R):
    t = jnp.arange(T)
    inv = ROPE_BASE ** (-jnp.arange(0, HD // 2, 2, dtype=F32) / (HD // 2))
    ang = jnp.concatenate([(t // GRID_W).astype(F32)[:, None] * inv, (t % GRID_W).astype(F32)[:, None] * inv], axis=-1)
    cos = jnp.concatenate([jnp.tile(jnp.cos(ang), (1, 4)), jnp.ones((R - T, 128), F32)], axis=0)
    sin = jnp.concatenate([jnp.tile(jnp.sin(ang), (1, 4)), jnp.zeros((R - T, 128), F32)], axis=0)
    return cos, sin


def kernel(x, c, ctx, c_ctx, w_mod, b_mod, norm_ffn1, w_ffn1_in, w_ffn1_out, norm_mix, w_in, w_pool, pool_scale, sink, w_out, norm_ffn2, w_ffn2_in, w_ffn2_out, norm_final, loss_target, m_c_ctx, m_w_mod, m_b_mod, m_norm_ffn1, m_w_ffn1_in, m_w_ffn1_out, m_norm_mix, m_w_in, m_w_pool, m_pool_scale, m_sink, m_w_out, m_norm_ffn2, m_w_ffn2_in, m_w_ffn2_out, m_norm_final, v_c_ctx, v_w_mod, v_b_mod, v_norm_ffn1, v_w_ffn1_in, v_w_ffn1_out, v_norm_mix, v_w_in, v_w_pool, v_pool_scale, v_sink, v_w_out, v_norm_ffn2, v_w_ffn2_in, v_w_ffn2_out, v_norm_final):
    T = x.shape[1]
    R = T + LC
    nl = w_mod.shape[0]
    cx, cy, cc = _coords()
    me = _lin((cx, cy, cc))
    pos = jnp.stack([cc, 2 * cx + cy]).astype(jnp.int32)
    mcols = w_mod.shape[2]

    shards = [([w_ffn1_in[l].T.astype(BF16), w_ffn1_out[l].astype(BF16)],
               [w_in[l].T.astype(BF16), w_out[l].astype(BF16)],
               [w_ffn2_in[l].T.astype(BF16), w_ffn2_out[l].astype(BF16)]) for l in range(nl)]

    got = _exchange("ag_c_w", _merge(_gather_direct([c]), _gather_a(shards[0][0] + shards[0][1])))
    c_all, w_first = got[0], got[1:]
    c16 = jnp.concatenate([c_all.reshape(NDEV, D), c_ctx[None], jnp.zeros((16 - NDEV - 1, D), F32)], axis=0)
    b_cols = lax.dynamic_slice(b_mod, (0, me * mcols), (nl, mcols)).reshape(nl, 1, mcols)
    got = _exchange("ag_mod_w", _merge(_gather_b(w_first), _gather_direct([_mod_fwd(c16, w_mod, b_cols, name="mod_fwd")])))
    w_first, mod_all = got[:4], got[4]
    mod_all = jnp.transpose(mod_all, (1, 2, 0, 3)).reshape(nl, 16, NMOD, D)
    mine = lax.dynamic_index_in_dim(mod_all, me, axis=1, keepdims=False)
    pad = jnp.zeros((nl, 16 - NMOD, D), F32)
    modv = jnp.stack([jnp.concatenate([mine, pad], axis=1), jnp.concatenate([mod_all[:, 8], pad], axis=1)], axis=1)

    gvec = [jnp.concatenate([norm_ffn1[l][None], norm_mix[l][None], norm_ffn2[l][None], jnp.zeros((5, D), F32)], axis=0)
            for l in range(nl)]
    cos, sin = _rope_tables(T, R)
    ps2 = [pool_scale[l][None] for l in range(nl)]

    h = jnp.concatenate([x[0], ctx[0]], axis=0)
    loss_all, dh, small, nf_all, big, last_partial = _forward_backward(
        h, loss_target[0], modv, gvec, shards, w_first, cos, sin, sink, w_pool, ps2, norm_final, pos, T=T)
    loss = jnp.sum(loss_all[:, 0, 0])
    grad_x = dh[:T][None]

    dm, g_b_mod, g_norms, g_nf, g_wp, g_ps, g_sk = _small_sums(
        [small[l][0:3] for l in range(nl)], nf_all, *[[small[l][k] for l in range(nl)] for k in (3, 4, 5)],
        name="small_sums")
    dm_cols = lax.dynamic_slice(dm, (0, 0, me * mcols), (nl, 16, mcols))
    g_w_mod, dc_part = _mod_bwd(c16, dm_cols, w_mod, name="mod_bwd")
    got = _exchange("rs1_tail", _merge(_scatter_1([last_partial]), _gather_direct([dc_part])))
    (c1o,), dc_all = _adds("ffn1_out_0", [last_partial], got[:1], pos=pos), got[1]

    delta, new_m, new_v = {}, {}, {}
    (delta["w_mod"], new_m["w_mod"], new_v["w_mod"]), got = _adamw(
        w_mod, g_w_mod, m_w_mod, v_w_mod, name="adamw_w_mod", carry=_scatter_2([c1o]))
    big[0][1] = (c1o, got[0])

    grads = {
        "b_mod": g_b_mod, "norm_ffn1": g_norms[:, 0], "norm_mix": g_norms[:, 1], "norm_ffn2": g_norms[:, 2],
        "w_pool": g_wp.reshape(w_pool.shape), "pool_scale": g_ps, "sink": g_sk[:, :, 0], "norm_final": g_nf.reshape(D),
        "w_mod": g_w_mod,
    }
    weights = dict(c_ctx=c_ctx, w_mod=w_mod, b_mod=b_mod, norm_ffn1=norm_ffn1, w_ffn1_in=w_ffn1_in, w_ffn1_out=w_ffn1_out,
                   norm_mix=norm_mix, w_in=w_in, w_pool=w_pool, pool_scale=pool_scale, sink=sink, w_out=w_out,
                   norm_ffn2=norm_ffn2, w_ffn2_in=w_ffn2_in, w_ffn2_out=w_ffn2_out, norm_final=norm_final)
    moms = dict(c_ctx=(m_c_ctx, v_c_ctx), w_mod=(m_w_mod, v_w_mod), b_mod=(m_b_mod, v_b_mod),
                norm_ffn1=(m_norm_ffn1, v_norm_ffn1), w_ffn1_in=(m_w_ffn1_in, v_w_ffn1_in),
                w_ffn1_out=(m_w_ffn1_out, v_w_ffn1_out), norm_mix=(m_norm_mix, v_norm_mix), w_in=(m_w_in, v_w_in),
                w_pool=(m_w_pool, v_w_pool), pool_scale=(m_pool_scale, v_pool_scale), sink=(m_sink, v_sink),
                w_out=(m_w_out, v_w_out), norm_ffn2=(m_norm_ffn2, v_norm_ffn2), w_ffn2_in=(m_w_ffn2_in, v_w_ffn2_in),
                w_ffn2_out=(m_w_ffn2_out, v_w_ffn2_out), norm_final=(m_norm_final, v_norm_final))
    order = list(weights)
    small_names = ["c_ctx", "b_mod", "norm_ffn1", "norm_mix", "w_pool", "pool_scale", "sink", "norm_ffn2", "norm_final"]

    def as2d(name, t):
        if name == "w_pool":
            return t.reshape(-1, 128)
        return t.reshape(1, -1) if t.ndim == 1 else t

    triples = [(as2d(n, weights[n]), None if n == "c_ctx" else as2d(n, grads[n]), as2d(n, moms[n][0]), as2d(n, moms[n][1]))
               for n in small_names]
    outs = _small_adamw(as2d("c_ctx", c_ctx), dc_all, triples, name="small_adamw")
    grads["c_ctx"] = outs[0].reshape(c_ctx.shape)
    for k, n in enumerate(small_names):
        delta[n], new_m[n], new_v[n] = (o.reshape(weights[n].shape) for o in outs[1 + 3 * k : 4 + 3 * k])
    for k, n in enumerate(["w_ffn1_in", "w_ffn1_out", "w_in", "w_out", "w_ffn2_in", "w_ffn2_out"]):
        turn = (lambda t: jnp.swapaxes(t, 1, 2)) if k % 2 == 0 else (lambda t: t)
        wmv = [turn(t) for t in (weights[n], *moms[n])]
        outs = None
        for l in reversed(range(nl)):
            outs = _sum_adamw(*big[l][k], pos, *wmv, l, outs, name=f"adamw_{n}_{l}")
        grads[n], delta[n], new_m[n], new_v[n] = (turn(o) for o in outs)

    return (loss, grad_x, *[grads[n] for n in order], *[delta[n] for n in order],
            *[new_m[n] for n in order], *[new_v[n] for n in order])


def _merge(*rounds):
    ins, outs, plan, local, n_alias = [], [], [], [], 0
    for r in rounds:
        assert r.n_alias == 0 or (not ins and r.n_alias == len(r.ins) == len(r.out_shapes))
        oi, oo = len(ins), len(outs)
        plan += [(k, None if i is None else i + oi, sf, o + oo, df) for k, i, sf, o, df in r.plan]
        local += [(i + oi, sf, o + oo, df) for i, sf, o, df in r.local_plan]
        ins += r.ins
        outs += r.out_shapes
        n_alias += r.n_alias
    return _Round(ins, outs, plan, local, n_alias)


def _forward_backward(h, target, modv, gvec, shards, w_first, cos, sin, sink, w_pool, ps2, norm_final, pos, *, T):
    nl = len(gvec)
    flat = lambda ws: [w.reshape(-1, D) for w in ws]
    saved = []
    w1, wm = flat(w_first[:2]), flat(w_first[2:])
    for l in range(nl):
        last = l == nl - 1
        h0 = h
        if l == 0:
            (h1, a1, b1, f1), got = _ffn_fwd(h0, modv[l], gvec[l], *w1, T=T, mrow=0, grow=0, ctx_active=True,
                                             name=f"ffn1_fwd_{l}", carry=_gather_a(shards[l][2]))
            (u, q, k4, v4), got = _mixproj_fwd(h1, modv[l], gvec[l], wm[0], cos, sin, T=T, name=f"mixproj_fwd_{l}",
                                               carry=_gather_b(got))
            w2 = flat(got)
        else:
            (h1, a1, b1, f1), got = _ffn_fwd(h0, modv[l], gvec[l], *w1, T=T, mrow=0, grow=0, ctx_active=True,
                                             name=f"ffn1_fwd_{l}", carry=_gather_b(nxt_m + nxt_2))
            wm, w2 = flat(got[:2]), flat(got[2:])
            (u, q, k4, v4), _ = _mixproj_fwd(h1, modv[l], gvec[l], wm[0], cos, sin, T=T, name=f"mixproj_fwd_{l}")
        (cat,), nxt_1 = _attnpool_fwd(u, q, k4, v4, sink[l], w_pool[l], ps2[l], T=T, name=f"attnpool_fwd_{l}",
                                      carry=None if last else _gather_a(shards[l + 1][0]))
        (h2, mo), nxt_m = _mixout_fwd(h1, cat, modv[l], wm[1], T=T, ctx_active=not last, name=f"mixout_fwd_{l}",
                                      carry=None if last else _gather_a(shards[l + 1][1]))
        (h3, a2, b2, f2), got = _ffn_fwd(h2, modv[l], gvec[l], *w2, T=T, mrow=6, grow=2, ctx_active=not last,
                                         name=f"ffn2_fwd_{l}",
                                         carry=None if last else _merge(_gather_b(nxt_1), _gather_a(shards[l + 1][2])))
        saved.append((h0, a1, b1, f1, h1, u, q, k4, v4, cat, mo, h2, a2, b2, f2, w1, wm, w2))
        h = h3
        if not last:
            w1, nxt_2 = flat(got[:2]), got[2:]

    dh, loss_part, dnf = _loss_head(h, target, norm_final[None], T=T, name="loss_head")

    adds = functools.partial(_adds, pos=pos)
    small, big = [None] * nl, {}
    prev = None
    for l in reversed(range(nl)):
        last = l == nl - 1
        h0, a1, b1, f1, h1, u, q, k4, v4, cat, mo, h2, a2, b2, f2, w1, wm, w2 = saved[l]
        (dh, dab, s, n, df, pk2), got = _ffn_bwd(
            h2, dh, a2, b2, f2, modv[l], gvec[l], *w2, T=T, mrow=6, grow=2, ctx_active=not last, name=f"ffn2_bwd_{l}",
            carry=_merge(_scatter_1(prev[0]), _gather_a(prev[1])) if prev else None)
        if prev:
            c1, small_a = adds(f"ffn1_{l + 1}", prev[0], got[:2]), got[2:]
        g_w2i, got = _wgrad(dab, n, bk=WG_BK, sh=2 * DFF // NDEV, name=f"wgrad_ffn2_in_{l}",
                            carry=_scatter_2(c1[:1]) if prev else None)
        if prev:
            big[l + 1][0] = (c1[0], got[0])
        g_w2o, got = _wgrad(s, df, bk=WG_BK, sh=DFF // NDEV, name=f"wgrad_ffn2_out_{l}",
                            carry=_scatter_2(c1[1:]) if prev else None)
        if prev:
            big[l + 1][1] = (c1[1], got[0])
        rnd = _scatter_1([g_w2i, g_w2o])
        (dcat, dmix, pko), got = _mixout_bwd(dh, mo, modv[l], wm[1], T=T, ctx_active=not last, name=f"mixout_bwd_{l}",
                                             carry=_merge(_gather_b(small_a), rnd) if prev else rnd)
        if prev:
            small[l + 1], got = got[: len(small_a)], got[len(small_a) :]
        c2 = adds(f"ffn2_{l}", [g_w2i, g_w2o], got)
        g_wo, _ = _wgrad(cat, dmix, bk=D, sh=D // NDEV, name=f"wgrad_out_{l}")
        dps, dwp, dsc = _pool_bwd(u, dcat, w_pool[l], ps2[l], T=T, name=f"pool_bwd_{l}")
        (du, dq, dk, dv, dsk), got = _attn_bwd(q, k4, v4, dcat, dps, sink[l], T=T, name=f"attn_bwd_{l}", carry=_scatter_2(c2))
        big[l] = [None, None, None, None, (c2[0], got[0]), (c2[1], got[1])]
        dh, dproj, n, pkm = _mixproj_bwd(h1, dh, du, dq, dk, dv, modv[l], gvec[l], wm[0], cos, sin, T=T, name=f"mixproj_bwd_{l}")
        g_wi, _ = _wgrad(dproj, n, bk=PROJ, sh=PROJ // NDEV, name=f"wgrad_in_{l}")
        (dh, dab, s, n, df, pk1), got = _ffn_bwd(h0, dh, a1, b1, f1, modv[l], gvec[l], *w1, T=T, mrow=0, grow=0,
                                                 ctx_active=True, name=f"ffn1_bwd_{l}", carry=_scatter_1([g_wi, g_wo]))
        cm = adds(f"mix_{l}", [g_wi, g_wo], got)
        mine = [pk1, pkm + pko, pk2, dwp, dsc, dsk]
        rnd = _merge(_scatter_2(cm), _gather_a(mine + [dnf, loss_part])) if l == 0 else _scatter_2(cm)
        g_w1i, got = _wgrad(dab, n, bk=WG_BK, sh=2 * DFF // NDEV, name=f"wgrad_ffn1_in_{l}", carry=rnd)
        big[l][2:4] = [(cm[0], got[0]), (cm[1], got[1])]
        if l > 0:
            g_w1o, _ = _wgrad(s, df, bk=WG_BK, sh=DFF // NDEV, name=f"wgrad_ffn1_out_{l}")
            prev = ([g_w1i, g_w1o], mine)
    (c1i,) = adds("ffn1_in_0", [g_w1i], _exchange("rs1_ffn1_in_0", _scatter_1([g_w1i])))
    g_w1o, got = _wgrad(s, df, bk=WG_BK, sh=DFF // NDEV, name="wgrad_ffn1_out_0",
                        carry=_merge(_gather_b(got[2:]), _scatter_2([c1i])))
    small[0], nf_all, loss_all = got[:6], got[6], got[7]
    big[0][0] = (c1i, got[8])
    return loss_all, dh, small, nf_all, big, g_w1o
```

```python
import functools

import jax
import jax.numpy as jnp
from jax import lax
from jax.experimental import pallas as pl
from jax.experimental.pallas import tpu as pltpu

F32, BF16 = jnp.float32, jnp.bfloat16

D = 1024
LC = 256
DFF = 2816
NMOD = 9
PW = 512
AW = 512
KVW = 128
PROJ = PW + AW + 2 * KVW
HD = 64
BLK = 128
GRID_W = 64
POOL_WINDOWS = (2, 4, 8, 16)
EPS = 1e-6
NEG = -1e30
ROPE_BASE = 10000.0
NDEV = 8
MESH = pl.DeviceIdType.MESH

ADAM_LR, ADAM_B1, ADAM_B2, ADAM_EPS, ADAM_WD, ADAM_STEP = 0.001, 0.9, 0.999, 1e-08, 0.01, 10

VMEM_LIMIT = 56 * 1024 * 1024
TM = 256
FFN_CHUNKS = ((0, 1536), (1536, 1280))
WG_BK = 1408

ANY = pl.BlockSpec(memory_space=pl.ANY)
SMEM = pl.BlockSpec(memory_space=pltpu.SMEM)


def _params(ngrid=1):
    return pltpu.CompilerParams(dimension_semantics=("arbitrary",) * ngrid, vmem_limit_bytes=VMEM_LIMIT)


def _dot(a, b):
    return jnp.dot(a, b, preferred_element_type=F32)


def _dot_nt(a, b):
    return lax.dot_general(a, b, (((1,), (1,)), ((), ())), preferred_element_type=F32)


def _dot_tn(a, b):
    return lax.dot_general(a, b, (((0,), (0,)), ((), ())), preferred_element_type=F32)


def _sigmoid(x):
    return 1.0 / (1.0 + jnp.exp(-x))


def _rows(tm, w):
    return pl.BlockSpec((tm, w), lambda i: (i, 0))


def _full(shape):
    nd = len(shape)
    return pl.BlockSpec(shape, lambda *_: (0,) * nd)


def _sds(shape, dtype):
    return jax.ShapeDtypeStruct(shape, dtype)


def _norm_mod(h, g, shift, scale):
    r = lax.rsqrt(jnp.mean(h * h, axis=-1, keepdims=True) + EPS)
    xhat = h * r
    y = xhat * g
    return r, xhat, y, y * (1.0 + scale) + shift


def _norm_mod_bwd(dn, r, xhat, y, g, scale):
    dshift = jnp.sum(dn, axis=0, keepdims=True)
    dscale = jnp.sum(dn * y, axis=0, keepdims=True)
    dy = dn * (1.0 + scale)
    dg = jnp.sum(dy * xhat, axis=0, keepdims=True)
    dxh = dy * g
    dh = r * (dxh - xhat * jnp.mean(dxh * xhat, axis=-1, keepdims=True))
    return dh, dshift, dscale, dg


def _acc_partials(part_ref, first, rows):
    @pl.when(first)
    def _():
        part_ref[...] = jnp.zeros_like(part_ref)

    for r, val in rows.items():
        part_ref[0, r : r + 1, :] += val


def _mod_spec(n_lat):
    return pl.BlockSpec((1, 16, D), lambda i: (i // n_lat, 0, 0))


def _part_spec(n_lat):
    return pl.BlockSpec((1, 8, D), lambda i: (i // n_lat, 0, 0))


def _load_weights(pairs, sem):
    copies = [pltpu.make_async_copy(src, dst, sem.at[k]) for k, (src, dst) in enumerate(pairs)]
    for cp in copies:
        cp.start()
    for cp in copies:
        cp.wait()


def _ffn_weight_copies(win_hbm, wout_hbm, win_v, wout_v, sem):
    loads = []
    for k, (c0, cw) in enumerate(FFN_CHUNKS):
        slabs = [(win_hbm, win_v, c0), (win_hbm, win_v, DFF + c0), (wout_hbm, wout_v, c0)]
        loads.append([pltpu.make_async_copy(src.at[pl.ds(r0, cw)], dst.at[pl.ds(r0, cw)], sem.at[3 * k + j])
                      for j, (src, dst, r0) in enumerate(slabs)])
    return loads


def _ffn_steps(i, n_active, loads, compute):
    @pl.when(i == 0)
    def _():
        for cp in sum(loads, []):
            cp.start()
        compute(loads)

    @pl.when(jnp.logical_and(i > 0, i < n_active))
    def _():
        compute(None)


def _wait_chunk(loads, k):
    if loads is not None:
        for cp in loads[k]:
            cp.wait()


def _ffn_fwd(h, modv, gvec, win, wout, *, T, mrow, grow, ctx_active, name, carry=None):
    R = h.shape[0]
    n_lat, n_tiles = T // TM, R // TM
    n_active = n_tiles if ctx_active else n_lat

    def body(h_ref, mod_ref, g_ref, win_hbm, wout_hbm, ho_ref, a_ref, b_ref, f_ref, win_v, wout_v, sem):
        i = pl.program_id(0)

        def compute(loads):
            h = h_ref[...]
            shift, scale, gate = (mod_ref[0, mrow + k : mrow + k + 1, :] for k in range(3))
            _, _, _, n = _norm_mod(h, g_ref[grow : grow + 1, :], shift, scale)
            n_bf = n.astype(BF16)
            acc = jnp.zeros((TM, D), F32)
            for k, (c0, cw) in enumerate(FFN_CHUNKS):
                _wait_chunk(loads, k)
                a = _dot_nt(n_bf, win_v[c0 : c0 + cw, :])
                b = _dot_nt(n_bf, win_v[DFF + c0 : DFF + c0 + cw, :])
                a_ref[:, c0 : c0 + cw] = a.astype(BF16)
                b_ref[:, c0 : c0 + cw] = b.astype(BF16)
                s = a * _sigmoid(a) * b
                acc = acc + _dot(s.astype(BF16), wout_v[c0 : c0 + cw, :])
            f_ref[...] = acc.astype(BF16)
            ho_ref[...] = h + (0.5 * gate) * acc

        _ffn_steps(i, n_active, _ffn_weight_copies(win_hbm, wout_hbm, win_v, wout_v, sem), compute)

        @pl.when(i >= n_active)
        def _():
            ho_ref[...] = h_ref[...]
            a_ref[...] = jnp.zeros_like(a_ref)
            b_ref[...] = jnp.zeros_like(b_ref)
            f_ref[...] = jnp.zeros_like(f_ref)

    return _call(
        body,
        name=name,
        grid=(n_tiles,),
        in_specs=[_rows(TM, D), _mod_spec(n_lat), _full((8, D)), ANY, ANY],
        out_specs=[_rows(TM, D), _rows(TM, DFF), _rows(TM, DFF), _rows(TM, D)],
        out_shape=[_sds((R, D), F32), _sds((R, DFF), BF16), _sds((R, DFF), BF16), _sds((R, D), BF16)],
        scratch_shapes=[pltpu.VMEM((2 * DFF, D), BF16), pltpu.VMEM((DFF, D), BF16),
                        pltpu.SemaphoreType.DMA((3 * len(FFN_CHUNKS),))],
        args=(h, modv, gvec, win, wout),
        carry=carry,
    )


def _ffn_bwd(h, dho, a, b, f, modv, gvec, win, wout, *, T, mrow, grow, ctx_active, name, carry=None):
    R = h.shape[0]
    n_lat, n_tiles = T // TM, R // TM
    n_active = n_tiles if ctx_active else n_lat

    def body(h_ref, dho_ref, a_ref, b_ref, f_ref, mod_ref, g_ref, win_hbm, wout_hbm,
             dh_ref, dab_ref, s_ref, n_ref, df_ref, part_ref, win_v, wout_v, sem):
        i = pl.program_id(0)
        first = jnp.logical_or(i == 0, i == n_lat)

        def compute(loads):
            h = h_ref[...]
            dho = dho_ref[...]
            shift, scale, gate = (mod_ref[0, mrow + k : mrow + k + 1, :] for k in range(3))
            g = g_ref[grow : grow + 1, :]
            r, xhat, y, n = _norm_mod(h, g, shift, scale)
            dgate = 0.5 * jnp.sum(dho * f_ref[...].astype(F32), axis=0, keepdims=True)
            df_bf = ((0.5 * gate) * dho).astype(BF16)
            df_ref[...] = df_bf
            n_ref[...] = n.astype(BF16)
            dn = jnp.zeros((TM, D), F32)
            for k, (c0, cw) in enumerate(FFN_CHUNKS):
                _wait_chunk(loads, k)
                ds = _dot_nt(df_bf, wout_v[c0 : c0 + cw, :])
                av = a_ref[:, c0 : c0 + cw].astype(F32)
                bv = b_ref[:, c0 : c0 + cw].astype(F32)
                sig = _sigmoid(av)
                sa = av * sig
                s_ref[:, c0 : c0 + cw] = (sa * bv).astype(BF16)
                da = (ds * bv * (sig * (1.0 + av * (1.0 - sig)))).astype(BF16)
                db = (ds * sa).astype(BF16)
                dab_ref[:, c0 : c0 + cw] = da
                dab_ref[:, DFF + c0 : DFF + c0 + cw] = db
                dn = dn + _dot(da, win_v[c0 : c0 + cw, :]) + _dot(db, win_v[DFF + c0 : DFF + c0 + cw, :])
            dh, dshift, dscale, dg = _norm_mod_bwd(dn, r, xhat, y, g, scale)
            dh_ref[...] = dho + dh
            _acc_partials(part_ref, first, {0: dshift, 1: dscale, 2: dgate, 3: dg})

        _ffn_steps(i, n_active, _ffn_weight_copies(win_hbm, wout_hbm, win_v, wout_v, sem), compute)

        @pl.when(i >= n_active)
        def _():
            dh_ref[...] = dho_ref[...]
            dab_ref[...] = jnp.zeros_like(dab_ref)
            s_ref[...] = jnp.zeros_like(s_ref)
            n_ref[...] = jnp.zeros_like(n_ref)
            df_ref[...] = jnp.zeros_like(df_ref)
            part_ref[...] = jnp.zeros_like(part_ref)

    return _call(
        body,
        name=name,
        grid=(n_tiles,),
        in_specs=[_rows(TM, D), _rows(TM, D), _rows(TM, DFF), _rows(TM, DFF), _rows(TM, D),
                  _mod_spec(n_lat), _full((8, D)), ANY, ANY],
        out_specs=[_rows(TM, D), _rows(TM, 2 * DFF), _rows(TM, DFF), _rows(TM, D), _rows(TM, D), _part_spec(n_lat)],
        out_shape=[_sds((R, D), F32), _sds((R, 2 * DFF), BF16), _sds((R, DFF), BF16), _sds((R, D), BF16),
                   _sds((R, D), BF16), _sds((2, 8, D), F32)],
        scratch_shapes=[pltpu.VMEM((2 * DFF, D), BF16), pltpu.VMEM((DFF, D), BF16),
                        pltpu.SemaphoreType.DMA((3 * len(FFN_CHUNKS),))],
        args=(h, dho, a, b, f, modv, gvec, win, wout),
        carry=carry,
    )


def _wgrad(x, y, *, bk, sh, name, carry=None):
    R, kx = x.shape
    n = y.shape[1]
    tr = R // 2
    nr, nsh = R // tr, bk // sh

    def body(x_ref, y_ref, o_ref, acc):
        r = pl.program_id(1)

        @pl.when(r == 0)
        def _():
            acc[...] = jnp.zeros_like(acc)

        acc[...] += _dot_tn(x_ref[...], y_ref[...])

        @pl.when(r == nr - 1)
        def _():
            for s in range(nsh):
                o_ref[s] = acc[s * sh : (s + 1) * sh, :].astype(BF16)

    (out,), got = _call(
        body,
        name=name,
        grid=(kx // bk, nr),
        in_specs=[pl.BlockSpec((tr, bk), lambda k, r: (r, k)), pl.BlockSpec((tr, n), lambda k, r: (r, 0))],
        out_specs=[pl.BlockSpec((nsh, sh, n), lambda k, r: (k, 0, 0))],
        out_shape=[_sds((kx // sh, sh, n), BF16)],
        scratch_shapes=[pltpu.VMEM((bk, n), F32)],
        args=(x, y),
        carry=carry,
    )
    return out, got


def _rot_half(x):
    lane = lax.broadcasted_iota(jnp.int32, x.shape, 1)
    return jnp.where((lane & (HD - 1)) < HD // 2, -pltpu.roll(x, 128 - HD // 2, 1), pltpu.roll(x, HD // 2, 1))


def _tile_sel():
    i = lax.broadcasted_iota(jnp.int32, (KVW, AW), 0)
    j = lax.broadcasted_iota(jnp.int32, (KVW, AW), 1)
    return jnp.where(i == (j // 256) * HD + (j & (HD - 1)), 1.0, 0.0).astype(BF16)


def _mixproj_fwd(h, modv, gvec, win, cos, sin, *, T, name, carry=None):
    R = h.shape[0]
    n_lat, n_tiles = T // TM, R // TM

    def body(h_ref, mod_ref, g_ref, win_ref, cos_ref, sin_ref, u_ref, q_ref, k4_ref, v4_ref):
        shift, scale = mod_ref[0, 3:4, :], mod_ref[0, 4:5, :]
        _, _, _, n = _norm_mod(h_ref[...], g_ref[1:2, :], shift, scale)
        proj = _dot_nt(n.astype(BF16), win_ref[...])
        u_ref[...] = proj[:, :PW]
        cs, sn = cos_ref[...], sin_ref[...]
        for s in range(AW // 128):
            x = proj[:, PW + 128 * s : PW + 128 * (s + 1)]
            q_ref[:, 128 * s : 128 * (s + 1)] = ((x * cs + _rot_half(x) * sn) * (HD ** -0.5)).astype(BF16)
        k = proj[:, PW + AW : PW + AW + KVW]
        k = (k * cs + _rot_half(k) * sn).astype(BF16)
        v = proj[:, PW + AW + KVW :].astype(BF16)
        sel = _tile_sel()
        k4_ref[...] = _dot(k, sel).astype(BF16)
        v4_ref[...] = _dot(v, sel).astype(BF16)

    return _call(
        body,
        name=name,
        grid=(n_tiles,),
        in_specs=[_rows(TM, D), _mod_spec(n_lat), _full((8, D)), _full((PROJ, D)), _rows(TM, 128), _rows(TM, 128)],
        out_specs=[_rows(TM, PW), _rows(TM, AW), _rows(TM, AW), _rows(TM, AW)],
        out_shape=[_sds((R, PW), F32), _sds((R, AW), BF16), _sds((R, AW), BF16), _sds((R, AW), BF16)],
        scratch_shapes=[],
        args=(h, modv, gvec, win, cos, sin),
        carry=carry,
    )


def _win_start(j, hi):
    return pl.multiple_of(jnp.clip((j - 1) * BLK, 0, hi - 3 * BLK), BLK)


def _hi_lo(x):
    hi = x.astype(BF16)
    return hi, (x - hi.astype(F32)).astype(BF16)


def _pool_bounds(t, w, T, R):
    is_ctx = t >= T
    lo = jnp.maximum(t - w // 2, jnp.where(is_ctx, T, 0))
    hi = jnp.minimum(t + w // 2, jnp.where(is_ctx, R, T))
    return lo, hi


def _pooled(u_v, j, T, R):
    start = _win_start(j, R)
    u3_hi, u3_lo = _hi_lo(u_v[pl.ds(start, 3 * BLK), :])
    ub = u_v[pl.ds(pl.multiple_of(j * BLK, BLK), BLK), :]
    t = j * BLK + lax.broadcasted_iota(jnp.int32, (BLK, 1), 0)
    pos = start + lax.broadcasted_iota(jnp.int32, (1, 3 * BLK), 1)
    pooled, counts = [], []
    for g, w in enumerate(POOL_WINDOWS):
        lo, hi = _pool_bounds(t, w, T, R)
        band = jnp.where(pos >= lo, jnp.where(pos < hi, 1.0, 0.0), 0.0).astype(BF16)
        sl = slice(g * 128, (g + 1) * 128)
        sums = _dot(band, u3_hi[:, sl]) + _dot(band, u3_lo[:, sl])
        cnt = (hi - lo).astype(F32)
        pooled.append(sums / cnt - ub[:, sl])
        counts.append(cnt)
    return pooled, counts


def _stack_heads(x):
    lane_h = lax.broadcasted_iota(jnp.int32, x.shape, 1) // HD
    return jnp.concatenate([jnp.where(lane_h == h, x, jnp.zeros_like(x)) for h in range(4)], axis=0)


def _unstack_heads(x):
    lane_h = lax.broadcasted_iota(jnp.int32, (BLK, 256), 1) // HD
    out = jnp.zeros((BLK, 256), F32)
    for h in range(4):
        out = out + jnp.where(lane_h == h, x[h * BLK : (h + 1) * BLK, :], 0.0)
    return out


def _window_mask(j, start_l, nbl):
    rowi = lax.broadcasted_iota(jnp.int32, (4 * BLK, 1), 0)
    qpos = j * BLK + (rowi & (BLK - 1))
    kpos = start_l + lax.broadcasted_iota(jnp.int32, (1, 3 * BLK), 1)
    reach = jnp.where(j < nbl, BLK, -1)
    return jnp.abs(kpos - qpos) <= reach


def _attn_exps(qs, kl, kc, sink_ref, g, valid):
    s_l = jnp.where(valid, _dot_nt(qs, kl), NEG)
    s_c = _dot_nt(qs, kc)
    rb = lax.broadcasted_iota(jnp.int32, (4 * BLK, 1), 0) // BLK
    sk = jnp.where(rb == 0, sink_ref[4 * g], jnp.where(rb == 1, sink_ref[4 * g + 1],
                   jnp.where(rb == 2, sink_ref[4 * g + 2], sink_ref[4 * g + 3])))
    m = jnp.maximum(jnp.maximum(jnp.max(s_l, axis=1, keepdims=True), jnp.max(s_c, axis=1, keepdims=True)), sk)
    e_l, e_c, e_s = jnp.exp(s_l - m), jnp.exp(s_c - m), jnp.exp(sk - m)
    inv = 1.0 / (jnp.sum(e_l, axis=1, keepdims=True) + jnp.sum(e_c, axis=1, keepdims=True) + e_s)
    return e_l, e_c, e_s, inv


def _attnpool_fwd(u, q, k4, v4, sink, w_pool, pool_scale, *, T, name, carry=None):
    R = u.shape[0]
    nb, nbl = R // BLK, T // BLK

    def body(q_ref, sink_ref, wp_ref, ps_ref, u_hbm, k4_hbm, v4_hbm, cat_ref, u_v, k4_v, v4_v, sem):
        j = pl.program_id(0)

        @pl.when(j == 0)
        def _():
            _load_weights([(u_hbm, u_v), (k4_hbm, k4_v), (v4_hbm, v4_v)], sem)

        pooled, _ = _pooled(u_v, j, T, R)
        for g in range(4):
            mixed = _dot(pooled[g].astype(BF16), wp_ref[g].astype(BF16)) * ps_ref[:, g * 128 : (g + 1) * 128]
            cat_ref[:, g * 128 : (g + 1) * 128] = mixed.astype(BF16)

        start_l = _win_start(j, T)
        valid = _window_mask(j, start_l, nbl)
        for g in range(2):
            gl = slice(g * 256, (g + 1) * 256)
            qs = _stack_heads(q_ref[:, gl])
            e_l, e_c, _, inv = _attn_exps(qs, k4_v[pl.ds(start_l, 3 * BLK), gl], k4_v[T:R, gl], sink_ref, g, valid)
            o = _dot(e_l.astype(BF16), v4_v[pl.ds(start_l, 3 * BLK), gl]) + _dot(e_c.astype(BF16), v4_v[T:R, gl])
            cat_ref[:, PW + g * 256 : PW + (g + 1) * 256] = _unstack_heads(o * inv).astype(BF16)

    return _call(
        body,
        name=name,
        grid=(nb,),
        in_specs=[_rows(BLK, AW), SMEM, _full((4, 128, 128)), _full((1, PW)), ANY, ANY, ANY],
        out_specs=[_rows(BLK, D)],
        out_shape=[_sds((R, D), BF16)],
        scratch_shapes=[pltpu.VMEM((R, PW), F32), pltpu.VMEM((R, AW), BF16), pltpu.VMEM((R, AW), BF16),
                        pltpu.SemaphoreType.DMA((3,))],
        args=(q, sink, w_pool, pool_scale, u, k4, v4),
        carry=carry,
    )


def _mixout_fwd(h, cat, modv, wout, *, T, ctx_active, name, carry=None):
    R = h.shape[0]
    n_lat, n_tiles = T // TM, R // TM

    def body(h_ref, cat_ref, mod_ref, w_ref, ho_ref, mo_ref):
        i = pl.program_id(0)

        def compute():
            mo = _dot(cat_ref[...], w_ref[...])
            mo_ref[...] = mo.astype(BF16)
            ho_ref[...] = h_ref[...] + mod_ref[0, 5:6, :] * mo

        if ctx_active:
            compute()
        else:
            pl.when(i < n_lat)(compute)

            @pl.when(i >= n_lat)
            def _():
                ho_ref[...] = h_ref[...]
                mo_ref[...] = jnp.zeros_like(mo_ref)

    return _call(
        body,
        name=name,
        grid=(n_tiles,),
        in_specs=[_rows(TM, D), _rows(TM, D), _mod_spec(n_lat), _full((D, D))],
        out_specs=[_rows(TM, D), _rows(TM, D)],
        out_shape=[_sds((R, D), F32), _sds((R, D), BF16)],
        scratch_shapes=[],
        args=(h, cat, modv, wout),
        carry=carry,
    )


def _mixout_bwd(dho, mo, modv, wout, *, T, ctx_active, name, carry=None):
    R = dho.shape[0]
    n_lat, n_tiles = T // TM, R // TM

    def body(dho_ref, mo_ref, mod_ref, w_ref, dcat_ref, dmix_ref, part_ref):
        i = pl.program_id(0)
        first = jnp.logical_or(i == 0, i == n_lat)

        def compute():
            dho = dho_ref[...]
            dmix = (mod_ref[0, 5:6, :] * dho).astype(BF16)
            dmix_ref[...] = dmix
            dcat_ref[...] = _dot_nt(dmix, w_ref[...])
            dgate = jnp.sum(dho * mo_ref[...].astype(F32), axis=0, keepdims=True)
            _acc_partials(part_ref, first, {2: dgate})

        if ctx_active:
            compute()
        else:
            pl.when(i < n_lat)(compute)

            @pl.when(i >= n_lat)
            def _():
                dcat_ref[...] = jnp.zeros_like(dcat_ref)
                dmix_ref[...] = jnp.zeros_like(dmix_ref)
                part_ref[...] = jnp.zeros_like(part_ref)

    return _call(
        body,
        name=name,
        grid=(n_tiles,),
        in_specs=[_rows(TM, D), _rows(TM, D), _mod_spec(n_lat), _full((D, D))],
        out_specs=[_rows(TM, D), _rows(TM, D), _part_spec(n_lat)],
        out_shape=[_sds((R, D), F32), _sds((R, D), BF16), _sds((2, 8, D), F32)],
        scratch_shapes=[],
        args=(dho, mo, modv, wout),
        carry=carry,
    )


def _pool_bwd(u, dcat, w_pool, pool_scale, *, T, name):
    R = u.shape[0]
    nb = R // BLK

    def body(dcat_ref, wp_ref, ps_ref, u_hbm, dps_ref, dwp_ref, dsc_ref, u_v, sem):
        j = pl.program_id(0)

        @pl.when(j == 0)
        def _():
            _load_weights([(u_hbm, u_v)], sem)
            dwp_ref[...] = jnp.zeros_like(dwp_ref)
            dsc_ref[...] = jnp.zeros_like(dsc_ref)

        pooled, counts = _pooled(u_v, j, T, R)
        for g in range(4):
            sl = slice(g * 128, (g + 1) * 128)
            p_bf = pooled[g].astype(BF16)
            w_bf = wp_ref[g].astype(BF16)
            dmixed = dcat_ref[:, sl]
            dsc_ref[0:1, sl] += jnp.sum(dmixed * _dot(p_bf, w_bf), axis=0, keepdims=True)
            dmp = (dmixed * ps_ref[:, sl]).astype(BF16)
            dwp_ref[sl, :] += _dot_tn(p_bf, dmp)
            dps_ref[:, sl] = _dot_nt(dmp, w_bf) / counts[g]

    return pl.pallas_call(
        body,
        name=name,
        grid=(nb,),
        in_specs=[_rows(BLK, D), _full((4, 128, 128)), _full((1, PW)), ANY],
        out_specs=[_rows(BLK, PW), _full((PW, 128)), _full((8, PW))],
        out_shape=[_sds((R, PW), F32), _sds((PW, 128), F32), _sds((8, PW), F32)],
        scratch_shapes=[pltpu.VMEM((R, PW), F32), pltpu.SemaphoreType.DMA((1,))],
        compiler_params=_params(),
    )(dcat, w_pool, pool_scale, u)


def _fold_heads(x):
    y = x[:, :128] + x[:, 128:]
    return y + pltpu.roll(y, HD, 1)


def _attn_bwd(q, k4, v4, dcat, dps, sink, *, T, name, carry=None):
    R = q.shape[0]
    nb, nbl = R // BLK, T // BLK

    def body(q_ref, dcat_ref, sink_ref, k4_hbm, v4_hbm, dps_hbm, du_ref, dq_ref, dk_ref, dv_ref, dsk_ref,
             k4_v, v4_v, dps_v, sem):
        j = pl.program_id(0)

        @pl.when(j == 0)
        def _():
            _load_weights([(k4_hbm, k4_v), (v4_hbm, v4_v), (dps_hbm, dps_v)], sem)
            dk_ref[...] = jnp.zeros_like(dk_ref)
            dv_ref[...] = jnp.zeros_like(dv_ref)
            dsk_ref[...] = jnp.zeros_like(dsk_ref)

        start = _win_start(j, R)
        d3_hi, d3_lo = _hi_lo(dps_v[pl.ds(start, 3 * BLK), :])
        db = dps_v[pl.ds(pl.multiple_of(j * BLK, BLK), BLK), :]
        pos = j * BLK + lax.broadcasted_iota(jnp.int32, (BLK, 1), 0)
        t_r = start + lax.broadcasted_iota(jnp.int32, (1, 3 * BLK), 1)
        for g, w in enumerate(POOL_WINDOWS):
            sl = slice(g * 128, (g + 1) * 128)
            lo_r, hi_r = _pool_bounds(t_r, w, T, R)
            band_t = jnp.where(pos >= lo_r, jnp.where(pos < hi_r, 1.0, 0.0), 0.0).astype(BF16)
            lo_c, hi_c = _pool_bounds(pos, w, T, R)
            du_ref[:, sl] = _dot(band_t, d3_hi[:, sl]) + _dot(band_t, d3_lo[:, sl]) - db[:, sl] * (hi_c - lo_c).astype(F32)

        start_l = _win_start(j, T)
        valid = _window_mask(j, start_l, nbl)
        rb = lax.broadcasted_iota(jnp.int32, (4 * BLK, 1), 0) // BLK
        lane = lax.broadcasted_iota(jnp.int32, (1, 128), 1)
        dk_l, dk_c, dv_l, dv_c = [], [], [], []
        for g in range(2):
            gl = slice(g * 256, (g + 1) * 256)
            qs = _stack_heads(q_ref[:, gl])
            kl, kc = k4_v[pl.ds(start_l, 3 * BLK), gl], k4_v[T:R, gl]
            vl, vc = v4_v[pl.ds(start_l, 3 * BLK), gl], v4_v[T:R, gl]
            e_l, e_c, e_s, inv = _attn_exps(qs, kl, kc, sink_ref, g, valid)
            p_l, p_c, p_s = e_l * inv, e_c * inv, e_s * inv
            dos = _stack_heads(dcat_ref[:, PW + g * 256 : PW + (g + 1) * 256]).astype(BF16)
            dp_l, dp_c = _dot_nt(dos, vl), _dot_nt(dos, vc)
            delta = jnp.sum(p_l * dp_l, axis=1, keepdims=True) + jnp.sum(p_c * dp_c, axis=1, keepdims=True)
            ds_l = (p_l * (dp_l - delta)).astype(BF16)
            ds_c = (p_c * (dp_c - delta)).astype(BF16)
            dq_ref[:, gl] = _unstack_heads(_dot(ds_l, kl) + _dot(ds_c, kc)) * (HD ** -0.5)
            dk_l.append(_fold_heads(_dot_tn(ds_l, qs)))
            dk_c.append(_fold_heads(_dot_tn(ds_c, qs)))
            dv_l.append(_fold_heads(_dot_tn(p_l.astype(BF16), dos)))
            dv_c.append(_fold_heads(_dot_tn(p_c.astype(BF16), dos)))
            dsink = -p_s * delta
            for h in range(4):
                tot = jnp.sum(jnp.where(rb == h, dsink, 0.0), axis=0, keepdims=True)
                dsk_ref[4 * g + h : 4 * g + h + 1, :] += jnp.broadcast_to(tot, (1, 128))
        first = lane < HD
        dk_ref[pl.ds(start_l, 3 * BLK), :] += jnp.where(first, dk_l[0], dk_l[1])
        dk_ref[T:R, :] += jnp.where(first, dk_c[0], dk_c[1])
        dv_ref[pl.ds(start_l, 3 * BLK), :] += jnp.where(first, dv_l[0], dv_l[1])
        dv_ref[T:R, :] += jnp.where(first, dv_c[0], dv_c[1])

    return _call(
        body,
        name=name,
        grid=(nb,),
        in_specs=[_rows(BLK, AW), _rows(BLK, D), SMEM, ANY, ANY, ANY],
        out_specs=[_rows(BLK, PW), _rows(BLK, AW), _full((R, KVW)), _full((R, KVW)), _full((8, 128))],
        out_shape=[_sds((R, PW), F32), _sds((R, AW), F32), _sds((R, KVW), F32), _sds((R, KVW), F32),
                   _sds((8, 128), F32)],
        scratch_shapes=[pltpu.VMEM((R, AW), BF16), pltpu.VMEM((R, AW), BF16), pltpu.VMEM((R, PW), F32),
                        pltpu.SemaphoreType.DMA((3,))],
        args=(q, dcat, sink, k4, v4, dps),
        carry=carry,
    )


def _mixproj_bwd(h, dho, du, dq, dk, dv, modv, gvec, win, cos, sin, *, T, name):
    R = h.shape[0]
    n_lat, n_tiles = T // TM, R // TM

    def body(h_ref, dho_ref, du_ref, dq_ref, dk_ref, dv_ref, mod_ref, g_ref, win_ref, cos_ref, sin_ref,
             dh_ref, dproj_ref, n_ref, part_ref):
        i = pl.program_id(0)
        first = jnp.logical_or(i == 0, i == n_lat)
        shift, scale = mod_ref[0, 3:4, :], mod_ref[0, 4:5, :]
        g = g_ref[1:2, :]
        r, xhat, y, n = _norm_mod(h_ref[...], g, shift, scale)
        n_ref[...] = n.astype(BF16)
        cs, sn = cos_ref[...], sin_ref[...]
        dproj_ref[:, :PW] = du_ref[...].astype(BF16)
        for s in range(AW // 128):
            x = dq_ref[:, 128 * s : 128 * (s + 1)]
            dproj_ref[:, PW + 128 * s : PW + 128 * (s + 1)] = (x * cs - _rot_half(x) * sn).astype(BF16)
        x = dk_ref[...]
        dproj_ref[:, PW + AW : PW + AW + KVW] = (x * cs - _rot_half(x) * sn).astype(BF16)
        dproj_ref[:, PW + AW + KVW :] = dv_ref[...].astype(BF16)
        dn = _dot(dproj_ref[...], win_ref[...])
        dh, dshift, dscale, dg = _norm_mod_bwd(dn, r, xhat, y, g, scale)
        dh_ref[...] = dho_ref[...] + dh
        _acc_partials(part_ref, first, {0: dshift, 1: dscale, 3: dg})

    return pl.pallas_call(
        body,
        name=name,
        grid=(n_tiles,),
        in_specs=[_rows(TM, D), _rows(TM, D), _rows(TM, PW), _rows(TM, AW), _rows(TM, KVW), _rows(TM, KVW),
                  _mod_spec(n_lat), _full((8, D)), _full((PROJ, D)), _rows(TM, 128), _rows(TM, 128)],
        out_specs=[_rows(TM, D), _rows(TM, PROJ), _rows(TM, D), _part_spec(n_lat)],
        out_shape=[_sds((R, D), F32), _sds((R, PROJ), BF16), _sds((R, D), BF16), _sds((2, 8, D), F32)],
        compiler_params=_params(),
    )(h, dho, du, dq, dk, dv, modv, gvec, win, cos, sin)


def _loss_head(h, target, g_final, *, T, name):
    R = h.shape[0]
    n_lat, n_tiles = T // TM, R // TM

    def body(h_ref, t_ref, g_ref, dh_ref, loss_ref, dg_ref):
        i = pl.program_id(0)

        @pl.when(i == 0)
        def _():
            loss_ref[...] = jnp.zeros_like(loss_ref)
            dg_ref[...] = jnp.zeros_like(dg_ref)

        @pl.when(i < n_lat)
        def _():
            h = h_ref[...]
            g = g_ref[...]
            r = lax.rsqrt(jnp.mean(h * h, axis=-1, keepdims=True) + EPS)
            xhat = h * r
            err = xhat * g - t_ref[...]
            tot = jnp.sum(jnp.sum(err * err, axis=1, keepdims=True), axis=0, keepdims=True)
            loss_ref[...] += jnp.broadcast_to(tot * (0.5 / D), loss_ref.shape)
            dy = err * (1.0 / D)
            dg_ref[0:1, :] += jnp.sum(dy * xhat, axis=0, keepdims=True)
            dxh = dy * g
            dh_ref[...] = r * (dxh - xhat * jnp.mean(dxh * xhat, axis=-1, keepdims=True))

        @pl.when(i >= n_lat)
        def _():
            dh_ref[...] = jnp.zeros_like(dh_ref)

    return pl.pallas_call(
        body,
        name=name,
        grid=(n_tiles,),
        in_specs=[_rows(TM, D), pl.BlockSpec((TM, D), lambda i: (jnp.minimum(i, n_lat - 1), 0)), _full((1, D))],
        out_specs=[_rows(TM, D), _full((8, 128)), _full((8, D))],
        out_shape=[_sds((R, D), F32), _sds((8, 128), F32), _sds((8, D), F32)],
        compiler_params=_params(),
    )(h, target, g_final)


def _mod_fwd(c16, w_mod, b_cols, *, name):
    nl, _, cols = w_mod.shape

    def body(c_ref, w_ref, b_ref, o_ref):
        c = c_ref[...]
        sc = (c * _sigmoid(c)).astype(BF16)
        o_ref[0] = _dot(sc, w_ref[0].astype(BF16)) + b_ref[0]

    return pl.pallas_call(
        body,
        name=name,
        grid=(nl,),
        in_specs=[_full((16, D)), pl.BlockSpec((1, D, cols), lambda l: (l, 0, 0)),
                  pl.BlockSpec((1, 1, cols), lambda l: (l, 0, 0))],
        out_specs=pl.BlockSpec((1, 16, cols), lambda l: (l, 0, 0)),
        out_shape=_sds((nl, 16, cols), F32),
        compiler_params=_params(),
    )(c16, w_mod, b_cols)


def _mod_bwd(c16, dm_cols, w_mod, *, name):
    nl, _, cols = w_mod.shape

    def body(c_ref, dm_ref, w_ref, gw_ref, dc_ref):
        c = c_ref[...]
        sc = (c * _sigmoid(c)).astype(BF16)
        dm = dm_ref[0].astype(BF16)
        gw_ref[0] = _dot_tn(sc, dm)
        dc_ref[0] = _dot_nt(dm, w_ref[0].astype(BF16))

    return pl.pallas_call(
        body,
        name=name,
        grid=(nl,),
        in_specs=[_full((16, D)), pl.BlockSpec((1, 16, cols), lambda l: (l, 0, 0)),
                  pl.BlockSpec((1, D, cols), lambda l: (l, 0, 0))],
        out_specs=[pl.BlockSpec((1, D, cols), lambda l: (l, 0, 0)), pl.BlockSpec((1, 16, D), lambda l: (l, 0, 0))],
        out_shape=[_sds((nl, D, cols), F32), _sds((nl, 16, D), F32)],
        compiler_params=_params(),
    )(c16, dm_cols, w_mod)


def _coords():
    return lax.axis_index("x"), lax.axis_index("y"), lax.axis_index("c")


FWD = 8


def _peer(k, x, y, c):
    if k == FWD:
        return (x ^ (1 - c), y ^ c, c)
    return (1 - x if k & 4 else x, 1 - y if k & 2 else y, 1 - c if k & 1 else c)


def _lin(p):
    return 4 * p[0] + 2 * p[1] + p[2]


def _view(ref, slot):
    return ref if slot is None else ref.at[slot]


class _Round:
    def __init__(self, ins, out_shapes, plan, local_plan=(), n_alias=0):
        self.ins, self.out_shapes = list(ins), list(out_shapes)
        self.plan, self.local_plan, self.n_alias = list(plan), list(local_plan), n_alias
        fed = {p[3] for p in self.plan if p[0] == FWD}
        self.feeders = [n for n, p in enumerate(self.plan) if p[0] in (2, 4, 6) and p[3] in fed]

    def sems(self):
        return [pltpu.SemaphoreType.DMA((len(self.plan),)), pltpu.SemaphoreType.DMA((len(self.plan),)),
                pltpu.SemaphoreType.DMA((max(len(self.local_plan), 1),))]

    def _remote(self, in_refs, out_refs, sems, incoming, pick):
        in_refs = list(out_refs[: self.n_alias]) + list(in_refs[self.n_alias :])
        x, y, c = _coords()
        me = _lin((x, y, c))
        copies = {}
        for idx, (k, ii, sfn, oi, dfn) in enumerate(self.plan):
            if not pick(idx, "d2d" if k == 1 else "fwd" if k == FWD else "ici"):
                continue
            peer = _peer(k, x, y, c)
            sender, receiver = (_lin(peer), me) if incoming else (me, _lin(peer))
            src = out_refs[oi] if ii is None else in_refs[ii]
            copies[idx] = pltpu.make_async_remote_copy(
                src_ref=_view(src, sfn(sender, receiver)), dst_ref=_view(out_refs[oi], dfn(sender, receiver)),
                send_sem=sems[0].at[idx], recv_sem=sems[1].at[idx], device_id=peer, device_id_type=MESH)
        return copies

    def _local(self, in_refs, out_refs, sems):
        in_refs = list(out_refs[: self.n_alias]) + list(in_refs[self.n_alias :])
        me = _lin(_coords())
        return [pltpu.make_async_copy(_view(in_refs[ii], sfn(me)), _view(out_refs[oi], dfn(me)), sems[2].at[idx])
                for idx, (ii, sfn, oi, dfn) in enumerate(self.local_plan)]

    def start(self, in_refs, out_refs, sems, links=("ici", "d2d")):
        for cp in self._remote(in_refs, out_refs, sems, False, lambda n, link: link in links).values():
            cp.start()
        if "ici" in links:
            for cp in self._local(in_refs, out_refs, sems):
                cp.start()

    def d2d(self):
        return [n for n, p in enumerate(self.plan) if p[0] == 1]

    def start_some(self, in_refs, out_refs, sems, which):
        for cp in self._remote(in_refs, out_refs, sems, False, lambda n, link: n in which).values():
            cp.start()

    def mid(self, in_refs, out_refs, sems):
        if self.feeders:
            for cp in self._remote(in_refs, out_refs, sems, True, lambda n, link: n in self.feeders).values():
                cp.wait_recv()
            for cp in self._remote(in_refs, out_refs, sems, False, lambda n, link: link == "fwd").values():
                cp.start()

    def finish(self, in_refs, out_refs, sems):
        for cp in self._remote(in_refs, out_refs, sems, True, lambda n, link: n not in self.feeders).values():
            cp.wait_recv()
        for cp in self._remote(in_refs, out_refs, sems, False, lambda n, link: True).values():
            cp.wait_send()
        for cp in self._local(in_refs, out_refs, sems):
            cp.wait()


def _exchange(name, rnd):
    n_in, n_out = len(rnd.ins), len(rnd.out_shapes)

    def body(*refs):
        in_refs, out_refs, sems = refs[:n_in], refs[n_in : n_in + n_out], refs[n_in + n_out :]
        rnd.start(in_refs, out_refs, sems)
        rnd.mid(in_refs, out_refs, sems)
        rnd.finish(in_refs, out_refs, sems)

    return pl.pallas_call(
        body, name=name, in_specs=[ANY] * n_in, out_specs=[ANY] * n_out, out_shape=rnd.out_shapes,
        scratch_shapes=rnd.sems(), input_output_aliases={i: i for i in range(rnd.n_alias)})(*rnd.ins)


def _call(body, *, name, grid, in_specs, out_specs, out_shape, scratch_shapes, args, carry=None):
    params = _params(len(grid))
    if carry is None:
        outs = pl.pallas_call(body, name=name, grid=grid, in_specs=in_specs, out_specs=out_specs, out_shape=out_shape,
                              scratch_shapes=scratch_shapes, compiler_params=params)(*args)
        return list(outs), []
    n_ci, n_co, n_cs = len(in_specs), len(out_shape), len(scratch_shapes)
    n_xi, n_xo = len(carry.ins), len(carry.out_shapes)

    def wrapped(*refs):
        ci, xi = refs[:n_ci], refs[n_ci : n_ci + n_xi]
        o0 = n_ci + n_xi
        co, xo = refs[o0 : o0 + n_co], refs[o0 + n_co : o0 + n_co + n_xo]
        s0 = o0 + n_co + n_xo
        cs, sems = refs[s0 : s0 + n_cs], refs[s0 + n_cs :]
        ids = [pl.program_id(a) for a in range(len(grid))]
        first = functools.reduce(jnp.logical_and, [i == 0 for i in ids])
        last = functools.reduce(jnp.logical_and, [i == g - 1 for i, g in zip(ids, grid)])

        @pl.when(first)
        def _():
            carry.start(xi, xo, sems, links=("ici",))

        step = functools.reduce(lambda acc, ig: acc * ig[1] + ig[0], zip(ids, grid), 0)
        n_steps = functools.reduce(lambda a, b: a * b, grid)
        if carry.feeders:

            @pl.when(step == min(n_steps - 1, (3 * n_steps) // 5))
            def _():
                carry.mid(xi, xo, sems)

        body(*ci, *co, *cs)

        when = {}
        for j, n in enumerate(carry.d2d()):
            when.setdefault(min(n_steps - 1, j % max(1, n_steps - 2)), []).append(n)
        for at, which in when.items():
            pl.when(step == at)(functools.partial(carry.start_some, xi, xo, sems, which))

        @pl.when(last)
        def _():
            carry.finish(xi, xo, sems)

    outs = pl.pallas_call(
        wrapped, name=name, grid=grid, in_specs=list(in_specs) + [ANY] * n_xi, out_specs=list(out_specs) + [ANY] * n_xo,
        out_shape=list(out_shape) + carry.out_shapes, scratch_shapes=list(scratch_shapes) + carry.sems(),
        input_output_aliases={n_ci + i: n_co + i for i in range(carry.n_alias)}, compiler_params=params,
    )(*args, *carry.ins)
    return list(outs[:n_co]), list(outs[n_co:])


def _gather_direct(arrays):
    na = len(arrays)
    outs = [_sds((NDEV,) + a.shape, a.dtype) for a in arrays]
    plan = [(k, i, lambda s, r: None, i, lambda s, r: s) for i in range(na) for k in range(1, NDEV)]
    return _Round(arrays, outs, plan, [(i, lambda m: None, i, lambda m: m) for i in range(na)])


def _gather_a(arrays):
    na = len(arrays)
    outs = [_sds((NDEV,) + a.shape, a.dtype) for a in arrays]
    plan = [(k, i, lambda s, r: None, i, lambda s, r: s) for i in range(na) for k in (2, 4)]
    handed = lambda s, r: s ^ (2 << (s & 1))
    plan += [(FWD, None, handed, i, handed) for i in range(na)]
    return _Round(arrays, outs, plan, [(i, lambda m: None, i, lambda m: m) for i in range(na)])


def _gather_b(got):
    na = len(got)
    plan = [(1, i, (lambda s, r, k=k: s ^ k), i, (lambda s, r, k=k: s ^ k)) for i in range(na) for k in (0, 2, 4, 6)]
    return _Round(got, [_sds(g.shape, g.dtype) for g in got], plan, n_alias=na)


def _scatter_1(grads):
    plan = [(1, i, (lambda s, r, q=q: 2 * q + (r & 1)), i, (lambda s, r, q=q: q))
            for i in range(len(grads)) for q in range(4)]
    return _Round(grads, [_sds((4,) + g.shape[1:], g.dtype) for g in grads], plan)


def _scatter_2(chip):
    plan = [(k, i, lambda s, r: r >> 1, i, (lambda s, r, j=j: j)) for i in range(len(chip)) for j, k in enumerate((2, 4, 6))]
    return _Round(chip, [_sds((3,) + g.shape[1:], g.dtype) for g in chip], plan)


def _add_pairs(grads, got, pos, *, name):
    n = len(grads)
    mine = lambda a: pl.BlockSpec((1,) + a.shape[1:], lambda q, p: (2 * q + p[0], 0, 0))
    slot = lambda a: pl.BlockSpec((1,) + a.shape[1:], lambda q, p: (q, 0, 0))

    def body(pos_ref, *refs):
        for g_ref, r_ref, o_ref in zip(refs[:n], refs[n : 2 * n], refs[2 * n :]):
            o_ref[...] = (g_ref[...].astype(F32) + r_ref[...].astype(F32)).astype(o_ref.dtype)

    return pl.pallas_call(
        body,
        name=name,
        grid_spec=pltpu.PrefetchScalarGridSpec(
            num_scalar_prefetch=1, grid=(4,),
            in_specs=[mine(g) for g in grads] + [slot(g) for g in grads],
            out_specs=[slot(g) for g in grads]),
        out_shape=[_sds((4,) + g.shape[1:], g.dtype) for g in grads],
        compiler_params=_params(),
    )(pos, *grads, *got)


def _sum_adamw(chip, got, pos, w, m, v, layer, prior, *, name):
    _, sh, wd = chip.shape
    nl, rows, cols = w.shape
    nb, blk = 2, (sh // 2, wd)
    part = lambda n: pl.BlockSpec((n, sh // 2, wd), lambda i, p: ((p[1] if n == 1 else 0), i, 0))
    mine = pl.BlockSpec(blk, lambda i, p: (layer * nb + i, 0))
    flat = lambda t: t.reshape(nl * rows, cols)
    n_prior = 0 if prior is None else 4

    def body(pos_ref, c_ref, r_ref, w_ref, m_ref, v_ref, *refs):
        g_ref, d_ref, m2_ref, v2_ref = refs[n_prior:]
        g = c_ref[0].astype(F32)
        for s in range(3):
            g = g + r_ref[s].astype(F32)
        g_ref[...] = g
        d_ref[...], m2_ref[...], v2_ref[...] = _adamw_math(w_ref[...], g, m_ref[...], v_ref[...])

    outs = pl.pallas_call(
        body,
        name=name,
        grid_spec=pltpu.PrefetchScalarGridSpec(
            num_scalar_prefetch=1, grid=(nb,),
            in_specs=[part(1), part(3), mine, mine, mine] + [ANY] * n_prior,
            out_specs=[mine] * 4),
        out_shape=[_sds((nl * rows, cols), F32)] * 4,
        input_output_aliases={6 + k: k for k in range(n_prior)},
        compiler_params=_params(),
    )(pos, chip, got, flat(w), flat(m), flat(v), *(flat(t) for t in prior or ()))
    return [o.reshape(w.shape) for o in outs]


def _adamw_math(w, g, m, v):
    m2 = ADAM_B1 * m + (1.0 - ADAM_B1) * g
    v2 = ADAM_B2 * v + (1.0 - ADAM_B2) * (g * g)
    m_hat = m2 / (1.0 - ADAM_B1 ** ADAM_STEP)
    v_hat = v2 / (1.0 - ADAM_B2 ** ADAM_STEP)
    delta = -ADAM_LR * (m_hat / (jnp.sqrt(v_hat) + ADAM_EPS) + ADAM_WD * w)
    return delta, m2, v2


def _adamw(w, g, m, v, *, name, carry=None):
    shape = w.shape
    flat = [t.reshape(-1, shape[-1]) for t in (w, g, m, v)]
    rows, cols = flat[0].shape
    tr = rows // 8 if rows % 64 == 0 else rows
    spec = _rows(tr, cols)

    def body(w_ref, g_ref, m_ref, v_ref, d_ref, m2_ref, v2_ref):
        d_ref[...], m2_ref[...], v2_ref[...] = _adamw_math(w_ref[...], g_ref[...], m_ref[...], v_ref[...])

    outs, got = _call(body, name=name, grid=(rows // tr,), in_specs=[spec] * 4, out_specs=[spec] * 3,
                      out_shape=[_sds((rows, cols), F32)] * 3, scratch_shapes=[], args=flat, carry=carry)
    return tuple(o.reshape(shape) for o in outs), got


def _adds(tag, grads, got, *, pos):
    return _add_pairs(list(grads), list(got)[: len(grads)], pos, name=f"rs_add_{tag}")


def _small_sums(packets, nf, dwp, dsc, dsk, *, name):
    flat = [p for layer in packets for p in layer]

    def total(ref, *idx):
        acc = ref[(0,) + idx]
        for dev in range(1, NDEV):
            acc = acc + ref[(dev,) + idx]
        return acc

    def body(*refs):
        pk = refs[:6]
        nf_ref, dwp0, dwp1, dsc0, dsc1, dsk0, dsk1 = refs[6:13]
        dm_ref, gb_ref, gn_ref, gnf_ref, gwp_ref, gps_ref, gsk_ref = refs[13:]
        dm_ref[...] = jnp.zeros_like(dm_ref)
        gn_ref[...] = jnp.zeros_like(gn_ref)
        for l in range(2):
            for sb in range(3):
                p = pk[3 * l + sb]
                for r in range(3):
                    col = slice((3 * sb + r) * D, (3 * sb + r + 1) * D)
                    lat = p[0, 0, r : r + 1, :]
                    dm_ref[l, 0:1, col] = lat
                    for dev in range(1, NDEV):
                        row = p[dev, 0, r : r + 1, :]
                        dm_ref[l, dev : dev + 1, col] = row
                        lat = lat + row
                    ctx = total(p, 1, slice(r, r + 1), slice(None))
                    dm_ref[l, 8:9, col] = ctx
                    gb_ref[l : l + 1, col] = lat + ctx
                gn_ref[l, sb : sb + 1, :] = total(p, 0, slice(3, 4), slice(None)) + total(p, 1, slice(3, 4), slice(None))
        gnf_ref[...] = total(nf_ref, slice(0, 1), slice(None))
        for l, (a, b, c) in enumerate(((dwp0, dsc0, dsk0), (dwp1, dsc1, dsk1))):
            gwp_ref[l] = total(a, slice(None), slice(None))
            gps_ref[l : l + 1, :] = total(b, slice(0, 1), slice(None))
            gsk_ref[l] = total(c, slice(None), slice(None))

    ins = flat + [nf, dwp[0], dwp[1], dsc[0], dsc[1], dsk[0], dsk[1]]
    return pl.pallas_call(
        body,
        name=name,
        out_shape=[_sds((2, 16, NMOD * D), F32), _sds((2, NMOD * D), F32), _sds((2, 8, D), F32), _sds((1, D), F32),
                   _sds((2, PW, 128), F32), _sds((2, PW), F32), _sds((2, 8, 128), F32)],
        compiler_params=pltpu.CompilerParams(vmem_limit_bytes=VMEM_LIMIT),
    )(*ins)


def _small_adamw(c_ctx, dc_all, triples, *, name):
    n = len(triples)

    def body(*refs):
        c_ref, dc_ref = refs[0], refs[1]
        ins = refs[2 : 2 + 4 * n - 1]
        outs = refs[2 + 4 * n - 1 :]
        acc = dc_ref[0, 0, 8:9, :] + dc_ref[0, 1, 8:9, :]
        for dev in range(1, NDEV):
            acc = acc + (dc_ref[dev, 0, 8:9, :] + dc_ref[dev, 1, 8:9, :])
        c = c_ref[...]
        sig = _sigmoid(c)
        g_c = acc * (sig * (1.0 + c * (1.0 - sig)))
        outs[0][...] = g_c
        pos = 0
        for k in range(n):
            if k == 0:
                w, g, m, v = ins[0][...], g_c, ins[1][...], ins[2][...]
                pos = 3
            else:
                w, g, m, v = (ins[pos + t][...] for t in range(4))
                pos += 4
            d, m2, v2 = _adamw_math(w, g, m, v)
            outs[1 + 3 * k][...], outs[2 + 3 * k][...], outs[3 + 3 * k][...] = d, m2, v2

    flat_in = [c_ctx, dc_all]
    out_shape = [_sds(c_ctx.shape, F32)]
    for k, (w, g, m, v) in enumerate(triples):
        flat_in += [w, m, v] if k == 0 else [w, g, m, v]
        out_shape += [_sds(w.shape, F32)] * 3
    return pl.pallas_call(body, name=name, out_shape=out_shape,
                          compiler_params=pltpu.CompilerParams(vmem_limit_bytes=VMEM_LIMIT))(*flat_in)


def _rope_tables(T, R):
    t = jnp.arange(T)
    inv = ROPE_BASE ** (-jnp.arange(0, HD // 2, 2, dtype=F32) / (HD // 2))
    ang = jnp.concatenate([(t // GRID_W).astype(F32)[:, None] * inv, (t % GRID_W).astype(F32)[:, None] * inv], axis=-1)
    cos = jnp.concatenate([jnp.tile(jnp.cos(ang), (1, 4)), jnp.ones((R - T, 128), F32)], axis=0)
    sin = jnp.concatenate([jnp.tile(jnp.sin(ang), (1, 4)), jnp.zeros((R - T, 128), F32)], axis=0)
    return cos, sin


def kernel(x, c, ctx, c_ctx, w_mod, b_mod, norm_ffn1, w_ffn1_in, w_ffn1_out, norm_mix, w_in, w_pool, pool_scale, sink, w_out, norm_ffn2, w_ffn2_in, w_ffn2_out, norm_final, loss_target, m_c_ctx, m_w_mod, m_b_mod, m_norm_ffn1, m_w_ffn1_in, m_w_ffn1_out, m_norm_mix, m_w_in, m_w_pool, m_pool_scale, m_sink, m_w_out, m_norm_ffn2, m_w_ffn2_in, m_w_ffn2_out, m_norm_final, v_c_ctx, v_w_mod, v_b_mod, v_norm_ffn1, v_w_ffn1_in, v_w_ffn1_out, v_norm_mix, v_w_in, v_w_pool, v_pool_scale, v_sink, v_w_out, v_norm_ffn2, v_w_ffn2_in, v_w_ffn2_out, v_norm_final):
    T = x.shape[1]
    R = T + LC
    nl = w_mod.shape[0]
    cx, cy, cc = _coords()
    me = _lin((cx, cy, cc))
    pos = jnp.stack([cc, 2 * cx + cy]).astype(jnp.int32)
    mcols = w_mod.shape[2]

    shards = [([w_ffn1_in[l].T.astype(BF16), w_ffn1_out[l].astype(BF16)],
               [w_in[l].T.astype(BF16), w_out[l].astype(BF16)],
               [w_ffn2_in[l].T.astype(BF16), w_ffn2_out[l].astype(BF16)]) for l in range(nl)]

    got = _exchange("ag_c_w", _merge(_gather_direct([c]), _gather_a(shards[0][0] + shards[0][1])))
    c_all, w_first = got[0], got[1:]
    c16 = jnp.concatenate([c_all.reshape(NDEV, D), c_ctx[None], jnp.zeros((16 - NDEV - 1, D), F32)], axis=0)
    b_cols = lax.dynamic_slice(b_mod, (0, me * mcols), (nl, mcols)).reshape(nl, 1, mcols)
    got = _exchange("ag_mod_w", _merge(_gather_b(w_first), _gather_direct([_mod_fwd(c16, w_mod, b_cols, name="mod_fwd")])))
    w_first, mod_all = got[:4], got[4]
    mod_all = jnp.transpose(mod_all, (1, 2, 0, 3)).reshape(nl, 16, NMOD, D)
    mine = lax.dynamic_index_in_dim(mod_all, me, axis=1, keepdims=False)
    pad = jnp.zeros((nl, 16 - NMOD, D), F32)
    modv = jnp.stack([jnp.concatenate([mine, pad], axis=1), jnp.concatenate([mod_all[:, 8], pad], axis=1)], axis=1)

    gvec = [jnp.concatenate([norm_ffn1[l][None], norm_mix[l][None], norm_ffn2[l][None], jnp.zeros((5, D), F32)], axis=0)
            for l in range(nl)]
    cos, sin = _rope_tables(T, R)
    ps2 = [pool_scale[l][None] for l in range(nl)]

    h = jnp.concatenate([x[0], ctx[0]], axis=0)
    loss_all, dh, small, nf_all, big, last_partial = _forward_backward(
        h, loss_target[0], modv, gvec, shards, w_first, cos, sin, sink, w_pool, ps2, norm_final, pos, T=T)
    loss = jnp.sum(loss_all[:, 0, 0])
    grad_x = dh[:T][None]

    dm, g_b_mod, g_norms, g_nf, g_wp, g_ps, g_sk = _small_sums(
        [small[l][0:3] for l in range(nl)], nf_all, *[[small[l][k] for l in range(nl)] for k in (3, 4, 5)],
        name="small_sums")
    dm_cols = lax.dynamic_slice(dm, (0, 0, me * mcols), (nl, 16, mcols))
    g_w_mod, dc_part = _mod_bwd(c16, dm_cols, w_mod, name="mod_bwd")
    got = _exchange("rs1_tail", _merge(_scatter_1([last_partial]), _gather_direct([dc_part])))
    (c1o,), dc_all = _adds("ffn1_out_0", [last_partial], got[:1], pos=pos), got[1]

    delta, new_m, new_v = {}, {}, {}
    (delta["w_mod"], new_m["w_mod"], new_v["w_mod"]), got = _adamw(
        w_mod, g_w_mod, m_w_mod, v_w_mod, name="adamw_w_mod", carry=_scatter_2([c1o]))
    big[0][1] = (c1o, got[0])

    grads = {
        "b_mod": g_b_mod, "norm_ffn1": g_norms[:, 0], "norm_mix": g_norms[:, 1], "norm_ffn2": g_norms[:, 2],
        "w_pool": g_wp.reshape(w_pool.shape), "pool_scale": g_ps, "sink": g_sk[:, :, 0], "norm_final": g_nf.reshape(D),
        "w_mod": g_w_mod,
    }
    weights = dict(c_ctx=c_ctx, w_mod=w_mod, b_mod=b_mod, norm_ffn1=norm_ffn1, w_ffn1_in=w_ffn1_in, w_ffn1_out=w_ffn1_out,
                   norm_mix=norm_mix, w_in=w_in, w_pool=w_pool, pool_scale=pool_scale, sink=sink, w_out=w_out,
                   norm_ffn2=norm_ffn2, w_ffn2_in=w_ffn2_in, w_ffn2_out=w_ffn2_out, norm_final=norm_final)
    moms = dict(c_ctx=(m_c_ctx, v_c_ctx), w_mod=(m_w_mod, v_w_mod), b_mod=(m_b_mod, v_b_mod),
                norm_ffn1=(m_norm_ffn1, v_norm_ffn1), w_ffn1_in=(m_w_ffn1_in, v_w_ffn1_in),
                w_ffn1_out=(m_w_ffn1_out, v_w_ffn1_out), norm_mix=(m_norm_mix, v_norm_mix), w_in=(m_w_in, v_w_in),
                w_pool=(m_w_pool, v_w_pool), pool_scale=(m_pool_scale, v_pool_scale), sink=(m_sink, v_sink),
                w_out=(m_w_out, v_w_out), norm_ffn2=(m_norm_ffn2, v_norm_ffn2), w_ffn2_in=(m_w_ffn2_in, v_w_ffn2_in),
                w_ffn2_out=(m_w_ffn2_out, v_w_ffn2_out), norm_final=(m_norm_final, v_norm_final))
    order = list(weights)
    small_names = ["c_ctx", "b_mod", "norm_ffn1", "norm_mix", "w_pool", "pool_scale", "sink", "norm_ffn2", "norm_final"]

    def as2d(name, t):
        if name == "w_pool":
            return t.reshape(-1, 128)
        return t.reshape(1, -1) if t.ndim == 1 else t

    triples = [(as2d(n, weights[n]), None if n == "c_ctx" else as2d(n, grads[n]), as2d(n, moms[n][0]), as2d(n, moms[n][1]))
               for n in small_names]
    outs = _small_adamw(as2d("c_ctx", c_ctx), dc_all, triples, name="small_adamw")
    grads["c_ctx"] = outs[0].reshape(c_ctx.shape)
    for k, n in enumerate(small_names):
        delta[n], new_m[n], new_v[n] = (o.reshape(weights[n].shape) for o in outs[1 + 3 * k : 4 + 3 * k])
    for k, n in enumerate(["w_ffn1_in", "w_ffn1_out", "w_in", "w_out", "w_ffn2_in", "w_ffn2_out"]):
        turn = (lambda t: jnp.swapaxes(t, 1, 2)) if k % 2 == 0 else (lambda t: t)
        wmv = [turn(t) for t in (weights[n], *moms[n])]
        outs = None
        for l in reversed(range(nl)):
            outs = _sum_adamw(*big[l][k], pos, *wmv, l, outs, name=f"adamw_{n}_{l}")
        grads[n], delta[n], new_m[n], new_v[n] = (turn(o) for o in outs)

    return (loss, grad_x, *[grads[n] for n in order], *[delta[n] for n in order],
            *[new_m[n] for n in order], *[new_v[n] for n in order])


def _merge(*rounds):
    ins, outs, plan, local, n_alias = [], [], [], [], 0
    for r in rounds:
        assert r.n_alias == 0 or (not ins and r.n_alias == len(r.ins) == len(r.out_shapes))
        oi, oo = len(ins), len(outs)
        plan += [(k, None if i is None else i + oi, sf, o + oo, df) for k, i, sf, o, df in r.plan]
        local += [(i + oi, sf, o + oo, df) for i, sf, o, df in r.local_plan]
        ins += r.ins
        outs += r.out_shapes
        n_alias += r.n_alias
    return _Round(ins, outs, plan, local, n_alias)


def _forward_backward(h, target, modv, gvec, shards, w_first, cos, sin, sink, w_pool, ps2, norm_final, pos, *, T):
    nl = len(gvec)
    flat = lambda ws: [w.reshape(-1, D) for w in ws]
    saved = []
    w1, wm = flat(w_first[:2]), flat(w_first[2:])
    for l in range(nl):
        last = l == nl - 1
        h0 = h
        if l == 0:
            (h1, a1, b1, f1), got = _ffn_fwd(h0, modv[l], gvec[l], *w1, T=T, mrow=0, grow=0, ctx_active=True,
                                             name=f"ffn1_fwd_{l}", carry=_gather_a(shards[l][2]))
            (u, q, k4, v4), got = _mixproj_fwd(h1, modv[l], gvec[l], wm[0], cos, sin, T=T, name=f"mixproj_fwd_{l}",
                                               carry=_gather_b(got))
            w2 = flat(got)
        else:
            (h1, a1, b1, f1), got = _ffn_fwd(h0, modv[l], gvec[l], *w1, T=T, mrow=0, grow=0, ctx_active=True,
                                             name=f"ffn1_fwd_{l}", carry=_gather_b(nxt_m + nxt_2))
            wm, w2 = flat(got[:2]), flat(got[2:])
            (u, q, k4, v4), _ = _mixproj_fwd(h1, modv[l], gvec[l], wm[0], cos, sin, T=T, name=f"mixproj_fwd_{l}")
        (cat,), nxt_1 = _attnpool_fwd(u, q, k4, v4, sink[l], w_pool[l], ps2[l], T=T, name=f"attnpool_fwd_{l}",
                                      carry=None if last else _gather_a(shards[l + 1][0]))
        (h2, mo), nxt_m = _mixout_fwd(h1, cat, modv[l], wm[1], T=T, ctx_active=not last, name=f"mixout_fwd_{l}",
                                      carry=None if last else _gather_a(shards[l + 1][1]))
        (h3, a2, b2, f2), got = _ffn_fwd(h2, modv[l], gvec[l], *w2, T=T, mrow=6, grow=2, ctx_active=not last,
                                         name=f"ffn2_fwd_{l}",
                                         carry=None if last else _merge(_gather_b(nxt_1), _gather_a(shards[l + 1][2])))
        saved.append((h0, a1, b1, f1, h1, u, q, k4, v4, cat, mo, h2, a2, b2, f2, w1, wm, w2))
        h = h3
        if not last:
            w1, nxt_2 = flat(got[:2]), got[2:]

    dh, loss_part, dnf = _loss_head(h, target, norm_final[None], T=T, name="loss_head")

    adds = functools.partial(_adds, pos=pos)
    small, big = [None] * nl, {}
    prev = None
    for l in reversed(range(nl)):
        last = l == nl - 1
        h0, a1, b1, f1, h1, u, q, k4, v4, cat, mo, h2, a2, b2, f2, w1, wm, w2 = saved[l]
        (dh, dab, s, n, df, pk2), got = _ffn_bwd(
            h2, dh, a2, b2, f2, modv[l], gvec[l], *w2, T=T, mrow=6, grow=2, ctx_active=not last, name=f"ffn2_bwd_{l}",
            carry=_merge(_scatter_1(prev[0]), _gather_a(prev[1])) if prev else None)
        if prev:
            c1, small_a = adds(f"ffn1_{l + 1}", prev[0], got[:2]), got[2:]
        g_w2i, got = _wgrad(dab, n, bk=WG_BK, sh=2 * DFF // NDEV, name=f"wgrad_ffn2_in_{l}",
                            carry=_scatter_2(c1[:1]) if prev else None)
        if prev:
            big[l + 1][0] = (c1[0], got[0])
        g_w2o, got = _wgrad(s, df, bk=WG_BK, sh=DFF // NDEV, name=f"wgrad_ffn2_out_{l}",
                            carry=_scatter_2(c1[1:]) if prev else None)
        if prev:
            big[l + 1][1] = (c1[1], got[0])
        rnd = _scatter_1([g_w2i, g_w2o])
        (dcat, dmix, pko), got = _mixout_bwd(dh, mo, modv[l], wm[1], T=T, ctx_active=not last, name=f"mixout_bwd_{l}",
                                             carry=_merge(_gather_b(small_a), rnd) if prev else rnd)
        if prev:
            small[l + 1], got = got[: len(small_a)], got[len(small_a) :]
        c2 = adds(f"ffn2_{l}", [g_w2i, g_w2o], got)
        g_wo, _ = _wgrad(cat, dmix, bk=D, sh=D // NDEV, name=f"wgrad_out_{l}")
        dps, dwp, dsc = _pool_bwd(u, dcat, w_pool[l], ps2[l], T=T, name=f"pool_bwd_{l}")
        (du, dq, dk, dv, dsk), got = _attn_bwd(q, k4, v4, dcat, dps, sink[l], T=T, name=f"attn_bwd_{l}", carry=_scatter_2(c2))
        big[l] = [None, None, None, None, (c2[0], got[0]), (c2[1], got[1])]
        dh, dproj, n, pkm = _mixproj_bwd(h1, dh, du, dq, dk, dv, modv[l], gvec[l], wm[0], cos, sin, T=T, name=f"mixproj_bwd_{l}")
        g_wi, _ = _wgrad(dproj, n, bk=PROJ, sh=PROJ // NDEV, name=f"wgrad_in_{l}")
        (dh, dab, s, n, df, pk1), got = _ffn_bwd(h0, dh, a1, b1, f1, modv[l], gvec[l], *w1, T=T, mrow=0, grow=0,
                                                 ctx_active=True, name=f"ffn1_bwd_{l}", carry=_scatter_1([g_wi, g_wo]))
        cm = adds(f"mix_{l}", [g_wi, g_wo], got)
        mine = [pk1, pkm + pko, pk2, dwp, dsc, dsk]
        rnd = _merge(_scatter_2(cm), _gather_a(mine + [dnf, loss_part])) if l == 0 else _scatter_2(cm)
        g_w1i, got = _wgrad(dab, n, bk=WG_BK, sh=2 * DFF // NDEV, name=f"wgrad_ffn1_in_{l}", carry=rnd)
        big[l][2:4] = [(cm[0], got[0]), (cm[1], got[1])]
        if l > 0:
            g_w1o, _ = _wgrad(s, df, bk=WG_BK, sh=DFF // NDEV, name=f"wgrad_ffn1_out_{l}")
            prev = ([g_w1i, g_w1o], mine)
    (c1i,) = adds("ffn1_in_0", [g_w1i], _exchange("rs1_ffn1_in_0", _scatter_1([g_w1i])))
    g_w1o, got = _wgrad(s, df, bk=WG_BK, sh=DFF // NDEV, name="wgrad_ffn1_out_0",
                        carry=_merge(_gather_b(got[2:]), _scatter_2([c1i])))
    small[0], nf_all, loss_all = got[:6], got[6], got[7]
    big[0][0] = (c1i, got[8])
    return loss_all, dh, small, nf_all, big, g_w1o
```

```python
import functools

import jax
import jax.numpy as jnp
from jax import lax
from jax.experimental import pallas as pl
from jax.experimental.pallas import tpu as pltpu

F32, BF16 = jnp.float32, jnp.bfloat16

D = 1024
LC = 256
DFF = 2816
NMOD = 9
PW = 512
AW = 512
KVW = 128
PROJ = PW + AW + 2 * KVW
HD = 64
BLK = 128
GRID_W = 64
POOL_WINDOWS = (2, 4, 8, 16)
EPS = 1e-6
NEG = -1e30
ROPE_BASE = 10000.0
NDEV = 8
MESH = pl.DeviceIdType.MESH

ADAM_LR, ADAM_B1, ADAM_B2, ADAM_EPS, ADAM_WD, ADAM_STEP = 0.001, 0.9, 0.999, 1e-08, 0.01, 10

VMEM_LIMIT = 56 * 1024 * 1024
TM = 256
FFN_CHUNKS = ((0, 1536), (1536, 1280))
WG_BK = 1408

ANY = pl.BlockSpec(memory_space=pl.ANY)
SMEM = pl.BlockSpec(memory_space=pltpu.SMEM)


def _params(ngrid=1):
    return pltpu.CompilerParams(dimension_semantics=("arbitrary",) * ngrid, vmem_limit_bytes=VMEM_LIMIT)


def _dot(a, b):
    return jnp.dot(a, b, preferred_element_type=F32)


def _dot_nt(a, b):
    return lax.dot_general(a, b, (((1,), (1,)), ((), ())), preferred_element_type=F32)


def _dot_tn(a, b):
    return lax.dot_general(a, b, (((0,), (0,)), ((), ())), preferred_element_type=F32)


def _sigmoid(x):
    return 1.0 / (1.0 + jnp.exp(-x))


def _rows(tm, w):
    return pl.BlockSpec((tm, w), lambda i: (i, 0))


def _full(shape):
    nd = len(shape)
    return pl.BlockSpec(shape, lambda *_: (0,) * nd)


def _sds(shape, dtype):
    return jax.ShapeDtypeStruct(shape, dtype)


def _norm_mod(h, g, shift, scale):
    r = lax.rsqrt(jnp.mean(h * h, axis=-1, keepdims=True) + EPS)
    xhat = h * r
    y = xhat * g
    return r, xhat, y, y * (1.0 + scale) + shift


def _norm_mod_bwd(dn, r, xhat, y, g, scale):
    dshift = jnp.sum(dn, axis=0, keepdims=True)
    dscale = jnp.sum(dn * y, axis=0, keepdims=True)
    dy = dn * (1.0 + scale)
    dg = jnp.sum(dy * xhat, axis=0, keepdims=True)
    dxh = dy * g
    dh = r * (dxh - xhat * jnp.mean(dxh * xhat, axis=-1, keepdims=True))
    return dh, dshift, dscale, dg


def _acc_partials(part_ref, first, rows):
    @pl.when(first)
    def _():
        part_ref[...] = jnp.zeros_like(part_ref)

    for r, val in rows.items():
        part_ref[0, r : r + 1, :] += val


def _mod_spec(n_lat):
    return pl.BlockSpec((1, 16, D), lambda i: (i // n_lat, 0, 0))


def _part_spec(n_lat):
    return pl.BlockSpec((1, 8, D), lambda i: (i // n_lat, 0, 0))


def _load_weights(pairs, sem):
    copies = [pltpu.make_async_copy(src, dst, sem.at[k]) for k, (src, dst) in enumerate(pairs)]
    for cp in copies:
        cp.start()
    for cp in copies:
        cp.wait()


def _ffn_weight_copies(win_hbm, wout_hbm, win_v, wout_v, sem):
    loads = []
    for k, (c0, cw) in enumerate(FFN_CHUNKS):
        slabs = [(win_hbm, win_v, c0), (win_hbm, win_v, DFF + c0), (wout_hbm, wout_v, c0)]
        loads.append([pltpu.make_async_copy(src.at[pl.ds(r0, cw)], dst.at[pl.ds(r0, cw)], sem.at[3 * k + j])
                      for j, (src, dst, r0) in enumerate(slabs)])
    return loads


def _ffn_steps(i, n_active, loads, compute):
    @pl.when(i == 0)
    def _():
        for cp in sum(loads, []):
            cp.start()
        compute(loads)

    @pl.when(jnp.logical_and(i > 0, i < n_active))
    def _():
        compute(None)


def _wait_chunk(loads, k):
    if loads is not None:
        for cp in loads[k]:
            cp.wait()


def _ffn_fwd(h, modv, gvec, win, wout, *, T, mrow, grow, ctx_active, name, carry=None):
    R = h.shape[0]
    n_lat, n_tiles = T // TM, R // TM
    n_active = n_tiles if ctx_active else n_lat

    def body(h_ref, mod_ref, g_ref, win_hbm, wout_hbm, ho_ref, a_ref, b_ref, f_ref, win_v, wout_v, sem):
        i = pl.program_id(0)

        def compute(loads):
            h = h_ref[...]
            shift, scale, gate = (mod_ref[0, mrow + k : mrow + k + 1, :] for k in range(3))
            _, _, _, n = _norm_mod(h, g_ref[grow : grow + 1, :], shift, scale)
            n_bf = n.astype(BF16)
            acc = jnp.zeros((TM, D), F32)
            for k, (c0, cw) in enumerate(FFN_CHUNKS):
                _wait_chunk(loads, k)
                a = _dot_nt(n_bf, win_v[c0 : c0 + cw, :])
                b = _dot_nt(n_bf, win_v[DFF + c0 : DFF + c0 + cw, :])
                a_ref[:, c0 : c0 + cw] = a.astype(BF16)
                b_ref[:, c0 : c0 + cw] = b.astype(BF16)
                s = a * _sigmoid(a) * b
                acc = acc + _dot(s.astype(BF16), wout_v[c0 : c0 + cw, :])
            f_ref[...] = acc.astype(BF16)
            ho_ref[...] = h + (0.5 * gate) * acc

        _ffn_steps(i, n_active, _ffn_weight_copies(win_hbm, wout_hbm, win_v, wout_v, sem), compute)

        @pl.when(i >= n_active)
        def _():
            ho_ref[...] = h_ref[...]
            a_ref[...] = jnp.zeros_like(a_ref)
            b_ref[...] = jnp.zeros_like(b_ref)
            f_ref[...] = jnp.zeros_like(f_ref)

    return _call(
        body,
        name=name,
        grid=(n_tiles,),
        in_specs=[_rows(TM, D), _mod_spec(n_lat), _full((8, D)), ANY, ANY],
        out_specs=[_rows(TM, D), _rows(TM, DFF), _rows(TM, DFF), _rows(TM, D)],
        out_shape=[_sds((R, D), F32), _sds((R, DFF), BF16), _sds((R, DFF), BF16), _sds((R, D), BF16)],
        scratch_shapes=[pltpu.VMEM((2 * DFF, D), BF16), pltpu.VMEM((DFF, D), BF16),
                        pltpu.SemaphoreType.DMA((3 * len(FFN_CHUNKS),))],
        args=(h, modv, gvec, win, wout),
        carry=carry,
    )


def _ffn_bwd(h, dho, a, b, f, modv, gvec, win, wout, *, T, mrow, grow, ctx_active, name, carry=None):
    R = h.shape[0]
    n_lat, n_tiles = T // TM, R // TM
    n_active = n_tiles if ctx_active else n_lat

    def body(h_ref, dho_ref, a_ref, b_ref, f_ref, mod_ref, g_ref, win_hbm, wout_hbm,
             dh_ref, dab_ref, s_ref, n_ref, df_ref, part_ref, win_v, wout_v, sem):
        i = pl.program_id(0)
        first = jnp.logical_or(i == 0, i == n_lat)

        def compute(loads):
            h = h_ref[...]
            dho = dho_ref[...]
            shift, scale, gate = (mod_ref[0, mrow + k : mrow + k + 1, :] for k in range(3))
            g = g_ref[grow : grow + 1, :]
            r, xhat, y, n = _norm_mod(h, g, shift, scale)
            dgate = 0.5 * jnp.sum(dho * f_ref[...].astype(F32), axis=0, keepdims=True)
            df_bf = ((0.5 * gate) * dho).astype(BF16)
            df_ref[...] = df_bf
            n_ref[...] = n.astype(BF16)
            dn = jnp.zeros((TM, D), F32)
            for k, (c0, cw) in enumerate(FFN_CHUNKS):
                _wait_chunk(loads, k)
                ds = _dot_nt(df_bf, wout_v[c0 : c0 + cw, :])
                av = a_ref[:, c0 : c0 + cw].astype(F32)
                bv = b_ref[:, c0 : c0 + cw].astype(F32)
                sig = _sigmoid(av)
                sa = av * sig
                s_ref[:, c0 : c0 + cw] = (sa * bv).astype(BF16)
                da = (ds * bv * (sig * (1.0 + av * (1.0 - sig)))).astype(BF16)
                db = (ds * sa).astype(BF16)
                dab_ref[:, c0 : c0 + cw] = da
                dab_ref[:, DFF + c0 : DFF + c0 + cw] = db
                dn = dn + _dot(da, win_v[c0 : c0 + cw, :]) + _dot(db, win_v[DFF + c0 : DFF + c0 + cw, :])
            dh, dshift, dscale, dg = _norm_mod_bwd(dn, r, xhat, y, g, scale)
            dh_ref[...] = dho + dh
            _acc_partials(part_ref, first, {0: dshift, 1: dscale, 2: dgate, 3: dg})

        _ffn_steps(i, n_active, _ffn_weight_copies(win_hbm, wout_hbm, win_v, wout_v, sem), compute)

        @pl.when(i >= n_active)
        def _():
            dh_ref[...] = dho_ref[...]
            dab_ref[...] = jnp.zeros_like(dab_ref)
            s_ref[...] = jnp.zeros_like(s_ref)
            n_ref[...] = jnp.zeros_like(n_ref)
            df_ref[...] = jnp.zeros_like(df_ref)
            part_ref[...] = jnp.zeros_like(part_ref)

    return _call(
        body,
        name=name,
        grid=(n_tiles,),
        in_specs=[_rows(TM, D), _rows(TM, D), _rows(TM, DFF), _rows(TM, DFF), _rows(TM, D),
                  _mod_spec(n_lat), _full((8, D)), ANY, ANY],
        out_specs=[_rows(TM, D), _rows(TM, 2 * DFF), _rows(TM, DFF), _rows(TM, D), _rows(TM, D), _part_spec(n_lat)],
        out_shape=[_sds((R, D), F32), _sds((R, 2 * DFF), BF16), _sds((R, DFF), BF16), _sds((R, D), BF16),
                   _sds((R, D), BF16), _sds((2, 8, D), F32)],
        scratch_shapes=[pltpu.VMEM((2 * DFF, D), BF16), pltpu.VMEM((DFF, D), BF16),
                        pltpu.SemaphoreType.DMA((3 * len(FFN_CHUNKS),))],
        args=(h, dho, a, b, f, modv, gvec, win, wout),
        carry=carry,
    )


def _wgrad(x, y, *, bk, sh, name, carry=None):
    R, kx = x.shape
    n = y.shape[1]
    tr = R // 2
    nr, nsh = R // tr, bk // sh

    def body(x_ref, y_ref, o_ref, acc):
        r = pl.program_id(1)

        @pl.when(r == 0)
        def _():
            acc[...] = jnp.zeros_like(acc)

        acc[...] += _dot_tn(x_ref[...], y_ref[...])

        @pl.when(r == nr - 1)
        def _():
            for s in range(nsh):
                o_ref[s] = acc[s * sh : (s + 1) * sh, :].astype(BF16)

    (out,), got = _call(
        body,
        name=name,
        grid=(kx // bk, nr),
        in_specs=[pl.BlockSpec((tr, bk), lambda k, r: (r, k)), pl.BlockSpec((tr, n), lambda k, r: (r, 0))],
        out_specs=[pl.BlockSpec((nsh, sh, n), lambda k, r: (k, 0, 0))],
        out_shape=[_sds((kx // sh, sh, n), BF16)],
        scratch_shapes=[pltpu.VMEM((bk, n), F32)],
        args=(x, y),
        carry=carry,
    )
    return out, got


def _rot_half(x):
    lane = lax.broadcasted_iota(jnp.int32, x.shape, 1)
    return jnp.where((lane & (HD - 1)) < HD // 2, -pltpu.roll(x, 128 - HD // 2, 1), pltpu.roll(x, HD // 2, 1))


def _tile_sel():
    i = lax.broadcasted_iota(jnp.int32, (KVW, AW), 0)
    j = lax.broadcasted_iota(jnp.int32, (KVW, AW), 1)
    return jnp.where(i == (j // 256) * HD + (j & (HD - 1)), 1.0, 0.0).astype(BF16)


def _mixproj_fwd(h, modv, gvec, win, cos, sin, *, T, name, carry=None):
    R = h.shape[0]
    n_lat, n_tiles = T // TM, R // TM

    def body(h_ref, mod_ref, g_ref, win_ref, cos_ref, sin_ref, u_ref, q_ref, k4_ref, v4_ref):
        shift, scale = mod_ref[0, 3:4, :], mod_ref[0, 4:5, :]
        _, _, _, n = _norm_mod(h_ref[...], g_ref[1:2, :], shift, scale)
        proj = _dot_nt(n.astype(BF16), win_ref[...])
        u_ref[...] = proj[:, :PW]
        cs, sn = cos_ref[...], sin_ref[...]
        for s in range(AW // 128):
            x = proj[:, PW + 128 * s : PW + 128 * (s + 1)]
            q_ref[:, 128 * s : 128 * (s + 1)] = ((x * cs + _rot_half(x) * sn) * (HD ** -0.5)).astype(BF16)
        k = proj[:, PW + AW : PW + AW + KVW]
        k = (k * cs + _rot_half(k) * sn).astype(BF16)
        v = proj[:, PW + AW + KVW :].astype(BF16)
        sel = _tile_sel()
        k4_ref[...] = _dot(k, sel).astype(BF16)
        v4_ref[...] = _dot(v, sel).astype(BF16)

    return _call(
        body,
        name=name,
        grid=(n_tiles,),
        in_specs=[_rows(TM, D), _mod_spec(n_lat), _full((8, D)), _full((PROJ, D)), _rows(TM, 128), _rows(TM, 128)],
        out_specs=[_rows(TM, PW), _rows(TM, AW), _rows(TM, AW), _rows(TM, AW)],
        out_shape=[_sds((R, PW), F32), _sds((R, AW), BF16), _sds((R, AW), BF16), _sds((R, AW), BF16)],
        scratch_shapes=[],
        args=(h, modv, gvec, win, cos, sin),
        carry=carry,
    )


def _win_start(j, hi):
    return pl.multiple_of(jnp.clip((j - 1) * BLK, 0, hi - 3 * BLK), BLK)


def _hi_lo(x):
    hi = x.astype(BF16)
    return hi, (x - hi.astype(F32)).astype(BF16)


def _pool_bounds(t, w, T, R):
    is_ctx = t >= T
    lo = jnp.maximum(t - w // 2, jnp.where(is_ctx, T, 0))
    hi = jnp.minimum(t + w // 2, jnp.where(is_ctx, R, T))
    return lo, hi


def _pooled(u_v, j, T, R):
    start = _win_start(j, R)
    u3_hi, u3_lo = _hi_lo(u_v[pl.ds(start, 3 * BLK), :])
    ub = u_v[pl.ds(pl.multiple_of(j * BLK, BLK), BLK), :]
    t = j * BLK + lax.broadcasted_iota(jnp.int32, (BLK, 1), 0)
    pos = start + lax.broadcasted_iota(jnp.int32, (1, 3 * BLK), 1)
    pooled, counts = [], []
    for g, w in enumerate(POOL_WINDOWS):
        lo, hi = _pool_bounds(t, w, T, R)
        band = jnp.where(pos >= lo, jnp.where(pos < hi, 1.0, 0.0), 0.0).astype(BF16)
        sl = slice(g * 128, (g + 1) * 128)
        sums = _dot(band, u3_hi[:, sl]) + _dot(band, u3_lo[:, sl])
        cnt = (hi - lo).astype(F32)
        pooled.append(sums / cnt - ub[:, sl])
        counts.append(cnt)
    return pooled, counts


def _stack_heads(x):
    lane_h = lax.broadcasted_iota(jnp.int32, x.shape, 1) // HD
    return jnp.concatenate([jnp.where(lane_h == h, x, jnp.zeros_like(x)) for h in range(4)], axis=0)


def _unstack_heads(x):
    lane_h = lax.broadcasted_iota(jnp.int32, (BLK, 256), 1) // HD
    out = jnp.zeros((BLK, 256), F32)
    for h in range(4):
        out = out + jnp.where(lane_h == h, x[h * BLK : (h + 1) * BLK, :], 0.0)
    return out


def _window_mask(j, start_l, nbl):
    rowi = lax.broadcasted_iota(jnp.int32, (4 * BLK, 1), 0)
    qpos = j * BLK + (rowi & (BLK - 1))
    kpos = start_l + lax.broadcasted_iota(jnp.int32, (1, 3 * BLK), 1)
    reach = jnp.where(j < nbl, BLK, -1)
    return jnp.abs(kpos - qpos) <= reach


def _attn_exps(qs, kl, kc, sink_ref, g, valid):
    s_l = jnp.where(valid, _dot_nt(qs, kl), NEG)
    s_c = _dot_nt(qs, kc)
    rb = lax.broadcasted_iota(jnp.int32, (4 * BLK, 1), 0) // BLK
    sk = jnp.where(rb == 0, sink_ref[4 * g], jnp.where(rb == 1, sink_ref[4 * g + 1],
                   jnp.where(rb == 2, sink_ref[4 * g + 2], sink_ref[4 * g + 3])))
    m = jnp.maximum(jnp.maximum(jnp.max(s_l, axis=1, keepdims=True), jnp.max(s_c, axis=1, keepdims=True)), sk)
    e_l, e_c, e_s = jnp.exp(s_l - m), jnp.exp(s_c - m), jnp.exp(sk - m)
    inv = 1.0 / (jnp.sum(e_l, axis=1, keepdims=True) + jnp.sum(e_c, axis=1, keepdims=True) + e_s)
    return e_l, e_c, e_s, inv


def _attnpool_fwd(u, q, k4, v4, sink, w_pool, pool_scale, *, T, name, carry=None):
    R = u.shape[0]
    nb, nbl = R // BLK, T // BLK

    def body(q_ref, sink_ref, wp_ref, ps_ref, u_hbm, k4_hbm, v4_hbm, cat_ref, u_v, k4_v, v4_v, sem):
        j = pl.program_id(0)

        @pl.when(j == 0)
        def _():
            _load_weights([(u_hbm, u_v), (k4_hbm, k4_v), (v4_hbm, v4_v)], sem)

        pooled, _ = _pooled(u_v, j, T, R)
        for g in range(4):
            mixed = _dot(pooled[g].astype(BF16), wp_ref[g].astype(BF16)) * ps_ref[:, g * 128 : (g + 1) * 128]
            cat_ref[:, g * 128 : (g + 1) * 128] = mixed.astype(BF16)

        start_l = _win_start(j, T)
        valid = _window_mask(j, start_l, nbl)
        for g in range(2):
            gl = slice(g * 256, (g + 1) * 256)
            qs = _stack_heads(q_ref[:, gl])
            e_l, e_c, _, inv = _attn_exps(qs, k4_v[pl.ds(start_l, 3 * BLK), gl], k4_v[T:R, gl], sink_ref, g, valid)
            o = _dot(e_l.astype(BF16), v4_v[pl.ds(start_l, 3 * BLK), gl]) + _dot(e_c.astype(BF16), v4_v[T:R, gl])
            cat_ref[:, PW + g * 256 : PW + (g + 1) * 256] = _unstack_heads(o * inv).astype(BF16)

    return _call(
        body,
        name=name,
        grid=(nb,),
        in_specs=[_rows(BLK, AW), SMEM, _full((4, 128, 128)), _full((1, PW)), ANY, ANY, ANY],
        out_specs=[_rows(BLK, D)],
        out_shape=[_sds((R, D), BF16)],
        scratch_shapes=[pltpu.VMEM((R, PW), F32), pltpu.VMEM((R, AW), BF16), pltpu.VMEM((R, AW), BF16),
                        pltpu.SemaphoreType.DMA((3,))],
        args=(q, sink, w_pool, pool_scale, u, k4, v4),
        carry=carry,
    )


def _mixout_fwd(h, cat, modv, wout, *, T, ctx_active, name, carry=None):
    R = h.shape[0]
    n_lat, n_tiles = T // TM, R // TM

    def body(h_ref, cat_ref, mod_ref, w_ref, ho_ref, mo_ref):
        i = pl.program_id(0)

        def compute():
            mo = _dot(cat_ref[...], w_ref[...])
            mo_ref[...] = mo.astype(BF16)
            ho_ref[...] = h_ref[...] + mod_ref[0, 5:6, :] * mo

        if ctx_active:
            compute()
        else:
            pl.when(i < n_lat)(compute)

            @pl.when(i >= n_lat)
            def _():
                ho_ref[...] = h_ref[...]
                mo_ref[...] = jnp.zeros_like(mo_ref)

    return _call(
        body,
        name=name,
        grid=(n_tiles,),
        in_specs=[_rows(TM, D), _rows(TM, D), _mod_spec(n_lat), _full((D, D))],
        out_specs=[_rows(TM, D), _rows(TM, D)],
        out_shape=[_sds((R, D), F32), _sds((R, D), BF16)],
        scratch_shapes=[],
        args=(h, cat, modv, wout),
        carry=carry,
    )


def _mixout_bwd(dho, mo, modv, wout, *, T, ctx_active, name, carry=None):
    R = dho.shape[0]
    n_lat, n_tiles = T // TM, R // TM

    def body(dho_ref, mo_ref, mod_ref, w_ref, dcat_ref, dmix_ref, part_ref):
        i = pl.program_id(0)
        first = jnp.logical_or(i == 0, i == n_lat)

        def compute():
            dho = dho_ref[...]
            dmix = (mod_ref[0, 5:6, :] * dho).astype(BF16)
            dmix_ref[...] = dmix
            dcat_ref[...] = _dot_nt(dmix, w_ref[...])
            dgate = jnp.sum(dho * mo_ref[...].astype(F32), axis=0, keepdims=True)
            _acc_partials(part_ref, first, {2: dgate})

        if ctx_active:
            compute()
        else:
            pl.when(i < n_lat)(compute)

            @pl.when(i >= n_lat)
            def _():
                dcat_ref[...] = jnp.zeros_like(dcat_ref)
                dmix_ref[...] = jnp.zeros_like(dmix_ref)
                part_ref[...] = jnp.zeros_like(part_ref)

    return _call(
        body,
        name=name,
        grid=(n_tiles,),
        in_specs=[_rows(TM, D), _rows(TM, D), _mod_spec(n_lat), _full((D, D))],
        out_specs=[_rows(TM, D), _rows(TM, D), _part_spec(n_lat)],
        out_shape=[_sds((R, D), F32), _sds((R, D), BF16), _sds((2, 8, D), F32)],
        scratch_shapes=[],
        args=(dho, mo, modv, wout),
        carry=carry,
    )


def _pool_bwd(u, dcat, w_pool, pool_scale, *, T, name):
    R = u.shape[0]
    nb = R // BLK

    def body(dcat_ref, wp_ref, ps_ref, u_hbm, dps_ref, dwp_ref, dsc_ref, u_v, sem):
        j = pl.program_id(0)

        @pl.when(j == 0)
        def _():
            _load_weights([(u_hbm, u_v)], sem)
            dwp_ref[...] = jnp.zeros_like(dwp_ref)
            dsc_ref[...] = jnp.zeros_like(dsc_ref)

        pooled, counts = _pooled(u_v, j, T, R)
        for g in range(4):
            sl = slice(g * 128, (g + 1) * 128)
            p_bf = pooled[g].astype(BF16)
            w_bf = wp_ref[g].astype(BF16)
            dmixed = dcat_ref[:, sl]
            dsc_ref[0:1, sl] += jnp.sum(dmixed * _dot(p_bf, w_bf), axis=0, keepdims=True)
            dmp = (dmixed * ps_ref[:, sl]).astype(BF16)
            dwp_ref[sl, :] += _dot_tn(p_bf, dmp)
            dps_ref[:, sl] = _dot_nt(dmp, w_bf) / counts[g]

    return pl.pallas_call(
        body,
        name=name,
        grid=(nb,),
        in_specs=[_rows(BLK, D), _full((4, 128, 128)), _full((1, PW)), ANY],
        out_specs=[_rows(BLK, PW), _full((PW, 128)), _full((8, PW))],
        out_shape=[_sds((R, PW), F32), _sds((PW, 128), F32), _sds((8, PW), F32)],
        scratch_shapes=[pltpu.VMEM((R, PW), F32), pltpu.SemaphoreType.DMA((1,))],
        compiler_params=_params(),
    )(dcat, w_pool, pool_scale, u)


def _fold_heads(x):
    y = x[:, :128] + x[:, 128:]
    return y + pltpu.roll(y, HD, 1)


def _attn_bwd(q, k4, v4, dcat, dps, sink, *, T, name, carry=None):
    R = q.shape[0]
    nb, nbl = R // BLK, T // BLK

    def body(q_ref, dcat_ref, sink_ref, k4_hbm, v4_hbm, dps_hbm, du_ref, dq_ref, dk_ref, dv_ref, dsk_ref,
             k4_v, v4_v, dps_v, sem):
        j = pl.program_id(0)

        @pl.when(j == 0)
        def _():
            _load_weights([(k4_hbm, k4_v), (v4_hbm, v4_v), (dps_hbm, dps_v)], sem)
            dk_ref[...] = jnp.zeros_like(dk_ref)
            dv_ref[...] = jnp.zeros_like(dv_ref)
            dsk_ref[...] = jnp.zeros_like(dsk_ref)

        start = _win_start(j, R)
        d3_hi, d3_lo = _hi_lo(dps_v[pl.ds(start, 3 * BLK), :])
        db = dps_v[pl.ds(pl.multiple_of(j * BLK, BLK), BLK), :]
        pos = j * BLK + lax.broadcasted_iota(jnp.int32, (BLK, 1), 0)
        t_r = start + lax.broadcasted_iota(jnp.int32, (1, 3 * BLK), 1)
        for g, w in enumerate(POOL_WINDOWS):
            sl = slice(g * 128, (g + 1) * 128)
            lo_r, hi_r = _pool_bounds(t_r, w, T, R)
            band_t = jnp.where(pos >= lo_r, jnp.where(pos < hi_r, 1.0, 0.0), 0.0).astype(BF16)
            lo_c, hi_c = _pool_bounds(pos, w, T, R)
            du_ref[:, sl] = _dot(band_t, d3_hi[:, sl]) + _dot(band_t, d3_lo[:, sl]) - db[:, sl] * (hi_c - lo_c).astype(F32)

        start_l = _win_start(j, T)
        valid = _window_mask(j, start_l, nbl)
        rb = lax.broadcasted_iota(jnp.int32, (4 * BLK, 1), 0) // BLK
        lane = lax.broadcasted_iota(jnp.int32, (1, 128), 1)
        dk_l, dk_c, dv_l, dv_c = [], [], [], []
        for g in range(2):
            gl = slice(g * 256, (g + 1) * 256)
            qs = _stack_heads(q_ref[:, gl])
            kl, kc = k4_v[pl.ds(start_l, 3 * BLK), gl], k4_v[T:R, gl]
            vl, vc = v4_v[pl.ds(start_l, 3 * BLK), gl], v4_v[T:R, gl]
            e_l, e_c, e_s, inv = _attn_exps(qs, kl, kc, sink_ref, g, valid)
            p_l, p_c, p_s = e_l * inv, e_c * inv, e_s * inv
            dos = _stack_heads(dcat_ref[:, PW + g * 256 : PW + (g + 1) * 256]).astype(BF16)
            dp_l, dp_c = _dot_nt(dos, vl), _dot_nt(dos, vc)
            delta = jnp.sum(p_l * dp_l, axis=1, keepdims=True) + jnp.sum(p_c * dp_c, axis=1, keepdims=True)
            ds_l = (p_l * (dp_l - delta)).astype(BF16)
            ds_c = (p_c * (dp_c - delta)).astype(BF16)
            dq_ref[:, gl] = _unstack_heads(_dot(ds_l, kl) + _dot(ds_c, kc)) * (HD ** -0.5)
            dk_l.append(_fold_heads(_dot_tn(ds_l, qs)))
            dk_c.append(_fold_heads(_dot_tn(ds_c, qs)))
            dv_l.append(_fold_heads(_dot_tn(p_l.astype(BF16), dos)))
            dv_c.append(_fold_heads(_dot_tn(p_c.astype(BF16), dos)))
            dsink = -p_s * delta
            for h in range(4):
                tot = jnp.sum(jnp.where(rb == h, dsink, 0.0), axis=0, keepdims=True)
                dsk_ref[4 * g + h : 4 * g + h + 1, :] += jnp.broadcast_to(tot, (1, 128))
        first = lane < HD
        dk_ref[pl.ds(start_l, 3 * BLK), :] += jnp.where(first, dk_l[0], dk_l[1])
        dk_ref[T:R, :] += jnp.where(first, dk_c[0], dk_c[1])
        dv_ref[pl.ds(start_l, 3 * BLK), :] += jnp.where(first, dv_l[0], dv_l[1])
        dv_ref[T:R, :] += jnp.where(first, dv_c[0], dv_c[1])

    return _call(
        body,
        name=name,
        grid=(nb,),
        in_specs=[_rows(BLK, AW), _rows(BLK, D), SMEM, ANY, ANY, ANY],
        out_specs=[_rows(BLK, PW), _rows(BLK, AW), _full((R, KVW)), _full((R, KVW)), _full((8, 128))],
        out_shape=[_sds((R, PW), F32), _sds((R, AW), F32), _sds((R, KVW), F32), _sds((R, KVW), F32),
                   _sds((8, 128), F32)],
        scratch_shapes=[pltpu.VMEM((R, AW), BF16), pltpu.VMEM((R, AW), BF16), pltpu.VMEM((R, PW), F32),
                        pltpu.SemaphoreType.DMA((3,))],
        args=(q, dcat, sink, k4, v4, dps),
        carry=carry,
    )


def _mixproj_bwd(h, dho, du, dq, dk, dv, modv, gvec, win, cos, sin, *, T, name):
    R = h.shape[0]
    n_lat, n_tiles = T // TM, R // TM

    def body(h_ref, dho_ref, du_ref, dq_ref, dk_ref, dv_ref, mod_ref, g_ref, win_ref, cos_ref, sin_ref,
             dh_ref, dproj_ref, n_ref, part_ref):
        i = pl.program_id(0)
        first = jnp.logical_or(i == 0, i == n_lat)
        shift, scale = mod_ref[0, 3:4, :], mod_ref[0, 4:5, :]
        g = g_ref[1:2, :]
        r, xhat, y, n = _norm_mod(h_ref[...], g, shift, scale)
        n_ref[...] = n.astype(BF16)
        cs, sn = cos_ref[...], sin_ref[...]
        dproj_ref[:, :PW] = du_ref[...].astype(BF16)
        for s in range(AW // 128):
            x = dq_ref[:, 128 * s : 128 * (s + 1)]
            dproj_ref[:, PW + 128 * s : PW + 128 * (s + 1)] = (x * cs - _rot_half(x) * sn).astype(BF16)
        x = dk_ref[...]
        dproj_ref[:, PW + AW : PW + AW + KVW] = (x * cs - _rot_half(x) * sn).astype(BF16)
        dproj_ref[:, PW + AW + KVW :] = dv_ref[...].astype(BF16)
        dn = _dot(dproj_ref[...], win_ref[...])
        dh, dshift, dscale, dg = _norm_mod_bwd(dn, r, xhat, y, g, scale)
        dh_ref[...] = dho_ref[...] + dh
        _acc_partials(part_ref, first, {0: dshift, 1: dscale, 3: dg})

    return pl.pallas_call(
        body,
        name=name,
        grid=(n_tiles,),
        in_specs=[_rows(TM, D), _rows(TM, D), _rows(TM, PW), _rows(TM, AW), _rows(TM, KVW), _rows(TM, KVW),
                  _mod_spec(n_lat), _full((8, D)), _full((PROJ, D)), _rows(TM, 128), _rows(TM, 128)],
        out_specs=[_rows(TM, D), _rows(TM, PROJ), _rows(TM, D), _part_spec(n_lat)],
        out_shape=[_sds((R, D), F32), _sds((R, PROJ), BF16), _sds((R, D), BF16), _sds((2, 8, D), F32)],
        compiler_params=_params(),
    )(h, dho, du, dq, dk, dv, modv, gvec, win, cos, sin)


def _loss_head(h, target, g_final, *, T, name):
    R = h.shape[0]
    n_lat, n_tiles = T // TM, R // TM

    def body(h_ref, t_ref, g_ref, dh_ref, loss_ref, dg_ref):
        i = pl.program_id(0)

        @pl.when(i == 0)
        def _():
            loss_ref[...] = jnp.zeros_like(loss_ref)
            dg_ref[...] = jnp.zeros_like(dg_ref)

        @pl.when(i < n_lat)
        def _():
            h = h_ref[...]
            g = g_ref[...]
            r = lax.rsqrt(jnp.mean(h * h, axis=-1, keepdims=True) + EPS)
            xhat = h * r
            err = xhat * g - t_ref[...]
            tot = jnp.sum(jnp.sum(err * err, axis=1, keepdims=True), axis=0, keepdims=True)
            loss_ref[...] += jnp.broadcast_to(tot * (0.5 / D), loss_ref.shape)
            dy = err * (1.0 / D)
            dg_ref[0:1, :] += jnp.sum(dy * xhat, axis=0, keepdims=True)
            dxh = dy * g
            dh_ref[...] = r * (dxh - xhat * jnp.mean(dxh * xhat, axis=-1, keepdims=True))

        @pl.when(i >= n_lat)
        def _():
            dh_ref[...] = jnp.zeros_like(dh_ref)

    return pl.pallas_call(
        body,
        name=name,
        grid=(n_tiles,),
        in_specs=[_rows(TM, D), pl.BlockSpec((TM, D), lambda i: (jnp.minimum(i, n_lat - 1), 0)), _full((1, D))],
        out_specs=[_rows(TM, D), _full((8, 128)), _full((8, D))],
        out_shape=[_sds((R, D), F32), _sds((8, 128), F32), _sds((8, D), F32)],
        compiler_params=_params(),
    )(h, target, g_final)


def _mod_fwd(c16, w_mod, b_cols, *, name):
    nl, _, cols = w_mod.shape

    def body(c_ref, w_ref, b_ref, o_ref):
        c = c_ref[...]
        sc = (c * _sigmoid(c)).astype(BF16)
        o_ref[0] = _dot(sc, w_ref[0].astype(BF16)) + b_ref[0]

    return pl.pallas_call(
        body,
        name=name,
        grid=(nl,),
        in_specs=[_full((16, D)), pl.BlockSpec((1, D, cols), lambda l: (l, 0, 0)),
                  pl.BlockSpec((1, 1, cols), lambda l: (l, 0, 0))],
        out_specs=pl.BlockSpec((1, 16, cols), lambda l: (l, 0, 0)),
        out_shape=_sds((nl, 16, cols), F32),
        compiler_params=_params(),
    )(c16, w_mod, b_cols)


def _mod_bwd(c16, dm_cols, w_mod, *, name):
    nl, _, cols = w_mod.shape

    def body(c_ref, dm_ref, w_ref, gw_ref, dc_ref):
        c = c_ref[...]
        sc = (c * _sigmoid(c)).astype(BF16)
        dm = dm_ref[0].astype(BF16)
        gw_ref[0] = _dot_tn(sc, dm)
        dc_ref[0] = _dot_nt(dm, w_ref[0].astype(BF16))

    return pl.pallas_call(
        body,
        name=name,
        grid=(nl,),
        in_specs=[_full((16, D)), pl.BlockSpec((1, 16, cols), lambda l: (l, 0, 0)),
                  pl.BlockSpec((1, D, cols), lambda l: (l, 0, 0))],
        out_specs=[pl.BlockSpec((1, D, cols), lambda l: (l, 0, 0)), pl.BlockSpec((1, 16, D), lambda l: (l, 0, 0))],
        out_shape=[_sds((nl, D, cols), F32), _sds((nl, 16, D), F32)],
        compiler_params=_params(),
    )(c16, dm_cols, w_mod)


def _coords():
    return lax.axis_index("x"), lax.axis_index("y"), lax.axis_index("c")


FWD = 8


def _peer(k, x, y, c):
    if k == FWD:
        return (x ^ (1 - c), y ^ c, c)
    return (1 - x if k & 4 else x, 1 - y if k & 2 else y, 1 - c if k & 1 else c)


def _lin(p):
    return 4 * p[0] + 2 * p[1] + p[2]


def _view(ref, slot):
    return ref if slot is None else ref.at[slot]


class _Round:
    def __init__(self, ins, out_shapes, plan, local_plan=(), n_alias=0):
        self.ins, self.out_shapes = list(ins), list(out_shapes)
        self.plan, self.local_plan, self.n_alias = list(plan), list(local_plan), n_alias
        fed = {p[3] for p in self.plan if p[0] == FWD}
        self.feeders = [n for n, p in enumerate(self.plan) if p[0] in (2, 4, 6) and p[3] in fed]

    def sems(self):
        return [pltpu.SemaphoreType.DMA((len(self.plan),)), pltpu.SemaphoreType.DMA((len(self.plan),)),
                pltpu.SemaphoreType.DMA((max(len(self.local_plan), 1),))]

    def _remote(self, in_refs, out_refs, sems, incoming, pick):
        in_refs = list(out_refs[: self.n_alias]) + list(in_refs[self.n_alias :])
        x, y, c = _coords()
        me = _lin((x, y, c))
        copies = {}
        for idx, (k, ii, sfn, oi, dfn) in enumerate(self.plan):
            if not pick(idx, "d2d" if k == 1 else "fwd" if k == FWD else "ici"):
                continue
            peer = _peer(k, x, y, c)
            sender, receiver = (_lin(peer), me) if incoming else (me, _lin(peer))
            src = out_refs[oi] if ii is None else in_refs[ii]
            copies[idx] = pltpu.make_async_remote_copy(
                src_ref=_view(src, sfn(sender, receiver)), dst_ref=_view(out_refs[oi], dfn(sender, receiver)),
                send_sem=sems[0].at[idx], recv_sem=sems[1].at[idx], device_id=peer, device_id_type=MESH)
        return copies

    def _local(self, in_refs, out_refs, sems):
        in_refs = list(out_refs[: self.n_alias]) + list(in_refs[self.n_alias :])
        me = _lin(_coords())
        return [pltpu.make_async_copy(_view(in_refs[ii], sfn(me)), _view(out_refs[oi], dfn(me)), sems[2].at[idx])
                for idx, (ii, sfn, oi, dfn) in enumerate(self.local_plan)]

    def start(self, in_refs, out_refs, sems, links=("ici", "d2d")):
        for cp in self._remote(in_refs, out_refs, sems, False, lambda n, link: link in links).values():
            cp.start()
        if "ici" in links:
            for cp in self._local(in_refs, out_refs, sems):
                cp.start()

    def mid(self, in_refs, out_refs, sems):
        if self.feeders:
            for cp in self._remote(in_refs, out_refs, sems, True, lambda n, link: n in self.feeders).values():
                cp.wait_recv()
            for cp in self._remote(in_refs, out_refs, sems, False, lambda n, link: link == "fwd").values():
                cp.start()

    def finish(self, in_refs, out_refs, sems):
        for cp in self._remote(in_refs, out_refs, sems, True, lambda n, link: n not in self.feeders).values():
            cp.wait_recv()
        for cp in self._remote(in_refs, out_refs, sems, False, lambda n, link: True).values():
            cp.wait_send()
        for cp in self._local(in_refs, out_refs, sems):
            cp.wait()


def _exchange(name, rnd):
    n_in, n_out = len(rnd.ins), len(rnd.out_shapes)

    def body(*refs):
        in_refs, out_refs, sems = refs[:n_in], refs[n_in : n_in + n_out], refs[n_in + n_out :]
        rnd.start(in_refs, out_refs, sems)
        rnd.mid(in_refs, out_refs, sems)
        rnd.finish(in_refs, out_refs, sems)

    return pl.pallas_call(
        body, name=name, in_specs=[ANY] * n_in, out_specs=[ANY] * n_out, out_shape=rnd.out_shapes,
        scratch_shapes=rnd.sems(), input_output_aliases={i: i for i in range(rnd.n_alias)})(*rnd.ins)


def _call(body, *, name, grid, in_specs, out_specs, out_shape, scratch_shapes, args, carry=None):
    params = _params(len(grid))
    if carry is None:
        outs = pl.pallas_call(body, name=name, grid=grid, in_specs=in_specs, out_specs=out_specs, out_shape=out_shape,
                              scratch_shapes=scratch_shapes, compiler_params=params)(*args)
        return list(outs), []
    n_ci, n_co, n_cs = len(in_specs), len(out_shape), len(scratch_shapes)
    n_xi, n_xo = len(carry.ins), len(carry.out_shapes)

    def wrapped(*refs):
        ci, xi = refs[:n_ci], refs[n_ci : n_ci + n_xi]
        o0 = n_ci + n_xi
        co, xo = refs[o0 : o0 + n_co], refs[o0 + n_co : o0 + n_co + n_xo]
        s0 = o0 + n_co + n_xo
        cs, sems = refs[s0 : s0 + n_cs], refs[s0 + n_cs :]
        ids = [pl.program_id(a) for a in range(len(grid))]
        first = functools.reduce(jnp.logical_and, [i == 0 for i in ids])
        last = functools.reduce(jnp.logical_and, [i == g - 1 for i, g in zip(ids, grid)])

        @pl.when(first)
        def _():
            carry.start(xi, xo, sems, links=("ici",))

        if carry.feeders:
            step = functools.reduce(lambda acc, ig: acc * ig[1] + ig[0], zip(ids, grid), 0)
            n_steps = functools.reduce(lambda a, b: a * b, grid)

            @pl.when(step == min(n_steps - 1, n_steps // 2))
            def _():
                carry.mid(xi, xo, sems)

        body(*ci, *co, *cs)

        @pl.when(first)
        def _():
            carry.start(xi, xo, sems, links=("d2d",))

        @pl.when(last)
        def _():
            carry.finish(xi, xo, sems)

    outs = pl.pallas_call(
        wrapped, name=name, grid=grid, in_specs=list(in_specs) + [ANY] * n_xi, out_specs=list(out_specs) + [ANY] * n_xo,
        out_shape=list(out_shape) + carry.out_shapes, scratch_shapes=list(scratch_shapes) + carry.sems(),
        input_output_aliases={n_ci + i: n_co + i for i in range(carry.n_alias)}, compiler_params=params,
    )(*args, *carry.ins)
    return list(outs[:n_co]), list(outs[n_co:])


def _gather_direct(arrays):
    na = len(arrays)
    outs = [_sds((NDEV,) + a.shape, a.dtype) for a in arrays]
    plan = [(k, i, lambda s, r: None, i, lambda s, r: s) for i in range(na) for k in range(1, NDEV)]
    return _Round(arrays, outs, plan, [(i, lambda m: None, i, lambda m: m) for i in range(na)])


def _gather_a(arrays):
    na = len(arrays)
    outs = [_sds((NDEV,) + a.shape, a.dtype) for a in arrays]
    plan = [(k, i, lambda s, r: None, i, lambda s, r: s) for i in range(na) for k in (2, 4)]
    handed = lambda s, r: s ^ (2 << (s & 1))
    plan += [(FWD, None, handed, i, handed) for i in range(na)]
    return _Round(arrays, outs, plan, [(i, lambda m: None, i, lambda m: m) for i in range(na)])


def _gather_b(got):
    na = len(got)
    plan = [(1, i, (lambda s, r, k=k: s ^ k), i, (lambda s, r, k=k: s ^ k)) for i in range(na) for k in (0, 2, 4, 6)]
    return _Round(got, [_sds(g.shape, g.dtype) for g in got], plan, n_alias=na)


def _scatter_1(grads):
    plan = [(1, i, (lambda s, r, q=q: 2 * q + (r & 1)), i, (lambda s, r, q=q: q))
            for i in range(len(grads)) for q in range(4)]
    return _Round(grads, [_sds((4,) + g.shape[1:], g.dtype) for g in grads], plan)


def _scatter_2(chip):
    plan = [(k, i, lambda s, r: r >> 1, i, (lambda s, r, j=j: j)) for i in range(len(chip)) for j, k in enumerate((2, 4, 6))]
    return _Round(chip, [_sds((3,) + g.shape[1:], g.dtype) for g in chip], plan)


def _add_pairs(grads, got, pos, *, name):
    n = len(grads)
    mine = lambda a: pl.BlockSpec((1,) + a.shape[1:], lambda q, p: (2 * q + p[0], 0, 0))
    slot = lambda a: pl.BlockSpec((1,) + a.shape[1:], lambda q, p: (q, 0, 0))

    def body(pos_ref, *refs):
        for g_ref, r_ref, o_ref in zip(refs[:n], refs[n : 2 * n], refs[2 * n :]):
            o_ref[...] = (g_ref[...].astype(F32) + r_ref[...].astype(F32)).astype(o_ref.dtype)

    return pl.pallas_call(
        body,
        name=name,
        grid_spec=pltpu.PrefetchScalarGridSpec(
            num_scalar_prefetch=1, grid=(4,),
            in_specs=[mine(g) for g in grads] + [slot(g) for g in grads],
            out_specs=[slot(g) for g in grads]),
        out_shape=[_sds((4,) + g.shape[1:], g.dtype) for g in grads],
        compiler_params=_params(),
    )(pos, *grads, *got)


def _sum_adamw(chip, got, pos, w, m, v, layer, prior, *, name):
    _, sh, wd = chip.shape
    nl, rows, cols = w.shape
    nb, blk = 2, (sh // 2, wd)
    part = lambda n: pl.BlockSpec((n, sh // 2, wd), lambda i, p: ((p[1] if n == 1 else 0), i, 0))
    mine = pl.BlockSpec(blk, lambda i, p: (layer * nb + i, 0))
    flat = lambda t: t.reshape(nl * rows, cols)
    n_prior = 0 if prior is None else 4

    def body(pos_ref, c_ref, r_ref, w_ref, m_ref, v_ref, *refs):
        g_ref, d_ref, m2_ref, v2_ref = refs[n_prior:]
        g = c_ref[0].astype(F32)
        for s in range(3):
            g = g + r_ref[s].astype(F32)
        g_ref[...] = g
        d_ref[...], m2_ref[...], v2_ref[...] = _adamw_math(w_ref[...], g, m_ref[...], v_ref[...])

    outs = pl.pallas_call(
        body,
        name=name,
        grid_spec=pltpu.PrefetchScalarGridSpec(
            num_scalar_prefetch=1, grid=(nb,),
            in_specs=[part(1), part(3), mine, mine, mine] + [ANY] * n_prior,
            out_specs=[mine] * 4),
        out_shape=[_sds((nl * rows, cols), F32)] * 4,
        input_output_aliases={6 + k: k for k in range(n_prior)},
        compiler_params=_params(),
    )(pos, chip, got, flat(w), flat(m), flat(v), *(flat(t) for t in prior or ()))
    return [o.reshape(w.shape) for o in outs]


def _adamw_math(w, g, m, v):
    m2 = ADAM_B1 * m + (1.0 - ADAM_B1) * g
    v2 = ADAM_B2 * v + (1.0 - ADAM_B2) * (g * g)
    m_hat = m2 / (1.0 - ADAM_B1 ** ADAM_STEP)
    v_hat = v2 / (1.0 - ADAM_B2 ** ADAM_STEP)
    delta = -ADAM_LR * (m_hat / (jnp.sqrt(v_hat) + ADAM_EPS) + ADAM_WD * w)
    return delta, m2, v2


def _adamw(w, g, m, v, *, name, carry=None):
    shape = w.shape
    flat = [t.reshape(-1, shape[-1]) for t in (w, g, m, v)]
    rows, cols = flat[0].shape
    tr = rows // 8 if rows % 64 == 0 else rows
    spec = _rows(tr, cols)

    def body(w_ref, g_ref, m_ref, v_ref, d_ref, m2_ref, v2_ref):
        d_ref[...], m2_ref[...], v2_ref[...] = _adamw_math(w_ref[...], g_ref[...], m_ref[...], v_ref[...])

    outs, got = _call(body, name=name, grid=(rows // tr,), in_specs=[spec] * 4, out_specs=[spec] * 3,
                      out_shape=[_sds((rows, cols), F32)] * 3, scratch_shapes=[], args=flat, carry=carry)
    return tuple(o.reshape(shape) for o in outs), got


def _adds(tag, grads, got, *, pos):
    return _add_pairs(list(grads), list(got)[: len(grads)], pos, name=f"rs_add_{tag}")


def _small_sums(packets, nf, dwp, dsc, dsk, *, name):
    flat = [p for layer in packets for p in layer]

    def total(ref, *idx):
        acc = ref[(0,) + idx]
        for dev in range(1, NDEV):
            acc = acc + ref[(dev,) + idx]
        return acc

    def body(*refs):
        pk = refs[:6]
        nf_ref, dwp0, dwp1, dsc0, dsc1, dsk0, dsk1 = refs[6:13]
        dm_ref, gb_ref, gn_ref, gnf_ref, gwp_ref, gps_ref, gsk_ref = refs[13:]
        dm_ref[...] = jnp.zeros_like(dm_ref)
        gn_ref[...] = jnp.zeros_like(gn_ref)
        for l in range(2):
            for sb in range(3):
                p = pk[3 * l + sb]
                for r in range(3):
                    col = slice((3 * sb + r) * D, (3 * sb + r + 1) * D)
                    lat = p[0, 0, r : r + 1, :]
                    dm_ref[l, 0:1, col] = lat
                    for dev in range(1, NDEV):
                        row = p[dev, 0, r : r + 1, :]
                        dm_ref[l, dev : dev + 1, col] = row
                        lat = lat + row
                    ctx = total(p, 1, slice(r, r + 1), slice(None))
                    dm_ref[l, 8:9, col] = ctx
                    gb_ref[l : l + 1, col] = lat + ctx
                gn_ref[l, sb : sb + 1, :] = total(p, 0, slice(3, 4), slice(None)) + total(p, 1, slice(3, 4), slice(None))
        gnf_ref[...] = total(nf_ref, slice(0, 1), slice(None))
        for l, (a, b, c) in enumerate(((dwp0, dsc0, dsk0), (dwp1, dsc1, dsk1))):
            gwp_ref[l] = total(a, slice(None), slice(None))
            gps_ref[l : l + 1, :] = total(b, slice(0, 1), slice(None))
            gsk_ref[l] = total(c, slice(None), slice(None))

    ins = flat + [nf, dwp[0], dwp[1], dsc[0], dsc[1], dsk[0], dsk[1]]
    return pl.pallas_call(
        body,
        name=name,
        out_shape=[_sds((2, 16, NMOD * D), F32), _sds((2, NMOD * D), F32), _sds((2, 8, D), F32), _sds((1, D), F32),
                   _sds((2, PW, 128), F32), _sds((2, PW), F32), _sds((2, 8, 128), F32)],
        compiler_params=pltpu.CompilerParams(vmem_limit_bytes=VMEM_LIMIT),
    )(*ins)


def _small_adamw(c_ctx, dc_all, triples, *, name):
    n = len(triples)

    def body(*refs):
        c_ref, dc_ref = refs[0], refs[1]
        ins = refs[2 : 2 + 4 * n - 1]
        outs = refs[2 + 4 * n - 1 :]
        acc = dc_ref[0, 0, 8:9, :] + dc_ref[0, 1, 8:9, :]
        for dev in range(1, NDEV):
            acc = acc + (dc_ref[dev, 0, 8:9, :] + dc_ref[dev, 1, 8:9, :])
        c = c_ref[...]
        sig = _sigmoid(c)
        g_c = acc * (sig * (1.0 + c * (1.0 - sig)))
        outs[0][...] = g_c
        pos = 0
        for k in range(n):
            if k == 0:
                w, g, m, v = ins[0][...], g_c, ins[1][...], ins[2][...]
                pos = 3
            else:
                w, g, m, v = (ins[pos + t][...] for t in range(4))
                pos += 4
            d, m2, v2 = _adamw_math(w, g, m, v)
            outs[1 + 3 * k][...], outs[2 + 3 * k][...], outs[3 + 3 * k][...] = d, m2, v2

    flat_in = [c_ctx, dc_all]
    out_shape = [_sds(c_ctx.shape, F32)]
    for k, (w, g, m, v) in enumerate(triples):
        flat_in += [w, m, v] if k == 0 else [w, g, m, v]
        out_shape += [_sds(w.shape, F32)] * 3
    return pl.pallas_call(body, name=name, out_shape=out_shape,
                          compiler_params=pltpu.CompilerParams(vmem_limit_bytes=VMEM_LIMIT))(*flat_in)


def _rope_tables(T, R):
    t = jnp.arange(T)
    inv = ROPE_BASE ** (-jnp.arange(0, HD // 2, 2, dtype=F32) / (HD // 2))
    ang = jnp.concatenate([(t // GRID_W).astype(F32)[:, None] * inv, (t % GRID_W).astype(F32)[:, None] * inv], axis=-1)
    cos = jnp.concatenate([jnp.tile(jnp.cos(ang), (1, 4)), jnp.ones((R - T, 128), F32)], axis=0)
    sin = jnp.concatenate([jnp.tile(jnp.sin(ang), (1, 4)), jnp.zeros((R - T, 128), F32)], axis=0)
    return cos, sin


def kernel(x, c, ctx, c_ctx, w_mod, b_mod, norm_ffn1, w_ffn1_in, w_ffn1_out, norm_mix, w_in, w_pool, pool_scale, sink, w_out, norm_ffn2, w_ffn2_in, w_ffn2_out, norm_final, loss_target, m_c_ctx, m_w_mod, m_b_mod, m_norm_ffn1, m_w_ffn1_in, m_w_ffn1_out, m_norm_mix, m_w_in, m_w_pool, m_pool_scale, m_sink, m_w_out, m_norm_ffn2, m_w_ffn2_in, m_w_ffn2_out, m_norm_final, v_c_ctx, v_w_mod, v_b_mod, v_norm_ffn1, v_w_ffn1_in, v_w_ffn1_out, v_norm_mix, v_w_in, v_w_pool, v_pool_scale, v_sink, v_w_out, v_norm_ffn2, v_w_ffn2_in, v_w_ffn2_out, v_norm_final):
    T = x.shape[1]
    R = T + LC
    nl = w_mod.shape[0]
    cx, cy, cc = _coords()
    me = _lin((cx, cy, cc))
    pos = jnp.stack([cc, 2 * cx + cy]).astype(jnp.int32)
    mcols = w_mod.shape[2]

    shards = [([w_ffn1_in[l].T.astype(BF16), w_ffn1_out[l].astype(BF16)],
               [w_in[l].T.astype(BF16), w_out[l].astype(BF16)],
               [w_ffn2_in[l].T.astype(BF16), w_ffn2_out[l].astype(BF16)]) for l in range(nl)]

    got = _exchange("ag_c_w", _merge(_gather_direct([c]), _gather_a(shards[0][0] + shards[0][1])))
    c_all, w_first = got[0], got[1:]
    c16 = jnp.concatenate([c_all.reshape(NDEV, D), c_ctx[None], jnp.zeros((16 - NDEV - 1, D), F32)], axis=0)
    b_cols = lax.dynamic_slice(b_mod, (0, me * mcols), (nl, mcols)).reshape(nl, 1, mcols)
    got = _exchange("ag_mod_w", _merge(_gather_b(w_first), _gather_direct([_mod_fwd(c16, w_mod, b_cols, name="mod_fwd")])))
    w_first, mod_all = got[:4], got[4]
    mod_all = jnp.transpose(mod_all, (1, 2, 0, 3)).reshape(nl, 16, NMOD, D)
    mine = lax.dynamic_index_in_dim(mod_all, me, axis=1, keepdims=False)
    pad = jnp.zeros((nl, 16 - NMOD, D), F32)
    modv = jnp.stack([jnp.concatenate([mine, pad], axis=1), jnp.concatenate([mod_all[:, 8], pad], axis=1)], axis=1)

    gvec = [jnp.concatenate([norm_ffn1[l][None], norm_mix[l][None], norm_ffn2[l][None], jnp.zeros((5, D), F32)], axis=0)
            for l in range(nl)]
    cos, sin = _rope_tables(T, R)
    ps2 = [pool_scale[l][None] for l in range(nl)]

    h = jnp.concatenate([x[0], ctx[0]], axis=0)
    loss_all, dh, small, nf_all, big, last_partial = _forward_backward(
        h, loss_target[0], modv, gvec, shards, w_first, cos, sin, sink, w_pool, ps2, norm_final, pos, T=T)
    loss = jnp.sum(loss_all[:, 0, 0])
    grad_x = dh[:T][None]

    dm, g_b_mod, g_norms, g_nf, g_wp, g_ps, g_sk = _small_sums(
        [small[l][0:3] for l in range(nl)], nf_all, *[[small[l][k] for l in range(nl)] for k in (3, 4, 5)],
        name="small_sums")
    dm_cols = lax.dynamic_slice(dm, (0, 0, me * mcols), (nl, 16, mcols))
    g_w_mod, dc_part = _mod_bwd(c16, dm_cols, w_mod, name="mod_bwd")
    got = _exchange("rs1_tail", _merge(_scatter_1([last_partial]), _gather_direct([dc_part])))
    (c1o,), dc_all = _adds("ffn1_out_0", [last_partial], got[:1], pos=pos), got[1]

    delta, new_m, new_v = {}, {}, {}
    (delta["w_mod"], new_m["w_mod"], new_v["w_mod"]), got = _adamw(
        w_mod, g_w_mod, m_w_mod, v_w_mod, name="adamw_w_mod", carry=_scatter_2([c1o]))
    big[0][1] = (c1o, got[0])

    grads = {
        "b_mod": g_b_mod, "norm_ffn1": g_norms[:, 0], "norm_mix": g_norms[:, 1], "norm_ffn2": g_norms[:, 2],
        "w_pool": g_wp.reshape(w_pool.shape), "pool_scale": g_ps, "sink": g_sk[:, :, 0], "norm_final": g_nf.reshape(D),
        "w_mod": g_w_mod,
    }
    weights = dict(c_ctx=c_ctx, w_mod=w_mod, b_mod=b_mod, norm_ffn1=norm_ffn1, w_ffn1_in=w_ffn1_in, w_ffn1_out=w_ffn1_out,
                   norm_mix=norm_mix, w_in=w_in, w_pool=w_pool, pool_scale=pool_scale, sink=sink, w_out=w_out,
                   norm_ffn2=norm_ffn2, w_ffn2_in=w_ffn2_in, w_ffn2_out=w_ffn2_out, norm_final=norm_final)
    moms = dict(c_ctx=(m_c_ctx, v_c_ctx), w_mod=(m_w_mod, v_w_mod), b_mod=(m_b_mod, v_b_mod),
                norm_ffn1=(m_norm_ffn1, v_norm_ffn1), w_ffn1_in=(m_w_ffn1_in, v_w_ffn1_in),
                w_ffn1_out=(m_w_ffn1_out, v_w_ffn1_out), norm_mix=(m_norm_mix, v_norm_mix), w_in=(m_w_in, v_w_in),
                w_pool=(m_w_pool, v_w_pool), pool_scale=(m_pool_scale, v_pool_scale), sink=(m_sink, v_sink),
                w_out=(m_w_out, v_w_out), norm_ffn2=(m_norm_ffn2, v_norm_ffn2), w_ffn2_in=(m_w_ffn2_in, v_w_ffn2_in),
                w_ffn2_out=(m_w_ffn2_out, v_w_ffn2_out), norm_final=(m_norm_final, v_norm_final))
    order = list(weights)
    small_names = ["c_ctx", "b_mod", "norm_ffn1", "norm_mix", "w_pool", "pool_scale", "sink", "norm_ffn2", "norm_final"]

    def as2d(name, t):
        if name == "w_pool":
            return t.reshape(-1, 128)
        return t.reshape(1, -1) if t.ndim == 1 else t

    triples = [(as2d(n, weights[n]), None if n == "c_ctx" else as2d(n, grads[n]), as2d(n, moms[n][0]), as2d(n, moms[n][1]))
               for n in small_names]
    outs = _small_adamw(as2d("c_ctx", c_ctx), dc_all, triples, name="small_adamw")
    grads["c_ctx"] = outs[0].reshape(c_ctx.shape)
    for k, n in enumerate(small_names):
        delta[n], new_m[n], new_v[n] = (o.reshape(weights[n].shape) for o in outs[1 + 3 * k : 4 + 3 * k])
    for k, n in enumerate(["w_ffn1_in", "w_ffn1_out", "w_in", "w_out", "w_ffn2_in", "w_ffn2_out"]):
        turn = (lambda t: jnp.swapaxes(t, 1, 2)) if k % 2 == 0 else (lambda t: t)
        wmv = [turn(t) for t in (weights[n], *moms[n])]
        outs = None
        for l in reversed(range(nl)):
            outs = _sum_adamw(*big[l][k], pos, *wmv, l, outs, name=f"adamw_{n}_{l}")
        grads[n], delta[n], new_m[n], new_v[n] = (turn(o) for o in outs)

    return (loss, grad_x, *[grads[n] for n in order], *[delta[n] for n in order],
            *[new_m[n] for n in order], *[new_v[n] for n in order])


def _merge(*rounds):
    ins, outs, plan, local, n_alias = [], [], [], [], 0
    for r in rounds:
        assert r.n_alias == 0 or (not ins and r.n_alias == len(r.ins) == len(r.out_shapes))
        oi, oo = len(ins), len(outs)
        plan += [(k, None if i is None else i + oi, sf, o + oo, df) for k, i, sf, o, df in r.plan]
        local += [(i + oi, sf, o + oo, df) for i, sf, o, df in r.local_plan]
        ins += r.ins
        outs += r.out_shapes
        n_alias += r.n_alias
    return _Round(ins, outs, plan, local, n_alias)


def _forward_backward(h, target, modv, gvec, shards, w_first, cos, sin, sink, w_pool, ps2, norm_final, pos, *, T):
    nl = len(gvec)
    flat = lambda ws: [w.reshape(-1, D) for w in ws]
    saved = []
    w1, wm = flat(w_first[:2]), flat(w_first[2:])
    for l in range(nl):
        last = l == nl - 1
        h0 = h
        if l == 0:
            (h1, a1, b1, f1), got = _ffn_fwd(h0, modv[l], gvec[l], *w1, T=T, mrow=0, grow=0, ctx_active=True,
                                             name=f"ffn1_fwd_{l}", carry=_gather_a(shards[l][2]))
            (u, q, k4, v4), got = _mixproj_fwd(h1, modv[l], gvec[l], wm[0], cos, sin, T=T, name=f"mixproj_fwd_{l}",
                                               carry=_gather_b(got))
            w2 = flat(got)
        else:
            (h1, a1, b1, f1), got = _ffn_fwd(h0, modv[l], gvec[l], *w1, T=T, mrow=0, grow=0, ctx_active=True,
                                             name=f"ffn1_fwd_{l}", carry=_gather_b(nxt_m + nxt_2))
            wm, w2 = flat(got[:2]), flat(got[2:])
            (u, q, k4, v4), _ = _mixproj_fwd(h1, modv[l], gvec[l], wm[0], cos, sin, T=T, name=f"mixproj_fwd_{l}")
        (cat,), nxt_1 = _attnpool_fwd(u, q, k4, v4, sink[l], w_pool[l], ps2[l], T=T, name=f"attnpool_fwd_{l}",
                                      carry=None if last else _gather_a(shards[l + 1][0]))
        (h2, mo), nxt_m = _mixout_fwd(h1, cat, modv[l], wm[1], T=T, ctx_active=not last, name=f"mixout_fwd_{l}",
                                      carry=None if last else _gather_a(shards[l + 1][1]))
        (h3, a2, b2, f2), got = _ffn_fwd(h2, modv[l], gvec[l], *w2, T=T, mrow=6, grow=2, ctx_active=not last,
                                         name=f"ffn2_fwd_{l}",
                                         carry=None if last else _merge(_gather_b(nxt_1), _gather_a(shards[l + 1][2])))
        saved.append((h0, a1, b1, f1, h1, u, q, k4, v4, cat, mo, h2, a2, b2, f2, w1, wm, w2))
        h = h3
        if not last:
            w1, nxt_2 = flat(got[:2]), got[2:]

    dh, loss_part, dnf = _loss_head(h, target, norm_final[None], T=T, name="loss_head")

    adds = functools.partial(_adds, pos=pos)
    small, big = [None] * nl, {}
    prev = None
    for l in reversed(range(nl)):
        last = l == nl - 1
        h0, a1, b1, f1, h1, u, q, k4, v4, cat, mo, h2, a2, b2, f2, w1, wm, w2 = saved[l]
        (dh, dab, s, n, df, pk2), got = _ffn_bwd(
            h2, dh, a2, b2, f2, modv[l], gvec[l], *w2, T=T, mrow=6, grow=2, ctx_active=not last, name=f"ffn2_bwd_{l}",
            carry=_merge(_scatter_1(prev[0]), _gather_a(prev[1])) if prev else None)
        if prev:
            c1, small_a = adds(f"ffn1_{l + 1}", prev[0], got[:2]), got[2:]
        g_w2i, got = _wgrad(dab, n, bk=WG_BK, sh=2 * DFF // NDEV, name=f"wgrad_ffn2_in_{l}",
                            carry=_scatter_2(c1[:1]) if prev else None)
        if prev:
            big[l + 1][0] = (c1[0], got[0])
        g_w2o, got = _wgrad(s, df, bk=WG_BK, sh=DFF // NDEV, name=f"wgrad_ffn2_out_{l}",
                            carry=_scatter_2(c1[1:]) if prev else None)
        if prev:
            big[l + 1][1] = (c1[1], got[0])
        rnd = _scatter_1([g_w2i, g_w2o])
        (dcat, dmix, pko), got = _mixout_bwd(dh, mo, modv[l], wm[1], T=T, ctx_active=not last, name=f"mixout_bwd_{l}",
                                             carry=_merge(_gather_b(small_a), rnd) if prev else rnd)
        if prev:
            small[l + 1], got = got[: len(small_a)], got[len(small_a) :]
        c2 = adds(f"ffn2_{l}", [g_w2i, g_w2o], got)
        g_wo, _ = _wgrad(cat, dmix, bk=D, sh=D // NDEV, name=f"wgrad_out_{l}")
        dps, dwp, dsc = _pool_bwd(u, dcat, w_pool[l], ps2[l], T=T, name=f"pool_bwd_{l}")
        (du, dq, dk, dv, dsk), got = _attn_bwd(q, k4, v4, dcat, dps, sink[l], T=T, name=f"attn_bwd_{l}", carry=_scatter_2(c2))
        big[l] = [None, None, None, None, (c2[0], got[0]), (c2[1], got[1])]
        dh, dproj, n, pkm = _mixproj_bwd(h1, dh, du, dq, dk, dv, modv[l], gvec[l], wm[0], cos, sin, T=T, name=f"mixproj_bwd_{l}")
        g_wi, _ = _wgrad(dproj, n, bk=PROJ, sh=PROJ // NDEV, name=f"wgrad_in_{l}")
        (dh, dab, s, n, df, pk1), got = _ffn_bwd(h0, dh, a1, b1, f1, modv[l], gvec[l], *w1, T=T, mrow=0, grow=0,
                                                 ctx_active=True, name=f"ffn1_bwd_{l}", carry=_scatter_1([g_wi, g_wo]))
        cm = adds(f"mix_{l}", [g_wi, g_wo], got)
        mine = [pk1, pkm + pko, pk2, dwp, dsc, dsk]
        rnd = _merge(_scatter_2(cm), _gather_a(mine + [dnf, loss_part])) if l == 0 else _scatter_2(cm)
        g_w1i, got = _wgrad(dab, n, bk=WG_BK, sh=2 * DFF // NDEV, name=f"wgrad_ffn1_in_{l}", carry=rnd)
        big[l][2:4] = [(cm[0], got[0]), (cm[1], got[1])]
        if l > 0:
            g_w1o, _ = _wgrad(s, df, bk=WG_BK, sh=DFF // NDEV, name=f"wgrad_ffn1_out_{l}")
            prev = ([g_w1i, g_w1o], mine)
    (c1i,) = adds("ffn1_in_0", [g_w1i], _exchange("rs1_ffn1_in_0", _scatter_1([g_w1i])))
    g_w1o, got = _wgrad(s, df, bk=WG_BK, sh=DFF // NDEV, name="wgrad_ffn1_out_0",
                        carry=_merge(_gather_b(got[2:]), _scatter_2([c1i])))
    small[0], nf_all, loss_all = got[:6], got[6], got[7]
    big[0][0] = (c1i, got[8])
    return loss_all, dh, small, nf_all, big, g_w1o
```

```python
import functools

import jax
import jax.numpy as jnp
from jax import lax
from jax.experimental import pallas as pl
from jax.experimental.pallas import tpu as pltpu

F32, BF16 = jnp.float32, jnp.bfloat16

D = 1024
LC = 256
DFF = 2816
NMOD = 9
PW = 512
AW = 512
KVW = 128
PROJ = PW + AW + 2 * KVW
HD = 64
BLK = 128
GRID_W = 64
POOL_WINDOWS = (2, 4, 8, 16)
EPS = 1e-6
NEG = -1e30
ROPE_BASE = 10000.0
NDEV = 8
MESH = pl.DeviceIdType.MESH

ADAM_LR, ADAM_B1, ADAM_B2, ADAM_EPS, ADAM_WD, ADAM_STEP = 0.001, 0.9, 0.999, 1e-08, 0.01, 10

VMEM_LIMIT = 56 * 1024 * 1024
TM = 256
FFN_CHUNKS = ((0, 1536), (1536, 1280))
WG_BK = 1408

ANY = pl.BlockSpec(memory_space=pl.ANY)
SMEM = pl.BlockSpec(memory_space=pltpu.SMEM)


def _params(ngrid=1):
    return pltpu.CompilerParams(dimension_semantics=("arbitrary",) * ngrid, vmem_limit_bytes=VMEM_LIMIT)


def _dot(a, b):
    return jnp.dot(a, b, preferred_element_type=F32)


def _dot_nt(a, b):
    return lax.dot_general(a, b, (((1,), (1,)), ((), ())), preferred_element_type=F32)


def _dot_tn(a, b):
    return lax.dot_general(a, b, (((0,), (0,)), ((), ())), preferred_element_type=F32)


def _sigmoid(x):
    return 1.0 / (1.0 + jnp.exp(-x))


def _rows(tm, w):
    return pl.BlockSpec((tm, w), lambda i: (i, 0))


def _full(shape):
    nd = len(shape)
    return pl.BlockSpec(shape, lambda *_: (0,) * nd)


def _sds(shape, dtype):
    return jax.ShapeDtypeStruct(shape, dtype)


def _norm_mod(h, g, shift, scale):
    r = lax.rsqrt(jnp.mean(h * h, axis=-1, keepdims=True) + EPS)
    xhat = h * r
    y = xhat * g
    return r, xhat, y, y * (1.0 + scale) + shift


def _norm_mod_bwd(dn, r, xhat, y, g, scale):
    dshift = jnp.sum(dn, axis=0, keepdims=True)
    dscale = jnp.sum(dn * y, axis=0, keepdims=True)
    dy = dn * (1.0 + scale)
    dg = jnp.sum(dy * xhat, axis=0, keepdims=True)
    dxh = dy * g
    dh = r * (dxh - xhat * jnp.mean(dxh * xhat, axis=-1, keepdims=True))
    return dh, dshift, dscale, dg


def _acc_partials(part_ref, first, rows):
    @pl.when(first)
    def _():
        part_ref[...] = jnp.zeros_like(part_ref)

    for r, val in rows.items():
        part_ref[0, r : r + 1, :] += val


def _mod_spec(n_lat):
    return pl.BlockSpec((1, 16, D), lambda i: (i // n_lat, 0, 0))


def _part_spec(n_lat):
    return pl.BlockSpec((1, 8, D), lambda i: (i // n_lat, 0, 0))


def _load_weights(pairs, sem):
    copies = [pltpu.make_async_copy(src, dst, sem.at[k]) for k, (src, dst) in enumerate(pairs)]
    for cp in copies:
        cp.start()
    for cp in copies:
        cp.wait()


def _ffn_weight_copies(win_hbm, wout_hbm, win_v, wout_v, sem):
    loads = []
    for k, (c0, cw) in enumerate(FFN_CHUNKS):
        slabs = [(win_hbm, win_v, c0), (win_hbm, win_v, DFF + c0), (wout_hbm, wout_v, c0)]
        loads.append([pltpu.make_async_copy(src.at[pl.ds(r0, cw)], dst.at[pl.ds(r0, cw)], sem.at[3 * k + j])
                      for j, (src, dst, r0) in enumerate(slabs)])
    return loads


def _ffn_steps(i, n_active, loads, compute):
    @pl.when(i == 0)
    def _():
        for cp in sum(loads, []):
            cp.start()
        compute(loads)

    @pl.when(jnp.logical_and(i > 0, i < n_active))
    def _():
        compute(None)


def _wait_chunk(loads, k):
    if loads is not None:
        for cp in loads[k]:
            cp.wait()


def _ffn_fwd(h, modv, gvec, win, wout, *, T, mrow, grow, ctx_active, name, carry=None):
    R = h.shape[0]
    n_lat, n_tiles = T // TM, R // TM
    n_active = n_tiles if ctx_active else n_lat

    def body(h_ref, mod_ref, g_ref, win_hbm, wout_hbm, ho_ref, a_ref, b_ref, f_ref, win_v, wout_v, sem):
        i = pl.program_id(0)

        def compute(loads):
            h = h_ref[...]
            shift, scale, gate = (mod_ref[0, mrow + k : mrow + k + 1, :] for k in range(3))
            _, _, _, n = _norm_mod(h, g_ref[grow : grow + 1, :], shift, scale)
            n_bf = n.astype(BF16)
            acc = jnp.zeros((TM, D), F32)
            for k, (c0, cw) in enumerate(FFN_CHUNKS):
                _wait_chunk(loads, k)
                a = _dot_nt(n_bf, win_v[c0 : c0 + cw, :])
                b = _dot_nt(n_bf, win_v[DFF + c0 : DFF + c0 + cw, :])
                a_ref[:, c0 : c0 + cw] = a.astype(BF16)
                b_ref[:, c0 : c0 + cw] = b.astype(BF16)
                s = a * _sigmoid(a) * b
                acc = acc + _dot(s.astype(BF16), wout_v[c0 : c0 + cw, :])
            f_ref[...] = acc.astype(BF16)
            ho_ref[...] = h + (0.5 * gate) * acc

        _ffn_steps(i, n_active, _ffn_weight_copies(win_hbm, wout_hbm, win_v, wout_v, sem), compute)

        @pl.when(i >= n_active)
        def _():
            ho_ref[...] = h_ref[...]
            a_ref[...] = jnp.zeros_like(a_ref)
            b_ref[...] = jnp.zeros_like(b_ref)
            f_ref[...] = jnp.zeros_like(f_ref)

    return _call(
        body,
        name=name,
        grid=(n_tiles,),
        in_specs=[_rows(TM, D), _mod_spec(n_lat), _full((8, D)), ANY, ANY],
        out_specs=[_rows(TM, D), _rows(TM, DFF), _rows(TM, DFF), _rows(TM, D)],
        out_shape=[_sds((R, D), F32), _sds((R, DFF), BF16), _sds((R, DFF), BF16), _sds((R, D), BF16)],
        scratch_shapes=[pltpu.VMEM((2 * DFF, D), BF16), pltpu.VMEM((DFF, D), BF16),
                        pltpu.SemaphoreType.DMA((3 * len(FFN_CHUNKS),))],
        args=(h, modv, gvec, win, wout),
        carry=carry,
    )


def _ffn_bwd(h, dho, a, b, f, modv, gvec, win, wout, *, T, mrow, grow, ctx_active, name, carry=None):
    R = h.shape[0]
    n_lat, n_tiles = T // TM, R // TM
    n_active = n_tiles if ctx_active else n_lat

    def body(h_ref, dho_ref, a_ref, b_ref, f_ref, mod_ref, g_ref, win_hbm, wout_hbm,
             dh_ref, dab_ref, s_ref, n_ref, df_ref, part_ref, win_v, wout_v, sem):
        i = pl.program_id(0)
        first = jnp.logical_or(i == 0, i == n_lat)

        def compute(loads):
            h = h_ref[...]
            dho = dho_ref[...]
            shift, scale, gate = (mod_ref[0, mrow + k : mrow + k + 1, :] for k in range(3))
            g = g_ref[grow : grow + 1, :]
            r, xhat, y, n = _norm_mod(h, g, shift, scale)
            dgate = 0.5 * jnp.sum(dho * f_ref[...].astype(F32), axis=0, keepdims=True)
            df_bf = ((0.5 * gate) * dho).astype(BF16)
            df_ref[...] = df_bf
            n_ref[...] = n.astype(BF16)
            dn = jnp.zeros((TM, D), F32)
            for k, (c0, cw) in enumerate(FFN_CHUNKS):
                _wait_chunk(loads, k)
                ds = _dot_nt(df_bf, wout_v[c0 : c0 + cw, :])
                av = a_ref[:, c0 : c0 + cw].astype(F32)
                bv = b_ref[:, c0 : c0 + cw].astype(F32)
                sig = _sigmoid(av)
                sa = av * sig
                s_ref[:, c0 : c0 + cw] = (sa * bv).astype(BF16)
                da = (ds * bv * (sig * (1.0 + av * (1.0 - sig)))).astype(BF16)
                db = (ds * sa).astype(BF16)
                dab_ref[:, c0 : c0 + cw] = da
                dab_ref[:, DFF + c0 : DFF + c0 + cw] = db
                dn = dn + _dot(da, win_v[c0 : c0 + cw, :]) + _dot(db, win_v[DFF + c0 : DFF + c0 + cw, :])
            dh, dshift, dscale, dg = _norm_mod_bwd(dn, r, xhat, y, g, scale)
            dh_ref[...] = dho + dh
            _acc_partials(part_ref, first, {0: dshift, 1: dscale, 2: dgate, 3: dg})

        _ffn_steps(i, n_active, _ffn_weight_copies(win_hbm, wout_hbm, win_v, wout_v, sem), compute)

        @pl.when(i >= n_active)
        def _():
            dh_ref[...] = dho_ref[...]
            dab_ref[...] = jnp.zeros_like(dab_ref)
            s_ref[...] = jnp.zeros_like(s_ref)
            n_ref[...] = jnp.zeros_like(n_ref)
            df_ref[...] = jnp.zeros_like(df_ref)
            part_ref[...] = jnp.zeros_like(part_ref)

    return _call(
        body,
        name=name,
        grid=(n_tiles,),
        in_specs=[_rows(TM, D), _rows(TM, D), _rows(TM, DFF), _rows(TM, DFF), _rows(TM, D),
                  _mod_spec(n_lat), _full((8, D)), ANY, ANY],
        out_specs=[_rows(TM, D), _rows(TM, 2 * DFF), _rows(TM, DFF), _rows(TM, D), _rows(TM, D), _part_spec(n_lat)],
        out_shape=[_sds((R, D), F32), _sds((R, 2 * DFF), BF16), _sds((R, DFF), BF16), _sds((R, D), BF16),
                   _sds((R, D), BF16), _sds((2, 8, D), F32)],
        scratch_shapes=[pltpu.VMEM((2 * DFF, D), BF16), pltpu.VMEM((DFF, D), BF16),
                        pltpu.SemaphoreType.DMA((3 * len(FFN_CHUNKS),))],
        args=(h, dho, a, b, f, modv, gvec, win, wout),
        carry=carry,
    )


def _wgrad(x, y, *, bk, sh, name, carry=None):
    R, kx = x.shape
    n = y.shape[1]
    tr = R // 2
    nr, nsh = R // tr, bk // sh

    def body(x_ref, y_ref, o_ref, acc):
        r = pl.program_id(1)

        @pl.when(r == 0)
        def _():
            acc[...] = jnp.zeros_like(acc)

        acc[...] += _dot_tn(x_ref[...], y_ref[...])

        @pl.when(r == nr - 1)
        def _():
            for s in range(nsh):
                o_ref[s] = acc[s * sh : (s + 1) * sh, :].astype(BF16)

    (out,), got = _call(
        body,
        name=name,
        grid=(kx // bk, nr),
        in_specs=[pl.BlockSpec((tr, bk), lambda k, r: (r, k)), pl.BlockSpec((tr, n), lambda k, r: (r, 0))],
        out_specs=[pl.BlockSpec((nsh, sh, n), lambda k, r: (k, 0, 0))],
        out_shape=[_sds((kx // sh, sh, n), BF16)],
        scratch_shapes=[pltpu.VMEM((bk, n), F32)],
        args=(x, y),
        carry=carry,
    )
    return out, got


def _rot_half(x):
    lane = lax.broadcasted_iota(jnp.int32, x.shape, 1)
    return jnp.where((lane & (HD - 1)) < HD // 2, -pltpu.roll(x, 128 - HD // 2, 1), pltpu.roll(x, HD // 2, 1))


def _tile_sel():
    i = lax.broadcasted_iota(jnp.int32, (KVW, AW), 0)
    j = lax.broadcasted_iota(jnp.int32, (KVW, AW), 1)
    return jnp.where(i == (j // 256) * HD + (j & (HD - 1)), 1.0, 0.0).astype(BF16)


def _mixproj_fwd(h, modv, gvec, win, cos, sin, *, T, name, carry=None):
    R = h.shape[0]
    n_lat, n_tiles = T // TM, R // TM

    def body(h_ref, mod_ref, g_ref, win_ref, cos_ref, sin_ref, u_ref, q_ref, k4_ref, v4_ref):
        shift, scale = mod_ref[0, 3:4, :], mod_ref[0, 4:5, :]
        _, _, _, n = _norm_mod(h_ref[...], g_ref[1:2, :], shift, scale)
        proj = _dot_nt(n.astype(BF16), win_ref[...])
        u_ref[...] = proj[:, :PW]
        cs, sn = cos_ref[...], sin_ref[...]
        for s in range(AW // 128):
            x = proj[:, PW + 128 * s : PW + 128 * (s + 1)]
            q_ref[:, 128 * s : 128 * (s + 1)] = ((x * cs + _rot_half(x) * sn) * (HD ** -0.5)).astype(BF16)
        k = proj[:, PW + AW : PW + AW + KVW]
        k = (k * cs + _rot_half(k) * sn).astype(BF16)
        v = proj[:, PW + AW + KVW :].astype(BF16)
        sel = _tile_sel()
        k4_ref[...] = _dot(k, sel).astype(BF16)
        v4_ref[...] = _dot(v, sel).astype(BF16)

    return _call(
        body,
        name=name,
        grid=(n_tiles,),
        in_specs=[_rows(TM, D), _mod_spec(n_lat), _full((8, D)), _full((PROJ, D)), _rows(TM, 128), _rows(TM, 128)],
        out_specs=[_rows(TM, PW), _rows(TM, AW), _rows(TM, AW), _rows(TM, AW)],
        out_shape=[_sds((R, PW), F32), _sds((R, AW), BF16), _sds((R, AW), BF16), _sds((R, AW), BF16)],
        scratch_shapes=[],
        args=(h, modv, gvec, win, cos, sin),
        carry=carry,
    )


def _win_start(j, hi):
    return pl.multiple_of(jnp.clip((j - 1) * BLK, 0, hi - 3 * BLK), BLK)


def _hi_lo(x):
    hi = x.astype(BF16)
    return hi, (x - hi.astype(F32)).astype(BF16)


def _pool_bounds(t, w, T, R):
    is_ctx = t >= T
    lo = jnp.maximum(t - w // 2, jnp.where(is_ctx, T, 0))
    hi = jnp.minimum(t + w // 2, jnp.where(is_ctx, R, T))
    return lo, hi


def _pooled(u_v, j, T, R):
    start = _win_start(j, R)
    u3_hi, u3_lo = _hi_lo(u_v[pl.ds(start, 3 * BLK), :])
    ub = u_v[pl.ds(pl.multiple_of(j * BLK, BLK), BLK), :]
    t = j * BLK + lax.broadcasted_iota(jnp.int32, (BLK, 1), 0)
    pos = start + lax.broadcasted_iota(jnp.int32, (1, 3 * BLK), 1)
    pooled, counts = [], []
    for g, w in enumerate(POOL_WINDOWS):
        lo, hi = _pool_bounds(t, w, T, R)
        band = jnp.where(pos >= lo, jnp.where(pos < hi, 1.0, 0.0), 0.0).astype(BF16)
        sl = slice(g * 128, (g + 1) * 128)
        sums = _dot(band, u3_hi[:, sl]) + _dot(band, u3_lo[:, sl])
        cnt = (hi - lo).astype(F32)
        pooled.append(sums / cnt - ub[:, sl])
        counts.append(cnt)
    return pooled, counts


def _stack_heads(x):
    lane_h = lax.broadcasted_iota(jnp.int32, x.shape, 1) // HD
    return jnp.concatenate([jnp.where(lane_h == h, x, jnp.zeros_like(x)) for h in range(4)], axis=0)


def _unstack_heads(x):
    lane_h = lax.broadcasted_iota(jnp.int32, (BLK, 256), 1) // HD
    out = jnp.zeros((BLK, 256), F32)
    for h in range(4):
        out = out + jnp.where(lane_h == h, x[h * BLK : (h + 1) * BLK, :], 0.0)
    return out


def _window_mask(j, start_l, nbl):
    rowi = lax.broadcasted_iota(jnp.int32, (4 * BLK, 1), 0)
    qpos = j * BLK + (rowi & (BLK - 1))
    kpos = start_l + lax.broadcasted_iota(jnp.int32, (1, 3 * BLK), 1)
    reach = jnp.where(j < nbl, BLK, -1)
    return jnp.abs(kpos - qpos) <= reach


def _attn_exps(qs, kl, kc, sink_ref, g, valid):
    s_l = jnp.where(valid, _dot_nt(qs, kl), NEG)
    s_c = _dot_nt(qs, kc)
    rb = lax.broadcasted_iota(jnp.int32, (4 * BLK, 1), 0) // BLK
    sk = jnp.where(rb == 0, sink_ref[4 * g], jnp.where(rb == 1, sink_ref[4 * g + 1],
                   jnp.where(rb == 2, sink_ref[4 * g + 2], sink_ref[4 * g + 3])))
    m = jnp.maximum(jnp.maximum(jnp.max(s_l, axis=1, keepdims=True), jnp.max(s_c, axis=1, keepdims=True)), sk)
    e_l, e_c, e_s = jnp.exp(s_l - m), jnp.exp(s_c - m), jnp.exp(sk - m)
    inv = 1.0 / (jnp.sum(e_l, axis=1, keepdims=True) + jnp.sum(e_c, axis=1, keepdims=True) + e_s)
    return e_l, e_c, e_s, inv


def _attnpool_fwd(u, q, k4, v4, sink, w_pool, pool_scale, *, T, name, carry=None):
    R = u.shape[0]
    nb, nbl = R // BLK, T // BLK

    def body(q_ref, sink_ref, wp_ref, ps_ref, u_hbm, k4_hbm, v4_hbm, cat_ref, u_v, k4_v, v4_v, sem):
        j = pl.program_id(0)

        @pl.when(j == 0)
        def _():
            _load_weights([(u_hbm, u_v), (k4_hbm, k4_v), (v4_hbm, v4_v)], sem)

        pooled, _ = _pooled(u_v, j, T, R)
        for g in range(4):
            mixed = _dot(pooled[g].astype(BF16), wp_ref[g].astype(BF16)) * ps_ref[:, g * 128 : (g + 1) * 128]
            cat_ref[:, g * 128 : (g + 1) * 128] = mixed.astype(BF16)

        start_l = _win_start(j, T)
        valid = _window_mask(j, start_l, nbl)
        for g in range(2):
            gl = slice(g * 256, (g + 1) * 256)
            qs = _stack_heads(q_ref[:, gl])
            e_l, e_c, _, inv = _attn_exps(qs, k4_v[pl.ds(start_l, 3 * BLK), gl], k4_v[T:R, gl], sink_ref, g, valid)
            o = _dot(e_l.astype(BF16), v4_v[pl.ds(start_l, 3 * BLK), gl]) + _dot(e_c.astype(BF16), v4_v[T:R, gl])
            cat_ref[:, PW + g * 256 : PW + (g + 1) * 256] = _unstack_heads(o * inv).astype(BF16)

    return _call(
        body,
        name=name,
        grid=(nb,),
        in_specs=[_rows(BLK, AW), SMEM, _full((4, 128, 128)), _full((1, PW)), ANY, ANY, ANY],
        out_specs=[_rows(BLK, D)],
        out_shape=[_sds((R, D), BF16)],
        scratch_shapes=[pltpu.VMEM((R, PW), F32), pltpu.VMEM((R, AW), BF16), pltpu.VMEM((R, AW), BF16),
                        pltpu.SemaphoreType.DMA((3,))],
        args=(q, sink, w_pool, pool_scale, u, k4, v4),
        carry=carry,
    )


def _mixout_fwd(h, cat, modv, wout, *, T, ctx_active, name, carry=None):
    R = h.shape[0]
    n_lat, n_tiles = T // TM, R // TM

    def body(h_ref, cat_ref, mod_ref, w_ref, ho_ref, mo_ref):
        i = pl.program_id(0)

        def compute():
            mo = _dot(cat_ref[...], w_ref[...])
            mo_ref[...] = mo.astype(BF16)
            ho_ref[...] = h_ref[...] + mod_ref[0, 5:6, :] * mo

        if ctx_active:
            compute()
        else:
            pl.when(i < n_lat)(compute)

            @pl.when(i >= n_lat)
            def _():
                ho_ref[...] = h_ref[...]
                mo_ref[...] = jnp.zeros_like(mo_ref)

    return _call(
        body,
        name=name,
        grid=(n_tiles,),
        in_specs=[_rows(TM, D), _rows(TM, D), _mod_spec(n_lat), _full((D, D))],
        out_specs=[_rows(TM, D), _rows(TM, D)],
        out_shape=[_sds((R, D), F32), _sds((R, D), BF16)],
        scratch_shapes=[],
        args=(h, cat, modv, wout),
        carry=carry,
    )


def _mixout_bwd(dho, mo, modv, wout, *, T, ctx_active, name, carry=None):
    R = dho.shape[0]
    n_lat, n_tiles = T // TM, R // TM

    def body(dho_ref, mo_ref, mod_ref, w_ref, dcat_ref, dmix_ref, part_ref):
        i = pl.program_id(0)
        first = jnp.logical_or(i == 0, i == n_lat)

        def compute():
            dho = dho_ref[...]
            dmix = (mod_ref[0, 5:6, :] * dho).astype(BF16)
            dmix_ref[...] = dmix
            dcat_ref[...] = _dot_nt(dmix, w_ref[...])
            dgate = jnp.sum(dho * mo_ref[...].astype(F32), axis=0, keepdims=True)
            _acc_partials(part_ref, first, {2: dgate})

        if ctx_active:
            compute()
        else:
            pl.when(i < n_lat)(compute)

            @pl.when(i >= n_lat)
            def _():
                dcat_ref[...] = jnp.zeros_like(dcat_ref)
                dmix_ref[...] = jnp.zeros_like(dmix_ref)
                part_ref[...] = jnp.zeros_like(part_ref)

    return _call(
        body,
        name=name,
        grid=(n_tiles,),
        in_specs=[_rows(TM, D), _rows(TM, D), _mod_spec(n_lat), _full((D, D))],
        out_specs=[_rows(TM, D), _rows(TM, D), _part_spec(n_lat)],
        out_shape=[_sds((R, D), F32), _sds((R, D), BF16), _sds((2, 8, D), F32)],
        scratch_shapes=[],
        args=(dho, mo, modv, wout),
        carry=carry,
    )


def _pool_bwd(u, dcat, w_pool, pool_scale, *, T, name):
    R = u.shape[0]
    nb = R // BLK

    def body(dcat_ref, wp_ref, ps_ref, u_hbm, dps_ref, dwp_ref, dsc_ref, u_v, sem):
        j = pl.program_id(0)

        @pl.when(j == 0)
        def _():
            _load_weights([(u_hbm, u_v)], sem)
            dwp_ref[...] = jnp.zeros_like(dwp_ref)
            dsc_ref[...] = jnp.zeros_like(dsc_ref)

        pooled, counts = _pooled(u_v, j, T, R)
        for g in range(4):
            sl = slice(g * 128, (g + 1) * 128)
            p_bf = pooled[g].astype(BF16)
            w_bf = wp_ref[g].astype(BF16)
            dmixed = dcat_ref[:, sl]
            dsc_ref[0:1, sl] += jnp.sum(dmixed * _dot(p_bf, w_bf), axis=0, keepdims=True)
            dmp = (dmixed * ps_ref[:, sl]).astype(BF16)
            dwp_ref[sl, :] += _dot_tn(p_bf, dmp)
            dps_ref[:, sl] = _dot_nt(dmp, w_bf) / counts[g]

    return pl.pallas_call(
        body,
        name=name,
        grid=(nb,),
        in_specs=[_rows(BLK, D), _full((4, 128, 128)), _full((1, PW)), ANY],
        out_specs=[_rows(BLK, PW), _full((PW, 128)), _full((8, PW))],
        out_shape=[_sds((R, PW), F32), _sds((PW, 128), F32), _sds((8, PW), F32)],
        scratch_shapes=[pltpu.VMEM((R, PW), F32), pltpu.SemaphoreType.DMA((1,))],
        compiler_params=_params(),
    )(dcat, w_pool, pool_scale, u)


def _fold_heads(x):
    y = x[:, :128] + x[:, 128:]
    return y + pltpu.roll(y, HD, 1)


def _attn_bwd(q, k4, v4, dcat, dps, sink, *, T, name, carry=None):
    R = q.shape[0]
    nb, nbl = R // BLK, T // BLK

    def body(q_ref, dcat_ref, sink_ref, k4_hbm, v4_hbm, dps_hbm, du_ref, dq_ref, dk_ref, dv_ref, dsk_ref,
             k4_v, v4_v, dps_v, sem):
        j = pl.program_id(0)

        @pl.when(j == 0)
        def _():
            _load_weights([(k4_hbm, k4_v), (v4_hbm, v4_v), (dps_hbm, dps_v)], sem)
            dk_ref[...] = jnp.zeros_like(dk_ref)
            dv_ref[...] = jnp.zeros_like(dv_ref)
            dsk_ref[...] = jnp.zeros_like(dsk_ref)

        start = _win_start(j, R)
        d3_hi, d3_lo = _hi_lo(dps_v[pl.ds(start, 3 * BLK), :])
        db = dps_v[pl.ds(pl.multiple_of(j * BLK, BLK), BLK), :]
        pos = j * BLK + lax.broadcasted_iota(jnp.int32, (BLK, 1), 0)
        t_r = start + lax.broadcasted_iota(jnp.int32, (1, 3 * BLK), 1)
        for g, w in enumerate(POOL_WINDOWS):
            sl = slice(g * 128, (g + 1) * 128)
            lo_r, hi_r = _pool_bounds(t_r, w, T, R)
            band_t = jnp.where(pos >= lo_r, jnp.where(pos < hi_r, 1.0, 0.0), 0.0).astype(BF16)
            lo_c, hi_c = _pool_bounds(pos, w, T, R)
            du_ref[:, sl] = _dot(band_t, d3_hi[:, sl]) + _dot(band_t, d3_lo[:, sl]) - db[:, sl] * (hi_c - lo_c).astype(F32)

        start_l = _win_start(j, T)
        valid = _window_mask(j, start_l, nbl)
        rb = lax.broadcasted_iota(jnp.int32, (4 * BLK, 1), 0) // BLK
        lane = lax.broadcasted_iota(jnp.int32, (1, 128), 1)
        dk_l, dk_c, dv_l, dv_c = [], [], [], []
        for g in range(2):
            gl = slice(g * 256, (g + 1) * 256)
            qs = _stack_heads(q_ref[:, gl])
            kl, kc = k4_v[pl.ds(start_l, 3 * BLK), gl], k4_v[T:R, gl]
            vl, vc = v4_v[pl.ds(start_l, 3 * BLK), gl], v4_v[T:R, gl]
            e_l, e_c, e_s, inv = _attn_exps(qs, kl, kc, sink_ref, g, valid)
            p_l, p_c, p_s = e_l * inv, e_c * inv, e_s * inv
            dos = _stack_heads(dcat_ref[:, PW + g * 256 : PW + (g + 1) * 256]).astype(BF16)
            dp_l, dp_c = _dot_nt(dos, vl), _dot_nt(dos, vc)
            delta = jnp.sum(p_l * dp_l, axis=1, keepdims=True) + jnp.sum(p_c * dp_c, axis=1, keepdims=True)
            ds_l = (p_l * (dp_l - delta)).astype(BF16)
            ds_c = (p_c * (dp_c - delta)).astype(BF16)
            dq_ref[:, gl] = _unstack_heads(_dot(ds_l, kl) + _dot(ds_c, kc)) * (HD ** -0.5)
            dk_l.append(_fold_heads(_dot_tn(ds_l, qs)))
            dk_c.append(_fold_heads(_dot_tn(ds_c, qs)))
            dv_l.append(_fold_heads(_dot_tn(p_l.astype(BF16), dos)))
            dv_c.append(_fold_heads(_dot_tn(p_c.astype(BF16), dos)))
            dsink = -p_s * delta
            for h in range(4):
                tot = jnp.sum(jnp.where(rb == h, dsink, 0.0), axis=0, keepdims=True)
                dsk_ref[4 * g + h : 4 * g + h + 1, :] += jnp.broadcast_to(tot, (1, 128))
        first = lane < HD
        dk_ref[pl.ds(start_l, 3 * BLK), :] += jnp.where(first, dk_l[0], dk_l[1])
        dk_ref[T:R, :] += jnp.where(first, dk_c[0], dk_c[1])
        dv_ref[pl.ds(start_l, 3 * BLK), :] += jnp.where(first, dv_l[0], dv_l[1])
        dv_ref[T:R, :] += jnp.where(first, dv_c[0], dv_c[1])

    return _call(
        body,
        name=name,
        grid=(nb,),
        in_specs=[_rows(BLK, AW), _rows(BLK, D), SMEM, ANY, ANY, ANY],
        out_specs=[_rows(BLK, PW), _rows(BLK, AW), _full((R, KVW)), _full((R, KVW)), _full((8, 128))],
        out_shape=[_sds((R, PW), F32), _sds((R, AW), F32), _sds((R, KVW), F32), _sds((R, KVW), F32),
                   _sds((8, 128), F32)],
        scratch_shapes=[pltpu.VMEM((R, AW), BF16), pltpu.VMEM((R, AW), BF16), pltpu.VMEM((R, PW), F32),
                        pltpu.SemaphoreType.DMA((3,))],
        args=(q, dcat, sink, k4, v4, dps),
        carry=carry,
    )


def _mixproj_bwd(h, dho, du, dq, dk, dv, modv, gvec, win, cos, sin, *, T, name):
    R = h.shape[0]
    n_lat, n_tiles = T // TM, R // TM

    def body(h_ref, dho_ref, du_ref, dq_ref, dk_ref, dv_ref, mod_ref, g_ref, win_ref, cos_ref, sin_ref,
             dh_ref, dproj_ref, n_ref, part_ref):
        i = pl.program_id(0)
        first = jnp.logical_or(i == 0, i == n_lat)
        shift, scale = mod_ref[0, 3:4, :], mod_ref[0, 4:5, :]
        g = g_ref[1:2, :]
        r, xhat, y, n = _norm_mod(h_ref[...], g, shift, scale)
        n_ref[...] = n.astype(BF16)
        cs, sn = cos_ref[...], sin_ref[...]
        dproj_ref[:, :PW] = du_ref[...].astype(BF16)
        for s in range(AW // 128):
            x = dq_ref[:, 128 * s : 128 * (s + 1)]
            dproj_ref[:, PW + 128 * s : PW + 128 * (s + 1)] = (x * cs - _rot_half(x) * sn).astype(BF16)
        x = dk_ref[...]
        dproj_ref[:, PW + AW : PW + AW + KVW] = (x * cs - _rot_half(x) * sn).astype(BF16)
        dproj_ref[:, PW + AW + KVW :] = dv_ref[...].astype(BF16)
        dn = _dot(dproj_ref[...], win_ref[...])
        dh, dshift, dscale, dg = _norm_mod_bwd(dn, r, xhat, y, g, scale)
        dh_ref[...] = dho_ref[...] + dh
        _acc_partials(part_ref, first, {0: dshift, 1: dscale, 3: dg})

    return pl.pallas_call(
        body,
        name=name,
        grid=(n_tiles,),
        in_specs=[_rows(TM, D), _rows(TM, D), _rows(TM, PW), _rows(TM, AW), _rows(TM, KVW), _rows(TM, KVW),
                  _mod_spec(n_lat), _full((8, D)), _full((PROJ, D)), _rows(TM, 128), _rows(TM, 128)],
        out_specs=[_rows(TM, D), _rows(TM, PROJ), _rows(TM, D), _part_spec(n_lat)],
        out_shape=[_sds((R, D), F32), _sds((R, PROJ), BF16), _sds((R, D), BF16), _sds((2, 8, D), F32)],
        compiler_params=_params(),
    )(h, dho, du, dq, dk, dv, modv, gvec, win, cos, sin)


def _loss_head(h, target, g_final, *, T, name):
    R = h.shape[0]
    n_lat, n_tiles = T // TM, R // TM

    def body(h_ref, t_ref, g_ref, dh_ref, loss_ref, dg_ref):
        i = pl.program_id(0)

        @pl.when(i == 0)
        def _():
            loss_ref[...] = jnp.zeros_like(loss_ref)
            dg_ref[...] = jnp.zeros_like(dg_ref)

        @pl.when(i < n_lat)
        def _():
            h = h_ref[...]
            g = g_ref[...]
            r = lax.rsqrt(jnp.mean(h * h, axis=-1, keepdims=True) + EPS)
            xhat = h * r
            err = xhat * g - t_ref[...]
            tot = jnp.sum(jnp.sum(err * err, axis=1, keepdims=True), axis=0, keepdims=True)
            loss_ref[...] += jnp.broadcast_to(tot * (0.5 / D), loss_ref.shape)
            dy = err * (1.0 / D)
            dg_ref[0:1, :] += jnp.sum(dy * xhat, axis=0, keepdims=True)
            dxh = dy * g
            dh_ref[...] = r * (dxh - xhat * jnp.mean(dxh * xhat, axis=-1, keepdims=True))

        @pl.when(i >= n_lat)
        def _():
            dh_ref[...] = jnp.zeros_like(dh_ref)

    return pl.pallas_call(
        body,
        name=name,
        grid=(n_tiles,),
        in_specs=[_rows(TM, D), pl.BlockSpec((TM, D), lambda i: (jnp.minimum(i, n_lat - 1), 0)), _full((1, D))],
        out_specs=[_rows(TM, D), _full((8, 128)), _full((8, D))],
        out_shape=[_sds((R, D), F32), _sds((8, 128), F32), _sds((8, D), F32)],
        compiler_params=_params(),
    )(h, target, g_final)


def _mod_fwd(c16, w_mod, b_cols, *, name):
    nl, _, cols = w_mod.shape

    def body(c_ref, w_ref, b_ref, o_ref):
        c = c_ref[...]
        sc = (c * _sigmoid(c)).astype(BF16)
        o_ref[0] = _dot(sc, w_ref[0].astype(BF16)) + b_ref[0]

    return pl.pallas_call(
        body,
        name=name,
        grid=(nl,),
        in_specs=[_full((16, D)), pl.BlockSpec((1, D, cols), lambda l: (l, 0, 0)),
                  pl.BlockSpec((1, 1, cols), lambda l: (l, 0, 0))],
        out_specs=pl.BlockSpec((1, 16, cols), lambda l: (l, 0, 0)),
        out_shape=_sds((nl, 16, cols), F32),
        compiler_params=_params(),
    )(c16, w_mod, b_cols)


def _mod_bwd(c16, dm_cols, w_mod, *, name):
    nl, _, cols = w_mod.shape

    def body(c_ref, dm_ref, w_ref, gw_ref, dc_ref):
        c = c_ref[...]
        sc = (c * _sigmoid(c)).astype(BF16)
        dm = dm_ref[0].astype(BF16)
        gw_ref[0] = _dot_tn(sc, dm)
        dc_ref[0] = _dot_nt(dm, w_ref[0].astype(BF16))

    return pl.pallas_call(
        body,
        name=name,
        grid=(nl,),
        in_specs=[_full((16, D)), pl.BlockSpec((1, 16, cols), lambda l: (l, 0, 0)),
                  pl.BlockSpec((1, D, cols), lambda l: (l, 0, 0))],
        out_specs=[pl.BlockSpec((1, D, cols), lambda l: (l, 0, 0)), pl.BlockSpec((1, 16, D), lambda l: (l, 0, 0))],
        out_shape=[_sds((nl, D, cols), F32), _sds((nl, 16, D), F32)],
        compiler_params=_params(),
    )(c16, dm_cols, w_mod)


def _coords():
    return lax.axis_index("x"), lax.axis_index("y"), lax.axis_index("c")


FWD = 8


def _peer(k, x, y, c):
    if k == FWD:
        return (x ^ (1 - c), y ^ c, c)
    return (1 - x if k & 4 else x, 1 - y if k & 2 else y, 1 - c if k & 1 else c)


def _lin(p):
    return 4 * p[0] + 2 * p[1] + p[2]


def _view(ref, slot):
    return ref if slot is None else ref.at[slot]


class _Round:
    def __init__(self, ins, out_shapes, plan, local_plan=(), n_alias=0):
        self.ins, self.out_shapes = list(ins), list(out_shapes)
        self.plan, self.local_plan, self.n_alias = list(plan), list(local_plan), n_alias
        fed = {p[3] for p in self.plan if p[0] == FWD}
        self.feeders = [n for n, p in enumerate(self.plan) if p[0] in (2, 4, 6) and p[3] in fed]

    def sems(self):
        return [pltpu.SemaphoreType.DMA((len(self.plan),)), pltpu.SemaphoreType.DMA((len(self.plan),)),
                pltpu.SemaphoreType.DMA((max(len(self.local_plan), 1),))]

    def _remote(self, in_refs, out_refs, sems, incoming, pick):
        in_refs = list(out_refs[: self.n_alias]) + list(in_refs[self.n_alias :])
        x, y, c = _coords()
        me = _lin((x, y, c))
        copies = {}
        for idx, (k, ii, sfn, oi, dfn) in enumerate(self.plan):
            if not pick(idx, "d2d" if k == 1 else "fwd" if k == FWD else "ici"):
                continue
            peer = _peer(k, x, y, c)
            sender, receiver = (_lin(peer), me) if incoming else (me, _lin(peer))
            src = out_refs[oi] if ii is None else in_refs[ii]
            copies[idx] = pltpu.make_async_remote_copy(
                src_ref=_view(src, sfn(sender, receiver)), dst_ref=_view(out_refs[oi], dfn(sender, receiver)),
                send_sem=sems[0].at[idx], recv_sem=sems[1].at[idx], device_id=peer, device_id_type=MESH)
        return copies

    def _local(self, in_refs, out_refs, sems):
        in_refs = list(out_refs[: self.n_alias]) + list(in_refs[self.n_alias :])
        me = _lin(_coords())
        return [pltpu.make_async_copy(_view(in_refs[ii], sfn(me)), _view(out_refs[oi], dfn(me)), sems[2].at[idx])
                for idx, (ii, sfn, oi, dfn) in enumerate(self.local_plan)]

    def start(self, in_refs, out_refs, sems, links=("ici", "d2d")):
        for cp in self._remote(in_refs, out_refs, sems, False, lambda n, link: link in links).values():
            cp.start()
        if "ici" in links:
            for cp in self._local(in_refs, out_refs, sems):
                cp.start()

    def mid(self, in_refs, out_refs, sems):
        if self.feeders:
            for cp in self._remote(in_refs, out_refs, sems, True, lambda n, link: n in self.feeders).values():
                cp.wait_recv()
            for cp in self._remote(in_refs, out_refs, sems, False, lambda n, link: link == "fwd").values():
                cp.start()

    def finish(self, in_refs, out_refs, sems):
        for cp in self._remote(in_refs, out_refs, sems, True, lambda n, link: n not in self.feeders).values():
            cp.wait_recv()
        for cp in self._remote(in_refs, out_refs, sems, False, lambda n, link: True).values():
            cp.wait_send()
        for cp in self._local(in_refs, out_refs, sems):
            cp.wait()


def _exchange(name, rnd):
    n_in, n_out = len(rnd.ins), len(rnd.out_shapes)

    def body(*refs):
        in_refs, out_refs, sems = refs[:n_in], refs[n_in : n_in + n_out], refs[n_in + n_out :]
        rnd.start(in_refs, out_refs, sems)
        rnd.mid(in_refs, out_refs, sems)
        rnd.finish(in_refs, out_refs, sems)

    return pl.pallas_call(
        body, name=name, in_specs=[ANY] * n_in, out_specs=[ANY] * n_out, out_shape=rnd.out_shapes,
        scratch_shapes=rnd.sems(), input_output_aliases={i: i for i in range(rnd.n_alias)})(*rnd.ins)


def _call(body, *, name, grid, in_specs, out_specs, out_shape, scratch_shapes, args, carry=None):
    params = _params(len(grid))
    if carry is None:
        outs = pl.pallas_call(body, name=name, grid=grid, in_specs=in_specs, out_specs=out_specs, out_shape=out_shape,
                              scratch_shapes=scratch_shapes, compiler_params=params)(*args)
        return list(outs), []
    n_ci, n_co, n_cs = len(in_specs), len(out_shape), len(scratch_shapes)
    n_xi, n_xo = len(carry.ins), len(carry.out_shapes)

    def wrapped(*refs):
        ci, xi = refs[:n_ci], refs[n_ci : n_ci + n_xi]
        o0 = n_ci + n_xi
        co, xo = refs[o0 : o0 + n_co], refs[o0 + n_co : o0 + n_co + n_xo]
        s0 = o0 + n_co + n_xo
        cs, sems = refs[s0 : s0 + n_cs], refs[s0 + n_cs :]
        ids = [pl.program_id(a) for a in range(len(grid))]
        first = functools.reduce(jnp.logical_and, [i == 0 for i in ids])
        last = functools.reduce(jnp.logical_and, [i == g - 1 for i, g in zip(ids, grid)])

        @pl.when(first)
        def _():
            carry.start(xi, xo, sems, links=("ici",))

        if carry.feeders:
            step = functools.reduce(lambda acc, ig: acc * ig[1] + ig[0], zip(ids, grid), 0)
            n_steps = functools.reduce(lambda a, b: a * b, grid)

            @pl.when(step == min(n_steps - 1, (7 * n_steps) // 10))
            def _():
                carry.mid(xi, xo, sems)

        body(*ci, *co, *cs)

        @pl.when(first)
        def _():
            carry.start(xi, xo, sems, links=("d2d",))

        @pl.when(last)
        def _():
            carry.finish(xi, xo, sems)

    outs = pl.pallas_call(
        wrapped, name=name, grid=grid, in_specs=list(in_specs) + [ANY] * n_xi, out_specs=list(out_specs) + [ANY] * n_xo,
        out_shape=list(out_shape) + carry.out_shapes, scratch_shapes=list(scratch_shapes) + carry.sems(),
        input_output_aliases={n_ci + i: n_co + i for i in range(carry.n_alias)}, compiler_params=params,
    )(*args, *carry.ins)
    return list(outs[:n_co]), list(outs[n_co:])


def _gather_direct(arrays):
    na = len(arrays)
    outs = [_sds((NDEV,) + a.shape, a.dtype) for a in arrays]
    plan = [(k, i, lambda s, r: None, i, lambda s, r: s) for i in range(na) for k in range(1, NDEV)]
    return _Round(arrays, outs, plan, [(i, lambda m: None, i, lambda m: m) for i in range(na)])


def _gather_a(arrays):
    na = len(arrays)
    outs = [_sds((NDEV,) + a.shape, a.dtype) for a in arrays]
    plan = [(k, i, lambda s, r: None, i, lambda s, r: s) for i in range(na) for k in (2, 4)]
    handed = lambda s, r: s ^ (2 << (s & 1))
    plan += [(FWD, None, handed, i, handed) for i in range(na)]
    return _Round(arrays, outs, plan, [(i, lambda m: None, i, lambda m: m) for i in range(na)])


def _gather_b(got):
    na = len(got)
    plan = [(1, i, (lambda s, r, k=k: s ^ k), i, (lambda s, r, k=k: s ^ k)) for i in range(na) for k in (0, 2, 4, 6)]
    return _Round(got, [_sds(g.shape, g.dtype) for g in got], plan, n_alias=na)


def _scatter_1(grads):
    plan = [(1, i, (lambda s, r, q=q: 2 * q + (r & 1)), i, (lambda s, r, q=q: q))
            for i in range(len(grads)) for q in range(4)]
    return _Round(grads, [_sds((4,) + g.shape[1:], g.dtype) for g in grads], plan)


def _scatter_2(chip):
    plan = [(k, i, lambda s, r: r >> 1, i, (lambda s, r, j=j: j)) for i in range(len(chip)) for j, k in enumerate((2, 4, 6))]
    return _Round(chip, [_sds((3,) + g.shape[1:], g.dtype) for g in chip], plan)


def _add_pairs(grads, got, pos, *, name):
    n = len(grads)
    mine = lambda a: pl.BlockSpec((1,) + a.shape[1:], lambda q, p: (2 * q + p[0], 0, 0))
    slot = lambda a: pl.BlockSpec((1,) + a.shape[1:], lambda q, p: (q, 0, 0))

    def body(pos_ref, *refs):
        for g_ref, r_ref, o_ref in zip(refs[:n], refs[n : 2 * n], refs[2 * n :]):
            o_ref[...] = (g_ref[...].astype(F32) + r_ref[...].astype(F32)).astype(o_ref.dtype)

    return pl.pallas_call(
        body,
        name=name,
        grid_spec=pltpu.PrefetchScalarGridSpec(
            num_scalar_prefetch=1, grid=(4,),
            in_specs=[mine(g) for g in grads] + [slot(g) for g in grads],
            out_specs=[slot(g) for g in grads]),
        out_shape=[_sds((4,) + g.shape[1:], g.dtype) for g in grads],
        compiler_params=_params(),
    )(pos, *grads, *got)


def _sum_adamw(chip, got, pos, w, m, v, layer, prior, *, name):
    _, sh, wd = chip.shape
    nl, rows, cols = w.shape
    nb, blk = 2, (sh // 2, wd)
    part = lambda n: pl.BlockSpec((n, sh // 2, wd), lambda i, p: ((p[1] if n == 1 else 0), i, 0))
    mine = pl.BlockSpec(blk, lambda i, p: (layer * nb + i, 0))
    flat = lambda t: t.reshape(nl * rows, cols)
    n_prior = 0 if prior is None else 4

    def body(pos_ref, c_ref, r_ref, w_ref, m_ref, v_ref, *refs):
        g_ref, d_ref, m2_ref, v2_ref = refs[n_prior:]
        g = c_ref[0].astype(F32)
        for s in range(3):
            g = g + r_ref[s].astype(F32)
        g_ref[...] = g
        d_ref[...], m2_ref[...], v2_ref[...] = _adamw_math(w_ref[...], g, m_ref[...], v_ref[...])

    outs = pl.pallas_call(
        body,
        name=name,
        grid_spec=pltpu.PrefetchScalarGridSpec(
            num_scalar_prefetch=1, grid=(nb,),
            in_specs=[part(1), part(3), mine, mine, mine] + [ANY] * n_prior,
            out_specs=[mine] * 4),
        out_shape=[_sds((nl * rows, cols), F32)] * 4,
        input_output_aliases={6 + k: k for k in range(n_prior)},
        compiler_params=_params(),
    )(pos, chip, got, flat(w), flat(m), flat(v), *(flat(t) for t in prior or ()))
    return [o.reshape(w.shape) for o in outs]


def _adamw_math(w, g, m, v):
    m2 = ADAM_B1 * m + (1.0 - ADAM_B1) * g
    v2 = ADAM_B2 * v + (1.0 - ADAM_B2) * (g * g)
    m_hat = m2 / (1.0 - ADAM_B1 ** ADAM_STEP)
    v_hat = v2 / (1.0 - ADAM_B2 ** ADAM_STEP)
    delta = -ADAM_LR * (m_hat / (jnp.sqrt(v_hat) + ADAM_EPS) + ADAM_WD * w)
    return delta, m2, v2


def _adamw(w, g, m, v, *, name, carry=None):
    shape = w.shape
    flat = [t.reshape(-1, shape[-1]) for t in (w, g, m, v)]
    rows, cols = flat[0].shape
    tr = rows // 8 if rows % 64 == 0 else rows
    spec = _rows(tr, cols)

    def body(w_ref, g_ref, m_ref, v_ref, d_ref, m2_ref, v2_ref):
        d_ref[...], m2_ref[...], v2_ref[...] = _adamw_math(w_ref[...], g_ref[...], m_ref[...], v_ref[...])

    outs, got = _call(body, name=name, grid=(rows // tr,), in_specs=[spec] * 4, out_specs=[spec] * 3,
                      out_shape=[_sds((rows, cols), F32)] * 3, scratch_shapes=[], args=flat, carry=carry)
    return tuple(o.reshape(shape) for o in outs), got


def _adds(tag, grads, got, *, pos):
    return _add_pairs(list(grads), list(got)[: len(grads)], pos, name=f"rs_add_{tag}")


def _small_sums(packets, nf, dwp, dsc, dsk, *, name):
    flat = [p for layer in packets for p in layer]

    def total(ref, *idx):
        acc = ref[(0,) + idx]
        for dev in range(1, NDEV):
            acc = acc + ref[(dev,) + idx]
        return acc

    def body(*refs):
        pk = refs[:6]
        nf_ref, dwp0, dwp1, dsc0, dsc1, dsk0, dsk1 = refs[6:13]
        dm_ref, gb_ref, gn_ref, gnf_ref, gwp_ref, gps_ref, gsk_ref = refs[13:]
        dm_ref[...] = jnp.zeros_like(dm_ref)
        gn_ref[...] = jnp.zeros_like(gn_ref)
        for l in range(2):
            for sb in range(3):
                p = pk[3 * l + sb]
                for r in range(3):
                    col = slice((3 * sb + r) * D, (3 * sb + r + 1) * D)
                    lat = p[0, 0, r : r + 1, :]
                    dm_ref[l, 0:1, col] = lat
                    for dev in range(1, NDEV):
                        row = p[dev, 0, r : r + 1, :]
                        dm_ref[l, dev : dev + 1, col] = row
                        lat = lat + row
                    ctx = total(p, 1, slice(r, r + 1), slice(None))
                    dm_ref[l, 8:9, col] = ctx
                    gb_ref[l : l + 1, col] = lat + ctx
                gn_ref[l, sb : sb + 1, :] = total(p, 0, slice(3, 4), slice(None)) + total(p, 1, slice(3, 4), slice(None))
        gnf_ref[...] = total(nf_ref, slice(0, 1), slice(None))
        for l, (a, b, c) in enumerate(((dwp0, dsc0, dsk0), (dwp1, dsc1, dsk1))):
            gwp_ref[l] = total(a, slice(None), slice(None))
            gps_ref[l : l + 1, :] = total(b, slice(0, 1), slice(None))
            gsk_ref[l] = total(c, slice(None), slice(None))

    ins = flat + [nf, dwp[0], dwp[1], dsc[0], dsc[1], dsk[0], dsk[1]]
    return pl.pallas_call(
        body,
        name=name,
        out_shape=[_sds((2, 16, NMOD * D), F32), _sds((2, NMOD * D), F32), _sds((2, 8, D), F32), _sds((1, D), F32),
                   _sds((2, PW, 128), F32), _sds((2, PW), F32), _sds((2, 8, 128), F32)],
        compiler_params=pltpu.CompilerParams(vmem_limit_bytes=VMEM_LIMIT),
    )(*ins)


def _small_adamw(c_ctx, dc_all, triples, *, name):
    n = len(triples)

    def body(*refs):
        c_ref, dc_ref = refs[0], refs[1]
        ins = refs[2 : 2 + 4 * n - 1]
        outs = refs[2 + 4 * n - 1 :]
        acc = dc_ref[0, 0, 8:9, :] + dc_ref[0, 1, 8:9, :]
        for dev in range(1, NDEV):
            acc = acc + (dc_ref[dev, 0, 8:9, :] + dc_ref[dev, 1, 8:9, :])
        c = c_ref[...]
        sig = _sigmoid(c)
        g_c = acc * (sig * (1.0 + c * (1.0 - sig)))
        outs[0][...] = g_c
        pos = 0
        for k in range(n):
            if k == 0:
                w, g, m, v = ins[0][...], g_c, ins[1][...], ins[2][...]
                pos = 3
            else:
                w, g, m, v = (ins[pos + t][...] for t in range(4))
                pos += 4
            d, m2, v2 = _adamw_math(w, g, m, v)
            outs[1 + 3 * k][...], outs[2 + 3 * k][...], outs[3 + 3 * k][...] = d, m2, v2

    flat_in = [c_ctx, dc_all]
    out_shape = [_sds(c_ctx.shape, F32)]
    for k, (w, g, m, v) in enumerate(triples):
        flat_in += [w, m, v] if k == 0 else [w, g, m, v]
        out_shape += [_sds(w.shape, F32)] * 3
    return pl.pallas_call(body, name=name, out_shape=out_shape,
                          compiler_params=pltpu.CompilerParams(vmem_limit_bytes=VMEM_LIMIT))(*flat_in)


def _rope_tables(T, R):
    t = jnp.arange(T)
    inv = ROPE_BASE ** (-jnp.arange(0, HD // 2, 2, dtype=F32) / (HD // 2))
    ang = jnp.concatenate([(t // GRID_W).astype(F32)[:, None] * inv, (t % GRID_W).astype(F32)[:, None] * inv], axis=-1)
    cos = jnp.concatenate([jnp.tile(jnp.cos(ang), (1, 4)), jnp.ones((R - T, 128), F32)], axis=0)
    sin = jnp.concatenate([jnp.tile(jnp.sin(ang), (1, 4)), jnp.zeros((R - T, 128), F32)], axis=0)
    return cos, sin


def kernel(x, c, ctx, c_ctx, w_mod, b_mod, norm_ffn1, w_ffn1_in, w_ffn1_out, norm_mix, w_in, w_pool, pool_scale, sink, w_out, norm_ffn2, w_ffn2_in, w_ffn2_out, norm_final, loss_target, m_c_ctx, m_w_mod, m_b_mod, m_norm_ffn1, m_w_ffn1_in, m_w_ffn1_out, m_norm_mix, m_w_in, m_w_pool, m_pool_scale, m_sink, m_w_out, m_norm_ffn2, m_w_ffn2_in, m_w_ffn2_out, m_norm_final, v_c_ctx, v_w_mod, v_b_mod, v_norm_ffn1, v_w_ffn1_in, v_w_ffn1_out, v_norm_mix, v_w_in, v_w_pool, v_pool_scale, v_sink, v_w_out, v_norm_ffn2, v_w_ffn2_in, v_w_ffn2_out, v_norm_final):
    T = x.shape[1]
    R = T + LC
    nl = w_mod.shape[0]
    cx, cy, cc = _coords()
    me = _lin((cx, cy, cc))
    pos = jnp.stack([cc, 2 * cx + cy]).astype(jnp.int32)
    mcols = w_mod.shape[2]

    shards = [([w_ffn1_in[l].T.astype(BF16), w_ffn1_out[l].astype(BF16)],
               [w_in[l].T.astype(BF16), w_out[l].astype(BF16)],
               [w_ffn2_in[l].T.astype(BF16), w_ffn2_out[l].astype(BF16)]) for l in range(nl)]

    got = _exchange("ag_c_w", _merge(_gather_direct([c]), _gather_a(shards[0][0] + shards[0][1])))
    c_all, w_first = got[0], got[1:]
    c16 = jnp.concatenate([c_all.reshape(NDEV, D), c_ctx[None], jnp.zeros((16 - NDEV - 1, D), F32)], axis=0)
    b_cols = lax.dynamic_slice(b_mod, (0, me * mcols), (nl, mcols)).reshape(nl, 1, mcols)
    got = _exchange("ag_mod_w", _merge(_gather_b(w_first), _gather_direct([_mod_fwd(c16, w_mod, b_cols, name="mod_fwd")])))
    w_first, mod_all = got[:4], got[4]
    mod_all = jnp.transpose(mod_all, (1, 2, 0, 3)).reshape(nl, 16, NMOD, D)
    mine = lax.dynamic_index_in_dim(mod_all, me, axis=1, keepdims=False)
    pad = jnp.zeros((nl, 16 - NMOD, D), F32)
    modv = jnp.stack([jnp.concatenate([mine, pad], axis=1), jnp.concatenate([mod_all[:, 8], pad], axis=1)], axis=1)

    gvec = [jnp.concatenate([norm_ffn1[l][None], norm_mix[l][None], norm_ffn2[l][None], jnp.zeros((5, D), F32)], axis=0)
            for l in range(nl)]
    cos, sin = _rope_tables(T, R)
    ps2 = [pool_scale[l][None] for l in range(nl)]

    h = jnp.concatenate([x[0], ctx[0]], axis=0)
    loss_all, dh, small, nf_all, big, last_partial = _forward_backward(
        h, loss_target[0], modv, gvec, shards, w_first, cos, sin, sink, w_pool, ps2, norm_final, pos, T=T)
    loss = jnp.sum(loss_all[:, 0, 0])
    grad_x = dh[:T][None]

    dm, g_b_mod, g_norms, g_nf, g_wp, g_ps, g_sk = _small_sums(
        [small[l][0:3] for l in range(nl)], nf_all, *[[small[l][k] for l in range(nl)] for k in (3, 4, 5)],
        name="small_sums")
    dm_cols = lax.dynamic_slice(dm, (0, 0, me * mcols), (nl, 16, mcols))
    g_w_mod, dc_part = _mod_bwd(c16, dm_cols, w_mod, name="mod_bwd")
    got = _exchange("rs1_tail", _merge(_scatter_1([last_partial]), _gather_direct([dc_part])))
    (c1o,), dc_all = _adds("ffn1_out_0", [last_partial], got[:1], pos=pos), got[1]

    delta, new_m, new_v = {}, {}, {}
    (delta["w_mod"], new_m["w_mod"], new_v["w_mod"]), got = _adamw(
        w_mod, g_w_mod, m_w_mod, v_w_mod, name="adamw_w_mod", carry=_scatter_2([c1o]))
    big[0][1] = (c1o, got[0])

    grads = {
        "b_mod": g_b_mod, "norm_ffn1": g_norms[:, 0], "norm_mix": g_norms[:, 1], "norm_ffn2": g_norms[:, 2],
        "w_pool": g_wp.reshape(w_pool.shape), "pool_scale": g_ps, "sink": g_sk[:, :, 0], "norm_final": g_nf.reshape(D),
        "w_mod": g_w_mod,
    }
    weights = dict(c_ctx=c_ctx, w_mod=w_mod, b_mod=b_mod, norm_ffn1=norm_ffn1, w_ffn1_in=w_ffn1_in, w_ffn1_out=w_ffn1_out,
                   norm_mix=norm_mix, w_in=w_in, w_pool=w_pool, pool_scale=pool_scale, sink=sink, w_out=w_out,
                   norm_ffn2=norm_ffn2, w_ffn2_in=w_ffn2_in, w_ffn2_out=w_ffn2_out, norm_final=norm_final)
    moms = dict(c_ctx=(m_c_ctx, v_c_ctx), w_mod=(m_w_mod, v_w_mod), b_mod=(m_b_mod, v_b_mod),
                norm_ffn1=(m_norm_ffn1, v_norm_ffn1), w_ffn1_in=(m_w_ffn1_in, v_w_ffn1_in),
                w_ffn1_out=(m_w_ffn1_out, v_w_ffn1_out), norm_mix=(m_norm_mix, v_norm_mix), w_in=(m_w_in, v_w_in),
                w_pool=(m_w_pool, v_w_pool), pool_scale=(m_pool_scale, v_pool_scale), sink=(m_sink, v_sink),
                w_out=(m_w_out, v_w_out), norm_ffn2=(m_norm_ffn2, v_norm_ffn2), w_ffn2_in=(m_w_ffn2_in, v_w_ffn2_in),
                w_ffn2_out=(m_w_ffn2_out, v_w_ffn2_out), norm_final=(m_norm_final, v_norm_final))
    order = list(weights)
    small_names = ["c_ctx", "b_mod", "norm_ffn1", "norm_mix", "w_pool", "pool_scale", "sink", "norm_ffn2", "norm_final"]

    def as2d(name, t):
        if name == "w_pool":
            return t.reshape(-1, 128)
        return t.reshape(1, -1) if t.ndim == 1 else t

    triples = [(as2d(n, weights[n]), None if n == "c_ctx" else as2d(n, grads[n]), as2d(n, moms[n][0]), as2d(n, moms[n][1]))
               for n in small_names]
    outs = _small_adamw(as2d("c_ctx", c_ctx), dc_all, triples, name="small_adamw")
    grads["c_ctx"] = outs[0].reshape(c_ctx.shape)
    for k, n in enumerate(small_names):
        delta[n], new_m[n], new_v[n] = (o.reshape(weights[n].shape) for o in outs[1 + 3 * k : 4 + 3 * k])
    for k, n in enumerate(["w_ffn1_in", "w_ffn1_out", "w_in", "w_out", "w_ffn2_in", "w_ffn2_out"]):
        turn = (lambda t: jnp.swapaxes(t, 1, 2)) if k % 2 == 0 else (lambda t: t)
        wmv = [turn(t) for t in (weights[n], *moms[n])]
        outs = None
        for l in reversed(range(nl)):
            outs = _sum_adamw(*big[l][k], pos, *wmv, l, outs, name=f"adamw_{n}_{l}")
        grads[n], delta[n], new_m[n], new_v[n] = (turn(o) for o in outs)

    return (loss, grad_x, *[grads[n] for n in order], *[delta[n] for n in order],
            *[new_m[n] for n in order], *[new_v[n] for n in order])


def _merge(*rounds):
    ins, outs, plan, local, n_alias = [], [], [], [], 0
    for r in rounds:
        assert r.n_alias == 0 or (not ins and r.n_alias == len(r.ins) == len(r.out_shapes))
        oi, oo = len(ins), len(outs)
        plan += [(k, None if i is None else i + oi, sf, o + oo, df) for k, i, sf, o, df in r.plan]
        local += [(i + oi, sf, o + oo, df) for i, sf, o, df in r.local_plan]
        ins += r.ins
        outs += r.out_shapes
        n_alias += r.n_alias
    return _Round(ins, outs, plan, local, n_alias)


def _forward_backward(h, target, modv, gvec, shards, w_first, cos, sin, sink, w_pool, ps2, norm_final, pos, *, T):
    nl = len(gvec)
    flat = lambda ws: [w.reshape(-1, D) for w in ws]
    saved = []
    w1, wm = flat(w_first[:2]), flat(w_first[2:])
    for l in range(nl):
        last = l == nl - 1
        h0 = h
        if l == 0:
            (h1, a1, b1, f1), got = _ffn_fwd(h0, modv[l], gvec[l], *w1, T=T, mrow=0, grow=0, ctx_active=True,
                                             name=f"ffn1_fwd_{l}", carry=_gather_a(shards[l][2]))
            (u, q, k4, v4), got = _mixproj_fwd(h1, modv[l], gvec[l], wm[0], cos, sin, T=T, name=f"mixproj_fwd_{l}",
                                               carry=_gather_b(got))
            w2 = flat(got)
        else:
            (h1, a1, b1, f1), got = _ffn_fwd(h0, modv[l], gvec[l], *w1, T=T, mrow=0, grow=0, ctx_active=True,
                                             name=f"ffn1_fwd_{l}", carry=_gather_b(nxt_m + nxt_2))
            wm, w2 = flat(got[:2]), flat(got[2:])
            (u, q, k4, v4), _ = _mixproj_fwd(h1, modv[l], gvec[l], wm[0], cos, sin, T=T, name=f"mixproj_fwd_{l}")
        (cat,), nxt_1 = _attnpool_fwd(u, q, k4, v4, sink[l], w_pool[l], ps2[l], T=T, name=f"attnpool_fwd_{l}",
                                      carry=None if last else _gather_a(shards[l + 1][0]))
        (h2, mo), nxt_m = _mixout_fwd(h1, cat, modv[l], wm[1], T=T, ctx_active=not last, name=f"mixout_fwd_{l}",
                                      carry=None if last else _gather_a(shards[l + 1][1]))
        (h3, a2, b2, f2), got = _ffn_fwd(h2, modv[l], gvec[l], *w2, T=T, mrow=6, grow=2, ctx_active=not last,
                                         name=f"ffn2_fwd_{l}",
                                         carry=None if last else _merge(_gather_b(nxt_1), _gather_a(shards[l + 1][2])))
        saved.append((h0, a1, b1, f1, h1, u, q, k4, v4, cat, mo, h2, a2, b2, f2, w1, wm, w2))
        h = h3
        if not last:
            w1, nxt_2 = flat(got[:2]), got[2:]

    dh, loss_part, dnf = _loss_head(h, target, norm_final[None], T=T, name="loss_head")

    adds = functools.partial(_adds, pos=pos)
    small, big = [None] * nl, {}
    prev = None
    for l in reversed(range(nl)):
        last = l == nl - 1
        h0, a1, b1, f1, h1, u, q, k4, v4, cat, mo, h2, a2, b2, f2, w1, wm, w2 = saved[l]
        (dh, dab, s, n, df, pk2), got = _ffn_bwd(
            h2, dh, a2, b2, f2, modv[l], gvec[l], *w2, T=T, mrow=6, grow=2, ctx_active=not last, name=f"ffn2_bwd_{l}",
            carry=_merge(_scatter_1(prev[0]), _gather_a(prev[1])) if prev else None)
        if prev:
            c1, small_a = adds(f"ffn1_{l + 1}", prev[0], got[:2]), got[2:]
        g_w2i, got = _wgrad(dab, n, bk=WG_BK, sh=2 * DFF // NDEV, name=f"wgrad_ffn2_in_{l}",
                            carry=_scatter_2(c1[:1]) if prev else None)
        if prev:
            big[l + 1][0] = (c1[0], got[0])
        g_w2o, got = _wgrad(s, df, bk=WG_BK, sh=DFF // NDEV, name=f"wgrad_ffn2_out_{l}",
                            carry=_scatter_2(c1[1:]) if prev else None)
        if prev:
            big[l + 1][1] = (c1[1], got[0])
        rnd = _scatter_1([g_w2i, g_w2o])
        (dcat, dmix, pko), got = _mixout_bwd(dh, mo, modv[l], wm[1], T=T, ctx_active=not last, name=f"mixout_bwd_{l}",
                                             carry=_merge(_gather_b(small_a), rnd) if prev else rnd)
        if prev:
            small[l + 1], got = got[: len(small_a)], got[len(small_a) :]
        c2 = adds(f"ffn2_{l}", [g_w2i, g_w2o], got)
        g_wo, _ = _wgrad(cat, dmix, bk=D, sh=D // NDEV, name=f"wgrad_out_{l}")
        dps, dwp, dsc = _pool_bwd(u, dcat, w_pool[l], ps2[l], T=T, name=f"pool_bwd_{l}")
        (du, dq, dk, dv, dsk), got = _attn_bwd(q, k4, v4, dcat, dps, sink[l], T=T, name=f"attn_bwd_{l}", carry=_scatter_2(c2))
        big[l] = [None, None, None, None, (c2[0], got[0]), (c2[1], got[1])]
        dh, dproj, n, pkm = _mixproj_bwd(h1, dh, du, dq, dk, dv, modv[l], gvec[l], wm[0], cos, sin, T=T, name=f"mixproj_bwd_{l}")
        g_wi, _ = _wgrad(dproj, n, bk=PROJ, sh=PROJ // NDEV, name=f"wgrad_in_{l}")
        (dh, dab, s, n, df, pk1), got = _ffn_bwd(h0, dh, a1, b1, f1, modv[l], gvec[l], *w1, T=T, mrow=0, grow=0,
                                                 ctx_active=True, name=f"ffn1_bwd_{l}", carry=_scatter_1([g_wi, g_wo]))
        cm = adds(f"mix_{l}", [g_wi, g_wo], got)
        mine = [pk1, pkm + pko, pk2, dwp, dsc, dsk]
        rnd = _merge(_scatter_2(cm), _gather_a(mine + [dnf, loss_part])) if l == 0 else _scatter_2(cm)
        g_w1i, got = _wgrad(dab, n, bk=WG_BK, sh=2 * DFF // NDEV, name=f"wgrad_ffn1_in_{l}", carry=rnd)
        big[l][2:4] = [(cm[0], got[0]), (cm[1], got[1])]
        if l > 0:
            g_w1o, _ = _wgrad(s, df, bk=WG_BK, sh=DFF // NDEV, name=f"wgrad_ffn1_out_{l}")
            prev = ([g_w1i, g_w1o], mine)
    (c1i,) = adds("ffn1_in_0", [g_w1i], _exchange("rs1_ffn1_in_0", _scatter_1([g_w1i])))
    g_w1o, got = _wgrad(s, df, bk=WG_BK, sh=DFF // NDEV, name="wgrad_ffn1_out_0",
                        carry=_merge(_gather_b(got[2:]), _scatter_2([c1i])))
    small[0], nf_all, loss_all = got[:6], got[6], got[7]
    big[0][0] = (c1i, got[8])
    return loss_all, dh, small, nf_all, big, g_w1o
```

```python
import functools

import jax
import jax.numpy as jnp
from jax import lax
from jax.experimental import pallas as pl
from jax.experimental.pallas import tpu as pltpu

F32, BF16 = jnp.float32, jnp.bfloat16

D = 1024
LC = 256
DFF = 2816
NMOD = 9
PW = 512
AW = 512
KVW = 128
PROJ = PW + AW + 2 * KVW
HD = 64
BLK = 128
GRID_W = 64
POOL_WINDOWS = (2, 4, 8, 16)
EPS = 1e-6
NEG = -1e30
ROPE_BASE = 10000.0
NDEV = 8
MESH = pl.DeviceIdType.MESH

ADAM_LR, ADAM_B1, ADAM_B2, ADAM_EPS, ADAM_WD, ADAM_STEP = 0.001, 0.9, 0.999, 1e-08, 0.01, 10

VMEM_LIMIT = 56 * 1024 * 1024
TM = 256
FFN_CHUNKS = ((0, 1536), (1536, 1280))
WG_BK = 1408

ANY = pl.BlockSpec(memory_space=pl.ANY)
SMEM = pl.BlockSpec(memory_space=pltpu.SMEM)


def _params(ngrid=1):
    return pltpu.CompilerParams(dimension_semantics=("arbitrary",) * ngrid, vmem_limit_bytes=VMEM_LIMIT)


def _dot(a, b):
    return jnp.dot(a, b, preferred_element_type=F32)


def _dot_nt(a, b):
    return lax.dot_general(a, b, (((1,), (1,)), ((), ())), preferred_element_type=F32)


def _dot_tn(a, b):
    return lax.dot_general(a, b, (((0,), (0,)), ((), ())), preferred_element_type=F32)


def _sigmoid(x):
    return 1.0 / (1.0 + jnp.exp(-x))


def _rows(tm, w):
    return pl.BlockSpec((tm, w), lambda i: (i, 0))


def _full(shape):
    nd = len(shape)
    return pl.BlockSpec(shape, lambda *_: (0,) * nd)


def _sds(shape, dtype):
    return jax.ShapeDtypeStruct(shape, dtype)


def _norm_mod(h, g, shift, scale):
    r = lax.rsqrt(jnp.mean(h * h, axis=-1, keepdims=True) + EPS)
    xhat = h * r
    y = xhat * g
    return r, xhat, y, y * (1.0 + scale) + shift


def _norm_mod_bwd(dn, r, xhat, y, g, scale):
    dshift = jnp.sum(dn, axis=0, keepdims=True)
    dscale = jnp.sum(dn * y, axis=0, keepdims=True)
    dy = dn * (1.0 + scale)
    dg = jnp.sum(dy * xhat, axis=0, keepdims=True)
    dxh = dy * g
    dh = r * (dxh - xhat * jnp.mean(dxh * xhat, axis=-1, keepdims=True))
    return dh, dshift, dscale, dg


def _acc_partials(part_ref, first, rows):
    @pl.when(first)
    def _():
        part_ref[...] = jnp.zeros_like(part_ref)

    for r, val in rows.items():
        part_ref[0, r : r + 1, :] += val


def _mod_spec(n_lat):
    return pl.BlockSpec((1, 16, D), lambda i: (i // n_lat, 0, 0))


def _part_spec(n_lat):
    return pl.BlockSpec((1, 8, D), lambda i: (i // n_lat, 0, 0))


def _load_weights(pairs, sem):
    copies = [pltpu.make_async_copy(src, dst, sem.at[k]) for k, (src, dst) in enumerate(pairs)]
    for cp in copies:
        cp.start()
    for cp in copies:
        cp.wait()


def _ffn_weight_copies(win_hbm, wout_hbm, win_v, wout_v, sem):
    loads = []
    for k, (c0, cw) in enumerate(FFN_CHUNKS):
        slabs = [(win_hbm, win_v, c0), (win_hbm, win_v, DFF + c0), (wout_hbm, wout_v, c0)]
        loads.append([pltpu.make_async_copy(src.at[pl.ds(r0, cw)], dst.at[pl.ds(r0, cw)], sem.at[3 * k + j])
                      for j, (src, dst, r0) in enumerate(slabs)])
    return loads


def _ffn_steps(i, n_active, loads, compute):
    @pl.when(i == 0)
    def _():
        for cp in sum(loads, []):
            cp.start()
        compute(loads)

    @pl.when(jnp.logical_and(i > 0, i < n_active))
    def _():
        compute(None)


def _wait_chunk(loads, k):
    if loads is not None:
        for cp in loads[k]:
            cp.wait()


def _ffn_fwd(h, modv, gvec, win, wout, *, T, mrow, grow, ctx_active, name, carry=None):
    R = h.shape[0]
    n_lat, n_tiles = T // TM, R // TM
    n_active = n_tiles if ctx_active else n_lat

    def body(h_ref, mod_ref, g_ref, win_hbm, wout_hbm, ho_ref, a_ref, b_ref, f_ref, win_v, wout_v, sem):
        i = pl.program_id(0)

        def compute(loads):
            h = h_ref[...]
            shift, scale, gate = (mod_ref[0, mrow + k : mrow + k + 1, :] for k in range(3))
            _, _, _, n = _norm_mod(h, g_ref[grow : grow + 1, :], shift, scale)
            n_bf = n.astype(BF16)
            acc = jnp.zeros((TM, D), F32)
            for k, (c0, cw) in enumerate(FFN_CHUNKS):
                _wait_chunk(loads, k)
                a = _dot_nt(n_bf, win_v[c0 : c0 + cw, :])
                b = _dot_nt(n_bf, win_v[DFF + c0 : DFF + c0 + cw, :])
                a_ref[:, c0 : c0 + cw] = a.astype(BF16)
                b_ref[:, c0 : c0 + cw] = b.astype(BF16)
                s = a * _sigmoid(a) * b
                acc = acc + _dot(s.astype(BF16), wout_v[c0 : c0 + cw, :])
            f_ref[...] = acc.astype(BF16)
            ho_ref[...] = h + (0.5 * gate) * acc

        _ffn_steps(i, n_active, _ffn_weight_copies(win_hbm, wout_hbm, win_v, wout_v, sem), compute)

        @pl.when(i >= n_active)
        def _():
            ho_ref[...] = h_ref[...]
            a_ref[...] = jnp.zeros_like(a_ref)
            b_ref[...] = jnp.zeros_like(b_ref)
            f_ref[...] = jnp.zeros_like(f_ref)

    return _call(
        body,
        name=name,
        grid=(n_tiles,),
        in_specs=[_rows(TM, D), _mod_spec(n_lat), _full((8, D)), ANY, ANY],
        out_specs=[_rows(TM, D), _rows(TM, DFF), _rows(TM, DFF), _rows(TM, D)],
        out_shape=[_sds((R, D), F32), _sds((R, DFF), BF16), _sds((R, DFF), BF16), _sds((R, D), BF16)],
        scratch_shapes=[pltpu.VMEM((2 * DFF, D), BF16), pltpu.VMEM((DFF, D), BF16),
                        pltpu.SemaphoreType.DMA((3 * len(FFN_CHUNKS),))],
        args=(h, modv, gvec, win, wout),
        carry=carry,
    )


def _ffn_bwd(h, dho, a, b, f, modv, gvec, win, wout, *, T, mrow, grow, ctx_active, name, carry=None):
    R = h.shape[0]
    n_lat, n_tiles = T // TM, R // TM
    n_active = n_tiles if ctx_active else n_lat

    def body(h_ref, dho_ref, a_ref, b_ref, f_ref, mod_ref, g_ref, win_hbm, wout_hbm,
             dh_ref, dab_ref, s_ref, n_ref, df_ref, part_ref, win_v, wout_v, sem):
        i = pl.program_id(0)
        first = jnp.logical_or(i == 0, i == n_lat)

        def compute(loads):
            h = h_ref[...]
            dho = dho_ref[...]
            shift, scale, gate = (mod_ref[0, mrow + k : mrow + k + 1, :] for k in range(3))
            g = g_ref[grow : grow + 1, :]
            r, xhat, y, n = _norm_mod(h, g, shift, scale)
            dgate = 0.5 * jnp.sum(dho * f_ref[...].astype(F32), axis=0, keepdims=True)
            df_bf = ((0.5 * gate) * dho).astype(BF16)
            df_ref[...] = df_bf
            n_ref[...] = n.astype(BF16)
            dn = jnp.zeros((TM, D), F32)
            for k, (c0, cw) in enumerate(FFN_CHUNKS):
                _wait_chunk(loads, k)
                ds = _dot_nt(df_bf, wout_v[c0 : c0 + cw, :])
                av = a_ref[:, c0 : c0 + cw].astype(F32)
                bv = b_ref[:, c0 : c0 + cw].astype(F32)
                sig = _sigmoid(av)
                sa = av * sig
                s_ref[:, c0 : c0 + cw] = (sa * bv).astype(BF16)
                da = (ds * bv * (sig * (1.0 + av * (1.0 - sig)))).astype(BF16)
                db = (ds * sa).astype(BF16)
                dab_ref[:, c0 : c0 + cw] = da
                dab_ref[:, DFF + c0 : DFF + c0 + cw] = db
                dn = dn + _dot(da, win_v[c0 : c0 + cw, :]) + _dot(db, win_v[DFF + c0 : DFF + c0 + cw, :])
            dh, dshift, dscale, dg = _norm_mod_bwd(dn, r, xhat, y, g, scale)
            dh_ref[...] = dho + dh
            _acc_partials(part_ref, first, {0: dshift, 1: dscale, 2: dgate, 3: dg})

        _ffn_steps(i, n_active, _ffn_weight_copies(win_hbm, wout_hbm, win_v, wout_v, sem), compute)

        @pl.when(i >= n_active)
        def _():
            dh_ref[...] = dho_ref[...]
            dab_ref[...] = jnp.zeros_like(dab_ref)
            s_ref[...] = jnp.zeros_like(s_ref)
            n_ref[...] = jnp.zeros_like(n_ref)
            df_ref[...] = jnp.zeros_like(df_ref)
            part_ref[...] = jnp.zeros_like(part_ref)

    return _call(
        body,
        name=name,
        grid=(n_tiles,),
        in_specs=[_rows(TM, D), _rows(TM, D), _rows(TM, DFF), _rows(TM, DFF), _rows(TM, D),
                  _mod_spec(n_lat), _full((8, D)), ANY, ANY],
        out_specs=[_rows(TM, D), _rows(TM, 2 * DFF), _rows(TM, DFF), _rows(TM, D), _rows(TM, D), _part_spec(n_lat)],
        out_shape=[_sds((R, D), F32), _sds((R, 2 * DFF), BF16), _sds((R, DFF), BF16), _sds((R, D), BF16),
                   _sds((R, D), BF16), _sds((2, 8, D), F32)],
        scratch_shapes=[pltpu.VMEM((2 * DFF, D), BF16), pltpu.VMEM((DFF, D), BF16),
                        pltpu.SemaphoreType.DMA((3 * len(FFN_CHUNKS),))],
        args=(h, dho, a, b, f, modv, gvec, win, wout),
        carry=carry,
    )


def _wgrad(x, y, *, bk, sh, name, carry=None):
    R, kx = x.shape
    n = y.shape[1]
    tr = R // 2
    nr, nsh = R // tr, bk // sh

    def body(x_ref, y_ref, o_ref, acc):
        r = pl.program_id(1)

        @pl.when(r == 0)
        def _():
            acc[...] = jnp.zeros_like(acc)

        acc[...] += _dot_tn(x_ref[...], y_ref[...])

        @pl.when(r == nr - 1)
        def _():
            for s in range(nsh):
                o_ref[s] = acc[s * sh : (s + 1) * sh, :].astype(BF16)

    (out,), got = _call(
        body,
        name=name,
        grid=(kx // bk, nr),
        in_specs=[pl.BlockSpec((tr, bk), lambda k, r: (r, k)), pl.BlockSpec((tr, n), lambda k, r: (r, 0))],
        out_specs=[pl.BlockSpec((nsh, sh, n), lambda k, r: (k, 0, 0))],
        out_shape=[_sds((kx // sh, sh, n), BF16)],
        scratch_shapes=[pltpu.VMEM((bk, n), F32)],
        args=(x, y),
        carry=carry,
    )
    return out, got


def _rot_half(x):
    lane = lax.broadcasted_iota(jnp.int32, x.shape, 1)
    return jnp.where((lane & (HD - 1)) < HD // 2, -pltpu.roll(x, 128 - HD // 2, 1), pltpu.roll(x, HD // 2, 1))


def _tile_sel():
    i = lax.broadcasted_iota(jnp.int32, (KVW, AW), 0)
    j = lax.broadcasted_iota(jnp.int32, (KVW, AW), 1)
    return jnp.where(i == (j // 256) * HD + (j & (HD - 1)), 1.0, 0.0).astype(BF16)


def _mixproj_fwd(h, modv, gvec, win, cos, sin, *, T, name, carry=None):
    R = h.shape[0]
    n_lat, n_tiles = T // TM, R // TM

    def body(h_ref, mod_ref, g_ref, win_ref, cos_ref, sin_ref, u_ref, q_ref, k4_ref, v4_ref):
        shift, scale = mod_ref[0, 3:4, :], mod_ref[0, 4:5, :]
        _, _, _, n = _norm_mod(h_ref[...], g_ref[1:2, :], shift, scale)
        proj = _dot_nt(n.astype(BF16), win_ref[...])
        u_ref[...] = proj[:, :PW]
        cs, sn = cos_ref[...], sin_ref[...]
        for s in range(AW // 128):
            x = proj[:, PW + 128 * s : PW + 128 * (s + 1)]
            q_ref[:, 128 * s : 128 * (s + 1)] = ((x * cs + _rot_half(x) * sn) * (HD ** -0.5)).astype(BF16)
        k = proj[:, PW + AW : PW + AW + KVW]
        k = (k * cs + _rot_half(k) * sn).astype(BF16)
        v = proj[:, PW + AW + KVW :].astype(BF16)
        sel = _tile_sel()
        k4_ref[...] = _dot(k, sel).astype(BF16)
        v4_ref[...] = _dot(v, sel).astype(BF16)

    return _call(
        body,
        name=name,
        grid=(n_tiles,),
        in_specs=[_rows(TM, D), _mod_spec(n_lat), _full((8, D)), _full((PROJ, D)), _rows(TM, 128), _rows(TM, 128)],
        out_specs=[_rows(TM, PW), _rows(TM, AW), _rows(TM, AW), _rows(TM, AW)],
        out_shape=[_sds((R, PW), F32), _sds((R, AW), BF16), _sds((R, AW), BF16), _sds((R, AW), BF16)],
        scratch_shapes=[],
        args=(h, modv, gvec, win, cos, sin),
        carry=carry,
    )


def _win_start(j, hi):
    return pl.multiple_of(jnp.clip((j - 1) * BLK, 0, hi - 3 * BLK), BLK)


def _hi_lo(x):
    hi = x.astype(BF16)
    return hi, (x - hi.astype(F32)).astype(BF16)


def _pool_bounds(t, w, T, R):
    is_ctx = t >= T
    lo = jnp.maximum(t - w // 2, jnp.where(is_ctx, T, 0))
    hi = jnp.minimum(t + w // 2, jnp.where(is_ctx, R, T))
    return lo, hi


def _pooled(u_v, j, T, R):
    start = _win_start(j, R)
    u3_hi, u3_lo = _hi_lo(u_v[pl.ds(start, 3 * BLK), :])
    ub = u_v[pl.ds(pl.multiple_of(j * BLK, BLK), BLK), :]
    t = j * BLK + lax.broadcasted_iota(jnp.int32, (BLK, 1), 0)
    pos = start + lax.broadcasted_iota(jnp.int32, (1, 3 * BLK), 1)
    pooled, counts = [], []
    for g, w in enumerate(POOL_WINDOWS):
        lo, hi = _pool_bounds(t, w, T, R)
        band = jnp.where(pos >= lo, jnp.where(pos < hi, 1.0, 0.0), 0.0).astype(BF16)
        sl = slice(g * 128, (g + 1) * 128)
        sums = _dot(band, u3_hi[:, sl]) + _dot(band, u3_lo[:, sl])
        cnt = (hi - lo).astype(F32)
        pooled.append(sums / cnt - ub[:, sl])
        counts.append(cnt)
    return pooled, counts


def _stack_heads(x):
    lane_h = lax.broadcasted_iota(jnp.int32, x.shape, 1) // HD
    return jnp.concatenate([jnp.where(lane_h == h, x, jnp.zeros_like(x)) for h in range(4)], axis=0)


def _unstack_heads(x):
    lane_h = lax.broadcasted_iota(jnp.int32, (BLK, 256), 1) // HD
    out = jnp.zeros((BLK, 256), F32)
    for h in range(4):
        out = out + jnp.where(lane_h == h, x[h * BLK : (h + 1) * BLK, :], 0.0)
    return out


def _window_mask(j, start_l, nbl):
    rowi = lax.broadcasted_iota(jnp.int32, (4 * BLK, 1), 0)
    qpos = j * BLK + (rowi & (BLK - 1))
    kpos = start_l + lax.broadcasted_iota(jnp.int32, (1, 3 * BLK), 1)
    reach = jnp.where(j < nbl, BLK, -1)
    return jnp.abs(kpos - qpos) <= reach


def _attn_exps(qs, kl, kc, sink_ref, g, valid):
    s_l = jnp.where(valid, _dot_nt(qs, kl), NEG)
    s_c = _dot_nt(qs, kc)
    rb = lax.broadcasted_iota(jnp.int32, (4 * BLK, 1), 0) // BLK
    sk = jnp.where(rb == 0, sink_ref[4 * g], jnp.where(rb == 1, sink_ref[4 * g + 1],
                   jnp.where(rb == 2, sink_ref[4 * g + 2], sink_ref[4 * g + 3])))
    m = jnp.maximum(jnp.maximum(jnp.max(s_l, axis=1, keepdims=True), jnp.max(s_c, axis=1, keepdims=True)), sk)
    e_l, e_c, e_s = jnp.exp(s_l - m), jnp.exp(s_c - m), jnp.exp(sk - m)
    inv = 1.0 / (jnp.sum(e_l, axis=1, keepdims=True) + jnp.sum(e_c, axis=1, keepdims=True) + e_s)
    return e_l, e_c, e_s, inv


def _attnpool_fwd(u, q, k4, v4, sink, w_pool, pool_scale, *, T, name, carry=None):
    R = u.shape[0]
    nb, nbl = R // BLK, T // BLK

    def body(q_ref, sink_ref, wp_ref, ps_ref, u_hbm, k4_hbm, v4_hbm, cat_ref, u_v, k4_v, v4_v, sem):
        j = pl.program_id(0)

        @pl.when(j == 0)
        def _():
            _load_weights([(u_hbm, u_v), (k4_hbm, k4_v), (v4_hbm, v4_v)], sem)

        pooled, _ = _pooled(u_v, j, T, R)
        for g in range(4):
            mixed = _dot(pooled[g].astype(BF16), wp_ref[g].astype(BF16)) * ps_ref[:, g * 128 : (g + 1) * 128]
            cat_ref[:, g * 128 : (g + 1) * 128] = mixed.astype(BF16)

        start_l = _win_start(j, T)
        valid = _window_mask(j, start_l, nbl)
        for g in range(2):
            gl = slice(g * 256, (g + 1) * 256)
            qs = _stack_heads(q_ref[:, gl])
            e_l, e_c, _, inv = _attn_exps(qs, k4_v[pl.ds(start_l, 3 * BLK), gl], k4_v[T:R, gl], sink_ref, g, valid)
            o = _dot(e_l.astype(BF16), v4_v[pl.ds(start_l, 3 * BLK), gl]) + _dot(e_c.astype(BF16), v4_v[T:R, gl])
            cat_ref[:, PW + g * 256 : PW + (g + 1) * 256] = _unstack_heads(o * inv).astype(BF16)

    return _call(
        body,
        name=name,
        grid=(nb,),
        in_specs=[_rows(BLK, AW), SMEM, _full((4, 128, 128)), _full((1, PW)), ANY, ANY, ANY],
        out_specs=[_rows(BLK, D)],
        out_shape=[_sds((R, D), BF16)],
        scratch_shapes=[pltpu.VMEM((R, PW), F32), pltpu.VMEM((R, AW), BF16), pltpu.VMEM((R, AW), BF16),
                        pltpu.SemaphoreType.DMA((3,))],
        args=(q, sink, w_pool, pool_scale, u, k4, v4),
        carry=carry,
    )


def _mixout_fwd(h, cat, modv, wout, *, T, ctx_active, name, carry=None):
    R = h.shape[0]
    n_lat, n_tiles = T // TM, R // TM

    def body(h_ref, cat_ref, mod_ref, w_ref, ho_ref, mo_ref):
        i = pl.program_id(0)

        def compute():
            mo = _dot(cat_ref[...], w_ref[...])
            mo_ref[...] = mo.astype(BF16)
            ho_ref[...] = h_ref[...] + mod_ref[0, 5:6, :] * mo

        if ctx_active:
            compute()
        else:
            pl.when(i < n_lat)(compute)

            @pl.when(i >= n_lat)
            def _():
                ho_ref[...] = h_ref[...]
                mo_ref[...] = jnp.zeros_like(mo_ref)

    return _call(
        body,
        name=name,
        grid=(n_tiles,),
        in_specs=[_rows(TM, D), _rows(TM, D), _mod_spec(n_lat), _full((D, D))],
        out_specs=[_rows(TM, D), _rows(TM, D)],
        out_shape=[_sds((R, D), F32), _sds((R, D), BF16)],
        scratch_shapes=[],
        args=(h, cat, modv, wout),
        carry=carry,
    )


def _mixout_bwd(dho, mo, modv, wout, *, T, ctx_active, name, carry=None):
    R = dho.shape[0]
    n_lat, n_tiles = T // TM, R // TM

    def body(dho_ref, mo_ref, mod_ref, w_ref, dcat_ref, dmix_ref, part_ref):
        i = pl.program_id(0)
        first = jnp.logical_or(i == 0, i == n_lat)

        def compute():
            dho = dho_ref[...]
            dmix = (mod_ref[0, 5:6, :] * dho).astype(BF16)
            dmix_ref[...] = dmix
            dcat_ref[...] = _dot_nt(dmix, w_ref[...])
            dgate = jnp.sum(dho * mo_ref[...].astype(F32), axis=0, keepdims=True)
            _acc_partials(part_ref, first, {2: dgate})

        if ctx_active:
            compute()
        else:
            pl.when(i < n_lat)(compute)

            @pl.when(i >= n_lat)
            def _():
                dcat_ref[...] = jnp.zeros_like(dcat_ref)
                dmix_ref[...] = jnp.zeros_like(dmix_ref)
                part_ref[...] = jnp.zeros_like(part_ref)

    return _call(
        body,
        name=name,
        grid=(n_tiles,),
        in_specs=[_rows(TM, D), _rows(TM, D), _mod_spec(n_lat), _full((D, D))],
        out_specs=[_rows(TM, D), _rows(TM, D), _part_spec(n_lat)],
        out_shape=[_sds((R, D), F32), _sds((R, D), BF16), _sds((2, 8, D), F32)],
        scratch_shapes=[],
        args=(dho, mo, modv, wout),
        carry=carry,
    )


def _pool_bwd(u, dcat, w_pool, pool_scale, *, T, name):
    R = u.shape[0]
    nb = R // BLK

    def body(dcat_ref, wp_ref, ps_ref, u_hbm, dps_ref, dwp_ref, dsc_ref, u_v, sem):
        j = pl.program_id(0)

        @pl.when(j == 0)
        def _():
            _load_weights([(u_hbm, u_v)], sem)
            dwp_ref[...] = jnp.zeros_like(dwp_ref)
            dsc_ref[...] = jnp.zeros_like(dsc_ref)

        pooled, counts = _pooled(u_v, j, T, R)
        for g in range(4):
            sl = slice(g * 128, (g + 1) * 128)
            p_bf = pooled[g].astype(BF16)
            w_bf = wp_ref[g].astype(BF16)
            dmixed = dcat_ref[:, sl]
            dsc_ref[0:1, sl] += jnp.sum(dmixed * _dot(p_bf, w_bf), axis=0, keepdims=True)
            dmp = (dmixed * ps_ref[:, sl]).astype(BF16)
            dwp_ref[sl, :] += _dot_tn(p_bf, dmp)
            dps_ref[:, sl] = _dot_nt(dmp, w_bf) / counts[g]

    return pl.pallas_call(
        body,
        name=name,
        grid=(nb,),
        in_specs=[_rows(BLK, D), _full((4, 128, 128)), _full((1, PW)), ANY],
        out_specs=[_rows(BLK, PW), _full((PW, 128)), _full((8, PW))],
        out_shape=[_sds((R, PW), F32), _sds((PW, 128), F32), _sds((8, PW), F32)],
        scratch_shapes=[pltpu.VMEM((R, PW), F32), pltpu.SemaphoreType.DMA((1,))],
        compiler_params=_params(),
    )(dcat, w_pool, pool_scale, u)


def _fold_heads(x):
    y = x[:, :128] + x[:, 128:]
    return y + pltpu.roll(y, HD, 1)


def _attn_bwd(q, k4, v4, dcat, dps, sink, *, T, name, carry=None):
    R = q.shape[0]
    nb, nbl = R // BLK, T // BLK

    def body(q_ref, dcat_ref, sink_ref, k4_hbm, v4_hbm, dps_hbm, du_ref, dq_ref, dk_ref, dv_ref, dsk_ref,
             k4_v, v4_v, dps_v, sem):
        j = pl.program_id(0)

        @pl.when(j == 0)
        def _():
            _load_weights([(k4_hbm, k4_v), (v4_hbm, v4_v), (dps_hbm, dps_v)], sem)
            dk_ref[...] = jnp.zeros_like(dk_ref)
            dv_ref[...] = jnp.zeros_like(dv_ref)
            dsk_ref[...] = jnp.zeros_like(dsk_ref)

        start = _win_start(j, R)
        d3_hi, d3_lo = _hi_lo(dps_v[pl.ds(start, 3 * BLK), :])
        db = dps_v[pl.ds(pl.multiple_of(j * BLK, BLK), BLK), :]
        pos = j * BLK + lax.broadcasted_iota(jnp.int32, (BLK, 1), 0)
        t_r = start + lax.broadcasted_iota(jnp.int32, (1, 3 * BLK), 1)
        for g, w in enumerate(POOL_WINDOWS):
            sl = slice(g * 128, (g + 1) * 128)
            lo_r, hi_r = _pool_bounds(t_r, w, T, R)
            band_t = jnp.where(pos >= lo_r, jnp.where(pos < hi_r, 1.0, 0.0), 0.0).astype(BF16)
            lo_c, hi_c = _pool_bounds(pos, w, T, R)
            du_ref[:, sl] = _dot(band_t, d3_hi[:, sl]) + _dot(band_t, d3_lo[:, sl]) - db[:, sl] * (hi_c - lo_c).astype(F32)

        start_l = _win_start(j, T)
        valid = _window_mask(j, start_l, nbl)
        rb = lax.broadcasted_iota(jnp.int32, (4 * BLK, 1), 0) // BLK
        lane = lax.broadcasted_iota(jnp.int32, (1, 128), 1)
        dk_l, dk_c, dv_l, dv_c = [], [], [], []
        for g in range(2):
            gl = slice(g * 256, (g + 1) * 256)
            qs = _stack_heads(q_ref[:, gl])
            kl, kc = k4_v[pl.ds(start_l, 3 * BLK), gl], k4_v[T:R, gl]
            vl, vc = v4_v[pl.ds(start_l, 3 * BLK), gl], v4_v[T:R, gl]
            e_l, e_c, e_s, inv = _attn_exps(qs, kl, kc, sink_ref, g, valid)
            p_l, p_c, p_s = e_l * inv, e_c * inv, e_s * inv
            dos = _stack_heads(dcat_ref[:, PW + g * 256 : PW + (g + 1) * 256]).astype(BF16)
            dp_l, dp_c = _dot_nt(dos, vl), _dot_nt(dos, vc)
            delta = jnp.sum(p_l * dp_l, axis=1, keepdims=True) + jnp.sum(p_c * dp_c, axis=1, keepdims=True)
            ds_l = (p_l * (dp_l - delta)).astype(BF16)
            ds_c = (p_c * (dp_c - delta)).astype(BF16)
            dq_ref[:, gl] = _unstack_heads(_dot(ds_l, kl) + _dot(ds_c, kc)) * (HD ** -0.5)
            dk_l.append(_fold_heads(_dot_tn(ds_l, qs)))
            dk_c.append(_fold_heads(_dot_tn(ds_c, qs)))
            dv_l.append(_fold_heads(_dot_tn(p_l.astype(BF16), dos)))
            dv_c.append(_fold_heads(_dot_tn(p_c.astype(BF16), dos)))
            dsink = -p_s * delta
            for h in range(4):
                tot = jnp.sum(jnp.where(rb == h, dsink, 0.0), axis=0, keepdims=True)
                dsk_ref[4 * g + h : 4 * g + h + 1, :] += jnp.broadcast_to(tot, (1, 128))
        first = lane < HD
        dk_ref[pl.ds(start_l, 3 * BLK), :] += jnp.where(first, dk_l[0], dk_l[1])
        dk_ref[T:R, :] += jnp.where(first, dk_c[0], dk_c[1])
        dv_ref[pl.ds(start_l, 3 * BLK), :] += jnp.where(first, dv_l[0], dv_l[1])
        dv_ref[T:R, :] += jnp.where(first, dv_c[0], dv_c[1])

    return _call(
        body,
        name=name,
        grid=(nb,),
        in_specs=[_rows(BLK, AW), _rows(BLK, D), SMEM, ANY, ANY, ANY],
        out_specs=[_rows(BLK, PW), _rows(BLK, AW), _full((R, KVW)), _full((R, KVW)), _full((8, 128))],
        out_shape=[_sds((R, PW), F32), _sds((R, AW), F32), _sds((R, KVW), F32), _sds((R, KVW), F32),
                   _sds((8, 128), F32)],
        scratch_shapes=[pltpu.VMEM((R, AW), BF16), pltpu.VMEM((R, AW), BF16), pltpu.VMEM((R, PW), F32),
                        pltpu.SemaphoreType.DMA((3,))],
        args=(q, dcat, sink, k4, v4, dps),
        carry=carry,
    )


def _mixproj_bwd(h, dho, du, dq, dk, dv, modv, gvec, win, cos, sin, *, T, name):
    R = h.shape[0]
    n_lat, n_tiles = T // TM, R // TM

    def body(h_ref, dho_ref, du_ref, dq_ref, dk_ref, dv_ref, mod_ref, g_ref, win_ref, cos_ref, sin_ref,
             dh_ref, dproj_ref, n_ref, part_ref):
        i = pl.program_id(0)
        first = jnp.logical_or(i == 0, i == n_lat)
        shift, scale = mod_ref[0, 3:4, :], mod_ref[0, 4:5, :]
        g = g_ref[1:2, :]
        r, xhat, y, n = _norm_mod(h_ref[...], g, shift, scale)
        n_ref[...] = n.astype(BF16)
        cs, sn = cos_ref[...], sin_ref[...]
        dproj_ref[:, :PW] = du_ref[...].astype(BF16)
        for s in range(AW // 128):
            x = dq_ref[:, 128 * s : 128 * (s + 1)]
            dproj_ref[:, PW + 128 * s : PW + 128 * (s + 1)] = (x * cs - _rot_half(x) * sn).astype(BF16)
        x = dk_ref[...]
        dproj_ref[:, PW + AW : PW + AW + KVW] = (x * cs - _rot_half(x) * sn).astype(BF16)
        dproj_ref[:, PW + AW + KVW :] = dv_ref[...].astype(BF16)
        dn = _dot(dproj_ref[...], win_ref[...])
        dh, dshift, dscale, dg = _norm_mod_bwd(dn, r, xhat, y, g, scale)
        dh_ref[...] = dho_ref[...] + dh
        _acc_partials(part_ref, first, {0: dshift, 1: dscale, 3: dg})

    return pl.pallas_call(
        body,
        name=name,
        grid=(n_tiles,),
        in_specs=[_rows(TM, D), _rows(TM, D), _rows(TM, PW), _rows(TM, AW), _rows(TM, KVW), _rows(TM, KVW),
                  _mod_spec(n_lat), _full((8, D)), _full((PROJ, D)), _rows(TM, 128), _rows(TM, 128)],
        out_specs=[_rows(TM, D), _rows(TM, PROJ), _rows(TM, D), _part_spec(n_lat)],
        out_shape=[_sds((R, D), F32), _sds((R, PROJ), BF16), _sds((R, D), BF16), _sds((2, 8, D), F32)],
        compiler_params=_params(),
    )(h, dho, du, dq, dk, dv, modv, gvec, win, cos, sin)


def _loss_head(h, target, g_final, *, T, name):
    R = h.shape[0]
    n_lat, n_tiles = T // TM, R // TM

    def body(h_ref, t_ref, g_ref, dh_ref, loss_ref, dg_ref):
        i = pl.program_id(0)

        @pl.when(i == 0)
        def _():
            loss_ref[...] = jnp.zeros_like(loss_ref)
            dg_ref[...] = jnp.zeros_like(dg_ref)

        @pl.when(i < n_lat)
        def _():
            h = h_ref[...]
            g = g_ref[...]
            r = lax.rsqrt(jnp.mean(h * h, axis=-1, keepdims=True) + EPS)
            xhat = h * r
            err = xhat * g - t_ref[...]
            tot = jnp.sum(jnp.sum(err * err, axis=1, keepdims=True), axis=0, keepdims=True)
            loss_ref[...] += jnp.broadcast_to(tot * (0.5 / D), loss_ref.shape)
            dy = err * (1.0 / D)
            dg_ref[0:1, :] += jnp.sum(dy * xhat, axis=0, keepdims=True)
            dxh = dy * g
            dh_ref[...] = r * (dxh - xhat * jnp.mean(dxh * xhat, axis=-1, keepdims=True))

        @pl.when(i >= n_lat)
        def _():
            dh_ref[...] = jnp.zeros_like(dh_ref)

    return pl.pallas_call(
        body,
        name=name,
        grid=(n_tiles,),
        in_specs=[_rows(TM, D), pl.BlockSpec((TM, D), lambda i: (jnp.minimum(i, n_lat - 1), 0)), _full((1, D))],
        out_specs=[_rows(TM, D), _full((8, 128)), _full((8, D))],
        out_shape=[_sds((R, D), F32), _sds((8, 128), F32), _sds((8, D), F32)],
        compiler_params=_params(),
    )(h, target, g_final)


def _mod_fwd(c16, w_mod, b_cols, *, name):
    nl, _, cols = w_mod.shape

    def body(c_ref, w_ref, b_ref, o_ref):
        c = c_ref[...]
        sc = (c * _sigmoid(c)).astype(BF16)
        o_ref[0] = _dot(sc, w_ref[0].astype(BF16)) + b_ref[0]

    return pl.pallas_call(
        body,
        name=name,
        grid=(nl,),
        in_specs=[_full((16, D)), pl.BlockSpec((1, D, cols), lambda l: (l, 0, 0)),
                  pl.BlockSpec((1, 1, cols), lambda l: (l, 0, 0))],
        out_specs=pl.BlockSpec((1, 16, cols), lambda l: (l, 0, 0)),
        out_shape=_sds((nl, 16, cols), F32),
        compiler_params=_params(),
    )(c16, w_mod, b_cols)


def _mod_bwd(c16, dm_cols, w_mod, *, name):
    nl, _, cols = w_mod.shape

    def body(c_ref, dm_ref, w_ref, gw_ref, dc_ref):
        c = c_ref[...]
        sc = (c * _sigmoid(c)).astype(BF16)
        dm = dm_ref[0].astype(BF16)
        gw_ref[0] = _dot_tn(sc, dm)
        dc_ref[0] = _dot_nt(dm, w_ref[0].astype(BF16))

    return pl.pallas_call(
        body,
        name=name,
        grid=(nl,),
        in_specs=[_full((16, D)), pl.BlockSpec((1, 16, cols), lambda l: (l, 0, 0)),
                  pl.BlockSpec((1, D, cols), lambda l: (l, 0, 0))],
        out_specs=[pl.BlockSpec((1, D, cols), lambda l: (l, 0, 0)), pl.BlockSpec((1, 16, D), lambda l: (l, 0, 0))],
        out_shape=[_sds((nl, D, cols), F32), _sds((nl, 16, D), F32)],
        compiler_params=_params(),
    )(c16, dm_cols, w_mod)


def _coords():
    return lax.axis_index("x"), lax.axis_index("y"), lax.axis_index("c")


FWD = 8


def _peer(k, x, y, c):
    if k == FWD:
        return (x ^ (1 - c), y ^ c, c)
    return (1 - x if k & 4 else x, 1 - y if k & 2 else y, 1 - c if k & 1 else c)


def _lin(p):
    return 4 * p[0] + 2 * p[1] + p[2]


def _view(ref, slot):
    return ref if slot is None else ref.at[slot]


class _Round:
    def __init__(self, ins, out_shapes, plan, local_plan=(), n_alias=0):
        self.ins, self.out_shapes = list(ins), list(out_shapes)
        self.plan, self.local_plan, self.n_alias = list(plan), list(local_plan), n_alias
        fed = {p[3] for p in self.plan if p[0] == FWD}
        self.feeders = [n for n, p in enumerate(self.plan) if p[0] in (2, 4, 6) and p[3] in fed]

    def sems(self):
        return [pltpu.SemaphoreType.DMA((len(self.plan),)), pltpu.SemaphoreType.DMA((len(self.plan),)),
                pltpu.SemaphoreType.DMA((max(len(self.local_plan), 1),))]

    def _remote(self, in_refs, out_refs, sems, incoming, pick):
        in_refs = list(out_refs[: self.n_alias]) + list(in_refs[self.n_alias :])
        x, y, c = _coords()
        me = _lin((x, y, c))
        copies = {}
        for idx, (k, ii, sfn, oi, dfn) in enumerate(self.plan):
            if not pick(idx, "d2d" if k == 1 else "fwd" if k == FWD else "ici"):
                continue
            peer = _peer(k, x, y, c)
            sender, receiver = (_lin(peer), me) if incoming else (me, _lin(peer))
            src = out_refs[oi] if ii is None else in_refs[ii]
            copies[idx] = pltpu.make_async_remote_copy(
                src_ref=_view(src, sfn(sender, receiver)), dst_ref=_view(out_refs[oi], dfn(sender, receiver)),
                send_sem=sems[0].at[idx], recv_sem=sems[1].at[idx], device_id=peer, device_id_type=MESH)
        return copies

    def _local(self, in_refs, out_refs, sems):
        in_refs = list(out_refs[: self.n_alias]) + list(in_refs[self.n_alias :])
        me = _lin(_coords())
        return [pltpu.make_async_copy(_view(in_refs[ii], sfn(me)), _view(out_refs[oi], dfn(me)), sems[2].at[idx])
                for idx, (ii, sfn, oi, dfn) in enumerate(self.local_plan)]

    def start(self, in_refs, out_refs, sems, links=("ici", "d2d")):
        for cp in self._remote(in_refs, out_refs, sems, False, lambda n, link: link in links).values():
            cp.start()
        if "ici" in links:
            for cp in self._local(in_refs, out_refs, sems):
                cp.start()

    def mid(self, in_refs, out_refs, sems):
        if self.feeders:
            for cp in self._remote(in_refs, out_refs, sems, True, lambda n, link: n in self.feeders).values():
                cp.wait_recv()
            for cp in self._remote(in_refs, out_refs, sems, False, lambda n, link: link == "fwd").values():
                cp.start()

    def finish(self, in_refs, out_refs, sems):
        for cp in self._remote(in_refs, out_refs, sems, True, lambda n, link: n not in self.feeders).values():
            cp.wait_recv()
        for cp in self._remote(in_refs, out_refs, sems, False, lambda n, link: True).values():
            cp.wait_send()
        for cp in self._local(in_refs, out_refs, sems):
            cp.wait()


def _exchange(name, rnd):
    n_in, n_out = len(rnd.ins), len(rnd.out_shapes)

    def body(*refs):
        in_refs, out_refs, sems = refs[:n_in], refs[n_in : n_in + n_out], refs[n_in + n_out :]
        rnd.start(in_refs, out_refs, sems)
        rnd.mid(in_refs, out_refs, sems)
        rnd.finish(in_refs, out_refs, sems)

    return pl.pallas_call(
        body, name=name, in_specs=[ANY] * n_in, out_specs=[ANY] * n_out, out_shape=rnd.out_shapes,
        scratch_shapes=rnd.sems(), input_output_aliases={i: i for i in range(rnd.n_alias)})(*rnd.ins)


def _call(body, *, name, grid, in_specs, out_specs, out_shape, scratch_shapes, args, carry=None):
    params = _params(len(grid))
    if carry is None:
        outs = pl.pallas_call(body, name=name, grid=grid, in_specs=in_specs, out_specs=out_specs, out_shape=out_shape,
                              scratch_shapes=scratch_shapes, compiler_params=params)(*args)
        return list(outs), []
    n_ci, n_co, n_cs = len(in_specs), len(out_shape), len(scratch_shapes)
    n_xi, n_xo = len(carry.ins), len(carry.out_shapes)

    def wrapped(*refs):
        ci, xi = refs[:n_ci], refs[n_ci : n_ci + n_xi]
        o0 = n_ci + n_xi
        co, xo = refs[o0 : o0 + n_co], refs[o0 + n_co : o0 + n_co + n_xo]
        s0 = o0 + n_co + n_xo
        cs, sems = refs[s0 : s0 + n_cs], refs[s0 + n_cs :]
        ids = [pl.program_id(a) for a in range(len(grid))]
        first = functools.reduce(jnp.logical_and, [i == 0 for i in ids])
        last = functools.reduce(jnp.logical_and, [i == g - 1 for i, g in zip(ids, grid)])

        @pl.when(first)
        def _():
            carry.start(xi, xo, sems, links=("ici",))

        if carry.feeders:
            step = functools.reduce(lambda acc, ig: acc * ig[1] + ig[0], zip(ids, grid), 0)
            n_steps = functools.reduce(lambda a, b: a * b, grid)

            @pl.when(step == min(n_steps - 1, (4 * n_steps) // 5))
            def _():
                carry.mid(xi, xo, sems)

        body(*ci, *co, *cs)

        @pl.when(first)
        def _():
            carry.start(xi, xo, sems, links=("d2d",))

        @pl.when(last)
        def _():
            carry.finish(xi, xo, sems)

    outs = pl.pallas_call(
        wrapped, name=name, grid=grid, in_specs=list(in_specs) + [ANY] * n_xi, out_specs=list(out_specs) + [ANY] * n_xo,
        out_shape=list(out_shape) + carry.out_shapes, scratch_shapes=list(scratch_shapes) + carry.sems(),
        input_output_aliases={n_ci + i: n_co + i for i in range(carry.n_alias)}, compiler_params=params,
    )(*args, *carry.ins)
    return list(outs[:n_co]), list(outs[n_co:])


def _gather_direct(arrays):
    na = len(arrays)
    outs = [_sds((NDEV,) + a.shape, a.dtype) for a in arrays]
    plan = [(k, i, lambda s, r: None, i, lambda s, r: s) for i in range(na) for k in range(1, NDEV)]
    return _Round(arrays, outs, plan, [(i, lambda m: None, i, lambda m: m) for i in range(na)])


def _gather_a(arrays):
    na = len(arrays)
    outs = [_sds((NDEV,) + a.shape, a.dtype) for a in arrays]
    plan = [(k, i, lambda s, r: None, i, lambda s, r: s) for i in range(na) for k in (2, 4)]
    handed = lambda s, r: s ^ (2 << (s & 1))
    plan += [(FWD, None, handed, i, handed) for i in range(na)]
    return _Round(arrays, outs, plan, [(i, lambda m: None, i, lambda m: m) for i in range(na)])


def _gather_b(got):
    na = len(got)
    plan = [(1, i, (lambda s, r, k=k: s ^ k), i, (lambda s, r, k=k: s ^ k)) for i in range(na) for k in (0, 2, 4, 6)]
    return _Round(got, [_sds(g.shape, g.dtype) for g in got], plan, n_alias=na)


def _scatter_1(grads):
    plan = [(1, i, (lambda s, r, q=q: 2 * q + (r & 1)), i, (lambda s, r, q=q: q))
            for i in range(len(grads)) for q in range(4)]
    return _Round(grads, [_sds((4,) + g.shape[1:], g.dtype) for g in grads], plan)


def _scatter_2(chip):
    plan = [(k, i, lambda s, r: r >> 1, i, (lambda s, r, j=j: j)) for i in range(len(chip)) for j, k in enumerate((2, 4, 6))]
    return _Round(chip, [_sds((3,) + g.shape[1:], g.dtype) for g in chip], plan)


def _add_pairs(grads, got, pos, *, name):
    n = len(grads)
    mine = lambda a: pl.BlockSpec((1,) + a.shape[1:], lambda q, p: (2 * q + p[0], 0, 0))
    slot = lambda a: pl.BlockSpec((1,) + a.shape[1:], lambda q, p: (q, 0, 0))

    def body(pos_ref, *refs):
        for g_ref, r_ref, o_ref in zip(refs[:n], refs[n : 2 * n], refs[2 * n :]):
            o_ref[...] = (g_ref[...].astype(F32) + r_ref[...].astype(F32)).astype(o_ref.dtype)

    return pl.pallas_call(
        body,
        name=name,
        grid_spec=pltpu.PrefetchScalarGridSpec(
            num_scalar_prefetch=1, grid=(4,),
            in_specs=[mine(g) for g in grads] + [slot(g) for g in grads],
            out_specs=[slot(g) for g in grads]),
        out_shape=[_sds((4,) + g.shape[1:], g.dtype) for g in grads],
        compiler_params=_params(),
    )(pos, *grads, *got)


def _sum_adamw(chip, got, pos, w, m, v, layer, prior, *, name):
    _, sh, wd = chip.shape
    nl, rows, cols = w.shape
    nb, blk = 2, (sh // 2, wd)
    part = lambda n: pl.BlockSpec((n, sh // 2, wd), lambda i, p: ((p[1] if n == 1 else 0), i, 0))
    mine = pl.BlockSpec(blk, lambda i, p: (layer * nb + i, 0))
    flat = lambda t: t.reshape(nl * rows, cols)
    n_prior = 0 if prior is None else 4

    def body(pos_ref, c_ref, r_ref, w_ref, m_ref, v_ref, *refs):
        g_ref, d_ref, m2_ref, v2_ref = refs[n_prior:]
        g = c_ref[0].astype(F32)
        for s in range(3):
            g = g + r_ref[s].astype(F32)
        g_ref[...] = g
        d_ref[...], m2_ref[...], v2_ref[...] = _adamw_math(w_ref[...], g, m_ref[...], v_ref[...])

    outs = pl.pallas_call(
        body,
        name=name,
        grid_spec=pltpu.PrefetchScalarGridSpec(
            num_scalar_prefetch=1, grid=(nb,),
            in_specs=[part(1), part(3), mine, mine, mine] + [ANY] * n_prior,
            out_specs=[mine] * 4),
        out_shape=[_sds((nl * rows, cols), F32)] * 4,
        input_output_aliases={6 + k: k for k in range(n_prior)},
        compiler_params=_params(),
    )(pos, chip, got, flat(w), flat(m), flat(v), *(flat(t) for t in prior or ()))
    return [o.reshape(w.shape) for o in outs]


def _adamw_math(w, g, m, v):
    m2 = ADAM_B1 * m + (1.0 - ADAM_B1) * g
    v2 = ADAM_B2 * v + (1.0 - ADAM_B2) * (g * g)
    m_hat = m2 / (1.0 - ADAM_B1 ** ADAM_STEP)
    v_hat = v2 / (1.0 - ADAM_B2 ** ADAM_STEP)
    delta = -ADAM_LR * (m_hat / (jnp.sqrt(v_hat) + ADAM_EPS) + ADAM_WD * w)
    return delta, m2, v2


def _adamw(w, g, m, v, *, name, carry=None):
    shape = w.shape
    flat = [t.reshape(-1, shape[-1]) for t in (w, g, m, v)]
    rows, cols = flat[0].shape
    tr = rows // 8 if rows % 64 == 0 else rows
    spec = _rows(tr, cols)

    def body(w_ref, g_ref, m_ref, v_ref, d_ref, m2_ref, v2_ref):
        d_ref[...], m2_ref[...], v2_ref[...] = _adamw_math(w_ref[...], g_ref[...], m_ref[...], v_ref[...])

    outs, got = _call(body, name=name, grid=(rows // tr,), in_specs=[spec] * 4, out_specs=[spec] * 3,
                      out_shape=[_sds((rows, cols), F32)] * 3, scratch_shapes=[], args=flat, carry=carry)
    return tuple(o.reshape(shape) for o in outs), got


def _adds(tag, grads, got, *, pos):
    return _add_pairs(list(grads), list(got)[: len(grads)], pos, name=f"rs_add_{tag}")


def _small_sums(packets, nf, dwp, dsc, dsk, *, name):
    flat = [p for layer in packets for p in layer]

    def total(ref, *idx):
        acc = ref[(0,) + idx]
        for dev in range(1, NDEV):
            acc = acc + ref[(dev,) + idx]
        return acc

    def body(*refs):
        pk = refs[:6]
        nf_ref, dwp0, dwp1, dsc0, dsc1, dsk0, dsk1 = refs[6:13]
        dm_ref, gb_ref, gn_ref, gnf_ref, gwp_ref, gps_ref, gsk_ref = refs[13:]
        dm_ref[...] = jnp.zeros_like(dm_ref)
        gn_ref[...] = jnp.zeros_like(gn_ref)
        for l in range(2):
            for sb in range(3):
                p = pk[3 * l + sb]
                for r in range(3):
                    col = slice((3 * sb + r) * D, (3 * sb + r + 1) * D)
                    lat = p[0, 0, r : r + 1, :]
                    dm_ref[l, 0:1, col] = lat
                    for dev in range(1, NDEV):
                        row = p[dev, 0, r : r + 1, :]
                        dm_ref[l, dev : dev + 1, col] = row
                        lat = lat + row
                    ctx = total(p, 1, slice(r, r + 1), slice(None))
                    dm_ref[l, 8:9, col] = ctx
                    gb_ref[l : l + 1, col] = lat + ctx
                gn_ref[l, sb : sb + 1, :] = total(p, 0, slice(3, 4), slice(None)) + total(p, 1, slice(3, 4), slice(None))
        gnf_ref[...] = total(nf_ref, slice(0, 1), slice(None))
        for l, (a, b, c) in enumerate(((dwp0, dsc0, dsk0), (dwp1, dsc1, dsk1))):
            gwp_ref[l] = total(a, slice(None), slice(None))
            gps_ref[l : l + 1, :] = total(b, slice(0, 1), slice(None))
            gsk_ref[l] = total(c, slice(None), slice(None))

    ins = flat + [nf, dwp[0], dwp[1], dsc[0], dsc[1], dsk[0], dsk[1]]
    return pl.pallas_call(
        body,
        name=name,
        out_shape=[_sds((2, 16, NMOD * D), F32), _sds((2, NMOD * D), F32), _sds((2, 8, D), F32), _sds((1, D), F32),
                   _sds((2, PW, 128), F32), _sds((2, PW), F32), _sds((2, 8, 128), F32)],
        compiler_params=pltpu.CompilerParams(vmem_limit_bytes=VMEM_LIMIT),
    )(*ins)


def _small_adamw(c_ctx, dc_all, triples, *, name):
    n = len(triples)

    def body(*refs):
        c_ref, dc_ref = refs[0], refs[1]
        ins = refs[2 : 2 + 4 * n - 1]
        outs = refs[2 + 4 * n - 1 :]
        acc = dc_ref[0, 0, 8:9, :] + dc_ref[0, 1, 8:9, :]
        for dev in range(1, NDEV):
            acc = acc + (dc_ref[dev, 0, 8:9, :] + dc_ref[dev, 1, 8:9, :])
        c = c_ref[...]
        sig = _sigmoid(c)
        g_c = acc * (sig * (1.0 + c * (1.0 - sig)))
        outs[0][...] = g_c
        pos = 0
        for k in range(n):
            if k == 0:
                w, g, m, v = ins[0][...], g_c, ins[1][...], ins[2][...]
                pos = 3
            else:
                w, g, m, v = (ins[pos + t][...] for t in range(4))
                pos += 4
            d, m2, v2 = _adamw_math(w, g, m, v)
            outs[1 + 3 * k][...], outs[2 + 3 * k][...], outs[3 + 3 * k][...] = d, m2, v2

    flat_in = [c_ctx, dc_all]
    out_shape = [_sds(c_ctx.shape, F32)]
    for k, (w, g, m, v) in enumerate(triples):
        flat_in += [w, m, v] if k == 0 else [w, g, m, v]
        out_shape += [_sds(w.shape, F32)] * 3
    return pl.pallas_call(body, name=name, out_shape=out_shape,
                          compiler_params=pltpu.CompilerParams(vmem_limit_bytes=VMEM_LIMIT))(*flat_in)


def _rope_tables(T, R):
    t = jnp.arange(T)
    inv = ROPE_BASE ** (-jnp.arange(0, HD // 2, 2, dtype=F32) / (HD // 2))
    ang = jnp.concatenate([(t // GRID_W).astype(F32)[:, None] * inv, (t % GRID_W).astype(F32)[:, None] * inv], axis=-1)
    cos = jnp.concatenate([jnp.tile(jnp.cos(ang), (1, 4)), jnp.ones((R - T, 128), F32)], axis=0)
    sin = jnp.concatenate([jnp.tile(jnp.sin(ang), (1, 4)), jnp.zeros((R - T, 128), F32)], axis=0)
    return cos, sin


def kernel(x, c, ctx, c_ctx, w_mod, b_mod, norm_ffn1, w_ffn1_in, w_ffn1_out, norm_mix, w_in, w_pool, pool_scale, sink, w_out, norm_ffn2, w_ffn2_in, w_ffn2_out, norm_final, loss_target, m_c_ctx, m_w_mod, m_b_mod, m_norm_ffn1, m_w_ffn1_in, m_w_ffn1_out, m_norm_mix, m_w_in, m_w_pool, m_pool_scale, m_sink, m_w_out, m_norm_ffn2, m_w_ffn2_in, m_w_ffn2_out, m_norm_final, v_c_ctx, v_w_mod, v_b_mod, v_norm_ffn1, v_w_ffn1_in, v_w_ffn1_out, v_norm_mix, v_w_in, v_w_pool, v_pool_scale, v_sink, v_w_out, v_norm_ffn2, v_w_ffn2_in, v_w_ffn2_out, v_norm_final):
    T = x.shape[1]
    R = T + LC
    nl = w_mod.shape[0]
    cx, cy, cc = _coords()
    me = _lin((cx, cy, cc))
    pos = jnp.stack([cc, 2 * cx + cy]).astype(jnp.int32)
    mcols = w_mod.shape[2]

    shards = [([w_ffn1_in[l].T.astype(BF16), w_ffn1_out[l].astype(BF16)],
               [w_in[l].T.astype(BF16), w_out[l].astype(BF16)],
               [w_ffn2_in[l].T.astype(BF16), w_ffn2_out[l].astype(BF16)]) for l in range(nl)]

    got = _exchange("ag_c_w", _merge(_gather_direct([c]), _gather_a(shards[0][0] + shards[0][1])))
    c_all, w_first = got[0], got[1:]
    c16 = jnp.concatenate([c_all.reshape(NDEV, D), c_ctx[None], jnp.zeros((16 - NDEV - 1, D), F32)], axis=0)
    b_cols = lax.dynamic_slice(b_mod, (0, me * mcols), (nl, mcols)).reshape(nl, 1, mcols)
    got = _exchange("ag_mod_w", _merge(_gather_b(w_first), _gather_direct([_mod_fwd(c16, w_mod, b_cols, name="mod_fwd")])))
    w_first, mod_all = got[:4], got[4]
    mod_all = jnp.transpose(mod_all, (1, 2, 0, 3)).reshape(nl, 16, NMOD, D)
    mine = lax.dynamic_index_in_dim(mod_all, me, axis=1, keepdims=False)
    pad = jnp.zeros((nl, 16 - NMOD, D), F32)
    modv = jnp.stack([jnp.concatenate([mine, pad], axis=1), jnp.concatenate([mod_all[:, 8], pad], axis=1)], axis=1)

    gvec = [jnp.concatenate([norm_ffn1[l][None], norm_mix[l][None], norm_ffn2[l][None], jnp.zeros((5, D), F32)], axis=0)
            for l in range(nl)]
    cos, sin = _rope_tables(T, R)
    ps2 = [pool_scale[l][None] for l in range(nl)]

    h = jnp.concatenate([x[0], ctx[0]], axis=0)
    loss_all, dh, small, nf_all, big, last_partial = _forward_backward(
        h, loss_target[0], modv, gvec, shards, w_first, cos, sin, sink, w_pool, ps2, norm_final, pos, T=T)
    loss = jnp.sum(loss_all[:, 0, 0])
    grad_x = dh[:T][None]

    dm, g_b_mod, g_norms, g_nf, g_wp, g_ps, g_sk = _small_sums(
        [small[l][0:3] for l in range(nl)], nf_all, *[[small[l][k] for l in range(nl)] for k in (3, 4, 5)],
        name="small_sums")
    dm_cols = lax.dynamic_slice(dm, (0, 0, me * mcols), (nl, 16, mcols))
    g_w_mod, dc_part = _mod_bwd(c16, dm_cols, w_mod, name="mod_bwd")
    got = _exchange("rs1_tail", _merge(_scatter_1([last_partial]), _gather_direct([dc_part])))
    (c1o,), dc_all = _adds("ffn1_out_0", [last_partial], got[:1], pos=pos), got[1]

    delta, new_m, new_v = {}, {}, {}
    (delta["w_mod"], new_m["w_mod"], new_v["w_mod"]), got = _adamw(
        w_mod, g_w_mod, m_w_mod, v_w_mod, name="adamw_w_mod", carry=_scatter_2([c1o]))
    big[0][1] = (c1o, got[0])

    grads = {
        "b_mod": g_b_mod, "norm_ffn1": g_norms[:, 0], "norm_mix": g_norms[:, 1], "norm_ffn2": g_norms[:, 2],
        "w_pool": g_wp.reshape(w_pool.shape), "pool_scale": g_ps, "sink": g_sk[:, :, 0], "norm_final": g_nf.reshape(D),
        "w_mod": g_w_mod,
    }
    weights = dict(c_ctx=c_ctx, w_mod=w_mod, b_mod=b_mod, norm_ffn1=norm_ffn1, w_ffn1_in=w_ffn1_in, w_ffn1_out=w_ffn1_out,
                   norm_mix=norm_mix, w_in=w_in, w_pool=w_pool, pool_scale=pool_scale, sink=sink, w_out=w_out,
                   norm_ffn2=norm_ffn2, w_ffn2_in=w_ffn2_in, w_ffn2_out=w_ffn2_out, norm_final=norm_final)
    moms = dict(c_ctx=(m_c_ctx, v_c_ctx), w_mod=(m_w_mod, v_w_mod), b_mod=(m_b_mod, v_b_mod),
                norm_ffn1=(m_norm_ffn1, v_norm_ffn1), w_ffn1_in=(m_w_ffn1_in, v_w_ffn1_in),
                w_ffn1_out=(m_w_ffn1_out, v_w_ffn1_out), norm_mix=(m_norm_mix, v_norm_mix), w_in=(m_w_in, v_w_in),
                w_pool=(m_w_pool, v_w_pool), pool_scale=(m_pool_scale, v_pool_scale), sink=(m_sink, v_sink),
                w_out=(m_w_out, v_w_out), norm_ffn2=(m_norm_ffn2, v_norm_ffn2), w_ffn2_in=(m_w_ffn2_in, v_w_ffn2_in),
                w_ffn2_out=(m_w_ffn2_out, v_w_ffn2_out), norm_final=(m_norm_final, v_norm_final))
    order = list(weights)
    small_names = ["c_ctx", "b_mod", "norm_ffn1", "norm_mix", "w_pool", "pool_scale", "sink", "norm_ffn2", "norm_final"]

    def as2d(name, t):
        if name == "w_pool":
            return t.reshape(-1, 128)
        return t.reshape(1, -1) if t.ndim == 1 else t

    triples = [(as2d(n, weights[n]), None if n == "c_ctx" else as2d(n, grads[n]), as2d(n, moms[n][0]), as2d(n, moms[n][1]))
               for n in small_names]
    outs = _small_adamw(as2d("c_ctx", c_ctx), dc_all, triples, name="small_adamw")
    grads["c_ctx"] = outs[0].reshape(c_ctx.shape)
    for k, n in enumerate(small_names):
        delta[n], new_m[n], new_v[n] = (o.reshape(weights[n].shape) for o in outs[1 + 3 * k : 4 + 3 * k])
    for k, n in enumerate(["w_ffn1_in", "w_ffn1_out", "w_in", "w_out", "w_ffn2_in", "w_ffn2_out"]):
        turn = (lambda t: jnp.swapaxes(t, 1, 2)) if k % 2 == 0 else (lambda t: t)
        wmv = [turn(t) for t in (weights[n], *moms[n])]
        outs = None
        for l in reversed(range(nl)):
            outs = _sum_adamw(*big[l][k], pos, *wmv, l, outs, name=f"adamw_{n}_{l}")
        grads[n], delta[n], new_m[n], new_v[n] = (turn(o) for o in outs)

    return (loss, grad_x, *[grads[n] for n in order], *[delta[n] for n in order],
            *[new_m[n] for n in order], *[new_v[n] for n in order])


def _merge(*rounds):
    ins, outs, plan, local, n_alias = [], [], [], [], 0
    for r in rounds:
        assert r.n_alias == 0 or (not ins and r.n_alias == len(r.ins) == len(r.out_shapes))
        oi, oo = len(ins), len(outs)
        plan += [(k, None if i is None else i + oi, sf, o + oo, df) for k, i, sf, o, df in r.plan]
        local += [(i + oi, sf, o + oo, df) for i, sf, o, df in r.local_plan]
        ins += r.ins
        outs += r.out_shapes
        n_alias += r.n_alias
    return _Round(ins, outs, plan, local, n_alias)


def _forward_backward(h, target, modv, gvec, shards, w_first, cos, sin, sink, w_pool, ps2, norm_final, pos, *, T):
    nl = len(gvec)
    flat = lambda ws: [w.reshape(-1, D) for w in ws]
    saved = []
    w1, wm = flat(w_first[:2]), flat(w_first[2:])
    for l in range(nl):
        last = l == nl - 1
        h0 = h
        if l == 0:
            (h1, a1, b1, f1), got = _ffn_fwd(h0, modv[l], gvec[l], *w1, T=T, mrow=0, grow=0, ctx_active=True,
                                             name=f"ffn1_fwd_{l}", carry=_gather_a(shards[l][2]))
            (u, q, k4, v4), got = _mixproj_fwd(h1, modv[l], gvec[l], wm[0], cos, sin, T=T, name=f"mixproj_fwd_{l}",
                                               carry=_gather_b(got))
            w2 = flat(got)
        else:
            (h1, a1, b1, f1), got = _ffn_fwd(h0, modv[l], gvec[l], *w1, T=T, mrow=0, grow=0, ctx_active=True,
                                             name=f"ffn1_fwd_{l}", carry=_gather_b(nxt_m + nxt_2))
            wm, w2 = flat(got[:2]), flat(got[2:])
            (u, q, k4, v4), _ = _mixproj_fwd(h1, modv[l], gvec[l], wm[0], cos, sin, T=T, name=f"mixproj_fwd_{l}")
        (cat,), nxt_1 = _attnpool_fwd(u, q, k4, v4, sink[l], w_pool[l], ps2[l], T=T, name=f"attnpool_fwd_{l}",
                                      carry=None if last else _gather_a(shards[l + 1][0]))
        (h2, mo), nxt_m = _mixout_fwd(h1, cat, modv[l], wm[1], T=T, ctx_active=not last, name=f"mixout_fwd_{l}",
                                      carry=None if last else _gather_a(shards[l + 1][1]))
        (h3, a2, b2, f2), got = _ffn_fwd(h2, modv[l], gvec[l], *w2, T=T, mrow=6, grow=2, ctx_active=not last,
                                         name=f"ffn2_fwd_{l}",
                                         carry=None if last else _merge(_gather_b(nxt_1), _gather_a(shards[l + 1][2])))
        saved.append((h0, a1, b1, f1, h1, u, q, k4, v4, cat, mo, h2, a2, b2, f2, w1, wm, w2))
        h = h3
        if not last:
            w1, nxt_2 = flat(got[:2]), got[2:]

    dh, loss_part, dnf = _loss_head(h, target, norm_final[None], T=T, name="loss_head")

    adds = functools.partial(_adds, pos=pos)
    small, big = [None] * nl, {}
    prev = None
    for l in reversed(range(nl)):
        last = l == nl - 1
        h0, a1, b1, f1, h1, u, q, k4, v4, cat, mo, h2, a2, b2, f2, w1, wm, w2 = saved[l]
        (dh, dab, s, n, df, pk2), got = _ffn_bwd(
            h2, dh, a2, b2, f2, modv[l], gvec[l], *w2, T=T, mrow=6, grow=2, ctx_active=not last, name=f"ffn2_bwd_{l}",
            carry=_merge(_scatter_1(prev[0]), _gather_a(prev[1])) if prev else None)
        if prev:
            c1, small_a = adds(f"ffn1_{l + 1}", prev[0], got[:2]), got[2:]
        g_w2i, got = _wgrad(dab, n, bk=WG_BK, sh=2 * DFF // NDEV, name=f"wgrad_ffn2_in_{l}",
                            carry=_scatter_2(c1[:1]) if prev else None)
        if prev:
            big[l + 1][0] = (c1[0], got[0])
        g_w2o, got = _wgrad(s, df, bk=WG_BK, sh=DFF // NDEV, name=f"wgrad_ffn2_out_{l}",
                            carry=_scatter_2(c1[1:]) if prev else None)
        if prev:
            big[l + 1][1] = (c1[1], got[0])
        rnd = _scatter_1([g_w2i, g_w2o])
        (dcat, dmix, pko), got = _mixout_bwd(dh, mo, modv[l], wm[1], T=T, ctx_active=not last, name=f"mixout_bwd_{l}",
                                             carry=_merge(_gather_b(small_a), rnd) if prev else rnd)
        if prev:
            small[l + 1], got = got[: len(small_a)], got[len(small_a) :]
        c2 = adds(f"ffn2_{l}", [g_w2i, g_w2o], got)
        g_wo, _ = _wgrad(cat, dmix, bk=D, sh=D // NDEV, name=f"wgrad_out_{l}")
        dps, dwp, dsc = _pool_bwd(u, dcat, w_pool[l], ps2[l], T=T, name=f"pool_bwd_{l}")
        (du, dq, dk, dv, dsk), got = _attn_bwd(q, k4, v4, dcat, dps, sink[l], T=T, name=f"attn_bwd_{l}", carry=_scatter_2(c2))
        big[l] = [None, None, None, None, (c2[0], got[0]), (c2[1], got[1])]
        dh, dproj, n, pkm = _mixproj_bwd(h1, dh, du, dq, dk, dv, modv[l], gvec[l], wm[0], cos, sin, T=T, name=f"mixproj_bwd_{l}")
        g_wi, _ = _wgrad(dproj, n, bk=PROJ, sh=PROJ // NDEV, name=f"wgrad_in_{l}")
        (dh, dab, s, n, df, pk1), got = _ffn_bwd(h0, dh, a1, b1, f1, modv[l], gvec[l], *w1, T=T, mrow=0, grow=0,
                                                 ctx_active=True, name=f"ffn1_bwd_{l}", carry=_scatter_1([g_wi, g_wo]))
        cm = adds(f"mix_{l}", [g_wi, g_wo], got)
        mine = [pk1, pkm + pko, pk2, dwp, dsc, dsk]
        rnd = _merge(_scatter_2(cm), _gather_a(mine + [dnf, loss_part])) if l == 0 else _scatter_2(cm)
        g_w1i, got = _wgrad(dab, n, bk=WG_BK, sh=2 * DFF // NDEV, name=f"wgrad_ffn1_in_{l}", carry=rnd)
        big[l][2:4] = [(cm[0], got[0]), (cm[1], got[1])]
        if l > 0:
            g_w1o, _ = _wgrad(s, df, bk=WG_BK, sh=DFF // NDEV, name=f"wgrad_ffn1_out_{l}")
            prev = ([g_w1i, g_w1o], mine)
    (c1i,) = adds("ffn1_in_0", [g_w1i], _exchange("rs1_ffn1_in_0", _scatter_1([g_w1i])))
    g_w1o, got = _wgrad(s, df, bk=WG_BK, sh=DFF // NDEV, name="wgrad_ffn1_out_0",
                        carry=_merge(_gather_b(got[2:]), _scatter_2([c1i])))
    small[0], nf_all, loss_all = got[:6], got[6], got[7]
    big[0][0] = (c1i, got[8])
    return loss_all, dh, small, nf_all, big, g_w1o
```
